```python
import jax
import jax.numpy as jnp
from jax import lax
import numpy as np

D_MODEL = 1024
BATCH = 32
SEQ = 256
DEPTH = 2
DEC_BATCH = 2
DEC_SEQ = 1024
PAST_LEN = 256

GRID_W = 64
N_DIR = 2
N_BRANCH = 4
W_MIX = 512
CONV_A = 3
H_LRU = 8
HB_LRU = W_MIX // H_LRU
CONV_B = 4
LRU_C = 8.0
H_WKV = 8
HS_WKV = W_MIX // H_WKV
LORA_W = 64
LORA_A = 64
LORA_G = 128
LNX_EPS = 64e-5
CHUNK = 128
G_SG = 4
C_SG = W_MIX // G_SG
PEER_HEADS = 8
N_KEYS = 128
N_EXPERTS = N_KEYS * N_KEYS
D_KEY = 256
HALF_KEY = D_KEY // 2
PEER_TOPK = 16
PEER_BLOCK = 128
N_MOD = 6
EPS = 1e-6
IN_SPLIT = (W_MIX,) * 8 + (N_DIR * LORA_W, N_DIR * LORA_A, LORA_G, 2 * W_MIX, N_BRANCH * D_MODEL)
N_IN = sum(IN_SPLIT)

kernel_name = 'hybrid_flow_prefix_step'


def rms_norm(x, g):
    xf = x.astype(jnp.float32)
    y = xf * lax.rsqrt(jnp.mean(xf * xf, axis=-1, keepdims=True) + EPS)
    return (y * g.astype(jnp.float32)).astype(x.dtype)


def layer_norm(x, g, b, eps):
    xf = x.astype(jnp.float32)
    xc = xf - jnp.mean(xf, axis=-1, keepdims=True)
    y = xc * lax.rsqrt(jnp.mean(xc * xc, axis=-1, keepdims=True) + eps)
    return (y * g.astype(jnp.float32) + b.astype(jnp.float32)).astype(x.dtype)


def conv_rows(x, w, n_rows):
    bsz, l, ch = x.shape
    k = w.shape[0]
    left = (k - 1) // 2
    row = l // n_rows
    xp = jnp.pad(x.reshape(bsz, n_rows, row, ch), ((0, 0), (0, 0), (left, k - 1 - left), (0, 0)))
    y = xp[:, :, 0:row] * w[0]
    for j in range(1, k):
        y = y + xp[:, :, j:j + row] * w[j]
    return y.reshape(bsz, l, ch)


def linear_scan(a, u, h0, reverse):
    def comb(e1, e2):
        return e1[0] * e2[0], e2[0] * e1[1] + e2[1]
    acum, ucum = lax.associative_scan(comb, (a, u), reverse=reverse, axis=1)
    h = acum * h0[:, None] + ucum
    return h, (h[:, 0] if reverse else h[:, -1])


def rg_lru(xb, lp, h0):
    bsz, l, _ = xb.shape
    xh = xb.reshape(bsz, l, H_LRU, HB_LRU)
    r = jax.nn.sigmoid(jnp.einsum('blhi,dhij->bldhj', xh, lp['lru_wa']).reshape(bsz, l, N_DIR, W_MIX) + lp['lru_ba'])
    i = jax.nn.sigmoid(jnp.einsum('blhi,dhij->bldhj', xh, lp['lru_wx']).reshape(bsz, l, N_DIR, W_MIX) + lp['lru_bx'])
    log_a = -LRU_C * r * jax.nn.softplus(-lp['lru_lambda'])
    a = jnp.exp(log_a)
    u = jnp.sqrt(-jnp.expm1(2.0 * log_a)) * (i * xb[:, :, None])
    h_f, s_f = linear_scan(a[:, :, 0], u[:, :, 0], h0[:, 0], False)
    h_b, s_b = linear_scan(a[:, :, 1], u[:, :, 1], h0[:, 1], True)
    return h_f + h_b, jnp.stack([s_f, s_b], axis=1)


def wkv_scan(s0, r, w, k, v, kk, a, reverse):
    def step(s, inp):
        r_t, w_t, k_t, v_t, kk_t, a_t = inp
        sa = jnp.einsum('bhvk,bhk->bhv', s, kk_t)
        s = s * w_t[:, :, None, :] - sa[..., None] * (kk_t * a_t)[:, :, None, :] + v_t[..., None] * k_t[:, :, None, :]
        return s, jnp.einsum('bhvk,bhk->bhv', s, r_t)
    xs = tuple(jnp.moveaxis(t, 1, 0) for t in (r, w, k, v, kk, a))
    s, ys = lax.scan(step, s0, xs, reverse=reverse)
    return jnp.moveaxis(ys, 0, 1), s


def rwkv7_mix(zr, zk, zv, zwd, zad, zgd, lp, s0):
    bsz, l, _ = zr.shape
    f32 = jnp.float32
    zr, zk, zv = zr.astype(f32), zk.astype(f32), zv.astype(f32)

    def heads(t):
        return t.reshape(bsz, l, H_WKV, HS_WKV)

    w = lp['rwkv_w0'] + jnp.einsum('bldr,drc->bldc', jnp.tanh(zwd.reshape(bsz, l, N_DIR, LORA_W)), lp['rwkv_w2'])
    decay = jnp.exp(-jnp.exp(-jax.nn.softplus(-w.astype(f32)) - 0.5))
    a = jax.nn.sigmoid((lp['rwkv_a0'] + jnp.einsum('bldr,drc->bldc', zad.reshape(bsz, l, N_DIR, LORA_A), lp['rwkv_a2'])).astype(f32))
    g = jax.nn.sigmoid(zgd) @ lp['rwkv_g2']
    kk = heads(zk * lp['rwkv_kk'])
    kk = kk / jnp.maximum(jnp.sqrt(jnp.sum(kk * kk, axis=-1, keepdims=True)), 1e-12)
    k_dir = zk[:, :, None] * (1.0 + (a - 1.0) * lp['rwkv_ka'])
    rh, vh = heads(zr), heads(zv)
    ys, bonus, states = [], [], []
    for d in range(N_DIR):
        kd = heads(k_dir[:, :, d])
        yd, sd = wkv_scan(s0[:, d].astype(f32), rh, heads(decay[:, :, d]), kd, vh, kk, heads(a[:, :, d]), d == 1)
        ys.append(yd)
        bonus.append(jnp.sum(rh * kd * lp['rwkv_rk'], axis=-1, keepdims=True) * vh)
        states.append(sd)
    y = ys[0] + ys[1]
    yc = y - jnp.mean(y, axis=-1, keepdims=True)
    y = yc * lax.rsqrt(jnp.mean(yc * yc, axis=-1, keepdims=True) + LNX_EPS)
    y = y.reshape(bsz, l, W_MIX) * lp['lnx_g'] + lp['lnx_b'] + (bonus[0] + bonus[1]).reshape(bsz, l, W_MIX)
    return (y * g).astype(zgd.dtype), jnp.stack(states, axis=1)


def chunk_sgu(zd, lp):
    bsz, l, _ = zd.shape
    zd = jax.nn.gelu(zd)
    u, v = zd[..., :W_MIX], zd[..., W_MIX:]
    v = layer_norm(v, lp['sg_ln_g'], lp['sg_ln_b'], 1e-5)
    vr = v.reshape(bsz, l // CHUNK, CHUNK, G_SG, C_SG)
    s = jnp.einsum('gpq,bnqgc->bnpgc', lp['sg_ws'], vr) + lp['sg_bs'].T[:, :, None]
    return u * s.reshape(bsz, l, W_MIX).astype(u.dtype)


def mixer_block(h, n_rows, lp, lru0, wkv0):
    bsz, l, _ = h.shape
    pts = [int(p) for p in np.cumsum(IN_SPLIT)[:-1]]
    z = h @ lp['w_in']
    (a_b, a_c, a_x, b_g, b_x, c_r, c_k, c_v, c_wd, c_ad, c_gd, d_in, gate_in) = jnp.split(z, pts, axis=-1)
    y_a = a_b * conv_rows(a_c * a_x, lp['conv_a_w'], n_rows)
    xb = conv_rows(b_x, lp['conv_b_w'], 1) + lp['conv_b_b']
    hb, lru_s = rg_lru(xb.astype(jnp.float32), lp, lru0.astype(jnp.float32))
    y_b = jax.nn.gelu(b_g) * hb.astype(h.dtype)
    y_c, wkv_s = rwkv7_mix(c_r, c_k, c_v, c_wd, c_ad, c_gd, lp, wkv0)
    y_d = chunk_sgu(d_in, lp)
    gates = jax.nn.sigmoid(gate_in.reshape(bsz, l, N_BRANCH, D_MODEL) + lp['gate_b'])
    br = jnp.stack([y_a, y_b.astype(y_a.dtype), y_c.astype(y_a.dtype), y_d.astype(y_a.dtype)], axis=2)
    merged = jnp.sum(gates * jnp.einsum('blnw,nwd->blnd', br, lp['w_branch']), axis=2)
    return merged @ lp['w_out'], lru_s, wkv_s


def peer_ffn(x, lp):
    bsz, l, d = x.shape
    n_tok = bsz * l
    xt = x.reshape(n_tok, d)
    q = (xt @ lp['peer_wq']).reshape(n_tok, PEER_HEADS, 2, HALF_KEY)
    s = jnp.einsum('thpc,pkc->thpk', q, lp['peer_keys']).astype(jnp.float32)
    s_top, i_top = lax.top_k(s, PEER_TOPK)
    n_cand = PEER_TOPK * PEER_TOPK
    cand_s = (s_top[:, :, 0, :, None] + s_top[:, :, 1, None, :]).reshape(n_tok, PEER_HEADS, n_cand)
    cand_e = (i_top[:, :, 0, :, None] * N_KEYS + i_top[:, :, 1, None, :]).reshape(n_tok, PEER_HEADS, n_cand)
    best_s, best_pos = lax.top_k(cand_s, PEER_TOPK)
    experts = jnp.take_along_axis(cand_e, best_pos, axis=-1)
    gates = jax.nn.softmax(best_s, axis=-1).astype(x.dtype)
    n_blk = n_tok // PEER_BLOCK

    def block(args):
        xb, eb, gb = args
        act = jax.nn.gelu(jnp.einsum('thkd,td->thk', lp['peer_u'][eb], xb))
        return jnp.einsum('thk,thkd->td', act * gb, lp['peer_v'][eb])

    out = lax.map(block, (xt.reshape(n_blk, PEER_BLOCK, d),
                          experts.reshape(n_blk, PEER_BLOCK, PEER_HEADS, PEER_TOPK),
                          gates.reshape(n_blk, PEER_BLOCK, PEER_HEADS, PEER_TOPK)))
    return out.reshape(bsz, l, d).astype(x.dtype)


def trunk_layer(x, mod, lp, n_rows, lru0, wkv0):
    shift1, scale1, gate1, shift2, scale2, gate2 = [mod[:, None, j] for j in range(N_MOD)]
    h = rms_norm(x, lp['norm1_g']) * (1.0 + scale1) + shift1
    m, lru_s, wkv_s = mixer_block(h, n_rows, lp, lru0, wkv0)
    x = x + gate1 * m
    h = rms_norm(x, lp['norm2_g']) * (1.0 + scale2) + shift2
    x = x + gate2 * peer_ffn(h, lp)
    return x, lru_s, wkv_s


def setup_inputs(seed: int = 0) -> dict:
    key = jax.random.key(seed)
    ks = iter(jax.random.split(key, 48))
    D = D_MODEL
    L = DEPTH

    def nrm(shape, scale):
        return jax.random.normal(next(ks), shape, jnp.float32) * scale

    def unif(shape, lo, hi):
        return jax.random.uniform(next(ks), shape, jnp.float32, lo, hi)

    s_lam = unif((L, N_DIR, W_MIX), 0.9, 0.999) ** (1.0 / LRU_C)
    return {
        'x_prompt': nrm((BATCH, SEQ, D), 1.0),
        'x_sample': nrm((DEC_BATCH, DEC_SEQ, D), 1.0),
        'state_lru': nrm((DEC_BATCH, L, N_DIR, W_MIX), 0.5),
        'state_wkv': nrm((DEC_BATCH, L, N_DIR, H_WKV, HS_WKV, HS_WKV), 0.3),
        'c': nrm((DEC_BATCH, D), 1.0),
        'c_ctx': nrm((D,), 1.0),
        'norm1_g': 1.0 + nrm((L, D), 0.02),
        'norm2_g': 1.0 + nrm((L, D), 0.02),
        'w_mod': nrm((L, D, N_MOD * D), 0.5 * D ** -0.5),
        'b_mod': nrm((L, N_MOD * D), 0.02),
        'w_in': nrm((L, D, N_IN), D ** -0.5),
        'conv_a_w': nrm((L, CONV_A, W_MIX), CONV_A ** -0.5),
        'conv_b_w': nrm((L, CONV_B, W_MIX), CONV_B ** -0.5),
        'conv_b_b': nrm((L, W_MIX), 0.02),
        'lru_wa': nrm((L, N_DIR, H_LRU, HB_LRU, HB_LRU), HB_LRU ** -0.5),
        'lru_ba': nrm((L, N_DIR, W_MIX), 0.02),
        'lru_wx': nrm((L, N_DIR, H_LRU, HB_LRU, HB_LRU), HB_LRU ** -0.5),
        'lru_bx': nrm((L, N_DIR, W_MIX), 0.02),
        'lru_lambda': jnp.log(s_lam) - jnp.log1p(-s_lam),
        'rwkv_w0': unif((L, N_DIR, W_MIX), -6.0, 1.0),
        'rwkv_w2': nrm((L, N_DIR, LORA_W, W_MIX), 0.1),
        'rwkv_a0': nrm((L, N_DIR, W_MIX), 0.1),
        'rwkv_a2': nrm((L, N_DIR, LORA_A, W_MIX), 0.1),
        'rwkv_g2': nrm((L, LORA_G, W_MIX), LORA_G ** -0.5),
        'rwkv_kk': 0.85 + nrm((L, W_MIX), 0.02),
        'rwkv_ka': 1.0 + nrm((L, W_MIX), 0.02),
        'rwkv_rk': nrm((L, H_WKV, HS_WKV), 0.1),
        'lnx_g': 1.0 + nrm((L, W_MIX), 0.02),
        'lnx_b': nrm((L, W_MIX), 0.02),
        'sg_ln_g': 1.0 + nrm((L, W_MIX), 0.02),
        'sg_ln_b': nrm((L, W_MIX), 0.02),
        'sg_ws': nrm((L, G_SG, CHUNK, CHUNK), 0.5 * CHUNK ** -0.5),
        'sg_bs': 1.0 + nrm((L, G_SG, CHUNK), 0.1),
        'gate_b': nrm((L, N_BRANCH, D), 0.02),
        'w_branch': nrm((L, N_BRANCH, W_MIX, D), W_MIX ** -0.5),
        'w_out': nrm((L, D, D), D ** -0.5),
        'peer_wq': nrm((L, D, PEER_HEADS * D_KEY), D ** -0.5),
        'peer_keys': nrm((L, 2, N_KEYS, HALF_KEY), HALF_KEY ** -0.5),
        'peer_u': nrm((L, N_EXPERTS, D), D ** -0.5),
        'peer_v': nrm((L, N_EXPERTS, D), 1.0),
        'final_norm_g': 1.0 + nrm((D,), 0.02),
    }


def reference(x_prompt, x_sample, state_lru, state_wkv, c, c_ctx, norm1_g, norm2_g, w_mod, b_mod, w_in,
              conv_a_w, conv_b_w, conv_b_b, lru_wa, lru_ba, lru_wx, lru_bx, lru_lambda, rwkv_w0, rwkv_w2,
              rwkv_a0, rwkv_a2, rwkv_g2, rwkv_kk, rwkv_ka, rwkv_rk, lnx_g, lnx_b, sg_ln_g, sg_ln_b, sg_ws,
              sg_bs, gate_b, w_branch, w_out, peer_wq, peer_keys, peer_u, peer_v, final_norm_g):
    n_rows = x_sample.shape[1] // GRID_W
    mod_ctx_in = jax.nn.silu(c_ctx)[None]
    mod_lat_in = jax.nn.silu(c)
    n_ctx = x_prompt.shape[0]
    lru_zero = jnp.zeros((n_ctx, N_DIR, W_MIX), jnp.float32)
    wkv_zero = jnp.zeros((n_ctx, N_DIR, H_WKV, HS_WKV, HS_WKV), jnp.float32)
    xp, xs = x_prompt, x_sample
    new_lru, new_wkv = [], []
    for i in range(DEPTH):
        lp = {'norm1_g': norm1_g[i], 'norm2_g': norm2_g[i], 'w_in': w_in[i], 'conv_a_w': conv_a_w[i],
              'conv_b_w': conv_b_w[i], 'conv_b_b': conv_b_b[i], 'lru_wa': lru_wa[i], 'lru_ba': lru_ba[i],
              'lru_wx': lru_wx[i], 'lru_bx': lru_bx[i], 'lru_lambda': lru_lambda[i], 'rwkv_w0': rwkv_w0[i],
              'rwkv_w2': rwkv_w2[i], 'rwkv_a0': rwkv_a0[i], 'rwkv_a2': rwkv_a2[i], 'rwkv_g2': rwkv_g2[i],
              'rwkv_kk': rwkv_kk[i], 'rwkv_ka': rwkv_ka[i], 'rwkv_rk': rwkv_rk[i], 'lnx_g': lnx_g[i],
              'lnx_b': lnx_b[i], 'sg_ln_g': sg_ln_g[i], 'sg_ln_b': sg_ln_b[i], 'sg_ws': sg_ws[i],
              'sg_bs': sg_bs[i], 'gate_b': gate_b[i], 'w_branch': w_branch[i], 'w_out': w_out[i],
              'peer_wq': peer_wq[i], 'peer_keys': peer_keys[i], 'peer_u': peer_u[i], 'peer_v': peer_v[i]}
        mod_ctx = (mod_ctx_in @ w_mod[i] + b_mod[i]).reshape(1, N_MOD, D_MODEL)
        mod_lat = (mod_lat_in @ w_mod[i] + b_mod[i]).reshape(-1, N_MOD, D_MODEL)
        xp, lru_s, wkv_s = trunk_layer(xp, mod_ctx, lp, 1, lru_zero, wkv_zero)
        new_lru.append(lru_s.astype(x_prompt.dtype))
        new_wkv.append(wkv_s.astype(x_prompt.dtype))
        xs, _, _ = trunk_layer(xs, mod_lat, lp, n_rows, state_lru[:, i], state_wkv[:, i])
    y_prompt = rms_norm(xp, final_norm_g)
    y_sample = rms_norm(xs, final_norm_g)
    new_state_lru = jnp.stack(new_lru, axis=1)
    new_state_wkv = jnp.stack(new_wkv, axis=1)
    return (y_prompt, y_sample, new_state_lru, new_state_wkv)
```

```python
import functools

import jax
import jax.numpy as jnp
from jax import lax
from jax.experimental import pallas as pl
from jax.experimental.pallas import tpu as pltpu

F32 = jnp.float32
BF16 = jnp.bfloat16
I32 = jnp.int32

D_MODEL = 1024
W_MIX = 512
N_DIR = 2
N_BRANCH = 4
H_WKV = 8
HS_WKV = 64
H_LRU = 8
HB_LRU = 64
LORA_W = 64
LORA_A = 64
LORA_G = 128
GRID_W = 64
CHUNK = 128
G_SG = 4
N_KEYS = 128
PEER_HEADS = 8
PEER_TOPK = 16
N_MOD = 6
EPS = 1e-6
LNX_EPS = 64e-5
LRU_C = 8.0

LANES = 128
SUBLANES = 8
TM = 256
TM_IN = 512
TN_IN = 1408
TM_PEER = 512
PEER_SPLIT = 2
PEER_EB = 512
Z_COLS = 5632
VMEM_LIMIT = 56 * 1024 * 1024

_NT = (((1,), (1,)), ((), ()))


def _cparams(sem, vmem=None):
    return pltpu.CompilerParams(dimension_semantics=sem, vmem_limit_bytes=vmem)


def _const_spec(shape):
    nd = len(shape)
    return pl.BlockSpec(shape, lambda *_: (0,) * nd)


def _softplus(x):
    return jnp.maximum(x, 0.0) + jnp.log1p(jnp.exp(-jnp.abs(x)))


def _rms(x, g):
    return x * lax.rsqrt(jnp.mean(x * x, axis=-1, keepdims=True) + EPS) * g


def _segsum(x, ind):
    hi = x.astype(BF16)
    lo = (x - hi.astype(F32)).astype(BF16)
    return (jnp.dot(hi, ind, preferred_element_type=F32)
            + jnp.dot(lo, ind, preferred_element_type=F32))


def _mod_index(i, tm, n_ctx_tok, lat_len):
    n_ctx_tiles = n_ctx_tok // tm
    tiles_per_seq = lat_len // tm
    return jnp.where(i < n_ctx_tiles, 0, 1 + lax.div(i - n_ctx_tiles, tiles_per_seq))


def _mod_body(s_ref, w_ref, b_ref, o_ref):
    s = s_ref[...]
    s = s * jax.nn.sigmoid(s)
    o_ref[0] = jnp.dot(s.astype(BF16), w_ref[0].astype(BF16), preferred_element_type=F32) + b_ref[0]


def _modulation(cond, w_mod, b_mod):
    depth = w_mod.shape[0]
    n = w_mod.shape[2]
    tn = 1536
    return pl.pallas_call(
        _mod_body,
        grid=(depth, n // tn),
        in_specs=[_const_spec((SUBLANES, D_MODEL)),
                  pl.BlockSpec((1, D_MODEL, tn), lambda l, j: (l, 0, j)),
                  pl.BlockSpec((1, 1, tn), lambda l, j: (l, 0, j))],
        out_specs=pl.BlockSpec((1, SUBLANES, tn), lambda l, j: (l, 0, j)),
        out_shape=jax.ShapeDtypeStruct((depth, SUBLANES, n), F32),
        compiler_params=_cparams(("parallel", "parallel")),
        name="modulation",
    )(cond, w_mod, b_mod.reshape(depth, 1, n))


def _in_body(x_ref, mod_ref, g_ref, w_ref, o_ref, h_scr):
    @pl.when(pl.program_id(1) == 0)
    def _():
        m = mod_ref[0]
        y = _rms(x_ref[...], g_ref[...])
        h_scr[...] = (y * (1.0 + m[1:2, :]) + m[0:1, :]).astype(BF16)

    o_ref[...] = jnp.dot(h_scr[...], w_ref[...], preferred_element_type=F32)


def _in_proj(x, mod, g, w, n_ctx_tok, lat_len):
    t = x.shape[0]
    midx = functools.partial(_mod_index, tm=TM_IN, n_ctx_tok=n_ctx_tok, lat_len=lat_len)
    return pl.pallas_call(
        _in_body,
        grid=(t // TM_IN, Z_COLS // TN_IN),
        in_specs=[pl.BlockSpec((TM_IN, D_MODEL), lambda i, j: (i, 0)),
                  pl.BlockSpec((1, N_MOD, D_MODEL), lambda i, j: (midx(i), 0, 0)),
                  _const_spec((1, D_MODEL)),
                  pl.BlockSpec((D_MODEL, TN_IN), lambda i, j: (0, j))],
        out_specs=pl.BlockSpec((TM_IN, TN_IN), lambda i, j: (i, j)),
        out_shape=jax.ShapeDtypeStruct((t, Z_COLS), F32),
        scratch_shapes=[pltpu.VMEM((TM_IN, D_MODEL), BF16)],
        compiler_params=_cparams(("parallel", "arbitrary"), VMEM_LIMIT),
        name="in_proj",
    )(x, mod, g, w)


def _prep_body(za_ref, zc_ref, zb_ref, zd_ref, zl_ref, hp_ref, hn_ref,
               caw_ref, cbw_ref, cbb_ref, lruw_ref, lrub_ref, lam_ref,
               w0_ref, w2_ref, a0_ref, a2_ref, g2_ref, kkw_ref, ka_ref, rk_ref,
               lng_ref, lnb_ref, ws_ref, bst_ref, ind_ref,
               ya_ref, yd_ref, gbg_ref, la_ref, lu_ref, r_ref, v_ref, kk_ref,
               w_ref, k_ref, b_ref, g_ref, bon_ref,
               *, n_ctx_tiles, tiles_per_seq):
    i = pl.program_id(0)
    is_ctx = i < n_ctx_tiles
    t = lax.broadcasted_iota(I32, (TM, 1), 0)
    ind = ind_ref[...]

    pm = jnp.where(is_ctx, TM - 1, GRID_W - 1)
    pos = t & pm
    a_b = za_ref[:, 0:W_MIX]
    ac = za_ref[:, W_MIX:2 * W_MIX] * za_ref[:, 2 * W_MIX:3 * W_MIX]
    up = jnp.where(pos == 0, 0.0, pltpu.roll(ac, 1, 0))
    dn = jnp.where(pos == pm, 0.0, pltpu.roll(ac, TM - 1, 0))
    ya_ref[...] = a_b * (caw_ref[0:1, :] * up + caw_ref[1:2, :] * ac + caw_ref[2:3, :] * dn)

    seq_tile = lax.rem(jnp.maximum(i - n_ctx_tiles, 0), tiles_per_seq)
    first = jnp.logical_or(is_ctx, seq_tile == 0)
    last = jnp.logical_or(is_ctx, seq_tile == tiles_per_seq - 1)
    prev = jnp.where(first, 0.0, hp_ref[SUBLANES - 1:SUBLANES, :])
    nxt0 = jnp.where(last, 0.0, hn_ref[0:1, :])
    nxt1 = jnp.where(last, 0.0, hn_ref[1:2, :])
    bx = zb_ref[:, W_MIX:2 * W_MIX]
    m1 = jnp.where(t == 0, prev, pltpu.roll(bx, 1, 0))
    p1 = jnp.where(t == TM - 1, nxt0, pltpu.roll(bx, TM - 1, 0))
    p2 = jnp.where(t == TM - 2, nxt0, jnp.where(t == TM - 1, nxt1, pltpu.roll(bx, TM - 2, 0)))
    xb = (cbw_ref[0:1, :] * m1 + cbw_ref[1:2, :] * bx + cbw_ref[2:3, :] * p1
          + cbw_ref[3:4, :] * p2 + cbb_ref[...])
    gates = jnp.dot(xb.astype(BF16), lruw_ref[...], preferred_element_type=F32) + lrub_ref[...]
    rg = jax.nn.sigmoid(gates[:, 0:2 * W_MIX])
    ig = jax.nn.sigmoid(gates[:, 2 * W_MIX:4 * W_MIX])
    log_a = -LRU_C * rg * _softplus(-lam_ref[...])
    xb2 = jnp.concatenate([xb, xb], axis=1)
    la_ref[...] = jnp.exp(log_a)
    lu_ref[...] = jnp.sqrt(jnp.tanh(-log_a) * (jnp.exp(2.0 * log_a) + 1.0)) * (ig * xb2)
    gbg_ref[...] = jax.nn.gelu(zb_ref[:, 0:W_MIX])

    zr = zc_ref[:, 0:W_MIX]
    zk = zc_ref[:, W_MIX:2 * W_MIX]
    zv = zc_ref[:, 2 * W_MIX:3 * W_MIX]
    zwd = zl_ref[:, 0:2 * LORA_W]
    zad = zl_ref[:, 2 * LORA_W:2 * LORA_W + 2 * LORA_A]
    zgd = zl_ref[:, 2 * LORA_W + 2 * LORA_A:2 * LORA_W + 2 * LORA_A + LORA_G]
    wlin = w0_ref[...] + jnp.dot(jnp.tanh(zwd).astype(BF16), w2_ref[...], preferred_element_type=F32)
    w_ref[...] = jnp.exp(-jnp.exp(-_softplus(-wlin) - 0.5))
    av = jax.nn.sigmoid(a0_ref[...] + jnp.dot(zad.astype(BF16), a2_ref[...], preferred_element_type=F32))
    g_ref[...] = jnp.dot(jax.nn.sigmoid(zgd).astype(BF16), g2_ref[...], preferred_element_type=F32)
    kkr = zk * kkw_ref[...]
    kkn = kkr / jnp.maximum(jnp.sqrt(_segsum(kkr * kkr, ind)), 1e-12)
    zk2 = jnp.concatenate([zk, zk], axis=1)
    ka2 = jnp.concatenate([ka_ref[...], ka_ref[...]], axis=1)
    kd = zk2 * (1.0 + (av - 1.0) * ka2)
    k_ref[...] = kd
    b_ref[...] = jnp.concatenate([kkn, kkn], axis=1) * av
    r_ref[...] = zr
    v_ref[...] = zv
    kk_ref[...] = kkn
    bon_ref[...] = _segsum(zr * (kd[:, 0:W_MIX] + kd[:, W_MIX:2 * W_MIX]) * rk_ref[...], ind) * zv

    zg = jax.nn.gelu(zd_ref[...])
    u = zg[:, 0:W_MIX]
    vv = zg[:, W_MIX:2 * W_MIX]
    vc = vv - jnp.mean(vv, axis=-1, keepdims=True)
    vn = vc * lax.rsqrt(jnp.mean(vc * vc, axis=-1, keepdims=True) + 1e-5) * lng_ref[...] + lnb_ref[...]
    for c in range(TM // CHUNK):
        rs = slice(c * CHUNK, (c + 1) * CHUNK)
        for gi in range(G_SG):
            cs = slice(gi * LANES, (gi + 1) * LANES)
            s = jnp.dot(ws_ref[gi], vn[rs, cs].astype(BF16), preferred_element_type=F32)
            yd_ref[rs, cs] = u[rs, cs] * (s + bst_ref[:, gi:gi + 1])


def _prep(z, lw, n_ctx_tok, lat_len):
    t = z.shape[0]
    n_tiles = t // TM
    n_ctx_tiles = n_ctx_tok // TM
    tiles_per_seq = lat_len // TM
    rows8 = TM // SUBLANES
    last_blk = t // SUBLANES - 1
    bx_blk = (3072 + W_MIX) // W_MIX
    z_specs = [
        pl.BlockSpec((TM, 1536), lambda i: (i, 0)),
        pl.BlockSpec((TM, 1536), lambda i: (i, 1)),
        pl.BlockSpec((TM, 1024), lambda i: (i, 3)),
        pl.BlockSpec((TM, 1024), lambda i: (i, 4)),
        pl.BlockSpec((TM, 512), lambda i: (i, 10)),
        pl.BlockSpec((SUBLANES, W_MIX), lambda i: (jnp.maximum(i * rows8 - 1, 0), bx_blk)),
        pl.BlockSpec((SUBLANES, W_MIX), lambda i: (jnp.minimum((i + 1) * rows8, last_blk), bx_blk)),
    ]
    wnames = ["conv_a_w", "conv_b_w", "conv_b_b", "lru_w", "lru_b", "lru_lam", "w0", "w2", "a0", "a2",
              "g2", "kk", "ka", "rk", "sg_ln_g", "sg_ln_b", "sg_ws", "sg_bst", "ind"]
    wts = [lw[n] for n in wnames]
    w_specs = [_const_spec(w.shape) for w in wts]
    widths = [W_MIX, W_MIX, W_MIX, 2 * W_MIX, 2 * W_MIX, W_MIX, W_MIX, W_MIX,
              2 * W_MIX, 2 * W_MIX, 2 * W_MIX, W_MIX, W_MIX]
    out_specs = [pl.BlockSpec((TM, wd), lambda i: (i, 0)) for wd in widths]
    out_shape = [jax.ShapeDtypeStruct((t, wd), F32) for wd in widths]
    return pl.pallas_call(
        functools.partial(_prep_body, n_ctx_tiles=n_ctx_tiles, tiles_per_seq=tiles_per_seq),
        grid=(n_tiles,),
        in_specs=z_specs + w_specs,
        out_specs=out_specs,
        out_shape=out_shape,
        compiler_params=_cparams(("parallel",), VMEM_LIMIT),
        name="branch_prep",
    )(z, z, z, z, z, z, z, *wts)


def _lru_body(a_ref, u_ref, h0_ref, h_ref, hf_ref, hs, *, tc):
    @pl.when(pl.program_id(0) == 0)
    def _():
        hs[...] = h0_ref[...]

    def step(t, h):
        h = a_ref[t] * h + u_ref[t]
        h_ref[t] = h
        return h

    h = lax.fori_loop(0, tc, step, hs[...])
    hs[...] = h
    hf_ref[...] = h


def _lru_scan(a, u, h0, tc):
    l, n, w = a.shape
    return pl.pallas_call(
        functools.partial(_lru_body, tc=tc),
        grid=(l // tc,),
        in_specs=[pl.BlockSpec((tc, n, w), lambda i: (i, 0, 0)),
                  pl.BlockSpec((tc, n, w), lambda i: (i, 0, 0)),
                  _const_spec((n, w))],
        out_specs=[pl.BlockSpec((tc, n, w), lambda i: (i, 0, 0)), _const_spec((n, w))],
        out_shape=[jax.ShapeDtypeStruct((l, n, w), F32), jax.ShapeDtypeStruct((n, w), F32)],
        scratch_shapes=[pltpu.VMEM((n, w), F32)],
        compiler_params=_cparams(("arbitrary",), VMEM_LIMIT),
        name="lru_scan",
    )(a, u, h0)


def _wkv_body(r_ref, w_ref, k_ref, kk_ref, b_ref, v_ref, s0_ref, y_ref, sf_ref, s_scr, *, tc, vl):
    @pl.when(pl.program_id(1) == 0)
    def _():
        s_scr[...] = s0_ref[...]

    def step(t, carry):
        r = r_ref[t]
        w = w_ref[t]
        k = k_ref[t]
        kk = kk_ref[t]
        b = b_ref[t]

        def vrow(v, c):
            s = s_scr[v]
            sa = jnp.sum(s * kk, axis=0, keepdims=True)
            vv = v_ref[t, pl.ds(v, 1), :]
            sn = s * w - sa * b + vv * k
            s_scr[v] = sn
            y_ref[t, pl.ds(v, 1), :] = jnp.sum(sn * r, axis=0, keepdims=True)
            return c

        return lax.fori_loop(0, vl, vrow, carry, unroll=4)

    lax.fori_loop(0, tc, step, 0)
    sf_ref[...] = s_scr[...]


def _wkv_scan(r, w, k, kk, b, v, s0, tc):
    l, hs, nl = r.shape
    vl = v.shape[1]
    kspec = pl.BlockSpec((tc, hs, LANES), lambda g, i: (i, 0, g))
    vspec = pl.BlockSpec((tc, vl, LANES), lambda g, i: (i, 0, g))
    sspec = pl.BlockSpec((vl, hs, LANES), lambda g, i: (0, 0, g))
    return pl.pallas_call(
        functools.partial(_wkv_body, tc=tc, vl=vl),
        grid=(nl // LANES, l // tc),
        in_specs=[kspec, kspec, kspec, kspec, kspec, vspec, sspec],
        out_specs=[vspec, sspec],
        out_shape=[jax.ShapeDtypeStruct((l, vl, nl), F32), jax.ShapeDtypeStruct((vl, hs, nl), F32)],
        scratch_shapes=[pltpu.VMEM((vl, hs, LANES), F32)],
        compiler_params=_cparams(("parallel", "arbitrary"), VMEM_LIMIT),
        name="wkv_scan",
    )(r, w, k, kk, b, v, s0)


def _merge_body(x_ref, mod_ref, n1g_ref, n2g_ref, wg_ref, gb_ref, wbr_ref, wo_ref,
                lnxg_ref, lnxb_ref, ind_ref,
                ya_ref, yd_ref, gbg_ref, hf_ref, hb_ref, yf_ref, yb_ref, g_ref, bon_ref,
                x1_ref, h2_ref):
    x = x_ref[...]
    m = mod_ref[0]
    ind = ind_ref[...]
    h = (_rms(x, n1g_ref[...]) * (1.0 + m[1:2, :]) + m[0:1, :]).astype(BF16)
    y_b = gbg_ref[...] * (hf_ref[...] + hb_ref[...])
    y = yf_ref[...] + yb_ref[...]
    yc = y - _segsum(y, ind) * (1.0 / HS_WKV)
    var = _segsum(yc * yc, ind) * (1.0 / HS_WKV)
    y_c = (yc * lax.rsqrt(var + LNX_EPS) * lnxg_ref[...] + lnxb_ref[...] + bon_ref[...]) * g_ref[...]
    merged = None
    for n, yn in enumerate((ya_ref[...], y_b, y_c, yd_ref[...])):
        cs = slice(n * D_MODEL, (n + 1) * D_MODEL)
        gate = jax.nn.sigmoid(jnp.dot(h, wg_ref[:, cs], preferred_element_type=F32) + gb_ref[:, cs])
        br = jnp.dot(yn.astype(BF16), wbr_ref[n * W_MIX:(n + 1) * W_MIX, :], preferred_element_type=F32)
        merged = gate * br if merged is None else merged + gate * br
    mo = jnp.dot(merged.astype(BF16), wo_ref[...], preferred_element_type=F32)
    x1 = x + m[2:3, :] * mo
    x1_ref[...] = x1
    h2_ref[...] = (_rms(x1, n2g_ref[...]) * (1.0 + m[4:5, :]) + m[3:4, :]).astype(BF16)


def _merge(x, mod, lw, branch_in, n_ctx_tok, lat_len):
    t = x.shape[0]
    midx = functools.partial(_mod_index, tm=TM, n_ctx_tok=n_ctx_tok, lat_len=lat_len)
    wnames = ["norm1_g", "norm2_g", "w_gate", "gate_b", "w_branch", "w_out", "lnx_g", "lnx_b", "ind"]
    wts = [lw[n] for n in wnames]
    tok = lambda wd: pl.BlockSpec((TM, wd), lambda i: (i, 0))
    return pl.pallas_call(
        _merge_body,
        grid=(t // TM,),
        in_specs=([tok(D_MODEL), pl.BlockSpec((1, N_MOD, D_MODEL), lambda i: (midx(i), 0, 0))]
                  + [_const_spec(w.shape) for w in wts] + [tok(W_MIX)] * len(branch_in)),
        out_specs=[tok(D_MODEL), tok(D_MODEL)],
        out_shape=[jax.ShapeDtypeStruct((t, D_MODEL), F32), jax.ShapeDtypeStruct((t, D_MODEL), BF16)],
        compiler_params=_cparams(("parallel",), VMEM_LIMIT),
        name="merge",
    )(x, mod, *wts, *branch_in)


_CAND_VALID = (8, 8, 8, 5, 4, 3, 2, 2, 2, 8)


def _route_body(h2_ref, wqt_ref, keys_ref, i_ref, j_ref, g_ref, q_scr, i_scr, j_scr, g_scr):
    q_scr[...] = lax.dot_general(wqt_ref[...], h2_ref[...], _NT, preferred_element_type=F32)
    kio = lax.broadcasted_iota(I32, (N_KEYS, LANES), 0)
    sub = lax.broadcasted_iota(I32, (SUBLANES, LANES), 0)
    kid = lax.broadcasted_iota(I32, (PEER_TOPK, LANES), 0)
    n_cand = len(_CAND_VALID) * SUBLANES
    rid = lax.broadcasted_iota(I32, (n_cand, LANES), 0)
    neg = -jnp.inf

    def bc(x, r):
        return jnp.broadcast_to(x[r:r + 1, :], (SUBLANES, LANES))

    def head(h, carry):
        tops = []
        for p in range(2):
            row = pl.multiple_of(h * (2 * N_KEYS) + p * N_KEYS, N_KEYS)
            q = q_scr[pl.ds(row, N_KEYS), :].astype(BF16)
            s = jnp.dot(keys_ref[p], q, preferred_element_type=F32)
            vals = jnp.zeros((PEER_TOPK, LANES), F32)
            idxs = jnp.zeros((PEER_TOPK, LANES), I32)
            for r in range(PEER_TOPK):
                m = jnp.max(s, axis=0, keepdims=True)
                cand = jnp.where(s == m, kio, N_KEYS)
                ix = jnp.min(cand, axis=0, keepdims=True)
                s = jnp.where(cand == ix, neg, s)
                vals = jnp.where(kid == r, m, vals)
                idxs = jnp.where(kid == r, ix, idxs)
            tops.append((vals, idxs))
        (a0, i0), (a1, i1) = tops
        lo, hi = slice(0, SUBLANES), slice(SUBLANES, 2 * SUBLANES)
        slabs = [bc(a0, 0) + a1[lo], bc(a0, 0) + a1[hi]]
        ci = [bc(i0, 0), bc(i0, 0)]
        cj = [i1[lo], i1[hi]]
        for r in range(1, SUBLANES):
            slabs.append(bc(a0, r) + a1[lo])
            ci.append(bc(i0, r))
            cj.append(i1[lo])
        slabs.append(a0[hi] + bc(a1, 0))
        ci.append(i0[hi])
        cj.append(bc(i1, 0))
        slabs = [jnp.where(sub < nv, sl, neg) for sl, nv in zip(slabs, _CAND_VALID)]
        c = jnp.concatenate(slabs, axis=0)
        ci = jnp.concatenate(ci, axis=0)
        cj = jnp.concatenate(cj, axis=0)
        vals = jnp.zeros((PEER_TOPK, LANES), F32)
        isel = jnp.zeros((PEER_TOPK, LANES), I32)
        jsel = jnp.zeros((PEER_TOPK, LANES), I32)
        for r in range(PEER_TOPK):
            m = jnp.max(c, axis=0, keepdims=True)
            pos = jnp.min(jnp.where(c == m, rid, n_cand), axis=0, keepdims=True)
            sel = rid == pos
            isel = jnp.where(kid == r, jnp.sum(jnp.where(sel, ci, 0), axis=0, keepdims=True), isel)
            jsel = jnp.where(kid == r, jnp.sum(jnp.where(sel, cj, 0), axis=0, keepdims=True), jsel)
            vals = jnp.where(kid == r, m, vals)
            c = jnp.where(sel, neg, c)
        e = jnp.exp(vals - vals[0:1, :])
        out_row = pl.multiple_of(h * PEER_TOPK, PEER_TOPK)
        g_scr[pl.ds(out_row, PEER_TOPK), :] = e / jnp.sum(e, axis=0, keepdims=True)
        i_scr[pl.ds(out_row, PEER_TOPK), :] = isel
        j_scr[pl.ds(out_row, PEER_TOPK), :] = jsel
        return carry

    lax.fori_loop(0, PEER_HEADS, head, 0)
    i_ref[...] = i_scr[...].T
    j_ref[...] = j_scr[...].T
    g_ref[...] = g_scr[...].T


def _route(h2, wqt, keys):
    t = h2.shape[0]
    nsel = PEER_HEADS * PEER_TOPK
    tok = pl.BlockSpec((LANES, nsel), lambda i: (i, 0))
    return pl.pallas_call(
        _route_body,
        grid=(t // LANES,),
        in_specs=[pl.BlockSpec((LANES, D_MODEL), lambda i: (i, 0)),
                  _const_spec(wqt.shape), _const_spec(keys.shape)],
        out_specs=[tok, tok, tok],
        out_shape=[jax.ShapeDtypeStruct((t, nsel), I32), jax.ShapeDtypeStruct((t, nsel), I32),
                   jax.ShapeDtypeStruct((t, nsel), F32)],
        scratch_shapes=[pltpu.VMEM((wqt.shape[0], LANES), F32), pltpu.VMEM((nsel, LANES), I32),
                        pltpu.VMEM((nsel, LANES), I32), pltpu.VMEM((nsel, LANES), F32)],
        compiler_params=_cparams(("parallel",), VMEM_LIMIT),
        name="peer_route",
    )(h2, wqt, keys)


def _dense_body(h2_ref, i_ref, j_ref, g_ref, u_ref, v_ref, x1_ref, mod_ref, fng_ref, o_ref,
                gs_scr, acc_scr, *, rows, pitch, n_e, final):
    half = pl.program_id(1)
    e = pl.program_id(2)
    tm = h2_ref.shape[0]
    nsel = i_ref.shape[1]

    @pl.when(e == 0)
    def _build():
        sio = lax.broadcasted_iota(I32, (rows, nsel), 0) + half * rows
        jio = lax.broadcasted_iota(I32, (N_KEYS, nsel), 0)

        def tok(t, c):
            irow = i_ref[pl.ds(t, 1), :]
            jrow = j_ref[pl.ds(t, 1), :]
            grow = g_ref[pl.ds(t, 1), :]
            at = jnp.where(sio == irow, grow, 0.0).astype(BF16)
            bt = jnp.where(jio == jrow, 1.0, 0.0).astype(BF16)
            gs_scr[pl.ds(pl.multiple_of(t * pitch, SUBLANES), rows), :] = lax.dot_general(
                at, bt, _NT, preferred_element_type=F32)
            return c

        lax.fori_loop(0, tm, tok, 0, unroll=4)

    @pl.when(jnp.logical_and(half == 0, e == 0))
    def _zero():
        acc_scr[...] = jnp.zeros_like(acc_scr)

    hmat = lax.dot_general(h2_ref[...], u_ref[...], _NT, preferred_element_type=F32)
    per_step = u_ref.shape[0] // N_KEYS
    gm = jnp.concatenate(
        [gs_scr[pl.ds(e * per_step + ii, tm, stride=pitch), :] for ii in range(per_step)], axis=1)
    act = (jax.nn.gelu(hmat) * gm).astype(BF16)
    acc_scr[...] += jnp.dot(act, v_ref[...], preferred_element_type=F32)

    @pl.when(jnp.logical_and(half == pl.num_programs(1) - 1, e == n_e - 1))
    def _out():
        x2 = x1_ref[...] + mod_ref[0][5:6, :] * acc_scr[...]
        o_ref[...] = _rms(x2, fng_ref[...]) if final else x2


def _peer_dense(h2, isel, jsel, gsel, u, v, x1, mod, fng, n_ctx_tok, lat_len, final):
    t = h2.shape[0]
    nsel = isel.shape[1]
    rows = N_KEYS // PEER_SPLIT
    pitch = rows + SUBLANES
    n_e = (rows * N_KEYS) // PEER_EB
    midx = functools.partial(_mod_index, tm=TM_PEER, n_ctx_tok=n_ctx_tok, lat_len=lat_len)
    tok = lambda wd: pl.BlockSpec((TM_PEER, wd), lambda m, s, e: (m, 0))
    espec = pl.BlockSpec((PEER_EB, D_MODEL), lambda m, s, e: (s * n_e + e, 0))
    return pl.pallas_call(
        functools.partial(_dense_body, rows=rows, pitch=pitch, n_e=n_e, final=final),
        grid=(t // TM_PEER, PEER_SPLIT, n_e),
        in_specs=[tok(D_MODEL), tok(nsel), tok(nsel), tok(nsel), espec, espec, tok(D_MODEL),
                  pl.BlockSpec((1, N_MOD, D_MODEL), lambda m, s, e: (midx(m), 0, 0)),
                  _const_spec((1, D_MODEL))],
        out_specs=tok(D_MODEL),
        out_shape=jax.ShapeDtypeStruct((t, D_MODEL), F32),
        scratch_shapes=[pltpu.VMEM((TM_PEER * pitch, N_KEYS), F32), pltpu.VMEM((TM_PEER, D_MODEL), F32)],
        compiler_params=_cparams(("parallel", "arbitrary", "arbitrary"), VMEM_LIMIT),
        name="peer_dense",
    )(h2, isel, jsel, gsel, u, v, x1, mod, fng)


def _time_major_dirs(x, flip_second):
    return jnp.stack([x[0], jnp.flip(x[1], axis=1)]) if flip_second else x


def _lru_layout(x, bsz, l):
    x = x.reshape(bsz, l, N_DIR, W_MIX)
    x = jnp.stack([x[:, :, 0], jnp.flip(x[:, :, 1], axis=1)])
    return x.transpose(2, 0, 1, 3).reshape(l, N_DIR * bsz, W_MIX)


def _lru_unlayout(h, bsz, l):
    h = h.reshape(l, N_DIR, bsz, W_MIX)
    hf = h[:, 0].transpose(1, 0, 2).reshape(bsz * l, W_MIX)
    hb = jnp.flip(h[:, 1].transpose(1, 0, 2), axis=1).reshape(bsz * l, W_MIX)
    return hf, hb


def _wkv_dirs(x, bsz, l, per_dir):
    if per_dir:
        x = x.reshape(bsz, l, N_DIR, H_WKV, HS_WKV)
        f, b = x[:, :, 0], x[:, :, 1]
    else:
        f = b = x.reshape(bsz, l, H_WKV, HS_WKV)
    return jnp.stack([f, jnp.flip(b, axis=1)])


def _wkv_klayout(x, vs):
    d, bsz, l, h, hs = x.shape
    x = x.transpose(2, 4, 0, 1, 3).reshape(l, hs, d * bsz * h)
    return jnp.tile(x, (1, 1, vs)) if vs > 1 else x


def _wkv_vlayout(x, vs):
    d, bsz, l, h, hs = x.shape
    x = x.transpose(2, 4, 0, 1, 3).reshape(l, vs, hs // vs, d * bsz * h)
    return x.transpose(0, 2, 1, 3).reshape(l, hs // vs, vs * d * bsz * h)


def _wkv_unvlayout(y, bsz, vs):
    l = y.shape[0]
    y = y.reshape(l, HS_WKV // vs, vs, N_DIR, bsz, H_WKV).transpose(3, 4, 0, 5, 2, 1)
    y = y.reshape(N_DIR, bsz, l, W_MIX)
    return y[0].reshape(bsz * l, W_MIX), jnp.flip(y[1], axis=1).reshape(bsz * l, W_MIX)


def _wkv_state_layout(s, vs):
    bsz = s.shape[0]
    s = s.reshape(bsz, N_DIR, H_WKV, vs, HS_WKV // vs, HS_WKV).transpose(4, 5, 3, 1, 0, 2)
    return s.reshape(HS_WKV // vs, HS_WKV, vs * N_DIR * bsz * H_WKV)


def _wkv_state_unlayout(s, bsz, vs):
    s = s.reshape(HS_WKV // vs, HS_WKV, vs, N_DIR, bsz, H_WKV).transpose(4, 3, 5, 2, 0, 1)
    return s.reshape(bsz, N_DIR, H_WKV, HS_WKV, HS_WKV)


def _lane_split(n_inst):
    if n_inst >= LANES:
        assert n_inst % LANES == 0
        return 1
    assert LANES % n_inst == 0
    return LANES // n_inst


def _run_scans(p, bsz, l, tok0, lru0, wkv0, tc_lru, tc_wkv):
    sl = slice(tok0, tok0 + bsz * l)
    a = _lru_layout(p["la"][sl], bsz, l)
    u = _lru_layout(p["lu"][sl], bsz, l)
    h0 = lru0.transpose(1, 0, 2).reshape(N_DIR * bsz, W_MIX)
    h, hfin = _lru_scan(a, u, h0, tc_lru)
    hf, hb = _lru_unlayout(h, bsz, l)
    lru_state = hfin.reshape(N_DIR, bsz, W_MIX).transpose(1, 0, 2)

    vs = _lane_split(N_DIR * bsz * H_WKV)
    kin = [_wkv_klayout(_wkv_dirs(p[n][sl], bsz, l, pd), vs)
           for n, pd in (("r", False), ("w", True), ("k", True), ("kk", False), ("b", True))]
    vin = _wkv_vlayout(_wkv_dirs(p["v"][sl], bsz, l, False), vs)
    y, sfin = _wkv_scan(*kin, vin, _wkv_state_layout(wkv0, vs), tc_wkv)
    yf, yb = _wkv_unvlayout(y, bsz, vs)
    return hf, hb, yf, yb, lru_state, _wkv_state_unlayout(sfin, bsz, vs)


def _layer_weights(i, prm):
    eye_h = jnp.eye(H_LRU, dtype=F32)
    eye_d = jnp.eye(N_DIR, dtype=F32)

    def lru_bd(wt):
        return jnp.einsum("dhij,hg->hidgj", wt, eye_h).reshape(W_MIX, N_DIR * W_MIX)

    def lora_bd(wt):
        r = wt.shape[1]
        return jnp.einsum("drc,de->drec", wt, eye_d).reshape(N_DIR * r, N_DIR * W_MIX)

    w_in = prm["w_in"][i]
    pad = jnp.zeros((D_MODEL, Z_COLS - 5504), F32)
    w_in_perm = jnp.concatenate(
        [w_in[:, 0:1536], w_in[:, 2560:4096], w_in[:, 1536:2560], w_in[:, 4480:5504],
         w_in[:, 4096:4480], pad], axis=1).astype(BF16)
    row = lambda x: x.reshape(1, -1).astype(F32)
    return {
        "w_in": w_in_perm,
        "w_gate": w_in[:, 5504:].astype(BF16),
        "norm1_g": row(prm["norm1_g"][i]),
        "norm2_g": row(prm["norm2_g"][i]),
        "conv_a_w": prm["conv_a_w"][i],
        "conv_b_w": prm["conv_b_w"][i],
        "conv_b_b": row(prm["conv_b_b"][i]),
        "lru_w": jnp.concatenate([lru_bd(prm["lru_wa"][i]), lru_bd(prm["lru_wx"][i])], axis=1).astype(BF16),
        "lru_b": jnp.concatenate([row(prm["lru_ba"][i]), row(prm["lru_bx"][i])], axis=1),
        "lru_lam": row(prm["lru_lambda"][i]),
        "w0": row(prm["rwkv_w0"][i]),
        "w2": lora_bd(prm["rwkv_w2"][i]).astype(BF16),
        "a0": row(prm["rwkv_a0"][i]),
        "a2": lora_bd(prm["rwkv_a2"][i]).astype(BF16),
        "g2": prm["rwkv_g2"][i].astype(BF16),
        "kk": row(prm["rwkv_kk"][i]),
        "ka": row(prm["rwkv_ka"][i]),
        "rk": row(prm["rwkv_rk"][i]),
        "lnx_g": row(prm["lnx_g"][i]),
        "lnx_b": row(prm["lnx_b"][i]),
        "sg_ln_g": row(prm["sg_ln_g"][i]),
        "sg_ln_b": row(prm["sg_ln_b"][i]),
        "sg_ws": prm["sg_ws"][i].astype(BF16),
        "sg_bst": prm["sg_bs"][i].T,
        "gate_b": row(prm["gate_b"][i]),
        "w_branch": prm["w_branch"][i].reshape(N_BRANCH * W_MIX, D_MODEL).astype(BF16),
        "w_out": prm["w_out"][i].astype(BF16),
        "wq_t": prm["peer_wq"][i].T.astype(BF16),
        "keys": prm["peer_keys"][i].astype(BF16),
        "peer_u": prm["peer_u"][i].astype(BF16),
        "peer_v": prm["peer_v"][i].astype(BF16),
        "ind": jnp.kron(jnp.eye(H_WKV, dtype=F32), jnp.ones((HS_WKV, HS_WKV), F32)).astype(BF16),
    }


def kernel(x_prompt, x_sample, state_lru, state_wkv, c, c_ctx, norm1_g, norm2_g, w_mod, b_mod, w_in, conv_a_w, conv_b_w, conv_b_b, lru_wa, lru_ba, lru_wx, lru_bx, lru_lambda, rwkv_w0, rwkv_w2, rwkv_a0, rwkv_a2, rwkv_g2, rwkv_kk, rwkv_ka, rwkv_rk, lnx_g, lnx_b, sg_ln_g, sg_ln_b, sg_ws, sg_bs, gate_b, w_branch, w_out, peer_wq, peer_keys, peer_u, peer_v, final_norm_g):
    prm = dict(norm1_g=norm1_g, norm2_g=norm2_g, w_in=w_in, conv_a_w=conv_a_w, conv_b_w=conv_b_w,
               conv_b_b=conv_b_b, lru_wa=lru_wa, lru_ba=lru_ba, lru_wx=lru_wx, lru_bx=lru_bx,
               lru_lambda=lru_lambda, rwkv_w0=rwkv_w0, rwkv_w2=rwkv_w2, rwkv_a0=rwkv_a0, rwkv_a2=rwkv_a2,
               rwkv_g2=rwkv_g2, rwkv_kk=rwkv_kk, rwkv_ka=rwkv_ka, rwkv_rk=rwkv_rk, lnx_g=lnx_g,
               lnx_b=lnx_b, sg_ln_g=sg_ln_g, sg_ln_b=sg_ln_b, sg_ws=sg_ws, sg_bs=sg_bs, gate_b=gate_b,
               w_branch=w_branch, w_out=w_out, peer_wq=peer_wq, peer_keys=peer_keys, peer_u=peer_u,
               peer_v=peer_v)
    bc, lc, _ = x_prompt.shape
    bl, ll, _ = x_sample.shape
    depth = w_mod.shape[0]
    n_ctx_tok = bc * lc
    assert lc == TM and ll % TM_IN == 0 and n_ctx_tok % TM_IN == 0 and bl + 1 <= SUBLANES
    assert ll % GRID_W == 0 and TM % GRID_W == 0

    cond = jnp.zeros((SUBLANES, D_MODEL), F32).at[0].set(c_ctx).at[1:1 + bl].set(c)
    mods = _modulation(cond, w_mod, b_mod).reshape(depth, SUBLANES, N_MOD, D_MODEL)
    fng = final_norm_g.reshape(1, D_MODEL)
    x = jnp.concatenate([x_prompt.reshape(n_ctx_tok, D_MODEL), x_sample.reshape(bl * ll, D_MODEL)], axis=0)
    lru_zero = jnp.zeros((bc, N_DIR, W_MIX), F32)
    wkv_zero = jnp.zeros((bc, N_DIR, H_WKV, HS_WKV, HS_WKV), F32)
    new_lru, new_wkv = [], []
    pnames = ["ya", "yd", "gbg", "la", "lu", "r", "v", "kk", "w", "k", "b", "g", "bon"]
    for i in range(depth):
        lw = _layer_weights(i, prm)
        mod = mods[i]
        z = _in_proj(x, mod, lw["norm1_g"], lw["w_in"], n_ctx_tok, ll)
        p = dict(zip(pnames, _prep(z, lw, n_ctx_tok, ll)))
        c_hf, c_hb, c_yf, c_yb, lru_s, wkv_s = _run_scans(p, bc, lc, 0, lru_zero, wkv_zero, 16, 32)
        l_hf, l_hb, l_yf, l_yb, _, _ = _run_scans(
            p, bl, ll, n_ctx_tok, state_lru[:, i].astype(F32), state_wkv[:, i].astype(F32), 128, 64)
        new_lru.append(lru_s)
        new_wkv.append(wkv_s)
        cat = lambda a, b: jnp.concatenate([a, b], axis=0)
        branch_in = [p["ya"], p["yd"], p["gbg"], cat(c_hf, l_hf), cat(c_hb, l_hb),
                     cat(c_yf, l_yf), cat(c_yb, l_yb), p["g"], p["bon"]]
        x1, h2 = _merge(x, mod, lw, branch_in, n_ctx_tok, ll)
        isel, jsel, gsel = _route(h2, lw["wq_t"], lw["keys"])
        x = _peer_dense(h2, isel, jsel, gsel, lw["peer_u"], lw["peer_v"], x1, mod, fng,
                        n_ctx_tok, ll, final=(i == depth - 1))
    y_prompt = x[:n_ctx_tok].reshape(bc, lc, D_MODEL)
    y_sample = x[n_ctx_tok:].reshape(bl, ll, D_MODEL)
    return (y_prompt, y_sample, jnp.stack(new_lru, axis=1), jnp.stack(new_wkv, axis=1))
```

```python
import functools

import numpy as np
import jax
import jax.numpy as jnp
from jax import lax
from jax.experimental import pallas as pl
from jax.experimental.pallas import tpu as pltpu

F32 = jnp.float32
BF16 = jnp.bfloat16
I32 = jnp.int32

D_MODEL = 1024
W_MIX = 512
N_DIR = 2
N_BRANCH = 4
H_WKV = 8
HS_WKV = 64
H_LRU = 8
HB_LRU = 64
LORA_W = 64
LORA_A = 64
LORA_G = 128
GRID_W = 64
CHUNK = 128
G_SG = 4
N_KEYS = 128
PEER_HEADS = 8
PEER_TOPK = 16
N_MOD = 6
EPS = 1e-6
LNX_EPS = 64e-5
LRU_C = 8.0

LANES = 128
SUBLANES = 8
TM = 256
TM_IN = 512
TN_IN = 1408
TM_PEER = 512
PEER_SPLIT = 2
PEER_EB = 512
Z_COLS = 5632
WKV_TC = LANES
WKV_PITCH = WKV_TC + SUBLANES
WKV_CTX_SPB = 8
LRU_CTX_SPB = 4
VMEM_LIMIT = 56 * 1024 * 1024

_NT = (((1,), (1,)), ((), ()))
_WKV_PERM = np.array([(n % H_WKV) * HS_WKV + n // H_WKV for n in range(W_MIX)])


def _cparams(sem, vmem=None):
    return pltpu.CompilerParams(dimension_semantics=sem, vmem_limit_bytes=vmem)


def _const_spec(shape):
    nd = len(shape)
    return pl.BlockSpec(shape, lambda *_: (0,) * nd)


def _softplus(x):
    return jnp.maximum(x, 0.0) + jnp.log1p(jnp.exp(-jnp.abs(x)))


def _rms(x, g):
    return x * lax.rsqrt(jnp.mean(x * x, axis=-1, keepdims=True) + EPS) * g


def _segsum(x, ind):
    hi = x.astype(BF16)
    lo = (x - hi.astype(F32)).astype(BF16)
    return (jnp.dot(hi, ind, preferred_element_type=F32)
            + jnp.dot(lo, ind, preferred_element_type=F32))


def _mod_index(i, tm, n_ctx_tok, lat_len):
    n_ctx_tiles = n_ctx_tok // tm
    tiles_per_seq = lat_len // tm
    return jnp.where(i < n_ctx_tiles, 0, 1 + lax.div(i - n_ctx_tiles, tiles_per_seq))


def _mod_body(s_ref, w_ref, b_ref, o_ref):
    s = s_ref[...]
    s = s * jax.nn.sigmoid(s)
    o_ref[0] = jnp.dot(s.astype(BF16), w_ref[0].astype(BF16), preferred_element_type=F32) + b_ref[0]


def _modulation(cond, w_mod, b_mod):
    depth = w_mod.shape[0]
    n = w_mod.shape[2]
    tn = 1536
    return pl.pallas_call(
        _mod_body,
        grid=(depth, n // tn),
        in_specs=[_const_spec((SUBLANES, D_MODEL)),
                  pl.BlockSpec((1, D_MODEL, tn), lambda l, j: (l, 0, j)),
                  pl.BlockSpec((1, 1, tn), lambda l, j: (l, 0, j))],
        out_specs=pl.BlockSpec((1, SUBLANES, tn), lambda l, j: (l, 0, j)),
        out_shape=jax.ShapeDtypeStruct((depth, SUBLANES, n), F32),
        compiler_params=_cparams(("parallel", "parallel"), VMEM_LIMIT),
        name="modulation",
    )(cond, w_mod, b_mod.reshape(depth, 1, n))


def _in_body(x_ref, mod_ref, g_ref, w_ref, o_ref, h_scr):
    @pl.when(pl.program_id(1) == 0)
    def _():
        m = mod_ref[0]
        y = _rms(x_ref[...], g_ref[...])
        h_scr[...] = (y * (1.0 + m[1:2, :]) + m[0:1, :]).astype(BF16)

    o_ref[...] = jnp.dot(h_scr[...], w_ref[...], preferred_element_type=F32)


def _in_proj(x, mod, g, w, n_ctx_tok, lat_len):
    t = x.shape[0]
    midx = functools.partial(_mod_index, tm=TM_IN, n_ctx_tok=n_ctx_tok, lat_len=lat_len)
    return pl.pallas_call(
        _in_body,
        grid=(t // TM_IN, Z_COLS // TN_IN),
        in_specs=[pl.BlockSpec((TM_IN, D_MODEL), lambda i, j: (i, 0)),
                  pl.BlockSpec((1, N_MOD, D_MODEL), lambda i, j: (midx(i), 0, 0)),
                  _const_spec((1, D_MODEL)),
                  pl.BlockSpec((D_MODEL, TN_IN), lambda i, j: (0, j))],
        out_specs=pl.BlockSpec((TM_IN, TN_IN), lambda i, j: (i, j)),
        out_shape=jax.ShapeDtypeStruct((t, Z_COLS), F32),
        scratch_shapes=[pltpu.VMEM((TM_IN, D_MODEL), BF16)],
        compiler_params=_cparams(("parallel", "arbitrary"), VMEM_LIMIT),
        name="in_proj",
    )(x, mod, g, w)


def _prep_body(za_ref, zc_ref, zb_ref, zd_ref, zl_ref, hp_ref, hn_ref,
               caw_ref, cbw_ref, cbb_ref, lruw_ref, lrub_ref, lam_ref,
               w0_ref, w2_ref, a0_ref, a2_ref, g2_ref, kkw_ref, ka_ref, rk_ref,
               lng_ref, lnb_ref, ws_ref, bst_ref, ind_ref,
               ya_ref, yd_ref, gbg_ref, la_ref, lu_ref, g_ref, bon_ref,
               rt_ref, vt_ref, kkt_ref, wt_ref, kt_ref, bt_ref,
               *, n_ctx_tiles, tiles_per_seq):
    i = pl.program_id(0)
    is_ctx = i < n_ctx_tiles
    t = lax.broadcasted_iota(I32, (TM, 1), 0)
    ind = ind_ref[...]

    pm = jnp.where(is_ctx, TM - 1, GRID_W - 1)
    pos = t & pm
    a_b = za_ref[:, 0:W_MIX]
    ac = za_ref[:, W_MIX:2 * W_MIX] * za_ref[:, 2 * W_MIX:3 * W_MIX]
    up = jnp.where(pos == 0, 0.0, pltpu.roll(ac, 1, 0))
    dn = jnp.where(pos == pm, 0.0, pltpu.roll(ac, TM - 1, 0))
    ya_ref[...] = a_b * (caw_ref[0:1, :] * up + caw_ref[1:2, :] * ac + caw_ref[2:3, :] * dn)

    seq_tile = lax.rem(jnp.maximum(i - n_ctx_tiles, 0), tiles_per_seq)
    first = jnp.logical_or(is_ctx, seq_tile == 0)
    last = jnp.logical_or(is_ctx, seq_tile == tiles_per_seq - 1)
    prev = jnp.where(first, 0.0, hp_ref[SUBLANES - 1:SUBLANES, :])
    nxt0 = jnp.where(last, 0.0, hn_ref[0:1, :])
    nxt1 = jnp.where(last, 0.0, hn_ref[1:2, :])
    bx = zb_ref[:, W_MIX:2 * W_MIX]
    m1 = jnp.where(t == 0, prev, pltpu.roll(bx, 1, 0))
    p1 = jnp.where(t == TM - 1, nxt0, pltpu.roll(bx, TM - 1, 0))
    p2 = jnp.where(t == TM - 2, nxt0, jnp.where(t == TM - 1, nxt1, pltpu.roll(bx, TM - 2, 0)))
    xb = (cbw_ref[0:1, :] * m1 + cbw_ref[1:2, :] * bx + cbw_ref[2:3, :] * p1
          + cbw_ref[3:4, :] * p2 + cbb_ref[...])
    gates = jnp.dot(xb.astype(BF16), lruw_ref[...], preferred_element_type=F32) + lrub_ref[...]
    rg = jax.nn.sigmoid(gates[:, 0:2 * W_MIX])
    ig = jax.nn.sigmoid(gates[:, 2 * W_MIX:4 * W_MIX])
    log_a = -LRU_C * rg * _softplus(-lam_ref[...])
    xb2 = jnp.concatenate([xb, xb], axis=1)
    la_ref[...] = jnp.exp(log_a)
    lu_ref[...] = jnp.sqrt(jnp.tanh(-log_a) * (jnp.exp(2.0 * log_a) + 1.0)) * (ig * xb2)
    gbg_ref[...] = jax.nn.gelu(zb_ref[:, 0:W_MIX])

    zr = zc_ref[:, 0:W_MIX]
    zk = zc_ref[:, W_MIX:2 * W_MIX]
    zv = zc_ref[:, 2 * W_MIX:3 * W_MIX]
    zwd = zl_ref[:, 0:2 * LORA_W]
    zad = zl_ref[:, 2 * LORA_W:2 * LORA_W + 2 * LORA_A]
    zgd = zl_ref[:, 2 * LORA_W + 2 * LORA_A:2 * LORA_W + 2 * LORA_A + LORA_G]
    wlin = w0_ref[...] + jnp.dot(jnp.tanh(zwd).astype(BF16), w2_ref[...], preferred_element_type=F32)
    wt_ref[0] = jnp.exp(-jnp.exp(-_softplus(-wlin) - 0.5)).T
    av = jax.nn.sigmoid(a0_ref[...] + jnp.dot(zad.astype(BF16), a2_ref[...], preferred_element_type=F32))
    g_ref[...] = jnp.dot(jax.nn.sigmoid(zgd).astype(BF16), g2_ref[...], preferred_element_type=F32)
    kkr = zk * kkw_ref[...]
    kkn = kkr / jnp.maximum(jnp.sqrt(_segsum(kkr * kkr, ind)), 1e-12)
    zk2 = jnp.concatenate([zk, zk], axis=1)
    ka2 = jnp.concatenate([ka_ref[...], ka_ref[...]], axis=1)
    kd = zk2 * (1.0 + (av - 1.0) * ka2)
    kt_ref[0] = kd.T
    bt_ref[0] = (jnp.concatenate([kkn, kkn], axis=1) * av).T
    rt_ref[0] = zr.T
    vt_ref[0] = zv.T
    kkt_ref[0] = kkn.T
    bon_ref[...] = _segsum(zr * (kd[:, 0:W_MIX] + kd[:, W_MIX:2 * W_MIX]) * rk_ref[...], ind) * zv

    zg = jax.nn.gelu(zd_ref[...])
    u = zg[:, 0:W_MIX]
    vv = zg[:, W_MIX:2 * W_MIX]
    vc = vv - jnp.mean(vv, axis=-1, keepdims=True)
    vn = vc * lax.rsqrt(jnp.mean(vc * vc, axis=-1, keepdims=True) + 1e-5) * lng_ref[...] + lnb_ref[...]
    for c in range(TM // CHUNK):
        rs = slice(c * CHUNK, (c + 1) * CHUNK)
        for gi in range(G_SG):
            cs = slice(gi * LANES, (gi + 1) * LANES)
            s = jnp.dot(ws_ref[gi], vn[rs, cs].astype(BF16), preferred_element_type=F32)
            yd_ref[rs, cs] = u[rs, cs] * (s + bst_ref[:, gi:gi + 1])


def _prep(z, lw, n_ctx_tok, lat_len):
    t = z.shape[0]
    n_tiles = t // TM
    n_ctx_tiles = n_ctx_tok // TM
    tiles_per_seq = lat_len // TM
    rows8 = TM // SUBLANES
    last_blk = t // SUBLANES - 1
    bx_blk = (3072 + W_MIX) // W_MIX
    z_specs = [
        pl.BlockSpec((TM, 1536), lambda i: (i, 0)),
        pl.BlockSpec((TM, 1536), lambda i: (i, 1)),
        pl.BlockSpec((TM, 1024), lambda i: (i, 3)),
        pl.BlockSpec((TM, 1024), lambda i: (i, 4)),
        pl.BlockSpec((TM, 512), lambda i: (i, 10)),
        pl.BlockSpec((SUBLANES, W_MIX), lambda i: (jnp.maximum(i * rows8 - 1, 0), bx_blk)),
        pl.BlockSpec((SUBLANES, W_MIX), lambda i: (jnp.minimum((i + 1) * rows8, last_blk), bx_blk)),
    ]
    wnames = ["conv_a_w", "conv_b_w", "conv_b_b", "lru_w", "lru_b", "lru_lam", "w0", "w2", "a0", "a2",
              "g2", "kk", "ka", "rk", "sg_ln_g", "sg_ln_b", "sg_ws", "sg_bst", "ind"]
    wts = [lw[n] for n in wnames]
    w_specs = [_const_spec(w.shape) for w in wts]
    widths = [W_MIX, W_MIX, W_MIX, 2 * W_MIX, 2 * W_MIX, W_MIX, W_MIX]
    t_rows = [W_MIX, W_MIX, W_MIX, 2 * W_MIX, 2 * W_MIX, 2 * W_MIX]
    out_specs = ([pl.BlockSpec((TM, wd), lambda i: (i, 0)) for wd in widths]
                 + [pl.BlockSpec((1, r, TM), lambda i: (i, 0, 0)) for r in t_rows])
    out_shape = ([jax.ShapeDtypeStruct((t, wd), F32) for wd in widths]
                 + [jax.ShapeDtypeStruct((n_tiles, r, TM), F32) for r in t_rows])
    return pl.pallas_call(
        functools.partial(_prep_body, n_ctx_tiles=n_ctx_tiles, tiles_per_seq=tiles_per_seq),
        grid=(n_tiles,),
        in_specs=z_specs + w_specs,
        out_specs=out_specs,
        out_shape=out_shape,
        compiler_params=_cparams(("parallel",), VMEM_LIMIT),
        name="branch_prep",
    )(z, z, z, z, z, z, z, *wts)


def _lru_body(a_ref, u_ref, h0_ref, h_ref, hf_ref, *, nseq, l):
    fw, bw = slice(0, W_MIX), slice(W_MIX, 2 * W_MIX)

    def step(s, carry):
        out = []
        for j in range(nseq):
            tf = j * l + s
            tb = j * l + (l - 1 - s)
            hf = a_ref[pl.ds(tf, 1), fw] * carry[2 * j] + u_ref[pl.ds(tf, 1), fw]
            hb = a_ref[pl.ds(tb, 1), bw] * carry[2 * j + 1] + u_ref[pl.ds(tb, 1), bw]
            h_ref[pl.ds(tf, 1), fw] = hf
            h_ref[pl.ds(tb, 1), bw] = hb
            out += [hf, hb]
        return tuple(out)

    init = []
    for j in range(nseq):
        init += [h0_ref[0, j:j + 1, fw], h0_ref[0, j:j + 1, bw]]
    fin = lax.fori_loop(0, l, step, tuple(init), unroll=2)
    for j in range(nseq):
        hf_ref[0, j:j + 1, fw] = fin[2 * j]
        hf_ref[0, j:j + 1, bw] = fin[2 * j + 1]


def _lru_scan(a, u, h0, l, nseq):
    n = a.shape[0] // l
    nb = n // nseq
    w = a.shape[1]
    tok = pl.BlockSpec((nseq * l, w), lambda i: (i, 0))
    st = pl.BlockSpec((1, nseq, w), lambda i: (i, 0, 0))
    h, hf = pl.pallas_call(
        functools.partial(_lru_body, nseq=nseq, l=l),
        grid=(nb,),
        in_specs=[tok, tok, st],
        out_specs=[tok, st],
        out_shape=[jax.ShapeDtypeStruct(a.shape, F32), jax.ShapeDtypeStruct((nb, nseq, w), F32)],
        compiler_params=_cparams(("parallel",), VMEM_LIMIT),
        name="lru_scan",
    )(a, u, h0.reshape(nb, nseq, w))
    return h, hf.reshape(n, w)


def _wkv_body(*refs, nsrc, spb, vs, n_sb, tc, pitch):
    vl_n = HS_WKV // vs
    n_in = 6 * nsrc
    k_srcs = [refs[o * nsrc:(o + 1) * nsrc] for o in range(5)]
    v_srcs = refs[5 * nsrc:n_in]
    s0_ref = refs[n_in]
    y_refs = refs[n_in + 1:n_in + 1 + nsrc]
    sf_ref = refs[n_in + 1 + nsrc]
    k_scr = refs[n_in + 2 + nsrc:n_in + 7 + nsrc]
    v_scr, y_scr, s_scr = refs[n_in + 7 + nsrc:n_in + 10 + nsrc]
    backward = pl.program_id(0) // n_sb == 1
    seqs = [(s, j) for s in range(nsrc) for j in range(spb)]

    @pl.when(pl.program_id(1) == 0)
    def _():
        s_scr[...] = s0_ref[0]

    def build_k(c, carry):
        row = pl.multiple_of(c * H_WKV, H_WKV)
        dst = pl.multiple_of(c * pitch, SUBLANES)
        for o in range(5):
            slab = [k_srcs[o][s][j, pl.ds(row, H_WKV), :] for s, j in seqs]
            k_scr[o][pl.ds(dst, tc), :] = jnp.concatenate(slab * vs, axis=0).T
        return carry

    lax.fori_loop(0, HS_WKV, build_k, 0)

    def build_v(vl, carry):
        slab = []
        for vsi in range(vs):
            row = pl.multiple_of((vsi * vl_n + vl) * H_WKV, H_WKV)
            slab += [v_srcs[s][j, pl.ds(row, H_WKV), :] for s, j in seqs]
        v_scr[pl.ds(pl.multiple_of(vl * pitch, SUBLANES), tc), :] = jnp.concatenate(slab, axis=0).T
        return carry

    lax.fori_loop(0, vl_n, build_v, 0)

    def step(s, carry):
        t = jnp.where(backward, tc - 1 - s, s)
        r, w, k, kk, b = [k_scr[o][pl.ds(t, HS_WKV, stride=pitch), :] for o in range(5)]

        def vrow(vl, c):
            st = s_scr[vl]
            sa = jnp.sum(st * kk, axis=0, keepdims=True)
            vv = v_scr[pl.ds(vl * pitch + t, 1), :]
            sn = st * w - sa * b + vv * k
            s_scr[vl] = sn
            y_scr[pl.ds(vl * pitch + t, 1), :] = jnp.sum(sn * r, axis=0, keepdims=True)
            return c

        return lax.fori_loop(0, vl_n, vrow, carry, unroll=2)

    lax.fori_loop(0, tc, step, 0)

    def emit_y(vl, carry):
        yt = y_scr[pl.ds(pl.multiple_of(vl * pitch, SUBLANES), tc), :].T
        for vsi in range(vs):
            row = pl.multiple_of((vsi * vl_n + vl) * H_WKV, H_WKV)
            for n, (s, j) in enumerate(seqs):
                lane0 = (vsi * len(seqs) + n) * H_WKV
                y_refs[s][0, j, pl.ds(row, H_WKV), :] = yt[lane0:lane0 + H_WKV, :]
        return carry

    lax.fori_loop(0, vl_n, emit_y, 0)
    sf_ref[0] = s_scr[...]


def _wkv_scan(rt, wt, kt, kkt, bt, vt, s0, *, tile0, n_seq, seq_tiles, spb):
    tc, pitch = WKV_TC, WKV_PITCH
    if spb > 1:
        assert seq_tiles == 1 and n_seq % spb == 0 and tile0 % spb == 0
        nsrc, n_sb = 1, n_seq // spb
    else:
        nsrc, n_sb = n_seq, 1
    inst = nsrc * spb * H_WKV
    vs = LANES // inst
    vl_n = HS_WKV // vs
    cpt = TM // tc
    n_chunks = seq_tiles * cpt

    def chunk(g, i):
        return jnp.where(g // n_sb == 1, n_chunks - 1 - i, i)

    def in_map(g, i, *, src, per_dir):
        ce = chunk(g, i)
        rb = (g // n_sb) if per_dir else 0
        if spb > 1:
            return (tile0 // spb + g % n_sb, rb, ce)
        return (tile0 + src * seq_tiles + ce // cpt, rb, ce % cpt)

    def out_map(g, i, *, src):
        ce = chunk(g, i)
        if spb > 1:
            return (g // n_sb, g % n_sb, 0, ce)
        return (g // n_sb, ce // cpt, 0, ce % cpt)

    in_specs, operands = [], []
    for arr, per_dir in ((rt, False), (wt, True), (kt, True), (kkt, False), (bt, True), (vt, False)):
        for src in range(nsrc):
            in_specs.append(pl.BlockSpec((spb, W_MIX, tc), functools.partial(in_map, src=src, per_dir=per_dir),
                                         pipeline_mode=pl.Buffered(1)))
            operands.append(arr)
    sspec = pl.BlockSpec((1, vl_n, HS_WKV, LANES), lambda g, i: (g, 0, 0, 0))
    in_specs.append(sspec)
    n_out_tiles = n_seq * seq_tiles // nsrc
    out_specs = [pl.BlockSpec((1, spb, W_MIX, tc), functools.partial(out_map, src=src)) for src in range(nsrc)]
    out_shape = [jax.ShapeDtypeStruct((N_DIR, n_out_tiles, W_MIX, TM), F32) for _ in range(nsrc)]
    res = pl.pallas_call(
        functools.partial(_wkv_body, nsrc=nsrc, spb=spb, vs=vs, n_sb=n_sb, tc=tc, pitch=pitch),
        grid=(N_DIR * n_sb, n_chunks),
        in_specs=in_specs,
        out_specs=out_specs + [sspec],
        out_shape=out_shape + [jax.ShapeDtypeStruct(s0.shape, F32)],
        scratch_shapes=([pltpu.VMEM((HS_WKV * pitch, LANES), F32)] * 5
                        + [pltpu.VMEM((vl_n * pitch, LANES), F32)] * 2
                        + [pltpu.VMEM((vl_n, HS_WKV, LANES), F32)]),
        compiler_params=_cparams(("parallel", "arbitrary"), VMEM_LIMIT),
        name="wkv_scan",
    )(*operands, s0)
    return res[:nsrc], res[nsrc]


def _merge_body(x_ref, mod_ref, n1g_ref, n2g_ref, wg_ref, gb_ref, wbr_ref, wo_ref,
                lnxg_ref, lnxb_ref, ind_ref,
                ya_ref, yd_ref, gbg_ref, h_ref, yf_ref, yb_ref, g_ref, bon_ref,
                x1_ref, h2_ref):
    x = x_ref[...]
    m = mod_ref[0]
    ind = ind_ref[...]
    h = (_rms(x, n1g_ref[...]) * (1.0 + m[1:2, :]) + m[0:1, :]).astype(BF16)
    y_b = gbg_ref[...] * (h_ref[:, 0:W_MIX] + h_ref[:, W_MIX:2 * W_MIX])
    y = (yf_ref[0, 0] + yb_ref[0, 0]).T
    yc = y - _segsum(y, ind) * (1.0 / HS_WKV)
    var = _segsum(yc * yc, ind) * (1.0 / HS_WKV)
    y_c = (yc * lax.rsqrt(var + LNX_EPS) * lnxg_ref[...] + lnxb_ref[...] + bon_ref[...]) * g_ref[...]
    merged = None
    for n, yn in enumerate((ya_ref[...], y_b, y_c, yd_ref[...])):
        cs = slice(n * D_MODEL, (n + 1) * D_MODEL)
        gate = jax.nn.sigmoid(jnp.dot(h, wg_ref[:, cs], preferred_element_type=F32) + gb_ref[:, cs])
        br = jnp.dot(yn.astype(BF16), wbr_ref[n * W_MIX:(n + 1) * W_MIX, :], preferred_element_type=F32)
        merged = gate * br if merged is None else merged + gate * br
    mo = jnp.dot(merged.astype(BF16), wo_ref[...], preferred_element_type=F32)
    x1 = x + m[2:3, :] * mo
    x1_ref[...] = x1
    h2_ref[...] = (_rms(x1, n2g_ref[...]) * (1.0 + m[4:5, :]) + m[3:4, :]).astype(BF16)


def _merge(x, mod, lw, tok_in, y_all, n_ctx_tok, lat_len):
    t = x.shape[0]
    midx = functools.partial(_mod_index, tm=TM, n_ctx_tok=n_ctx_tok, lat_len=lat_len)
    wnames = ["norm1_g", "norm2_g", "w_gate", "gate_b", "w_branch", "w_out", "lnx_g", "lnx_b", "ind"]
    wts = [lw[n] for n in wnames]
    tok = lambda wd: pl.BlockSpec((TM, wd), lambda i: (i, 0))
    ya, yd, gbg, h, g, bon = tok_in
    yspec = lambda d: pl.BlockSpec((1, 1, W_MIX, TM), lambda i: (d, i, 0, 0))
    return pl.pallas_call(
        _merge_body,
        grid=(t // TM,),
        in_specs=([tok(D_MODEL), pl.BlockSpec((1, N_MOD, D_MODEL), lambda i: (midx(i), 0, 0))]
                  + [_const_spec(w.shape) for w in wts]
                  + [tok(W_MIX), tok(W_MIX), tok(W_MIX), tok(2 * W_MIX), yspec(0), yspec(1),
                     tok(W_MIX), tok(W_MIX)]),
        out_specs=[tok(D_MODEL), tok(D_MODEL)],
        out_shape=[jax.ShapeDtypeStruct((t, D_MODEL), F32), jax.ShapeDtypeStruct((t, D_MODEL), BF16)],
        compiler_params=_cparams(("parallel",), VMEM_LIMIT),
        name="merge",
    )(x, mod, *wts, ya, yd, gbg, h, y_all, y_all, g, bon)


_CAND_VALID = (8, 8, 8, 5, 4, 3, 2, 2, 2, 8)


def _route_body(h2_ref, wqt_ref, keys_ref, i_ref, j_ref, g_ref, q_scr, i_scr, j_scr, g_scr):
    q_scr[...] = lax.dot_general(wqt_ref[...], h2_ref[...], _NT, preferred_element_type=F32)
    kio = lax.broadcasted_iota(I32, (N_KEYS, LANES), 0)
    sub = lax.broadcasted_iota(I32, (SUBLANES, LANES), 0)
    kid = lax.broadcasted_iota(I32, (PEER_TOPK, LANES), 0)
    n_cand = len(_CAND_VALID) * SUBLANES
    rid = lax.broadcasted_iota(I32, (n_cand, LANES), 0)
    neg = -jnp.inf

    def bc(x, r):
        return jnp.broadcast_to(x[r:r + 1, :], (SUBLANES, LANES))

    def head(h, carry):
        tops = []
        for p in range(2):
            row = pl.multiple_of(h * (2 * N_KEYS) + p * N_KEYS, N_KEYS)
            q = q_scr[pl.ds(row, N_KEYS), :].astype(BF16)
            s = jnp.dot(keys_ref[p], q, preferred_element_type=F32)
            vals = jnp.zeros((PEER_TOPK, LANES), F32)
            idxs = jnp.zeros((PEER_TOPK, LANES), I32)
            for r in range(PEER_TOPK):
                m = jnp.max(s, axis=0, keepdims=True)
                cand = jnp.where(s == m, kio, N_KEYS)
                ix = jnp.min(cand, axis=0, keepdims=True)
                s = jnp.where(cand == ix, neg, s)
                vals = jnp.where(kid == r, m, vals)
                idxs = jnp.where(kid == r, ix, idxs)
            tops.append((vals, idxs))
        (a0, i0), (a1, i1) = tops
        lo, hi = slice(0, SUBLANES), slice(SUBLANES, 2 * SUBLANES)
        slabs = [bc(a0, 0) + a1[lo], bc(a0, 0) + a1[hi]]
        ci = [bc(i0, 0), bc(i0, 0)]
        cj = [i1[lo], i1[hi]]
        for r in range(1, SUBLANES):
            slabs.append(bc(a0, r) + a1[lo])
            ci.append(bc(i0, r))
            cj.append(i1[lo])
        slabs.append(a0[hi] + bc(a1, 0))
        ci.append(i0[hi])
        cj.append(bc(i1, 0))
        slabs = [jnp.where(sub < nv, sl, neg) for sl, nv in zip(slabs, _CAND_VALID)]
        c = jnp.concatenate(slabs, axis=0)
        ci = jnp.concatenate(ci, axis=0)
        cj = jnp.concatenate(cj, axis=0)
        vals = jnp.zeros((PEER_TOPK, LANES), F32)
        isel = jnp.zeros((PEER_TOPK, LANES), I32)
        jsel = jnp.zeros((PEER_TOPK, LANES), I32)
        for r in range(PEER_TOPK):
            m = jnp.max(c, axis=0, keepdims=True)
            pos = jnp.min(jnp.where(c == m, rid, n_cand), axis=0, keepdims=True)
            sel = rid == pos
            isel = jnp.where(kid == r, jnp.sum(jnp.where(sel, ci, 0), axis=0, keepdims=True), isel)
            jsel = jnp.where(kid == r, jnp.sum(jnp.where(sel, cj, 0), axis=0, keepdims=True), jsel)
            vals = jnp.where(kid == r, m, vals)
            c = jnp.where(sel, neg, c)
        e = jnp.exp(vals - vals[0:1, :])
        out_row = pl.multiple_of(h * PEER_TOPK, PEER_TOPK)
        g_scr[pl.ds(out_row, PEER_TOPK), :] = e / jnp.sum(e, axis=0, keepdims=True)
        i_scr[pl.ds(out_row, PEER_TOPK), :] = isel
        j_scr[pl.ds(out_row, PEER_TOPK), :] = jsel
        return carry

    lax.fori_loop(0, PEER_HEADS, head, 0)
    i_ref[...] = i_scr[...].T
    j_ref[...] = j_scr[...].T
    g_ref[...] = g_scr[...].T


def _route(h2, wqt, keys):
    t = h2.shape[0]
    nsel = PEER_HEADS * PEER_TOPK
    tok = pl.BlockSpec((LANES, nsel), lambda i: (i, 0))
    return pl.pallas_call(
        _route_body,
        grid=(t // LANES,),
        in_specs=[pl.BlockSpec((LANES, D_MODEL), lambda i: (i, 0)),
                  _const_spec(wqt.shape), _const_spec(keys.shape)],
        out_specs=[tok, tok, tok],
        out_shape=[jax.ShapeDtypeStruct((t, nsel), I32), jax.ShapeDtypeStruct((t, nsel), I32),
                   jax.ShapeDtypeStruct((t, nsel), F32)],
        scratch_shapes=[pltpu.VMEM((wqt.shape[0], LANES), F32), pltpu.VMEM((nsel, LANES), I32),
                        pltpu.VMEM((nsel, LANES), I32), pltpu.VMEM((nsel, LANES), F32)],
        compiler_params=_cparams(("parallel",), VMEM_LIMIT),
        name="peer_route",
    )(h2, wqt, keys)


def _dense_body(h2_ref, i_ref, j_ref, g_ref, u_ref, v_ref, x1_ref, mod_ref, fng_ref, o_ref,
                gs_scr, acc_scr, *, rows, pitch, n_e, final):
    half = pl.program_id(1)
    e = pl.program_id(2)
    tm = h2_ref.shape[0]
    nsel = i_ref.shape[1]

    @pl.when(e == 0)
    def _build():
        sio = lax.broadcasted_iota(I32, (rows, nsel), 0) + half * rows
        jio = lax.broadcasted_iota(I32, (N_KEYS, nsel), 0)

        def tok(t, c):
            irow = i_ref[pl.ds(t, 1), :]
            jrow = j_ref[pl.ds(t, 1), :]
            grow = g_ref[pl.ds(t, 1), :]
            at = jnp.where(sio == irow, grow, 0.0).astype(BF16)
            bt = jnp.where(jio == jrow, 1.0, 0.0).astype(BF16)
            gs_scr[pl.ds(pl.multiple_of(t * pitch, SUBLANES), rows), :] = lax.dot_general(
                at, bt, _NT, preferred_element_type=F32)
            return c

        lax.fori_loop(0, tm, tok, 0, unroll=16)

    @pl.when(jnp.logical_and(half == 0, e == 0))
    def _zero():
        acc_scr[...] = jnp.zeros_like(acc_scr)

    hmat = lax.dot_general(h2_ref[...], u_ref[...], _NT, preferred_element_type=F32)
    per_step = u_ref.shape[0] // N_KEYS
    gm = jnp.concatenate(
        [gs_scr[pl.ds(e * per_step + ii, tm, stride=pitch), :] for ii in range(per_step)], axis=1)
    act = (jax.nn.gelu(hmat) * gm).astype(BF16)
    acc_scr[...] += jnp.dot(act, v_ref[...], preferred_element_type=F32)

    @pl.when(jnp.logical_and(half == pl.num_programs(1) - 1, e == n_e - 1))
    def _out():
        x2 = x1_ref[...] + mod_ref[0][5:6, :] * acc_scr[...]
        o_ref[...] = _rms(x2, fng_ref[...]) if final else x2


def _peer_dense(h2, isel, jsel, gsel, u, v, x1, mod, fng, n_ctx_tok, lat_len, final):
    t = h2.shape[0]
    nsel = isel.shape[1]
    rows = N_KEYS // PEER_SPLIT
    pitch = rows + SUBLANES
    n_e = (rows * N_KEYS) // PEER_EB
    midx = functools.partial(_mod_index, tm=TM_PEER, n_ctx_tok=n_ctx_tok, lat_len=lat_len)
    tok = lambda wd: pl.BlockSpec((TM_PEER, wd), lambda m, s, e: (m, 0))
    espec = pl.BlockSpec((PEER_EB, D_MODEL), lambda m, s, e: (s * n_e + e, 0))
    return pl.pallas_call(
        functools.partial(_dense_body, rows=rows, pitch=pitch, n_e=n_e, final=final),
        grid=(t // TM_PEER, PEER_SPLIT, n_e),
        in_specs=[tok(D_MODEL), tok(nsel), tok(nsel), tok(nsel), espec, espec, tok(D_MODEL),
                  pl.BlockSpec((1, N_MOD, D_MODEL), lambda m, s, e: (midx(m), 0, 0)),
                  _const_spec((1, D_MODEL))],
        out_specs=tok(D_MODEL),
        out_shape=jax.ShapeDtypeStruct((t, D_MODEL), F32),
        scratch_shapes=[pltpu.VMEM((TM_PEER * pitch, N_KEYS), F32), pltpu.VMEM((TM_PEER, D_MODEL), F32)],
        compiler_params=_cparams(("parallel", "arbitrary", "arbitrary"), VMEM_LIMIT),
        name="peer_dense",
    )(h2, isel, jsel, gsel, u, v, x1, mod, fng)


def _wkv_state_in(s, vs):
    n = s.shape[0]
    vl = HS_WKV // vs
    s = s.reshape(n, N_DIR, H_WKV, vs, vl, HS_WKV).transpose(1, 4, 5, 3, 0, 2)
    return s.reshape(N_DIR, vl, HS_WKV, vs * n * H_WKV)


def _wkv_state_out(s, n_sb, spb, vs):
    vl = HS_WKV // vs
    s = s.reshape(N_DIR, n_sb, vl, HS_WKV, vs, spb, H_WKV).transpose(1, 5, 0, 6, 4, 2, 3)
    return s.reshape(n_sb * spb, N_DIR, H_WKV, HS_WKV, HS_WKV)


def _layer_weights(i, prm):
    eye_h = jnp.eye(H_LRU, dtype=F32)
    eye_d = jnp.eye(N_DIR, dtype=F32)
    perm = _WKV_PERM

    def lru_bd(wt):
        return jnp.einsum("dhij,hg->hidgj", wt, eye_h).reshape(W_MIX, N_DIR * W_MIX)

    def lora_bd(wt):
        r = wt.shape[1]
        return jnp.einsum("drc,de->drec", wt, eye_d).reshape(N_DIR * r, N_DIR * W_MIX)

    w_in = prm["w_in"][i]
    pad = jnp.zeros((D_MODEL, Z_COLS - 5504), F32)
    rkv = [w_in[:, 2560 + j * W_MIX:2560 + (j + 1) * W_MIX][:, perm] for j in range(3)]
    w_in_perm = jnp.concatenate(
        [w_in[:, 0:1536]] + rkv + [w_in[:, 1536:2560], w_in[:, 4480:5504], w_in[:, 4096:4480], pad],
        axis=1).astype(BF16)
    row = lambda x: x.reshape(1, -1).astype(F32)
    head_of = np.arange(W_MIX) % H_WKV
    w_branch = prm["w_branch"][i]
    w_branch = jnp.concatenate([w_branch[0], w_branch[1], w_branch[2][perm, :], w_branch[3]], axis=0)
    return {
        "w_in": w_in_perm,
        "w_gate": w_in[:, 5504:].astype(BF16),
        "norm1_g": row(prm["norm1_g"][i]),
        "norm2_g": row(prm["norm2_g"][i]),
        "conv_a_w": prm["conv_a_w"][i],
        "conv_b_w": prm["conv_b_w"][i],
        "conv_b_b": row(prm["conv_b_b"][i]),
        "lru_w": jnp.concatenate([lru_bd(prm["lru_wa"][i]), lru_bd(prm["lru_wx"][i])], axis=1).astype(BF16),
        "lru_b": jnp.concatenate([row(prm["lru_ba"][i]), row(prm["lru_bx"][i])], axis=1),
        "lru_lam": row(prm["lru_lambda"][i]),
        "w0": row(prm["rwkv_w0"][i][:, perm]),
        "w2": lora_bd(prm["rwkv_w2"][i][:, :, perm]).astype(BF16),
        "a0": row(prm["rwkv_a0"][i][:, perm]),
        "a2": lora_bd(prm["rwkv_a2"][i][:, :, perm]).astype(BF16),
        "g2": prm["rwkv_g2"][i][:, perm].astype(BF16),
        "kk": row(prm["rwkv_kk"][i][perm]),
        "ka": row(prm["rwkv_ka"][i][perm]),
        "rk": row(prm["rwkv_rk"][i].reshape(W_MIX)[perm]),
        "lnx_g": row(prm["lnx_g"][i][perm]),
        "lnx_b": row(prm["lnx_b"][i][perm]),
        "sg_ln_g": row(prm["sg_ln_g"][i]),
        "sg_ln_b": row(prm["sg_ln_b"][i]),
        "sg_ws": prm["sg_ws"][i].astype(BF16),
        "sg_bst": prm["sg_bs"][i].T,
        "gate_b": row(prm["gate_b"][i]),
        "w_branch": w_branch.astype(BF16),
        "w_out": prm["w_out"][i].astype(BF16),
        "wq_t": prm["peer_wq"][i].T.astype(BF16),
        "keys": prm["peer_keys"][i].astype(BF16),
        "peer_u": prm["peer_u"][i].astype(BF16),
        "peer_v": prm["peer_v"][i].astype(BF16),
        "ind": jnp.asarray(head_of[:, None] == head_of[None, :], BF16),
    }


def kernel(x_prompt, x_sample, state_lru, state_wkv, c, c_ctx, norm1_g, norm2_g, w_mod, b_mod, w_in, conv_a_w, conv_b_w, conv_b_b, lru_wa, lru_ba, lru_wx, lru_bx, lru_lambda, rwkv_w0, rwkv_w2, rwkv_a0, rwkv_a2, rwkv_g2, rwkv_kk, rwkv_ka, rwkv_rk, lnx_g, lnx_b, sg_ln_g, sg_ln_b, sg_ws, sg_bs, gate_b, w_branch, w_out, peer_wq, peer_keys, peer_u, peer_v, final_norm_g):
    prm = dict(norm1_g=norm1_g, norm2_g=norm2_g, w_in=w_in, conv_a_w=conv_a_w, conv_b_w=conv_b_w,
               conv_b_b=conv_b_b, lru_wa=lru_wa, lru_ba=lru_ba, lru_wx=lru_wx, lru_bx=lru_bx,
               lru_lambda=lru_lambda, rwkv_w0=rwkv_w0, rwkv_w2=rwkv_w2, rwkv_a0=rwkv_a0, rwkv_a2=rwkv_a2,
               rwkv_g2=rwkv_g2, rwkv_kk=rwkv_kk, rwkv_ka=rwkv_ka, rwkv_rk=rwkv_rk, lnx_g=lnx_g,
               lnx_b=lnx_b, sg_ln_g=sg_ln_g, sg_ln_b=sg_ln_b, sg_ws=sg_ws, sg_bs=sg_bs, gate_b=gate_b,
               w_branch=w_branch, w_out=w_out, peer_wq=peer_wq, peer_keys=peer_keys, peer_u=peer_u,
               peer_v=peer_v)
    bc, lc, _ = x_prompt.shape
    bl, ll, _ = x_sample.shape
    depth = w_mod.shape[0]
    n_ctx_tok = bc * lc
    n_ctx_tiles = n_ctx_tok // TM
    lat_tiles = ll // TM
    ctx_spb = min(WKV_CTX_SPB, bc)
    lru_spb = min(LRU_CTX_SPB, bc)
    assert lc == TM and ll % TM_IN == 0 and n_ctx_tok % TM_IN == 0 and bl + 1 <= SUBLANES
    assert ll % GRID_W == 0 and TM % GRID_W == 0 and bc % ctx_spb == 0 and bc % lru_spb == 0
    assert LANES % (ctx_spb * H_WKV) == 0 and LANES % (bl * H_WKV) == 0

    cond = jnp.zeros((SUBLANES, D_MODEL), F32).at[0].set(c_ctx).at[1:1 + bl].set(c)
    mods = _modulation(cond, w_mod, b_mod).reshape(depth, SUBLANES, N_MOD, D_MODEL)
    fng = final_norm_g.reshape(1, D_MODEL)
    x = jnp.concatenate([x_prompt.reshape(n_ctx_tok, D_MODEL), x_sample.reshape(bl * ll, D_MODEL)], axis=0)
    ctx_vs = LANES // (ctx_spb * H_WKV)
    lat_vs = LANES // (bl * H_WKV)
    n_sb = bc // ctx_spb
    wkv_zero = jnp.zeros((N_DIR * n_sb, HS_WKV // ctx_vs, HS_WKV, LANES), F32)
    lru_zero = jnp.zeros((bc, N_DIR * W_MIX), F32)
    new_lru, new_wkv = [], []
    pnames = ["ya", "yd", "gbg", "la", "lu", "g", "bon", "rt", "vt", "kkt", "wt", "kt", "bt"]
    for i in range(depth):
        lw = _layer_weights(i, prm)
        mod = mods[i]
        z = _in_proj(x, mod, lw["norm1_g"], lw["w_in"], n_ctx_tok, ll)
        p = dict(zip(pnames, _prep(z, lw, n_ctx_tok, ll)))

        h_c, lru_s = _lru_scan(p["la"][:n_ctx_tok], p["lu"][:n_ctx_tok], lru_zero, lc, lru_spb)
        h_l, _ = _lru_scan(p["la"][n_ctx_tok:], p["lu"][n_ctx_tok:],
                           state_lru[:, i].astype(F32).reshape(bl, N_DIR * W_MIX), ll, 1)
        new_lru.append(lru_s.reshape(bc, N_DIR, W_MIX))

        wkv_in = [p[n] for n in ("rt", "wt", "kt", "kkt", "bt", "vt")]
        (y_c,), s_c = _wkv_scan(*wkv_in, wkv_zero, tile0=0, n_seq=bc, seq_tiles=1, spb=ctx_spb)
        y_l, _ = _wkv_scan(*wkv_in, _wkv_state_in(state_wkv[:, i].astype(F32), lat_vs),
                           tile0=n_ctx_tiles, n_seq=bl, seq_tiles=lat_tiles, spb=1)
        new_wkv.append(_wkv_state_out(s_c, n_sb, ctx_spb, ctx_vs))
        y_all = jnp.concatenate([y_c, *y_l], axis=1)

        tok_in = [p["ya"], p["yd"], p["gbg"], jnp.concatenate([h_c, h_l], axis=0), p["g"], p["bon"]]
        x1, h2 = _merge(x, mod, lw, tok_in, y_all, n_ctx_tok, ll)
        isel, jsel, gsel = _route(h2, lw["wq_t"], lw["keys"])
        x = _peer_dense(h2, isel, jsel, gsel, lw["peer_u"], lw["peer_v"], x1, mod, fng,
                        n_ctx_tok, ll, final=(i == depth - 1))
    y_prompt = x[:n_ctx_tok].reshape(bc, lc, D_MODEL)
    y_sample = x[n_ctx_tok:].reshape(bl, ll, D_MODEL)
    return (y_prompt, y_sample, jnp.stack(new_lru, axis=1), jnp.stack(new_wkv, axis=1))
```

```python
import functools

import numpy as np
import jax
import jax.numpy as jnp
from jax import lax
from jax.experimental import pallas as pl
from jax.experimental.pallas import tpu as pltpu

F32 = jnp.float32
BF16 = jnp.bfloat16
I32 = jnp.int32

D_MODEL = 1024
W_MIX = 512
N_DIR = 2
N_BRANCH = 4
H_WKV = 8
HS_WKV = 64
H_LRU = 8
HB_LRU = 64
LORA_W = 64
LORA_A = 64
LORA_G = 128
GRID_W = 64
CHUNK = 128
G_SG = 4
N_KEYS = 128
PEER_HEADS = 8
PEER_TOPK = 16
N_MOD = 6
EPS = 1e-6
LNX_EPS = 64e-5
LRU_C = 8.0

LANES = 128
SUBLANES = 8
TM = 256
TM_IN = 512
TN_IN = 1408
TM_PEER = 512
PEER_EB = 512
Z_COLS = 5632
WKV_TC = LANES
WKV_CTX_SPB = 8
LRU_CTX_SPB = 4
VMEM_LIMIT = 56 * 1024 * 1024

_NT = (((1,), (1,)), ((), ()))
_WKV_PERM = np.array([(n % H_WKV) * HS_WKV + n // H_WKV for n in range(W_MIX)])


def _cparams(sem, vmem=None):
    return pltpu.CompilerParams(dimension_semantics=sem, vmem_limit_bytes=vmem)


def _const_spec(shape):
    nd = len(shape)
    return pl.BlockSpec(shape, lambda *_: (0,) * nd)


def _softplus(x):
    return jnp.maximum(x, 0.0) + jnp.log1p(jnp.exp(-jnp.abs(x)))


def _rms(x, g):
    return x * lax.rsqrt(jnp.mean(x * x, axis=-1, keepdims=True) + EPS) * g


def _segsum(x, ind):
    hi = x.astype(BF16)
    lo = (x - hi.astype(F32)).astype(BF16)
    return (jnp.dot(hi, ind, preferred_element_type=F32)
            + jnp.dot(lo, ind, preferred_element_type=F32))


def _mod_index(i, tm, n_ctx_tok, lat_len):
    n_ctx_tiles = n_ctx_tok // tm
    tiles_per_seq = lat_len // tm
    return jnp.where(i < n_ctx_tiles, 0, 1 + lax.div(i - n_ctx_tiles, tiles_per_seq))


def _mod_body(s_ref, w_ref, b_ref, o_ref):
    s = s_ref[...]
    s = s * jax.nn.sigmoid(s)
    o_ref[0] = jnp.dot(s.astype(BF16), w_ref[0].astype(BF16), preferred_element_type=F32) + b_ref[0]


def _modulation(cond, w_mod, b_mod):
    depth = w_mod.shape[0]
    n = w_mod.shape[2]
    tn = 1536
    return pl.pallas_call(
        _mod_body,
        grid=(depth, n // tn),
        in_specs=[_const_spec((SUBLANES, D_MODEL)),
                  pl.BlockSpec((1, D_MODEL, tn), lambda l, j: (l, 0, j)),
                  pl.BlockSpec((1, 1, tn), lambda l, j: (l, 0, j))],
        out_specs=pl.BlockSpec((1, SUBLANES, tn), lambda l, j: (l, 0, j)),
        out_shape=jax.ShapeDtypeStruct((depth, SUBLANES, n), F32),
        compiler_params=_cparams(("parallel", "parallel"), VMEM_LIMIT),
        name="modulation",
    )(cond, w_mod, b_mod.reshape(depth, 1, n))


def _in_body(x_ref, mod_ref, g_ref, w_ref, o_ref, h_scr):
    @pl.when(pl.program_id(1) == 0)
    def _():
        m = mod_ref[0]
        y = _rms(x_ref[...], g_ref[...])
        h_scr[...] = (y * (1.0 + m[1:2, :]) + m[0:1, :]).astype(BF16)

    o_ref[...] = jnp.dot(h_scr[...], w_ref[...], preferred_element_type=F32)


def _in_proj(x, mod, g, w, n_ctx_tok, lat_len):
    t = x.shape[0]
    midx = functools.partial(_mod_index, tm=TM_IN, n_ctx_tok=n_ctx_tok, lat_len=lat_len)
    return pl.pallas_call(
        _in_body,
        grid=(t // TM_IN, Z_COLS // TN_IN),
        in_specs=[pl.BlockSpec((TM_IN, D_MODEL), lambda i, j: (i, 0)),
                  pl.BlockSpec((1, N_MOD, D_MODEL), lambda i, j: (midx(i), 0, 0)),
                  _const_spec((1, D_MODEL)),
                  pl.BlockSpec((D_MODEL, TN_IN), lambda i, j: (0, j))],
        out_specs=pl.BlockSpec((TM_IN, TN_IN), lambda i, j: (i, j)),
        out_shape=jax.ShapeDtypeStruct((t, Z_COLS), F32),
        scratch_shapes=[pltpu.VMEM((TM_IN, D_MODEL), BF16)],
        compiler_params=_cparams(("parallel", "arbitrary"), VMEM_LIMIT),
        name="in_proj",
    )(x, mod, g, w)


def _prep_body(za_ref, zc_ref, zb_ref, zd_ref, zl_ref, hp_ref, hn_ref,
               caw_ref, cbw_ref, cbb_ref, lruw_ref, lrub_ref, lam_ref,
               w0_ref, w2_ref, a0_ref, a2_ref, g2_ref, kkw_ref, ka_ref, rk_ref,
               lng_ref, lnb_ref, ws_ref, bst_ref, ind_ref,
               ya_ref, yd_ref, gbg_ref, la_ref, lu_ref, g_ref, bon_ref,
               rt_ref, vt_ref, kkt_ref, wt_ref, kt_ref, bt_ref,
               *, n_ctx_tiles, tiles_per_seq):
    i = pl.program_id(0)
    is_ctx = i < n_ctx_tiles
    t = lax.broadcasted_iota(I32, (TM, 1), 0)
    ind = ind_ref[...]

    pm = jnp.where(is_ctx, TM - 1, GRID_W - 1)
    pos = t & pm
    a_b = za_ref[:, 0:W_MIX]
    ac = za_ref[:, W_MIX:2 * W_MIX] * za_ref[:, 2 * W_MIX:3 * W_MIX]
    up = jnp.where(pos == 0, 0.0, pltpu.roll(ac, 1, 0))
    dn = jnp.where(pos == pm, 0.0, pltpu.roll(ac, TM - 1, 0))
    ya_ref[...] = a_b * (caw_ref[0:1, :] * up + caw_ref[1:2, :] * ac + caw_ref[2:3, :] * dn)

    seq_tile = lax.rem(jnp.maximum(i - n_ctx_tiles, 0), tiles_per_seq)
    first = jnp.logical_or(is_ctx, seq_tile == 0)
    last = jnp.logical_or(is_ctx, seq_tile == tiles_per_seq - 1)
    prev = jnp.where(first, 0.0, hp_ref[SUBLANES - 1:SUBLANES, :])
    nxt0 = jnp.where(last, 0.0, hn_ref[0:1, :])
    nxt1 = jnp.where(last, 0.0, hn_ref[1:2, :])
    bx = zb_ref[:, W_MIX:2 * W_MIX]
    m1 = jnp.where(t == 0, prev, pltpu.roll(bx, 1, 0))
    p1 = jnp.where(t == TM - 1, nxt0, pltpu.roll(bx, TM - 1, 0))
    p2 = jnp.where(t == TM - 2, nxt0, jnp.where(t == TM - 1, nxt1, pltpu.roll(bx, TM - 2, 0)))
    xb = (cbw_ref[0:1, :] * m1 + cbw_ref[1:2, :] * bx + cbw_ref[2:3, :] * p1
          + cbw_ref[3:4, :] * p2 + cbb_ref[...])
    gates = jnp.dot(xb.astype(BF16), lruw_ref[...], preferred_element_type=F32) + lrub_ref[...]
    rg = jax.nn.sigmoid(gates[:, 0:2 * W_MIX])
    ig = jax.nn.sigmoid(gates[:, 2 * W_MIX:4 * W_MIX])
    log_a = -LRU_C * rg * _softplus(-lam_ref[...])
    xb2 = jnp.concatenate([xb, xb], axis=1)
    la_ref[...] = jnp.exp(log_a)
    lu_ref[...] = jnp.sqrt(jnp.tanh(-log_a) * (jnp.exp(2.0 * log_a) + 1.0)) * (ig * xb2)
    gbg_ref[...] = jax.nn.gelu(zb_ref[:, 0:W_MIX])

    zr = zc_ref[:, 0:W_MIX]
    zk = zc_ref[:, W_MIX:2 * W_MIX]
    zv = zc_ref[:, 2 * W_MIX:3 * W_MIX]
    zwd = zl_ref[:, 0:2 * LORA_W]
    zad = zl_ref[:, 2 * LORA_W:2 * LORA_W + 2 * LORA_A]
    zgd = zl_ref[:, 2 * LORA_W + 2 * LORA_A:2 * LORA_W + 2 * LORA_A + LORA_G]
    wlin = w0_ref[...] + jnp.dot(jnp.tanh(zwd).astype(BF16), w2_ref[...], preferred_element_type=F32)
    wt_ref[0] = jnp.exp(-jnp.exp(-_softplus(-wlin) - 0.5)).T
    av = jax.nn.sigmoid(a0_ref[...] + jnp.dot(zad.astype(BF16), a2_ref[...], preferred_element_type=F32))
    g_ref[...] = jnp.dot(jax.nn.sigmoid(zgd).astype(BF16), g2_ref[...], preferred_element_type=F32)
    kkr = zk * kkw_ref[...]
    kkn = kkr / jnp.maximum(jnp.sqrt(_segsum(kkr * kkr, ind)), 1e-12)
    zk2 = jnp.concatenate([zk, zk], axis=1)
    ka2 = jnp.concatenate([ka_ref[...], ka_ref[...]], axis=1)
    kd = zk2 * (1.0 + (av - 1.0) * ka2)
    kt_ref[0] = kd.T
    bt_ref[0] = (jnp.concatenate([kkn, kkn], axis=1) * av).T
    rt_ref[0] = zr.T
    vt_ref[0] = zv.T
    kkt_ref[0] = kkn.T
    bon_ref[...] = _segsum(zr * (kd[:, 0:W_MIX] + kd[:, W_MIX:2 * W_MIX]) * rk_ref[...], ind) * zv

    zg = jax.nn.gelu(zd_ref[...])
    u = zg[:, 0:W_MIX]
    vv = zg[:, W_MIX:2 * W_MIX]
    vc = vv - jnp.mean(vv, axis=-1, keepdims=True)
    vn = vc * lax.rsqrt(jnp.mean(vc * vc, axis=-1, keepdims=True) + 1e-5) * lng_ref[...] + lnb_ref[...]
    for c in range(TM // CHUNK):
        rs = slice(c * CHUNK, (c + 1) * CHUNK)
        for gi in range(G_SG):
            cs = slice(gi * LANES, (gi + 1) * LANES)
            s = jnp.dot(ws_ref[gi], vn[rs, cs].astype(BF16), preferred_element_type=F32)
            yd_ref[rs, cs] = u[rs, cs] * (s + bst_ref[:, gi:gi + 1])


def _prep(z, lw, n_ctx_tok, lat_len):
    t = z.shape[0]
    n_tiles = t // TM
    n_ctx_tiles = n_ctx_tok // TM
    tiles_per_seq = lat_len // TM
    rows8 = TM // SUBLANES
    last_blk = t // SUBLANES - 1
    bx_blk = (3072 + W_MIX) // W_MIX
    z_specs = [
        pl.BlockSpec((TM, 1536), lambda i: (i, 0)),
        pl.BlockSpec((TM, 1536), lambda i: (i, 1)),
        pl.BlockSpec((TM, 1024), lambda i: (i, 3)),
        pl.BlockSpec((TM, 1024), lambda i: (i, 4)),
        pl.BlockSpec((TM, 512), lambda i: (i, 10)),
        pl.BlockSpec((SUBLANES, W_MIX), lambda i: (jnp.maximum(i * rows8 - 1, 0), bx_blk)),
        pl.BlockSpec((SUBLANES, W_MIX), lambda i: (jnp.minimum((i + 1) * rows8, last_blk), bx_blk)),
    ]
    wnames = ["conv_a_w", "conv_b_w", "conv_b_b", "lru_w", "lru_b", "lru_lam", "w0", "w2", "a0", "a2",
              "g2", "kk", "ka", "rk", "sg_ln_g", "sg_ln_b", "sg_ws", "sg_bst", "ind"]
    wts = [lw[n] for n in wnames]
    w_specs = [_const_spec(w.shape) for w in wts]
    widths = [W_MIX, W_MIX, W_MIX, 2 * W_MIX, 2 * W_MIX, W_MIX, W_MIX]
    t_rows = [W_MIX, W_MIX, W_MIX, 2 * W_MIX, 2 * W_MIX, 2 * W_MIX]
    out_specs = ([pl.BlockSpec((TM, wd), lambda i: (i, 0)) for wd in widths]
                 + [pl.BlockSpec((1, r, TM), lambda i: (i, 0, 0)) for r in t_rows])
    out_shape = ([jax.ShapeDtypeStruct((t, wd), F32) for wd in widths]
                 + [jax.ShapeDtypeStruct((n_tiles, r, TM), F32) for r in t_rows])
    return pl.pallas_call(
        functools.partial(_prep_body, n_ctx_tiles=n_ctx_tiles, tiles_per_seq=tiles_per_seq),
        grid=(n_tiles,),
        in_specs=z_specs + w_specs,
        out_specs=out_specs,
        out_shape=out_shape,
        compiler_params=_cparams(("parallel",), VMEM_LIMIT),
        name="branch_prep",
    )(z, z, z, z, z, z, z, *wts)


def _lru_body(a_ref, u_ref, h0_ref, h_ref, hf_ref, *, nseq, l):
    fw, bw = slice(0, W_MIX), slice(W_MIX, 2 * W_MIX)

    def step(s, carry):
        out = []
        for j in range(nseq):
            tf = j * l + s
            tb = j * l + (l - 1 - s)
            hf = a_ref[pl.ds(tf, 1), fw] * carry[2 * j] + u_ref[pl.ds(tf, 1), fw]
            hb = a_ref[pl.ds(tb, 1), bw] * carry[2 * j + 1] + u_ref[pl.ds(tb, 1), bw]
            h_ref[pl.ds(tf, 1), fw] = hf
            h_ref[pl.ds(tb, 1), bw] = hb
            out += [hf, hb]
        return tuple(out)

    init = []
    for j in range(nseq):
        init += [h0_ref[0, j:j + 1, fw], h0_ref[0, j:j + 1, bw]]
    fin = lax.fori_loop(0, l, step, tuple(init), unroll=2)
    for j in range(nseq):
        hf_ref[0, j:j + 1, fw] = fin[2 * j]
        hf_ref[0, j:j + 1, bw] = fin[2 * j + 1]


def _lru_scan(a, u, h0, l, nseq):
    n = a.shape[0] // l
    nb = n // nseq
    w = a.shape[1]
    tok = pl.BlockSpec((nseq * l, w), lambda i: (i, 0))
    st = pl.BlockSpec((1, nseq, w), lambda i: (i, 0, 0))
    h, hf = pl.pallas_call(
        functools.partial(_lru_body, nseq=nseq, l=l),
        grid=(nb,),
        in_specs=[tok, tok, st],
        out_specs=[tok, st],
        out_shape=[jax.ShapeDtypeStruct(a.shape, F32), jax.ShapeDtypeStruct((nb, nseq, w), F32)],
        compiler_params=_cparams(("parallel",), VMEM_LIMIT),
        name="lru_scan",
    )(a, u, h0.reshape(nb, nseq, w))
    return h, hf.reshape(n, w)


_SLOT_ORDER = (0, 4, 2, 6, 1, 5, 3, 7)


def _rowsum8(parts):
    sub = lax.broadcasted_iota(I32, (SUBLANES, LANES), 0)
    slots = [parts[i] for i in _SLOT_ORDER]
    roll = pltpu.roll
    lvl1 = [jnp.where(sub < 4, a + roll(a, 4, 0), b + roll(b, 4, 0))
            for a, b in zip(slots[0::2], slots[1::2])]
    lvl2 = [jnp.where((sub & 3) < 2, a + roll(a, 6, 0), roll(b + roll(b, 6, 0), 2, 0))
            for a, b in zip(lvl1[0::2], lvl1[1::2])]
    a, b = lvl2
    return jnp.where((sub & 1) == 0, a + roll(a, 7, 0), roll(b + roll(b, 7, 0), 1, 0))


def _fold8(x):
    return jnp.sum(x.reshape(HS_WKV // SUBLANES, SUBLANES, LANES), axis=0)


def _wkv_body(*refs, nsrc, spb, vs, n_sb, tc, kp, vp):
    vl_n = HS_WKV // vs
    n_in = 6 * nsrc
    k_srcs = [refs[o * nsrc:(o + 1) * nsrc] for o in range(5)]
    v_srcs = refs[5 * nsrc:n_in]
    s0_ref = refs[n_in]
    y_refs = refs[n_in + 1:n_in + 1 + nsrc]
    sf_ref = refs[n_in + 1 + nsrc]
    k_scr = refs[n_in + 2 + nsrc:n_in + 7 + nsrc]
    v_scr, y_scr, s_scr, sa_scr = refs[n_in + 7 + nsrc:n_in + 11 + nsrc]
    r_scr, w_scr, k_scr_, kk_scr, b_scr = k_scr
    backward = pl.program_id(0) // n_sb == 1
    seqs = [(s, j) for s in range(nsrc) for j in range(spb)]

    @pl.when(pl.program_id(1) == 0)
    def _():
        s_scr[...] = s0_ref[0]

    def build_k(c, carry):
        row = pl.multiple_of(c * H_WKV, H_WKV)
        for o in range(5):
            slab = [k_srcs[o][s][j, pl.ds(row, H_WKV), :] for s, j in seqs]
            k_scr[o][pl.ds(c, tc, stride=kp), :] = jnp.concatenate(slab * vs, axis=0).T
        return carry

    lax.fori_loop(0, HS_WKV, build_k, 0)

    def build_v(vl, carry):
        slab = []
        for vsi in range(vs):
            row = pl.multiple_of((vsi * vl_n + vl) * H_WKV, H_WKV)
            slab += [v_srcs[s][j, pl.ds(row, H_WKV), :] for s, j in seqs]
        v_scr[pl.ds(vl, tc, stride=vp), :] = jnp.concatenate(slab, axis=0).T
        return carry

    lax.fori_loop(0, vl_n, build_v, 0)

    def step(s, carry):
        t = jnp.where(backward, tc - 1 - s, s)
        krow = pl.multiple_of(t * kp, SUBLANES)
        vrow = pl.multiple_of(t * vp, SUBLANES)
        kslab = pl.ds(krow, HS_WKV)
        for g in range(vl_n // SUBLANES):
            parts = [_fold8(s_scr[g * SUBLANES + i] * kk_scr[kslab, :]) for i in range(SUBLANES)]
            sa_scr[g * SUBLANES:(g + 1) * SUBLANES, :] = _rowsum8(parts)
        for g in range(vl_n // SUBLANES):
            parts = []
            for i in range(SUBLANES):
                vl = g * SUBLANES + i
                sa = sa_scr[vl:vl + 1, :]
                vv = v_scr[pl.ds(vrow + vl, 1), :]
                sn = s_scr[vl] * w_scr[kslab, :] - sa * b_scr[kslab, :] + vv * k_scr_[kslab, :]
                s_scr[vl] = sn
                parts.append(_fold8(sn * r_scr[kslab, :]))
            y_scr[pl.ds(pl.multiple_of(vrow + g * SUBLANES, SUBLANES), SUBLANES), :] = _rowsum8(parts)
        return carry

    lax.fori_loop(0, tc, step, 0)

    def emit_y(vl, carry):
        yt = y_scr[pl.ds(vl, tc, stride=vp), :].T
        for vsi in range(vs):
            row = pl.multiple_of((vsi * vl_n + vl) * H_WKV, H_WKV)
            for n, (s, j) in enumerate(seqs):
                lane0 = (vsi * len(seqs) + n) * H_WKV
                y_refs[s][0, j, pl.ds(row, H_WKV), :] = yt[lane0:lane0 + H_WKV, :]
        return carry

    lax.fori_loop(0, vl_n, emit_y, 0)
    sf_ref[0] = s_scr[...]


def _wkv_scan(rt, wt, kt, kkt, bt, vt, s0, *, tile0, n_seq, seq_tiles, spb):
    tc = WKV_TC
    if spb > 1:
        assert seq_tiles == 1 and n_seq % spb == 0 and tile0 % spb == 0
        nsrc, n_sb = 1, n_seq // spb
    else:
        nsrc, n_sb = n_seq, 1
    inst = nsrc * spb * H_WKV
    vs = LANES // inst
    vl_n = HS_WKV // vs
    assert vl_n % SUBLANES == 0, "value rows are processed eight at a time"
    cpt = TM // tc
    n_chunks = seq_tiles * cpt
    kp = HS_WKV + SUBLANES
    vp = vl_n + SUBLANES if ((vl_n + SUBLANES) // SUBLANES) % 2 else vl_n + 2 * SUBLANES

    def chunk(g, i):
        return jnp.where(g // n_sb == 1, n_chunks - 1 - i, i)

    def in_map(g, i, *, src, per_dir):
        ce = chunk(g, i)
        rb = (g // n_sb) if per_dir else 0
        if spb > 1:
            return (tile0 // spb + g % n_sb, rb, ce)
        return (tile0 + src * seq_tiles + ce // cpt, rb, ce % cpt)

    def out_map(g, i, *, src):
        ce = chunk(g, i)
        if spb > 1:
            return (g // n_sb, g % n_sb, 0, ce)
        return (g // n_sb, ce // cpt, 0, ce % cpt)

    in_specs, operands = [], []
    for arr, per_dir in ((rt, False), (wt, True), (kt, True), (kkt, False), (bt, True), (vt, False)):
        for src in range(nsrc):
            in_specs.append(pl.BlockSpec((spb, W_MIX, tc), functools.partial(in_map, src=src, per_dir=per_dir),
                                         pipeline_mode=pl.Buffered(1)))
            operands.append(arr)
    sspec = pl.BlockSpec((1, vl_n, HS_WKV, LANES), lambda g, i: (g, 0, 0, 0))
    in_specs.append(sspec)
    n_out_tiles = n_seq * seq_tiles // nsrc
    out_specs = [pl.BlockSpec((1, spb, W_MIX, tc), functools.partial(out_map, src=src)) for src in range(nsrc)]
    out_shape = [jax.ShapeDtypeStruct((N_DIR, n_out_tiles, W_MIX, TM), F32) for _ in range(nsrc)]
    res = pl.pallas_call(
        functools.partial(_wkv_body, nsrc=nsrc, spb=spb, vs=vs, n_sb=n_sb, tc=tc, kp=kp, vp=vp),
        grid=(N_DIR * n_sb, n_chunks),
        in_specs=in_specs,
        out_specs=out_specs + [sspec],
        out_shape=out_shape + [jax.ShapeDtypeStruct(s0.shape, F32)],
        scratch_shapes=([pltpu.VMEM((tc * kp, LANES), F32)] * 5
                        + [pltpu.VMEM((tc * vp, LANES), F32)] * 2
                        + [pltpu.VMEM((vl_n, HS_WKV, LANES), F32), pltpu.VMEM((vl_n, LANES), F32)]),
        compiler_params=_cparams(("parallel", "arbitrary"), VMEM_LIMIT),
        name="wkv_scan",
    )(*operands, s0)
    return res[:nsrc], res[nsrc]


def _merge_body(x_ref, mod_ref, n1g_ref, n2g_ref, wg_ref, gb_ref, wbr_ref, wo_ref,
                lnxg_ref, lnxb_ref, ind_ref,
                ya_ref, yd_ref, gbg_ref, h_ref, yf_ref, yb_ref, g_ref, bon_ref,
                x1_ref, h2_ref):
    x = x_ref[...]
    m = mod_ref[0]
    ind = ind_ref[...]
    h = (_rms(x, n1g_ref[...]) * (1.0 + m[1:2, :]) + m[0:1, :]).astype(BF16)
    y_b = gbg_ref[...] * (h_ref[:, 0:W_MIX] + h_ref[:, W_MIX:2 * W_MIX])
    y = (yf_ref[0, 0] + yb_ref[0, 0]).T
    yc = y - _segsum(y, ind) * (1.0 / HS_WKV)
    var = _segsum(yc * yc, ind) * (1.0 / HS_WKV)
    y_c = (yc * lax.rsqrt(var + LNX_EPS) * lnxg_ref[...] + lnxb_ref[...] + bon_ref[...]) * g_ref[...]
    merged = None
    for n, yn in enumerate((ya_ref[...], y_b, y_c, yd_ref[...])):
        cs = slice(n * D_MODEL, (n + 1) * D_MODEL)
        gate = jax.nn.sigmoid(jnp.dot(h, wg_ref[:, cs], preferred_element_type=F32) + gb_ref[:, cs])
        br = jnp.dot(yn.astype(BF16), wbr_ref[n * W_MIX:(n + 1) * W_MIX, :], preferred_element_type=F32)
        merged = gate * br if merged is None else merged + gate * br
    mo = jnp.dot(merged.astype(BF16), wo_ref[...], preferred_element_type=F32)
    x1 = x + m[2:3, :] * mo
    x1_ref[...] = x1
    h2_ref[...] = (_rms(x1, n2g_ref[...]) * (1.0 + m[4:5, :]) + m[3:4, :]).astype(BF16)


def _merge(x, mod, lw, tok_in, y_all, n_ctx_tok, lat_len):
    t = x.shape[0]
    midx = functools.partial(_mod_index, tm=TM, n_ctx_tok=n_ctx_tok, lat_len=lat_len)
    wnames = ["norm1_g", "norm2_g", "w_gate", "gate_b", "w_branch", "w_out", "lnx_g", "lnx_b", "ind"]
    wts = [lw[n] for n in wnames]
    tok = lambda wd: pl.BlockSpec((TM, wd), lambda i: (i, 0))
    ya, yd, gbg, h, g, bon = tok_in
    yspec = lambda d: pl.BlockSpec((1, 1, W_MIX, TM), lambda i: (d, i, 0, 0))
    return pl.pallas_call(
        _merge_body,
        grid=(t // TM,),
        in_specs=([tok(D_MODEL), pl.BlockSpec((1, N_MOD, D_MODEL), lambda i: (midx(i), 0, 0))]
                  + [_const_spec(w.shape) for w in wts]
                  + [tok(W_MIX), tok(W_MIX), tok(W_MIX), tok(2 * W_MIX), yspec(0), yspec(1),
                     tok(W_MIX), tok(W_MIX)]),
        out_specs=[tok(D_MODEL), tok(D_MODEL)],
        out_shape=[jax.ShapeDtypeStruct((t, D_MODEL), F32), jax.ShapeDtypeStruct((t, D_MODEL), BF16)],
        compiler_params=_cparams(("parallel",), VMEM_LIMIT),
        name="merge",
    )(x, mod, *wts, ya, yd, gbg, h, y_all, y_all, g, bon)


_CAND_VALID = (8, 8, 8, 5, 4, 3, 2, 2, 2, 8)


def _route_body(h2_ref, wqt_ref, keys_ref, e_ref, g_ref, q_scr, e_scr, g_scr):
    q_scr[...] = lax.dot_general(wqt_ref[...], h2_ref[...], _NT, preferred_element_type=F32)
    kio = lax.broadcasted_iota(I32, (N_KEYS, LANES), 0)
    sub = lax.broadcasted_iota(I32, (SUBLANES, LANES), 0)
    kid = lax.broadcasted_iota(I32, (PEER_TOPK, LANES), 0)
    neg = -jnp.inf

    def bc(x, r):
        return jnp.broadcast_to(x[r:r + 1, :], (SUBLANES, LANES))

    def head(h, carry):
        tops = []
        for p in range(2):
            row = pl.multiple_of(h * (2 * N_KEYS) + p * N_KEYS, N_KEYS)
            q = q_scr[pl.ds(row, N_KEYS), :].astype(BF16)
            s = jnp.dot(keys_ref[p], q, preferred_element_type=F32)
            vals = jnp.zeros((PEER_TOPK, LANES), F32)
            idxs = jnp.zeros((PEER_TOPK, LANES), I32)
            for r in range(PEER_TOPK):
                m = jnp.max(s, axis=0, keepdims=True)
                cand = jnp.where(s == m, kio, N_KEYS)
                ix = jnp.min(cand, axis=0, keepdims=True)
                s = jnp.where(cand == ix, neg, s)
                vals = jnp.where(kid == r, m, vals)
                idxs = jnp.where(kid == r, ix, idxs)
            tops.append((vals, idxs))
        (a0, i0), (a1, i1) = tops
        lo, hi = slice(0, SUBLANES), slice(SUBLANES, 2 * SUBLANES)
        slabs = [bc(a0, 0) + a1[lo], bc(a0, 0) + a1[hi]]
        ci = [bc(i0, 0), bc(i0, 0)]
        cj = [i1[lo], i1[hi]]
        for r in range(1, SUBLANES):
            slabs.append(bc(a0, r) + a1[lo])
            ci.append(bc(i0, r))
            cj.append(i1[lo])
        slabs.append(a0[hi] + bc(a1, 0))
        ci.append(i0[hi])
        cj.append(bc(i1, 0))
        slabs = [jnp.where(sub < nv, sl, neg) for sl, nv in zip(slabs, _CAND_VALID)]
        ids = [a * N_KEYS + b for a, b in zip(ci, cj)]
        vals = jnp.zeros((PEER_TOPK, LANES), F32)
        esel = jnp.zeros((PEER_TOPK, LANES), I32)
        for r in range(PEER_TOPK):
            level = list(zip(slabs, ids))
            while len(level) > 1:
                nxt = []
                for (va, ea), (vb, eb) in zip(level[0::2], level[1::2]):
                    take = vb > va
                    nxt.append((jnp.where(take, vb, va), jnp.where(take, eb, ea)))
                if len(level) % 2:
                    nxt.append(level[-1])
                level = nxt
            v8, e8 = level[0]
            for sh in (4, 2, 1):
                vr, er = pltpu.roll(v8, sh, 0), pltpu.roll(e8, sh, 0)
                take = vr > v8
                v8, e8 = jnp.where(take, vr, v8), jnp.where(take, er, e8)
            m, ex = v8[0:1, :], e8[0:1, :]
            slabs = [jnp.where(eid == ex, neg, sl) for sl, eid in zip(slabs, ids)]
            vals = jnp.where(kid == r, m, vals)
            esel = jnp.where(kid == r, ex, esel)
        e = jnp.exp(vals - vals[0:1, :])
        out_row = pl.multiple_of(h * PEER_TOPK, PEER_TOPK)
        g_scr[pl.ds(out_row, PEER_TOPK), :] = e / jnp.sum(e, axis=0, keepdims=True)
        e_scr[pl.ds(out_row, PEER_TOPK), :] = esel
        return carry

    lax.fori_loop(0, PEER_HEADS, head, 0)
    e_ref[...] = e_scr[...].T
    g_ref[...] = g_scr[...].T


def _route(h2, wqt, keys):
    t = h2.shape[0]
    nsel = PEER_HEADS * PEER_TOPK
    tok = pl.BlockSpec((LANES, nsel), lambda i: (i, 0))
    return pl.pallas_call(
        _route_body,
        grid=(t // LANES,),
        in_specs=[pl.BlockSpec((LANES, D_MODEL), lambda i: (i, 0)),
                  _const_spec(wqt.shape), _const_spec(keys.shape)],
        out_specs=[tok, tok],
        out_shape=[jax.ShapeDtypeStruct((t, nsel), I32), jax.ShapeDtypeStruct((t, nsel), F32)],
        scratch_shapes=[pltpu.VMEM((wqt.shape[0], LANES), F32), pltpu.VMEM((nsel, LANES), I32),
                        pltpu.VMEM((nsel, LANES), F32)],
        compiler_params=_cparams(("parallel",), VMEM_LIMIT),
        name="peer_route",
    )(h2, wqt, keys)


def _dense_body(h2_ref, e_ref, g_ref, u_ref, v_ref, x1_ref, mod_ref, fng_ref, o_ref,
                gs_scr, acc_scr, *, rows, pitch, final):
    e = pl.program_id(1)
    tm = h2_ref.shape[0]
    nsel = e_ref.shape[1]

    @pl.when(e == 0)
    def _build():
        kio = lax.broadcasted_iota(I32, (N_KEYS, nsel), 0)

        def tok(t, c):
            erow = e_ref[pl.ds(t, 1), :]
            grow = g_ref[pl.ds(t, 1), :]
            at = jnp.where(kio == (erow >> 7), grow, 0.0).astype(BF16)
            bt = jnp.where(kio == (erow & (N_KEYS - 1)), 1.0, 0.0).astype(BF16)
            gt = lax.dot_general(at, bt, _NT, preferred_element_type=F32)
            hi = pltpu.bitcast(gt[0:rows, :].astype(BF16).astype(F32), jnp.uint32)
            lo = pltpu.bitcast(gt[rows:2 * rows, :].astype(BF16).astype(F32), jnp.uint32)
            gs_scr[pl.ds(pl.multiple_of(t * pitch, SUBLANES), rows), :] = hi | (lo >> 16)
            return c

        lax.fori_loop(0, tm, tok, 0, unroll=16)
        acc_scr[...] = jnp.zeros_like(acc_scr)

    hmat = lax.dot_general(h2_ref[...], u_ref[...], _NT, preferred_element_type=F32)
    per_step = u_ref.shape[0] // N_KEYS
    steps_per_half = rows // per_step
    row0 = lax.rem(e, steps_per_half) * per_step
    shift = jnp.where(e < steps_per_half, 0, 16).astype(jnp.uint32)
    words = jnp.concatenate(
        [gs_scr[pl.ds(row0 + ii, tm, stride=pitch), :] for ii in range(per_step)], axis=1)
    gm = pltpu.bitcast((words << shift) & jnp.uint32(0xFFFF0000), F32)
    act = (jax.nn.gelu(hmat) * gm).astype(BF16)
    acc_scr[...] += jnp.dot(act, v_ref[...], preferred_element_type=F32)

    @pl.when(e == pl.num_programs(1) - 1)
    def _out():
        x2 = x1_ref[...] + mod_ref[0][5:6, :] * acc_scr[...]
        o_ref[...] = _rms(x2, fng_ref[...]) if final else x2


def _peer_dense(h2, esel, gsel, u, v, x1, mod, fng, n_ctx_tok, lat_len, final):
    t = h2.shape[0]
    nsel = esel.shape[1]
    rows = N_KEYS // 2
    pitch = rows + SUBLANES
    n_e = (N_KEYS * N_KEYS) // PEER_EB
    midx = functools.partial(_mod_index, tm=TM_PEER, n_ctx_tok=n_ctx_tok, lat_len=lat_len)
    tok = lambda wd: pl.BlockSpec((TM_PEER, wd), lambda m, e: (m, 0))
    espec = pl.BlockSpec((PEER_EB, D_MODEL), lambda m, e: (e, 0))
    return pl.pallas_call(
        functools.partial(_dense_body, rows=rows, pitch=pitch, final=final),
        grid=(t // TM_PEER, n_e),
        in_specs=[tok(D_MODEL), tok(nsel), tok(nsel), espec, espec, tok(D_MODEL),
                  pl.BlockSpec((1, N_MOD, D_MODEL), lambda m, e: (midx(m), 0, 0)),
                  _const_spec((1, D_MODEL))],
        out_specs=tok(D_MODEL),
        out_shape=jax.ShapeDtypeStruct((t, D_MODEL), F32),
        scratch_shapes=[pltpu.VMEM((TM_PEER * pitch, N_KEYS), jnp.uint32),
                        pltpu.VMEM((TM_PEER, D_MODEL), F32)],
        compiler_params=_cparams(("parallel", "arbitrary"), VMEM_LIMIT),
        name="peer_dense",
    )(h2, esel, gsel, u, v, x1, mod, fng)


def _wkv_state_in(s, vs):
    n = s.shape[0]
    vl = HS_WKV // vs
    s = s.reshape(n, N_DIR, H_WKV, vs, vl, HS_WKV).transpose(1, 4, 5, 3, 0, 2)
    return s.reshape(N_DIR, vl, HS_WKV, vs * n * H_WKV)


def _wkv_state_out(s, n_sb, spb, vs):
    vl = HS_WKV // vs
    s = s.reshape(N_DIR, n_sb, vl, HS_WKV, vs, spb, H_WKV).transpose(1, 5, 0, 6, 4, 2, 3)
    return s.reshape(n_sb * spb, N_DIR, H_WKV, HS_WKV, HS_WKV)


def _layer_weights(i, prm):
    eye_h = jnp.eye(H_LRU, dtype=F32)
    eye_d = jnp.eye(N_DIR, dtype=F32)
    perm = _WKV_PERM

    def lru_bd(wt):
        return jnp.einsum("dhij,hg->hidgj", wt, eye_h).reshape(W_MIX, N_DIR * W_MIX)

    def lora_bd(wt):
        r = wt.shape[1]
        return jnp.einsum("drc,de->drec", wt, eye_d).reshape(N_DIR * r, N_DIR * W_MIX)

    w_in = prm["w_in"][i]
    pad = jnp.zeros((D_MODEL, Z_COLS - 5504), F32)
    rkv = [w_in[:, 2560 + j * W_MIX:2560 + (j + 1) * W_MIX][:, perm] for j in range(3)]
    w_in_perm = jnp.concatenate(
        [w_in[:, 0:1536]] + rkv + [w_in[:, 1536:2560], w_in[:, 4480:5504], w_in[:, 4096:4480], pad],
        axis=1).astype(BF16)
    row = lambda x: x.reshape(1, -1).astype(F32)
    head_of = np.arange(W_MIX) % H_WKV
    w_branch = prm["w_branch"][i]
    w_branch = jnp.concatenate([w_branch[0], w_branch[1], w_branch[2][perm, :], w_branch[3]], axis=0)
    return {
        "w_in": w_in_perm,
        "w_gate": w_in[:, 5504:].astype(BF16),
        "norm1_g": row(prm["norm1_g"][i]),
        "norm2_g": row(prm["norm2_g"][i]),
        "conv_a_w": prm["conv_a_w"][i],
        "conv_b_w": prm["conv_b_w"][i],
        "conv_b_b": row(prm["conv_b_b"][i]),
        "lru_w": jnp.concatenate([lru_bd(prm["lru_wa"][i]), lru_bd(prm["lru_wx"][i])], axis=1).astype(BF16),
        "lru_b": jnp.concatenate([row(prm["lru_ba"][i]), row(prm["lru_bx"][i])], axis=1),
        "lru_lam": row(prm["lru_lambda"][i]),
        "w0": row(prm["rwkv_w0"][i][:, perm]),
        "w2": lora_bd(prm["rwkv_w2"][i][:, :, perm]).astype(BF16),
        "a0": row(prm["rwkv_a0"][i][:, perm]),
        "a2": lora_bd(prm["rwkv_a2"][i][:, :, perm]).astype(BF16),
        "g2": prm["rwkv_g2"][i][:, perm].astype(BF16),
        "kk": row(prm["rwkv_kk"][i][perm]),
        "ka": row(prm["rwkv_ka"][i][perm]),
        "rk": row(prm["rwkv_rk"][i].reshape(W_MIX)[perm]),
        "lnx_g": row(prm["lnx_g"][i][perm]),
        "lnx_b": row(prm["lnx_b"][i][perm]),
        "sg_ln_g": row(prm["sg_ln_g"][i]),
        "sg_ln_b": row(prm["sg_ln_b"][i]),
        "sg_ws": prm["sg_ws"][i].astype(BF16),
        "sg_bst": prm["sg_bs"][i].T,
        "gate_b": row(prm["gate_b"][i]),
        "w_branch": w_branch.astype(BF16),
        "w_out": prm["w_out"][i].astype(BF16),
        "wq_t": prm["peer_wq"][i].T.astype(BF16),
        "keys": prm["peer_keys"][i].astype(BF16),
        "peer_u": prm["peer_u"][i].astype(BF16),
        "peer_v": prm["peer_v"][i].astype(BF16),
        "ind": jnp.asarray(head_of[:, None] == head_of[None, :], BF16),
    }


def kernel(x_prompt, x_sample, state_lru, state_wkv, c, c_ctx, norm1_g, norm2_g, w_mod, b_mod, w_in, conv_a_w, conv_b_w, conv_b_b, lru_wa, lru_ba, lru_wx, lru_bx, lru_lambda, rwkv_w0, rwkv_w2, rwkv_a0, rwkv_a2, rwkv_g2, rwkv_kk, rwkv_ka, rwkv_rk, lnx_g, lnx_b, sg_ln_g, sg_ln_b, sg_ws, sg_bs, gate_b, w_branch, w_out, peer_wq, peer_keys, peer_u, peer_v, final_norm_g):
    prm = dict(norm1_g=norm1_g, norm2_g=norm2_g, w_in=w_in, conv_a_w=conv_a_w, conv_b_w=conv_b_w,
               conv_b_b=conv_b_b, lru_wa=lru_wa, lru_ba=lru_ba, lru_wx=lru_wx, lru_bx=lru_bx,
               lru_lambda=lru_lambda, rwkv_w0=rwkv_w0, rwkv_w2=rwkv_w2, rwkv_a0=rwkv_a0, rwkv_a2=rwkv_a2,
               rwkv_g2=rwkv_g2, rwkv_kk=rwkv_kk, rwkv_ka=rwkv_ka, rwkv_rk=rwkv_rk, lnx_g=lnx_g,
               lnx_b=lnx_b, sg_ln_g=sg_ln_g, sg_ln_b=sg_ln_b, sg_ws=sg_ws, sg_bs=sg_bs, gate_b=gate_b,
               w_branch=w_branch, w_out=w_out, peer_wq=peer_wq, peer_keys=peer_keys, peer_u=peer_u,
               peer_v=peer_v)
    bc, lc, _ = x_prompt.shape
    bl, ll, _ = x_sample.shape
    depth = w_mod.shape[0]
    n_ctx_tok = bc * lc
    n_ctx_tiles = n_ctx_tok // TM
    lat_tiles = ll // TM
    ctx_spb = min(WKV_CTX_SPB, bc)
    lru_spb = min(LRU_CTX_SPB, bc)
    assert lc == TM and ll % TM_IN == 0 and n_ctx_tok % TM_IN == 0 and bl + 1 <= SUBLANES
    assert ll % GRID_W == 0 and TM % GRID_W == 0 and bc % ctx_spb == 0 and bc % lru_spb == 0
    assert LANES % (ctx_spb * H_WKV) == 0 and LANES % (bl * H_WKV) == 0

    cond = jnp.zeros((SUBLANES, D_MODEL), F32).at[0].set(c_ctx).at[1:1 + bl].set(c)
    mods = _modulation(cond, w_mod, b_mod).reshape(depth, SUBLANES, N_MOD, D_MODEL)
    fng = final_norm_g.reshape(1, D_MODEL)
    x = jnp.concatenate([x_prompt.reshape(n_ctx_tok, D_MODEL), x_sample.reshape(bl * ll, D_MODEL)], axis=0)
    ctx_vs = LANES // (ctx_spb * H_WKV)
    lat_vs = LANES // (bl * H_WKV)
    n_sb = bc // ctx_spb
    wkv_zero = jnp.zeros((N_DIR * n_sb, HS_WKV // ctx_vs, HS_WKV, LANES), F32)
    lru_zero = jnp.zeros((bc, N_DIR * W_MIX), F32)
    new_lru, new_wkv = [], []
    pnames = ["ya", "yd", "gbg", "la", "lu", "g", "bon", "rt", "vt", "kkt", "wt", "kt", "bt"]
    for i in range(depth):
        lw = _layer_weights(i, prm)
        mod = mods[i]
        z = _in_proj(x, mod, lw["norm1_g"], lw["w_in"], n_ctx_tok, ll)
        p = dict(zip(pnames, _prep(z, lw, n_ctx_tok, ll)))

        h_c, lru_s = _lru_scan(p["la"][:n_ctx_tok], p["lu"][:n_ctx_tok], lru_zero, lc, lru_spb)
        h_l, _ = _lru_scan(p["la"][n_ctx_tok:], p["lu"][n_ctx_tok:],
                           state_lru[:, i].astype(F32).reshape(bl, N_DIR * W_MIX), ll, 1)
        new_lru.append(lru_s.reshape(bc, N_DIR, W_MIX))

        wkv_in = [p[n] for n in ("rt", "wt", "kt", "kkt", "bt", "vt")]
        (y_c,), s_c = _wkv_scan(*wkv_in, wkv_zero, tile0=0, n_seq=bc, seq_tiles=1, spb=ctx_spb)
        y_l, _ = _wkv_scan(*wkv_in, _wkv_state_in(state_wkv[:, i].astype(F32), lat_vs),
                           tile0=n_ctx_tiles, n_seq=bl, seq_tiles=lat_tiles, spb=1)
        new_wkv.append(_wkv_state_out(s_c, n_sb, ctx_spb, ctx_vs))
        y_all = jnp.concatenate([y_c, *y_l], axis=1)

        tok_in = [p["ya"], p["yd"], p["gbg"], jnp.concatenate([h_c, h_l], axis=0), p["g"], p["bon"]]
        x1, h2 = _merge(x, mod, lw, tok_in, y_all, n_ctx_tok, ll)
        esel, gsel = _route(h2, lw["wq_t"], lw["keys"])
        x = _peer_dense(h2, esel, gsel, lw["peer_u"], lw["peer_v"], x1, mod, fng,
                        n_ctx_tok, ll, final=(i == depth - 1))
    y_prompt = x[:n_ctx_tok].reshape(bc, lc, D_MODEL)
    y_sample = x[n_ctx_tok:].reshape(bl, ll, D_MODEL)
    return (y_prompt, y_sample, jnp.stack(new_lru, axis=1), jnp.stack(new_wkv, axis=1))
```

```python
import functools

import numpy as np
import jax
import jax.numpy as jnp
from jax import lax
from jax.experimental import pallas as pl
from jax.experimental.pallas import tpu as pltpu

F32 = jnp.float32
BF16 = jnp.bfloat16
I32 = jnp.int32

D_MODEL = 1024
W_MIX = 512
N_DIR = 2
N_BRANCH = 4
H_WKV = 8
HS_WKV = 64
H_LRU = 8
HB_LRU = 64
LORA_W = 64
LORA_A = 64
LORA_G = 128
GRID_W = 64
CHUNK = 128
G_SG = 4
N_KEYS = 128
PEER_HEADS = 8
PEER_TOPK = 16
N_MOD = 6
EPS = 1e-6
LNX_EPS = 64e-5
LRU_C = 8.0

LANES = 128
SUBLANES = 8
TM = 256
TM_IN = 1024
TN_IN = 1408
TM_PEER = 512
PEER_EB = 2048
PEER_SUB = 512
Z_COLS = 5632
WKV_TC = LANES
WKV_CTX_SPB = 8
VMEM_LIMIT = 56 * 1024 * 1024

_NT = (((1,), (1,)), ((), ()))
_WKV_PERM = np.array([(n % H_WKV) * HS_WKV + n // H_WKV for n in range(W_MIX)])


def _cparams(sem, vmem=None):
    return pltpu.CompilerParams(dimension_semantics=sem, vmem_limit_bytes=vmem)


def _const_spec(shape):
    nd = len(shape)
    return pl.BlockSpec(shape, lambda *_: (0,) * nd)


def _softplus(x):
    return jnp.maximum(x, 0.0) + jnp.log1p(jnp.exp(-jnp.abs(x)))


def _rms(x, g):
    return x * lax.rsqrt(jnp.mean(x * x, axis=-1, keepdims=True) + EPS) * g


def _segsum(x, ind):
    hi = x.astype(BF16)
    lo = (x - hi.astype(F32)).astype(BF16)
    return (jnp.dot(hi, ind, preferred_element_type=F32)
            + jnp.dot(lo, ind, preferred_element_type=F32))


def _mod_index(i, tm, n_ctx_tok, lat_len):
    n_ctx_tiles = n_ctx_tok // tm
    tiles_per_seq = lat_len // tm
    return jnp.where(i < n_ctx_tiles, 0, 1 + lax.div(i - n_ctx_tiles, tiles_per_seq))


def _mod_body(s_ref, w_ref, b_ref, o_ref):
    s = s_ref[...]
    s = s * jax.nn.sigmoid(s)
    o_ref[0] = jnp.dot(s.astype(BF16), w_ref[0].astype(BF16), preferred_element_type=F32) + b_ref[0]


def _modulation(cond, w_mod, b_mod):
    depth = w_mod.shape[0]
    n = w_mod.shape[2]
    tn = 1536
    return pl.pallas_call(
        _mod_body,
        grid=(depth, n // tn),
        in_specs=[_const_spec((SUBLANES, D_MODEL)),
                  pl.BlockSpec((1, D_MODEL, tn), lambda l, j: (l, 0, j)),
                  pl.BlockSpec((1, 1, tn), lambda l, j: (l, 0, j))],
        out_specs=pl.BlockSpec((1, SUBLANES, tn), lambda l, j: (l, 0, j)),
        out_shape=jax.ShapeDtypeStruct((depth, SUBLANES, n), F32),
        compiler_params=_cparams(("parallel", "parallel"), VMEM_LIMIT),
        name="modulation",
    )(cond, w_mod, b_mod.reshape(depth, 1, n))


def _in_body(x_ref, mod_ref, g_ref, w_ref, o_ref, h_scr):
    @pl.when(pl.program_id(1) == 0)
    def _():
        m = mod_ref[0]
        y = _rms(x_ref[...], g_ref[...])
        h_scr[...] = (y * (1.0 + m[1:2, :]) + m[0:1, :]).astype(BF16)

    o_ref[...] = jnp.dot(h_scr[...], w_ref[...], preferred_element_type=F32)


def _in_proj(x, mod, g, w, n_ctx_tok, lat_len):
    t = x.shape[0]
    midx = functools.partial(_mod_index, tm=TM_IN, n_ctx_tok=n_ctx_tok, lat_len=lat_len)
    return pl.pallas_call(
        _in_body,
        grid=(t // TM_IN, Z_COLS // TN_IN),
        in_specs=[pl.BlockSpec((TM_IN, D_MODEL), lambda i, j: (i, 0)),
                  pl.BlockSpec((1, N_MOD, D_MODEL), lambda i, j: (midx(i), 0, 0)),
                  _const_spec((1, D_MODEL)),
                  pl.BlockSpec((D_MODEL, TN_IN), lambda i, j: (0, j))],
        out_specs=pl.BlockSpec((TM_IN, TN_IN), lambda i, j: (i, j)),
        out_shape=jax.ShapeDtypeStruct((t, Z_COLS), F32),
        scratch_shapes=[pltpu.VMEM((TM_IN, D_MODEL), BF16)],
        compiler_params=_cparams(("parallel", "arbitrary"), VMEM_LIMIT),
        name="in_proj",
    )(x, mod, g, w)


def _prep_body(za_ref, zc_ref, zb_ref, zd_ref, zl_ref, hp_ref, hn_ref,
               caw_ref, cbw_ref, cbb_ref, lruw_ref, lrub_ref, lam_ref,
               w0_ref, w2_ref, a0_ref, a2_ref, g2_ref, kkw_ref, ka_ref, rk_ref,
               lng_ref, lnb_ref, ws_ref, bst_ref, ind_ref,
               ya_ref, yd_ref, gbg_ref, la_ref, lu_ref, g_ref, bon_ref,
               rt_ref, vt_ref, kkt_ref, wt_ref, kt_ref, bt_ref,
               *, n_ctx_tiles, tiles_per_seq):
    i = pl.program_id(0)
    is_ctx = i < n_ctx_tiles
    t = lax.broadcasted_iota(I32, (TM, 1), 0)
    ind = ind_ref[...]

    pm = jnp.where(is_ctx, TM - 1, GRID_W - 1)
    pos = t & pm
    a_b = za_ref[:, 0:W_MIX]
    ac = za_ref[:, W_MIX:2 * W_MIX] * za_ref[:, 2 * W_MIX:3 * W_MIX]
    up = jnp.where(pos == 0, 0.0, pltpu.roll(ac, 1, 0))
    dn = jnp.where(pos == pm, 0.0, pltpu.roll(ac, TM - 1, 0))
    ya_ref[...] = a_b * (caw_ref[0:1, :] * up + caw_ref[1:2, :] * ac + caw_ref[2:3, :] * dn)

    seq_tile = lax.rem(jnp.maximum(i - n_ctx_tiles, 0), tiles_per_seq)
    first = jnp.logical_or(is_ctx, seq_tile == 0)
    last = jnp.logical_or(is_ctx, seq_tile == tiles_per_seq - 1)
    prev = jnp.where(first, 0.0, hp_ref[SUBLANES - 1:SUBLANES, :])
    nxt0 = jnp.where(last, 0.0, hn_ref[0:1, :])
    nxt1 = jnp.where(last, 0.0, hn_ref[1:2, :])
    bx = zb_ref[:, W_MIX:2 * W_MIX]
    m1 = jnp.where(t == 0, prev, pltpu.roll(bx, 1, 0))
    p1 = jnp.where(t == TM - 1, nxt0, pltpu.roll(bx, TM - 1, 0))
    p2 = jnp.where(t == TM - 2, nxt0, jnp.where(t == TM - 1, nxt1, pltpu.roll(bx, TM - 2, 0)))
    xb = (cbw_ref[0:1, :] * m1 + cbw_ref[1:2, :] * bx + cbw_ref[2:3, :] * p1
          + cbw_ref[3:4, :] * p2 + cbb_ref[...])
    gates = jnp.dot(xb.astype(BF16), lruw_ref[...], preferred_element_type=F32) + lrub_ref[...]
    rg = jax.nn.sigmoid(gates[:, 0:2 * W_MIX])
    ig = jax.nn.sigmoid(gates[:, 2 * W_MIX:4 * W_MIX])
    log_a = -LRU_C * rg * _softplus(-lam_ref[...])
    xb2 = jnp.concatenate([xb, xb], axis=1)
    la_ref[...] = jnp.exp(log_a)
    lu_ref[...] = jnp.sqrt(jnp.tanh(-log_a) * (jnp.exp(2.0 * log_a) + 1.0)) * (ig * xb2)
    gbg_ref[...] = jax.nn.gelu(zb_ref[:, 0:W_MIX])

    zr = zc_ref[:, 0:W_MIX]
    zk = zc_ref[:, W_MIX:2 * W_MIX]
    zv = zc_ref[:, 2 * W_MIX:3 * W_MIX]
    zwd = zl_ref[:, 0:2 * LORA_W]
    zad = zl_ref[:, 2 * LORA_W:2 * LORA_W + 2 * LORA_A]
    zgd = zl_ref[:, 2 * LORA_W + 2 * LORA_A:2 * LORA_W + 2 * LORA_A + LORA_G]
    wlin = w0_ref[...] + jnp.dot(jnp.tanh(zwd).astype(BF16), w2_ref[...], preferred_element_type=F32)
    wt_ref[0] = jnp.exp(-jnp.exp(-_softplus(-wlin) - 0.5)).T
    av = jax.nn.sigmoid(a0_ref[...] + jnp.dot(zad.astype(BF16), a2_ref[...], preferred_element_type=F32))
    g_ref[...] = jnp.dot(jax.nn.sigmoid(zgd).astype(BF16), g2_ref[...], preferred_element_type=F32)
    kkr = zk * kkw_ref[...]
    kkn = kkr / jnp.maximum(jnp.sqrt(_segsum(kkr * kkr, ind)), 1e-12)
    zk2 = jnp.concatenate([zk, zk], axis=1)
    ka2 = jnp.concatenate([ka_ref[...], ka_ref[...]], axis=1)
    kd = zk2 * (1.0 + (av - 1.0) * ka2)
    kt_ref[0] = kd.T
    bt_ref[0] = (jnp.concatenate([kkn, kkn], axis=1) * av).T
    rt_ref[0] = zr.T
    vt_ref[0] = zv.T
    kkt_ref[0] = kkn.T
    bon_ref[...] = _segsum(zr * (kd[:, 0:W_MIX] + kd[:, W_MIX:2 * W_MIX]) * rk_ref[...], ind) * zv

    zg = jax.nn.gelu(zd_ref[...])
    u = zg[:, 0:W_MIX]
    vv = zg[:, W_MIX:2 * W_MIX]
    vc = vv - jnp.mean(vv, axis=-1, keepdims=True)
    vn = vc * lax.rsqrt(jnp.mean(vc * vc, axis=-1, keepdims=True) + 1e-5) * lng_ref[...] + lnb_ref[...]
    for c in range(TM // CHUNK):
        rs = slice(c * CHUNK, (c + 1) * CHUNK)
        for gi in range(G_SG):
            cs = slice(gi * LANES, (gi + 1) * LANES)
            s = jnp.dot(ws_ref[gi], vn[rs, cs].astype(BF16), preferred_element_type=F32)
            yd_ref[rs, cs] = u[rs, cs] * (s + bst_ref[:, gi:gi + 1])


def _prep(z, lw, n_ctx_tok, lat_len):
    t = z.shape[0]
    n_tiles = t // TM
    n_ctx_tiles = n_ctx_tok // TM
    tiles_per_seq = lat_len // TM
    rows8 = TM // SUBLANES
    last_blk = t // SUBLANES - 1
    bx_blk = (3072 + W_MIX) // W_MIX
    z_specs = [
        pl.BlockSpec((TM, 1536), lambda i: (i, 0)),
        pl.BlockSpec((TM, 1536), lambda i: (i, 1)),
        pl.BlockSpec((TM, 1024), lambda i: (i, 3)),
        pl.BlockSpec((TM, 1024), lambda i: (i, 4)),
        pl.BlockSpec((TM, 512), lambda i: (i, 10)),
        pl.BlockSpec((SUBLANES, W_MIX), lambda i: (jnp.maximum(i * rows8 - 1, 0), bx_blk)),
        pl.BlockSpec((SUBLANES, W_MIX), lambda i: (jnp.minimum((i + 1) * rows8, last_blk), bx_blk)),
    ]
    wnames = ["conv_a_w", "conv_b_w", "conv_b_b", "lru_w", "lru_b", "lru_lam", "w0", "w2", "a0", "a2",
              "g2", "kk", "ka", "rk", "sg_ln_g", "sg_ln_b", "sg_ws", "sg_bst", "ind"]
    wts = [lw[n] for n in wnames]
    w_specs = [_const_spec(w.shape) for w in wts]
    widths = [W_MIX, W_MIX, W_MIX, 2 * W_MIX, 2 * W_MIX, W_MIX, W_MIX]
    t_rows = [W_MIX, W_MIX, W_MIX, 2 * W_MIX, 2 * W_MIX, 2 * W_MIX]
    out_specs = ([pl.BlockSpec((TM, wd), lambda i: (i, 0)) for wd in widths]
                 + [pl.BlockSpec((1, r, TM), lambda i: (i, 0, 0)) for r in t_rows])
    out_shape = ([jax.ShapeDtypeStruct((t, wd), F32) for wd in widths]
                 + [jax.ShapeDtypeStruct((n_tiles, r, TM), F32) for r in t_rows])
    return pl.pallas_call(
        functools.partial(_prep_body, n_ctx_tiles=n_ctx_tiles, tiles_per_seq=tiles_per_seq),
        grid=(n_tiles,),
        in_specs=z_specs + w_specs,
        out_specs=out_specs,
        out_shape=out_shape,
        compiler_params=_cparams(("parallel",), VMEM_LIMIT),
        name="branch_prep",
    )(z, z, z, z, z, z, z, *wts)


def _lru_body(a_ref, u_ref, h0_ref, h_ref, hf_ref, *, n_ctx_blocks, ctx_cfg, lat_cfg):
    fw, bw = slice(0, W_MIX), slice(W_MIX, 2 * W_MIX)

    def scan(nseq, l):
        def step(s, carry):
            out = []
            for j in range(nseq):
                tf = j * l + s
                tb = j * l + (l - 1 - s)
                hf = a_ref[pl.ds(tf, 1), fw] * carry[2 * j] + u_ref[pl.ds(tf, 1), fw]
                hb = a_ref[pl.ds(tb, 1), bw] * carry[2 * j + 1] + u_ref[pl.ds(tb, 1), bw]
                h_ref[pl.ds(tf, 1), fw] = hf
                h_ref[pl.ds(tb, 1), bw] = hb
                out += [hf, hb]
            return tuple(out)

        init = []
        for j in range(nseq):
            init += [h0_ref[0, j:j + 1, fw], h0_ref[0, j:j + 1, bw]]
        fin = lax.fori_loop(0, l, step, tuple(init), unroll=2)
        hf_ref[0] = h0_ref[0]
        for j in range(nseq):
            hf_ref[0, j:j + 1, fw] = fin[2 * j]
            hf_ref[0, j:j + 1, bw] = fin[2 * j + 1]

    is_ctx = pl.program_id(0) < n_ctx_blocks
    pl.when(is_ctx)(lambda: scan(*ctx_cfg))
    pl.when(jnp.logical_not(is_ctx))(lambda: scan(*lat_cfg))


def _lru_scan(a, u, h0, *, rows, n_ctx_blocks, ctx_cfg, lat_cfg):
    nb = a.shape[0] // rows
    w = a.shape[1]
    tok = pl.BlockSpec((rows, w), lambda i: (i, 0))
    st = pl.BlockSpec((1,) + h0.shape[1:], lambda i: (i, 0, 0))
    return pl.pallas_call(
        functools.partial(_lru_body, n_ctx_blocks=n_ctx_blocks, ctx_cfg=ctx_cfg, lat_cfg=lat_cfg),
        grid=(nb,),
        in_specs=[tok, tok, st],
        out_specs=[tok, st],
        out_shape=[jax.ShapeDtypeStruct(a.shape, F32), jax.ShapeDtypeStruct(h0.shape, F32)],
        compiler_params=_cparams(("parallel",), VMEM_LIMIT),
        name="lru_scan",
    )(a, u, h0)


_SLOT_ORDER = (0, 4, 2, 6, 1, 5, 3, 7)


def _rowsum8(parts):
    sub = lax.broadcasted_iota(I32, (SUBLANES, LANES), 0)
    slots = [parts[i] for i in _SLOT_ORDER]
    roll = pltpu.roll
    lvl1 = [jnp.where(sub < 4, a + roll(a, 4, 0), b + roll(b, 4, 0))
            for a, b in zip(slots[0::2], slots[1::2])]
    lvl2 = [jnp.where((sub & 3) < 2, a + roll(a, 6, 0), roll(b + roll(b, 6, 0), 2, 0))
            for a, b in zip(lvl1[0::2], lvl1[1::2])]
    a, b = lvl2
    return jnp.where((sub & 1) == 0, a + roll(a, 7, 0), roll(b + roll(b, 7, 0), 1, 0))


def _fold8(x):
    return jnp.sum(x.reshape(HS_WKV // SUBLANES, SUBLANES, LANES), axis=0)


def _wkv_body(*refs, nsrc, spb, vs, n_sb, tc, kp, vp):
    vl_n = HS_WKV // vs
    n_in = 6 * nsrc
    k_srcs = [refs[o * nsrc:(o + 1) * nsrc] for o in range(5)]
    v_srcs = refs[5 * nsrc:n_in]
    s0_ref = refs[n_in]
    y_ref = refs[n_in + 1]
    sf_ref = refs[n_in + 2]
    k_scr = refs[n_in + 3:n_in + 8]
    v_scr, y_scr, s_scr, sa_scr = refs[n_in + 8:n_in + 12]
    r_scr, w_scr, k_scr_, kk_scr, b_scr = k_scr
    backward = pl.program_id(0) // n_sb == 1
    seqs = [(s, j) for s in range(nsrc) for j in range(spb)]

    @pl.when(pl.program_id(1) == 0)
    def _():
        s_scr[...] = s0_ref[0]

    def build_k(c, carry):
        row = pl.multiple_of(c * H_WKV, H_WKV)
        for o in range(5):
            slab = [k_srcs[o][s][j, pl.ds(row, H_WKV), :] for s, j in seqs]
            k_scr[o][pl.ds(c, tc, stride=kp), :] = jnp.concatenate(slab * vs, axis=0).T
        return carry

    lax.fori_loop(0, HS_WKV, build_k, 0)

    def build_v(vl, carry):
        slab = []
        for vsi in range(vs):
            row = pl.multiple_of((vsi * vl_n + vl) * H_WKV, H_WKV)
            slab += [v_srcs[s][j, pl.ds(row, H_WKV), :] for s, j in seqs]
        v_scr[pl.ds(vl, tc, stride=vp), :] = jnp.concatenate(slab, axis=0).T
        return carry

    lax.fori_loop(0, vl_n, build_v, 0)

    def step(s, carry):
        t = jnp.where(backward, tc - 1 - s, s)
        krow = pl.multiple_of(t * kp, SUBLANES)
        vrow = pl.multiple_of(t * vp, SUBLANES)
        kslab = pl.ds(krow, HS_WKV)
        for g in range(vl_n // SUBLANES):
            parts = [_fold8(s_scr[g * SUBLANES + i] * kk_scr[kslab, :]) for i in range(SUBLANES)]
            sa_scr[g * SUBLANES:(g + 1) * SUBLANES, :] = _rowsum8(parts)
        for g in range(vl_n // SUBLANES):
            parts = []
            for i in range(SUBLANES):
                vl = g * SUBLANES + i
                sa = sa_scr[vl:vl + 1, :]
                vv = v_scr[pl.ds(vrow + vl, 1), :]
                sn = s_scr[vl] * w_scr[kslab, :] - sa * b_scr[kslab, :] + vv * k_scr_[kslab, :]
                s_scr[vl] = sn
                parts.append(_fold8(sn * r_scr[kslab, :]))
            y_scr[pl.ds(pl.multiple_of(vrow + g * SUBLANES, SUBLANES), SUBLANES), :] = _rowsum8(parts)
        return carry

    lax.fori_loop(0, tc, step, 0)

    def emit_y(vl, carry):
        yt = y_scr[pl.ds(vl, tc, stride=vp), :].T
        for vsi in range(vs):
            row = pl.multiple_of((vsi * vl_n + vl) * H_WKV, H_WKV)
            for n, (s, j) in enumerate(seqs):
                lane0 = (vsi * len(seqs) + n) * H_WKV
                y_ref[0, 0, s * spb + j, pl.ds(row, H_WKV), :] = yt[lane0:lane0 + H_WKV, :]
        return carry

    lax.fori_loop(0, vl_n, emit_y, 0)
    sf_ref[0] = s_scr[...]


def _wkv_scan(rt, wt, kt, kkt, bt, vt, s0, *, tile0, n_seq, seq_tiles, spb):
    tc = WKV_TC
    if spb > 1:
        assert seq_tiles == 1 and n_seq % spb == 0 and tile0 % spb == 0
        nsrc, n_sb = 1, n_seq // spb
    else:
        nsrc, n_sb = n_seq, 1
    inst = nsrc * spb * H_WKV
    vs = LANES // inst
    vl_n = HS_WKV // vs
    assert vl_n % SUBLANES == 0, "value rows are processed eight at a time"
    cpt = TM // tc
    n_chunks = seq_tiles * cpt
    kp = HS_WKV + SUBLANES
    vp = vl_n + SUBLANES if ((vl_n + SUBLANES) // SUBLANES) % 2 else vl_n + 2 * SUBLANES

    def chunk(g, i):
        return jnp.where(g // n_sb == 1, n_chunks - 1 - i, i)

    def in_map(g, i, *, src, per_dir):
        ce = chunk(g, i)
        rb = (g // n_sb) if per_dir else 0
        if spb > 1:
            return (tile0 // spb + g % n_sb, rb, ce)
        return (tile0 + src * seq_tiles + ce // cpt, rb, ce % cpt)

    def out_map(g, i):
        ce = chunk(g, i)
        if spb > 1:
            return (g // n_sb, 0, g % n_sb, 0, ce)
        return (g // n_sb, ce // cpt, 0, 0, ce % cpt)

    in_specs, operands = [], []
    for arr, per_dir in ((rt, False), (wt, True), (kt, True), (kkt, False), (bt, True), (vt, False)):
        for src in range(nsrc):
            in_specs.append(pl.BlockSpec((spb, W_MIX, tc), functools.partial(in_map, src=src, per_dir=per_dir),
                                         pipeline_mode=pl.Buffered(1)))
            operands.append(arr)
    sspec = pl.BlockSpec((1, vl_n, HS_WKV, LANES), lambda g, i: (g, 0, 0, 0))
    in_specs.append(sspec)
    out_specs = [pl.BlockSpec((1, 1, nsrc * spb, W_MIX, tc), out_map)]
    out_shape = [jax.ShapeDtypeStruct((N_DIR, seq_tiles, n_seq, W_MIX, TM), F32)]
    res = pl.pallas_call(
        functools.partial(_wkv_body, nsrc=nsrc, spb=spb, vs=vs, n_sb=n_sb, tc=tc, kp=kp, vp=vp),
        grid=(N_DIR * n_sb, n_chunks),
        in_specs=in_specs,
        out_specs=out_specs + [sspec],
        out_shape=out_shape + [jax.ShapeDtypeStruct(s0.shape, F32)],
        scratch_shapes=([pltpu.VMEM((tc * kp, LANES), F32)] * 5
                        + [pltpu.VMEM((tc * vp, LANES), F32)] * 2
                        + [pltpu.VMEM((vl_n, HS_WKV, LANES), F32), pltpu.VMEM((vl_n, LANES), F32)]),
        compiler_params=_cparams(("parallel", "arbitrary"), VMEM_LIMIT),
        name="wkv_scan",
    )(*operands, s0)
    return res[0], res[1]


def _merge_body(x_ref, mod_ref, n1g_ref, n2g_ref, wg_ref, gb_ref, wbr_ref, wo_ref,
                lnxg_ref, lnxb_ref, ind_ref,
                ya_ref, yd_ref, gbg_ref, h_ref, ycf_ref, ycb_ref, ylf_ref, ylb_ref, g_ref, bon_ref,
                x1_ref, h2_ref, *, n_ctx_tiles):
    x = x_ref[...]
    m = mod_ref[0]
    ind = ind_ref[...]
    h = (_rms(x, n1g_ref[...]) * (1.0 + m[1:2, :]) + m[0:1, :]).astype(BF16)
    y_b = gbg_ref[...] * (h_ref[:, 0:W_MIX] + h_ref[:, W_MIX:2 * W_MIX])
    is_ctx = pl.program_id(0) < n_ctx_tiles
    y = jnp.where(is_ctx, ycf_ref[0, 0, 0] + ycb_ref[0, 0, 0], ylf_ref[0, 0, 0] + ylb_ref[0, 0, 0]).T
    yc = y - _segsum(y, ind) * (1.0 / HS_WKV)
    var = _segsum(yc * yc, ind) * (1.0 / HS_WKV)
    y_c = (yc * lax.rsqrt(var + LNX_EPS) * lnxg_ref[...] + lnxb_ref[...] + bon_ref[...]) * g_ref[...]
    merged = None
    for n, yn in enumerate((ya_ref[...], y_b, y_c, yd_ref[...])):
        cs = slice(n * D_MODEL, (n + 1) * D_MODEL)
        gate = jax.nn.sigmoid(jnp.dot(h, wg_ref[:, cs], preferred_element_type=F32) + gb_ref[:, cs])
        br = jnp.dot(yn.astype(BF16), wbr_ref[n * W_MIX:(n + 1) * W_MIX, :], preferred_element_type=F32)
        merged = gate * br if merged is None else merged + gate * br
    mo = jnp.dot(merged.astype(BF16), wo_ref[...], preferred_element_type=F32)
    x1 = x + m[2:3, :] * mo
    x1_ref[...] = x1
    h2_ref[...] = (_rms(x1, n2g_ref[...]) * (1.0 + m[4:5, :]) + m[3:4, :]).astype(BF16)


def _merge(x, mod, lw, tok_in, y_ctx, y_lat, n_ctx_tok, lat_len):
    t = x.shape[0]
    n_ctx_tiles = n_ctx_tok // TM
    tps = lat_len // TM
    midx = functools.partial(_mod_index, tm=TM, n_ctx_tok=n_ctx_tok, lat_len=lat_len)
    wnames = ["norm1_g", "norm2_g", "w_gate", "gate_b", "w_branch", "w_out", "lnx_g", "lnx_b", "ind"]
    wts = [lw[n] for n in wnames]
    tok = lambda wd: pl.BlockSpec((TM, wd), lambda i: (i, 0))
    ya, yd, gbg, h, g, bon = tok_in
    yblock = (1, 1, 1, W_MIX, TM)

    def ctx_spec(d):
        return pl.BlockSpec(yblock, lambda i: (d, 0, jnp.minimum(i, n_ctx_tiles - 1), 0, 0))

    def lat_spec(d):
        def imap(i):
            r = jnp.maximum(i - n_ctx_tiles, 0)
            return (d, lax.rem(r, tps), lax.div(r, tps), 0, 0)
        return pl.BlockSpec(yblock, imap)

    return pl.pallas_call(
        functools.partial(_merge_body, n_ctx_tiles=n_ctx_tiles),
        grid=(t // TM,),
        in_specs=([tok(D_MODEL), pl.BlockSpec((1, N_MOD, D_MODEL), lambda i: (midx(i), 0, 0))]
                  + [_const_spec(w.shape) for w in wts]
                  + [tok(W_MIX), tok(W_MIX), tok(W_MIX), tok(2 * W_MIX),
                     ctx_spec(0), ctx_spec(1), lat_spec(0), lat_spec(1), tok(W_MIX), tok(W_MIX)]),
        out_specs=[tok(D_MODEL), tok(D_MODEL)],
        out_shape=[jax.ShapeDtypeStruct((t, D_MODEL), F32), jax.ShapeDtypeStruct((t, D_MODEL), BF16)],
        compiler_params=_cparams(("parallel",), VMEM_LIMIT),
        name="merge",
    )(x, mod, *wts, ya, yd, gbg, h, y_ctx, y_ctx, y_lat, y_lat, g, bon)


_CAND_VALID = (8, 8, 8, 5, 4, 3, 2, 2, 2, 8)


def _route_body(h2_ref, wqt_ref, keys_ref, e_ref, g_ref, q_scr, e_scr, g_scr):
    q_scr[...] = lax.dot_general(wqt_ref[...], h2_ref[...], _NT, preferred_element_type=F32)
    kio = lax.broadcasted_iota(I32, (N_KEYS, LANES), 0)
    sub = lax.broadcasted_iota(I32, (SUBLANES, LANES), 0)
    kid = lax.broadcasted_iota(I32, (PEER_TOPK, LANES), 0)
    neg = -jnp.inf

    def bc(x, r):
        return jnp.broadcast_to(x[r:r + 1, :], (SUBLANES, LANES))

    def head(h, carry):
        tops = []
        for p in range(2):
            row = pl.multiple_of(h * (2 * N_KEYS) + p * N_KEYS, N_KEYS)
            q = q_scr[pl.ds(row, N_KEYS), :].astype(BF16)
            s = jnp.dot(keys_ref[p], q, preferred_element_type=F32)
            vals = jnp.zeros((PEER_TOPK, LANES), F32)
            idxs = jnp.zeros((PEER_TOPK, LANES), I32)
            for r in range(PEER_TOPK):
                m = jnp.max(s, axis=0, keepdims=True)
                cand = jnp.where(s == m, kio, N_KEYS)
                ix = jnp.min(cand, axis=0, keepdims=True)
                s = jnp.where(cand == ix, neg, s)
                vals = jnp.where(kid == r, m, vals)
                idxs = jnp.where(kid == r, ix, idxs)
            tops.append((vals, idxs))
        (a0, i0), (a1, i1) = tops
        lo, hi = slice(0, SUBLANES), slice(SUBLANES, 2 * SUBLANES)
        slabs = [bc(a0, 0) + a1[lo], bc(a0, 0) + a1[hi]]
        ci = [bc(i0, 0), bc(i0, 0)]
        cj = [i1[lo], i1[hi]]
        for r in range(1, SUBLANES):
            slabs.append(bc(a0, r) + a1[lo])
            ci.append(bc(i0, r))
            cj.append(i1[lo])
        slabs.append(a0[hi] + bc(a1, 0))
        ci.append(i0[hi])
        cj.append(bc(i1, 0))
        slabs = [jnp.where(sub < nv, sl, neg) for sl, nv in zip(slabs, _CAND_VALID)]
        ids = [a * N_KEYS + b for a, b in zip(ci, cj)]
        vals = jnp.zeros((PEER_TOPK, LANES), F32)
        esel = jnp.zeros((PEER_TOPK, LANES), I32)
        for r in range(PEER_TOPK):
            level = list(zip(slabs, ids))
            while len(level) > 1:
                nxt = []
                for (va, ea), (vb, eb) in zip(level[0::2], level[1::2]):
                    take = vb > va
                    nxt.append((jnp.where(take, vb, va), jnp.where(take, eb, ea)))
                if len(level) % 2:
                    nxt.append(level[-1])
                level = nxt
            v8, e8 = level[0]
            for sh in (4, 2, 1):
                vr, er = pltpu.roll(v8, sh, 0), pltpu.roll(e8, sh, 0)
                take = vr > v8
                v8, e8 = jnp.where(take, vr, v8), jnp.where(take, er, e8)
            m, ex = v8[0:1, :], e8[0:1, :]
            slabs = [jnp.where(eid == ex, neg, sl) for sl, eid in zip(slabs, ids)]
            vals = jnp.where(kid == r, m, vals)
            esel = jnp.where(kid == r, ex, esel)
        e = jnp.exp(vals - vals[0:1, :])
        out_row = pl.multiple_of(h * PEER_TOPK, PEER_TOPK)
        g_scr[pl.ds(out_row, PEER_TOPK), :] = e / jnp.sum(e, axis=0, keepdims=True)
        e_scr[pl.ds(out_row, PEER_TOPK), :] = esel
        return carry

    lax.fori_loop(0, PEER_HEADS, head, 0, unroll=2)
    e_ref[...] = e_scr[...].T
    g_ref[...] = g_scr[...].T


def _route(h2, wqt, keys):
    t = h2.shape[0]
    nsel = PEER_HEADS * PEER_TOPK
    tok = pl.BlockSpec((LANES, nsel), lambda i: (i, 0))
    return pl.pallas_call(
        _route_body,
        grid=(t // LANES,),
        in_specs=[pl.BlockSpec((LANES, D_MODEL), lambda i: (i, 0)),
                  _const_spec(wqt.shape), _const_spec(keys.shape)],
        out_specs=[tok, tok],
        out_shape=[jax.ShapeDtypeStruct((t, nsel), I32), jax.ShapeDtypeStruct((t, nsel), F32)],
        scratch_shapes=[pltpu.VMEM((wqt.shape[0], LANES), F32), pltpu.VMEM((nsel, LANES), I32),
                        pltpu.VMEM((nsel, LANES), F32)],
        compiler_params=_cparams(("parallel",), VMEM_LIMIT),
        name="peer_route",
    )(h2, wqt, keys)


def _dense_body(h2_ref, e_ref, g_ref, u_ref, v_ref, x1_ref, mod_ref, fng_ref, o_ref,
                gs_scr, acc_scr, *, rows, pitch, final):
    e = pl.program_id(1)
    tm = h2_ref.shape[0]
    nsel = e_ref.shape[1]

    @pl.when(e == 0)
    def _build():
        kio = lax.broadcasted_iota(I32, (N_KEYS, nsel), 0)

        def tok(t, c):
            erow = e_ref[pl.ds(t, 1), :]
            grow = g_ref[pl.ds(t, 1), :]
            at = jnp.where(kio == (erow >> 7), grow, 0.0).astype(BF16)
            bt = jnp.where(kio == (erow & (N_KEYS - 1)), 1.0, 0.0).astype(BF16)
            gt = lax.dot_general(at, bt, _NT, preferred_element_type=F32)
            hi = pltpu.bitcast(gt[0:rows, :].astype(BF16).astype(F32), jnp.uint32)
            lo = pltpu.bitcast(gt[rows:2 * rows, :].astype(BF16).astype(F32), jnp.uint32)
            gs_scr[pl.ds(pl.multiple_of(t * pitch, SUBLANES), rows), :] = hi | (lo >> 16)
            return c

        lax.fori_loop(0, tm, tok, 0, unroll=16)
        acc_scr[...] = jnp.zeros_like(acc_scr)

    per_sub = PEER_SUB // N_KEYS
    per_step = u_ref.shape[0] // N_KEYS
    steps_per_half = rows // per_step
    row0 = lax.rem(e, steps_per_half) * per_step
    shift = jnp.where(e < steps_per_half, 0, 16).astype(jnp.uint32)
    h2 = h2_ref[...]
    total = None
    for sb in range(per_step // per_sub):
        es = slice(sb * PEER_SUB, (sb + 1) * PEER_SUB)
        hmat = lax.dot_general(h2, u_ref[es, :], _NT, preferred_element_type=F32)
        words = jnp.concatenate(
            [gs_scr[pl.ds(row0 + sb * per_sub + ii, tm, stride=pitch), :] for ii in range(per_sub)], axis=1)
        gm = pltpu.bitcast((words << shift) & jnp.uint32(0xFFFF0000), F32)
        act = (jax.nn.gelu(hmat) * gm).astype(BF16)
        part = jnp.dot(act, v_ref[es, :], preferred_element_type=F32)
        total = part if total is None else total + part
    acc_scr[...] += total

    @pl.when(e == pl.num_programs(1) - 1)
    def _out():
        x2 = x1_ref[...] + mod_ref[0][5:6, :] * acc_scr[...]
        o_ref[...] = _rms(x2, fng_ref[...]) if final else x2


def _peer_dense(h2, esel, gsel, u, v, x1, mod, fng, n_ctx_tok, lat_len, final):
    t = h2.shape[0]
    nsel = esel.shape[1]
    rows = N_KEYS // 2
    pitch = rows + SUBLANES
    n_e = (N_KEYS * N_KEYS) // PEER_EB
    midx = functools.partial(_mod_index, tm=TM_PEER, n_ctx_tok=n_ctx_tok, lat_len=lat_len)
    tok = lambda wd: pl.BlockSpec((TM_PEER, wd), lambda m, e: (m, 0))
    espec = pl.BlockSpec((PEER_EB, D_MODEL), lambda m, e: (e, 0))
    return pl.pallas_call(
        functools.partial(_dense_body, rows=rows, pitch=pitch, final=final),
        grid=(t // TM_PEER, n_e),
        in_specs=[tok(D_MODEL), tok(nsel), tok(nsel), espec, espec, tok(D_MODEL),
                  pl.BlockSpec((1, N_MOD, D_MODEL), lambda m, e: (midx(m), 0, 0)),
                  _const_spec((1, D_MODEL))],
        out_specs=tok(D_MODEL),
        out_shape=jax.ShapeDtypeStruct((t, D_MODEL), F32),
        scratch_shapes=[pltpu.VMEM((TM_PEER * pitch, N_KEYS), jnp.uint32),
                        pltpu.VMEM((TM_PEER, D_MODEL), F32)],
        compiler_params=_cparams(("parallel", "arbitrary"), VMEM_LIMIT),
        name="peer_dense",
    )(h2, esel, gsel, u, v, x1, mod, fng)


def _wkv_state_in(s, vs):
    n = s.shape[0]
    vl = HS_WKV // vs
    s = s.reshape(n, N_DIR, H_WKV, vs, vl, HS_WKV).transpose(1, 4, 5, 3, 0, 2)
    return s.reshape(N_DIR, vl, HS_WKV, vs * n * H_WKV)


def _wkv_state_out(s, n_sb, spb, vs):
    vl = HS_WKV // vs
    s = s.reshape(N_DIR, n_sb, vl, HS_WKV, vs, spb, H_WKV).transpose(1, 5, 0, 6, 4, 2, 3)
    return s.reshape(n_sb * spb, N_DIR, H_WKV, HS_WKV, HS_WKV)


def _layer_weights(i, prm):
    eye_h = jnp.eye(H_LRU, dtype=F32)
    eye_d = jnp.eye(N_DIR, dtype=F32)
    perm = _WKV_PERM

    def lru_bd(wt):
        return jnp.einsum("dhij,hg->hidgj", wt, eye_h).reshape(W_MIX, N_DIR * W_MIX)

    def lora_bd(wt):
        r = wt.shape[1]
        return jnp.einsum("drc,de->drec", wt, eye_d).reshape(N_DIR * r, N_DIR * W_MIX)

    w_in = prm["w_in"][i]
    pad = jnp.zeros((D_MODEL, Z_COLS - 5504), F32)
    rkv = [w_in[:, 2560 + j * W_MIX:2560 + (j + 1) * W_MIX][:, perm] for j in range(3)]
    w_in_perm = jnp.concatenate(
        [w_in[:, 0:1536]] + rkv + [w_in[:, 1536:2560], w_in[:, 4480:5504], w_in[:, 4096:4480], pad],
        axis=1).astype(BF16)
    row = lambda x: x.reshape(1, -1).astype(F32)
    head_of = np.arange(W_MIX) % H_WKV
    w_branch = prm["w_branch"][i]
    w_branch = jnp.concatenate([w_branch[0], w_branch[1], w_branch[2][perm, :], w_branch[3]], axis=0)
    return {
        "w_in": w_in_perm,
        "w_gate": w_in[:, 5504:].astype(BF16),
        "norm1_g": row(prm["norm1_g"][i]),
        "norm2_g": row(prm["norm2_g"][i]),
        "conv_a_w": prm["conv_a_w"][i],
        "conv_b_w": prm["conv_b_w"][i],
        "conv_b_b": row(prm["conv_b_b"][i]),
        "lru_w": jnp.concatenate([lru_bd(prm["lru_wa"][i]), lru_bd(prm["lru_wx"][i])], axis=1).astype(BF16),
        "lru_b": jnp.concatenate([row(prm["lru_ba"][i]), row(prm["lru_bx"][i])], axis=1),
        "lru_lam": row(prm["lru_lambda"][i]),
        "w0": row(prm["rwkv_w0"][i][:, perm]),
        "w2": lora_bd(prm["rwkv_w2"][i][:, :, perm]).astype(BF16),
        "a0": row(prm["rwkv_a0"][i][:, perm]),
        "a2": lora_bd(prm["rwkv_a2"][i][:, :, perm]).astype(BF16),
        "g2": prm["rwkv_g2"][i][:, perm].astype(BF16),
        "kk": row(prm["rwkv_kk"][i][perm]),
        "ka": row(prm["rwkv_ka"][i][perm]),
        "rk": row(prm["rwkv_rk"][i].reshape(W_MIX)[perm]),
        "lnx_g": row(prm["lnx_g"][i][perm]),
        "lnx_b": row(prm["lnx_b"][i][perm]),
        "sg_ln_g": row(prm["sg_ln_g"][i]),
        "sg_ln_b": row(prm["sg_ln_b"][i]),
        "sg_ws": prm["sg_ws"][i].astype(BF16),
        "sg_bst": prm["sg_bs"][i].T,
        "gate_b": row(prm["gate_b"][i]),
        "w_branch": w_branch.astype(BF16),
        "w_out": prm["w_out"][i].astype(BF16),
        "wq_t": prm["peer_wq"][i].T.astype(BF16),
        "keys": prm["peer_keys"][i].astype(BF16),
        "peer_u": prm["peer_u"][i].astype(BF16),
        "peer_v": prm["peer_v"][i].astype(BF16),
        "ind": jnp.asarray(head_of[:, None] == head_of[None, :], BF16),
    }


def kernel(x_prompt, x_sample, state_lru, state_wkv, c, c_ctx, norm1_g, norm2_g, w_mod, b_mod, w_in, conv_a_w, conv_b_w, conv_b_b, lru_wa, lru_ba, lru_wx, lru_bx, lru_lambda, rwkv_w0, rwkv_w2, rwkv_a0, rwkv_a2, rwkv_g2, rwkv_kk, rwkv_ka, rwkv_rk, lnx_g, lnx_b, sg_ln_g, sg_ln_b, sg_ws, sg_bs, gate_b, w_branch, w_out, peer_wq, peer_keys, peer_u, peer_v, final_norm_g):
    prm = dict(norm1_g=norm1_g, norm2_g=norm2_g, w_in=w_in, conv_a_w=conv_a_w, conv_b_w=conv_b_w,
               conv_b_b=conv_b_b, lru_wa=lru_wa, lru_ba=lru_ba, lru_wx=lru_wx, lru_bx=lru_bx,
               lru_lambda=lru_lambda, rwkv_w0=rwkv_w0, rwkv_w2=rwkv_w2, rwkv_a0=rwkv_a0, rwkv_a2=rwkv_a2,
               rwkv_g2=rwkv_g2, rwkv_kk=rwkv_kk, rwkv_ka=rwkv_ka, rwkv_rk=rwkv_rk, lnx_g=lnx_g,
               lnx_b=lnx_b, sg_ln_g=sg_ln_g, sg_ln_b=sg_ln_b, sg_ws=sg_ws, sg_bs=sg_bs, gate_b=gate_b,
               w_branch=w_branch, w_out=w_out, peer_wq=peer_wq, peer_keys=peer_keys, peer_u=peer_u,
               peer_v=peer_v)
    bc, lc, _ = x_prompt.shape
    bl, ll, _ = x_sample.shape
    depth = w_mod.shape[0]
    n_ctx_tok = bc * lc
    n_ctx_tiles = n_ctx_tok // TM
    lat_tiles = ll // TM
    ctx_spb = min(WKV_CTX_SPB, bc)
    lru_spb = ll // lc
    assert lc == TM and ll % TM_IN == 0 and n_ctx_tok % TM_IN == 0 and bl + 1 <= SUBLANES
    assert ll % GRID_W == 0 and TM % GRID_W == 0 and bc % ctx_spb == 0 and n_ctx_tok % ll == 0
    assert LANES % (ctx_spb * H_WKV) == 0 and LANES % (bl * H_WKV) == 0

    cond = jnp.zeros((SUBLANES, D_MODEL), F32).at[0].set(c_ctx).at[1:1 + bl].set(c)
    mods = _modulation(cond, w_mod, b_mod).reshape(depth, SUBLANES, N_MOD, D_MODEL)
    fng = final_norm_g.reshape(1, D_MODEL)
    x = jnp.concatenate([x_prompt.reshape(n_ctx_tok, D_MODEL), x_sample.reshape(bl * ll, D_MODEL)], axis=0)
    ctx_vs = LANES // (ctx_spb * H_WKV)
    lat_vs = LANES // (bl * H_WKV)
    n_sb = bc // ctx_spb
    wkv_zero = jnp.zeros((N_DIR * n_sb, HS_WKV // ctx_vs, HS_WKV, LANES), F32)
    lru_zero = jnp.zeros((n_ctx_tok // ll, lru_spb, N_DIR * W_MIX), F32)
    new_lru, new_wkv = [], []
    pnames = ["ya", "yd", "gbg", "la", "lu", "g", "bon", "rt", "vt", "kkt", "wt", "kt", "bt"]
    for i in range(depth):
        lw = _layer_weights(i, prm)
        mod = mods[i]
        z = _in_proj(x, mod, lw["norm1_g"], lw["w_in"], n_ctx_tok, ll)
        p = dict(zip(pnames, _prep(z, lw, n_ctx_tok, ll)))

        lat_h0 = jnp.zeros((bl, lru_spb, N_DIR * W_MIX), F32).at[:, 0].set(
            state_lru[:, i].astype(F32).reshape(bl, N_DIR * W_MIX))
        h, lru_s = _lru_scan(p["la"], p["lu"], jnp.concatenate([lru_zero, lat_h0], axis=0),
                             rows=ll, n_ctx_blocks=n_ctx_tok // ll, ctx_cfg=(lru_spb, lc), lat_cfg=(1, ll))
        new_lru.append(lru_s[:n_ctx_tok // ll].reshape(bc, N_DIR, W_MIX))

        wkv_in = [p[n] for n in ("rt", "wt", "kt", "kkt", "bt", "vt")]
        y_c, s_c = _wkv_scan(*wkv_in, wkv_zero, tile0=0, n_seq=bc, seq_tiles=1, spb=ctx_spb)
        y_l, _ = _wkv_scan(*wkv_in, _wkv_state_in(state_wkv[:, i].astype(F32), lat_vs),
                           tile0=n_ctx_tiles, n_seq=bl, seq_tiles=lat_tiles, spb=1)
        new_wkv.append(_wkv_state_out(s_c, n_sb, ctx_spb, ctx_vs))

        tok_in = [p["ya"], p["yd"], p["gbg"], h, p["g"], p["bon"]]
        x1, h2 = _merge(x, mod, lw, tok_in, y_c, y_l, n_ctx_tok, ll)
        esel, gsel = _route(h2, lw["wq_t"], lw["keys"])
        x = _peer_dense(h2, esel, gsel, lw["peer_u"], lw["peer_v"], x1, mod, fng,
                        n_ctx_tok, ll, final=(i == depth - 1))
    y_prompt = x[:n_ctx_tok].reshape(bc, lc, D_MODEL)
    y_sample = x[n_ctx_tok:].reshape(bl, ll, D_MODEL)
    return (y_prompt, y_sample, jnp.stack(new_lru, axis=1), jnp.stack(new_wkv, axis=1))
```

```python
import functools

import numpy as np
import jax
import jax.numpy as jnp
from jax import lax
from jax.experimental import pallas as pl
from jax.experimental.pallas import tpu as pltpu

F32 = jnp.float32
BF16 = jnp.bfloat16
I32 = jnp.int32

D_MODEL = 1024
W_MIX = 512
N_DIR = 2
N_BRANCH = 4
H_WKV = 8
HS_WKV = 64
H_LRU = 8
HB_LRU = 64
LORA_W = 64
LORA_A = 64
LORA_G = 128
GRID_W = 64
CHUNK = 128
G_SG = 4
N_KEYS = 128
PEER_HEADS = 8
PEER_TOPK = 16
N_MOD = 6
EPS = 1e-6
LNX_EPS = 64e-5
LRU_C = 8.0

LANES = 128
SUBLANES = 8
TM = 256
TM_IN = 1024
TN_IN = 1408
TM_PEER = 512
PEER_EB = 1024
PEER_SUB = 512
Z_COLS = 5632
WKV_TC = LANES
WKV_CTX_SPB = 8
VMEM_LIMIT = 56 * 1024 * 1024

_NT = (((1,), (1,)), ((), ()))
_WKV_PERM = np.array([(n % H_WKV) * HS_WKV + n // H_WKV for n in range(W_MIX)])


def _cparams(sem, vmem=None):
    return pltpu.CompilerParams(dimension_semantics=sem, vmem_limit_bytes=vmem)


def _const_spec(shape):
    nd = len(shape)
    return pl.BlockSpec(shape, lambda *_: (0,) * nd)


def _softplus(x):
    return jnp.maximum(x, 0.0) + jnp.log1p(jnp.exp(-jnp.abs(x)))


def _rms(x, g):
    return x * lax.rsqrt(jnp.mean(x * x, axis=-1, keepdims=True) + EPS) * g


def _segsum(x, ind):
    hi = x.astype(BF16)
    lo = (x - hi.astype(F32)).astype(BF16)
    return (jnp.dot(hi, ind, preferred_element_type=F32)
            + jnp.dot(lo, ind, preferred_element_type=F32))


def _mod_index(i, tm, n_ctx_tok, lat_len):
    n_ctx_tiles = n_ctx_tok // tm
    tiles_per_seq = lat_len // tm
    return jnp.where(i < n_ctx_tiles, 0, 1 + lax.div(i - n_ctx_tiles, tiles_per_seq))


def _mod_body(s_ref, w_ref, b_ref, o_ref):
    s = s_ref[...]
    s = s * jax.nn.sigmoid(s)
    o_ref[0] = jnp.dot(s.astype(BF16), w_ref[0].astype(BF16), preferred_element_type=F32) + b_ref[0]


def _modulation(cond, w_mod, b_mod):
    depth = w_mod.shape[0]
    n = w_mod.shape[2]
    tn = 1536
    return pl.pallas_call(
        _mod_body,
        grid=(depth, n // tn),
        in_specs=[_const_spec((SUBLANES, D_MODEL)),
                  pl.BlockSpec((1, D_MODEL, tn), lambda l, j: (l, 0, j)),
                  pl.BlockSpec((1, 1, tn), lambda l, j: (l, 0, j))],
        out_specs=pl.BlockSpec((1, SUBLANES, tn), lambda l, j: (l, 0, j)),
        out_shape=jax.ShapeDtypeStruct((depth, SUBLANES, n), F32),
        compiler_params=_cparams(("parallel", "parallel"), VMEM_LIMIT),
        name="modulation",
    )(cond, w_mod, b_mod.reshape(depth, 1, n))


def _in_body(x_ref, mod_ref, g_ref, w_ref, o_ref, h_scr):
    @pl.when(pl.program_id(1) == 0)
    def _():
        m = mod_ref[0]
        y = _rms(x_ref[...], g_ref[...])
        h_scr[...] = (y * (1.0 + m[1:2, :]) + m[0:1, :]).astype(BF16)

    o_ref[...] = jnp.dot(h_scr[...], w_ref[...], preferred_element_type=F32)


def _in_proj(x, mod, g, w, n_ctx_tok, lat_len):
    t = x.shape[0]
    midx = functools.partial(_mod_index, tm=TM_IN, n_ctx_tok=n_ctx_tok, lat_len=lat_len)
    return pl.pallas_call(
        _in_body,
        grid=(t // TM_IN, Z_COLS // TN_IN),
        in_specs=[pl.BlockSpec((TM_IN, D_MODEL), lambda i, j: (i, 0)),
                  pl.BlockSpec((1, N_MOD, D_MODEL), lambda i, j: (midx(i), 0, 0)),
                  _const_spec((1, D_MODEL)),
                  pl.BlockSpec((D_MODEL, TN_IN), lambda i, j: (0, j))],
        out_specs=pl.BlockSpec((TM_IN, TN_IN), lambda i, j: (i, j)),
        out_shape=jax.ShapeDtypeStruct((t, Z_COLS), F32),
        scratch_shapes=[pltpu.VMEM((TM_IN, D_MODEL), BF16)],
        compiler_params=_cparams(("parallel", "arbitrary"), VMEM_LIMIT),
        name="in_proj",
    )(x, mod, g, w)


def _prep_body(za_ref, zc_ref, zb_ref, zd_ref, zl_ref, hp_ref, hn_ref,
               caw_ref, cbw_ref, cbb_ref, lruw_ref, lrub_ref, lam_ref,
               w0_ref, w2_ref, a0_ref, a2_ref, g2_ref, kkw_ref, ka_ref, rk_ref,
               lng_ref, lnb_ref, ws_ref, bst_ref, ind_ref,
               ya_ref, yd_ref, gbg_ref, la_ref, lu_ref, g_ref, bon_ref,
               rt_ref, vt_ref, kkt_ref, wt_ref, kt_ref, bt_ref,
               *, n_ctx_tiles, tiles_per_seq):
    i = pl.program_id(0)
    is_ctx = i < n_ctx_tiles
    t = lax.broadcasted_iota(I32, (TM, 1), 0)
    ind = ind_ref[...]

    pm = jnp.where(is_ctx, TM - 1, GRID_W - 1)
    pos = t & pm
    a_b = za_ref[:, 0:W_MIX]
    ac = za_ref[:, W_MIX:2 * W_MIX] * za_ref[:, 2 * W_MIX:3 * W_MIX]
    up = jnp.where(pos == 0, 0.0, pltpu.roll(ac, 1, 0))
    dn = jnp.where(pos == pm, 0.0, pltpu.roll(ac, TM - 1, 0))
    ya_ref[...] = a_b * (caw_ref[0:1, :] * up + caw_ref[1:2, :] * ac + caw_ref[2:3, :] * dn)

    seq_tile = lax.rem(jnp.maximum(i - n_ctx_tiles, 0), tiles_per_seq)
    first = jnp.logical_or(is_ctx, seq_tile == 0)
    last = jnp.logical_or(is_ctx, seq_tile == tiles_per_seq - 1)
    prev = jnp.where(first, 0.0, hp_ref[SUBLANES - 1:SUBLANES, :])
    nxt0 = jnp.where(last, 0.0, hn_ref[0:1, :])
    nxt1 = jnp.where(last, 0.0, hn_ref[1:2, :])
    bx = zb_ref[:, W_MIX:2 * W_MIX]
    m1 = jnp.where(t == 0, prev, pltpu.roll(bx, 1, 0))
    p1 = jnp.where(t == TM - 1, nxt0, pltpu.roll(bx, TM - 1, 0))
    p2 = jnp.where(t == TM - 2, nxt0, jnp.where(t == TM - 1, nxt1, pltpu.roll(bx, TM - 2, 0)))
    xb = (cbw_ref[0:1, :] * m1 + cbw_ref[1:2, :] * bx + cbw_ref[2:3, :] * p1
          + cbw_ref[3:4, :] * p2 + cbb_ref[...])
    gates = jnp.dot(xb.astype(BF16), lruw_ref[...], preferred_element_type=F32) + lrub_ref[...]
    rg = jax.nn.sigmoid(gates[:, 0:2 * W_MIX])
    ig = jax.nn.sigmoid(gates[:, 2 * W_MIX:4 * W_MIX])
    log_a = -LRU_C * rg * _softplus(-lam_ref[...])
    xb2 = jnp.concatenate([xb, xb], axis=1)
    la_ref[...] = jnp.exp(log_a)
    lu_ref[...] = jnp.sqrt(jnp.tanh(-log_a) * (jnp.exp(2.0 * log_a) + 1.0)) * (ig * xb2)
    gbg_ref[...] = jax.nn.gelu(zb_ref[:, 0:W_MIX])

    zr = zc_ref[:, 0:W_MIX]
    zk = zc_ref[:, W_MIX:2 * W_MIX]
    zv = zc_ref[:, 2 * W_MIX:3 * W_MIX]
    zwd = zl_ref[:, 0:2 * LORA_W]
    zad = zl_ref[:, 2 * LORA_W:2 * LORA_W + 2 * LORA_A]
    zgd = zl_ref[:, 2 * LORA_W + 2 * LORA_A:2 * LORA_W + 2 * LORA_A + LORA_G]
    wlin = w0_ref[...] + jnp.dot(jnp.tanh(zwd).astype(BF16), w2_ref[...], preferred_element_type=F32)
    wt_ref[0] = jnp.exp(-jnp.exp(-_softplus(-wlin) - 0.5)).T
    av = jax.nn.sigmoid(a0_ref[...] + jnp.dot(zad.astype(BF16), a2_ref[...], preferred_element_type=F32))
    g_ref[...] = jnp.dot(jax.nn.sigmoid(zgd).astype(BF16), g2_ref[...], preferred_element_type=F32)
    kkr = zk * kkw_ref[...]
    kkn = kkr / jnp.maximum(jnp.sqrt(_segsum(kkr * kkr, ind)), 1e-12)
    zk2 = jnp.concatenate([zk, zk], axis=1)
    ka2 = jnp.concatenate([ka_ref[...], ka_ref[...]], axis=1)
    kd = zk2 * (1.0 + (av - 1.0) * ka2)
    kt_ref[0] = kd.T
    bt_ref[0] = (jnp.concatenate([kkn, kkn], axis=1) * av).T
    rt_ref[0] = zr.T
    vt_ref[0] = zv.T
    kkt_ref[0] = kkn.T
    bon_ref[...] = _segsum(zr * (kd[:, 0:W_MIX] + kd[:, W_MIX:2 * W_MIX]) * rk_ref[...], ind) * zv

    zg = jax.nn.gelu(zd_ref[...])
    u = zg[:, 0:W_MIX]
    vv = zg[:, W_MIX:2 * W_MIX]
    vc = vv - jnp.mean(vv, axis=-1, keepdims=True)
    vn = vc * lax.rsqrt(jnp.mean(vc * vc, axis=-1, keepdims=True) + 1e-5) * lng_ref[...] + lnb_ref[...]
    for c in range(TM // CHUNK):
        rs = slice(c * CHUNK, (c + 1) * CHUNK)
        for gi in range(G_SG):
            cs = slice(gi * LANES, (gi + 1) * LANES)
            s = jnp.dot(ws_ref[gi], vn[rs, cs].astype(BF16), preferred_element_type=F32)
            yd_ref[rs, cs] = u[rs, cs] * (s + bst_ref[:, gi:gi + 1])


def _prep(z, lw, n_ctx_tok, lat_len):
    t = z.shape[0]
    n_tiles = t // TM
    n_ctx_tiles = n_ctx_tok // TM
    tiles_per_seq = lat_len // TM
    rows8 = TM // SUBLANES
    last_blk = t // SUBLANES - 1
    bx_blk = (3072 + W_MIX) // W_MIX
    z_specs = [
        pl.BlockSpec((TM, 1536), lambda i: (i, 0)),
        pl.BlockSpec((TM, 1536), lambda i: (i, 1)),
        pl.BlockSpec((TM, 1024), lambda i: (i, 3)),
        pl.BlockSpec((TM, 1024), lambda i: (i, 4)),
        pl.BlockSpec((TM, 512), lambda i: (i, 10)),
        pl.BlockSpec((SUBLANES, W_MIX), lambda i: (jnp.maximum(i * rows8 - 1, 0), bx_blk)),
        pl.BlockSpec((SUBLANES, W_MIX), lambda i: (jnp.minimum((i + 1) * rows8, last_blk), bx_blk)),
    ]
    wnames = ["conv_a_w", "conv_b_w", "conv_b_b", "lru_w", "lru_b", "lru_lam", "w0", "w2", "a0", "a2",
              "g2", "kk", "ka", "rk", "sg_ln_g", "sg_ln_b", "sg_ws", "sg_bst", "ind"]
    wts = [lw[n] for n in wnames]
    w_specs = [_const_spec(w.shape) for w in wts]
    widths = [W_MIX, W_MIX, W_MIX, 2 * W_MIX, 2 * W_MIX, W_MIX, W_MIX]
    t_rows = [W_MIX, W_MIX, W_MIX, 2 * W_MIX, 2 * W_MIX, 2 * W_MIX]
    out_specs = ([pl.BlockSpec((TM, wd), lambda i: (i, 0)) for wd in widths]
                 + [pl.BlockSpec((1, r, TM), lambda i: (i, 0, 0)) for r in t_rows])
    out_shape = ([jax.ShapeDtypeStruct((t, wd), F32) for wd in widths]
                 + [jax.ShapeDtypeStruct((n_tiles, r, TM), F32) for r in t_rows])
    return pl.pallas_call(
        functools.partial(_prep_body, n_ctx_tiles=n_ctx_tiles, tiles_per_seq=tiles_per_seq),
        grid=(n_tiles,),
        in_specs=z_specs + w_specs,
        out_specs=out_specs,
        out_shape=out_shape,
        compiler_params=_cparams(("parallel",), VMEM_LIMIT),
        name="branch_prep",
    )(z, z, z, z, z, z, z, *wts)


def _lru_body(a_ref, u_ref, h0_ref, h_ref, hf_ref, *, n_ctx_blocks, ctx_cfg, lat_cfg):
    fw, bw = slice(0, W_MIX), slice(W_MIX, 2 * W_MIX)

    def scan(nseq, l):
        def step(s, carry):
            out = []
            for j in range(nseq):
                tf = j * l + s
                tb = j * l + (l - 1 - s)
                hf = a_ref[pl.ds(tf, 1), fw] * carry[2 * j] + u_ref[pl.ds(tf, 1), fw]
                hb = a_ref[pl.ds(tb, 1), bw] * carry[2 * j + 1] + u_ref[pl.ds(tb, 1), bw]
                h_ref[pl.ds(tf, 1), fw] = hf
                h_ref[pl.ds(tb, 1), bw] = hb
                out += [hf, hb]
            return tuple(out)

        init = []
        for j in range(nseq):
            init += [h0_ref[0, j:j + 1, fw], h0_ref[0, j:j + 1, bw]]
        fin = lax.fori_loop(0, l, step, tuple(init), unroll=2)
        hf_ref[0] = h0_ref[0]
        for j in range(nseq):
            hf_ref[0, j:j + 1, fw] = fin[2 * j]
            hf_ref[0, j:j + 1, bw] = fin[2 * j + 1]

    is_ctx = pl.program_id(0) < n_ctx_blocks
    pl.when(is_ctx)(lambda: scan(*ctx_cfg))
    pl.when(jnp.logical_not(is_ctx))(lambda: scan(*lat_cfg))


def _lru_scan(a, u, h0, *, rows, n_ctx_blocks, ctx_cfg, lat_cfg):
    nb = a.shape[0] // rows
    w = a.shape[1]
    tok = pl.BlockSpec((rows, w), lambda i: (i, 0))
    st = pl.BlockSpec((1,) + h0.shape[1:], lambda i: (i, 0, 0))
    return pl.pallas_call(
        functools.partial(_lru_body, n_ctx_blocks=n_ctx_blocks, ctx_cfg=ctx_cfg, lat_cfg=lat_cfg),
        grid=(nb,),
        in_specs=[tok, tok, st],
        out_specs=[tok, st],
        out_shape=[jax.ShapeDtypeStruct(a.shape, F32), jax.ShapeDtypeStruct(h0.shape, F32)],
        compiler_params=_cparams(("parallel",), VMEM_LIMIT),
        name="lru_scan",
    )(a, u, h0)


_SLOT_ORDER = (0, 4, 2, 6, 1, 5, 3, 7)


def _rowsum8(parts):
    sub = lax.broadcasted_iota(I32, (SUBLANES, LANES), 0)
    slots = [parts[i] for i in _SLOT_ORDER]
    roll = pltpu.roll
    lvl1 = [jnp.where(sub < 4, a + roll(a, 4, 0), b + roll(b, 4, 0))
            for a, b in zip(slots[0::2], slots[1::2])]
    lvl2 = [jnp.where((sub & 3) < 2, a + roll(a, 6, 0), roll(b + roll(b, 6, 0), 2, 0))
            for a, b in zip(lvl1[0::2], lvl1[1::2])]
    a, b = lvl2
    return jnp.where((sub & 1) == 0, a + roll(a, 7, 0), roll(b + roll(b, 7, 0), 1, 0))


def _fold8(x):
    return jnp.sum(x.reshape(HS_WKV // SUBLANES, SUBLANES, LANES), axis=0)


def _wkv_body(*refs, nsrc, spb, vs, n_sb, tc, kp, vp):
    vl_n = HS_WKV // vs
    n_in = 6 * nsrc
    k_srcs = [refs[o * nsrc:(o + 1) * nsrc] for o in range(5)]
    v_srcs = refs[5 * nsrc:n_in]
    s0_ref = refs[n_in]
    y_ref = refs[n_in + 1]
    sf_ref = refs[n_in + 2]
    k_scr = refs[n_in + 3:n_in + 8]
    v_scr, y_scr, s_scr, sa_scr = refs[n_in + 8:n_in + 12]
    r_scr, w_scr, k_scr_, kk_scr, b_scr = k_scr
    backward = pl.program_id(0) // n_sb == 1
    seqs = [(s, j) for s in range(nsrc) for j in range(spb)]

    @pl.when(pl.program_id(1) == 0)
    def _():
        s_scr[...] = s0_ref[0]

    def build_k(c, carry):
        row = pl.multiple_of(c * H_WKV, H_WKV)
        for o in range(5):
            slab = [k_srcs[o][s][j, pl.ds(row, H_WKV), :] for s, j in seqs]
            k_scr[o][pl.ds(c, tc, stride=kp), :] = jnp.concatenate(slab * vs, axis=0).T
        return carry

    lax.fori_loop(0, HS_WKV, build_k, 0)

    def build_v(vl, carry):
        slab = []
        for vsi in range(vs):
            row = pl.multiple_of((vsi * vl_n + vl) * H_WKV, H_WKV)
            slab += [v_srcs[s][j, pl.ds(row, H_WKV), :] for s, j in seqs]
        v_scr[pl.ds(vl, tc, stride=vp), :] = jnp.concatenate(slab, axis=0).T
        return carry

    lax.fori_loop(0, vl_n, build_v, 0)

    def step(s, carry):
        t = jnp.where(backward, tc - 1 - s, s)
        krow = pl.multiple_of(t * kp, SUBLANES)
        vrow = pl.multiple_of(t * vp, SUBLANES)
        kslab = pl.ds(krow, HS_WKV)
        for g in range(vl_n // SUBLANES):
            parts = [_fold8(s_scr[g * SUBLANES + i] * kk_scr[kslab, :]) for i in range(SUBLANES)]
            sa_scr[g * SUBLANES:(g + 1) * SUBLANES, :] = _rowsum8(parts)
        for g in range(vl_n // SUBLANES):
            parts = []
            for i in range(SUBLANES):
                vl = g * SUBLANES + i
                sa = sa_scr[vl:vl + 1, :]
                vv = v_scr[pl.ds(vrow + vl, 1), :]
                sn = s_scr[vl] * w_scr[kslab, :] - sa * b_scr[kslab, :] + vv * k_scr_[kslab, :]
                s_scr[vl] = sn
                parts.append(_fold8(sn * r_scr[kslab, :]))
            y_scr[pl.ds(pl.multiple_of(vrow + g * SUBLANES, SUBLANES), SUBLANES), :] = _rowsum8(parts)
        return carry

    lax.fori_loop(0, tc, step, 0)

    def emit_y(vl, carry):
        yt = y_scr[pl.ds(vl, tc, stride=vp), :].T
        for vsi in range(vs):
            row = pl.multiple_of((vsi * vl_n + vl) * H_WKV, H_WKV)
            for n, (s, j) in enumerate(seqs):
                lane0 = (vsi * len(seqs) + n) * H_WKV
                y_ref[0, 0, s * spb + j, pl.ds(row, H_WKV), :] = yt[lane0:lane0 + H_WKV, :]
        return carry

    lax.fori_loop(0, vl_n, emit_y, 0)
    sf_ref[0] = s_scr[...]


def _wkv_scan(rt, wt, kt, kkt, bt, vt, s0, *, tile0, n_seq, seq_tiles, spb):
    tc = WKV_TC
    if spb > 1:
        assert seq_tiles == 1 and n_seq % spb == 0 and tile0 % spb == 0
        nsrc, n_sb = 1, n_seq // spb
    else:
        nsrc, n_sb = n_seq, 1
    inst = nsrc * spb * H_WKV
    vs = LANES // inst
    vl_n = HS_WKV // vs
    assert vl_n % SUBLANES == 0, "value rows are processed eight at a time"
    cpt = TM // tc
    n_chunks = seq_tiles * cpt
    kp = HS_WKV + SUBLANES
    vp = vl_n + SUBLANES if ((vl_n + SUBLANES) // SUBLANES) % 2 else vl_n + 2 * SUBLANES

    def chunk(g, i):
        return jnp.where(g // n_sb == 1, n_chunks - 1 - i, i)

    def in_map(g, i, *, src, per_dir):
        ce = chunk(g, i)
        rb = (g // n_sb) if per_dir else 0
        if spb > 1:
            return (tile0 // spb + g % n_sb, rb, ce)
        return (tile0 + src * seq_tiles + ce // cpt, rb, ce % cpt)

    def out_map(g, i):
        ce = chunk(g, i)
        if spb > 1:
            return (g // n_sb, 0, g % n_sb, 0, ce)
        return (g // n_sb, ce // cpt, 0, 0, ce % cpt)

    in_specs, operands = [], []
    for arr, per_dir in ((rt, False), (wt, True), (kt, True), (kkt, False), (bt, True), (vt, False)):
        for src in range(nsrc):
            in_specs.append(pl.BlockSpec((spb, W_MIX, tc), functools.partial(in_map, src=src, per_dir=per_dir),
                                         pipeline_mode=pl.Buffered(1)))
            operands.append(arr)
    sspec = pl.BlockSpec((1, vl_n, HS_WKV, LANES), lambda g, i: (g, 0, 0, 0))
    in_specs.append(sspec)
    out_specs = [pl.BlockSpec((1, 1, nsrc * spb, W_MIX, tc), out_map)]
    out_shape = [jax.ShapeDtypeStruct((N_DIR, seq_tiles, n_seq, W_MIX, TM), F32)]
    res = pl.pallas_call(
        functools.partial(_wkv_body, nsrc=nsrc, spb=spb, vs=vs, n_sb=n_sb, tc=tc, kp=kp, vp=vp),
        grid=(N_DIR * n_sb, n_chunks),
        in_specs=in_specs,
        out_specs=out_specs + [sspec],
        out_shape=out_shape + [jax.ShapeDtypeStruct(s0.shape, F32)],
        scratch_shapes=([pltpu.VMEM((tc * kp, LANES), F32)] * 5
                        + [pltpu.VMEM((tc * vp, LANES), F32)] * 2
                        + [pltpu.VMEM((vl_n, HS_WKV, LANES), F32), pltpu.VMEM((vl_n, LANES), F32)]),
        compiler_params=_cparams(("parallel", "arbitrary"), VMEM_LIMIT),
        name="wkv_scan",
    )(*operands, s0)
    return res[0], res[1]


def _merge_body(x_ref, mod_ref, n1g_ref, n2g_ref, wg_ref, gb_ref, wbr_ref, wo_ref,
                lnxg_ref, lnxb_ref, ind_ref,
                ya_ref, yd_ref, gbg_ref, h_ref, ycf_ref, ycb_ref, ylf_ref, ylb_ref, g_ref, bon_ref,
                x1_ref, h2_ref, *, n_ctx_tiles):
    x = x_ref[...]
    m = mod_ref[0]
    ind = ind_ref[...]
    h = (_rms(x, n1g_ref[...]) * (1.0 + m[1:2, :]) + m[0:1, :]).astype(BF16)
    y_b = gbg_ref[...] * (h_ref[:, 0:W_MIX] + h_ref[:, W_MIX:2 * W_MIX])
    is_ctx = pl.program_id(0) < n_ctx_tiles
    y = jnp.where(is_ctx, ycf_ref[0, 0, 0] + ycb_ref[0, 0, 0], ylf_ref[0, 0, 0] + ylb_ref[0, 0, 0]).T
    yc = y - _segsum(y, ind) * (1.0 / HS_WKV)
    var = _segsum(yc * yc, ind) * (1.0 / HS_WKV)
    y_c = (yc * lax.rsqrt(var + LNX_EPS) * lnxg_ref[...] + lnxb_ref[...] + bon_ref[...]) * g_ref[...]
    merged = None
    for n, yn in enumerate((ya_ref[...], y_b, y_c, yd_ref[...])):
        cs = slice(n * D_MODEL, (n + 1) * D_MODEL)
        gate = jax.nn.sigmoid(jnp.dot(h, wg_ref[:, cs], preferred_element_type=F32) + gb_ref[:, cs])
        br = jnp.dot(yn.astype(BF16), wbr_ref[n * W_MIX:(n + 1) * W_MIX, :], preferred_element_type=F32)
        merged = gate * br if merged is None else merged + gate * br
    mo = jnp.dot(merged.astype(BF16), wo_ref[...], preferred_element_type=F32)
    x1 = x + m[2:3, :] * mo
    x1_ref[...] = x1
    h2_ref[...] = (_rms(x1, n2g_ref[...]) * (1.0 + m[4:5, :]) + m[3:4, :]).astype(BF16)


def _merge(x, mod, lw, tok_in, y_ctx, y_lat, n_ctx_tok, lat_len):
    t = x.shape[0]
    n_ctx_tiles = n_ctx_tok // TM
    tps = lat_len // TM
    midx = functools.partial(_mod_index, tm=TM, n_ctx_tok=n_ctx_tok, lat_len=lat_len)
    wnames = ["norm1_g", "norm2_g", "w_gate", "gate_b", "w_branch", "w_out", "lnx_g", "lnx_b", "ind"]
    wts = [lw[n] for n in wnames]
    tok = lambda wd: pl.BlockSpec((TM, wd), lambda i: (i, 0))
    ya, yd, gbg, h, g, bon = tok_in
    yblock = (1, 1, 1, W_MIX, TM)

    def ctx_spec(d):
        return pl.BlockSpec(yblock, lambda i: (d, 0, jnp.minimum(i, n_ctx_tiles - 1), 0, 0))

    def lat_spec(d):
        def imap(i):
            r = jnp.maximum(i - n_ctx_tiles, 0)
            return (d, lax.rem(r, tps), lax.div(r, tps), 0, 0)
        return pl.BlockSpec(yblock, imap)

    return pl.pallas_call(
        functools.partial(_merge_body, n_ctx_tiles=n_ctx_tiles),
        grid=(t // TM,),
        in_specs=([tok(D_MODEL), pl.BlockSpec((1, N_MOD, D_MODEL), lambda i: (midx(i), 0, 0))]
                  + [_const_spec(w.shape) for w in wts]
                  + [tok(W_MIX), tok(W_MIX), tok(W_MIX), tok(2 * W_MIX),
                     ctx_spec(0), ctx_spec(1), lat_spec(0), lat_spec(1), tok(W_MIX), tok(W_MIX)]),
        out_specs=[tok(D_MODEL), tok(D_MODEL)],
        out_shape=[jax.ShapeDtypeStruct((t, D_MODEL), F32), jax.ShapeDtypeStruct((t, D_MODEL), BF16)],
        compiler_params=_cparams(("parallel",), VMEM_LIMIT),
        name="merge",
    )(x, mod, *wts, ya, yd, gbg, h, y_ctx, y_ctx, y_lat, y_lat, g, bon)


_CAND_VALID = (8, 8, 8, 5, 4, 3, 2, 2, 2, 8)


def _route_head(qs, keys_ref):
    kio = lax.broadcasted_iota(I32, (N_KEYS, LANES), 0)
    sub = lax.broadcasted_iota(I32, (SUBLANES, LANES), 0)
    kid = lax.broadcasted_iota(I32, (PEER_TOPK, LANES), 0)
    neg = -jnp.inf

    def bc(x, r):
        return jnp.broadcast_to(x[r:r + 1, :], (SUBLANES, LANES))

    def head():
        tops = []
        for p in range(2):
            s = jnp.dot(keys_ref[p], qs[p], preferred_element_type=F32)
            vals = jnp.zeros((PEER_TOPK, LANES), F32)
            idxs = jnp.zeros((PEER_TOPK, LANES), I32)
            for r in range(PEER_TOPK):
                m = jnp.max(s, axis=0, keepdims=True)
                cand = jnp.where(s == m, kio, N_KEYS)
                ix = jnp.min(cand, axis=0, keepdims=True)
                s = jnp.where(cand == ix, neg, s)
                vals = jnp.where(kid == r, m, vals)
                idxs = jnp.where(kid == r, ix, idxs)
            tops.append((vals, idxs))
        (a0, i0), (a1, i1) = tops
        lo, hi = slice(0, SUBLANES), slice(SUBLANES, 2 * SUBLANES)
        slabs = [bc(a0, 0) + a1[lo], bc(a0, 0) + a1[hi]]
        ci = [bc(i0, 0), bc(i0, 0)]
        cj = [i1[lo], i1[hi]]
        for r in range(1, SUBLANES):
            slabs.append(bc(a0, r) + a1[lo])
            ci.append(bc(i0, r))
            cj.append(i1[lo])
        slabs.append(a0[hi] + bc(a1, 0))
        ci.append(i0[hi])
        cj.append(bc(i1, 0))
        slabs = [jnp.where(sub < nv, sl, neg) for sl, nv in zip(slabs, _CAND_VALID)]
        ids = [a * N_KEYS + b for a, b in zip(ci, cj)]
        vals = jnp.zeros((PEER_TOPK, LANES), F32)
        esel = jnp.zeros((PEER_TOPK, LANES), I32)
        for r in range(PEER_TOPK):
            level = list(zip(slabs, ids))
            while len(level) > 1:
                nxt = []
                for (va, ea), (vb, eb) in zip(level[0::2], level[1::2]):
                    take = vb > va
                    nxt.append((jnp.where(take, vb, va), jnp.where(take, eb, ea)))
                if len(level) % 2:
                    nxt.append(level[-1])
                level = nxt
            v8, e8 = level[0]
            for sh in (4, 2, 1):
                vr, er = pltpu.roll(v8, sh, 0), pltpu.roll(e8, sh, 0)
                take = vr > v8
                v8, e8 = jnp.where(take, vr, v8), jnp.where(take, er, e8)
            m, ex = v8[0:1, :], e8[0:1, :]
            slabs = [jnp.where(eid == ex, neg, sl) for sl, eid in zip(slabs, ids)]
            vals = jnp.where(kid == r, m, vals)
            esel = jnp.where(kid == r, ex, esel)
        e = jnp.exp(vals - vals[0:1, :])
        return esel, e / jnp.sum(e, axis=0, keepdims=True)

    return head()


def _peer_body(h2_ref, h2n_ref, wqt_ref, keys_ref, u_ref, v_ref, x1_ref, mod_ref, fng_ref, o_ref,
               q_scr, e_scr, g_scr, et_scr, gt_scr, gs_scr, acc_scr, *, rows, pitch, units, final):
    m = pl.program_id(0)
    e = pl.program_id(1)
    tm = h2_ref.shape[0]
    n_chunks = tm // LANES
    nsel = PEER_HEADS * PEER_TOPK
    slot = lax.rem(m, 2)

    def project_queries(src_ref):
        q = lax.dot_general(wqt_ref[...], src_ref[...], _NT, preferred_element_type=F32).astype(BF16)
        for c in range(n_chunks):
            q_scr[c] = q[:, c * LANES:(c + 1) * LANES]

    def route_unit(u, dst):
        c = u // PEER_HEADS
        h = lax.rem(u, PEER_HEADS)
        qs = [q_scr[c, pl.ds(pl.multiple_of(h * (2 * N_KEYS) + p * N_KEYS, N_KEYS), N_KEYS), :]
              for p in range(2)]
        esel, gates = _route_head(qs, keys_ref)
        row = pl.multiple_of(h * PEER_TOPK, PEER_TOPK)
        e_scr[dst, c, pl.ds(row, PEER_TOPK), :] = esel
        g_scr[dst, c, pl.ds(row, PEER_TOPK), :] = gates

    @pl.when(jnp.logical_and(e == 0, m == 0))
    def _first_tile_routing():
        project_queries(h2_ref)

        def unit(u, c):
            route_unit(u, 0)
            return c

        lax.fori_loop(0, n_chunks * PEER_HEADS, unit, 0)

    @pl.when(e == 0)
    def _build():
        for c in range(n_chunks):
            et_scr[c * LANES:(c + 1) * LANES, :] = e_scr[slot, c].T
            gt_scr[c * LANES:(c + 1) * LANES, :] = g_scr[slot, c].T
        kio = lax.broadcasted_iota(I32, (N_KEYS, nsel), 0)

        def tok(t, c):
            erow = et_scr[pl.ds(t, 1), :]
            grow = gt_scr[pl.ds(t, 1), :]
            at = jnp.where(kio == (erow >> 7), grow, 0.0).astype(BF16)
            bt = jnp.where(kio == (erow & (N_KEYS - 1)), 1.0, 0.0).astype(BF16)
            gt = lax.dot_general(at, bt, _NT, preferred_element_type=F32)
            hi = pltpu.bitcast(gt[0:rows, :].astype(BF16).astype(F32), jnp.uint32)
            lo = pltpu.bitcast(gt[rows:2 * rows, :].astype(BF16).astype(F32), jnp.uint32)
            gs_scr[pl.ds(pl.multiple_of(t * pitch, SUBLANES), rows), :] = hi | (lo >> 16)
            return c

        lax.fori_loop(0, tm, tok, 0, unroll=16)
        acc_scr[...] = jnp.zeros_like(acc_scr)
        project_queries(h2n_ref)

    for k in range(units):
        route_unit(e * units + k, 1 - slot)

    per_sub = PEER_SUB // N_KEYS
    per_step = u_ref.shape[0] // N_KEYS
    steps_per_half = rows // per_step
    row0 = lax.rem(e, steps_per_half) * per_step
    shift = jnp.where(e < steps_per_half, 0, 16).astype(jnp.uint32)
    h2 = h2_ref[...]
    total = None
    for sb in range(per_step // per_sub):
        es = slice(sb * PEER_SUB, (sb + 1) * PEER_SUB)
        hmat = lax.dot_general(h2, u_ref[es, :], _NT, preferred_element_type=F32)
        words = jnp.concatenate(
            [gs_scr[pl.ds(row0 + sb * per_sub + ii, tm, stride=pitch), :] for ii in range(per_sub)], axis=1)
        gm = pltpu.bitcast((words << shift) & jnp.uint32(0xFFFF0000), F32)
        act = (jax.nn.gelu(hmat) * gm).astype(BF16)
        part = jnp.dot(act, v_ref[es, :], preferred_element_type=F32)
        total = part if total is None else total + part
    acc_scr[...] += total

    @pl.when(e == pl.num_programs(1) - 1)
    def _out():
        x2 = x1_ref[...] + mod_ref[0][5:6, :] * acc_scr[...]
        o_ref[...] = _rms(x2, fng_ref[...]) if final else x2


def _peer(h2, wqt, keys, u, v, x1, mod, fng, n_ctx_tok, lat_len, final):
    t = h2.shape[0]
    nsel = PEER_HEADS * PEER_TOPK
    rows = N_KEYS // 2
    pitch = rows + SUBLANES
    n_e = (N_KEYS * N_KEYS) // PEER_EB
    n_m = t // TM_PEER
    n_chunks = TM_PEER // LANES
    units = (n_chunks * PEER_HEADS) // n_e
    assert units * n_e == n_chunks * PEER_HEADS
    midx = functools.partial(_mod_index, tm=TM_PEER, n_ctx_tok=n_ctx_tok, lat_len=lat_len)
    tok = lambda wd: pl.BlockSpec((TM_PEER, wd), lambda m, e: (m, 0))
    nxt = pl.BlockSpec((TM_PEER, D_MODEL), lambda m, e: (jnp.minimum(m + 1, n_m - 1), 0))
    espec = pl.BlockSpec((PEER_EB, D_MODEL), lambda m, e: (e, 0))
    single = dict(pipeline_mode=pl.Buffered(1))
    return pl.pallas_call(
        functools.partial(_peer_body, rows=rows, pitch=pitch, units=units, final=final),
        grid=(n_m, n_e),
        in_specs=[tok(D_MODEL), nxt,
                  pl.BlockSpec(wqt.shape, lambda m, e: (0, 0), **single),
                  pl.BlockSpec(keys.shape, lambda m, e: (0, 0, 0), **single),
                  espec, espec, tok(D_MODEL),
                  pl.BlockSpec((1, N_MOD, D_MODEL), lambda m, e: (midx(m), 0, 0)),
                  _const_spec((1, D_MODEL))],
        out_specs=tok(D_MODEL),
        out_shape=jax.ShapeDtypeStruct((t, D_MODEL), F32),
        scratch_shapes=[pltpu.VMEM((n_chunks, wqt.shape[0], LANES), BF16),
                        pltpu.VMEM((2, n_chunks, nsel, LANES), I32),
                        pltpu.VMEM((2, n_chunks, nsel, LANES), F32),
                        pltpu.VMEM((TM_PEER, nsel), I32),
                        pltpu.VMEM((TM_PEER, nsel), F32),
                        pltpu.VMEM((TM_PEER * pitch, N_KEYS), jnp.uint32),
                        pltpu.VMEM((TM_PEER, D_MODEL), F32)],
        compiler_params=_cparams(("arbitrary", "arbitrary"), VMEM_LIMIT),
        name="peer",
    )(h2, h2, wqt, keys, u, v, x1, mod, fng)


def _wkv_state_in(s, vs):
    n = s.shape[0]
    vl = HS_WKV // vs
    s = s.reshape(n, N_DIR, H_WKV, vs, vl, HS_WKV).transpose(1, 4, 5, 3, 0, 2)
    return s.reshape(N_DIR, vl, HS_WKV, vs * n * H_WKV)


def _wkv_state_out(s, n_sb, spb, vs):
    vl = HS_WKV // vs
    s = s.reshape(N_DIR, n_sb, vl, HS_WKV, vs, spb, H_WKV).transpose(1, 5, 0, 6, 4, 2, 3)
    return s.reshape(n_sb * spb, N_DIR, H_WKV, HS_WKV, HS_WKV)


def _layer_weights(i, prm):
    eye_h = jnp.eye(H_LRU, dtype=F32)
    eye_d = jnp.eye(N_DIR, dtype=F32)
    perm = _WKV_PERM

    def lru_bd(wt):
        return jnp.einsum("dhij,hg->hidgj", wt, eye_h).reshape(W_MIX, N_DIR * W_MIX)

    def lora_bd(wt):
        r = wt.shape[1]
        return jnp.einsum("drc,de->drec", wt, eye_d).reshape(N_DIR * r, N_DIR * W_MIX)

    w_in = prm["w_in"][i]
    pad = jnp.zeros((D_MODEL, Z_COLS - 5504), F32)
    rkv = [w_in[:, 2560 + j * W_MIX:2560 + (j + 1) * W_MIX][:, perm] for j in range(3)]
    w_in_perm = jnp.concatenate(
        [w_in[:, 0:1536]] + rkv + [w_in[:, 1536:2560], w_in[:, 4480:5504], w_in[:, 4096:4480], pad],
        axis=1).astype(BF16)
    row = lambda x: x.reshape(1, -1).astype(F32)
    head_of = np.arange(W_MIX) % H_WKV
    w_branch = prm["w_branch"][i]
    w_branch = jnp.concatenate([w_branch[0], w_branch[1], w_branch[2][perm, :], w_branch[3]], axis=0)
    return {
        "w_in": w_in_perm,
        "w_gate": w_in[:, 5504:].astype(BF16),
        "norm1_g": row(prm["norm1_g"][i]),
        "norm2_g": row(prm["norm2_g"][i]),
        "conv_a_w": prm["conv_a_w"][i],
        "conv_b_w": prm["conv_b_w"][i],
        "conv_b_b": row(prm["conv_b_b"][i]),
        "lru_w": jnp.concatenate([lru_bd(prm["lru_wa"][i]), lru_bd(prm["lru_wx"][i])], axis=1).astype(BF16),
        "lru_b": jnp.concatenate([row(prm["lru_ba"][i]), row(prm["lru_bx"][i])], axis=1),
        "lru_lam": row(prm["lru_lambda"][i]),
        "w0": row(prm["rwkv_w0"][i][:, perm]),
        "w2": lora_bd(prm["rwkv_w2"][i][:, :, perm]).astype(BF16),
        "a0": row(prm["rwkv_a0"][i][:, perm]),
        "a2": lora_bd(prm["rwkv_a2"][i][:, :, perm]).astype(BF16),
        "g2": prm["rwkv_g2"][i][:, perm].astype(BF16),
        "kk": row(prm["rwkv_kk"][i][perm]),
        "ka": row(prm["rwkv_ka"][i][perm]),
        "rk": row(prm["rwkv_rk"][i].reshape(W_MIX)[perm]),
        "lnx_g": row(prm["lnx_g"][i][perm]),
        "lnx_b": row(prm["lnx_b"][i][perm]),
        "sg_ln_g": row(prm["sg_ln_g"][i]),
        "sg_ln_b": row(prm["sg_ln_b"][i]),
        "sg_ws": prm["sg_ws"][i].astype(BF16),
        "sg_bst": prm["sg_bs"][i].T,
        "gate_b": row(prm["gate_b"][i]),
        "w_branch": w_branch.astype(BF16),
        "w_out": prm["w_out"][i].astype(BF16),
        "wq_t": prm["peer_wq"][i].T.astype(BF16),
        "keys": prm["peer_keys"][i].astype(BF16),
        "peer_u": prm["peer_u"][i].astype(BF16),
        "peer_v": prm["peer_v"][i].astype(BF16),
        "ind": jnp.asarray(head_of[:, None] == head_of[None, :], BF16),
    }


def kernel(x_prompt, x_sample, state_lru, state_wkv, c, c_ctx, norm1_g, norm2_g, w_mod, b_mod, w_in, conv_a_w, conv_b_w, conv_b_b, lru_wa, lru_ba, lru_wx, lru_bx, lru_lambda, rwkv_w0, rwkv_w2, rwkv_a0, rwkv_a2, rwkv_g2, rwkv_kk, rwkv_ka, rwkv_rk, lnx_g, lnx_b, sg_ln_g, sg_ln_b, sg_ws, sg_bs, gate_b, w_branch, w_out, peer_wq, peer_keys, peer_u, peer_v, final_norm_g):
    prm = dict(norm1_g=norm1_g, norm2_g=norm2_g, w_in=w_in, conv_a_w=conv_a_w, conv_b_w=conv_b_w,
               conv_b_b=conv_b_b, lru_wa=lru_wa, lru_ba=lru_ba, lru_wx=lru_wx, lru_bx=lru_bx,
               lru_lambda=lru_lambda, rwkv_w0=rwkv_w0, rwkv_w2=rwkv_w2, rwkv_a0=rwkv_a0, rwkv_a2=rwkv_a2,
               rwkv_g2=rwkv_g2, rwkv_kk=rwkv_kk, rwkv_ka=rwkv_ka, rwkv_rk=rwkv_rk, lnx_g=lnx_g,
               lnx_b=lnx_b, sg_ln_g=sg_ln_g, sg_ln_b=sg_ln_b, sg_ws=sg_ws, sg_bs=sg_bs, gate_b=gate_b,
               w_branch=w_branch, w_out=w_out, peer_wq=peer_wq, peer_keys=peer_keys, peer_u=peer_u,
               peer_v=peer_v)
    bc, lc, _ = x_prompt.shape
    bl, ll, _ = x_sample.shape
    depth = w_mod.shape[0]
    n_ctx_tok = bc * lc
    n_ctx_tiles = n_ctx_tok // TM
    lat_tiles = ll // TM
    ctx_spb = min(WKV_CTX_SPB, bc)
    lru_spb = ll // lc
    assert lc == TM and ll % TM_IN == 0 and n_ctx_tok % TM_IN == 0 and bl + 1 <= SUBLANES
    assert ll % GRID_W == 0 and TM % GRID_W == 0 and bc % ctx_spb == 0 and n_ctx_tok % ll == 0
    assert LANES % (ctx_spb * H_WKV) == 0 and LANES % (bl * H_WKV) == 0

    cond = jnp.zeros((SUBLANES, D_MODEL), F32).at[0].set(c_ctx).at[1:1 + bl].set(c)
    mods = _modulation(cond, w_mod, b_mod).reshape(depth, SUBLANES, N_MOD, D_MODEL)
    fng = final_norm_g.reshape(1, D_MODEL)
    x = jnp.concatenate([x_prompt.reshape(n_ctx_tok, D_MODEL), x_sample.reshape(bl * ll, D_MODEL)], axis=0)
    ctx_vs = LANES // (ctx_spb * H_WKV)
    lat_vs = LANES // (bl * H_WKV)
    n_sb = bc // ctx_spb
    wkv_zero = jnp.zeros((N_DIR * n_sb, HS_WKV // ctx_vs, HS_WKV, LANES), F32)
    lru_zero = jnp.zeros((n_ctx_tok // ll, lru_spb, N_DIR * W_MIX), F32)
    new_lru, new_wkv = [], []
    pnames = ["ya", "yd", "gbg", "la", "lu", "g", "bon", "rt", "vt", "kkt", "wt", "kt", "bt"]
    for i in range(depth):
        lw = _layer_weights(i, prm)
        mod = mods[i]
        z = _in_proj(x, mod, lw["norm1_g"], lw["w_in"], n_ctx_tok, ll)
        p = dict(zip(pnames, _prep(z, lw, n_ctx_tok, ll)))

        lat_h0 = jnp.zeros((bl, lru_spb, N_DIR * W_MIX), F32).at[:, 0].set(
            state_lru[:, i].astype(F32).reshape(bl, N_DIR * W_MIX))
        h, lru_s = _lru_scan(p["la"], p["lu"], jnp.concatenate([lru_zero, lat_h0], axis=0),
                             rows=ll, n_ctx_blocks=n_ctx_tok // ll, ctx_cfg=(lru_spb, lc), lat_cfg=(1, ll))
        new_lru.append(lru_s[:n_ctx_tok // ll].reshape(bc, N_DIR, W_MIX))

        wkv_in = [p[n] for n in ("rt", "wt", "kt", "kkt", "bt", "vt")]
        y_c, s_c = _wkv_scan(*wkv_in, wkv_zero, tile0=0, n_seq=bc, seq_tiles=1, spb=ctx_spb)
        y_l, _ = _wkv_scan(*wkv_in, _wkv_state_in(state_wkv[:, i].astype(F32), lat_vs),
                           tile0=n_ctx_tiles, n_seq=bl, seq_tiles=lat_tiles, spb=1)
        new_wkv.append(_wkv_state_out(s_c, n_sb, ctx_spb, ctx_vs))

        tok_in = [p["ya"], p["yd"], p["gbg"], h, p["g"], p["bon"]]
        x1, h2 = _merge(x, mod, lw, tok_in, y_c, y_l, n_ctx_tok, ll)
        x = _peer(h2, lw["wq_t"], lw["keys"], lw["peer_u"], lw["peer_v"], x1, mod, fng,
                        n_ctx_tok, ll, final=(i == depth - 1))
    y_prompt = x[:n_ctx_tok].reshape(bc, lc, D_MODEL)
    y_sample = x[n_ctx_tok:].reshape(bl, ll, D_MODEL)
    return (y_prompt, y_sample, jnp.stack(new_lru, axis=1), jnp.stack(new_wkv, axis=1))
```

```python
import functools

import numpy as np
import jax
import jax.numpy as jnp
from jax import lax
from jax.experimental import pallas as pl
from jax.experimental.pallas import tpu as pltpu

F32 = jnp.float32
BF16 = jnp.bfloat16
I32 = jnp.int32

D_MODEL = 1024
W_MIX = 512
N_DIR = 2
N_BRANCH = 4
H_WKV = 8
HS_WKV = 64
H_LRU = 8
HB_LRU = 64
LORA_W = 64
LORA_A = 64
LORA_G = 128
GRID_W = 64
CHUNK = 128
G_SG = 4
N_KEYS = 128
PEER_HEADS = 8
PEER_TOPK = 16
N_MOD = 6
EPS = 1e-6
LNX_EPS = 64e-5
LRU_C = 8.0

LANES = 128
SUBLANES = 8
TM = 256
TM_IN = 1024
TN_IN = 1408
TM_PEER = 512
PEER_EB = 1024
PEER_SUB = 512
Z_COLS = 5632
WKV_TC = LANES
WKV_CTX_SPB = 8
VMEM_LIMIT = 56 * 1024 * 1024

_NT = (((1,), (1,)), ((), ()))
_WKV_PERM = np.array([(n % H_WKV) * HS_WKV + n // H_WKV for n in range(W_MIX)])


def _cparams(sem, vmem=None):
    return pltpu.CompilerParams(dimension_semantics=sem, vmem_limit_bytes=vmem)


def _const_spec(shape):
    nd = len(shape)
    return pl.BlockSpec(shape, lambda *_: (0,) * nd)


def _softplus(x):
    return jnp.maximum(x, 0.0) + jnp.log1p(jnp.exp(-jnp.abs(x)))


def _rms(x, g):
    return x * lax.rsqrt(jnp.mean(x * x, axis=-1, keepdims=True) + EPS) * g


def _segsum(x, ind):
    hi = x.astype(BF16)
    lo = (x - hi.astype(F32)).astype(BF16)
    return (jnp.dot(hi, ind, preferred_element_type=F32)
            + jnp.dot(lo, ind, preferred_element_type=F32))


def _mod_index(i, tm, n_ctx_tok, lat_len):
    n_ctx_tiles = n_ctx_tok // tm
    tiles_per_seq = lat_len // tm
    return jnp.where(i < n_ctx_tiles, 0, 1 + lax.div(i - n_ctx_tiles, tiles_per_seq))


def _mod_body(s_ref, w_ref, b_ref, o_ref):
    s = s_ref[...]
    s = s * jax.nn.sigmoid(s)
    o_ref[0] = jnp.dot(s.astype(BF16), w_ref[0].astype(BF16), preferred_element_type=F32) + b_ref[0]


def _modulation(cond, w_mod, b_mod):
    depth = w_mod.shape[0]
    n = w_mod.shape[2]
    tn = 1536
    return pl.pallas_call(
        _mod_body,
        grid=(depth, n // tn),
        in_specs=[_const_spec((SUBLANES, D_MODEL)),
                  pl.BlockSpec((1, D_MODEL, tn), lambda l, j: (l, 0, j)),
                  pl.BlockSpec((1, 1, tn), lambda l, j: (l, 0, j))],
        out_specs=pl.BlockSpec((1, SUBLANES, tn), lambda l, j: (l, 0, j)),
        out_shape=jax.ShapeDtypeStruct((depth, SUBLANES, n), F32),
        compiler_params=_cparams(("parallel", "parallel"), VMEM_LIMIT),
        name="modulation",
    )(cond, w_mod, b_mod.reshape(depth, 1, n))


def _in_body(x_ref, mod_ref, g_ref, w_ref, o_ref, h_scr):
    @pl.when(pl.program_id(1) == 0)
    def _():
        m = mod_ref[0]
        y = _rms(x_ref[...], g_ref[...])
        h_scr[...] = (y * (1.0 + m[1:2, :]) + m[0:1, :]).astype(BF16)

    o_ref[...] = jnp.dot(h_scr[...], w_ref[...], preferred_element_type=F32)


def _in_proj(x, mod, g, w, n_ctx_tok, lat_len):
    t = x.shape[0]
    midx = functools.partial(_mod_index, tm=TM_IN, n_ctx_tok=n_ctx_tok, lat_len=lat_len)
    return pl.pallas_call(
        _in_body,
        grid=(t // TM_IN, Z_COLS // TN_IN),
        in_specs=[pl.BlockSpec((TM_IN, D_MODEL), lambda i, j: (i, 0)),
                  pl.BlockSpec((1, N_MOD, D_MODEL), lambda i, j: (midx(i), 0, 0)),
                  _const_spec((1, D_MODEL)),
                  pl.BlockSpec((D_MODEL, TN_IN), lambda i, j: (0, j))],
        out_specs=pl.BlockSpec((TM_IN, TN_IN), lambda i, j: (i, j)),
        out_shape=jax.ShapeDtypeStruct((t, Z_COLS), F32),
        scratch_shapes=[pltpu.VMEM((TM_IN, D_MODEL), BF16)],
        compiler_params=_cparams(("parallel", "arbitrary"), VMEM_LIMIT),
        name="in_proj",
    )(x, mod, g, w)


def _prep_body(za_ref, zc_ref, zb_ref, zd_ref, zl_ref, hp_ref, hn_ref,
               caw_ref, cbw_ref, cbb_ref, lruw_ref, lrub_ref, lam_ref,
               w0_ref, w2_ref, a0_ref, a2_ref, g2_ref, kkw_ref, ka_ref, rk_ref,
               lng_ref, lnb_ref, ws_ref, bst_ref, ind_ref,
               ya_ref, yd_ref, gbg_ref, la_ref, lu_ref, g_ref, bon_ref,
               rt_ref, vt_ref, kkt_ref, wt_ref, kt_ref, bt_ref,
               *, n_ctx_tiles, tiles_per_seq):
    i = pl.program_id(0)
    is_ctx = i < n_ctx_tiles
    t = lax.broadcasted_iota(I32, (TM, 1), 0)
    ind = ind_ref[...]

    pm = jnp.where(is_ctx, TM - 1, GRID_W - 1)
    pos = t & pm
    a_b = za_ref[:, 0:W_MIX]
    ac = za_ref[:, W_MIX:2 * W_MIX] * za_ref[:, 2 * W_MIX:3 * W_MIX]
    up = jnp.where(pos == 0, 0.0, pltpu.roll(ac, 1, 0))
    dn = jnp.where(pos == pm, 0.0, pltpu.roll(ac, TM - 1, 0))
    ya_ref[...] = a_b * (caw_ref[0:1, :] * up + caw_ref[1:2, :] * ac + caw_ref[2:3, :] * dn)

    seq_tile = lax.rem(jnp.maximum(i - n_ctx_tiles, 0), tiles_per_seq)
    first = jnp.logical_or(is_ctx, seq_tile == 0)
    last = jnp.logical_or(is_ctx, seq_tile == tiles_per_seq - 1)
    prev = jnp.where(first, 0.0, hp_ref[SUBLANES - 1:SUBLANES, :])
    nxt0 = jnp.where(last, 0.0, hn_ref[0:1, :])
    nxt1 = jnp.where(last, 0.0, hn_ref[1:2, :])
    bx = zb_ref[:, W_MIX:2 * W_MIX]
    m1 = jnp.where(t == 0, prev, pltpu.roll(bx, 1, 0))
    p1 = jnp.where(t == TM - 1, nxt0, pltpu.roll(bx, TM - 1, 0))
    p2 = jnp.where(t == TM - 2, nxt0, jnp.where(t == TM - 1, nxt1, pltpu.roll(bx, TM - 2, 0)))
    xb = (cbw_ref[0:1, :] * m1 + cbw_ref[1:2, :] * bx + cbw_ref[2:3, :] * p1
          + cbw_ref[3:4, :] * p2 + cbb_ref[...])
    gates = jnp.dot(xb.astype(BF16), lruw_ref[...], preferred_element_type=F32) + lrub_ref[...]
    rg = jax.nn.sigmoid(gates[:, 0:2 * W_MIX])
    ig = jax.nn.sigmoid(gates[:, 2 * W_MIX:4 * W_MIX])
    log_a = -LRU_C * rg * _softplus(-lam_ref[...])
    xb2 = jnp.concatenate([xb, xb], axis=1)
    la_ref[...] = jnp.exp(log_a)
    lu_ref[...] = jnp.sqrt(jnp.tanh(-log_a) * (jnp.exp(2.0 * log_a) + 1.0)) * (ig * xb2)
    gbg_ref[...] = jax.nn.gelu(zb_ref[:, 0:W_MIX])

    zr = zc_ref[:, 0:W_MIX]
    zk = zc_ref[:, W_MIX:2 * W_MIX]
    zv = zc_ref[:, 2 * W_MIX:3 * W_MIX]
    zwd = zl_ref[:, 0:2 * LORA_W]
    zad = zl_ref[:, 2 * LORA_W:2 * LORA_W + 2 * LORA_A]
    zgd = zl_ref[:, 2 * LORA_W + 2 * LORA_A:2 * LORA_W + 2 * LORA_A + LORA_G]
    wlin = w0_ref[...] + jnp.dot(jnp.tanh(zwd).astype(BF16), w2_ref[...], preferred_element_type=F32)
    wt_ref[0] = jnp.exp(-jnp.exp(-_softplus(-wlin) - 0.5)).T
    av = jax.nn.sigmoid(a0_ref[...] + jnp.dot(zad.astype(BF16), a2_ref[...], preferred_element_type=F32))
    g_ref[...] = jnp.dot(jax.nn.sigmoid(zgd).astype(BF16), g2_ref[...], preferred_element_type=F32)
    kkr = zk * kkw_ref[...]
    kkn = kkr / jnp.maximum(jnp.sqrt(_segsum(kkr * kkr, ind)), 1e-12)
    zk2 = jnp.concatenate([zk, zk], axis=1)
    ka2 = jnp.concatenate([ka_ref[...], ka_ref[...]], axis=1)
    kd = zk2 * (1.0 + (av - 1.0) * ka2)
    kt_ref[0] = kd.T
    bt_ref[0] = (jnp.concatenate([kkn, kkn], axis=1) * av).T
    rt_ref[0] = zr.T
    vt_ref[0] = zv.T
    kkt_ref[0] = kkn.T
    bon_ref[...] = _segsum(zr * (kd[:, 0:W_MIX] + kd[:, W_MIX:2 * W_MIX]) * rk_ref[...], ind) * zv

    zg = jax.nn.gelu(zd_ref[...])
    u = zg[:, 0:W_MIX]
    vv = zg[:, W_MIX:2 * W_MIX]
    vc = vv - jnp.mean(vv, axis=-1, keepdims=True)
    vn = vc * lax.rsqrt(jnp.mean(vc * vc, axis=-1, keepdims=True) + 1e-5) * lng_ref[...] + lnb_ref[...]
    for c in range(TM // CHUNK):
        rs = slice(c * CHUNK, (c + 1) * CHUNK)
        for gi in range(G_SG):
            cs = slice(gi * LANES, (gi + 1) * LANES)
            s = jnp.dot(ws_ref[gi], vn[rs, cs].astype(BF16), preferred_element_type=F32)
            yd_ref[rs, cs] = u[rs, cs] * (s + bst_ref[:, gi:gi + 1])


def _prep(z, lw, n_ctx_tok, lat_len):
    t = z.shape[0]
    n_tiles = t // TM
    n_ctx_tiles = n_ctx_tok // TM
    tiles_per_seq = lat_len // TM
    rows8 = TM // SUBLANES
    last_blk = t // SUBLANES - 1
    bx_blk = (3072 + W_MIX) // W_MIX
    z_specs = [
        pl.BlockSpec((TM, 1536), lambda i: (i, 0)),
        pl.BlockSpec((TM, 1536), lambda i: (i, 1)),
        pl.BlockSpec((TM, 1024), lambda i: (i, 3)),
        pl.BlockSpec((TM, 1024), lambda i: (i, 4)),
        pl.BlockSpec((TM, 512), lambda i: (i, 10)),
        pl.BlockSpec((SUBLANES, W_MIX), lambda i: (jnp.maximum(i * rows8 - 1, 0), bx_blk)),
        pl.BlockSpec((SUBLANES, W_MIX), lambda i: (jnp.minimum((i + 1) * rows8, last_blk), bx_blk)),
    ]
    wnames = ["conv_a_w", "conv_b_w", "conv_b_b", "lru_w", "lru_b", "lru_lam", "w0", "w2", "a0", "a2",
              "g2", "kk", "ka", "rk", "sg_ln_g", "sg_ln_b", "sg_ws", "sg_bst", "ind"]
    wts = [lw[n] for n in wnames]
    w_specs = [_const_spec(w.shape) for w in wts]
    widths = [W_MIX, W_MIX, W_MIX, 2 * W_MIX, 2 * W_MIX, W_MIX, W_MIX]
    t_rows = [W_MIX, W_MIX, W_MIX, 2 * W_MIX, 2 * W_MIX, 2 * W_MIX]
    out_specs = ([pl.BlockSpec((TM, wd), lambda i: (i, 0)) for wd in widths]
                 + [pl.BlockSpec((1, r, TM), lambda i: (i, 0, 0)) for r in t_rows])
    out_shape = ([jax.ShapeDtypeStruct((t, wd), F32) for wd in widths]
                 + [jax.ShapeDtypeStruct((n_tiles, r, TM), F32) for r in t_rows])
    return pl.pallas_call(
        functools.partial(_prep_body, n_ctx_tiles=n_ctx_tiles, tiles_per_seq=tiles_per_seq),
        grid=(n_tiles,),
        in_specs=z_specs + w_specs,
        out_specs=out_specs,
        out_shape=out_shape,
        compiler_params=_cparams(("parallel",), VMEM_LIMIT),
        name="branch_prep",
    )(z, z, z, z, z, z, z, *wts)


def _lru_body(a_ref, u_ref, h0_ref, h_ref, hf_ref, *, n_ctx_blocks, ctx_cfg, lat_cfg):
    fw, bw = slice(0, W_MIX), slice(W_MIX, 2 * W_MIX)

    def scan(nseq, l):
        def step(s, carry):
            out = []
            for j in range(nseq):
                tf = j * l + s
                tb = j * l + (l - 1 - s)
                hf = a_ref[pl.ds(tf, 1), fw] * carry[2 * j] + u_ref[pl.ds(tf, 1), fw]
                hb = a_ref[pl.ds(tb, 1), bw] * carry[2 * j + 1] + u_ref[pl.ds(tb, 1), bw]
                h_ref[pl.ds(tf, 1), fw] = hf
                h_ref[pl.ds(tb, 1), bw] = hb
                out += [hf, hb]
            return tuple(out)

        init = []
        for j in range(nseq):
            init += [h0_ref[0, j:j + 1, fw], h0_ref[0, j:j + 1, bw]]
        fin = lax.fori_loop(0, l, step, tuple(init), unroll=2)
        hf_ref[0] = h0_ref[0]
        for j in range(nseq):
            hf_ref[0, j:j + 1, fw] = fin[2 * j]
            hf_ref[0, j:j + 1, bw] = fin[2 * j + 1]

    is_ctx = pl.program_id(0) < n_ctx_blocks
    pl.when(is_ctx)(lambda: scan(*ctx_cfg))
    pl.when(jnp.logical_not(is_ctx))(lambda: scan(*lat_cfg))


def _lru_scan(a, u, h0, *, rows, n_ctx_blocks, ctx_cfg, lat_cfg):
    nb = a.shape[0] // rows
    w = a.shape[1]
    tok = pl.BlockSpec((rows, w), lambda i: (i, 0))
    st = pl.BlockSpec((1,) + h0.shape[1:], lambda i: (i, 0, 0))
    return pl.pallas_call(
        functools.partial(_lru_body, n_ctx_blocks=n_ctx_blocks, ctx_cfg=ctx_cfg, lat_cfg=lat_cfg),
        grid=(nb,),
        in_specs=[tok, tok, st],
        out_specs=[tok, st],
        out_shape=[jax.ShapeDtypeStruct(a.shape, F32), jax.ShapeDtypeStruct(h0.shape, F32)],
        compiler_params=_cparams(("parallel",), VMEM_LIMIT),
        name="lru_scan",
    )(a, u, h0)


_SLOT_ORDER = (0, 4, 2, 6, 1, 5, 3, 7)


def _rowsum8(parts):
    sub = lax.broadcasted_iota(I32, (SUBLANES, LANES), 0)
    slots = [parts[i] for i in _SLOT_ORDER]
    roll = pltpu.roll
    lvl1 = [jnp.where(sub < 4, a + roll(a, 4, 0), b + roll(b, 4, 0))
            for a, b in zip(slots[0::2], slots[1::2])]
    lvl2 = [jnp.where((sub & 3) < 2, a + roll(a, 6, 0), roll(b + roll(b, 6, 0), 2, 0))
            for a, b in zip(lvl1[0::2], lvl1[1::2])]
    a, b = lvl2
    return jnp.where((sub & 1) == 0, a + roll(a, 7, 0), roll(b + roll(b, 7, 0), 1, 0))


def _fold8(x):
    return jnp.sum(x.reshape(HS_WKV // SUBLANES, SUBLANES, LANES), axis=0)


def _wkv_body(*refs, nsrc, spb, vs, n_sb, tc, kp, vp):
    vl_n = HS_WKV // vs
    n_in = 6 * nsrc
    k_srcs = [refs[o * nsrc:(o + 1) * nsrc] for o in range(5)]
    v_srcs = refs[5 * nsrc:n_in]
    s0_ref = refs[n_in]
    y_ref = refs[n_in + 1]
    sf_ref = refs[n_in + 2]
    k_scr = refs[n_in + 3:n_in + 8]
    v_scr, y_scr, s_scr, sa_scr = refs[n_in + 8:n_in + 12]
    r_scr, w_scr, k_scr_, kk_scr, b_scr = k_scr
    backward = pl.program_id(0) // n_sb == 1
    seqs = [(s, j) for s in range(nsrc) for j in range(spb)]

    @pl.when(pl.program_id(1) == 0)
    def _():
        s_scr[...] = s0_ref[0]

    def build_k(c, carry):
        row = pl.multiple_of(c * H_WKV, H_WKV)
        for o in range(5):
            slab = [k_srcs[o][s][j, pl.ds(row, H_WKV), :] for s, j in seqs]
            k_scr[o][pl.ds(c, tc, stride=kp), :] = jnp.concatenate(slab * vs, axis=0).T
        return carry

    lax.fori_loop(0, HS_WKV, build_k, 0, unroll=4)

    def build_v(vl, carry):
        slab = []
        for vsi in range(vs):
            row = pl.multiple_of((vsi * vl_n + vl) * H_WKV, H_WKV)
            slab += [v_srcs[s][j, pl.ds(row, H_WKV), :] for s, j in seqs]
        v_scr[pl.ds(vl, tc, stride=vp), :] = jnp.concatenate(slab, axis=0).T
        return carry

    lax.fori_loop(0, vl_n, build_v, 0, unroll=4)

    def step(s, carry):
        t = jnp.where(backward, tc - 1 - s, s)
        krow = pl.multiple_of(t * kp, SUBLANES)
        vrow = pl.multiple_of(t * vp, SUBLANES)
        kslab = pl.ds(krow, HS_WKV)
        for g in range(vl_n // SUBLANES):
            parts = [_fold8(s_scr[g * SUBLANES + i] * kk_scr[kslab, :]) for i in range(SUBLANES)]
            sa_scr[g * SUBLANES:(g + 1) * SUBLANES, :] = _rowsum8(parts)
        for g in range(vl_n // SUBLANES):
            parts = []
            for i in range(SUBLANES):
                vl = g * SUBLANES + i
                sa = sa_scr[vl:vl + 1, :]
                vv = v_scr[pl.ds(vrow + vl, 1), :]
                sn = s_scr[vl] * w_scr[kslab, :] - sa * b_scr[kslab, :] + vv * k_scr_[kslab, :]
                s_scr[vl] = sn
                parts.append(_fold8(sn * r_scr[kslab, :]))
            y_scr[pl.ds(pl.multiple_of(vrow + g * SUBLANES, SUBLANES), SUBLANES), :] = _rowsum8(parts)
        return carry

    lax.fori_loop(0, tc, step, 0)

    def emit_y(vl, carry):
        yt = y_scr[pl.ds(vl, tc, stride=vp), :].T
        for vsi in range(vs):
            row = pl.multiple_of((vsi * vl_n + vl) * H_WKV, H_WKV)
            for n, (s, j) in enumerate(seqs):
                lane0 = (vsi * len(seqs) + n) * H_WKV
                y_ref[0, 0, s * spb + j, pl.ds(row, H_WKV), :] = yt[lane0:lane0 + H_WKV, :]
        return carry

    lax.fori_loop(0, vl_n, emit_y, 0, unroll=4)
    sf_ref[0] = s_scr[...]


def _wkv_scan(rt, wt, kt, kkt, bt, vt, s0, *, tile0, n_seq, seq_tiles, spb):
    tc = WKV_TC
    if spb > 1:
        assert seq_tiles == 1 and n_seq % spb == 0 and tile0 % spb == 0
        nsrc, n_sb = 1, n_seq // spb
    else:
        nsrc, n_sb = n_seq, 1
    inst = nsrc * spb * H_WKV
    vs = LANES // inst
    vl_n = HS_WKV // vs
    assert vl_n % SUBLANES == 0, "value rows are processed eight at a time"
    cpt = TM // tc
    n_chunks = seq_tiles * cpt
    kp = HS_WKV + SUBLANES
    vp = vl_n + SUBLANES if ((vl_n + SUBLANES) // SUBLANES) % 2 else vl_n + 2 * SUBLANES

    def chunk(g, i):
        return jnp.where(g // n_sb == 1, n_chunks - 1 - i, i)

    def in_map(g, i, *, src, per_dir):
        ce = chunk(g, i)
        rb = (g // n_sb) if per_dir else 0
        if spb > 1:
            return (tile0 // spb + g % n_sb, rb, ce)
        return (tile0 + src * seq_tiles + ce // cpt, rb, ce % cpt)

    def out_map(g, i):
        ce = chunk(g, i)
        if spb > 1:
            return (g // n_sb, 0, g % n_sb, 0, ce)
        return (g // n_sb, ce // cpt, 0, 0, ce % cpt)

    in_specs, operands = [], []
    for arr, per_dir in ((rt, False), (wt, True), (kt, True), (kkt, False), (bt, True), (vt, False)):
        for src in range(nsrc):
            in_specs.append(pl.BlockSpec((spb, W_MIX, tc), functools.partial(in_map, src=src, per_dir=per_dir),
                                         pipeline_mode=pl.Buffered(1)))
            operands.append(arr)
    sspec = pl.BlockSpec((1, vl_n, HS_WKV, LANES), lambda g, i: (g, 0, 0, 0))
    in_specs.append(sspec)
    out_specs = [pl.BlockSpec((1, 1, nsrc * spb, W_MIX, tc), out_map)]
    out_shape = [jax.ShapeDtypeStruct((N_DIR, seq_tiles, n_seq, W_MIX, TM), F32)]
    res = pl.pallas_call(
        functools.partial(_wkv_body, nsrc=nsrc, spb=spb, vs=vs, n_sb=n_sb, tc=tc, kp=kp, vp=vp),
        grid=(N_DIR * n_sb, n_chunks),
        in_specs=in_specs,
        out_specs=out_specs + [sspec],
        out_shape=out_shape + [jax.ShapeDtypeStruct(s0.shape, F32)],
        scratch_shapes=([pltpu.VMEM((tc * kp, LANES), F32)] * 5
                        + [pltpu.VMEM((tc * vp, LANES), F32)] * 2
                        + [pltpu.VMEM((vl_n, HS_WKV, LANES), F32), pltpu.VMEM((vl_n, LANES), F32)]),
        compiler_params=_cparams(("parallel", "arbitrary"), VMEM_LIMIT),
        name="wkv_scan",
    )(*operands, s0)
    return res[0], res[1]


def _merge_body(x_ref, mod_ref, n1g_ref, n2g_ref, wg_ref, gb_ref, wbr_ref, wo_ref,
                lnxg_ref, lnxb_ref, ind_ref,
                ya_ref, yd_ref, gbg_ref, h_ref, ycf_ref, ycb_ref, ylf_ref, ylb_ref, g_ref, bon_ref,
                x1_ref, h2_ref, *, n_ctx_tiles):
    x = x_ref[...]
    m = mod_ref[0]
    ind = ind_ref[...]
    h = (_rms(x, n1g_ref[...]) * (1.0 + m[1:2, :]) + m[0:1, :]).astype(BF16)
    y_b = gbg_ref[...] * (h_ref[:, 0:W_MIX] + h_ref[:, W_MIX:2 * W_MIX])
    is_ctx = pl.program_id(0) < n_ctx_tiles
    y = jnp.where(is_ctx, ycf_ref[0, 0, 0] + ycb_ref[0, 0, 0], ylf_ref[0, 0, 0] + ylb_ref[0, 0, 0]).T
    yc = y - _segsum(y, ind) * (1.0 / HS_WKV)
    var = _segsum(yc * yc, ind) * (1.0 / HS_WKV)
    y_c = (yc * lax.rsqrt(var + LNX_EPS) * lnxg_ref[...] + lnxb_ref[...] + bon_ref[...]) * g_ref[...]
    merged = None
    for n, yn in enumerate((ya_ref[...], y_b, y_c, yd_ref[...])):
        cs = slice(n * D_MODEL, (n + 1) * D_MODEL)
        gate = jax.nn.sigmoid(jnp.dot(h, wg_ref[:, cs], preferred_element_type=F32) + gb_ref[:, cs])
        br = jnp.dot(yn.astype(BF16), wbr_ref[n * W_MIX:(n + 1) * W_MIX, :], preferred_element_type=F32)
        merged = gate * br if merged is None else merged + gate * br
    mo = jnp.dot(merged.astype(BF16), wo_ref[...], preferred_element_type=F32)
    x1 = x + m[2:3, :] * mo
    x1_ref[...] = x1
    h2_ref[...] = (_rms(x1, n2g_ref[...]) * (1.0 + m[4:5, :]) + m[3:4, :]).astype(BF16)


def _merge(x, mod, lw, tok_in, y_ctx, y_lat, n_ctx_tok, lat_len):
    t = x.shape[0]
    n_ctx_tiles = n_ctx_tok // TM
    tps = lat_len // TM
    midx = functools.partial(_mod_index, tm=TM, n_ctx_tok=n_ctx_tok, lat_len=lat_len)
    wnames = ["norm1_g", "norm2_g", "w_gate", "gate_b", "w_branch", "w_out", "lnx_g", "lnx_b", "ind"]
    wts = [lw[n] for n in wnames]
    tok = lambda wd: pl.BlockSpec((TM, wd), lambda i: (i, 0))
    ya, yd, gbg, h, g, bon = tok_in
    yblock = (1, 1, 1, W_MIX, TM)

    def ctx_spec(d):
        return pl.BlockSpec(yblock, lambda i: (d, 0, jnp.minimum(i, n_ctx_tiles - 1), 0, 0))

    def lat_spec(d):
        def imap(i):
            r = jnp.maximum(i - n_ctx_tiles, 0)
            return (d, lax.rem(r, tps), lax.div(r, tps), 0, 0)
        return pl.BlockSpec(yblock, imap)

    return pl.pallas_call(
        functools.partial(_merge_body, n_ctx_tiles=n_ctx_tiles),
        grid=(t // TM,),
        in_specs=([tok(D_MODEL), pl.BlockSpec((1, N_MOD, D_MODEL), lambda i: (midx(i), 0, 0))]
                  + [_const_spec(w.shape) for w in wts]
                  + [tok(W_MIX), tok(W_MIX), tok(W_MIX), tok(2 * W_MIX),
                     ctx_spec(0), ctx_spec(1), lat_spec(0), lat_spec(1), tok(W_MIX), tok(W_MIX)]),
        out_specs=[tok(D_MODEL), tok(D_MODEL)],
        out_shape=[jax.ShapeDtypeStruct((t, D_MODEL), F32), jax.ShapeDtypeStruct((t, D_MODEL), BF16)],
        compiler_params=_cparams(("parallel",), VMEM_LIMIT),
        name="merge",
    )(x, mod, *wts, ya, yd, gbg, h, y_ctx, y_ctx, y_lat, y_lat, g, bon)


_CAND_VALID = (8, 8, 8, 5, 4, 3, 2, 2, 2, 8)


def _route_head(qs, keys_ref):
    kio = lax.broadcasted_iota(I32, (N_KEYS, LANES), 0)
    sub = lax.broadcasted_iota(I32, (SUBLANES, LANES), 0)
    kid = lax.broadcasted_iota(I32, (PEER_TOPK, LANES), 0)
    neg = -jnp.inf

    def bc(x, r):
        return jnp.broadcast_to(x[r:r + 1, :], (SUBLANES, LANES))

    def head():
        tops = []
        for p in range(2):
            s = jnp.dot(keys_ref[p], qs[p], preferred_element_type=F32)
            vals = jnp.zeros((PEER_TOPK, LANES), F32)
            idxs = jnp.zeros((PEER_TOPK, LANES), I32)
            for r in range(PEER_TOPK):
                m = jnp.max(s, axis=0, keepdims=True)
                cand = jnp.where(s == m, kio, N_KEYS)
                ix = jnp.min(cand, axis=0, keepdims=True)
                s = jnp.where(cand == ix, neg, s)
                vals = jnp.where(kid == r, m, vals)
                idxs = jnp.where(kid == r, ix, idxs)
            tops.append((vals, idxs))
        (a0, i0), (a1, i1) = tops
        lo, hi = slice(0, SUBLANES), slice(SUBLANES, 2 * SUBLANES)
        slabs = [bc(a0, 0) + a1[lo], bc(a0, 0) + a1[hi]]
        ci = [bc(i0, 0), bc(i0, 0)]
        cj = [i1[lo], i1[hi]]
        for r in range(1, SUBLANES):
            slabs.append(bc(a0, r) + a1[lo])
            ci.append(bc(i0, r))
            cj.append(i1[lo])
        slabs.append(a0[hi] + bc(a1, 0))
        ci.append(i0[hi])
        cj.append(bc(i1, 0))
        slabs = [jnp.where(sub < nv, sl, neg) for sl, nv in zip(slabs, _CAND_VALID)]
        ids = [a * N_KEYS + b for a, b in zip(ci, cj)]
        vals = jnp.zeros((PEER_TOPK, LANES), F32)
        esel = jnp.zeros((PEER_TOPK, LANES), I32)
        for r in range(PEER_TOPK):
            level = list(zip(slabs, ids))
            while len(level) > 1:
                nxt = []
                for (va, ea), (vb, eb) in zip(level[0::2], level[1::2]):
                    take = vb > va
                    nxt.append((jnp.where(take, vb, va), jnp.where(take, eb, ea)))
                if len(level) % 2:
                    nxt.append(level[-1])
                level = nxt
            v8, e8 = level[0]
            for sh in (4, 2, 1):
                vr, er = pltpu.roll(v8, sh, 0), pltpu.roll(e8, sh, 0)
                take = vr > v8
                v8, e8 = jnp.where(take, vr, v8), jnp.where(take, er, e8)
            m, ex = v8[0:1, :], e8[0:1, :]
            slabs = [jnp.where(eid == ex, neg, sl) for sl, eid in zip(slabs, ids)]
            vals = jnp.where(kid == r, m, vals)
            esel = jnp.where(kid == r, ex, esel)
        e = jnp.exp(vals - vals[0:1, :])
        return esel, e / jnp.sum(e, axis=0, keepdims=True)

    return head()


def _peer_body(h2_ref, h2n_ref, wqt_ref, keys_ref, ut_ref, v_ref, x1_ref, mod_ref, fng_ref, o_ref,
               q_scr, e_scr, g_scr, et_scr, gt_scr, gs_scr, acc_scr, *, rows, pitch, units, final):
    m = pl.program_id(0)
    e = pl.program_id(1)
    tm = h2_ref.shape[0]
    n_chunks = tm // LANES
    nsel = PEER_HEADS * PEER_TOPK
    slot = lax.rem(m, 2)

    def project_queries(src_ref):
        q = lax.dot_general(wqt_ref[...], src_ref[...], _NT, preferred_element_type=F32).astype(BF16)
        for c in range(n_chunks):
            q_scr[c] = q[:, c * LANES:(c + 1) * LANES]

    def route_unit(u, dst):
        c = u // PEER_HEADS
        h = lax.rem(u, PEER_HEADS)
        qs = [q_scr[c, pl.ds(pl.multiple_of(h * (2 * N_KEYS) + p * N_KEYS, N_KEYS), N_KEYS), :]
              for p in range(2)]
        esel, gates = _route_head(qs, keys_ref)
        row = pl.multiple_of(h * PEER_TOPK, PEER_TOPK)
        e_scr[dst, c, pl.ds(row, PEER_TOPK), :] = esel
        g_scr[dst, c, pl.ds(row, PEER_TOPK), :] = gates

    @pl.when(jnp.logical_and(e == 0, m == 0))
    def _first_tile_routing():
        project_queries(h2_ref)

        def unit(u, c):
            route_unit(u, 0)
            return c

        lax.fori_loop(0, n_chunks * PEER_HEADS, unit, 0)

    @pl.when(e == 0)
    def _build():
        for c in range(n_chunks):
            et_scr[c * LANES:(c + 1) * LANES, :] = e_scr[slot, c].T
            gt_scr[c * LANES:(c + 1) * LANES, :] = g_scr[slot, c].T
        kio = lax.broadcasted_iota(I32, (N_KEYS, nsel), 0)

        def tok(t, c):
            erow = et_scr[pl.ds(t, 1), :]
            grow = gt_scr[pl.ds(t, 1), :]
            at = jnp.where(kio == (erow >> 7), grow, 0.0).astype(BF16)
            bt = jnp.where(kio == (erow & (N_KEYS - 1)), 1.0, 0.0).astype(BF16)
            gt = lax.dot_general(at, bt, _NT, preferred_element_type=F32)
            hi = pltpu.bitcast(gt[0:rows, :], jnp.uint32) & jnp.uint32(0xFFFF0000)
            lo = pltpu.bitcast(gt[rows:2 * rows, :], jnp.uint32) >> 16
            gs_scr[pl.ds(pl.multiple_of(t * pitch, SUBLANES), rows), :] = hi | lo
            return c

        lax.fori_loop(0, tm, tok, 0, unroll=16)
        acc_scr[...] = jnp.zeros_like(acc_scr)
        project_queries(h2n_ref)

    for k in range(units):
        route_unit(e * units + k, 1 - slot)

    per_sub = PEER_SUB // N_KEYS
    per_step = v_ref.shape[0] // N_KEYS
    steps_per_half = rows // per_step
    row0 = lax.rem(e, steps_per_half) * per_step
    shift = jnp.where(e < steps_per_half, 0, 16).astype(jnp.uint32)
    h2 = h2_ref[...]
    total = None
    for sb in range(per_step // per_sub):
        es = slice(sb * PEER_SUB, (sb + 1) * PEER_SUB)
        hmat = jnp.dot(h2, ut_ref[:, es], preferred_element_type=F32)
        words = jnp.concatenate(
            [gs_scr[pl.ds(row0 + sb * per_sub + ii, tm, stride=pitch), :] for ii in range(per_sub)], axis=1)
        gm = pltpu.bitcast((words << shift) & jnp.uint32(0xFFFF0000), F32)
        act = (jax.nn.gelu(hmat) * gm).astype(BF16)
        part = jnp.dot(act, v_ref[es, :], preferred_element_type=F32)
        total = part if total is None else total + part
    acc_scr[...] += total

    @pl.when(e == pl.num_programs(1) - 1)
    def _out():
        x2 = x1_ref[...] + mod_ref[0][5:6, :] * acc_scr[...]
        o_ref[...] = _rms(x2, fng_ref[...]) if final else x2


def _peer(h2, wqt, keys, u, v, x1, mod, fng, n_ctx_tok, lat_len, final):
    t = h2.shape[0]
    nsel = PEER_HEADS * PEER_TOPK
    rows = N_KEYS // 2
    pitch = rows + SUBLANES
    n_e = (N_KEYS * N_KEYS) // PEER_EB
    n_m = t // TM_PEER
    n_chunks = TM_PEER // LANES
    units = (n_chunks * PEER_HEADS) // n_e
    assert units * n_e == n_chunks * PEER_HEADS
    midx = functools.partial(_mod_index, tm=TM_PEER, n_ctx_tok=n_ctx_tok, lat_len=lat_len)
    tok = lambda wd: pl.BlockSpec((TM_PEER, wd), lambda m, e: (m, 0))
    nxt = pl.BlockSpec((TM_PEER, D_MODEL), lambda m, e: (jnp.minimum(m + 1, n_m - 1), 0))
    espec = pl.BlockSpec((PEER_EB, D_MODEL), lambda m, e: (e, 0))
    single = dict(pipeline_mode=pl.Buffered(1))
    return pl.pallas_call(
        functools.partial(_peer_body, rows=rows, pitch=pitch, units=units, final=final),
        grid=(n_m, n_e),
        in_specs=[tok(D_MODEL), nxt,
                  pl.BlockSpec(wqt.shape, lambda m, e: (0, 0), **single),
                  pl.BlockSpec(keys.shape, lambda m, e: (0, 0, 0), **single),
                  pl.BlockSpec((D_MODEL, PEER_EB), lambda m, e: (0, e)), espec, tok(D_MODEL),
                  pl.BlockSpec((1, N_MOD, D_MODEL), lambda m, e: (midx(m), 0, 0)),
                  _const_spec((1, D_MODEL))],
        out_specs=tok(D_MODEL),
        out_shape=jax.ShapeDtypeStruct((t, D_MODEL), F32),
        scratch_shapes=[pltpu.VMEM((n_chunks, wqt.shape[0], LANES), BF16),
                        pltpu.VMEM((2, n_chunks, nsel, LANES), I32),
                        pltpu.VMEM((2, n_chunks, nsel, LANES), F32),
                        pltpu.VMEM((TM_PEER, nsel), I32),
                        pltpu.VMEM((TM_PEER, nsel), F32),
                        pltpu.VMEM((TM_PEER * pitch, N_KEYS), jnp.uint32),
                        pltpu.VMEM((TM_PEER, D_MODEL), F32)],
        compiler_params=_cparams(("arbitrary", "arbitrary"), VMEM_LIMIT),
        name="peer",
    )(h2, h2, wqt, keys, u, v, x1, mod, fng)


def _wkv_state_in(s, vs):
    n = s.shape[0]
    vl = HS_WKV // vs
    s = s.reshape(n, N_DIR, H_WKV, vs, vl, HS_WKV).transpose(1, 4, 5, 3, 0, 2)
    return s.reshape(N_DIR, vl, HS_WKV, vs * n * H_WKV)


def _wkv_state_out(s, n_sb, spb, vs):
    vl = HS_WKV // vs
    s = s.reshape(N_DIR, n_sb, vl, HS_WKV, vs, spb, H_WKV).transpose(1, 5, 0, 6, 4, 2, 3)
    return s.reshape(n_sb * spb, N_DIR, H_WKV, HS_WKV, HS_WKV)


def _layer_weights(i, prm):
    eye_h = jnp.eye(H_LRU, dtype=F32)
    eye_d = jnp.eye(N_DIR, dtype=F32)
    perm = _WKV_PERM

    def lru_bd(wt):
        return jnp.einsum("dhij,hg->hidgj", wt, eye_h).reshape(W_MIX, N_DIR * W_MIX)

    def lora_bd(wt):
        r = wt.shape[1]
        return jnp.einsum("drc,de->drec", wt, eye_d).reshape(N_DIR * r, N_DIR * W_MIX)

    w_in = prm["w_in"][i]
    pad = jnp.zeros((D_MODEL, Z_COLS - 5504), F32)
    rkv = [w_in[:, 2560 + j * W_MIX:2560 + (j + 1) * W_MIX][:, perm] for j in range(3)]
    w_in_perm = jnp.concatenate(
        [w_in[:, 0:1536]] + rkv + [w_in[:, 1536:2560], w_in[:, 4480:5504], w_in[:, 4096:4480], pad],
        axis=1).astype(BF16)
    row = lambda x: x.reshape(1, -1).astype(F32)
    head_of = np.arange(W_MIX) % H_WKV
    w_branch = prm["w_branch"][i]
    w_branch = jnp.concatenate([w_branch[0], w_branch[1], w_branch[2][perm, :], w_branch[3]], axis=0)
    return {
        "w_in": w_in_perm,
        "w_gate": w_in[:, 5504:].astype(BF16),
        "norm1_g": row(prm["norm1_g"][i]),
        "norm2_g": row(prm["norm2_g"][i]),
        "conv_a_w": prm["conv_a_w"][i],
        "conv_b_w": prm["conv_b_w"][i],
        "conv_b_b": row(prm["conv_b_b"][i]),
        "lru_w": jnp.concatenate([lru_bd(prm["lru_wa"][i]), lru_bd(prm["lru_wx"][i])], axis=1).astype(BF16),
        "lru_b": jnp.concatenate([row(prm["lru_ba"][i]), row(prm["lru_bx"][i])], axis=1),
        "lru_lam": row(prm["lru_lambda"][i]),
        "w0": row(prm["rwkv_w0"][i][:, perm]),
        "w2": lora_bd(prm["rwkv_w2"][i][:, :, perm]).astype(BF16),
        "a0": row(prm["rwkv_a0"][i][:, perm]),
        "a2": lora_bd(prm["rwkv_a2"][i][:, :, perm]).astype(BF16),
        "g2": prm["rwkv_g2"][i][:, perm].astype(BF16),
        "kk": row(prm["rwkv_kk"][i][perm]),
        "ka": row(prm["rwkv_ka"][i][perm]),
        "rk": row(prm["rwkv_rk"][i].reshape(W_MIX)[perm]),
        "lnx_g": row(prm["lnx_g"][i][perm]),
        "lnx_b": row(prm["lnx_b"][i][perm]),
        "sg_ln_g": row(prm["sg_ln_g"][i]),
        "sg_ln_b": row(prm["sg_ln_b"][i]),
        "sg_ws": prm["sg_ws"][i].astype(BF16),
        "sg_bst": prm["sg_bs"][i].T,
        "gate_b": row(prm["gate_b"][i]),
        "w_branch": w_branch.astype(BF16),
        "w_out": prm["w_out"][i].astype(BF16),
        "wq_t": prm["peer_wq"][i].T.astype(BF16),
        "keys": prm["peer_keys"][i].astype(BF16),
        "peer_ut": prm["peer_u"][i].astype(BF16).T,
        "peer_v": prm["peer_v"][i].astype(BF16),
        "ind": jnp.asarray(head_of[:, None] == head_of[None, :], BF16),
    }


def kernel(x_prompt, x_sample, state_lru, state_wkv, c, c_ctx, norm1_g, norm2_g, w_mod, b_mod, w_in, conv_a_w, conv_b_w, conv_b_b, lru_wa, lru_ba, lru_wx, lru_bx, lru_lambda, rwkv_w0, rwkv_w2, rwkv_a0, rwkv_a2, rwkv_g2, rwkv_kk, rwkv_ka, rwkv_rk, lnx_g, lnx_b, sg_ln_g, sg_ln_b, sg_ws, sg_bs, gate_b, w_branch, w_out, peer_wq, peer_keys, peer_u, peer_v, final_norm_g):
    prm = dict(norm1_g=norm1_g, norm2_g=norm2_g, w_in=w_in, conv_a_w=conv_a_w, conv_b_w=conv_b_w,
               conv_b_b=conv_b_b, lru_wa=lru_wa, lru_ba=lru_ba, lru_wx=lru_wx, lru_bx=lru_bx,
               lru_lambda=lru_lambda, rwkv_w0=rwkv_w0, rwkv_w2=rwkv_w2, rwkv_a0=rwkv_a0, rwkv_a2=rwkv_a2,
               rwkv_g2=rwkv_g2, rwkv_kk=rwkv_kk, rwkv_ka=rwkv_ka, rwkv_rk=rwkv_rk, lnx_g=lnx_g,
               lnx_b=lnx_b, sg_ln_g=sg_ln_g, sg_ln_b=sg_ln_b, sg_ws=sg_ws, sg_bs=sg_bs, gate_b=gate_b,
               w_branch=w_branch, w_out=w_out, peer_wq=peer_wq, peer_keys=peer_keys, peer_u=peer_u,
               peer_v=peer_v)
    bc, lc, _ = x_prompt.shape
    bl, ll, _ = x_sample.shape
    depth = w_mod.shape[0]
    n_ctx_tok = bc * lc
    n_ctx_tiles = n_ctx_tok // TM
    lat_tiles = ll // TM
    ctx_spb = min(WKV_CTX_SPB, bc)
    lru_spb = ll // lc
    assert lc == TM and ll % TM_IN == 0 and n_ctx_tok % TM_IN == 0 and bl + 1 <= SUBLANES
    assert ll % GRID_W == 0 and TM % GRID_W == 0 and bc % ctx_spb == 0 and n_ctx_tok % ll == 0
    assert LANES % (ctx_spb * H_WKV) == 0 and LANES % (bl * H_WKV) == 0

    cond = jnp.zeros((SUBLANES, D_MODEL), F32).at[0].set(c_ctx).at[1:1 + bl].set(c)
    mods = _modulation(cond, w_mod, b_mod).reshape(depth, SUBLANES, N_MOD, D_MODEL)
    fng = final_norm_g.reshape(1, D_MODEL)
    x = jnp.concatenate([x_prompt.reshape(n_ctx_tok, D_MODEL), x_sample.reshape(bl * ll, D_MODEL)], axis=0)
    ctx_vs = LANES // (ctx_spb * H_WKV)
    lat_vs = LANES // (bl * H_WKV)
    n_sb = bc // ctx_spb
    wkv_zero = jnp.zeros((N_DIR * n_sb, HS_WKV // ctx_vs, HS_WKV, LANES), F32)
    lru_zero = jnp.zeros((n_ctx_tok // ll, lru_spb, N_DIR * W_MIX), F32)
    new_lru, new_wkv = [], []
    pnames = ["ya", "yd", "gbg", "la", "lu", "g", "bon", "rt", "vt", "kkt", "wt", "kt", "bt"]
    for i in range(depth):
        lw = _layer_weights(i, prm)
        mod = mods[i]
        z = _in_proj(x, mod, lw["norm1_g"], lw["w_in"], n_ctx_tok, ll)
        p = dict(zip(pnames, _prep(z, lw, n_ctx_tok, ll)))

        lat_h0 = jnp.zeros((bl, lru_spb, N_DIR * W_MIX), F32).at[:, 0].set(
            state_lru[:, i].astype(F32).reshape(bl, N_DIR * W_MIX))
        h, lru_s = _lru_scan(p["la"], p["lu"], jnp.concatenate([lru_zero, lat_h0], axis=0),
                             rows=ll, n_ctx_blocks=n_ctx_tok // ll, ctx_cfg=(lru_spb, lc), lat_cfg=(1, ll))
        new_lru.append(lru_s[:n_ctx_tok // ll].reshape(bc, N_DIR, W_MIX))

        wkv_in = [p[n] for n in ("rt", "wt", "kt", "kkt", "bt", "vt")]
        y_c, s_c = _wkv_scan(*wkv_in, wkv_zero, tile0=0, n_seq=bc, seq_tiles=1, spb=ctx_spb)
        y_l, _ = _wkv_scan(*wkv_in, _wkv_state_in(state_wkv[:, i].astype(F32), lat_vs),
                           tile0=n_ctx_tiles, n_seq=bl, seq_tiles=lat_tiles, spb=1)
        new_wkv.append(_wkv_state_out(s_c, n_sb, ctx_spb, ctx_vs))

        tok_in = [p["ya"], p["yd"], p["gbg"], h, p["g"], p["bon"]]
        x1, h2 = _merge(x, mod, lw, tok_in, y_c, y_l, n_ctx_tok, ll)
        x = _peer(h2, lw["wq_t"], lw["keys"], lw["peer_ut"], lw["peer_v"], x1, mod, fng,
                        n_ctx_tok, ll, final=(i == depth - 1))
    y_prompt = x[:n_ctx_tok].reshape(bc, lc, D_MODEL)
    y_sample = x[n_ctx_tok:].reshape(bl, ll, D_MODEL)
    return (y_prompt, y_sample, jnp.stack(new_lru, axis=1), jnp.stack(new_wkv, axis=1))
```

```python
import functools

import numpy as np
import jax
import jax.numpy as jnp
from jax import lax
from jax.experimental import pallas as pl
from jax.experimental.pallas import tpu as pltpu

F32 = jnp.float32
BF16 = jnp.bfloat16
I32 = jnp.int32

D_MODEL = 1024
W_MIX = 512
N_DIR = 2
N_BRANCH = 4
H_WKV = 8
HS_WKV = 64
H_LRU = 8
HB_LRU = 64
LORA_W = 64
LORA_A = 64
LORA_G = 128
GRID_W = 64
CHUNK = 128
G_SG = 4
N_KEYS = 128
PEER_HEADS = 8
PEER_TOPK = 16
N_MOD = 6
EPS = 1e-6
LNX_EPS = 64e-5
LRU_C = 8.0

LANES = 128
SUBLANES = 8
TM = 256
TM_PEER = 512
PEER_EB = 1024
PEER_SUB = 512
Z_COLS = 5632
WKV_TC = LANES
WKV_CTX_SPB = 8
VMEM_LIMIT = 56 * 1024 * 1024

_NT = (((1,), (1,)), ((), ()))
_WKV_PERM = np.array([(n % H_WKV) * HS_WKV + n // H_WKV for n in range(W_MIX)])


def _cparams(sem, vmem=None):
    return pltpu.CompilerParams(dimension_semantics=sem, vmem_limit_bytes=vmem)


def _const_spec(shape):
    nd = len(shape)
    return pl.BlockSpec(shape, lambda *_: (0,) * nd)


def _softplus(x):
    return jnp.maximum(x, 0.0) + jnp.log1p(jnp.exp(-jnp.abs(x)))


def _rms(x, g):
    return x * lax.rsqrt(jnp.mean(x * x, axis=-1, keepdims=True) + EPS) * g


def _segsum(x, ind):
    hi = x.astype(BF16)
    lo = (x - hi.astype(F32)).astype(BF16)
    return (jnp.dot(hi, ind, preferred_element_type=F32)
            + jnp.dot(lo, ind, preferred_element_type=F32))


def _mod_index(i, tm, n_ctx_tok, lat_len):
    n_ctx_tiles = n_ctx_tok // tm
    tiles_per_seq = lat_len // tm
    return jnp.where(i < n_ctx_tiles, 0, 1 + lax.div(i - n_ctx_tiles, tiles_per_seq))


def _mod_body(s_ref, w_ref, b_ref, o_ref):
    s = s_ref[...]
    s = s * jax.nn.sigmoid(s)
    o_ref[0] = jnp.dot(s.astype(BF16), w_ref[0].astype(BF16), preferred_element_type=F32) + b_ref[0]


def _modulation(cond, w_mod, b_mod):
    depth = w_mod.shape[0]
    n = w_mod.shape[2]
    tn = 1536
    return pl.pallas_call(
        _mod_body,
        grid=(depth, n // tn),
        in_specs=[_const_spec((SUBLANES, D_MODEL)),
                  pl.BlockSpec((1, D_MODEL, tn), lambda l, j: (l, 0, j)),
                  pl.BlockSpec((1, 1, tn), lambda l, j: (l, 0, j))],
        out_specs=pl.BlockSpec((1, SUBLANES, tn), lambda l, j: (l, 0, j)),
        out_shape=jax.ShapeDtypeStruct((depth, SUBLANES, n), F32),
        compiler_params=_cparams(("parallel", "parallel"), VMEM_LIMIT),
        name="modulation",
    )(cond, w_mod, b_mod.reshape(depth, 1, n))


def _prep_body(x_ref, xp_ref, xn_ref, mod_ref, n1g_ref, win_ref,
               caw_ref, cbw_ref, cbb_ref, lruw_ref, lrub_ref, lam_ref,
               w0_ref, w2_ref, a0_ref, a2_ref, g2_ref, kkw_ref, ka_ref, rk_ref,
               lng_ref, lnb_ref, ws_ref, bst_ref, ind_ref,
               ya_ref, yd_ref, gbg_ref, la_ref, lu_ref, g_ref, bon_ref,
               rt_ref, vt_ref, kkt_ref, wt_ref, kt_ref, bt_ref,
               *, n_ctx_tiles, tiles_per_seq):
    i = pl.program_id(0)
    is_ctx = i < n_ctx_tiles
    t = lax.broadcasted_iota(I32, (TM, 1), 0)
    ind = ind_ref[...]
    m = mod_ref[0]

    def modulated(xv):
        return (_rms(xv, n1g_ref[...]) * (1.0 + m[1:2, :]) + m[0:1, :]).astype(BF16)

    def project(hv, lo, hi):
        return jnp.dot(hv, win_ref[:, lo:hi], preferred_element_type=F32)

    h = modulated(x_ref[...])
    za = project(h, 0, 1536)
    zc = project(h, 1536, 3072)
    zb = project(h, 3072, 4096)
    zd = project(h, 4096, 5120)
    zl = project(h, 5120, 5120 + 2 * LORA_W + 2 * LORA_A + LORA_G)
    halo_prev = project(modulated(xp_ref[...]), 3072 + W_MIX, 4096)
    halo_next = project(modulated(xn_ref[...]), 3072 + W_MIX, 4096)

    pm = jnp.where(is_ctx, TM - 1, GRID_W - 1)
    pos = t & pm
    a_b = za[:, 0:W_MIX]
    ac = za[:, W_MIX:2 * W_MIX] * za[:, 2 * W_MIX:3 * W_MIX]
    up = jnp.where(pos == 0, 0.0, pltpu.roll(ac, 1, 0))
    dn = jnp.where(pos == pm, 0.0, pltpu.roll(ac, TM - 1, 0))
    ya_ref[...] = a_b * (caw_ref[0:1, :] * up + caw_ref[1:2, :] * ac + caw_ref[2:3, :] * dn)

    seq_tile = lax.rem(jnp.maximum(i - n_ctx_tiles, 0), tiles_per_seq)
    first = jnp.logical_or(is_ctx, seq_tile == 0)
    last = jnp.logical_or(is_ctx, seq_tile == tiles_per_seq - 1)
    prev = jnp.where(first, 0.0, halo_prev[SUBLANES - 1:SUBLANES, :])
    nxt0 = jnp.where(last, 0.0, halo_next[0:1, :])
    nxt1 = jnp.where(last, 0.0, halo_next[1:2, :])
    bx = zb[:, W_MIX:2 * W_MIX]
    m1 = jnp.where(t == 0, prev, pltpu.roll(bx, 1, 0))
    p1 = jnp.where(t == TM - 1, nxt0, pltpu.roll(bx, TM - 1, 0))
    p2 = jnp.where(t == TM - 2, nxt0, jnp.where(t == TM - 1, nxt1, pltpu.roll(bx, TM - 2, 0)))
    xb = (cbw_ref[0:1, :] * m1 + cbw_ref[1:2, :] * bx + cbw_ref[2:3, :] * p1
          + cbw_ref[3:4, :] * p2 + cbb_ref[...])
    gates = jnp.dot(xb.astype(BF16), lruw_ref[...], preferred_element_type=F32) + lrub_ref[...]
    rg = jax.nn.sigmoid(gates[:, 0:2 * W_MIX])
    ig = jax.nn.sigmoid(gates[:, 2 * W_MIX:4 * W_MIX])
    log_a = -LRU_C * rg * _softplus(-lam_ref[...])
    xb2 = jnp.concatenate([xb, xb], axis=1)
    la_ref[...] = jnp.exp(log_a)
    lu_ref[...] = jnp.sqrt(jnp.tanh(-log_a) * (jnp.exp(2.0 * log_a) + 1.0)) * (ig * xb2)
    gbg_ref[...] = jax.nn.gelu(zb[:, 0:W_MIX])

    zr = zc[:, 0:W_MIX]
    zk = zc[:, W_MIX:2 * W_MIX]
    zv = zc[:, 2 * W_MIX:3 * W_MIX]
    zwd = zl[:, 0:2 * LORA_W]
    zad = zl[:, 2 * LORA_W:2 * LORA_W + 2 * LORA_A]
    zgd = zl[:, 2 * LORA_W + 2 * LORA_A:2 * LORA_W + 2 * LORA_A + LORA_G]
    wlin = w0_ref[...] + jnp.dot(jnp.tanh(zwd).astype(BF16), w2_ref[...], preferred_element_type=F32)
    wt_ref[0] = jnp.exp(-jnp.exp(-_softplus(-wlin) - 0.5)).T
    av = jax.nn.sigmoid(a0_ref[...] + jnp.dot(zad.astype(BF16), a2_ref[...], preferred_element_type=F32))
    g_ref[...] = jnp.dot(jax.nn.sigmoid(zgd).astype(BF16), g2_ref[...], preferred_element_type=F32)
    kkr = zk * kkw_ref[...]
    kkn = kkr / jnp.maximum(jnp.sqrt(_segsum(kkr * kkr, ind)), 1e-12)
    zk2 = jnp.concatenate([zk, zk], axis=1)
    ka2 = jnp.concatenate([ka_ref[...], ka_ref[...]], axis=1)
    kd = zk2 * (1.0 + (av - 1.0) * ka2)
    kt_ref[0] = kd.T
    bt_ref[0] = (jnp.concatenate([kkn, kkn], axis=1) * av).T
    rt_ref[0] = zr.T
    vt_ref[0] = zv.T
    kkt_ref[0] = kkn.T
    bon_ref[...] = _segsum(zr * (kd[:, 0:W_MIX] + kd[:, W_MIX:2 * W_MIX]) * rk_ref[...], ind) * zv

    zg = jax.nn.gelu(zd)
    u = zg[:, 0:W_MIX]
    vv = zg[:, W_MIX:2 * W_MIX]
    vc = vv - jnp.mean(vv, axis=-1, keepdims=True)
    vn = vc * lax.rsqrt(jnp.mean(vc * vc, axis=-1, keepdims=True) + 1e-5) * lng_ref[...] + lnb_ref[...]
    for c in range(TM // CHUNK):
        rs = slice(c * CHUNK, (c + 1) * CHUNK)
        for gi in range(G_SG):
            cs = slice(gi * LANES, (gi + 1) * LANES)
            s = jnp.dot(ws_ref[gi], vn[rs, cs].astype(BF16), preferred_element_type=F32)
            yd_ref[rs, cs] = u[rs, cs] * (s + bst_ref[:, gi:gi + 1])


def _prep(x, mod, lw, n_ctx_tok, lat_len):
    t = x.shape[0]
    n_tiles = t // TM
    n_ctx_tiles = n_ctx_tok // TM
    tiles_per_seq = lat_len // TM
    rows8 = TM // SUBLANES
    last_blk = t // SUBLANES - 1
    midx = functools.partial(_mod_index, tm=TM, n_ctx_tok=n_ctx_tok, lat_len=lat_len)
    x_specs = [
        pl.BlockSpec((TM, D_MODEL), lambda i: (i, 0)),
        pl.BlockSpec((SUBLANES, D_MODEL), lambda i: (jnp.maximum(i * rows8 - 1, 0), 0)),
        pl.BlockSpec((SUBLANES, D_MODEL), lambda i: (jnp.minimum((i + 1) * rows8, last_blk), 0)),
        pl.BlockSpec((1, N_MOD, D_MODEL), lambda i: (midx(i), 0, 0)),
        _const_spec((1, D_MODEL)),
        pl.BlockSpec(lw["w_in"].shape, lambda i: (0, 0), pipeline_mode=pl.Buffered(1)),
    ]
    wnames = ["conv_a_w", "conv_b_w", "conv_b_b", "lru_w", "lru_b", "lru_lam", "w0", "w2", "a0", "a2",
              "g2", "kk", "ka", "rk", "sg_ln_g", "sg_ln_b", "sg_ws", "sg_bst", "ind"]
    wts = [lw[n] for n in wnames]
    w_specs = [_const_spec(w.shape) for w in wts]
    widths = [W_MIX, W_MIX, W_MIX, 2 * W_MIX, 2 * W_MIX, W_MIX, W_MIX]
    t_rows = [W_MIX, W_MIX, W_MIX, 2 * W_MIX, 2 * W_MIX, 2 * W_MIX]
    out_specs = ([pl.BlockSpec((TM, wd), lambda i: (i, 0)) for wd in widths]
                 + [pl.BlockSpec((1, r, TM), lambda i: (i, 0, 0)) for r in t_rows])
    out_shape = ([jax.ShapeDtypeStruct((t, wd), F32) for wd in widths]
                 + [jax.ShapeDtypeStruct((n_tiles, r, TM), F32) for r in t_rows])
    return pl.pallas_call(
        functools.partial(_prep_body, n_ctx_tiles=n_ctx_tiles, tiles_per_seq=tiles_per_seq),
        grid=(n_tiles,),
        in_specs=x_specs + w_specs,
        out_specs=out_specs,
        out_shape=out_shape,
        compiler_params=_cparams(("parallel",), VMEM_LIMIT),
        name="branch_prep",
    )(x, x, x, mod, lw["norm1_g"], lw["w_in"], *wts)


def _lru_body(a_ref, u_ref, h0_ref, h_ref, hf_ref, *, n_ctx_blocks, ctx_cfg, lat_cfg):
    fw, bw = slice(0, W_MIX), slice(W_MIX, 2 * W_MIX)

    def scan(nseq, l):
        def step(s, carry):
            out = []
            for j in range(nseq):
                tf = j * l + s
                tb = j * l + (l - 1 - s)
                hf = a_ref[pl.ds(tf, 1), fw] * carry[2 * j] + u_ref[pl.ds(tf, 1), fw]
                hb = a_ref[pl.ds(tb, 1), bw] * carry[2 * j + 1] + u_ref[pl.ds(tb, 1), bw]
                h_ref[pl.ds(tf, 1), fw] = hf
                h_ref[pl.ds(tb, 1), bw] = hb
                out += [hf, hb]
            return tuple(out)

        init = []
        for j in range(nseq):
            init += [h0_ref[0, j:j + 1, fw], h0_ref[0, j:j + 1, bw]]
        fin = lax.fori_loop(0, l, step, tuple(init), unroll=2)
        hf_ref[0] = h0_ref[0]
        for j in range(nseq):
            hf_ref[0, j:j + 1, fw] = fin[2 * j]
            hf_ref[0, j:j + 1, bw] = fin[2 * j + 1]

    is_ctx = pl.program_id(0) < n_ctx_blocks
    pl.when(is_ctx)(lambda: scan(*ctx_cfg))
    pl.when(jnp.logical_not(is_ctx))(lambda: scan(*lat_cfg))


def _lru_scan(a, u, h0, *, rows, n_ctx_blocks, ctx_cfg, lat_cfg):
    nb = a.shape[0] // rows
    w = a.shape[1]
    tok = pl.BlockSpec((rows, w), lambda i: (i, 0))
    st = pl.BlockSpec((1,) + h0.shape[1:], lambda i: (i, 0, 0))
    return pl.pallas_call(
        functools.partial(_lru_body, n_ctx_blocks=n_ctx_blocks, ctx_cfg=ctx_cfg, lat_cfg=lat_cfg),
        grid=(nb,),
        in_specs=[tok, tok, st],
        out_specs=[tok, st],
        out_shape=[jax.ShapeDtypeStruct(a.shape, F32), jax.ShapeDtypeStruct(h0.shape, F32)],
        compiler_params=_cparams(("parallel",), VMEM_LIMIT),
        name="lru_scan",
    )(a, u, h0)


_SLOT_ORDER = (0, 4, 2, 6, 1, 5, 3, 7)


def _rowsum8(parts):
    sub = lax.broadcasted_iota(I32, (SUBLANES, LANES), 0)
    slots = [parts[i] for i in _SLOT_ORDER]
    roll = pltpu.roll
    lvl1 = [jnp.where(sub < 4, a + roll(a, 4, 0), b + roll(b, 4, 0))
            for a, b in zip(slots[0::2], slots[1::2])]
    lvl2 = [jnp.where((sub & 3) < 2, a + roll(a, 6, 0), roll(b + roll(b, 6, 0), 2, 0))
            for a, b in zip(lvl1[0::2], lvl1[1::2])]
    a, b = lvl2
    return jnp.where((sub & 1) == 0, a + roll(a, 7, 0), roll(b + roll(b, 7, 0), 1, 0))


def _fold8(x):
    return jnp.sum(x.reshape(HS_WKV // SUBLANES, SUBLANES, LANES), axis=0)


def _wkv_body(*refs, nsrc, spb, vs, n_sb, tc, kp, vp):
    vl_n = HS_WKV // vs
    n_in = 6 * nsrc
    k_srcs = [refs[o * nsrc:(o + 1) * nsrc] for o in range(5)]
    v_srcs = refs[5 * nsrc:n_in]
    s0_ref = refs[n_in]
    y_ref = refs[n_in + 1]
    sf_ref = refs[n_in + 2]
    k_scr = refs[n_in + 3:n_in + 8]
    v_scr, y_scr, s_scr, sa_scr = refs[n_in + 8:n_in + 12]
    r_scr, w_scr, k_scr_, kk_scr, b_scr = k_scr
    backward = pl.program_id(0) // n_sb == 1
    seqs = [(s, j) for s in range(nsrc) for j in range(spb)]

    @pl.when(pl.program_id(1) == 0)
    def _():
        s_scr[...] = s0_ref[0]

    def build_k(c, carry):
        row = pl.multiple_of(c * H_WKV, H_WKV)
        for o in range(5):
            slab = [k_srcs[o][s][j, pl.ds(row, H_WKV), :] for s, j in seqs]
            k_scr[o][pl.ds(c, tc, stride=kp), :] = jnp.concatenate(slab * vs, axis=0).T
        return carry

    lax.fori_loop(0, HS_WKV, build_k, 0, unroll=4)

    def build_v(vl, carry):
        slab = []
        for vsi in range(vs):
            row = pl.multiple_of((vsi * vl_n + vl) * H_WKV, H_WKV)
            slab += [v_srcs[s][j, pl.ds(row, H_WKV), :] for s, j in seqs]
        v_scr[pl.ds(vl, tc, stride=vp), :] = jnp.concatenate(slab, axis=0).T
        return carry

    lax.fori_loop(0, vl_n, build_v, 0, unroll=4)

    def step(s, carry):
        t = jnp.where(backward, tc - 1 - s, s)
        krow = pl.multiple_of(t * kp, SUBLANES)
        vrow = pl.multiple_of(t * vp, SUBLANES)
        kslab = pl.ds(krow, HS_WKV)
        for g in range(vl_n // SUBLANES):
            parts = [_fold8(s_scr[g * SUBLANES + i] * kk_scr[kslab, :]) for i in range(SUBLANES)]
            sa_scr[g * SUBLANES:(g + 1) * SUBLANES, :] = _rowsum8(parts)
        for g in range(vl_n // SUBLANES):
            parts = []
            for i in range(SUBLANES):
                vl = g * SUBLANES + i
                sa = sa_scr[vl:vl + 1, :]
                vv = v_scr[pl.ds(vrow + vl, 1), :]
                sn = s_scr[vl] * w_scr[kslab, :] - sa * b_scr[kslab, :] + vv * k_scr_[kslab, :]
                s_scr[vl] = sn
                parts.append(_fold8(sn * r_scr[kslab, :]))
            y_scr[pl.ds(pl.multiple_of(vrow + g * SUBLANES, SUBLANES), SUBLANES), :] = _rowsum8(parts)
        return carry

    lax.fori_loop(0, tc, step, 0)

    def emit_y(vl, carry):
        yt = y_scr[pl.ds(vl, tc, stride=vp), :].T
        for vsi in range(vs):
            row = pl.multiple_of((vsi * vl_n + vl) * H_WKV, H_WKV)
            for n, (s, j) in enumerate(seqs):
                lane0 = (vsi * len(seqs) + n) * H_WKV
                y_ref[0, 0, s * spb + j, pl.ds(row, H_WKV), :] = yt[lane0:lane0 + H_WKV, :]
        return carry

    lax.fori_loop(0, vl_n, emit_y, 0, unroll=4)
    sf_ref[0] = s_scr[...]


def _wkv_scan(rt, wt, kt, kkt, bt, vt, s0, *, tile0, n_seq, seq_tiles, spb):
    tc = WKV_TC
    if spb > 1:
        assert seq_tiles == 1 and n_seq % spb == 0 and tile0 % spb == 0
        nsrc, n_sb = 1, n_seq // spb
    else:
        nsrc, n_sb = n_seq, 1
    inst = nsrc * spb * H_WKV
    vs = LANES // inst
    vl_n = HS_WKV // vs
    assert vl_n % SUBLANES == 0, "value rows are processed eight at a time"
    cpt = TM // tc
    n_chunks = seq_tiles * cpt
    kp = HS_WKV + SUBLANES
    vp = vl_n + SUBLANES if ((vl_n + SUBLANES) // SUBLANES) % 2 else vl_n + 2 * SUBLANES

    def chunk(g, i):
        return jnp.where(g // n_sb == 1, n_chunks - 1 - i, i)

    def in_map(g, i, *, src, per_dir):
        ce = chunk(g, i)
        rb = (g // n_sb) if per_dir else 0
        if spb > 1:
            return (tile0 // spb + g % n_sb, rb, ce)
        return (tile0 + src * seq_tiles + ce // cpt, rb, ce % cpt)

    def out_map(g, i):
        ce = chunk(g, i)
        if spb > 1:
            return (g // n_sb, 0, g % n_sb, 0, ce)
        return (g // n_sb, ce // cpt, 0, 0, ce % cpt)

    in_specs, operands = [], []
    for arr, per_dir in ((rt, False), (wt, True), (kt, True), (kkt, False), (bt, True), (vt, False)):
        for src in range(nsrc):
            in_specs.append(pl.BlockSpec((spb, W_MIX, tc), functools.partial(in_map, src=src, per_dir=per_dir),
                                         pipeline_mode=pl.Buffered(1)))
            operands.append(arr)
    sspec = pl.BlockSpec((1, vl_n, HS_WKV, LANES), lambda g, i: (g, 0, 0, 0))
    in_specs.append(sspec)
    out_specs = [pl.BlockSpec((1, 1, nsrc * spb, W_MIX, tc), out_map)]
    out_shape = [jax.ShapeDtypeStruct((N_DIR, seq_tiles, n_seq, W_MIX, TM), F32)]
    res = pl.pallas_call(
        functools.partial(_wkv_body, nsrc=nsrc, spb=spb, vs=vs, n_sb=n_sb, tc=tc, kp=kp, vp=vp),
        grid=(N_DIR * n_sb, n_chunks),
        in_specs=in_specs,
        out_specs=out_specs + [sspec],
        out_shape=out_shape + [jax.ShapeDtypeStruct(s0.shape, F32)],
        scratch_shapes=([pltpu.VMEM((tc * kp, LANES), F32)] * 5
                        + [pltpu.VMEM((tc * vp, LANES), F32)] * 2
                        + [pltpu.VMEM((vl_n, HS_WKV, LANES), F32), pltpu.VMEM((vl_n, LANES), F32)]),
        compiler_params=_cparams(("parallel", "arbitrary"), VMEM_LIMIT),
        name="wkv_scan",
    )(*operands, s0)
    return res[0], res[1]


def _merge_body(x_ref, mod_ref, n1g_ref, n2g_ref, wg_ref, gb_ref, wbr_ref, wo_ref,
                lnxg_ref, lnxb_ref, ind_ref,
                ya_ref, yd_ref, gbg_ref, h_ref, ycf_ref, ycb_ref, ylf_ref, ylb_ref, g_ref, bon_ref,
                x1_ref, h2_ref, *, n_ctx_tiles):
    x = x_ref[...]
    m = mod_ref[0]
    ind = ind_ref[...]
    h = (_rms(x, n1g_ref[...]) * (1.0 + m[1:2, :]) + m[0:1, :]).astype(BF16)
    y_b = gbg_ref[...] * (h_ref[:, 0:W_MIX] + h_ref[:, W_MIX:2 * W_MIX])
    is_ctx = pl.program_id(0) < n_ctx_tiles
    y = jnp.where(is_ctx, ycf_ref[0, 0, 0] + ycb_ref[0, 0, 0], ylf_ref[0, 0, 0] + ylb_ref[0, 0, 0]).T
    yc = y - _segsum(y, ind) * (1.0 / HS_WKV)
    var = _segsum(yc * yc, ind) * (1.0 / HS_WKV)
    y_c = (yc * lax.rsqrt(var + LNX_EPS) * lnxg_ref[...] + lnxb_ref[...] + bon_ref[...]) * g_ref[...]
    merged = None
    for n, yn in enumerate((ya_ref[...], y_b, y_c, yd_ref[...])):
        cs = slice(n * D_MODEL, (n + 1) * D_MODEL)
        gate = jax.nn.sigmoid(jnp.dot(h, wg_ref[:, cs], preferred_element_type=F32) + gb_ref[:, cs])
        br = jnp.dot(yn.astype(BF16), wbr_ref[n * W_MIX:(n + 1) * W_MIX, :], preferred_element_type=F32)
        merged = gate * br if merged is None else merged + gate * br
    mo = jnp.dot(merged.astype(BF16), wo_ref[...], preferred_element_type=F32)
    x1 = x + m[2:3, :] * mo
    x1_ref[...] = x1
    h2_ref[...] = (_rms(x1, n2g_ref[...]) * (1.0 + m[4:5, :]) + m[3:4, :]).astype(BF16)


def _merge(x, mod, lw, tok_in, y_ctx, y_lat, n_ctx_tok, lat_len):
    t = x.shape[0]
    n_ctx_tiles = n_ctx_tok // TM
    tps = lat_len // TM
    midx = functools.partial(_mod_index, tm=TM, n_ctx_tok=n_ctx_tok, lat_len=lat_len)
    wnames = ["norm1_g", "norm2_g", "w_gate", "gate_b", "w_branch", "w_out", "lnx_g", "lnx_b", "ind"]
    wts = [lw[n] for n in wnames]
    tok = lambda wd: pl.BlockSpec((TM, wd), lambda i: (i, 0))
    ya, yd, gbg, h, g, bon = tok_in
    yblock = (1, 1, 1, W_MIX, TM)

    def ctx_spec(d):
        return pl.BlockSpec(yblock, lambda i: (d, 0, jnp.minimum(i, n_ctx_tiles - 1), 0, 0))

    def lat_spec(d):
        def imap(i):
            r = jnp.maximum(i - n_ctx_tiles, 0)
            return (d, lax.rem(r, tps), lax.div(r, tps), 0, 0)
        return pl.BlockSpec(yblock, imap)

    return pl.pallas_call(
        functools.partial(_merge_body, n_ctx_tiles=n_ctx_tiles),
        grid=(t // TM,),
        in_specs=([tok(D_MODEL), pl.BlockSpec((1, N_MOD, D_MODEL), lambda i: (midx(i), 0, 0))]
                  + [_const_spec(w.shape) for w in wts]
                  + [tok(W_MIX), tok(W_MIX), tok(W_MIX), tok(2 * W_MIX),
                     ctx_spec(0), ctx_spec(1), lat_spec(0), lat_spec(1), tok(W_MIX), tok(W_MIX)]),
        out_specs=[tok(D_MODEL), tok(D_MODEL)],
        out_shape=[jax.ShapeDtypeStruct((t, D_MODEL), F32), jax.ShapeDtypeStruct((t, D_MODEL), BF16)],
        compiler_params=_cparams(("parallel",), VMEM_LIMIT),
        name="merge",
    )(x, mod, *wts, ya, yd, gbg, h, y_ctx, y_ctx, y_lat, y_lat, g, bon)


_CAND_VALID = (8, 8, 8, 5, 4, 3, 2, 2, 2, 8)


def _route_head(qs, keys_ref):
    kio = lax.broadcasted_iota(I32, (N_KEYS, LANES), 0)
    sub = lax.broadcasted_iota(I32, (SUBLANES, LANES), 0)
    kid = lax.broadcasted_iota(I32, (PEER_TOPK, LANES), 0)
    neg = -jnp.inf

    def bc(x, r):
        return jnp.broadcast_to(x[r:r + 1, :], (SUBLANES, LANES))

    def head():
        tops = []
        for p in range(2):
            s = jnp.dot(keys_ref[p], qs[p], preferred_element_type=F32)
            vals = jnp.zeros((PEER_TOPK, LANES), F32)
            idxs = jnp.zeros((PEER_TOPK, LANES), I32)
            for r in range(PEER_TOPK):
                m = jnp.max(s, axis=0, keepdims=True)
                cand = jnp.where(s == m, kio, N_KEYS)
                ix = jnp.min(cand, axis=0, keepdims=True)
                s = jnp.where(cand == ix, neg, s)
                vals = jnp.where(kid == r, m, vals)
                idxs = jnp.where(kid == r, ix, idxs)
            tops.append((vals, idxs))
        (a0, i0), (a1, i1) = tops
        lo, hi = slice(0, SUBLANES), slice(SUBLANES, 2 * SUBLANES)
        slabs = [bc(a0, 0) + a1[lo], bc(a0, 0) + a1[hi]]
        ci = [bc(i0, 0), bc(i0, 0)]
        cj = [i1[lo], i1[hi]]
        for r in range(1, SUBLANES):
            slabs.append(bc(a0, r) + a1[lo])
            ci.append(bc(i0, r))
            cj.append(i1[lo])
        slabs.append(a0[hi] + bc(a1, 0))
        ci.append(i0[hi])
        cj.append(bc(i1, 0))
        slabs = [jnp.where(sub < nv, sl, neg) for sl, nv in zip(slabs, _CAND_VALID)]
        ids = [a * N_KEYS + b for a, b in zip(ci, cj)]
        vals = jnp.zeros((PEER_TOPK, LANES), F32)
        esel = jnp.zeros((PEER_TOPK, LANES), I32)
        for r in range(PEER_TOPK):
            level = list(zip(slabs, ids))
            while len(level) > 1:
                nxt = []
                for (va, ea), (vb, eb) in zip(level[0::2], level[1::2]):
                    take = vb > va
                    nxt.append((jnp.where(take, vb, va), jnp.where(take, eb, ea)))
                if len(level) % 2:
                    nxt.append(level[-1])
                level = nxt
            v8, e8 = level[0]
            for sh in (4, 2, 1):
                vr, er = pltpu.roll(v8, sh, 0), pltpu.roll(e8, sh, 0)
                take = vr > v8
                v8, e8 = jnp.where(take, vr, v8), jnp.where(take, er, e8)
            m, ex = v8[0:1, :], e8[0:1, :]
            slabs = [jnp.where(eid == ex, neg, sl) for sl, eid in zip(slabs, ids)]
            vals = jnp.where(kid == r, m, vals)
            esel = jnp.where(kid == r, ex, esel)
        e = jnp.exp(vals - vals[0:1, :])
        return esel, e / jnp.sum(e, axis=0, keepdims=True)

    return head()


def _peer_body(h2_ref, h2n_ref, wqt_ref, keys_ref, u_ref, v_ref, x1_ref, mod_ref, fng_ref, o_ref,
               q_scr, e_scr, g_scr, et_scr, gt_scr, gs_scr, acc_scr, *, rows, pitch, units, final):
    m = pl.program_id(0)
    e = pl.program_id(1)
    tm = h2_ref.shape[0]
    n_chunks = tm // LANES
    nsel = PEER_HEADS * PEER_TOPK
    slot = lax.rem(m, 2)

    def project_queries(src_ref):
        q = lax.dot_general(wqt_ref[...], src_ref[...], _NT, preferred_element_type=F32).astype(BF16)
        for c in range(n_chunks):
            q_scr[c] = q[:, c * LANES:(c + 1) * LANES]

    def route_unit(u, dst):
        c = u // PEER_HEADS
        h = lax.rem(u, PEER_HEADS)
        qs = [q_scr[c, pl.ds(pl.multiple_of(h * (2 * N_KEYS) + p * N_KEYS, N_KEYS), N_KEYS), :]
              for p in range(2)]
        esel, gates = _route_head(qs, keys_ref)
        row = pl.multiple_of(h * PEER_TOPK, PEER_TOPK)
        e_scr[dst, c, pl.ds(row, PEER_TOPK), :] = esel
        g_scr[dst, c, pl.ds(row, PEER_TOPK), :] = gates

    @pl.when(jnp.logical_and(e == 0, m == 0))
    def _first_tile_routing():
        project_queries(h2_ref)

        def unit(u, c):
            route_unit(u, 0)
            return c

        lax.fori_loop(0, n_chunks * PEER_HEADS, unit, 0)

    @pl.when(e == 0)
    def _build():
        for c in range(n_chunks):
            et_scr[c * LANES:(c + 1) * LANES, :] = e_scr[slot, c].T
            gt_scr[c * LANES:(c + 1) * LANES, :] = g_scr[slot, c].T
        kio = lax.broadcasted_iota(I32, (N_KEYS, nsel), 0)

        def tok(t, c):
            erow = et_scr[pl.ds(t, 1), :]
            grow = gt_scr[pl.ds(t, 1), :]
            at = jnp.where(kio == (erow >> 7), grow, 0.0).astype(BF16)
            bt = jnp.where(kio == (erow & (N_KEYS - 1)), 1.0, 0.0).astype(BF16)
            gt = lax.dot_general(at, bt, _NT, preferred_element_type=F32)
            hi = pltpu.bitcast(gt[0:rows, :], jnp.uint32) & jnp.uint32(0xFFFF0000)
            lo = pltpu.bitcast(gt[rows:2 * rows, :], jnp.uint32) >> 16
            gs_scr[pl.ds(pl.multiple_of(t * pitch, SUBLANES), rows), :] = hi | lo
            return c

        lax.fori_loop(0, tm, tok, 0, unroll=16)
        acc_scr[...] = jnp.zeros_like(acc_scr)
        project_queries(h2n_ref)

    for k in range(units):
        route_unit(e * units + k, 1 - slot)

    per_sub = PEER_SUB // N_KEYS
    per_step = u_ref.shape[0] // N_KEYS
    steps_per_half = rows // per_step
    row0 = lax.rem(e, steps_per_half) * per_step
    shift = jnp.where(e < steps_per_half, 0, 16).astype(jnp.uint32)
    h2 = h2_ref[...]
    total = None
    for sb in range(per_step // per_sub):
        es = slice(sb * PEER_SUB, (sb + 1) * PEER_SUB)
        hmat = lax.dot_general(h2, u_ref[es, :], _NT, preferred_element_type=F32)
        words = jnp.concatenate(
            [gs_scr[pl.ds(row0 + sb * per_sub + ii, tm, stride=pitch), :] for ii in range(per_sub)], axis=1)
        gm = pltpu.bitcast((words << shift) & jnp.uint32(0xFFFF0000), F32)
        act = (jax.nn.gelu(hmat) * gm).astype(BF16)
        part = jnp.dot(act, v_ref[es, :], preferred_element_type=F32)
        total = part if total is None else total + part
    acc_scr[...] += total

    @pl.when(e == pl.num_programs(1) - 1)
    def _out():
        x2 = x1_ref[...] + mod_ref[0][5:6, :] * acc_scr[...]
        o_ref[...] = _rms(x2, fng_ref[...]) if final else x2


def _peer(h2, wqt, keys, u, v, x1, mod, fng, n_ctx_tok, lat_len, final):
    t = h2.shape[0]
    nsel = PEER_HEADS * PEER_TOPK
    rows = N_KEYS // 2
    pitch = rows + SUBLANES
    n_e = (N_KEYS * N_KEYS) // PEER_EB
    n_m = t // TM_PEER
    n_chunks = TM_PEER // LANES
    units = (n_chunks * PEER_HEADS) // n_e
    assert units * n_e == n_chunks * PEER_HEADS
    midx = functools.partial(_mod_index, tm=TM_PEER, n_ctx_tok=n_ctx_tok, lat_len=lat_len)
    tok = lambda wd: pl.BlockSpec((TM_PEER, wd), lambda m, e: (m, 0))
    nxt = pl.BlockSpec((TM_PEER, D_MODEL), lambda m, e: (jnp.minimum(m + 1, n_m - 1), 0))
    espec = pl.BlockSpec((PEER_EB, D_MODEL), lambda m, e: (e, 0))
    single = dict(pipeline_mode=pl.Buffered(1))
    return pl.pallas_call(
        functools.partial(_peer_body, rows=rows, pitch=pitch, units=units, final=final),
        grid=(n_m, n_e),
        in_specs=[tok(D_MODEL), nxt,
                  pl.BlockSpec(wqt.shape, lambda m, e: (0, 0), **single),
                  pl.BlockSpec(keys.shape, lambda m, e: (0, 0, 0), **single),
                  espec, espec, tok(D_MODEL),
                  pl.BlockSpec((1, N_MOD, D_MODEL), lambda m, e: (midx(m), 0, 0)),
                  _const_spec((1, D_MODEL))],
        out_specs=tok(D_MODEL),
        out_shape=jax.ShapeDtypeStruct((t, D_MODEL), F32),
        scratch_shapes=[pltpu.VMEM((n_chunks, wqt.shape[0], LANES), BF16),
                        pltpu.VMEM((2, n_chunks, nsel, LANES), I32),
                        pltpu.VMEM((2, n_chunks, nsel, LANES), F32),
                        pltpu.VMEM((TM_PEER, nsel), I32),
                        pltpu.VMEM((TM_PEER, nsel), F32),
                        pltpu.VMEM((TM_PEER * pitch, N_KEYS), jnp.uint32),
                        pltpu.VMEM((TM_PEER, D_MODEL), F32)],
        compiler_params=_cparams(("arbitrary", "arbitrary"), VMEM_LIMIT),
        name="peer",
    )(h2, h2, wqt, keys, u, v, x1, mod, fng)


def _wkv_state_in(s, vs):
    n = s.shape[0]
    vl = HS_WKV // vs
    s = s.reshape(n, N_DIR, H_WKV, vs, vl, HS_WKV).transpose(1, 4, 5, 3, 0, 2)
    return s.reshape(N_DIR, vl, HS_WKV, vs * n * H_WKV)


def _wkv_state_out(s, n_sb, spb, vs):
    vl = HS_WKV // vs
    s = s.reshape(N_DIR, n_sb, vl, HS_WKV, vs, spb, H_WKV).transpose(1, 5, 0, 6, 4, 2, 3)
    return s.reshape(n_sb * spb, N_DIR, H_WKV, HS_WKV, HS_WKV)


def _layer_weights(i, prm):
    eye_h = jnp.eye(H_LRU, dtype=F32)
    eye_d = jnp.eye(N_DIR, dtype=F32)
    perm = _WKV_PERM

    def lru_bd(wt):
        return jnp.einsum("dhij,hg->hidgj", wt, eye_h).reshape(W_MIX, N_DIR * W_MIX)

    def lora_bd(wt):
        r = wt.shape[1]
        return jnp.einsum("drc,de->drec", wt, eye_d).reshape(N_DIR * r, N_DIR * W_MIX)

    w_in = prm["w_in"][i]
    pad = jnp.zeros((D_MODEL, Z_COLS - 5504), F32)
    rkv = [w_in[:, 2560 + j * W_MIX:2560 + (j + 1) * W_MIX][:, perm] for j in range(3)]
    w_in_perm = jnp.concatenate(
        [w_in[:, 0:1536]] + rkv + [w_in[:, 1536:2560], w_in[:, 4480:5504], w_in[:, 4096:4480], pad],
        axis=1).astype(BF16)
    row = lambda x: x.reshape(1, -1).astype(F32)
    head_of = np.arange(W_MIX) % H_WKV
    w_branch = prm["w_branch"][i]
    w_branch = jnp.concatenate([w_branch[0], w_branch[1], w_branch[2][perm, :], w_branch[3]], axis=0)
    return {
        "w_in": w_in_perm,
        "w_gate": w_in[:, 5504:].astype(BF16),
        "norm1_g": row(prm["norm1_g"][i]),
        "norm2_g": row(prm["norm2_g"][i]),
        "conv_a_w": prm["conv_a_w"][i],
        "conv_b_w": prm["conv_b_w"][i],
        "conv_b_b": row(prm["conv_b_b"][i]),
        "lru_w": jnp.concatenate([lru_bd(prm["lru_wa"][i]), lru_bd(prm["lru_wx"][i])], axis=1).astype(BF16),
        "lru_b": jnp.concatenate([row(prm["lru_ba"][i]), row(prm["lru_bx"][i])], axis=1),
        "lru_lam": row(prm["lru_lambda"][i]),
        "w0": row(prm["rwkv_w0"][i][:, perm]),
        "w2": lora_bd(prm["rwkv_w2"][i][:, :, perm]).astype(BF16),
        "a0": row(prm["rwkv_a0"][i][:, perm]),
        "a2": lora_bd(prm["rwkv_a2"][i][:, :, perm]).astype(BF16),
        "g2": prm["rwkv_g2"][i][:, perm].astype(BF16),
        "kk": row(prm["rwkv_kk"][i][perm]),
        "ka": row(prm["rwkv_ka"][i][perm]),
        "rk": row(prm["rwkv_rk"][i].reshape(W_MIX)[perm]),
        "lnx_g": row(prm["lnx_g"][i][perm]),
        "lnx_b": row(prm["lnx_b"][i][perm]),
        "sg_ln_g": row(prm["sg_ln_g"][i]),
        "sg_ln_b": row(prm["sg_ln_b"][i]),
        "sg_ws": prm["sg_ws"][i].astype(BF16),
        "sg_bst": prm["sg_bs"][i].T,
        "gate_b": row(prm["gate_b"][i]),
        "w_branch": w_branch.astype(BF16),
        "w_out": prm["w_out"][i].astype(BF16),
        "wq_t": prm["peer_wq"][i].T.astype(BF16),
        "keys": prm["peer_keys"][i].astype(BF16),
        "peer_u": prm["peer_u"][i].astype(BF16),
        "peer_v": prm["peer_v"][i].astype(BF16),
        "ind": jnp.asarray(head_of[:, None] == head_of[None, :], BF16),
    }


def kernel(x_prompt, x_sample, state_lru, state_wkv, c, c_ctx, norm1_g, norm2_g, w_mod, b_mod, w_in, conv_a_w, conv_b_w, conv_b_b, lru_wa, lru_ba, lru_wx, lru_bx, lru_lambda, rwkv_w0, rwkv_w2, rwkv_a0, rwkv_a2, rwkv_g2, rwkv_kk, rwkv_ka, rwkv_rk, lnx_g, lnx_b, sg_ln_g, sg_ln_b, sg_ws, sg_bs, gate_b, w_branch, w_out, peer_wq, peer_keys, peer_u, peer_v, final_norm_g):
    prm = dict(norm1_g=norm1_g, norm2_g=norm2_g, w_in=w_in, conv_a_w=conv_a_w, conv_b_w=conv_b_w,
               conv_b_b=conv_b_b, lru_wa=lru_wa, lru_ba=lru_ba, lru_wx=lru_wx, lru_bx=lru_bx,
               lru_lambda=lru_lambda, rwkv_w0=rwkv_w0, rwkv_w2=rwkv_w2, rwkv_a0=rwkv_a0, rwkv_a2=rwkv_a2,
               rwkv_g2=rwkv_g2, rwkv_kk=rwkv_kk, rwkv_ka=rwkv_ka, rwkv_rk=rwkv_rk, lnx_g=lnx_g,
               lnx_b=lnx_b, sg_ln_g=sg_ln_g, sg_ln_b=sg_ln_b, sg_ws=sg_ws, sg_bs=sg_bs, gate_b=gate_b,
               w_branch=w_branch, w_out=w_out, peer_wq=peer_wq, peer_keys=peer_keys, peer_u=peer_u,
               peer_v=peer_v)
    bc, lc, _ = x_prompt.shape
    bl, ll, _ = x_sample.shape
    depth = w_mod.shape[0]
    n_ctx_tok = bc * lc
    n_ctx_tiles = n_ctx_tok // TM
    lat_tiles = ll // TM
    ctx_spb = min(WKV_CTX_SPB, bc)
    lru_spb = ll // lc
    assert lc == TM and ll % TM_PEER == 0 and n_ctx_tok % TM_PEER == 0 and bl + 1 <= SUBLANES
    assert ll % GRID_W == 0 and TM % GRID_W == 0 and bc % ctx_spb == 0 and n_ctx_tok % ll == 0
    assert LANES % (ctx_spb * H_WKV) == 0 and LANES % (bl * H_WKV) == 0

    cond = jnp.zeros((SUBLANES, D_MODEL), F32).at[0].set(c_ctx).at[1:1 + bl].set(c)
    mods = _modulation(cond, w_mod, b_mod).reshape(depth, SUBLANES, N_MOD, D_MODEL)
    fng = final_norm_g.reshape(1, D_MODEL)
    x = jnp.concatenate([x_prompt.reshape(n_ctx_tok, D_MODEL), x_sample.reshape(bl * ll, D_MODEL)], axis=0)
    ctx_vs = LANES // (ctx_spb * H_WKV)
    lat_vs = LANES // (bl * H_WKV)
    n_sb = bc // ctx_spb
    wkv_zero = jnp.zeros((N_DIR * n_sb, HS_WKV // ctx_vs, HS_WKV, LANES), F32)
    lru_zero = jnp.zeros((n_ctx_tok // ll, lru_spb, N_DIR * W_MIX), F32)
    new_lru, new_wkv = [], []
    pnames = ["ya", "yd", "gbg", "la", "lu", "g", "bon", "rt", "vt", "kkt", "wt", "kt", "bt"]
    for i in range(depth):
        lw = _layer_weights(i, prm)
        mod = mods[i]
        p = dict(zip(pnames, _prep(x, mod, lw, n_ctx_tok, ll)))

        lat_h0 = jnp.zeros((bl, lru_spb, N_DIR * W_MIX), F32).at[:, 0].set(
            state_lru[:, i].astype(F32).reshape(bl, N_DIR * W_MIX))
        h, lru_s = _lru_scan(p["la"], p["lu"], jnp.concatenate([lru_zero, lat_h0], axis=0),
                             rows=ll, n_ctx_blocks=n_ctx_tok // ll, ctx_cfg=(lru_spb, lc), lat_cfg=(1, ll))
        new_lru.append(lru_s[:n_ctx_tok // ll].reshape(bc, N_DIR, W_MIX))

        wkv_in = [p[n] for n in ("rt", "wt", "kt", "kkt", "bt", "vt")]
        y_c, s_c = _wkv_scan(*wkv_in, wkv_zero, tile0=0, n_seq=bc, seq_tiles=1, spb=ctx_spb)
        y_l, _ = _wkv_scan(*wkv_in, _wkv_state_in(state_wkv[:, i].astype(F32), lat_vs),
                           tile0=n_ctx_tiles, n_seq=bl, seq_tiles=lat_tiles, spb=1)
        new_wkv.append(_wkv_state_out(s_c, n_sb, ctx_spb, ctx_vs))

        tok_in = [p["ya"], p["yd"], p["gbg"], h, p["g"], p["bon"]]
        x1, h2 = _merge(x, mod, lw, tok_in, y_c, y_l, n_ctx_tok, ll)
        x = _peer(h2, lw["wq_t"], lw["keys"], lw["peer_u"], lw["peer_v"], x1, mod, fng,
                        n_ctx_tok, ll, final=(i == depth - 1))
    y_prompt = x[:n_ctx_tok].reshape(bc, lc, D_MODEL)
    y_sample = x[n_ctx_tok:].reshape(bl, ll, D_MODEL)
    return (y_prompt, y_sample, jnp.stack(new_lru, axis=1), jnp.stack(new_wkv, axis=1))
```

```python
import functools

import numpy as np
import jax
import jax.numpy as jnp
from jax import lax
from jax.experimental import pallas as pl
from jax.experimental.pallas import tpu as pltpu

F32 = jnp.float32
BF16 = jnp.bfloat16
I32 = jnp.int32

D_MODEL = 1024
W_MIX = 512
N_DIR = 2
N_BRANCH = 4
H_WKV = 8
HS_WKV = 64
H_LRU = 8
HB_LRU = 64
LORA_W = 64
LORA_A = 64
LORA_G = 128
GRID_W = 64
CHUNK = 128
G_SG = 4
N_KEYS = 128
PEER_HEADS = 8
PEER_TOPK = 16
N_MOD = 6
EPS = 1e-6
LNX_EPS = 64e-5
LRU_C = 8.0

LANES = 128
SUBLANES = 8
TM = 256
TM_PEER = 512
PEER_EB = 1024
PEER_SUB = 512
Z_COLS = 5632
WKV_TC = LANES
WKV_CTX_SPB = 8
VMEM_LIMIT = 56 * 1024 * 1024

_NT = (((1,), (1,)), ((), ()))
_WKV_PERM = np.array([(n % H_WKV) * HS_WKV + n // H_WKV for n in range(W_MIX)])


def _cparams(sem, vmem=None):
    return pltpu.CompilerParams(dimension_semantics=sem, vmem_limit_bytes=vmem)


def _const_spec(shape):
    nd = len(shape)
    return pl.BlockSpec(shape, lambda *_: (0,) * nd)


def _softplus(x):
    return jnp.maximum(x, 0.0) + jnp.log1p(jnp.exp(-jnp.abs(x)))


def _rms(x, g):
    return x * lax.rsqrt(jnp.mean(x * x, axis=-1, keepdims=True) + EPS) * g


def _segsum(x, ind):
    hi = x.astype(BF16)
    lo = (x - hi.astype(F32)).astype(BF16)
    return (jnp.dot(hi, ind, preferred_element_type=F32)
            + jnp.dot(lo, ind, preferred_element_type=F32))


def _mod_index(i, tm, n_ctx_tok, lat_len):
    n_ctx_tiles = n_ctx_tok // tm
    tiles_per_seq = lat_len // tm
    return jnp.where(i < n_ctx_tiles, 0, 1 + lax.div(i - n_ctx_tiles, tiles_per_seq))


def _mod_body(s_ref, w_ref, b_ref, o_ref):
    s = s_ref[...]
    s = s * jax.nn.sigmoid(s)
    o_ref[0] = jnp.dot(s.astype(BF16), w_ref[0].astype(BF16), preferred_element_type=F32) + b_ref[0]


def _modulation(cond, w_mod, b_mod):
    depth = w_mod.shape[0]
    n = w_mod.shape[2]
    tn = 1536
    return pl.pallas_call(
        _mod_body,
        grid=(depth, n // tn),
        in_specs=[_const_spec((SUBLANES, D_MODEL)),
                  pl.BlockSpec((1, D_MODEL, tn), lambda l, j: (l, 0, j)),
                  pl.BlockSpec((1, 1, tn), lambda l, j: (l, 0, j))],
        out_specs=pl.BlockSpec((1, SUBLANES, tn), lambda l, j: (l, 0, j)),
        out_shape=jax.ShapeDtypeStruct((depth, SUBLANES, n), F32),
        compiler_params=_cparams(("parallel", "parallel"), VMEM_LIMIT),
        name="modulation",
    )(cond, w_mod, b_mod.reshape(depth, 1, n))


def _prep_body(xc_ref, xl_ref, xp_ref, xn_ref, mod_ref, n1g_ref, win_ref,
               caw_ref, cbw_ref, cbb_ref, lruw_ref, lrub_ref, lam_ref,
               w0_ref, w2_ref, a0_ref, a2_ref, g2_ref, kkw_ref, ka_ref, rk_ref,
               lng_ref, lnb_ref, ws_ref, bst_ref, ind_ref,
               ya_ref, yd_ref, gbg_ref, la_ref, lu_ref, g_ref, bon_ref,
               rt_ref, vt_ref, kkt_ref, wt_ref, kt_ref, bt_ref,
               *, n_ctx_tiles, tiles_per_seq):
    i = pl.program_id(0)
    is_ctx = i < n_ctx_tiles
    t = lax.broadcasted_iota(I32, (TM, 1), 0)
    ind = ind_ref[...]
    m = mod_ref[0]

    def modulated(xv):
        return (_rms(xv, n1g_ref[...]) * (1.0 + m[1:2, :]) + m[0:1, :]).astype(BF16)

    def project(hv, lo, hi):
        return jnp.dot(hv, win_ref[:, lo:hi], preferred_element_type=F32)

    h = modulated(jnp.where(is_ctx, xc_ref[...], xl_ref[...]))
    za = project(h, 0, 1536)
    zc = project(h, 1536, 3072)
    zb = project(h, 3072, 4096)
    zd = project(h, 4096, 5120)
    zl = project(h, 5120, 5120 + 2 * LORA_W + 2 * LORA_A + LORA_G)
    halo_prev = project(modulated(xp_ref[...]), 3072 + W_MIX, 4096)
    halo_next = project(modulated(xn_ref[...]), 3072 + W_MIX, 4096)

    pm = jnp.where(is_ctx, TM - 1, GRID_W - 1)
    pos = t & pm
    a_b = za[:, 0:W_MIX]
    ac = za[:, W_MIX:2 * W_MIX] * za[:, 2 * W_MIX:3 * W_MIX]
    up = jnp.where(pos == 0, 0.0, pltpu.roll(ac, 1, 0))
    dn = jnp.where(pos == pm, 0.0, pltpu.roll(ac, TM - 1, 0))
    ya_ref[...] = a_b * (caw_ref[0:1, :] * up + caw_ref[1:2, :] * ac + caw_ref[2:3, :] * dn)

    seq_tile = lax.rem(jnp.maximum(i - n_ctx_tiles, 0), tiles_per_seq)
    first = jnp.logical_or(is_ctx, seq_tile == 0)
    last = jnp.logical_or(is_ctx, seq_tile == tiles_per_seq - 1)
    prev = jnp.where(first, 0.0, halo_prev[SUBLANES - 1:SUBLANES, :])
    nxt0 = jnp.where(last, 0.0, halo_next[0:1, :])
    nxt1 = jnp.where(last, 0.0, halo_next[1:2, :])
    bx = zb[:, W_MIX:2 * W_MIX]
    m1 = jnp.where(t == 0, prev, pltpu.roll(bx, 1, 0))
    p1 = jnp.where(t == TM - 1, nxt0, pltpu.roll(bx, TM - 1, 0))
    p2 = jnp.where(t == TM - 2, nxt0, jnp.where(t == TM - 1, nxt1, pltpu.roll(bx, TM - 2, 0)))
    xb = (cbw_ref[0:1, :] * m1 + cbw_ref[1:2, :] * bx + cbw_ref[2:3, :] * p1
          + cbw_ref[3:4, :] * p2 + cbb_ref[...])
    gates = jnp.dot(xb.astype(BF16), lruw_ref[...], preferred_element_type=F32) + lrub_ref[...]
    rg = jax.nn.sigmoid(gates[:, 0:2 * W_MIX])
    ig = jax.nn.sigmoid(gates[:, 2 * W_MIX:4 * W_MIX])
    log_a = -LRU_C * rg * _softplus(-lam_ref[...])
    xb2 = jnp.concatenate([xb, xb], axis=1)
    la_ref[...] = jnp.exp(log_a)
    lu_ref[...] = jnp.sqrt(jnp.tanh(-log_a) * (jnp.exp(2.0 * log_a) + 1.0)) * (ig * xb2)
    gbg_ref[...] = jax.nn.gelu(zb[:, 0:W_MIX])

    zr = zc[:, 0:W_MIX]
    zk = zc[:, W_MIX:2 * W_MIX]
    zv = zc[:, 2 * W_MIX:3 * W_MIX]
    zwd = zl[:, 0:2 * LORA_W]
    zad = zl[:, 2 * LORA_W:2 * LORA_W + 2 * LORA_A]
    zgd = zl[:, 2 * LORA_W + 2 * LORA_A:2 * LORA_W + 2 * LORA_A + LORA_G]
    wlin = w0_ref[...] + jnp.dot(jnp.tanh(zwd).astype(BF16), w2_ref[...], preferred_element_type=F32)
    wt_ref[0] = jnp.exp(-jnp.exp(-_softplus(-wlin) - 0.5)).T
    av = jax.nn.sigmoid(a0_ref[...] + jnp.dot(zad.astype(BF16), a2_ref[...], preferred_element_type=F32))
    g_ref[...] = jnp.dot(jax.nn.sigmoid(zgd).astype(BF16), g2_ref[...], preferred_element_type=F32)
    kkr = zk * kkw_ref[...]
    kkn = kkr / jnp.maximum(jnp.sqrt(_segsum(kkr * kkr, ind)), 1e-12)
    zk2 = jnp.concatenate([zk, zk], axis=1)
    ka2 = jnp.concatenate([ka_ref[...], ka_ref[...]], axis=1)
    kd = zk2 * (1.0 + (av - 1.0) * ka2)
    kt_ref[0] = kd.T
    bt_ref[0] = (jnp.concatenate([kkn, kkn], axis=1) * av).T
    rt_ref[0] = zr.T
    vt_ref[0] = zv.T
    kkt_ref[0] = kkn.T
    bon_ref[...] = _segsum(zr * (kd[:, 0:W_MIX] + kd[:, W_MIX:2 * W_MIX]) * rk_ref[...], ind) * zv

    zg = jax.nn.gelu(zd)
    u = zg[:, 0:W_MIX]
    vv = zg[:, W_MIX:2 * W_MIX]
    vc = vv - jnp.mean(vv, axis=-1, keepdims=True)
    vn = vc * lax.rsqrt(jnp.mean(vc * vc, axis=-1, keepdims=True) + 1e-5) * lng_ref[...] + lnb_ref[...]
    for c in range(TM // CHUNK):
        rs = slice(c * CHUNK, (c + 1) * CHUNK)
        for gi in range(G_SG):
            cs = slice(gi * LANES, (gi + 1) * LANES)
            s = jnp.dot(ws_ref[gi], vn[rs, cs].astype(BF16), preferred_element_type=F32)
            yd_ref[rs, cs] = u[rs, cs] * (s + bst_ref[:, gi:gi + 1])


def _dual_specs(rows, n_ctx_blocks, **kw):
    return [pl.BlockSpec((rows, D_MODEL), lambda i, *_: (jnp.minimum(i, n_ctx_blocks - 1), 0), **kw),
            pl.BlockSpec((rows, D_MODEL), lambda i, *_: (jnp.maximum(i - n_ctx_blocks, 0), 0), **kw)]


def _prep(x_ctx, x_lat, mod, lw, lat_len):
    n_ctx_tok = x_ctx.shape[0]
    t = n_ctx_tok + x_lat.shape[0]
    n_tiles = t // TM
    n_ctx_tiles = n_ctx_tok // TM
    tiles_per_seq = lat_len // TM
    rows8 = TM // SUBLANES
    last_blk = x_lat.shape[0] // SUBLANES - 1
    midx = functools.partial(_mod_index, tm=TM, n_ctx_tok=n_ctx_tok, lat_len=lat_len)

    def lat_blk8(i, off):
        return (jnp.clip((i - n_ctx_tiles) * rows8 + off, 0, last_blk), 0)

    x_specs = _dual_specs(TM, n_ctx_tiles) + [
        pl.BlockSpec((SUBLANES, D_MODEL), lambda i: lat_blk8(i, -1)),
        pl.BlockSpec((SUBLANES, D_MODEL), lambda i: lat_blk8(i, rows8)),
        pl.BlockSpec((1, N_MOD, D_MODEL), lambda i: (midx(i), 0, 0)),
        _const_spec((1, D_MODEL)),
        pl.BlockSpec(lw["w_in"].shape, lambda i: (0, 0), pipeline_mode=pl.Buffered(1)),
    ]
    wnames = ["conv_a_w", "conv_b_w", "conv_b_b", "lru_w", "lru_b", "lru_lam", "w0", "w2", "a0", "a2",
              "g2", "kk", "ka", "rk", "sg_ln_g", "sg_ln_b", "sg_ws", "sg_bst", "ind"]
    wts = [lw[n] for n in wnames]
    w_specs = [_const_spec(w.shape) for w in wts]
    widths = [W_MIX, W_MIX, W_MIX, 2 * W_MIX, 2 * W_MIX, W_MIX, W_MIX]
    t_rows = [W_MIX, W_MIX, W_MIX, 2 * W_MIX, 2 * W_MIX, 2 * W_MIX]
    out_specs = ([pl.BlockSpec((TM, wd), lambda i: (i, 0)) for wd in widths]
                 + [pl.BlockSpec((1, r, TM), lambda i: (i, 0, 0)) for r in t_rows])
    out_shape = ([jax.ShapeDtypeStruct((t, wd), F32) for wd in widths]
                 + [jax.ShapeDtypeStruct((n_tiles, r, TM), F32) for r in t_rows])
    return pl.pallas_call(
        functools.partial(_prep_body, n_ctx_tiles=n_ctx_tiles, tiles_per_seq=tiles_per_seq),
        grid=(n_tiles,),
        in_specs=x_specs + w_specs,
        out_specs=out_specs,
        out_shape=out_shape,
        compiler_params=_cparams(("parallel",), VMEM_LIMIT),
        name="branch_prep",
    )(x_ctx, x_lat, x_lat, x_lat, mod, lw["norm1_g"], lw["w_in"], *wts)


def _lru_body(a_ref, u_ref, h0_ref, h_ref, hf_ref, *, n_ctx_blocks, ctx_cfg, lat_cfg):
    fw, bw = slice(0, W_MIX), slice(W_MIX, 2 * W_MIX)

    def scan(nseq, l):
        def step(s, carry):
            out = []
            for j in range(nseq):
                tf = j * l + s
                tb = j * l + (l - 1 - s)
                hf = a_ref[pl.ds(tf, 1), fw] * carry[2 * j] + u_ref[pl.ds(tf, 1), fw]
                hb = a_ref[pl.ds(tb, 1), bw] * carry[2 * j + 1] + u_ref[pl.ds(tb, 1), bw]
                h_ref[pl.ds(tf, 1), fw] = hf
                h_ref[pl.ds(tb, 1), bw] = hb
                out += [hf, hb]
            return tuple(out)

        init = []
        for j in range(nseq):
            init += [h0_ref[0, j:j + 1, fw], h0_ref[0, j:j + 1, bw]]
        fin = lax.fori_loop(0, l, step, tuple(init), unroll=2)
        hf_ref[0] = h0_ref[0]
        for j in range(nseq):
            hf_ref[0, j:j + 1, fw] = fin[2 * j]
            hf_ref[0, j:j + 1, bw] = fin[2 * j + 1]

    is_ctx = pl.program_id(0) < n_ctx_blocks
    pl.when(is_ctx)(lambda: scan(*ctx_cfg))
    pl.when(jnp.logical_not(is_ctx))(lambda: scan(*lat_cfg))


def _lru_scan(a, u, h0, *, rows, n_ctx_blocks, ctx_cfg, lat_cfg):
    nb = a.shape[0] // rows
    w = a.shape[1]
    tok = pl.BlockSpec((rows, w), lambda i: (i, 0))
    st = pl.BlockSpec((1,) + h0.shape[1:], lambda i: (i, 0, 0))
    return pl.pallas_call(
        functools.partial(_lru_body, n_ctx_blocks=n_ctx_blocks, ctx_cfg=ctx_cfg, lat_cfg=lat_cfg),
        grid=(nb,),
        in_specs=[tok, tok, st],
        out_specs=[tok, st],
        out_shape=[jax.ShapeDtypeStruct(a.shape, F32), jax.ShapeDtypeStruct(h0.shape, F32)],
        compiler_params=_cparams(("parallel",), VMEM_LIMIT),
        name="lru_scan",
    )(a, u, h0)


_SLOT_ORDER = (0, 4, 2, 6, 1, 5, 3, 7)


def _rowsum8(parts):
    sub = lax.broadcasted_iota(I32, (SUBLANES, LANES), 0)
    slots = [parts[i] for i in _SLOT_ORDER]
    roll = pltpu.roll
    lvl1 = [jnp.where(sub < 4, a + roll(a, 4, 0), b + roll(b, 4, 0))
            for a, b in zip(slots[0::2], slots[1::2])]
    lvl2 = [jnp.where((sub & 3) < 2, a + roll(a, 6, 0), roll(b + roll(b, 6, 0), 2, 0))
            for a, b in zip(lvl1[0::2], lvl1[1::2])]
    a, b = lvl2
    return jnp.where((sub & 1) == 0, a + roll(a, 7, 0), roll(b + roll(b, 7, 0), 1, 0))


def _fold8(x):
    return jnp.sum(x.reshape(HS_WKV // SUBLANES, SUBLANES, LANES), axis=0)


def _wkv_body(*refs, nsrc, spb, vs, n_sb, tc, kp, vp):
    vl_n = HS_WKV // vs
    n_in = 6 * nsrc
    k_srcs = [refs[o * nsrc:(o + 1) * nsrc] for o in range(5)]
    v_srcs = refs[5 * nsrc:n_in]
    s0_ref = refs[n_in]
    y_ref = refs[n_in + 1]
    sf_ref = refs[n_in + 2]
    k_scr = refs[n_in + 3:n_in + 8]
    v_scr, y_scr, s_scr, sa_scr = refs[n_in + 8:n_in + 12]
    r_scr, w_scr, k_scr_, kk_scr, b_scr = k_scr
    backward = pl.program_id(0) // n_sb == 1
    seqs = [(s, j) for s in range(nsrc) for j in range(spb)]

    @pl.when(pl.program_id(1) == 0)
    def _():
        s_scr[...] = s0_ref[0]

    def build_k(c, carry):
        row = pl.multiple_of(c * H_WKV, H_WKV)
        for o in range(5):
            slab = [k_srcs[o][s][j, pl.ds(row, H_WKV), :] for s, j in seqs]
            k_scr[o][pl.ds(c, tc, stride=kp), :] = jnp.concatenate(slab * vs, axis=0).T
        return carry

    lax.fori_loop(0, HS_WKV, build_k, 0, unroll=4)

    def build_v(vl, carry):
        slab = []
        for vsi in range(vs):
            row = pl.multiple_of((vsi * vl_n + vl) * H_WKV, H_WKV)
            slab += [v_srcs[s][j, pl.ds(row, H_WKV), :] for s, j in seqs]
        v_scr[pl.ds(vl, tc, stride=vp), :] = jnp.concatenate(slab, axis=0).T
        return carry

    lax.fori_loop(0, vl_n, build_v, 0, unroll=4)

    def step(s, carry):
        t = jnp.where(backward, tc - 1 - s, s)
        krow = pl.multiple_of(t * kp, SUBLANES)
        vrow = pl.multiple_of(t * vp, SUBLANES)
        kslab = pl.ds(krow, HS_WKV)
        for g in range(vl_n // SUBLANES):
            parts = [_fold8(s_scr[g * SUBLANES + i] * kk_scr[kslab, :]) for i in range(SUBLANES)]
            sa_scr[g * SUBLANES:(g + 1) * SUBLANES, :] = _rowsum8(parts)
        for g in range(vl_n // SUBLANES):
            parts = []
            for i in range(SUBLANES):
                vl = g * SUBLANES + i
                sa = sa_scr[vl:vl + 1, :]
                vv = v_scr[pl.ds(vrow + vl, 1), :]
                sn = s_scr[vl] * w_scr[kslab, :] - sa * b_scr[kslab, :] + vv * k_scr_[kslab, :]
                s_scr[vl] = sn
                parts.append(_fold8(sn * r_scr[kslab, :]))
            y_scr[pl.ds(pl.multiple_of(vrow + g * SUBLANES, SUBLANES), SUBLANES), :] = _rowsum8(parts)
        return carry

    lax.fori_loop(0, tc, step, 0)

    def emit_y(vl, carry):
        yt = y_scr[pl.ds(vl, tc, stride=vp), :].T
        for vsi in range(vs):
            row = pl.multiple_of((vsi * vl_n + vl) * H_WKV, H_WKV)
            for n, (s, j) in enumerate(seqs):
                lane0 = (vsi * len(seqs) + n) * H_WKV
                y_ref[0, 0, s * spb + j, pl.ds(row, H_WKV), :] = yt[lane0:lane0 + H_WKV, :]
        return carry

    lax.fori_loop(0, vl_n, emit_y, 0, unroll=4)
    sf_ref[0] = s_scr[...]


def _wkv_scan(rt, wt, kt, kkt, bt, vt, s0, *, tile0, n_seq, seq_tiles, spb):
    tc = WKV_TC
    if spb > 1:
        assert seq_tiles == 1 and n_seq % spb == 0 and tile0 % spb == 0
        nsrc, n_sb = 1, n_seq // spb
    else:
        nsrc, n_sb = n_seq, 1
    inst = nsrc * spb * H_WKV
    vs = LANES // inst
    vl_n = HS_WKV // vs
    assert vl_n % SUBLANES == 0, "value rows are processed eight at a time"
    cpt = TM // tc
    n_chunks = seq_tiles * cpt
    kp = HS_WKV + SUBLANES
    vp = vl_n + SUBLANES if ((vl_n + SUBLANES) // SUBLANES) % 2 else vl_n + 2 * SUBLANES

    def chunk(g, i):
        return jnp.where(g // n_sb == 1, n_chunks - 1 - i, i)

    def in_map(g, i, *, src, per_dir):
        ce = chunk(g, i)
        rb = (g // n_sb) if per_dir else 0
        if spb > 1:
            return (tile0 // spb + g % n_sb, rb, ce)
        return (tile0 + src * seq_tiles + ce // cpt, rb, ce % cpt)

    def out_map(g, i):
        ce = chunk(g, i)
        if spb > 1:
            return (g // n_sb, 0, g % n_sb, 0, ce)
        return (g // n_sb, ce // cpt, 0, 0, ce % cpt)

    in_specs, operands = [], []
    for arr, per_dir in ((rt, False), (wt, True), (kt, True), (kkt, False), (bt, True), (vt, False)):
        for src in range(nsrc):
            in_specs.append(pl.BlockSpec((spb, W_MIX, tc), functools.partial(in_map, src=src, per_dir=per_dir),
                                         pipeline_mode=pl.Buffered(1)))
            operands.append(arr)
    sspec = pl.BlockSpec((1, vl_n, HS_WKV, LANES), lambda g, i: (g, 0, 0, 0))
    in_specs.append(sspec)
    out_specs = [pl.BlockSpec((1, 1, nsrc * spb, W_MIX, tc), out_map)]
    out_shape = [jax.ShapeDtypeStruct((N_DIR, seq_tiles, n_seq, W_MIX, TM), F32)]
    res = pl.pallas_call(
        functools.partial(_wkv_body, nsrc=nsrc, spb=spb, vs=vs, n_sb=n_sb, tc=tc, kp=kp, vp=vp),
        grid=(N_DIR * n_sb, n_chunks),
        in_specs=in_specs,
        out_specs=out_specs + [sspec],
        out_shape=out_shape + [jax.ShapeDtypeStruct(s0.shape, F32)],
        scratch_shapes=([pltpu.VMEM((tc * kp, LANES), F32)] * 5
                        + [pltpu.VMEM((tc * vp, LANES), F32)] * 2
                        + [pltpu.VMEM((vl_n, HS_WKV, LANES), F32), pltpu.VMEM((vl_n, LANES), F32)]),
        compiler_params=_cparams(("parallel", "arbitrary"), VMEM_LIMIT),
        name="wkv_scan",
    )(*operands, s0)
    return res[0], res[1]


def _merge_body(xc_ref, xl_ref, mod_ref, n1g_ref, n2g_ref, wg_ref, gb_ref, wbr_ref, wo_ref,
                lnxg_ref, lnxb_ref, ind_ref,
                ya_ref, yd_ref, gbg_ref, h_ref, ycf_ref, ycb_ref, ylf_ref, ylb_ref, g_ref, bon_ref,
                x1_ref, h2_ref, *, n_ctx_tiles):
    is_ctx = pl.program_id(0) < n_ctx_tiles
    x = jnp.where(is_ctx, xc_ref[...], xl_ref[...])
    m = mod_ref[0]
    ind = ind_ref[...]
    h = (_rms(x, n1g_ref[...]) * (1.0 + m[1:2, :]) + m[0:1, :]).astype(BF16)
    y_b = gbg_ref[...] * (h_ref[:, 0:W_MIX] + h_ref[:, W_MIX:2 * W_MIX])
    y =jnp.where(is_ctx, ycf_ref[0, 0, 0] + ycb_ref[0, 0, 0], ylf_ref[0, 0, 0] + ylb_ref[0, 0, 0]).T
    yc = y - _segsum(y, ind) * (1.0 / HS_WKV)
    var = _segsum(yc * yc, ind) * (1.0 / HS_WKV)
    y_c = (yc * lax.rsqrt(var + LNX_EPS) * lnxg_ref[...] + lnxb_ref[...] + bon_ref[...]) * g_ref[...]
    merged = None
    for n, yn in enumerate((ya_ref[...], y_b, y_c, yd_ref[...])):
        cs = slice(n * D_MODEL, (n + 1) * D_MODEL)
        gate = jax.nn.sigmoid(jnp.dot(h, wg_ref[:, cs], preferred_element_type=F32) + gb_ref[:, cs])
        br = jnp.dot(yn.astype(BF16), wbr_ref[n * W_MIX:(n + 1) * W_MIX, :], preferred_element_type=F32)
        merged = gate * br if merged is None else merged + gate * br
    mo = jnp.dot(merged.astype(BF16), wo_ref[...], preferred_element_type=F32)
    x1 = x + m[2:3, :] * mo
    x1_ref[...] = x1
    h2_ref[...] = (_rms(x1, n2g_ref[...]) * (1.0 + m[4:5, :]) + m[3:4, :]).astype(BF16)


def _merge(x_ctx, x_lat, mod, lw, tok_in, y_ctx, y_lat, lat_len):
    n_ctx_tok = x_ctx.shape[0]
    t = n_ctx_tok + x_lat.shape[0]
    n_ctx_tiles = n_ctx_tok // TM
    tps = lat_len // TM
    midx = functools.partial(_mod_index, tm=TM, n_ctx_tok=n_ctx_tok, lat_len=lat_len)
    wnames = ["norm1_g", "norm2_g", "w_gate", "gate_b", "w_branch", "w_out", "lnx_g", "lnx_b", "ind"]
    wts = [lw[n] for n in wnames]
    tok = lambda wd: pl.BlockSpec((TM, wd), lambda i: (i, 0))
    ya, yd, gbg, h, g, bon = tok_in
    yblock = (1, 1, 1, W_MIX, TM)

    def ctx_spec(d):
        return pl.BlockSpec(yblock, lambda i: (d, 0, jnp.minimum(i, n_ctx_tiles - 1), 0, 0))

    def lat_spec(d):
        def imap(i):
            r = jnp.maximum(i - n_ctx_tiles, 0)
            return (d, lax.rem(r, tps), lax.div(r, tps), 0, 0)
        return pl.BlockSpec(yblock, imap)

    return pl.pallas_call(
        functools.partial(_merge_body, n_ctx_tiles=n_ctx_tiles),
        grid=(t // TM,),
        in_specs=(_dual_specs(TM, n_ctx_tiles)
                  + [pl.BlockSpec((1, N_MOD, D_MODEL), lambda i: (midx(i), 0, 0))]
                  + [_const_spec(w.shape) for w in wts]
                  + [tok(W_MIX), tok(W_MIX), tok(W_MIX), tok(2 * W_MIX),
                     ctx_spec(0), ctx_spec(1), lat_spec(0), lat_spec(1), tok(W_MIX), tok(W_MIX)]),
        out_specs=[tok(D_MODEL), tok(D_MODEL)],
        out_shape=[jax.ShapeDtypeStruct((t, D_MODEL), F32), jax.ShapeDtypeStruct((t, D_MODEL), BF16)],
        compiler_params=_cparams(("parallel",), VMEM_LIMIT),
        name="merge",
    )(x_ctx, x_lat, mod, *wts, ya, yd, gbg, h, y_ctx, y_ctx, y_lat, y_lat, g, bon)


_CAND_VALID = (8, 8, 8, 5, 4, 3, 2, 2, 2, 8)


def _route_head(qs, keys_ref):
    kio = lax.broadcasted_iota(I32, (N_KEYS, LANES), 0)
    sub = lax.broadcasted_iota(I32, (SUBLANES, LANES), 0)
    kid = lax.broadcasted_iota(I32, (PEER_TOPK, LANES), 0)
    neg = -jnp.inf

    def bc(x, r):
        return jnp.broadcast_to(x[r:r + 1, :], (SUBLANES, LANES))

    def head():
        tops = []
        for p in range(2):
            s = jnp.dot(keys_ref[p], qs[p], preferred_element_type=F32)
            vals = jnp.zeros((PEER_TOPK, LANES), F32)
            idxs = jnp.zeros((PEER_TOPK, LANES), I32)
            for r in range(PEER_TOPK):
                m = jnp.max(s, axis=0, keepdims=True)
                cand = jnp.where(s == m, kio, N_KEYS)
                ix = jnp.min(cand, axis=0, keepdims=True)
                s = jnp.where(cand == ix, neg, s)
                vals = jnp.where(kid == r, m, vals)
                idxs = jnp.where(kid == r, ix, idxs)
            tops.append((vals, idxs))
        (a0, i0), (a1, i1) = tops
        lo, hi = slice(0, SUBLANES), slice(SUBLANES, 2 * SUBLANES)
        slabs = [bc(a0, 0) + a1[lo], bc(a0, 0) + a1[hi]]
        ci = [bc(i0, 0), bc(i0, 0)]
        cj = [i1[lo], i1[hi]]
        for r in range(1, SUBLANES):
            slabs.append(bc(a0, r) + a1[lo])
            ci.append(bc(i0, r))
            cj.append(i1[lo])
        slabs.append(a0[hi] + bc(a1, 0))
        ci.append(i0[hi])
        cj.append(bc(i1, 0))
        slabs = [jnp.where(sub < nv, sl, neg) for sl, nv in zip(slabs, _CAND_VALID)]
        ids = [a * N_KEYS + b for a, b in zip(ci, cj)]
        vals = jnp.zeros((PEER_TOPK, LANES), F32)
        esel = jnp.zeros((PEER_TOPK, LANES), I32)
        for r in range(PEER_TOPK):
            level = list(zip(slabs, ids))
            while len(level) > 1:
                nxt = []
                for (va, ea), (vb, eb) in zip(level[0::2], level[1::2]):
                    take = vb > va
                    nxt.append((jnp.where(take, vb, va), jnp.where(take, eb, ea)))
                if len(level) % 2:
                    nxt.append(level[-1])
                level = nxt
            v8, e8 = level[0]
            for sh in (4, 2, 1):
                vr, er = pltpu.roll(v8, sh, 0), pltpu.roll(e8, sh, 0)
                take = vr > v8
                v8, e8 = jnp.where(take, vr, v8), jnp.where(take, er, e8)
            m, ex = v8[0:1, :], e8[0:1, :]
            slabs = [jnp.where(eid == ex, neg, sl) for sl, eid in zip(slabs, ids)]
            vals = jnp.where(kid == r, m, vals)
            esel = jnp.where(kid == r, ex, esel)
        e = jnp.exp(vals - vals[0:1, :])
        return esel, e / jnp.sum(e, axis=0, keepdims=True)

    return head()


def _peer_body(h2_ref, h2n_ref, wqt_ref, keys_ref, u_ref, v_ref, x1_ref, mod_ref, fng_ref,
               oc_ref, ol_ref,
               q_scr, e_scr, g_scr, et_scr, gt_scr, gs_scr, acc_scr,
               *, rows, pitch, units, n_ctx_tiles, final):
    m = pl.program_id(0)
    e = pl.program_id(1)
    tm = h2_ref.shape[0]
    n_chunks = tm // LANES
    nsel = PEER_HEADS * PEER_TOPK
    slot = lax.rem(m, 2)

    def project_queries(src_ref):
        q = lax.dot_general(wqt_ref[...], src_ref[...], _NT, preferred_element_type=F32).astype(BF16)
        for c in range(n_chunks):
            q_scr[c] = q[:, c * LANES:(c + 1) * LANES]

    def route_unit(u, dst):
        c = u // PEER_HEADS
        h = lax.rem(u, PEER_HEADS)
        qs = [q_scr[c, pl.ds(pl.multiple_of(h * (2 * N_KEYS) + p * N_KEYS, N_KEYS), N_KEYS), :]
              for p in range(2)]
        esel, gates = _route_head(qs, keys_ref)
        row = pl.multiple_of(h * PEER_TOPK, PEER_TOPK)
        e_scr[dst, c, pl.ds(row, PEER_TOPK), :] = esel
        g_scr[dst, c, pl.ds(row, PEER_TOPK), :] = gates

    @pl.when(jnp.logical_and(e == 0, m == 0))
    def _first_tile_routing():
        project_queries(h2_ref)

        def unit(u, c):
            route_unit(u, 0)
            return c

        lax.fori_loop(0, n_chunks * PEER_HEADS, unit, 0)

    @pl.when(e == 0)
    def _build():
        for c in range(n_chunks):
            et_scr[c * LANES:(c + 1) * LANES, :] = e_scr[slot, c].T
            gt_scr[c * LANES:(c + 1) * LANES, :] = g_scr[slot, c].T
        kio = lax.broadcasted_iota(I32, (N_KEYS, nsel), 0)

        def tok(t, c):
            erow = et_scr[pl.ds(t, 1), :]
            grow = gt_scr[pl.ds(t, 1), :]
            at = jnp.where(kio == (erow >> 7), grow, 0.0).astype(BF16)
            bt = jnp.where(kio == (erow & (N_KEYS - 1)), 1.0, 0.0).astype(BF16)
            gt = lax.dot_general(at, bt, _NT, preferred_element_type=F32)
            hi = pltpu.bitcast(gt[0:rows, :], jnp.uint32) & jnp.uint32(0xFFFF0000)
            lo = pltpu.bitcast(gt[rows:2 * rows, :], jnp.uint32) >> 16
            gs_scr[pl.ds(pl.multiple_of(t * pitch, SUBLANES), rows), :] = hi | lo
            return c

        lax.fori_loop(0, tm, tok, 0, unroll=16)
        acc_scr[...] = jnp.zeros_like(acc_scr)
        project_queries(h2n_ref)

    for k in range(units):
        route_unit(e * units + k, 1 - slot)

    per_sub = PEER_SUB // N_KEYS
    per_step = u_ref.shape[0] // N_KEYS
    steps_per_half = rows // per_step
    row0 = lax.rem(e, steps_per_half) * per_step
    shift = jnp.where(e < steps_per_half, 0, 16).astype(jnp.uint32)
    h2 = h2_ref[...]
    total = None
    for sb in range(per_step // per_sub):
        es = slice(sb * PEER_SUB, (sb + 1) * PEER_SUB)
        hmat = lax.dot_general(h2, u_ref[es, :], _NT, preferred_element_type=F32)
        words = jnp.concatenate(
            [gs_scr[pl.ds(row0 + sb * per_sub + ii, tm, stride=pitch), :] for ii in range(per_sub)], axis=1)
        gm = pltpu.bitcast((words << shift) & jnp.uint32(0xFFFF0000), F32)
        act = jax.nn.gelu(hmat.astype(BF16)) * gm.astype(BF16)
        part = jnp.dot(act, v_ref[es, :], preferred_element_type=F32)
        total = part if total is None else total + part
    acc_scr[...] += total

    def result():
        x2 = x1_ref[...] + mod_ref[0][5:6, :] * acc_scr[...]
        return _rms(x2, fng_ref[...]) if final else x2

    last = e == pl.num_programs(1) - 1

    @pl.when(jnp.logical_and(last, m < n_ctx_tiles))
    def _out_ctx():
        oc_ref[...] = result()

    @pl.when(jnp.logical_and(last, m >= n_ctx_tiles))
    def _out_lat():
        ol_ref[...] = result()


def _peer(h2, wqt, keys, u, v, x1, mod, fng, n_ctx_tok, lat_len, final):
    t = h2.shape[0]
    nsel = PEER_HEADS * PEER_TOPK
    rows = N_KEYS // 2
    pitch = rows + SUBLANES
    n_e = (N_KEYS * N_KEYS) // PEER_EB
    n_m = t // TM_PEER
    n_chunks = TM_PEER // LANES
    units = (n_chunks * PEER_HEADS) // n_e
    assert units * n_e == n_chunks * PEER_HEADS
    midx = functools.partial(_mod_index, tm=TM_PEER, n_ctx_tok=n_ctx_tok, lat_len=lat_len)
    tok = lambda wd: pl.BlockSpec((TM_PEER, wd), lambda m, e: (m, 0))
    nxt = pl.BlockSpec((TM_PEER, D_MODEL), lambda m, e: (jnp.minimum(m + 1, n_m - 1), 0))
    espec = pl.BlockSpec((PEER_EB, D_MODEL), lambda m, e: (e, 0))
    single = dict(pipeline_mode=pl.Buffered(1))
    return pl.pallas_call(
        functools.partial(_peer_body, rows=rows, pitch=pitch, units=units,
                          n_ctx_tiles=n_ctx_tok // TM_PEER, final=final),
        grid=(n_m, n_e),
        in_specs=[tok(D_MODEL), nxt,
                  pl.BlockSpec(wqt.shape, lambda m, e: (0, 0), **single),
                  pl.BlockSpec(keys.shape, lambda m, e: (0, 0, 0), **single),
                  espec, espec,
                  pl.BlockSpec((TM_PEER, D_MODEL), lambda m, e: (m, 0), **single),
                  pl.BlockSpec((1, N_MOD, D_MODEL), lambda m, e: (midx(m), 0, 0)),
                  _const_spec((1, D_MODEL))],
        out_specs=_dual_specs(TM_PEER, n_ctx_tok // TM_PEER, **single),
        out_shape=[jax.ShapeDtypeStruct((n_ctx_tok, D_MODEL), F32),
                   jax.ShapeDtypeStruct((t - n_ctx_tok, D_MODEL), F32)],
        scratch_shapes=[pltpu.VMEM((n_chunks, wqt.shape[0], LANES), BF16),
                        pltpu.VMEM((2, n_chunks, nsel, LANES), I32),
                        pltpu.VMEM((2, n_chunks, nsel, LANES), F32),
                        pltpu.VMEM((TM_PEER, nsel), I32),
                        pltpu.VMEM((TM_PEER, nsel), F32),
                        pltpu.VMEM((TM_PEER * pitch, N_KEYS), jnp.uint32),
                        pltpu.VMEM((TM_PEER, D_MODEL), F32)],
        compiler_params=_cparams(("arbitrary", "arbitrary"), VMEM_LIMIT),
        name="peer",
    )(h2, h2, wqt, keys, u, v, x1, mod, fng)


def _wkv_state_in(s, vs):
    n = s.shape[0]
    vl = HS_WKV // vs
    s = s.reshape(n, N_DIR, H_WKV, vs, vl, HS_WKV).transpose(1, 4, 5, 3, 0, 2)
    return s.reshape(N_DIR, vl, HS_WKV, vs * n * H_WKV)


def _wkv_state_out(s, n_sb, spb, vs):
    vl = HS_WKV // vs
    s = s.reshape(N_DIR, n_sb, vl, HS_WKV, vs, spb, H_WKV).transpose(1, 5, 0, 6, 4, 2, 3)
    return s.reshape(n_sb * spb, N_DIR, H_WKV, HS_WKV, HS_WKV)


def _layer_weights(i, prm):
    eye_h = jnp.eye(H_LRU, dtype=F32)
    eye_d = jnp.eye(N_DIR, dtype=F32)
    perm = _WKV_PERM

    def lru_bd(wt):
        return jnp.einsum("dhij,hg->hidgj", wt, eye_h).reshape(W_MIX, N_DIR * W_MIX)

    def lora_bd(wt):
        r = wt.shape[1]
        return jnp.einsum("drc,de->drec", wt, eye_d).reshape(N_DIR * r, N_DIR * W_MIX)

    w_in = prm["w_in"][i]
    pad = jnp.zeros((D_MODEL, Z_COLS - 5504), F32)
    rkv = [w_in[:, 2560 + j * W_MIX:2560 + (j + 1) * W_MIX][:, perm] for j in range(3)]
    w_in_perm = jnp.concatenate(
        [w_in[:, 0:1536]] + rkv + [w_in[:, 1536:2560], w_in[:, 4480:5504], w_in[:, 4096:4480], pad],
        axis=1).astype(BF16)
    row = lambda x: x.reshape(1, -1).astype(F32)
    head_of = np.arange(W_MIX) % H_WKV
    w_branch = prm["w_branch"][i]
    w_branch = jnp.concatenate([w_branch[0], w_branch[1], w_branch[2][perm, :], w_branch[3]], axis=0)
    return {
        "w_in": w_in_perm,
        "w_gate": w_in[:, 5504:].astype(BF16),
        "norm1_g": row(prm["norm1_g"][i]),
        "norm2_g": row(prm["norm2_g"][i]),
        "conv_a_w": prm["conv_a_w"][i],
        "conv_b_w": prm["conv_b_w"][i],
        "conv_b_b": row(prm["conv_b_b"][i]),
        "lru_w": jnp.concatenate([lru_bd(prm["lru_wa"][i]), lru_bd(prm["lru_wx"][i])], axis=1).astype(BF16),
        "lru_b": jnp.concatenate([row(prm["lru_ba"][i]), row(prm["lru_bx"][i])], axis=1),
        "lru_lam": row(prm["lru_lambda"][i]),
        "w0": row(prm["rwkv_w0"][i][:, perm]),
        "w2": lora_bd(prm["rwkv_w2"][i][:, :, perm]).astype(BF16),
        "a0": row(prm["rwkv_a0"][i][:, perm]),
        "a2": lora_bd(prm["rwkv_a2"][i][:, :, perm]).astype(BF16),
        "g2": prm["rwkv_g2"][i][:, perm].astype(BF16),
        "kk": row(prm["rwkv_kk"][i][perm]),
        "ka": row(prm["rwkv_ka"][i][perm]),
        "rk": row(prm["rwkv_rk"][i].reshape(W_MIX)[perm]),
        "lnx_g": row(prm["lnx_g"][i][perm]),
        "lnx_b": row(prm["lnx_b"][i][perm]),
        "sg_ln_g": row(prm["sg_ln_g"][i]),
        "sg_ln_b": row(prm["sg_ln_b"][i]),
        "sg_ws": prm["sg_ws"][i].astype(BF16),
        "sg_bst": prm["sg_bs"][i].T,
        "gate_b": row(prm["gate_b"][i]),
        "w_branch": w_branch.astype(BF16),
        "w_out": prm["w_out"][i].astype(BF16),
        "wq_t": prm["peer_wq"][i].T.astype(BF16),
        "keys": prm["peer_keys"][i].astype(BF16),
        "peer_u": prm["peer_u"][i].astype(BF16),
        "peer_v": prm["peer_v"][i].astype(BF16),
        "ind": jnp.asarray(head_of[:, None] == head_of[None, :], BF16),
    }


def kernel(x_prompt, x_sample, state_lru, state_wkv, c, c_ctx, norm1_g, norm2_g, w_mod, b_mod, w_in, conv_a_w, conv_b_w, conv_b_b, lru_wa, lru_ba, lru_wx, lru_bx, lru_lambda, rwkv_w0, rwkv_w2, rwkv_a0, rwkv_a2, rwkv_g2, rwkv_kk, rwkv_ka, rwkv_rk, lnx_g, lnx_b, sg_ln_g, sg_ln_b, sg_ws, sg_bs, gate_b, w_branch, w_out, peer_wq, peer_keys, peer_u, peer_v, final_norm_g):
    prm = dict(norm1_g=norm1_g, norm2_g=norm2_g, w_in=w_in, conv_a_w=conv_a_w, conv_b_w=conv_b_w,
               conv_b_b=conv_b_b, lru_wa=lru_wa, lru_ba=lru_ba, lru_wx=lru_wx, lru_bx=lru_bx,
               lru_lambda=lru_lambda, rwkv_w0=rwkv_w0, rwkv_w2=rwkv_w2, rwkv_a0=rwkv_a0, rwkv_a2=rwkv_a2,
               rwkv_g2=rwkv_g2, rwkv_kk=rwkv_kk, rwkv_ka=rwkv_ka, rwkv_rk=rwkv_rk, lnx_g=lnx_g,
               lnx_b=lnx_b, sg_ln_g=sg_ln_g, sg_ln_b=sg_ln_b, sg_ws=sg_ws, sg_bs=sg_bs, gate_b=gate_b,
               w_branch=w_branch, w_out=w_out, peer_wq=peer_wq, peer_keys=peer_keys, peer_u=peer_u,
               peer_v=peer_v)
    bc, lc, _ = x_prompt.shape
    bl, ll, _ = x_sample.shape
    depth = w_mod.shape[0]
    n_ctx_tok = bc * lc
    n_ctx_tiles = n_ctx_tok // TM
    lat_tiles = ll // TM
    ctx_spb = min(WKV_CTX_SPB, bc)
    lru_spb = ll // lc
    assert lc == TM and ll % TM_PEER == 0 and n_ctx_tok % TM_PEER == 0 and bl + 1 <= SUBLANES
    assert ll % GRID_W == 0 and TM % GRID_W == 0 and bc % ctx_spb == 0 and n_ctx_tok % ll == 0
    assert LANES % (ctx_spb * H_WKV) == 0 and LANES % (bl * H_WKV) == 0

    cond = jnp.zeros((SUBLANES, D_MODEL), F32).at[0].set(c_ctx).at[1:1 + bl].set(c)
    mods = _modulation(cond, w_mod, b_mod).reshape(depth, SUBLANES, N_MOD, D_MODEL)
    fng = final_norm_g.reshape(1, D_MODEL)
    x_ctx = x_prompt.reshape(n_ctx_tok, D_MODEL)
    x_lat = x_sample.reshape(bl * ll, D_MODEL)
    ctx_vs = LANES // (ctx_spb * H_WKV)
    lat_vs = LANES // (bl * H_WKV)
    n_sb = bc // ctx_spb
    wkv_zero = jnp.zeros((N_DIR * n_sb, HS_WKV // ctx_vs, HS_WKV, LANES), F32)
    lru_zero = jnp.zeros((n_ctx_tok // ll, lru_spb, N_DIR * W_MIX), F32)
    new_lru, new_wkv = [], []
    pnames = ["ya", "yd", "gbg", "la", "lu", "g", "bon", "rt", "vt", "kkt", "wt", "kt", "bt"]
    for i in range(depth):
        lw = _layer_weights(i, prm)
        mod = mods[i]
        p = dict(zip(pnames, _prep(x_ctx, x_lat, mod, lw, ll)))

        lat_h0 = jnp.zeros((bl, lru_spb, N_DIR * W_MIX), F32).at[:, 0].set(
            state_lru[:, i].astype(F32).reshape(bl, N_DIR * W_MIX))
        h, lru_s = _lru_scan(p["la"], p["lu"], jnp.concatenate([lru_zero, lat_h0], axis=0),
                             rows=ll, n_ctx_blocks=n_ctx_tok // ll, ctx_cfg=(lru_spb, lc), lat_cfg=(1, ll))
        new_lru.append(lru_s[:n_ctx_tok // ll].reshape(bc, N_DIR, W_MIX))

        wkv_in = [p[n] for n in ("rt", "wt", "kt", "kkt", "bt", "vt")]
        y_c, s_c = _wkv_scan(*wkv_in, wkv_zero, tile0=0, n_seq=bc, seq_tiles=1, spb=ctx_spb)
        y_l, _ = _wkv_scan(*wkv_in, _wkv_state_in(state_wkv[:, i].astype(F32), lat_vs),
                           tile0=n_ctx_tiles, n_seq=bl, seq_tiles=lat_tiles, spb=1)
        new_wkv.append(_wkv_state_out(s_c, n_sb, ctx_spb, ctx_vs))

        tok_in = [p["ya"], p["yd"], p["gbg"], h, p["g"], p["bon"]]
        x1, h2 = _merge(x_ctx, x_lat, mod, lw, tok_in, y_c, y_l, ll)
        x_ctx, x_lat = _peer(h2, lw["wq_t"], lw["keys"], lw["peer_u"], lw["peer_v"], x1, mod, fng,
                             n_ctx_tok, ll, final=(i == depth - 1))
    y_prompt = x_ctx.reshape(bc, lc, D_MODEL)
    y_sample = x_lat.reshape(bl, ll, D_MODEL)
    return (y_prompt, y_sample, jnp.stack(new_lru, axis=1), jnp.stack(new_wkv, axis=1))
```

```python
import functools

import numpy as np
import jax
import jax.numpy as jnp
from jax import lax
from jax.experimental import pallas as pl
from jax.experimental.pallas import tpu as pltpu

F32 = jnp.float32
BF16 = jnp.bfloat16
I32 = jnp.int32

D_MODEL = 1024
W_MIX = 512
N_DIR = 2
N_BRANCH = 4
H_WKV = 8
HS_WKV = 64
H_LRU = 8
HB_LRU = 64
LORA_W = 64
LORA_A = 64
LORA_G = 128
GRID_W = 64
CHUNK = 128
G_SG = 4
N_KEYS = 128
PEER_HEADS = 8
PEER_TOPK = 16
N_MOD = 6
EPS = 1e-6
LNX_EPS = 64e-5
LRU_C = 8.0

LANES = 128
SUBLANES = 8
TM = 256
TM_PEER = 512
PEER_EB = 1024
PEER_SUB = 512
Z_COLS = 5632
WKV_TC = LANES
WKV_CTX_SPB = 8
VMEM_LIMIT = 56 * 1024 * 1024

_NT = (((1,), (1,)), ((), ()))
_WKV_PERM = np.array([(n % H_WKV) * HS_WKV + n // H_WKV for n in range(W_MIX)])


def _cparams(sem, vmem=None):
    return pltpu.CompilerParams(dimension_semantics=sem, vmem_limit_bytes=vmem)


def _const_spec(shape):
    nd = len(shape)
    return pl.BlockSpec(shape, lambda *_: (0,) * nd)


def _softplus(x):
    return jnp.maximum(x, 0.0) + jnp.log1p(jnp.exp(-jnp.abs(x)))


def _rms(x, g):
    return x * lax.rsqrt(jnp.mean(x * x, axis=-1, keepdims=True) + EPS) * g


def _segsum(x, ind):
    hi = x.astype(BF16)
    lo = (x - hi.astype(F32)).astype(BF16)
    return (jnp.dot(hi, ind, preferred_element_type=F32)
            + jnp.dot(lo, ind, preferred_element_type=F32))


def _mod_index(i, tm, n_ctx_tok, lat_len):
    n_ctx_tiles = n_ctx_tok // tm
    tiles_per_seq = lat_len // tm
    return jnp.where(i < n_ctx_tiles, 0, 1 + lax.div(i - n_ctx_tiles, tiles_per_seq))


def _mod_body(s_ref, w_ref, b_ref, o_ref):
    s = s_ref[...]
    s = s * jax.nn.sigmoid(s)
    o_ref[0] = jnp.dot(s.astype(BF16), w_ref[0].astype(BF16), preferred_element_type=F32) + b_ref[0]


def _modulation(cond, w_mod, b_mod):
    depth = w_mod.shape[0]
    n = w_mod.shape[2]
    tn = 1536
    return pl.pallas_call(
        _mod_body,
        grid=(depth, n // tn),
        in_specs=[_const_spec((SUBLANES, D_MODEL)),
                  pl.BlockSpec((1, D_MODEL, tn), lambda l, j: (l, 0, j)),
                  pl.BlockSpec((1, 1, tn), lambda l, j: (l, 0, j))],
        out_specs=pl.BlockSpec((1, SUBLANES, tn), lambda l, j: (l, 0, j)),
        out_shape=jax.ShapeDtypeStruct((depth, SUBLANES, n), F32),
        compiler_params=_cparams(("parallel", "parallel"), VMEM_LIMIT),
        name="modulation",
    )(cond, w_mod, b_mod.reshape(depth, 1, n))


def _prep_body(xc_ref, xl_ref, xp_ref, xn_ref, mod_ref, n1g_ref, win_ref,
               caw_ref, cbw_ref, cbb_ref, lruw_ref, lrub_ref, lam_ref,
               w0_ref, w2_ref, a0_ref, a2_ref, g2_ref, kkw_ref, ka_ref, rk_ref,
               lng_ref, lnb_ref, ws_ref, bst_ref, ind_ref,
               ya_ref, yd_ref, gbg_ref, la_ref, lu_ref, g_ref, bon_ref,
               rt_ref, vt_ref, kkt_ref, wt_ref, kt_ref, bt_ref,
               *, n_ctx_tiles, tiles_per_seq):
    i = pl.program_id(0)
    is_ctx = i < n_ctx_tiles
    t = lax.broadcasted_iota(I32, (TM, 1), 0)
    ind = ind_ref[...]
    m = mod_ref[0]

    def modulated(xv):
        return (_rms(xv, n1g_ref[...]) * (1.0 + m[1:2, :]) + m[0:1, :]).astype(BF16)

    def project(hv, lo, hi):
        return jnp.dot(hv, win_ref[:, lo:hi], preferred_element_type=F32)

    h = modulated(jnp.where(is_ctx, xc_ref[...], xl_ref[...]))
    za = project(h, 0, 1536)
    zc = project(h, 1536, 3072)
    zb = project(h, 3072, 4096)
    zd = project(h, 4096, 5120)
    zl = project(h, 5120, 5120 + 2 * LORA_W + 2 * LORA_A + LORA_G)
    halo_prev = project(modulated(xp_ref[...]), 3072 + W_MIX, 4096)
    halo_next = project(modulated(xn_ref[...]), 3072 + W_MIX, 4096)

    pm = jnp.where(is_ctx, TM - 1, GRID_W - 1)
    pos = t & pm
    a_b = za[:, 0:W_MIX]
    ac = za[:, W_MIX:2 * W_MIX] * za[:, 2 * W_MIX:3 * W_MIX]
    up = jnp.where(pos == 0, 0.0, pltpu.roll(ac, 1, 0))
    dn = jnp.where(pos == pm, 0.0, pltpu.roll(ac, TM - 1, 0))
    ya_ref[...] = a_b * (caw_ref[0:1, :] * up + caw_ref[1:2, :] * ac + caw_ref[2:3, :] * dn)

    seq_tile = lax.rem(jnp.maximum(i - n_ctx_tiles, 0), tiles_per_seq)
    first = jnp.logical_or(is_ctx, seq_tile == 0)
    last = jnp.logical_or(is_ctx, seq_tile == tiles_per_seq - 1)
    prev = jnp.where(first, 0.0, halo_prev[SUBLANES - 1:SUBLANES, :])
    nxt0 = jnp.where(last, 0.0, halo_next[0:1, :])
    nxt1 = jnp.where(last, 0.0, halo_next[1:2, :])
    bx = zb[:, W_MIX:2 * W_MIX]
    m1 = jnp.where(t == 0, prev, pltpu.roll(bx, 1, 0))
    p1 = jnp.where(t == TM - 1, nxt0, pltpu.roll(bx, TM - 1, 0))
    p2 = jnp.where(t == TM - 2, nxt0, jnp.where(t == TM - 1, nxt1, pltpu.roll(bx, TM - 2, 0)))
    xb = (cbw_ref[0:1, :] * m1 + cbw_ref[1:2, :] * bx + cbw_ref[2:3, :] * p1
          + cbw_ref[3:4, :] * p2 + cbb_ref[...])
    gates = jnp.dot(xb.astype(BF16), lruw_ref[...], preferred_element_type=F32) + lrub_ref[...]
    rg = jax.nn.sigmoid(gates[:, 0:2 * W_MIX])
    ig = jax.nn.sigmoid(gates[:, 2 * W_MIX:4 * W_MIX])
    log_a = -LRU_C * rg * _softplus(-lam_ref[...])
    xb2 = jnp.concatenate([xb, xb], axis=1)
    la_ref[...] = jnp.exp(log_a)
    lu_ref[...] = jnp.sqrt(jnp.tanh(-log_a) * (jnp.exp(2.0 * log_a) + 1.0)) * (ig * xb2)
    gbg_ref[...] = jax.nn.gelu(zb[:, 0:W_MIX])

    zr = zc[:, 0:W_MIX]
    zk = zc[:, W_MIX:2 * W_MIX]
    zv = zc[:, 2 * W_MIX:3 * W_MIX]
    zwd = zl[:, 0:2 * LORA_W]
    zad = zl[:, 2 * LORA_W:2 * LORA_W + 2 * LORA_A]
    zgd = zl[:, 2 * LORA_W + 2 * LORA_A:2 * LORA_W + 2 * LORA_A + LORA_G]
    wlin = w0_ref[...] + jnp.dot(jnp.tanh(zwd).astype(BF16), w2_ref[...], preferred_element_type=F32)
    wt_ref[0] = jnp.exp(-jnp.exp(-_softplus(-wlin) - 0.5)).T
    av = jax.nn.sigmoid(a0_ref[...] + jnp.dot(zad.astype(BF16), a2_ref[...], preferred_element_type=F32))
    g_ref[...] = jnp.dot(jax.nn.sigmoid(zgd).astype(BF16), g2_ref[...], preferred_element_type=F32)
    kkr = zk * kkw_ref[...]
    kkn = kkr / jnp.maximum(jnp.sqrt(_segsum(kkr * kkr, ind)), 1e-12)
    zk2 = jnp.concatenate([zk, zk], axis=1)
    ka2 = jnp.concatenate([ka_ref[...], ka_ref[...]], axis=1)
    kd = zk2 * (1.0 + (av - 1.0) * ka2)
    kt_ref[0] = kd.T
    bt_ref[0] = (jnp.concatenate([kkn, kkn], axis=1) * av).T
    rt_ref[0] = zr.T
    vt_ref[0] = zv.T
    kkt_ref[0] = kkn.T
    bon_ref[...] = _segsum(zr * (kd[:, 0:W_MIX] + kd[:, W_MIX:2 * W_MIX]) * rk_ref[...], ind) * zv

    zg = jax.nn.gelu(zd)
    u = zg[:, 0:W_MIX]
    vv = zg[:, W_MIX:2 * W_MIX]
    vc = vv - jnp.mean(vv, axis=-1, keepdims=True)
    vn = vc * lax.rsqrt(jnp.mean(vc * vc, axis=-1, keepdims=True) + 1e-5) * lng_ref[...] + lnb_ref[...]
    for c in range(TM // CHUNK):
        rs = slice(c * CHUNK, (c + 1) * CHUNK)
        for gi in range(G_SG):
            cs = slice(gi * LANES, (gi + 1) * LANES)
            s = jnp.dot(ws_ref[gi], vn[rs, cs].astype(BF16), preferred_element_type=F32)
            yd_ref[rs, cs] = u[rs, cs] * (s + bst_ref[:, gi:gi + 1])


def _dual_specs(rows, n_ctx_blocks, **kw):
    return [pl.BlockSpec((rows, D_MODEL), lambda i, *_: (jnp.minimum(i, n_ctx_blocks - 1), 0), **kw),
            pl.BlockSpec((rows, D_MODEL), lambda i, *_: (jnp.maximum(i - n_ctx_blocks, 0), 0), **kw)]


def _prep(x_ctx, x_lat, mod, lw, lat_len):
    n_ctx_tok = x_ctx.shape[0]
    t = n_ctx_tok + x_lat.shape[0]
    n_tiles = t // TM
    n_ctx_tiles = n_ctx_tok // TM
    tiles_per_seq = lat_len // TM
    rows8 = TM // SUBLANES
    last_blk = x_lat.shape[0] // SUBLANES - 1
    midx = functools.partial(_mod_index, tm=TM, n_ctx_tok=n_ctx_tok, lat_len=lat_len)

    def lat_blk8(i, off):
        return (jnp.clip((i - n_ctx_tiles) * rows8 + off, 0, last_blk), 0)

    x_specs = _dual_specs(TM, n_ctx_tiles) + [
        pl.BlockSpec((SUBLANES, D_MODEL), lambda i: lat_blk8(i, -1)),
        pl.BlockSpec((SUBLANES, D_MODEL), lambda i: lat_blk8(i, rows8)),
        pl.BlockSpec((1, N_MOD, D_MODEL), lambda i: (midx(i), 0, 0)),
        _const_spec((1, D_MODEL)),
        pl.BlockSpec(lw["w_in"].shape, lambda i: (0, 0), pipeline_mode=pl.Buffered(1)),
    ]
    wnames = ["conv_a_w", "conv_b_w", "conv_b_b", "lru_w", "lru_b", "lru_lam", "w0", "w2", "a0", "a2",
              "g2", "kk", "ka", "rk", "sg_ln_g", "sg_ln_b", "sg_ws", "sg_bst", "ind"]
    wts = [lw[n] for n in wnames]
    w_specs = [_const_spec(w.shape) for w in wts]
    widths = [W_MIX, W_MIX, W_MIX, 2 * W_MIX, 2 * W_MIX, W_MIX, W_MIX]
    t_rows = [W_MIX, W_MIX, W_MIX, 2 * W_MIX, 2 * W_MIX, 2 * W_MIX]
    out_specs = ([pl.BlockSpec((TM, wd), lambda i: (i, 0)) for wd in widths]
                 + [pl.BlockSpec((1, r, TM), lambda i: (i, 0, 0)) for r in t_rows])
    out_shape = ([jax.ShapeDtypeStruct((t, wd), F32) for wd in widths]
                 + [jax.ShapeDtypeStruct((n_tiles, r, TM), F32) for r in t_rows])
    return pl.pallas_call(
        functools.partial(_prep_body, n_ctx_tiles=n_ctx_tiles, tiles_per_seq=tiles_per_seq),
        grid=(n_tiles,),
        in_specs=x_specs + w_specs,
        out_specs=out_specs,
        out_shape=out_shape,
        compiler_params=_cparams(("parallel",), VMEM_LIMIT),
        name="branch_prep",
    )(x_ctx, x_lat, x_lat, x_lat, mod, lw["norm1_g"], lw["w_in"], *wts)


def _lru_body(a_ref, u_ref, h0_ref, h_ref, hf_ref, *, n_ctx_blocks, ctx_cfg, lat_cfg):
    fw, bw = slice(0, W_MIX), slice(W_MIX, 2 * W_MIX)

    def scan(nseq, l):
        def step(s, carry):
            out = []
            for j in range(nseq):
                tf = j * l + s
                tb = j * l + (l - 1 - s)
                hf = a_ref[pl.ds(tf, 1), fw] * carry[2 * j] + u_ref[pl.ds(tf, 1), fw]
                hb = a_ref[pl.ds(tb, 1), bw] * carry[2 * j + 1] + u_ref[pl.ds(tb, 1), bw]
                h_ref[pl.ds(tf, 1), fw] = hf
                h_ref[pl.ds(tb, 1), bw] = hb
                out += [hf, hb]
            return tuple(out)

        init = []
        for j in range(nseq):
            init += [h0_ref[0, j:j + 1, fw], h0_ref[0, j:j + 1, bw]]
        fin = lax.fori_loop(0, l, step, tuple(init), unroll=2)
        hf_ref[0] = h0_ref[0]
        for j in range(nseq):
            hf_ref[0, j:j + 1, fw] = fin[2 * j]
            hf_ref[0, j:j + 1, bw] = fin[2 * j + 1]

    is_ctx = pl.program_id(0) < n_ctx_blocks
    pl.when(is_ctx)(lambda: scan(*ctx_cfg))
    pl.when(jnp.logical_not(is_ctx))(lambda: scan(*lat_cfg))


def _lru_scan(a, u, h0, *, rows, n_ctx_blocks, ctx_cfg, lat_cfg):
    nb = a.shape[0] // rows
    w = a.shape[1]
    tok = pl.BlockSpec((rows, w), lambda i: (i, 0))
    st = pl.BlockSpec((1,) + h0.shape[1:], lambda i: (i, 0, 0))
    return pl.pallas_call(
        functools.partial(_lru_body, n_ctx_blocks=n_ctx_blocks, ctx_cfg=ctx_cfg, lat_cfg=lat_cfg),
        grid=(nb,),
        in_specs=[tok, tok, st],
        out_specs=[tok, st],
        out_shape=[jax.ShapeDtypeStruct(a.shape, F32), jax.ShapeDtypeStruct(h0.shape, F32)],
        compiler_params=_cparams(("parallel",), VMEM_LIMIT),
        name="lru_scan",
    )(a, u, h0)


_SLOT_ORDER = (0, 4, 2, 6, 1, 5, 3, 7)


def _rowsum8(parts):
    sub = lax.broadcasted_iota(I32, (SUBLANES, LANES), 0)
    slots = [parts[i] for i in _SLOT_ORDER]
    roll = pltpu.roll
    lvl1 = [jnp.where(sub < 4, a + roll(a, 4, 0), b + roll(b, 4, 0))
            for a, b in zip(slots[0::2], slots[1::2])]
    lvl2 = [jnp.where((sub & 3) < 2, a + roll(a, 6, 0), roll(b + roll(b, 6, 0), 2, 0))
            for a, b in zip(lvl1[0::2], lvl1[1::2])]
    a, b = lvl2
    return jnp.where((sub & 1) == 0, a + roll(a, 7, 0), roll(b + roll(b, 7, 0), 1, 0))


def _fold8(x):
    return jnp.sum(x.reshape(HS_WKV // SUBLANES, SUBLANES, LANES), axis=0)


def _wkv_body(*refs, nsrc, spb, vs, n_sb, tc, kp, vp):
    vl_n = HS_WKV // vs
    n_in = 6 * nsrc
    k_srcs = [refs[o * nsrc:(o + 1) * nsrc] for o in range(5)]
    v_srcs = refs[5 * nsrc:n_in]
    s0_ref = refs[n_in]
    y_ref = refs[n_in + 1]
    sf_ref = refs[n_in + 2]
    k_scr = refs[n_in + 3:n_in + 8]
    v_scr, y_scr, s_scr, sa_scr = refs[n_in + 8:n_in + 12]
    r_scr, w_scr, k_scr_, kk_scr, b_scr = k_scr
    backward = pl.program_id(0) // n_sb == 1
    seqs = [(s, j) for s in range(nsrc) for j in range(spb)]

    @pl.when(pl.program_id(1) == 0)
    def _():
        s_scr[...] = s0_ref[0]

    def build_k(c, carry):
        row = pl.multiple_of(c * H_WKV, H_WKV)
        for o in range(5):
            slab = [k_srcs[o][s][j, pl.ds(row, H_WKV), :] for s, j in seqs]
            k_scr[o][pl.ds(c, tc, stride=kp), :] = jnp.concatenate(slab * vs, axis=0).T
        return carry

    lax.fori_loop(0, HS_WKV, build_k, 0, unroll=4)

    def build_v(vl, carry):
        slab = []
        for vsi in range(vs):
            row = pl.multiple_of((vsi * vl_n + vl) * H_WKV, H_WKV)
            slab += [v_srcs[s][j, pl.ds(row, H_WKV), :] for s, j in seqs]
        v_scr[pl.ds(vl, tc, stride=vp), :] = jnp.concatenate(slab, axis=0).T
        return carry

    lax.fori_loop(0, vl_n, build_v, 0, unroll=4)

    def step(s, carry):
        t = jnp.where(backward, tc - 1 - s, s)
        krow = pl.multiple_of(t * kp, SUBLANES)
        vrow = pl.multiple_of(t * vp, SUBLANES)
        kslab = pl.ds(krow, HS_WKV)
        for g in range(vl_n // SUBLANES):
            parts = [_fold8(s_scr[g * SUBLANES + i] * kk_scr[kslab, :]) for i in range(SUBLANES)]
            sa_scr[g * SUBLANES:(g + 1) * SUBLANES, :] = _rowsum8(parts)
        for g in range(vl_n // SUBLANES):
            parts = []
            for i in range(SUBLANES):
                vl = g * SUBLANES + i
                sa = sa_scr[vl:vl + 1, :]
                vv = v_scr[pl.ds(vrow + vl, 1), :]
                sn = s_scr[vl] * w_scr[kslab, :] - sa * b_scr[kslab, :] + vv * k_scr_[kslab, :]
                s_scr[vl] = sn
                parts.append(_fold8(sn * r_scr[kslab, :]))
            y_scr[pl.ds(pl.multiple_of(vrow + g * SUBLANES, SUBLANES), SUBLANES), :] = _rowsum8(parts)
        return carry

    lax.fori_loop(0, tc, step, 0)

    def emit_y(vl, carry):
        yt = y_scr[pl.ds(vl, tc, stride=vp), :].T
        for vsi in range(vs):
            row = pl.multiple_of((vsi * vl_n + vl) * H_WKV, H_WKV)
            for n, (s, j) in enumerate(seqs):
                lane0 = (vsi * len(seqs) + n) * H_WKV
                y_ref[0, 0, s * spb + j, pl.ds(row, H_WKV), :] = yt[lane0:lane0 + H_WKV, :]
        return carry

    lax.fori_loop(0, vl_n, emit_y, 0, unroll=4)
    sf_ref[0] = s_scr[...]


def _wkv_scan(rt, wt, kt, kkt, bt, vt, s0, *, tile0, n_seq, seq_tiles, spb):
    tc = WKV_TC
    if spb > 1:
        assert seq_tiles == 1 and n_seq % spb == 0 and tile0 % spb == 0
        nsrc, n_sb = 1, n_seq // spb
    else:
        nsrc, n_sb = n_seq, 1
    inst = nsrc * spb * H_WKV
    vs = LANES // inst
    vl_n = HS_WKV // vs
    assert vl_n % SUBLANES == 0, "value rows are processed eight at a time"
    cpt = TM // tc
    n_chunks = seq_tiles * cpt
    kp = HS_WKV + SUBLANES
    vp = vl_n + SUBLANES if ((vl_n + SUBLANES) // SUBLANES) % 2 else vl_n + 2 * SUBLANES

    def chunk(g, i):
        return jnp.where(g // n_sb == 1, n_chunks - 1 - i, i)

    def in_map(g, i, *, src, per_dir):
        ce = chunk(g, i)
        rb = (g // n_sb) if per_dir else 0
        if spb > 1:
            return (tile0 // spb + g % n_sb, rb, ce)
        return (tile0 + src * seq_tiles + ce // cpt, rb, ce % cpt)

    def out_map(g, i):
        ce = chunk(g, i)
        if spb > 1:
            return (g // n_sb, 0, g % n_sb, 0, ce)
        return (g // n_sb, ce // cpt, 0, 0, ce % cpt)

    in_specs, operands = [], []
    for arr, per_dir in ((rt, False), (wt, True), (kt, True), (kkt, False), (bt, True), (vt, False)):
        for src in range(nsrc):
            in_specs.append(pl.BlockSpec((spb, W_MIX, tc), functools.partial(in_map, src=src, per_dir=per_dir),
                                         pipeline_mode=pl.Buffered(1)))
            operands.append(arr)
    sspec = pl.BlockSpec((1, vl_n, HS_WKV, LANES), lambda g, i: (g, 0, 0, 0))
    in_specs.append(sspec)
    out_specs = [pl.BlockSpec((1, 1, nsrc * spb, W_MIX, tc), out_map)]
    out_shape = [jax.ShapeDtypeStruct((N_DIR, seq_tiles, n_seq, W_MIX, TM), F32)]
    res = pl.pallas_call(
        functools.partial(_wkv_body, nsrc=nsrc, spb=spb, vs=vs, n_sb=n_sb, tc=tc, kp=kp, vp=vp),
        grid=(N_DIR * n_sb, n_chunks),
        in_specs=in_specs,
        out_specs=out_specs + [sspec],
        out_shape=out_shape + [jax.ShapeDtypeStruct(s0.shape, F32)],
        scratch_shapes=([pltpu.VMEM((tc * kp, LANES), F32)] * 5
                        + [pltpu.VMEM((tc * vp, LANES), F32)] * 2
                        + [pltpu.VMEM((vl_n, HS_WKV, LANES), F32), pltpu.VMEM((vl_n, LANES), F32)]),
        compiler_params=_cparams(("parallel", "arbitrary"), VMEM_LIMIT),
        name="wkv_scan",
    )(*operands, s0)
    return res[0], res[1]


def _merge_body(xc_ref, xl_ref, mod_ref, n1g_ref, n2g_ref, wg_ref, gb_ref, wbr_ref, wo_ref,
                lnxg_ref, lnxb_ref, ind_ref,
                ya_ref, yd_ref, gbg_ref, h_ref, ycf_ref, ycb_ref, ylf_ref, ylb_ref, g_ref, bon_ref,
                x1_ref, h2_ref, *, n_ctx_tiles):
    is_ctx = pl.program_id(0) < n_ctx_tiles
    x = jnp.where(is_ctx, xc_ref[...], xl_ref[...])
    m = mod_ref[0]
    ind = ind_ref[...]
    h = (_rms(x, n1g_ref[...]) * (1.0 + m[1:2, :]) + m[0:1, :]).astype(BF16)
    y_b = gbg_ref[...] * (h_ref[:, 0:W_MIX] + h_ref[:, W_MIX:2 * W_MIX])
    y =jnp.where(is_ctx, ycf_ref[0, 0, 0] + ycb_ref[0, 0, 0], ylf_ref[0, 0, 0] + ylb_ref[0, 0, 0]).T
    yc = y - _segsum(y, ind) * (1.0 / HS_WKV)
    var = _segsum(yc * yc, ind) * (1.0 / HS_WKV)
    y_c = (yc * lax.rsqrt(var + LNX_EPS) * lnxg_ref[...] + lnxb_ref[...] + bon_ref[...]) * g_ref[...]
    merged = None
    for n, yn in enumerate((ya_ref[...], y_b, y_c, yd_ref[...])):
        cs = slice(n * D_MODEL, (n + 1) * D_MODEL)
        gate = jax.nn.sigmoid(jnp.dot(h, wg_ref[:, cs], preferred_element_type=F32) + gb_ref[:, cs])
        br = jnp.dot(yn.astype(BF16), wbr_ref[n * W_MIX:(n + 1) * W_MIX, :], preferred_element_type=F32)
        merged = gate * br if merged is None else merged + gate * br
    mo = jnp.dot(merged.astype(BF16), wo_ref[...], preferred_element_type=F32)
    x1 = x + m[2:3, :] * mo
    x1_ref[...] = x1
    h2_ref[...] = (_rms(x1, n2g_ref[...]) * (1.0 + m[4:5, :]) + m[3:4, :]).astype(BF16)


def _merge(x_ctx, x_lat, mod, lw, tok_in, y_ctx, y_lat, lat_len):
    n_ctx_tok = x_ctx.shape[0]
    t = n_ctx_tok + x_lat.shape[0]
    n_ctx_tiles = n_ctx_tok // TM
    tps = lat_len // TM
    midx = functools.partial(_mod_index, tm=TM, n_ctx_tok=n_ctx_tok, lat_len=lat_len)
    wnames = ["norm1_g", "norm2_g", "w_gate", "gate_b", "w_branch", "w_out", "lnx_g", "lnx_b", "ind"]
    wts = [lw[n] for n in wnames]
    tok = lambda wd: pl.BlockSpec((TM, wd), lambda i: (i, 0))
    ya, yd, gbg, h, g, bon = tok_in
    yblock = (1, 1, 1, W_MIX, TM)

    def ctx_spec(d):
        return pl.BlockSpec(yblock, lambda i: (d, 0, jnp.minimum(i, n_ctx_tiles - 1), 0, 0))

    def lat_spec(d):
        def imap(i):
            r = jnp.maximum(i - n_ctx_tiles, 0)
            return (d, lax.rem(r, tps), lax.div(r, tps), 0, 0)
        return pl.BlockSpec(yblock, imap)

    return pl.pallas_call(
        functools.partial(_merge_body, n_ctx_tiles=n_ctx_tiles),
        grid=(t // TM,),
        in_specs=(_dual_specs(TM, n_ctx_tiles)
                  + [pl.BlockSpec((1, N_MOD, D_MODEL), lambda i: (midx(i), 0, 0))]
                  + [_const_spec(w.shape) for w in wts]
                  + [tok(W_MIX), tok(W_MIX), tok(W_MIX), tok(2 * W_MIX),
                     ctx_spec(0), ctx_spec(1), lat_spec(0), lat_spec(1), tok(W_MIX), tok(W_MIX)]),
        out_specs=[tok(D_MODEL), tok(D_MODEL)],
        out_shape=[jax.ShapeDtypeStruct((t, D_MODEL), F32), jax.ShapeDtypeStruct((t, D_MODEL), BF16)],
        compiler_params=_cparams(("parallel",), VMEM_LIMIT),
        name="merge",
    )(x_ctx, x_lat, mod, *wts, ya, yd, gbg, h, y_ctx, y_ctx, y_lat, y_lat, g, bon)


_CAND_VALID = (8, 8, 8, 5, 4, 3, 2, 2, 2, 8)


def _route_head(qs, keys_ref):
    kio = lax.broadcasted_iota(I32, (N_KEYS, LANES), 0)
    sub = lax.broadcasted_iota(I32, (SUBLANES, LANES), 0)
    kid = lax.broadcasted_iota(I32, (PEER_TOPK, LANES), 0)
    neg = -jnp.inf

    def bc(x, r):
        return jnp.broadcast_to(x[r:r + 1, :], (SUBLANES, LANES))

    def head():
        tops = []
        for p in range(2):
            s = jnp.dot(keys_ref[p], qs[p], preferred_element_type=F32)
            vals = jnp.zeros((PEER_TOPK, LANES), F32)
            idxs = jnp.zeros((PEER_TOPK, LANES), I32)
            for r in range(PEER_TOPK):
                m = jnp.max(s, axis=0, keepdims=True)
                cand = jnp.where(s == m, kio, N_KEYS)
                ix = jnp.min(cand, axis=0, keepdims=True)
                s = jnp.where(cand == ix, neg, s)
                vals = jnp.where(kid == r, m, vals)
                idxs = jnp.where(kid == r, ix, idxs)
            tops.append((vals, idxs))
        (a0, i0), (a1, i1) = tops
        lo, hi = slice(0, SUBLANES), slice(SUBLANES, 2 * SUBLANES)
        slabs = [bc(a0, 0) + a1[lo], bc(a0, 0) + a1[hi]]
        ci = [bc(i0, 0), bc(i0, 0)]
        cj = [i1[lo], i1[hi]]
        for r in range(1, SUBLANES):
            slabs.append(bc(a0, r) + a1[lo])
            ci.append(bc(i0, r))
            cj.append(i1[lo])
        slabs.append(a0[hi] + bc(a1, 0))
        ci.append(i0[hi])
        cj.append(bc(i1, 0))
        slabs = [jnp.where(sub < nv, sl, neg) for sl, nv in zip(slabs, _CAND_VALID)]
        ids = [a * N_KEYS + b for a, b in zip(ci, cj)]
        vals = jnp.zeros((PEER_TOPK, LANES), F32)
        esel = jnp.zeros((PEER_TOPK, LANES), I32)
        for r in range(PEER_TOPK):
            level = list(zip(slabs, ids))
            while len(level) > 1:
                nxt = []
                for (va, ea), (vb, eb) in zip(level[0::2], level[1::2]):
                    take = vb > va
                    nxt.append((jnp.where(take, vb, va), jnp.where(take, eb, ea)))
                if len(level) % 2:
                    nxt.append(level[-1])
                level = nxt
            v8, e8 = level[0]
            for sh in (4, 2, 1):
                vr, er = pltpu.roll(v8, sh, 0), pltpu.roll(e8, sh, 0)
                take = vr > v8
                v8, e8 = jnp.where(take, vr, v8), jnp.where(take, er, e8)
            m, ex = v8[0:1, :], e8[0:1, :]
            slabs = [jnp.where(eid == ex, neg, sl) for sl, eid in zip(slabs, ids)]
            vals = jnp.where(kid == r, m, vals)
            esel = jnp.where(kid == r, ex, esel)
        e = jnp.exp(vals - vals[0:1, :])
        return esel, e / jnp.sum(e, axis=0, keepdims=True)

    return head()


def _peer_body(h2_ref, h2n_ref, wqt_ref, keys_ref, u_ref, v_ref, x1_ref, mod_ref, fng_ref,
               oc_ref, ol_ref,
               q_scr, e_scr, g_scr, et_scr, gt_scr, gs_scr, acc_scr,
               *, rows, pitch, units, n_ctx_tiles, final):
    m = pl.program_id(0)
    e = pl.program_id(1)
    tm = h2_ref.shape[0]
    n_chunks = tm // LANES
    nsel = PEER_HEADS * PEER_TOPK
    slot = lax.rem(m, 2)

    def project_queries(src_ref):
        q = lax.dot_general(wqt_ref[...], src_ref[...], _NT, preferred_element_type=F32).astype(BF16)
        for c in range(n_chunks):
            q_scr[c] = q[:, c * LANES:(c + 1) * LANES]

    def route_unit(u, dst):
        c = u // PEER_HEADS
        h = lax.rem(u, PEER_HEADS)
        qs = [q_scr[c, pl.ds(pl.multiple_of(h * (2 * N_KEYS) + p * N_KEYS, N_KEYS), N_KEYS), :]
              for p in range(2)]
        esel, gates = _route_head(qs, keys_ref)
        row = pl.multiple_of(h * PEER_TOPK, PEER_TOPK)
        e_scr[dst, c, pl.ds(row, PEER_TOPK), :] = esel
        g_scr[dst, c, pl.ds(row, PEER_TOPK), :] = gates

    @pl.when(jnp.logical_and(e == 0, m == 0))
    def _first_tile_routing():
        project_queries(h2_ref)

        def unit(u, c):
            route_unit(u, 0)
            return c

        lax.fori_loop(0, n_chunks * PEER_HEADS, unit, 0)

    @pl.when(e == 0)
    def _build():
        for c in range(n_chunks):
            et_scr[c * LANES:(c + 1) * LANES, :] = e_scr[slot, c].T
            gt_scr[c * LANES:(c + 1) * LANES, :] = g_scr[slot, c].T
        kio = lax.broadcasted_iota(I32, (N_KEYS, nsel), 0)

        def tok(t, c):
            erow = et_scr[pl.ds(t, 1), :]
            grow = gt_scr[pl.ds(t, 1), :]
            at = jnp.where(kio == (erow >> 7), grow, 0.0).astype(BF16)
            bt = jnp.where(kio == (erow & (N_KEYS - 1)), 1.0, 0.0).astype(BF16)
            gt = lax.dot_general(at, bt, _NT, preferred_element_type=F32)
            hi = pltpu.bitcast(gt[0:rows, :], jnp.uint32) & jnp.uint32(0xFFFF0000)
            lo = pltpu.bitcast(gt[rows:2 * rows, :], jnp.uint32) >> 16
            gs_scr[pl.ds(pl.multiple_of(t * pitch, SUBLANES), rows), :] = hi | lo
            return c

        lax.fori_loop(0, tm, tok, 0, unroll=16)
        acc_scr[...] = jnp.zeros_like(acc_scr)
        project_queries(h2n_ref)

    for k in range(units):
        route_unit(e * units + k, 1 - slot)

    per_sub = PEER_SUB // N_KEYS
    per_step = u_ref.shape[0] // N_KEYS
    steps_per_half = rows // per_step
    row0 = lax.rem(e, steps_per_half) * per_step
    shift = jnp.where(e < steps_per_half, 0, 16).astype(jnp.uint32)
    h2 = h2_ref[...]
    total = None
    for sb in range(per_step // per_sub):
        es = slice(sb * PEER_SUB, (sb + 1) * PEER_SUB)
        hmat = lax.dot_general(h2, u_ref[es, :], _NT, preferred_element_type=F32)
        words = jnp.concatenate(
            [gs_scr[pl.ds(row0 + sb * per_sub + ii, tm, stride=pitch), :] for ii in range(per_sub)], axis=1)
        gm = pltpu.bitcast((words << shift) & jnp.uint32(0xFFFF0000), F32)
        act = (jax.nn.gelu(hmat) * gm).astype(BF16)
        part = jnp.dot(act, v_ref[es, :], preferred_element_type=F32)
        total = part if total is None else total + part
    acc_scr[...] += total

    def result():
        x2 = x1_ref[...] + mod_ref[0][5:6, :] * acc_scr[...]
        return _rms(x2, fng_ref[...]) if final else x2

    last = e == pl.num_programs(1) - 1

    @pl.when(jnp.logical_and(last, m < n_ctx_tiles))
    def _out_ctx():
        oc_ref[...] = result()

    @pl.when(jnp.logical_and(last, m >= n_ctx_tiles))
    def _out_lat():
        ol_ref[...] = result()


def _peer(h2, wqt, keys, u, v, layer, x1, mod, fng, n_ctx_tok, lat_len, final):
    t = h2.shape[0]
    nsel = PEER_HEADS * PEER_TOPK
    rows = N_KEYS // 2
    pitch = rows + SUBLANES
    n_e = (N_KEYS * N_KEYS) // PEER_EB
    n_m = t // TM_PEER
    n_chunks = TM_PEER // LANES
    units = (n_chunks * PEER_HEADS) // n_e
    assert units * n_e == n_chunks * PEER_HEADS
    midx = functools.partial(_mod_index, tm=TM_PEER, n_ctx_tok=n_ctx_tok, lat_len=lat_len)
    tok = lambda wd: pl.BlockSpec((TM_PEER, wd), lambda m, e: (m, 0))
    nxt = pl.BlockSpec((TM_PEER, D_MODEL), lambda m, e: (jnp.minimum(m + 1, n_m - 1), 0))
    espec = pl.BlockSpec((None, PEER_EB, D_MODEL), lambda m, e: (layer, e, 0))
    single = dict(pipeline_mode=pl.Buffered(1))
    return pl.pallas_call(
        functools.partial(_peer_body, rows=rows, pitch=pitch, units=units,
                          n_ctx_tiles=n_ctx_tok // TM_PEER, final=final),
        grid=(n_m, n_e),
        in_specs=[tok(D_MODEL), nxt,
                  pl.BlockSpec(wqt.shape, lambda m, e: (0, 0), **single),
                  pl.BlockSpec(keys.shape, lambda m, e: (0, 0, 0), **single),
                  espec, espec,
                  pl.BlockSpec((TM_PEER, D_MODEL), lambda m, e: (m, 0), **single),
                  pl.BlockSpec((1, N_MOD, D_MODEL), lambda m, e: (midx(m), 0, 0)),
                  _const_spec((1, D_MODEL))],
        out_specs=_dual_specs(TM_PEER, n_ctx_tok // TM_PEER, **single),
        out_shape=[jax.ShapeDtypeStruct((n_ctx_tok, D_MODEL), F32),
                   jax.ShapeDtypeStruct((t - n_ctx_tok, D_MODEL), F32)],
        scratch_shapes=[pltpu.VMEM((n_chunks, wqt.shape[0], LANES), BF16),
                        pltpu.VMEM((2, n_chunks, nsel, LANES), I32),
                        pltpu.VMEM((2, n_chunks, nsel, LANES), F32),
                        pltpu.VMEM((TM_PEER, nsel), I32),
                        pltpu.VMEM((TM_PEER, nsel), F32),
                        pltpu.VMEM((TM_PEER * pitch, N_KEYS), jnp.uint32),
                        pltpu.VMEM((TM_PEER, D_MODEL), F32)],
        compiler_params=_cparams(("arbitrary", "arbitrary"), VMEM_LIMIT),
        name="peer",
    )(h2, h2, wqt, keys, u, v, x1, mod, fng)


def _wkv_state_in(s, vs):
    n = s.shape[0]
    vl = HS_WKV // vs
    s = s.reshape(n, N_DIR, H_WKV, vs, vl, HS_WKV).transpose(1, 4, 5, 3, 0, 2)
    return s.reshape(N_DIR, vl, HS_WKV, vs * n * H_WKV)


def _wkv_state_out(s, n_sb, spb, vs):
    vl = HS_WKV // vs
    s = s.reshape(N_DIR, n_sb, vl, HS_WKV, vs, spb, H_WKV).transpose(1, 5, 0, 6, 4, 2, 3)
    return s.reshape(n_sb * spb, N_DIR, H_WKV, HS_WKV, HS_WKV)


def _layer_weights(i, prm):
    eye_h = jnp.eye(H_LRU, dtype=F32)
    eye_d = jnp.eye(N_DIR, dtype=F32)
    perm = _WKV_PERM

    def lru_bd(wt):
        return jnp.einsum("dhij,hg->hidgj", wt, eye_h).reshape(W_MIX, N_DIR * W_MIX)

    def lora_bd(wt):
        r = wt.shape[1]
        return jnp.einsum("drc,de->drec", wt, eye_d).reshape(N_DIR * r, N_DIR * W_MIX)

    w_in = prm["w_in"][i]
    pad = jnp.zeros((D_MODEL, Z_COLS - 5504), F32)
    rkv = [w_in[:, 2560 + j * W_MIX:2560 + (j + 1) * W_MIX][:, perm] for j in range(3)]
    w_in_perm = jnp.concatenate(
        [w_in[:, 0:1536]] + rkv + [w_in[:, 1536:2560], w_in[:, 4480:5504], w_in[:, 4096:4480], pad],
        axis=1).astype(BF16)
    row = lambda x: x.reshape(1, -1).astype(F32)
    head_of = np.arange(W_MIX) % H_WKV
    w_branch = prm["w_branch"][i]
    w_branch = jnp.concatenate([w_branch[0], w_branch[1], w_branch[2][perm, :], w_branch[3]], axis=0)
    return {
        "w_in": w_in_perm,
        "w_gate": w_in[:, 5504:].astype(BF16),
        "norm1_g": row(prm["norm1_g"][i]),
        "norm2_g": row(prm["norm2_g"][i]),
        "conv_a_w": prm["conv_a_w"][i],
        "conv_b_w": prm["conv_b_w"][i],
        "conv_b_b": row(prm["conv_b_b"][i]),
        "lru_w": jnp.concatenate([lru_bd(prm["lru_wa"][i]), lru_bd(prm["lru_wx"][i])], axis=1).astype(BF16),
        "lru_b": jnp.concatenate([row(prm["lru_ba"][i]), row(prm["lru_bx"][i])], axis=1),
        "lru_lam": row(prm["lru_lambda"][i]),
        "w0": row(prm["rwkv_w0"][i][:, perm]),
        "w2": lora_bd(prm["rwkv_w2"][i][:, :, perm]).astype(BF16),
        "a0": row(prm["rwkv_a0"][i][:, perm]),
        "a2": lora_bd(prm["rwkv_a2"][i][:, :, perm]).astype(BF16),
        "g2": prm["rwkv_g2"][i][:, perm].astype(BF16),
        "kk": row(prm["rwkv_kk"][i][perm]),
        "ka": row(prm["rwkv_ka"][i][perm]),
        "rk": row(prm["rwkv_rk"][i].reshape(W_MIX)[perm]),
        "lnx_g": row(prm["lnx_g"][i][perm]),
        "lnx_b": row(prm["lnx_b"][i][perm]),
        "sg_ln_g": row(prm["sg_ln_g"][i]),
        "sg_ln_b": row(prm["sg_ln_b"][i]),
        "sg_ws": prm["sg_ws"][i].astype(BF16),
        "sg_bst": prm["sg_bs"][i].T,
        "gate_b": row(prm["gate_b"][i]),
        "w_branch": w_branch.astype(BF16),
        "w_out": prm["w_out"][i].astype(BF16),
        "wq_t": prm["peer_wq"][i].T.astype(BF16),
        "keys": prm["peer_keys"][i].astype(BF16),
        "ind": jnp.asarray(head_of[:, None] == head_of[None, :], BF16),
    }


def kernel(x_prompt, x_sample, state_lru, state_wkv, c, c_ctx, norm1_g, norm2_g, w_mod, b_mod, w_in, conv_a_w, conv_b_w, conv_b_b, lru_wa, lru_ba, lru_wx, lru_bx, lru_lambda, rwkv_w0, rwkv_w2, rwkv_a0, rwkv_a2, rwkv_g2, rwkv_kk, rwkv_ka, rwkv_rk, lnx_g, lnx_b, sg_ln_g, sg_ln_b, sg_ws, sg_bs, gate_b, w_branch, w_out, peer_wq, peer_keys, peer_u, peer_v, final_norm_g):
    prm = dict(norm1_g=norm1_g, norm2_g=norm2_g, w_in=w_in, conv_a_w=conv_a_w, conv_b_w=conv_b_w,
               conv_b_b=conv_b_b, lru_wa=lru_wa, lru_ba=lru_ba, lru_wx=lru_wx, lru_bx=lru_bx,
               lru_lambda=lru_lambda, rwkv_w0=rwkv_w0, rwkv_w2=rwkv_w2, rwkv_a0=rwkv_a0, rwkv_a2=rwkv_a2,
               rwkv_g2=rwkv_g2, rwkv_kk=rwkv_kk, rwkv_ka=rwkv_ka, rwkv_rk=rwkv_rk, lnx_g=lnx_g,
               lnx_b=lnx_b, sg_ln_g=sg_ln_g, sg_ln_b=sg_ln_b, sg_ws=sg_ws, sg_bs=sg_bs, gate_b=gate_b,
               w_branch=w_branch, w_out=w_out, peer_wq=peer_wq, peer_keys=peer_keys, peer_u=peer_u,
               peer_v=peer_v)
    bc, lc, _ = x_prompt.shape
    bl, ll, _ = x_sample.shape
    depth = w_mod.shape[0]
    n_ctx_tok = bc * lc
    n_ctx_tiles = n_ctx_tok // TM
    lat_tiles = ll // TM
    ctx_spb = min(WKV_CTX_SPB, bc)
    lru_spb = ll // lc
    assert lc == TM and ll % TM_PEER == 0 and n_ctx_tok % TM_PEER == 0 and bl + 1 <= SUBLANES
    assert ll % GRID_W == 0 and TM % GRID_W == 0 and bc % ctx_spb == 0 and n_ctx_tok % ll == 0
    assert LANES % (ctx_spb * H_WKV) == 0 and LANES % (bl * H_WKV) == 0

    cond = jnp.zeros((SUBLANES, D_MODEL), F32).at[0].set(c_ctx).at[1:1 + bl].set(c)
    mods = _modulation(cond, w_mod, b_mod).reshape(depth, SUBLANES, N_MOD, D_MODEL)
    fng = final_norm_g.reshape(1, D_MODEL)
    x_ctx = x_prompt.reshape(n_ctx_tok, D_MODEL)
    x_lat = x_sample.reshape(bl * ll, D_MODEL)
    ctx_vs = LANES // (ctx_spb * H_WKV)
    lat_vs = LANES // (bl * H_WKV)
    n_sb = bc // ctx_spb
    wkv_zero = jnp.zeros((N_DIR * n_sb, HS_WKV // ctx_vs, HS_WKV, LANES), F32)
    lru_zero = jnp.zeros((n_ctx_tok // ll, lru_spb, N_DIR * W_MIX), F32)
    u_all = peer_u.astype(BF16)
    v_all = peer_v.astype(BF16)
    new_lru, new_wkv = [], []
    pnames = ["ya", "yd", "gbg", "la", "lu", "g", "bon", "rt", "vt", "kkt", "wt", "kt", "bt"]
    for i in range(depth):
        lw = _layer_weights(i, prm)
        mod = mods[i]
        p = dict(zip(pnames, _prep(x_ctx, x_lat, mod, lw, ll)))

        lat_h0 = jnp.zeros((bl, lru_spb, N_DIR * W_MIX), F32).at[:, 0].set(
            state_lru[:, i].astype(F32).reshape(bl, N_DIR * W_MIX))
        h, lru_s = _lru_scan(p["la"], p["lu"], jnp.concatenate([lru_zero, lat_h0], axis=0),
                             rows=ll, n_ctx_blocks=n_ctx_tok // ll, ctx_cfg=(lru_spb, lc), lat_cfg=(1, ll))
        new_lru.append(lru_s[:n_ctx_tok // ll].reshape(bc, N_DIR, W_MIX))

        wkv_in = [p[n] for n in ("rt", "wt", "kt", "kkt", "bt", "vt")]
        y_c, s_c = _wkv_scan(*wkv_in, wkv_zero, tile0=0, n_seq=bc, seq_tiles=1, spb=ctx_spb)
        y_l, _ = _wkv_scan(*wkv_in, _wkv_state_in(state_wkv[:, i].astype(F32), lat_vs),
                           tile0=n_ctx_tiles, n_seq=bl, seq_tiles=lat_tiles, spb=1)
        new_wkv.append(_wkv_state_out(s_c, n_sb, ctx_spb, ctx_vs))

        tok_in = [p["ya"], p["yd"], p["gbg"], h, p["g"], p["bon"]]
        x1, h2 = _merge(x_ctx, x_lat, mod, lw, tok_in, y_c, y_l, ll)
        x_ctx, x_lat = _peer(h2, lw["wq_t"], lw["keys"], u_all, v_all, i, x1, mod, fng,
                             n_ctx_tok, ll, final=(i == depth - 1))
    y_prompt = x_ctx.reshape(bc, lc, D_MODEL)
    y_sample = x_lat.reshape(bl, ll, D_MODEL)
    return (y_prompt, y_sample, jnp.stack(new_lru, axis=1), jnp.stack(new_wkv, axis=1))
```

```python
import functools

import numpy as np
import jax
import jax.numpy as jnp
from jax import lax
from jax.experimental import pallas as pl
from jax.experimental.pallas import tpu as pltpu

F32 = jnp.float32
BF16 = jnp.bfloat16
I32 = jnp.int32

D_MODEL = 1024
W_MIX = 512
N_DIR = 2
N_BRANCH = 4
H_WKV = 8
HS_WKV = 64
H_LRU = 8
HB_LRU = 64
LORA_W = 64
LORA_A = 64
LORA_G = 128
GRID_W = 64
CHUNK = 128
G_SG = 4
N_KEYS = 128
PEER_HEADS = 8
PEER_TOPK = 16
N_MOD = 6
EPS = 1e-6
LNX_EPS = 64e-5
LRU_C = 8.0

LANES = 128
SUBLANES = 8
TM = 256
TM_PEER = 512
PEER_EB = 1024
PEER_SUB = 512
Z_COLS = 5632
WKV_TC = LANES
WKV_CTX_SPB = 8
VMEM_LIMIT = 56 * 1024 * 1024

_NT = (((1,), (1,)), ((), ()))
_WKV_PERM = np.array([(n % H_WKV) * HS_WKV + n // H_WKV for n in range(W_MIX)])


def _cparams(sem, vmem=None):
    return pltpu.CompilerParams(dimension_semantics=sem, vmem_limit_bytes=vmem)


def _const_spec(shape):
    nd = len(shape)
    return pl.BlockSpec(shape, lambda *_: (0,) * nd)


def _softplus(x):
    return jnp.maximum(x, 0.0) + jnp.log1p(jnp.exp(-jnp.abs(x)))


def _rms(x, g):
    return x * lax.rsqrt(jnp.mean(x * x, axis=-1, keepdims=True) + EPS) * g


def _segsum(x, ind):
    hi = x.astype(BF16)
    lo = (x - hi.astype(F32)).astype(BF16)
    return (jnp.dot(hi, ind, preferred_element_type=F32)
            + jnp.dot(lo, ind, preferred_element_type=F32))


def _mod_index(i, tm, n_ctx_tok, lat_len):
    n_ctx_tiles = n_ctx_tok // tm
    tiles_per_seq = lat_len // tm
    return jnp.where(i < n_ctx_tiles, 0, 1 + lax.div(i - n_ctx_tiles, tiles_per_seq))


def _mod_body(s_ref, w_ref, b_ref, o_ref):
    s = s_ref[...]
    s = s * jax.nn.sigmoid(s)
    o_ref[0] = jnp.dot(s.astype(BF16), w_ref[0].astype(BF16), preferred_element_type=F32) + b_ref[0]


def _modulation(cond, w_mod, b_mod):
    depth = w_mod.shape[0]
    n = w_mod.shape[2]
    tn = 1536
    return pl.pallas_call(
        _mod_body,
        grid=(depth, n // tn),
        in_specs=[_const_spec((SUBLANES, D_MODEL)),
                  pl.BlockSpec((1, D_MODEL, tn), lambda l, j: (l, 0, j)),
                  pl.BlockSpec((1, 1, tn), lambda l, j: (l, 0, j))],
        out_specs=pl.BlockSpec((1, SUBLANES, tn), lambda l, j: (l, 0, j)),
        out_shape=jax.ShapeDtypeStruct((depth, SUBLANES, n), F32),
        compiler_params=_cparams(("parallel", "parallel"), VMEM_LIMIT),
        name="modulation",
    )(cond, w_mod, b_mod.reshape(depth, 1, n))


def _prep_body(xc_ref, xl_ref, xp_ref, xn_ref, mod_ref, n1g_ref, win_ref,
               caw_ref, cbw_ref, cbb_ref, lruw_ref, lrub_ref, lam_ref,
               w0_ref, w2_ref, a0_ref, a2_ref, g2_ref, kkw_ref, ka_ref, rk_ref,
               lng_ref, lnb_ref, ws_ref, bst_ref, ind_ref,
               ya_ref, yd_ref, gbg_ref, la_ref, lu_ref, g_ref, bon_ref,
               rt_ref, vt_ref, kkt_ref, wt_ref, kt_ref, bt_ref,
               *, n_ctx_tiles, tiles_per_seq):
    i = pl.program_id(0)
    is_ctx = i < n_ctx_tiles
    t = lax.broadcasted_iota(I32, (TM, 1), 0)
    ind = ind_ref[...]
    m = mod_ref[0]

    def modulated(xv):
        return (_rms(xv, n1g_ref[...]) * (1.0 + m[1:2, :]) + m[0:1, :]).astype(BF16)

    def project(hv, lo, hi):
        return jnp.dot(hv, win_ref[:, lo:hi], preferred_element_type=F32)

    h = modulated(jnp.where(is_ctx, xc_ref[...], xl_ref[...]))
    za = project(h, 0, 1536)
    zc = project(h, 1536, 3072)
    zb = project(h, 3072, 4096)
    zd = project(h, 4096, 5120)
    zl = project(h, 5120, 5120 + 2 * LORA_W + 2 * LORA_A + LORA_G)
    halo_prev = project(modulated(xp_ref[...]), 3072 + W_MIX, 4096)
    halo_next = project(modulated(xn_ref[...]), 3072 + W_MIX, 4096)

    pm = jnp.where(is_ctx, TM - 1, GRID_W - 1)
    pos = t & pm
    a_b = za[:, 0:W_MIX]
    ac = za[:, W_MIX:2 * W_MIX] * za[:, 2 * W_MIX:3 * W_MIX]
    up = jnp.where(pos == 0, 0.0, pltpu.roll(ac, 1, 0))
    dn = jnp.where(pos == pm, 0.0, pltpu.roll(ac, TM - 1, 0))
    ya_ref[...] = a_b * (caw_ref[0:1, :] * up + caw_ref[1:2, :] * ac + caw_ref[2:3, :] * dn)

    seq_tile = lax.rem(jnp.maximum(i - n_ctx_tiles, 0), tiles_per_seq)
    first = jnp.logical_or(is_ctx, seq_tile == 0)
    last = jnp.logical_or(is_ctx, seq_tile == tiles_per_seq - 1)
    prev = jnp.where(first, 0.0, halo_prev[SUBLANES - 1:SUBLANES, :])
    nxt0 = jnp.where(last, 0.0, halo_next[0:1, :])
    nxt1 = jnp.where(last, 0.0, halo_next[1:2, :])
    bx = zb[:, W_MIX:2 * W_MIX]
    m1 = jnp.where(t == 0, prev, pltpu.roll(bx, 1, 0))
    p1 = jnp.where(t == TM - 1, nxt0, pltpu.roll(bx, TM - 1, 0))
    p2 = jnp.where(t == TM - 2, nxt0, jnp.where(t == TM - 1, nxt1, pltpu.roll(bx, TM - 2, 0)))
    xb = (cbw_ref[0:1, :] * m1 + cbw_ref[1:2, :] * bx + cbw_ref[2:3, :] * p1
          + cbw_ref[3:4, :] * p2 + cbb_ref[...])
    gates = jnp.dot(xb.astype(BF16), lruw_ref[...], preferred_element_type=F32) + lrub_ref[...]
    rg = jax.nn.sigmoid(gates[:, 0:2 * W_MIX])
    ig = jax.nn.sigmoid(gates[:, 2 * W_MIX:4 * W_MIX])
    log_a = -LRU_C * rg * _softplus(-lam_ref[...])
    xb2 = jnp.concatenate([xb, xb], axis=1)
    la_ref[...] = jnp.exp(log_a)
    lu_ref[...] = jnp.sqrt(jnp.tanh(-log_a) * (jnp.exp(2.0 * log_a) + 1.0)) * (ig * xb2)
    gbg_ref[...] = jax.nn.gelu(zb[:, 0:W_MIX])

    zr = zc[:, 0:W_MIX]
    zk = zc[:, W_MIX:2 * W_MIX]
    zv = zc[:, 2 * W_MIX:3 * W_MIX]
    zwd = zl[:, 0:2 * LORA_W]
    zad = zl[:, 2 * LORA_W:2 * LORA_W + 2 * LORA_A]
    zgd = zl[:, 2 * LORA_W + 2 * LORA_A:2 * LORA_W + 2 * LORA_A + LORA_G]
    wlin = w0_ref[...] + jnp.dot(jnp.tanh(zwd).astype(BF16), w2_ref[...], preferred_element_type=F32)
    wt_ref[0] = jnp.exp(-jnp.exp(-_softplus(-wlin) - 0.5)).T
    av = jax.nn.sigmoid(a0_ref[...] + jnp.dot(zad.astype(BF16), a2_ref[...], preferred_element_type=F32))
    g_ref[...] = jnp.dot(jax.nn.sigmoid(zgd).astype(BF16), g2_ref[...], preferred_element_type=F32)
    kkr = zk * kkw_ref[...]
    kkn = kkr / jnp.maximum(jnp.sqrt(_segsum(kkr * kkr, ind)), 1e-12)
    zk2 = jnp.concatenate([zk, zk], axis=1)
    ka2 = jnp.concatenate([ka_ref[...], ka_ref[...]], axis=1)
    kd = zk2 * (1.0 + (av - 1.0) * ka2)
    kt_ref[0] = kd.T
    bt_ref[0] = (jnp.concatenate([kkn, kkn], axis=1) * av).T
    rt_ref[0] = zr.T
    vt_ref[0] = zv.T
    kkt_ref[0] = kkn.T
    bon_ref[...] = _segsum(zr * (kd[:, 0:W_MIX] + kd[:, W_MIX:2 * W_MIX]) * rk_ref[...], ind) * zv

    zg = jax.nn.gelu(zd)
    u = zg[:, 0:W_MIX]
    vv = zg[:, W_MIX:2 * W_MIX]
    vc = vv - jnp.mean(vv, axis=-1, keepdims=True)
    vn = vc * lax.rsqrt(jnp.mean(vc * vc, axis=-1, keepdims=True) + 1e-5) * lng_ref[...] + lnb_ref[...]
    for c in range(TM // CHUNK):
        rs = slice(c * CHUNK, (c + 1) * CHUNK)
        for gi in range(G_SG):
            cs = slice(gi * LANES, (gi + 1) * LANES)
            s = jnp.dot(ws_ref[gi], vn[rs, cs].astype(BF16), preferred_element_type=F32)
            yd_ref[rs, cs] = u[rs, cs] * (s + bst_ref[:, gi:gi + 1])


def _dual_specs(rows, n_ctx_blocks, **kw):
    return [pl.BlockSpec((rows, D_MODEL), lambda i, *_: (jnp.minimum(i, n_ctx_blocks - 1), 0), **kw),
            pl.BlockSpec((rows, D_MODEL), lambda i, *_: (jnp.maximum(i - n_ctx_blocks, 0), 0), **kw)]


def _prep(x_ctx, x_lat, mod, lw, lat_len):
    n_ctx_tok = x_ctx.shape[0]
    t = n_ctx_tok + x_lat.shape[0]
    n_tiles = t // TM
    n_ctx_tiles = n_ctx_tok // TM
    tiles_per_seq = lat_len // TM
    rows8 = TM // SUBLANES
    last_blk = x_lat.shape[0] // SUBLANES - 1
    midx = functools.partial(_mod_index, tm=TM, n_ctx_tok=n_ctx_tok, lat_len=lat_len)

    def lat_blk8(i, off):
        return (jnp.clip((i - n_ctx_tiles) * rows8 + off, 0, last_blk), 0)

    x_specs = _dual_specs(TM, n_ctx_tiles) + [
        pl.BlockSpec((SUBLANES, D_MODEL), lambda i: lat_blk8(i, -1)),
        pl.BlockSpec((SUBLANES, D_MODEL), lambda i: lat_blk8(i, rows8)),
        pl.BlockSpec((1, N_MOD, D_MODEL), lambda i: (midx(i), 0, 0)),
        _const_spec((1, D_MODEL)),
        pl.BlockSpec(lw["w_in"].shape, lambda i: (0, 0), pipeline_mode=pl.Buffered(1)),
    ]
    wnames = ["conv_a_w", "conv_b_w", "conv_b_b", "lru_w", "lru_b", "lru_lam", "w0", "w2", "a0", "a2",
              "g2", "kk", "ka", "rk", "sg_ln_g", "sg_ln_b", "sg_ws", "sg_bst", "ind"]
    wts = [lw[n] for n in wnames]
    w_specs = [_const_spec(w.shape) for w in wts]
    widths = [W_MIX, W_MIX, W_MIX, 2 * W_MIX, 2 * W_MIX, W_MIX, W_MIX]
    t_rows = [W_MIX, W_MIX, W_MIX, 2 * W_MIX, 2 * W_MIX, 2 * W_MIX]
    out_specs = ([pl.BlockSpec((TM, wd), lambda i: (i, 0)) for wd in widths]
                 + [pl.BlockSpec((1, r, TM), lambda i: (i, 0, 0)) for r in t_rows])
    out_shape = ([jax.ShapeDtypeStruct((t, wd), F32) for wd in widths]
                 + [jax.ShapeDtypeStruct((n_tiles, r, TM), F32) for r in t_rows])
    return pl.pallas_call(
        functools.partial(_prep_body, n_ctx_tiles=n_ctx_tiles, tiles_per_seq=tiles_per_seq),
        grid=(n_tiles,),
        in_specs=x_specs + w_specs,
        out_specs=out_specs,
        out_shape=out_shape,
        compiler_params=_cparams(("parallel",), VMEM_LIMIT),
        name="branch_prep",
    )(x_ctx, x_lat, x_lat, x_lat, mod, lw["norm1_g"], lw["w_in"], *wts)


def _lru_body(a_ref, u_ref, h0_ref, h_ref, hf_ref, *, n_ctx_blocks, ctx_cfg, lat_cfg):
    fw, bw = slice(0, W_MIX), slice(W_MIX, 2 * W_MIX)

    def scan(nseq, l):
        def step(s, carry):
            out = []
            for j in range(nseq):
                tf = j * l + s
                tb = j * l + (l - 1 - s)
                hf = a_ref[pl.ds(tf, 1), fw] * carry[2 * j] + u_ref[pl.ds(tf, 1), fw]
                hb = a_ref[pl.ds(tb, 1), bw] * carry[2 * j + 1] + u_ref[pl.ds(tb, 1), bw]
                h_ref[pl.ds(tf, 1), fw] = hf
                h_ref[pl.ds(tb, 1), bw] = hb
                out += [hf, hb]
            return tuple(out)

        init = []
        for j in range(nseq):
            init += [h0_ref[0, j:j + 1, fw], h0_ref[0, j:j + 1, bw]]
        fin = lax.fori_loop(0, l, step, tuple(init), unroll=2)
        hf_ref[0] = h0_ref[0]
        for j in range(nseq):
            hf_ref[0, j:j + 1, fw] = fin[2 * j]
            hf_ref[0, j:j + 1, bw] = fin[2 * j + 1]

    is_ctx = pl.program_id(0) < n_ctx_blocks
    pl.when(is_ctx)(lambda: scan(*ctx_cfg))
    pl.when(jnp.logical_not(is_ctx))(lambda: scan(*lat_cfg))


def _lru_scan(a, u, h0, *, rows, n_ctx_blocks, ctx_cfg, lat_cfg):
    nb = a.shape[0] // rows
    w = a.shape[1]
    tok = pl.BlockSpec((rows, w), lambda i: (i, 0))
    st = pl.BlockSpec((1,) + h0.shape[1:], lambda i: (i, 0, 0))
    return pl.pallas_call(
        functools.partial(_lru_body, n_ctx_blocks=n_ctx_blocks, ctx_cfg=ctx_cfg, lat_cfg=lat_cfg),
        grid=(nb,),
        in_specs=[tok, tok, st],
        out_specs=[tok, st],
        out_shape=[jax.ShapeDtypeStruct(a.shape, F32), jax.ShapeDtypeStruct(h0.shape, F32)],
        compiler_params=_cparams(("parallel",), VMEM_LIMIT),
        name="lru_scan",
    )(a, u, h0)


_SLOT_ORDER = (0, 4, 2, 6, 1, 5, 3, 7)


def _rowsum8(parts):
    sub = lax.broadcasted_iota(I32, (SUBLANES, LANES), 0)
    slots = [parts[i] for i in _SLOT_ORDER]
    roll = pltpu.roll
    lvl1 = [jnp.where(sub < 4, a + roll(a, 4, 0), b + roll(b, 4, 0))
            for a, b in zip(slots[0::2], slots[1::2])]
    lvl2 = [jnp.where((sub & 3) < 2, a + roll(a, 6, 0), roll(b + roll(b, 6, 0), 2, 0))
            for a, b in zip(lvl1[0::2], lvl1[1::2])]
    a, b = lvl2
    return jnp.where((sub & 1) == 0, a + roll(a, 7, 0), roll(b + roll(b, 7, 0), 1, 0))


def _fold8(x):
    return jnp.sum(x.reshape(HS_WKV // SUBLANES, SUBLANES, LANES), axis=0)


def _wkv_body(*refs, nsrc, spb, vs, n_sb, tc, kp, vp):
    vl_n = HS_WKV // vs
    n_in = 6 * nsrc
    k_srcs = [refs[o * nsrc:(o + 1) * nsrc] for o in range(5)]
    v_srcs = refs[5 * nsrc:n_in]
    s0_ref = refs[n_in]
    y_ref = refs[n_in + 1]
    sf_ref = refs[n_in + 2]
    k_scr = refs[n_in + 3:n_in + 8]
    v_scr, y_scr, s_scr, sa_scr = refs[n_in + 8:n_in + 12]
    r_scr, w_scr, k_scr_, kk_scr, b_scr = k_scr
    backward = pl.program_id(0) // n_sb == 1
    seqs = [(s, j) for s in range(nsrc) for j in range(spb)]

    @pl.when(pl.program_id(1) == 0)
    def _():
        s_scr[...] = s0_ref[0]

    def build_k(c, carry):
        row = pl.multiple_of(c * H_WKV, H_WKV)
        for o in range(5):
            slab = [k_srcs[o][s][j, pl.ds(row, H_WKV), :] for s, j in seqs]
            k_scr[o][pl.ds(c, tc, stride=kp), :] = jnp.concatenate(slab * vs, axis=0).T
        return carry

    lax.fori_loop(0, HS_WKV, build_k, 0, unroll=4)

    def build_v(vl, carry):
        slab = []
        for vsi in range(vs):
            row = pl.multiple_of((vsi * vl_n + vl) * H_WKV, H_WKV)
            slab += [v_srcs[s][j, pl.ds(row, H_WKV), :] for s, j in seqs]
        v_scr[pl.ds(vl, tc, stride=vp), :] = jnp.concatenate(slab, axis=0).T
        return carry

    lax.fori_loop(0, vl_n, build_v, 0, unroll=4)

    def step(s, carry):
        t = jnp.where(backward, tc - 1 - s, s)
        krow = pl.multiple_of(t * kp, SUBLANES)
        vrow = pl.multiple_of(t * vp, SUBLANES)
        kslab = pl.ds(krow, HS_WKV)
        for g in range(vl_n // SUBLANES):
            parts = [_fold8(s_scr[g * SUBLANES + i] * kk_scr[kslab, :]) for i in range(SUBLANES)]
            sa_scr[g * SUBLANES:(g + 1) * SUBLANES, :] = _rowsum8(parts)
        for g in range(vl_n // SUBLANES):
            parts = []
            for i in range(SUBLANES):
                vl = g * SUBLANES + i
                sa = sa_scr[vl:vl + 1, :]
                vv = v_scr[pl.ds(vrow + vl, 1), :]
                sn = s_scr[vl] * w_scr[kslab, :] - sa * b_scr[kslab, :] + vv * k_scr_[kslab, :]
                s_scr[vl] = sn
                parts.append(_fold8(sn * r_scr[kslab, :]))
            y_scr[pl.ds(pl.multiple_of(vrow + g * SUBLANES, SUBLANES), SUBLANES), :] = _rowsum8(parts)
        return carry

    lax.fori_loop(0, tc, step, 0)

    def emit_y(vl, carry):
        yt = y_scr[pl.ds(vl, tc, stride=vp), :].T
        for vsi in range(vs):
            row = pl.multiple_of((vsi * vl_n + vl) * H_WKV, H_WKV)
            for n, (s, j) in enumerate(seqs):
                lane0 = (vsi * len(seqs) + n) * H_WKV
                y_ref[0, 0, s * spb + j, pl.ds(row, H_WKV), :] = yt[lane0:lane0 + H_WKV, :]
        return carry

    lax.fori_loop(0, vl_n, emit_y, 0, unroll=4)
    sf_ref[0] = s_scr[...]


def _wkv_scan(rt, wt, kt, kkt, bt, vt, s0, *, tile0, n_seq, seq_tiles, spb):
    tc = WKV_TC
    if spb > 1:
        assert seq_tiles == 1 and n_seq % spb == 0 and tile0 % spb == 0
        nsrc, n_sb = 1, n_seq // spb
    else:
        nsrc, n_sb = n_seq, 1
    inst = nsrc * spb * H_WKV
    vs = LANES // inst
    vl_n = HS_WKV // vs
    assert vl_n % SUBLANES == 0, "value rows are processed eight at a time"
    cpt = TM // tc
    n_chunks = seq_tiles * cpt
    kp = HS_WKV + SUBLANES
    vp = vl_n + SUBLANES if ((vl_n + SUBLANES) // SUBLANES) % 2 else vl_n + 2 * SUBLANES

    def chunk(g, i):
        return jnp.where(g // n_sb == 1, n_chunks - 1 - i, i)

    def in_map(g, i, *, src, per_dir):
        ce = chunk(g, i)
        rb = (g // n_sb) if per_dir else 0
        if spb > 1:
            return (tile0 // spb + g % n_sb, rb, ce)
        return (tile0 + src * seq_tiles + ce // cpt, rb, ce % cpt)

    def out_map(g, i):
        ce = chunk(g, i)
        if spb > 1:
            return (g // n_sb, 0, g % n_sb, 0, ce)
        return (g // n_sb, ce // cpt, 0, 0, ce % cpt)

    in_specs, operands = [], []
    for arr, per_dir in ((rt, False), (wt, True), (kt, True), (kkt, False), (bt, True), (vt, False)):
        for src in range(nsrc):
            in_specs.append(pl.BlockSpec((spb, W_MIX, tc), functools.partial(in_map, src=src, per_dir=per_dir),
                                         pipeline_mode=pl.Buffered(1)))
            operands.append(arr)
    sspec = pl.BlockSpec((1, vl_n, HS_WKV, LANES), lambda g, i: (g, 0, 0, 0))
    in_specs.append(sspec)
    out_specs = [pl.BlockSpec((1, 1, nsrc * spb, W_MIX, tc), out_map)]
    out_shape = [jax.ShapeDtypeStruct((N_DIR, seq_tiles, n_seq, W_MIX, TM), F32)]
    res = pl.pallas_call(
        functools.partial(_wkv_body, nsrc=nsrc, spb=spb, vs=vs, n_sb=n_sb, tc=tc, kp=kp, vp=vp),
        grid=(N_DIR * n_sb, n_chunks),
        in_specs=in_specs,
        out_specs=out_specs + [sspec],
        out_shape=out_shape + [jax.ShapeDtypeStruct(s0.shape, F32)],
        scratch_shapes=([pltpu.VMEM((tc * kp, LANES), F32)] * 5
                        + [pltpu.VMEM((tc * vp, LANES), F32)] * 2
                        + [pltpu.VMEM((vl_n, HS_WKV, LANES), F32), pltpu.VMEM((vl_n, LANES), F32)]),
        compiler_params=_cparams(("parallel", "arbitrary"), VMEM_LIMIT),
        name="wkv_scan",
    )(*operands, s0)
    return res[0], res[1]


def _merge_body(xc_ref, xl_ref, mod_ref, n1g_ref, n2g_ref, wg_ref, gb_ref, wbr_ref, wo_ref,
                lnxg_ref, lnxb_ref, ind_ref,
                ya_ref, yd_ref, gbg_ref, h_ref, ycf_ref, ycb_ref, ylf_ref, ylb_ref, g_ref, bon_ref,
                x1_ref, h2_ref, *, n_ctx_tiles):
    is_ctx = pl.program_id(0) < n_ctx_tiles
    x = jnp.where(is_ctx, xc_ref[...], xl_ref[...])
    m = mod_ref[0]
    ind = ind_ref[...]
    h = (_rms(x, n1g_ref[...]) * (1.0 + m[1:2, :]) + m[0:1, :]).astype(BF16)
    y_b = gbg_ref[...] * (h_ref[:, 0:W_MIX] + h_ref[:, W_MIX:2 * W_MIX])
    y =jnp.where(is_ctx, ycf_ref[0, 0, 0] + ycb_ref[0, 0, 0], ylf_ref[0, 0, 0] + ylb_ref[0, 0, 0]).T
    yc = y - _segsum(y, ind) * (1.0 / HS_WKV)
    var = _segsum(yc * yc, ind) * (1.0 / HS_WKV)
    y_c = (yc * lax.rsqrt(var + LNX_EPS) * lnxg_ref[...] + lnxb_ref[...] + bon_ref[...]) * g_ref[...]
    merged = None
    for n, yn in enumerate((ya_ref[...], y_b, y_c, yd_ref[...])):
        cs = slice(n * D_MODEL, (n + 1) * D_MODEL)
        gate = jax.nn.sigmoid(jnp.dot(h, wg_ref[:, cs], preferred_element_type=F32) + gb_ref[:, cs])
        br = jnp.dot(yn.astype(BF16), wbr_ref[n * W_MIX:(n + 1) * W_MIX, :], preferred_element_type=F32)
        merged = gate * br if merged is None else merged + gate * br
    mo = jnp.dot(merged.astype(BF16), wo_ref[...], preferred_element_type=F32)
    x1 = x + m[2:3, :] * mo
    x1_ref[...] = x1
    h2_ref[...] = (_rms(x1, n2g_ref[...]) * (1.0 + m[4:5, :]) + m[3:4, :]).astype(BF16)


def _merge(x_ctx, x_lat, mod, lw, tok_in, y_ctx, y_lat, lat_len):
    n_ctx_tok = x_ctx.shape[0]
    t = n_ctx_tok + x_lat.shape[0]
    n_ctx_tiles = n_ctx_tok // TM
    tps = lat_len // TM
    midx = functools.partial(_mod_index, tm=TM, n_ctx_tok=n_ctx_tok, lat_len=lat_len)
    wnames = ["norm1_g", "norm2_g", "w_gate", "gate_b", "w_branch", "w_out", "lnx_g", "lnx_b", "ind"]
    wts = [lw[n] for n in wnames]
    tok = lambda wd: pl.BlockSpec((TM, wd), lambda i: (i, 0))
    ya, yd, gbg, h, g, bon = tok_in
    yblock = (1, 1, 1, W_MIX, TM)

    def ctx_spec(d):
        return pl.BlockSpec(yblock, lambda i: (d, 0, jnp.minimum(i, n_ctx_tiles - 1), 0, 0))

    def lat_spec(d):
        def imap(i):
            r = jnp.maximum(i - n_ctx_tiles, 0)
            return (d, lax.rem(r, tps), lax.div(r, tps), 0, 0)
        return pl.BlockSpec(yblock, imap)

    return pl.pallas_call(
        functools.partial(_merge_body, n_ctx_tiles=n_ctx_tiles),
        grid=(t // TM,),
        in_specs=(_dual_specs(TM, n_ctx_tiles)
                  + [pl.BlockSpec((1, N_MOD, D_MODEL), lambda i: (midx(i), 0, 0))]
                  + [_const_spec(w.shape) for w in wts]
                  + [tok(W_MIX), tok(W_MIX), tok(W_MIX), tok(2 * W_MIX),
                     ctx_spec(0), ctx_spec(1), lat_spec(0), lat_spec(1), tok(W_MIX), tok(W_MIX)]),
        out_specs=[tok(D_MODEL), tok(D_MODEL)],
        out_shape=[jax.ShapeDtypeStruct((t, D_MODEL), F32), jax.ShapeDtypeStruct((t, D_MODEL), BF16)],
        compiler_params=_cparams(("parallel",), VMEM_LIMIT),
        name="merge",
    )(x_ctx, x_lat, mod, *wts, ya, yd, gbg, h, y_ctx, y_ctx, y_lat, y_lat, g, bon)


_CAND_VALID = (8, 8, 8, 5, 4, 3, 2, 2, 2, 8)


def _route_head(qs, keys_ref):
    kio = lax.broadcasted_iota(I32, (N_KEYS, LANES), 0)
    sub = lax.broadcasted_iota(I32, (SUBLANES, LANES), 0)
    kid = lax.broadcasted_iota(I32, (PEER_TOPK, LANES), 0)
    neg = -jnp.inf

    def bc(x, r):
        return jnp.broadcast_to(x[r:r + 1, :], (SUBLANES, LANES))

    def head():
        tops = []
        for p in range(2):
            s = jnp.dot(keys_ref[p], qs[p], preferred_element_type=F32)
            vals = jnp.zeros((PEER_TOPK, LANES), F32)
            idxs = jnp.zeros((PEER_TOPK, LANES), I32)
            for r in range(PEER_TOPK):
                m = jnp.max(s, axis=0, keepdims=True)
                cand = jnp.where(s == m, kio, N_KEYS)
                ix = jnp.min(cand, axis=0, keepdims=True)
                s = jnp.where(cand == ix, neg, s)
                vals = jnp.where(kid == r, m, vals)
                idxs = jnp.where(kid == r, ix, idxs)
            tops.append((vals, idxs))
        (a0, i0), (a1, i1) = tops
        lo, hi = slice(0, SUBLANES), slice(SUBLANES, 2 * SUBLANES)
        slabs = [bc(a0, 0) + a1[lo], bc(a0, 0) + a1[hi]]
        ci = [bc(i0, 0), bc(i0, 0)]
        cj = [i1[lo], i1[hi]]
        for r in range(1, SUBLANES):
            slabs.append(bc(a0, r) + a1[lo])
            ci.append(bc(i0, r))
            cj.append(i1[lo])
        slabs.append(a0[hi] + bc(a1, 0))
        ci.append(i0[hi])
        cj.append(bc(i1, 0))
        slabs = [jnp.where(sub < nv, sl, neg) for sl, nv in zip(slabs, _CAND_VALID)]
        ids = [a * N_KEYS + b for a, b in zip(ci, cj)]
        vals = jnp.zeros((PEER_TOPK, LANES), F32)
        esel = jnp.zeros((PEER_TOPK, LANES), I32)
        for r in range(PEER_TOPK):
            level = list(zip(slabs, ids))
            while len(level) > 1:
                nxt = []
                for (va, ea), (vb, eb) in zip(level[0::2], level[1::2]):
                    take = vb > va
                    nxt.append((jnp.where(take, vb, va), jnp.where(take, eb, ea)))
                if len(level) % 2:
                    nxt.append(level[-1])
                level = nxt
            v8, e8 = level[0]
            for sh in (4, 2, 1):
                vr, er = pltpu.roll(v8, sh, 0), pltpu.roll(e8, sh, 0)
                take = vr > v8
                v8, e8 = jnp.where(take, vr, v8), jnp.where(take, er, e8)
            m, ex = v8[0:1, :], e8[0:1, :]
            slabs = [jnp.where(eid == ex, neg, sl) for sl, eid in zip(slabs, ids)]
            vals = jnp.where(kid == r, m, vals)
            esel = jnp.where(kid == r, ex, esel)
        e = jnp.exp(vals - vals[0:1, :])
        return esel, e / jnp.sum(e, axis=0, keepdims=True)

    return head()


def _peer_body(h2_ref, h2n_ref, wqt_ref, keys_ref, u_ref, v_ref, x1_ref, mod_ref, fng_ref,
               oc_ref, ol_ref,
               q_scr, e_scr, g_scr, et_scr, gt_scr, gs_scr, acc_scr,
               *, rows, pitch, units, n_ctx_tiles, final):
    m = pl.program_id(0)
    e = pl.program_id(1)
    tm = h2_ref.shape[0]
    n_chunks = tm // LANES
    nsel = PEER_HEADS * PEER_TOPK
    slot = lax.rem(m, 2)

    def project_queries(src_ref):
        q = lax.dot_general(wqt_ref[...], src_ref[...], _NT, preferred_element_type=F32).astype(BF16)
        for c in range(n_chunks):
            q_scr[c] = q[:, c * LANES:(c + 1) * LANES]

    def route_unit(u, dst):
        c = u // PEER_HEADS
        h = lax.rem(u, PEER_HEADS)
        qs = [q_scr[c, pl.ds(pl.multiple_of(h * (2 * N_KEYS) + p * N_KEYS, N_KEYS), N_KEYS), :]
              for p in range(2)]
        esel, gates = _route_head(qs, keys_ref)
        row = pl.multiple_of(h * PEER_TOPK, PEER_TOPK)
        e_scr[dst, c, pl.ds(row, PEER_TOPK), :] = esel
        g_scr[dst, c, pl.ds(row, PEER_TOPK), :] = gates

    @pl.when(jnp.logical_and(e == 0, m == 0))
    def _first_tile_routing():
        project_queries(h2_ref)

        def unit(u, c):
            route_unit(u, 0)
            return c

        lax.fori_loop(0, n_chunks * PEER_HEADS, unit, 0)

    @pl.when(e == 0)
    def _build():
        for c in range(n_chunks):
            et_scr[c * LANES:(c + 1) * LANES, :] = e_scr[slot, c].T
            gt_scr[c * LANES:(c + 1) * LANES, :] = g_scr[slot, c].T
        kio = lax.broadcasted_iota(I32, (N_KEYS, nsel), 0)

        def tok(t, c):
            erow = et_scr[pl.ds(t, 1), :]
            grow = gt_scr[pl.ds(t, 1), :]
            at = jnp.where(kio == (erow >> 7), grow, 0.0).astype(BF16)
            bt = jnp.where(kio == (erow & (N_KEYS - 1)), 1.0, 0.0).astype(BF16)
            gt = lax.dot_general(at, bt, _NT, preferred_element_type=F32)
            hi = pltpu.bitcast(gt[0:rows, :], jnp.uint32) & jnp.uint32(0xFFFF0000)
            lo = pltpu.bitcast(gt[rows:2 * rows, :], jnp.uint32) >> 16
            gs_scr[pl.ds(pl.multiple_of(t * pitch, SUBLANES), rows), :] = hi | lo
            return c

        lax.fori_loop(0, tm, tok, 0, unroll=16)
        acc_scr[...] = jnp.zeros_like(acc_scr)
        project_queries(h2n_ref)

    for k in range(units):
        route_unit(e * units + k, 1 - slot)

    per_sub = PEER_SUB // N_KEYS
    per_step = u_ref.shape[0] // N_KEYS
    steps_per_half = rows // per_step
    row0 = lax.rem(e, steps_per_half) * per_step
    shift = jnp.where(e < steps_per_half, 0, 16).astype(jnp.uint32)
    h2 = h2_ref[...]
    total = None
    for sb in range(per_step // per_sub):
        es = slice(sb * PEER_SUB, (sb + 1) * PEER_SUB)
        hmat = lax.dot_general(h2, u_ref[es, :], _NT, preferred_element_type=F32)
        words = jnp.concatenate(
            [gs_scr[pl.ds(row0 + sb * per_sub + ii, tm, stride=pitch), :] for ii in range(per_sub)], axis=1)
        gm = pltpu.bitcast((words << shift) & jnp.uint32(0xFFFF0000), F32)
        act = jax.nn.gelu(hmat.astype(BF16)) * gm.astype(BF16)
        part = jnp.dot(act, v_ref[es, :], preferred_element_type=F32)
        total = part if total is None else total + part
    acc_scr[...] += total

    def result():
        x2 = x1_ref[...] + mod_ref[0][5:6, :] * acc_scr[...]
        return _rms(x2, fng_ref[...]) if final else x2

    last = e == pl.num_programs(1) - 1

    @pl.when(jnp.logical_and(last, m < n_ctx_tiles))
    def _out_ctx():
        oc_ref[...] = result()

    @pl.when(jnp.logical_and(last, m >= n_ctx_tiles))
    def _out_lat():
        ol_ref[...] = result()


def _peer(h2, wqt, keys, u, v, layer, x1, mod, fng, n_ctx_tok, lat_len, final):
    t = h2.shape[0]
    nsel = PEER_HEADS * PEER_TOPK
    rows = N_KEYS // 2
    pitch = rows + SUBLANES
    n_e = (N_KEYS * N_KEYS) // PEER_EB
    n_m = t // TM_PEER
    n_chunks = TM_PEER // LANES
    units = (n_chunks * PEER_HEADS) // n_e
    assert units * n_e == n_chunks * PEER_HEADS
    midx = functools.partial(_mod_index, tm=TM_PEER, n_ctx_tok=n_ctx_tok, lat_len=lat_len)
    tok = lambda wd: pl.BlockSpec((TM_PEER, wd), lambda m, e: (m, 0))
    nxt = pl.BlockSpec((TM_PEER, D_MODEL), lambda m, e: (jnp.minimum(m + 1, n_m - 1), 0),
                       pipeline_mode=pl.Buffered(1))
    espec = pl.BlockSpec((None, PEER_EB, D_MODEL), lambda m, e: (layer, e, 0))
    single = dict(pipeline_mode=pl.Buffered(1))
    return pl.pallas_call(
        functools.partial(_peer_body, rows=rows, pitch=pitch, units=units,
                          n_ctx_tiles=n_ctx_tok // TM_PEER, final=final),
        grid=(n_m, n_e),
        in_specs=[tok(D_MODEL), nxt,
                  pl.BlockSpec(wqt.shape, lambda m, e: (0, 0), **single),
                  pl.BlockSpec(keys.shape, lambda m, e: (0, 0, 0), **single),
                  espec, espec,
                  pl.BlockSpec((TM_PEER, D_MODEL), lambda m, e: (m, 0), **single),
                  pl.BlockSpec((1, N_MOD, D_MODEL), lambda m, e: (midx(m), 0, 0)),
                  _const_spec((1, D_MODEL))],
        out_specs=_dual_specs(TM_PEER, n_ctx_tok // TM_PEER),
        out_shape=[jax.ShapeDtypeStruct((n_ctx_tok, D_MODEL), F32),
                   jax.ShapeDtypeStruct((t - n_ctx_tok, D_MODEL), F32)],
        scratch_shapes=[pltpu.VMEM((n_chunks, wqt.shape[0], LANES), BF16),
                        pltpu.VMEM((2, n_chunks, nsel, LANES), I32),
                        pltpu.VMEM((2, n_chunks, nsel, LANES), F32),
                        pltpu.VMEM((TM_PEER, nsel), I32),
                        pltpu.VMEM((TM_PEER, nsel), F32),
                        pltpu.VMEM((TM_PEER * pitch, N_KEYS), jnp.uint32),
                        pltpu.VMEM((TM_PEER, D_MODEL), F32)],
        compiler_params=_cparams(("arbitrary", "arbitrary"), VMEM_LIMIT),
        name="peer",
    )(h2, h2, wqt, keys, u, v, x1, mod, fng)


def _wkv_state_in(s, vs):
    n = s.shape[0]
    vl = HS_WKV // vs
    s = s.reshape(n, N_DIR, H_WKV, vs, vl, HS_WKV).transpose(1, 4, 5, 3, 0, 2)
    return s.reshape(N_DIR, vl, HS_WKV, vs * n * H_WKV)


def _wkv_state_out(s, n_sb, spb, vs):
    vl = HS_WKV // vs
    s = s.reshape(N_DIR, n_sb, vl, HS_WKV, vs, spb, H_WKV).transpose(1, 5, 0, 6, 4, 2, 3)
    return s.reshape(n_sb * spb, N_DIR, H_WKV, HS_WKV, HS_WKV)


def _layer_weights(i, prm):
    eye_h = jnp.eye(H_LRU, dtype=F32)
    eye_d = jnp.eye(N_DIR, dtype=F32)
    perm = _WKV_PERM

    def lru_bd(wt):
        return jnp.einsum("dhij,hg->hidgj", wt, eye_h).reshape(W_MIX, N_DIR * W_MIX)

    def lora_bd(wt):
        r = wt.shape[1]
        return jnp.einsum("drc,de->drec", wt, eye_d).reshape(N_DIR * r, N_DIR * W_MIX)

    w_in = prm["w_in"][i]
    pad = jnp.zeros((D_MODEL, Z_COLS - 5504), F32)
    rkv = [w_in[:, 2560 + j * W_MIX:2560 + (j + 1) * W_MIX][:, perm] for j in range(3)]
    w_in_perm = jnp.concatenate(
        [w_in[:, 0:1536]] + rkv + [w_in[:, 1536:2560], w_in[:, 4480:5504], w_in[:, 4096:4480], pad],
        axis=1).astype(BF16)
    row = lambda x: x.reshape(1, -1).astype(F32)
    head_of = np.arange(W_MIX) % H_WKV
    w_branch = prm["w_branch"][i]
    w_branch = jnp.concatenate([w_branch[0], w_branch[1], w_branch[2][perm, :], w_branch[3]], axis=0)
    return {
        "w_in": w_in_perm,
        "w_gate": w_in[:, 5504:].astype(BF16),
        "norm1_g": row(prm["norm1_g"][i]),
        "norm2_g": row(prm["norm2_g"][i]),
        "conv_a_w": prm["conv_a_w"][i],
        "conv_b_w": prm["conv_b_w"][i],
        "conv_b_b": row(prm["conv_b_b"][i]),
        "lru_w": jnp.concatenate([lru_bd(prm["lru_wa"][i]), lru_bd(prm["lru_wx"][i])], axis=1).astype(BF16),
        "lru_b": jnp.concatenate([row(prm["lru_ba"][i]), row(prm["lru_bx"][i])], axis=1),
        "lru_lam": row(prm["lru_lambda"][i]),
        "w0": row(prm["rwkv_w0"][i][:, perm]),
        "w2": lora_bd(prm["rwkv_w2"][i][:, :, perm]).astype(BF16),
        "a0": row(prm["rwkv_a0"][i][:, perm]),
        "a2": lora_bd(prm["rwkv_a2"][i][:, :, perm]).astype(BF16),
        "g2": prm["rwkv_g2"][i][:, perm].astype(BF16),
        "kk": row(prm["rwkv_kk"][i][perm]),
        "ka": row(prm["rwkv_ka"][i][perm]),
        "rk": row(prm["rwkv_rk"][i].reshape(W_MIX)[perm]),
        "lnx_g": row(prm["lnx_g"][i][perm]),
        "lnx_b": row(prm["lnx_b"][i][perm]),
        "sg_ln_g": row(prm["sg_ln_g"][i]),
        "sg_ln_b": row(prm["sg_ln_b"][i]),
        "sg_ws": prm["sg_ws"][i].astype(BF16),
        "sg_bst": prm["sg_bs"][i].T,
        "gate_b": row(prm["gate_b"][i]),
        "w_branch": w_branch.astype(BF16),
        "w_out": prm["w_out"][i].astype(BF16),
        "wq_t": prm["peer_wq"][i].T.astype(BF16),
        "keys": prm["peer_keys"][i].astype(BF16),
        "ind": jnp.asarray(head_of[:, None] == head_of[None, :], BF16),
    }


def kernel(x_prompt, x_sample, state_lru, state_wkv, c, c_ctx, norm1_g, norm2_g, w_mod, b_mod, w_in, conv_a_w, conv_b_w, conv_b_b, lru_wa, lru_ba, lru_wx, lru_bx, lru_lambda, rwkv_w0, rwkv_w2, rwkv_a0, rwkv_a2, rwkv_g2, rwkv_kk, rwkv_ka, rwkv_rk, lnx_g, lnx_b, sg_ln_g, sg_ln_b, sg_ws, sg_bs, gate_b, w_branch, w_out, peer_wq, peer_keys, peer_u, peer_v, final_norm_g):
    prm = dict(norm1_g=norm1_g, norm2_g=norm2_g, w_in=w_in, conv_a_w=conv_a_w, conv_b_w=conv_b_w,
               conv_b_b=conv_b_b, lru_wa=lru_wa, lru_ba=lru_ba, lru_wx=lru_wx, lru_bx=lru_bx,
               lru_lambda=lru_lambda, rwkv_w0=rwkv_w0, rwkv_w2=rwkv_w2, rwkv_a0=rwkv_a0, rwkv_a2=rwkv_a2,
               rwkv_g2=rwkv_g2, rwkv_kk=rwkv_kk, rwkv_ka=rwkv_ka, rwkv_rk=rwkv_rk, lnx_g=lnx_g,
               lnx_b=lnx_b, sg_ln_g=sg_ln_g, sg_ln_b=sg_ln_b, sg_ws=sg_ws, sg_bs=sg_bs, gate_b=gate_b,
               w_branch=w_branch, w_out=w_out, peer_wq=peer_wq, peer_keys=peer_keys, peer_u=peer_u,
               peer_v=peer_v)
    bc, lc, _ = x_prompt.shape
    bl, ll, _ = x_sample.shape
    depth = w_mod.shape[0]
    n_ctx_tok = bc * lc
    n_ctx_tiles = n_ctx_tok // TM
    lat_tiles = ll // TM
    ctx_spb = min(WKV_CTX_SPB, bc)
    lru_spb = ll // lc
    assert lc == TM and ll % TM_PEER == 0 and n_ctx_tok % TM_PEER == 0 and bl + 1 <= SUBLANES
    assert ll % GRID_W == 0 and TM % GRID_W == 0 and bc % ctx_spb == 0 and n_ctx_tok % ll == 0
    assert LANES % (ctx_spb * H_WKV) == 0 and LANES % (bl * H_WKV) == 0

    cond = jnp.zeros((SUBLANES, D_MODEL), F32).at[0].set(c_ctx).at[1:1 + bl].set(c)
    mods = _modulation(cond, w_mod, b_mod).reshape(depth, SUBLANES, N_MOD, D_MODEL)
    fng = final_norm_g.reshape(1, D_MODEL)
    x_ctx = x_prompt.reshape(n_ctx_tok, D_MODEL)
    x_lat = x_sample.reshape(bl * ll, D_MODEL)
    ctx_vs = LANES // (ctx_spb * H_WKV)
    lat_vs = LANES // (bl * H_WKV)
    n_sb = bc // ctx_spb
    wkv_zero = jnp.zeros((N_DIR * n_sb, HS_WKV // ctx_vs, HS_WKV, LANES), F32)
    lru_zero = jnp.zeros((n_ctx_tok // ll, lru_spb, N_DIR * W_MIX), F32)
    u_all = peer_u.astype(BF16)
    v_all = peer_v.astype(BF16)
    new_lru, new_wkv = [], []
    pnames = ["ya", "yd", "gbg", "la", "lu", "g", "bon", "rt", "vt", "kkt", "wt", "kt", "bt"]
    for i in range(depth):
        lw = _layer_weights(i, prm)
        mod = mods[i]
        p = dict(zip(pnames, _prep(x_ctx, x_lat, mod, lw, ll)))

        lat_h0 = jnp.zeros((bl, lru_spb, N_DIR * W_MIX), F32).at[:, 0].set(
            state_lru[:, i].astype(F32).reshape(bl, N_DIR * W_MIX))
        h, lru_s = _lru_scan(p["la"], p["lu"], jnp.concatenate([lru_zero, lat_h0], axis=0),
                             rows=ll, n_ctx_blocks=n_ctx_tok // ll, ctx_cfg=(lru_spb, lc), lat_cfg=(1, ll))
        new_lru.append(lru_s[:n_ctx_tok // ll].reshape(bc, N_DIR, W_MIX))

        wkv_in = [p[n] for n in ("rt", "wt", "kt", "kkt", "bt", "vt")]
        y_c, s_c = _wkv_scan(*wkv_in, wkv_zero, tile0=0, n_seq=bc, seq_tiles=1, spb=ctx_spb)
        y_l, _ = _wkv_scan(*wkv_in, _wkv_state_in(state_wkv[:, i].astype(F32), lat_vs),
                           tile0=n_ctx_tiles, n_seq=bl, seq_tiles=lat_tiles, spb=1)
        new_wkv.append(_wkv_state_out(s_c, n_sb, ctx_spb, ctx_vs))

        tok_in = [p["ya"], p["yd"], p["gbg"], h, p["g"], p["bon"]]
        x1, h2 = _merge(x_ctx, x_lat, mod, lw, tok_in, y_c, y_l, ll)
        x_ctx, x_lat = _peer(h2, lw["wq_t"], lw["keys"], u_all, v_all, i, x1, mod, fng,
                             n_ctx_tok, ll, final=(i == depth - 1))
    y_prompt = x_ctx.reshape(bc, lc, D_MODEL)
    y_sample = x_lat.reshape(bl, ll, D_MODEL)
    return (y_prompt, y_sample, jnp.stack(new_lru, axis=1), jnp.stack(new_wkv, axis=1))
```

```python
import functools

import numpy as np
import jax
import jax.numpy as jnp
from jax import lax
from jax.experimental import pallas as pl
from jax.experimental.pallas import tpu as pltpu

F32 = jnp.float32
BF16 = jnp.bfloat16
I32 = jnp.int32

D_MODEL = 1024
W_MIX = 512
N_DIR = 2
N_BRANCH = 4
H_WKV = 8
HS_WKV = 64
H_LRU = 8
HB_LRU = 64
LORA_W = 64
LORA_A = 64
LORA_G = 128
GRID_W = 64
CHUNK = 128
G_SG = 4
N_KEYS = 128
PEER_HEADS = 8
PEER_TOPK = 16
N_MOD = 6
EPS = 1e-6
LNX_EPS = 64e-5
LRU_C = 8.0

LANES = 128
SUBLANES = 8
TM = 256
TM_PEER = 512
PEER_EB = 1024
PEER_SUB = 512
Z_COLS = 5632
WKV_TC = LANES
WKV_CTX_SPB = 8
VMEM_LIMIT = 56 * 1024 * 1024

_NT = (((1,), (1,)), ((), ()))
_WKV_PERM = np.array([(n % H_WKV) * HS_WKV + n // H_WKV for n in range(W_MIX)])


def _cparams(sem, vmem=None):
    return pltpu.CompilerParams(dimension_semantics=sem, vmem_limit_bytes=vmem)


def _const_spec(shape):
    nd = len(shape)
    return pl.BlockSpec(shape, lambda *_: (0,) * nd)


def _softplus(x):
    return jnp.maximum(x, 0.0) + jnp.log1p(jnp.exp(-jnp.abs(x)))


def _rms(x, g):
    return x * lax.rsqrt(jnp.mean(x * x, axis=-1, keepdims=True) + EPS) * g


def _segsum(x, ind):
    hi = x.astype(BF16)
    lo = (x - hi.astype(F32)).astype(BF16)
    return (jnp.dot(hi, ind, preferred_element_type=F32)
            + jnp.dot(lo, ind, preferred_element_type=F32))


def _mod_index(i, tm, n_ctx_tok, lat_len):
    n_ctx_tiles = n_ctx_tok // tm
    tiles_per_seq = lat_len // tm
    return jnp.where(i < n_ctx_tiles, 0, 1 + lax.div(i - n_ctx_tiles, tiles_per_seq))


def _mod_body(s_ref, w_ref, b_ref, o_ref):
    s = s_ref[...]
    s = s * jax.nn.sigmoid(s)
    o_ref[0] = jnp.dot(s.astype(BF16), w_ref[0].astype(BF16), preferred_element_type=F32) + b_ref[0]


def _modulation(cond, w_mod, b_mod):
    depth = w_mod.shape[0]
    n = w_mod.shape[2]
    tn = 1536
    return pl.pallas_call(
        _mod_body,
        grid=(depth, n // tn),
        in_specs=[_const_spec((SUBLANES, D_MODEL)),
                  pl.BlockSpec((1, D_MODEL, tn), lambda l, j: (l, 0, j)),
                  pl.BlockSpec((1, 1, tn), lambda l, j: (l, 0, j))],
        out_specs=pl.BlockSpec((1, SUBLANES, tn), lambda l, j: (l, 0, j)),
        out_shape=jax.ShapeDtypeStruct((depth, SUBLANES, n), F32),
        compiler_params=_cparams(("parallel", "parallel"), VMEM_LIMIT),
        name="modulation",
    )(cond, w_mod, b_mod.reshape(depth, 1, n))


def _prep_body(xc_ref, xl_ref, xp_ref, xn_ref, mod_ref, n1g_ref, win_ref,
               caw_ref, cbw_ref, cbb_ref, lruw_ref, lrub_ref, lam_ref,
               w0_ref, w2_ref, a0_ref, a2_ref, g2_ref, kkw_ref, ka_ref, rk_ref,
               lng_ref, lnb_ref, ws_ref, bst_ref, ind_ref,
               ya_ref, yd_ref, gbg_ref, la_ref, lu_ref, g_ref, bon_ref,
               rt_ref, vt_ref, kkt_ref, wt_ref, kt_ref, bt_ref,
               *, n_ctx_tiles, tiles_per_seq):
    i = pl.program_id(0)
    is_ctx = i < n_ctx_tiles
    t = lax.broadcasted_iota(I32, (TM, 1), 0)
    ind = ind_ref[...]
    m = mod_ref[0]

    def modulated(xv):
        return (_rms(xv, n1g_ref[...]) * (1.0 + m[1:2, :]) + m[0:1, :]).astype(BF16)

    def project(hv, lo, hi):
        return jnp.dot(hv, win_ref[:, lo:hi], preferred_element_type=F32)

    h = modulated(jnp.where(is_ctx, xc_ref[...], xl_ref[...]))
    za = project(h, 0, 1536)
    zc = project(h, 1536, 3072)
    zb = project(h, 3072, 4096)
    zd = project(h, 4096, 5120)
    zl = project(h, 5120, 5120 + 2 * LORA_W + 2 * LORA_A + LORA_G)
    halo_prev = project(modulated(xp_ref[...]), 3072 + W_MIX, 4096)
    halo_next = project(modulated(xn_ref[...]), 3072 + W_MIX, 4096)

    pm = jnp.where(is_ctx, TM - 1, GRID_W - 1)
    pos = t & pm
    a_b = za[:, 0:W_MIX]
    ac = za[:, W_MIX:2 * W_MIX] * za[:, 2 * W_MIX:3 * W_MIX]
    up = jnp.where(pos == 0, 0.0, pltpu.roll(ac, 1, 0))
    dn = jnp.where(pos == pm, 0.0, pltpu.roll(ac, TM - 1, 0))
    ya_ref[...] = a_b * (caw_ref[0:1, :] * up + caw_ref[1:2, :] * ac + caw_ref[2:3, :] * dn)

    seq_tile = lax.rem(jnp.maximum(i - n_ctx_tiles, 0), tiles_per_seq)
    first = jnp.logical_or(is_ctx, seq_tile == 0)
    last = jnp.logical_or(is_ctx, seq_tile == tiles_per_seq - 1)
    prev = jnp.where(first, 0.0, halo_prev[SUBLANES - 1:SUBLANES, :])
    nxt0 = jnp.where(last, 0.0, halo_next[0:1, :])
    nxt1 = jnp.where(last, 0.0, halo_next[1:2, :])
    bx = zb[:, W_MIX:2 * W_MIX]
    m1 = jnp.where(t == 0, prev, pltpu.roll(bx, 1, 0))
    p1 = jnp.where(t == TM - 1, nxt0, pltpu.roll(bx, TM - 1, 0))
    p2 = jnp.where(t == TM - 2, nxt0, jnp.where(t == TM - 1, nxt1, pltpu.roll(bx, TM - 2, 0)))
    xb = (cbw_ref[0:1, :] * m1 + cbw_ref[1:2, :] * bx + cbw_ref[2:3, :] * p1
          + cbw_ref[3:4, :] * p2 + cbb_ref[...])
    gates = jnp.dot(xb.astype(BF16), lruw_ref[...], preferred_element_type=F32) + lrub_ref[...]
    rg = jax.nn.sigmoid(gates[:, 0:2 * W_MIX])
    ig = jax.nn.sigmoid(gates[:, 2 * W_MIX:4 * W_MIX])
    log_a = -LRU_C * rg * _softplus(-lam_ref[...])
    xb2 = jnp.concatenate([xb, xb], axis=1)
    la_ref[...] = jnp.exp(log_a)
    lu_ref[...] = jnp.sqrt(jnp.tanh(-log_a) * (jnp.exp(2.0 * log_a) + 1.0)) * (ig * xb2)
    gbg_ref[...] = jax.nn.gelu(zb[:, 0:W_MIX])

    zr = zc[:, 0:W_MIX]
    zk = zc[:, W_MIX:2 * W_MIX]
    zv = zc[:, 2 * W_MIX:3 * W_MIX]
    zwd = zl[:, 0:2 * LORA_W]
    zad = zl[:, 2 * LORA_W:2 * LORA_W + 2 * LORA_A]
    zgd = zl[:, 2 * LORA_W + 2 * LORA_A:2 * LORA_W + 2 * LORA_A + LORA_G]
    wlin = w0_ref[...] + jnp.dot(jnp.tanh(zwd).astype(BF16), w2_ref[...], preferred_element_type=F32)
    wt_ref[0] = jnp.exp(-jnp.exp(-_softplus(-wlin) - 0.5)).T
    av = jax.nn.sigmoid(a0_ref[...] + jnp.dot(zad.astype(BF16), a2_ref[...], preferred_element_type=F32))
    g_ref[...] = jnp.dot(jax.nn.sigmoid(zgd).astype(BF16), g2_ref[...], preferred_element_type=F32)
    kkr = zk * kkw_ref[...]
    kkn = kkr / jnp.maximum(jnp.sqrt(_segsum(kkr * kkr, ind)), 1e-12)
    zk2 = jnp.concatenate([zk, zk], axis=1)
    ka2 = jnp.concatenate([ka_ref[...], ka_ref[...]], axis=1)
    kd = zk2 * (1.0 + (av - 1.0) * ka2)
    kt_ref[0] = kd.T
    bt_ref[0] = (jnp.concatenate([kkn, kkn], axis=1) * av).T
    rt_ref[0] = zr.T
    vt_ref[0] = zv.T
    kkt_ref[0] = kkn.T
    bon_ref[...] = _segsum(zr * (kd[:, 0:W_MIX] + kd[:, W_MIX:2 * W_MIX]) * rk_ref[...], ind) * zv

    zg = jax.nn.gelu(zd)
    u = zg[:, 0:W_MIX]
    vv = zg[:, W_MIX:2 * W_MIX]
    vc = vv - jnp.mean(vv, axis=-1, keepdims=True)
    vn = vc * lax.rsqrt(jnp.mean(vc * vc, axis=-1, keepdims=True) + 1e-5) * lng_ref[...] + lnb_ref[...]
    for c in range(TM // CHUNK):
        rs = slice(c * CHUNK, (c + 1) * CHUNK)
        for gi in range(G_SG):
            cs = slice(gi * LANES, (gi + 1) * LANES)
            s = jnp.dot(ws_ref[gi], vn[rs, cs].astype(BF16), preferred_element_type=F32)
            yd_ref[rs, cs] = u[rs, cs] * (s + bst_ref[:, gi:gi + 1])


def _dual_specs(rows, n_ctx_blocks, **kw):
    return [pl.BlockSpec((rows, D_MODEL), lambda i, *_: (jnp.minimum(i, n_ctx_blocks - 1), 0), **kw),
            pl.BlockSpec((rows, D_MODEL), lambda i, *_: (jnp.maximum(i - n_ctx_blocks, 0), 0), **kw)]


def _prep(x_ctx, x_lat, mod, lw, lat_len):
    n_ctx_tok = x_ctx.shape[0]
    t = n_ctx_tok + x_lat.shape[0]
    n_tiles = t // TM
    n_ctx_tiles = n_ctx_tok // TM
    tiles_per_seq = lat_len // TM
    rows8 = TM // SUBLANES
    last_blk = x_lat.shape[0] // SUBLANES - 1
    midx = functools.partial(_mod_index, tm=TM, n_ctx_tok=n_ctx_tok, lat_len=lat_len)

    def lat_blk8(i, off):
        return (jnp.clip((i - n_ctx_tiles) * rows8 + off, 0, last_blk), 0)

    x_specs = _dual_specs(TM, n_ctx_tiles) + [
        pl.BlockSpec((SUBLANES, D_MODEL), lambda i: lat_blk8(i, -1)),
        pl.BlockSpec((SUBLANES, D_MODEL), lambda i: lat_blk8(i, rows8)),
        pl.BlockSpec((1, N_MOD, D_MODEL), lambda i: (midx(i), 0, 0)),
        _const_spec((1, D_MODEL)),
        pl.BlockSpec(lw["w_in"].shape, lambda i: (0, 0), pipeline_mode=pl.Buffered(1)),
    ]
    wnames = ["conv_a_w", "conv_b_w", "conv_b_b", "lru_w", "lru_b", "lru_lam", "w0", "w2", "a0", "a2",
              "g2", "kk", "ka", "rk", "sg_ln_g", "sg_ln_b", "sg_ws", "sg_bst", "ind"]
    wts = [lw[n] for n in wnames]
    w_specs = [_const_spec(w.shape) for w in wts]
    widths = [W_MIX, W_MIX, W_MIX, 2 * W_MIX, 2 * W_MIX, W_MIX, W_MIX]
    t_rows = [W_MIX, W_MIX, W_MIX, 2 * W_MIX, 2 * W_MIX, 2 * W_MIX]
    out_specs = ([pl.BlockSpec((TM, wd), lambda i: (i, 0)) for wd in widths]
                 + [pl.BlockSpec((1, r, TM), lambda i: (i, 0, 0)) for r in t_rows])
    out_shape = ([jax.ShapeDtypeStruct((t, wd), F32) for wd in widths]
                 + [jax.ShapeDtypeStruct((n_tiles, r, TM), F32) for r in t_rows])
    return pl.pallas_call(
        functools.partial(_prep_body, n_ctx_tiles=n_ctx_tiles, tiles_per_seq=tiles_per_seq),
        grid=(n_tiles,),
        in_specs=x_specs + w_specs,
        out_specs=out_specs,
        out_shape=out_shape,
        compiler_params=_cparams(("parallel",), VMEM_LIMIT),
        name="branch_prep",
    )(x_ctx, x_lat, x_lat, x_lat, mod, lw["norm1_g"], lw["w_in"], *wts)


def _lru_body(a_ref, u_ref, h0_ref, h_ref, hf_ref, *, n_ctx_blocks, ctx_cfg, lat_cfg):
    fw, bw = slice(0, W_MIX), slice(W_MIX, 2 * W_MIX)

    def scan(nseq, l):
        def step(s, carry):
            out = []
            for j in range(nseq):
                tf = j * l + s
                tb = j * l + (l - 1 - s)
                hf = a_ref[pl.ds(tf, 1), fw] * carry[2 * j] + u_ref[pl.ds(tf, 1), fw]
                hb = a_ref[pl.ds(tb, 1), bw] * carry[2 * j + 1] + u_ref[pl.ds(tb, 1), bw]
                h_ref[pl.ds(tf, 1), fw] = hf
                h_ref[pl.ds(tb, 1), bw] = hb
                out += [hf, hb]
            return tuple(out)

        init = []
        for j in range(nseq):
            init += [h0_ref[0, j:j + 1, fw], h0_ref[0, j:j + 1, bw]]
        fin = lax.fori_loop(0, l, step, tuple(init), unroll=2)
        hf_ref[0] = h0_ref[0]
        for j in range(nseq):
            hf_ref[0, j:j + 1, fw] = fin[2 * j]
            hf_ref[0, j:j + 1, bw] = fin[2 * j + 1]

    is_ctx = pl.program_id(0) < n_ctx_blocks
    pl.when(is_ctx)(lambda: scan(*ctx_cfg))
    pl.when(jnp.logical_not(is_ctx))(lambda: scan(*lat_cfg))


def _lru_scan(a, u, h0, *, rows, n_ctx_blocks, ctx_cfg, lat_cfg):
    nb = a.shape[0] // rows
    w = a.shape[1]
    tok = pl.BlockSpec((rows, w), lambda i: (i, 0))
    st = pl.BlockSpec((1,) + h0.shape[1:], lambda i: (i, 0, 0))
    return pl.pallas_call(
        functools.partial(_lru_body, n_ctx_blocks=n_ctx_blocks, ctx_cfg=ctx_cfg, lat_cfg=lat_cfg),
        grid=(nb,),
        in_specs=[tok, tok, st],
        out_specs=[tok, st],
        out_shape=[jax.ShapeDtypeStruct(a.shape, F32), jax.ShapeDtypeStruct(h0.shape, F32)],
        compiler_params=_cparams(("parallel",), VMEM_LIMIT),
        name="lru_scan",
    )(a, u, h0)


_SLOT_ORDER = (0, 4, 2, 6, 1, 5, 3, 7)


def _rowsum8(parts):
    sub = lax.broadcasted_iota(I32, (SUBLANES, LANES), 0)
    slots = [parts[i] for i in _SLOT_ORDER]
    roll = pltpu.roll
    lvl1 = [jnp.where(sub < 4, a + roll(a, 4, 0), b + roll(b, 4, 0))
            for a, b in zip(slots[0::2], slots[1::2])]
    lvl2 = [jnp.where((sub & 3) < 2, a + roll(a, 6, 0), roll(b + roll(b, 6, 0), 2, 0))
            for a, b in zip(lvl1[0::2], lvl1[1::2])]
    a, b = lvl2
    return jnp.where((sub & 1) == 0, a + roll(a, 7, 0), roll(b + roll(b, 7, 0), 1, 0))


def _fold8(x):
    return jnp.sum(x.reshape(HS_WKV // SUBLANES, SUBLANES, LANES), axis=0)


def _wkv_body(*refs, nsrc, spb, vs, n_sb, tc, kp, vp):
    vl_n = HS_WKV // vs
    n_in = 6 * nsrc
    k_srcs = [refs[o * nsrc:(o + 1) * nsrc] for o in range(5)]
    v_srcs = refs[5 * nsrc:n_in]
    s0_ref = refs[n_in]
    y_ref = refs[n_in + 1]
    sf_ref = refs[n_in + 2]
    k_scr = refs[n_in + 3:n_in + 8]
    v_scr, y_scr, s_scr, sa_scr = refs[n_in + 8:n_in + 12]
    r_scr, w_scr, k_scr_, kk_scr, b_scr = k_scr
    backward = pl.program_id(0) // n_sb == 1
    seqs = [(s, j) for s in range(nsrc) for j in range(spb)]

    @pl.when(pl.program_id(1) == 0)
    def _():
        s_scr[...] = s0_ref[0]

    def build_k(c, carry):
        row = pl.multiple_of(c * H_WKV, H_WKV)
        for o in range(5):
            slab = [k_srcs[o][s][j, pl.ds(row, H_WKV), :] for s, j in seqs]
            k_scr[o][pl.ds(c, tc, stride=kp), :] = jnp.concatenate(slab * vs, axis=0).T
        return carry

    lax.fori_loop(0, HS_WKV, build_k, 0, unroll=4)

    def build_v(vl, carry):
        slab = []
        for vsi in range(vs):
            row = pl.multiple_of((vsi * vl_n + vl) * H_WKV, H_WKV)
            slab += [v_srcs[s][j, pl.ds(row, H_WKV), :] for s, j in seqs]
        v_scr[pl.ds(vl, tc, stride=vp), :] = jnp.concatenate(slab, axis=0).T
        return carry

    lax.fori_loop(0, vl_n, build_v, 0, unroll=4)

    def step(s, carry):
        t = jnp.where(backward, tc - 1 - s, s)
        krow = pl.multiple_of(t * kp, SUBLANES)
        vrow = pl.multiple_of(t * vp, SUBLANES)
        kslab = pl.ds(krow, HS_WKV)
        for g in range(vl_n // SUBLANES):
            parts = [_fold8(s_scr[g * SUBLANES + i] * kk_scr[kslab, :]) for i in range(SUBLANES)]
            sa_scr[g * SUBLANES:(g + 1) * SUBLANES, :] = _rowsum8(parts)
        for g in range(vl_n // SUBLANES):
            parts = []
            for i in range(SUBLANES):
                vl = g * SUBLANES + i
                sa = sa_scr[vl:vl + 1, :]
                vv = v_scr[pl.ds(vrow + vl, 1), :]
                sn = s_scr[vl] * w_scr[kslab, :] - sa * b_scr[kslab, :] + vv * k_scr_[kslab, :]
                s_scr[vl] = sn
                parts.append(_fold8(sn * r_scr[kslab, :]))
            y_scr[pl.ds(pl.multiple_of(vrow + g * SUBLANES, SUBLANES), SUBLANES), :] = _rowsum8(parts)
        return carry

    lax.fori_loop(0, tc, step, 0)

    def emit_y(vl, carry):
        yt = y_scr[pl.ds(vl, tc, stride=vp), :].T
        for vsi in range(vs):
            row = pl.multiple_of((vsi * vl_n + vl) * H_WKV, H_WKV)
            for n, (s, j) in enumerate(seqs):
                lane0 = (vsi * len(seqs) + n) * H_WKV
                y_ref[0, 0, s * spb + j, pl.ds(row, H_WKV), :] = yt[lane0:lane0 + H_WKV, :]
        return carry

    lax.fori_loop(0, vl_n, emit_y, 0, unroll=4)
    sf_ref[0] = s_scr[...]


def _wkv_scan(rt, wt, kt, kkt, bt, vt, s0, *, tile0, n_seq, seq_tiles, spb):
    tc = WKV_TC
    if spb > 1:
        assert seq_tiles == 1 and n_seq % spb == 0 and tile0 % spb == 0
        nsrc, n_sb = 1, n_seq // spb
    else:
        nsrc, n_sb = n_seq, 1
    inst = nsrc * spb * H_WKV
    vs = LANES // inst
    vl_n = HS_WKV // vs
    assert vl_n % SUBLANES == 0, "value rows are processed eight at a time"
    cpt = TM // tc
    n_chunks = seq_tiles * cpt
    kp = HS_WKV + SUBLANES
    vp = vl_n + SUBLANES if ((vl_n + SUBLANES) // SUBLANES) % 2 else vl_n + 2 * SUBLANES

    def chunk(g, i):
        return jnp.where(g // n_sb == 1, n_chunks - 1 - i, i)

    def in_map(g, i, *, src, per_dir):
        ce = chunk(g, i)
        rb = (g // n_sb) if per_dir else 0
        if spb > 1:
            return (tile0 // spb + g % n_sb, rb, ce)
        return (tile0 + src * seq_tiles + ce // cpt, rb, ce % cpt)

    def out_map(g, i):
        ce = chunk(g, i)
        if spb > 1:
            return (g // n_sb, 0, g % n_sb, 0, ce)
        return (g // n_sb, ce // cpt, 0, 0, ce % cpt)

    in_specs, operands = [], []
    for arr, per_dir in ((rt, False), (wt, True), (kt, True), (kkt, False), (bt, True), (vt, False)):
        for src in range(nsrc):
            in_specs.append(pl.BlockSpec((spb, W_MIX, tc), functools.partial(in_map, src=src, per_dir=per_dir),
                                         pipeline_mode=pl.Buffered(1)))
            operands.append(arr)
    sspec = pl.BlockSpec((1, vl_n, HS_WKV, LANES), lambda g, i: (g, 0, 0, 0))
    in_specs.append(sspec)
    out_specs = [pl.BlockSpec((1, 1, nsrc * spb, W_MIX, tc), out_map)]
    out_shape = [jax.ShapeDtypeStruct((N_DIR, seq_tiles, n_seq, W_MIX, TM), F32)]
    res = pl.pallas_call(
        functools.partial(_wkv_body, nsrc=nsrc, spb=spb, vs=vs, n_sb=n_sb, tc=tc, kp=kp, vp=vp),
        grid=(N_DIR * n_sb, n_chunks),
        in_specs=in_specs,
        out_specs=out_specs + [sspec],
        out_shape=out_shape + [jax.ShapeDtypeStruct(s0.shape, F32)],
        scratch_shapes=([pltpu.VMEM((tc * kp, LANES), F32)] * 5
                        + [pltpu.VMEM((tc * vp, LANES), F32)] * 2
                        + [pltpu.VMEM((vl_n, HS_WKV, LANES), F32), pltpu.VMEM((vl_n, LANES), F32)]),
        compiler_params=_cparams(("parallel", "arbitrary"), VMEM_LIMIT),
        name="wkv_scan",
    )(*operands, s0)
    return res[0], res[1]


def _merge_body(xc_ref, xl_ref, mod_ref, n1g_ref, n2g_ref, wg_ref, gb_ref, wbr_ref, wo_ref,
                lnxg_ref, lnxb_ref, ind_ref,
                ya_ref, yd_ref, gbg_ref, h_ref, ycf_ref, ycb_ref, ylf_ref, ylb_ref, g_ref, bon_ref,
                x1_ref, h2_ref, *, n_ctx_tiles):
    is_ctx = pl.program_id(0) < n_ctx_tiles
    x = jnp.where(is_ctx, xc_ref[...], xl_ref[...])
    m = mod_ref[0]
    ind = ind_ref[...]
    h = (_rms(x, n1g_ref[...]) * (1.0 + m[1:2, :]) + m[0:1, :]).astype(BF16)
    y_b = gbg_ref[...] * (h_ref[:, 0:W_MIX] + h_ref[:, W_MIX:2 * W_MIX])
    y =jnp.where(is_ctx, ycf_ref[0, 0, 0] + ycb_ref[0, 0, 0], ylf_ref[0, 0, 0] + ylb_ref[0, 0, 0]).T
    yc = y - _segsum(y, ind) * (1.0 / HS_WKV)
    var = _segsum(yc * yc, ind) * (1.0 / HS_WKV)
    y_c = (yc * lax.rsqrt(var + LNX_EPS) * lnxg_ref[...] + lnxb_ref[...] + bon_ref[...]) * g_ref[...]
    merged = None
    for n, yn in enumerate((ya_ref[...], y_b, y_c, yd_ref[...])):
        cs = slice(n * D_MODEL, (n + 1) * D_MODEL)
        gate = jax.nn.sigmoid(jnp.dot(h, wg_ref[:, cs], preferred_element_type=F32) + gb_ref[:, cs])
        br = jnp.dot(yn.astype(BF16), wbr_ref[n * W_MIX:(n + 1) * W_MIX, :], preferred_element_type=F32)
        merged = gate * br if merged is None else merged + gate * br
    mo = jnp.dot(merged.astype(BF16), wo_ref[...], preferred_element_type=F32)
    x1 = x + m[2:3, :] * mo
    x1_ref[...] = x1
    h2_ref[...] = (_rms(x1, n2g_ref[...]) * (1.0 + m[4:5, :]) + m[3:4, :]).astype(BF16)


def _merge(x_ctx, x_lat, mod, lw, tok_in, y_ctx, y_lat, lat_len):
    n_ctx_tok = x_ctx.shape[0]
    t = n_ctx_tok + x_lat.shape[0]
    n_ctx_tiles = n_ctx_tok // TM
    tps = lat_len // TM
    midx = functools.partial(_mod_index, tm=TM, n_ctx_tok=n_ctx_tok, lat_len=lat_len)
    wnames = ["norm1_g", "norm2_g", "w_gate", "gate_b", "w_branch", "w_out", "lnx_g", "lnx_b", "ind"]
    wts = [lw[n] for n in wnames]
    tok = lambda wd: pl.BlockSpec((TM, wd), lambda i: (i, 0))
    ya, yd, gbg, h, g, bon = tok_in
    yblock = (1, 1, 1, W_MIX, TM)

    def ctx_spec(d):
        return pl.BlockSpec(yblock, lambda i: (d, 0, jnp.minimum(i, n_ctx_tiles - 1), 0, 0))

    def lat_spec(d):
        def imap(i):
            r = jnp.maximum(i - n_ctx_tiles, 0)
            return (d, lax.rem(r, tps), lax.div(r, tps), 0, 0)
        return pl.BlockSpec(yblock, imap)

    return pl.pallas_call(
        functools.partial(_merge_body, n_ctx_tiles=n_ctx_tiles),
        grid=(t // TM,),
        in_specs=(_dual_specs(TM, n_ctx_tiles)
                  + [pl.BlockSpec((1, N_MOD, D_MODEL), lambda i: (midx(i), 0, 0))]
                  + [_const_spec(w.shape) for w in wts]
                  + [tok(W_MIX), tok(W_MIX), tok(W_MIX), tok(2 * W_MIX),
                     ctx_spec(0), ctx_spec(1), lat_spec(0), lat_spec(1), tok(W_MIX), tok(W_MIX)]),
        out_specs=[tok(D_MODEL), tok(D_MODEL)],
        out_shape=[jax.ShapeDtypeStruct((t, D_MODEL), F32), jax.ShapeDtypeStruct((t, D_MODEL), BF16)],
        compiler_params=_cparams(("parallel",), VMEM_LIMIT),
        name="merge",
    )(x_ctx, x_lat, mod, *wts, ya, yd, gbg, h, y_ctx, y_ctx, y_lat, y_lat, g, bon)


_CAND_VALID = (8, 8, 8, 5, 4, 3, 2, 2, 2, 8)


def _oddeven_pairs(n):
    def merge(lo, hi, r):
        step = r * 2
        if step < hi - lo:
            yield from merge(lo, hi, step)
            yield from merge(lo + r, hi, step)
            yield from [(i, i + r) for i in range(lo + r, hi - r, step)]
        else:
            yield (lo, lo + r)

    def sort(lo, hi):
        if hi - lo >= 1:
            mid = lo + (hi - lo) // 2
            yield from sort(lo, mid)
            yield from sort(mid + 1, hi)
            yield from merge(lo, hi, 1)

    return tuple(sort(0, n - 1))


_SORT16 = _oddeven_pairs(N_KEYS // SUBLANES)


def _route_head(qs, keys_ref):
    kio = lax.broadcasted_iota(I32, (N_KEYS, LANES), 0)
    sub = lax.broadcasted_iota(I32, (SUBLANES, LANES), 0)
    kid = lax.broadcasted_iota(I32, (PEER_TOPK, LANES), 0)
    neg = -jnp.inf

    def bc(x, r):
        return jnp.broadcast_to(x[r:r + 1, :], (SUBLANES, LANES))

    def head():
        tops = []
        for p in range(2):
            s = jnp.dot(keys_ref[p], qs[p], preferred_element_type=F32)
            cols = [s[j * SUBLANES:(j + 1) * SUBLANES, :] for j in range(N_KEYS // SUBLANES)]
            cidx = [sub + j * SUBLANES for j in range(N_KEYS // SUBLANES)]
            for a, b in _SORT16:
                take = cols[b] > cols[a]
                cols[a], cols[b] = jnp.where(take, cols[b], cols[a]), jnp.where(take, cols[a], cols[b])
                cidx[a], cidx[b] = jnp.where(take, cidx[b], cidx[a]), jnp.where(take, cidx[a], cidx[b])
            vals = jnp.zeros((PEER_TOPK, LANES), F32)
            idxs = jnp.zeros((PEER_TOPK, LANES), I32)
            for r in range(PEER_TOPK):
                v8, i8 = cols[0], cidx[0]
                for sh in (4, 2, 1):
                    vr, ir = pltpu.roll(v8, sh, 0), pltpu.roll(i8, sh, 0)
                    take = vr > v8
                    v8, i8 = jnp.where(take, vr, v8), jnp.where(take, ir, i8)
                m, ix = v8[0:1, :], i8[0:1, :]
                vals = jnp.where(kid == r, m, vals)
                idxs = jnp.where(kid == r, ix, idxs)
                popped = cidx[0] == ix
                for j in range(PEER_TOPK - 1 - r):
                    cols[j] = jnp.where(popped, cols[j + 1], cols[j])
                    cidx[j] = jnp.where(popped, cidx[j + 1], cidx[j])
            tops.append((vals, idxs))
        (a0, i0), (a1, i1) = tops
        lo, hi = slice(0, SUBLANES), slice(SUBLANES, 2 * SUBLANES)
        slabs = [bc(a0, 0) + a1[lo], bc(a0, 0) + a1[hi]]
        ci = [bc(i0, 0), bc(i0, 0)]
        cj = [i1[lo], i1[hi]]
        for r in range(1, SUBLANES):
            slabs.append(bc(a0, r) + a1[lo])
            ci.append(bc(i0, r))
            cj.append(i1[lo])
        slabs.append(a0[hi] + bc(a1, 0))
        ci.append(i0[hi])
        cj.append(bc(i1, 0))
        slabs = [jnp.where(sub < nv, sl, neg) for sl, nv in zip(slabs, _CAND_VALID)]
        ids = [a * N_KEYS + b for a, b in zip(ci, cj)]
        vals = jnp.zeros((PEER_TOPK, LANES), F32)
        esel = jnp.zeros((PEER_TOPK, LANES), I32)
        for r in range(PEER_TOPK):
            level = list(zip(slabs, ids))
            while len(level) > 1:
                nxt = []
                for (va, ea), (vb, eb) in zip(level[0::2], level[1::2]):
                    take = vb > va
                    nxt.append((jnp.where(take, vb, va), jnp.where(take, eb, ea)))
                if len(level) % 2:
                    nxt.append(level[-1])
                level = nxt
            v8, e8 = level[0]
            for sh in (4, 2, 1):
                vr, er = pltpu.roll(v8, sh, 0), pltpu.roll(e8, sh, 0)
                take = vr > v8
                v8, e8 = jnp.where(take, vr, v8), jnp.where(take, er, e8)
            m, ex = v8[0:1, :], e8[0:1, :]
            slabs = [jnp.where(eid == ex, neg, sl) for sl, eid in zip(slabs, ids)]
            vals = jnp.where(kid == r, m, vals)
            esel = jnp.where(kid == r, ex, esel)
        e = jnp.exp(vals - vals[0:1, :])
        return esel, e / jnp.sum(e, axis=0, keepdims=True)

    return head()


def _peer_body(h2_ref, h2n_ref, wqt_ref, keys_ref, u_ref, v_ref, x1_ref, mod_ref, fng_ref,
               oc_ref, ol_ref,
               q_scr, e_scr, g_scr, et_scr, gt_scr, gs_scr, acc_scr,
               *, rows, pitch, units, n_ctx_tiles, final):
    m = pl.program_id(0)
    e = pl.program_id(1)
    tm = h2_ref.shape[0]
    n_chunks = tm // LANES
    nsel = PEER_HEADS * PEER_TOPK
    slot = lax.rem(m, 2)

    def project_queries(src_ref):
        q = lax.dot_general(wqt_ref[...], src_ref[...], _NT, preferred_element_type=F32).astype(BF16)
        for c in range(n_chunks):
            q_scr[c] = q[:, c * LANES:(c + 1) * LANES]

    def route_unit(u, dst):
        c = u // PEER_HEADS
        h = lax.rem(u, PEER_HEADS)
        qs = [q_scr[c, pl.ds(pl.multiple_of(h * (2 * N_KEYS) + p * N_KEYS, N_KEYS), N_KEYS), :]
              for p in range(2)]
        esel, gates = _route_head(qs, keys_ref)
        row = pl.multiple_of(h * PEER_TOPK, PEER_TOPK)
        e_scr[dst, c, pl.ds(row, PEER_TOPK), :] = esel
        g_scr[dst, c, pl.ds(row, PEER_TOPK), :] = gates

    @pl.when(jnp.logical_and(e == 0, m == 0))
    def _first_tile_routing():
        project_queries(h2_ref)

        def unit(u, c):
            route_unit(u, 0)
            return c

        lax.fori_loop(0, n_chunks * PEER_HEADS, unit, 0)

    @pl.when(e == 0)
    def _build():
        for c in range(n_chunks):
            et_scr[c * LANES:(c + 1) * LANES, :] = e_scr[slot, c].T
            gt_scr[c * LANES:(c + 1) * LANES, :] = g_scr[slot, c].T
        kio = lax.broadcasted_iota(I32, (N_KEYS, nsel), 0)

        def tok(t, c):
            erow = et_scr[pl.ds(t, 1), :]
            grow = gt_scr[pl.ds(t, 1), :]
            at = jnp.where(kio == (erow >> 7), grow, 0.0).astype(BF16)
            bt = jnp.where(kio == (erow & (N_KEYS - 1)), 1.0, 0.0).astype(BF16)
            gt = lax.dot_general(at, bt, _NT, preferred_element_type=F32)
            hi = pltpu.bitcast(gt[0:rows, :], jnp.uint32) & jnp.uint32(0xFFFF0000)
            lo = pltpu.bitcast(gt[rows:2 * rows, :], jnp.uint32) >> 16
            gs_scr[pl.ds(pl.multiple_of(t * pitch, SUBLANES), rows), :] = hi | lo
            return c

        lax.fori_loop(0, tm, tok, 0, unroll=16)
        acc_scr[...] = jnp.zeros_like(acc_scr)
        project_queries(h2n_ref)

    for k in range(units):
        route_unit(e * units + k, 1 - slot)

    per_sub = PEER_SUB // N_KEYS
    per_step = u_ref.shape[0] // N_KEYS
    steps_per_half = rows // per_step
    row0 = lax.rem(e, steps_per_half) * per_step
    shift = jnp.where(e < steps_per_half, 0, 16).astype(jnp.uint32)
    h2 = h2_ref[...]
    total = None
    for sb in range(per_step // per_sub):
        es = slice(sb * PEER_SUB, (sb + 1) * PEER_SUB)
        hmat = lax.dot_general(h2, u_ref[es, :], _NT, preferred_element_type=F32)
        words = jnp.concatenate(
            [gs_scr[pl.ds(row0 + sb * per_sub + ii, tm, stride=pitch), :] for ii in range(per_sub)], axis=1)
        gm = pltpu.bitcast((words << shift) & jnp.uint32(0xFFFF0000), F32)
        act = jax.nn.gelu(hmat.astype(BF16)) * gm.astype(BF16)
        part = jnp.dot(act, v_ref[es, :], preferred_element_type=F32)
        total = part if total is None else total + part
    acc_scr[...] += total

    def result():
        x2 = x1_ref[...] + mod_ref[0][5:6, :] * acc_scr[...]
        return _rms(x2, fng_ref[...]) if final else x2

    last = e == pl.num_programs(1) - 1

    @pl.when(jnp.logical_and(last, m < n_ctx_tiles))
    def _out_ctx():
        oc_ref[...] = result()

    @pl.when(jnp.logical_and(last, m >= n_ctx_tiles))
    def _out_lat():
        ol_ref[...] = result()


def _peer(h2, wqt, keys, u, v, layer, x1, mod, fng, n_ctx_tok, lat_len, final):
    t = h2.shape[0]
    nsel = PEER_HEADS * PEER_TOPK
    rows = N_KEYS // 2
    pitch = rows + SUBLANES
    n_e = (N_KEYS * N_KEYS) // PEER_EB
    n_m = t // TM_PEER
    n_chunks = TM_PEER // LANES
    units = (n_chunks * PEER_HEADS) // n_e
    assert units * n_e == n_chunks * PEER_HEADS
    midx = functools.partial(_mod_index, tm=TM_PEER, n_ctx_tok=n_ctx_tok, lat_len=lat_len)
    tok = lambda wd: pl.BlockSpec((TM_PEER, wd), lambda m, e: (m, 0))
    nxt = pl.BlockSpec((TM_PEER, D_MODEL), lambda m, e: (jnp.minimum(m + 1, n_m - 1), 0),
                       pipeline_mode=pl.Buffered(1))
    espec = pl.BlockSpec((None, PEER_EB, D_MODEL), lambda m, e: (layer, e, 0))
    single = dict(pipeline_mode=pl.Buffered(1))
    return pl.pallas_call(
        functools.partial(_peer_body, rows=rows, pitch=pitch, units=units,
                          n_ctx_tiles=n_ctx_tok // TM_PEER, final=final),
        grid=(n_m, n_e),
        in_specs=[tok(D_MODEL), nxt,
                  pl.BlockSpec(wqt.shape, lambda m, e: (0, 0), **single),
                  pl.BlockSpec(keys.shape, lambda m, e: (0, 0, 0), **single),
                  espec, espec,
                  pl.BlockSpec((TM_PEER, D_MODEL), lambda m, e: (m, 0), **single),
                  pl.BlockSpec((1, N_MOD, D_MODEL), lambda m, e: (midx(m), 0, 0)),
                  _const_spec((1, D_MODEL))],
        out_specs=_dual_specs(TM_PEER, n_ctx_tok // TM_PEER),
        out_shape=[jax.ShapeDtypeStruct((n_ctx_tok, D_MODEL), F32),
                   jax.ShapeDtypeStruct((t - n_ctx_tok, D_MODEL), F32)],
        scratch_shapes=[pltpu.VMEM((n_chunks, wqt.shape[0], LANES), BF16),
                        pltpu.VMEM((2, n_chunks, nsel, LANES), I32),
                        pltpu.VMEM((2, n_chunks, nsel, LANES), F32),
                        pltpu.VMEM((TM_PEER, nsel), I32),
                        pltpu.VMEM((TM_PEER, nsel), F32),
                        pltpu.VMEM((TM_PEER * pitch, N_KEYS), jnp.uint32),
                        pltpu.VMEM((TM_PEER, D_MODEL), F32)],
        compiler_params=_cparams(("arbitrary", "arbitrary"), VMEM_LIMIT),
        name="peer",
    )(h2, h2, wqt, keys, u, v, x1, mod, fng)


def _wkv_state_in(s, vs):
    n = s.shape[0]
    vl = HS_WKV // vs
    s = s.reshape(n, N_DIR, H_WKV, vs, vl, HS_WKV).transpose(1, 4, 5, 3, 0, 2)
    return s.reshape(N_DIR, vl, HS_WKV, vs * n * H_WKV)


def _wkv_state_out(s, n_sb, spb, vs):
    vl = HS_WKV // vs
    s = s.reshape(N_DIR, n_sb, vl, HS_WKV, vs, spb, H_WKV).transpose(1, 5, 0, 6, 4, 2, 3)
    return s.reshape(n_sb * spb, N_DIR, H_WKV, HS_WKV, HS_WKV)


def _layer_weights(i, prm):
    eye_h = jnp.eye(H_LRU, dtype=F32)
    eye_d = jnp.eye(N_DIR, dtype=F32)
    perm = _WKV_PERM

    def lru_bd(wt):
        return jnp.einsum("dhij,hg->hidgj", wt, eye_h).reshape(W_MIX, N_DIR * W_MIX)

    def lora_bd(wt):
        r = wt.shape[1]
        return jnp.einsum("drc,de->drec", wt, eye_d).reshape(N_DIR * r, N_DIR * W_MIX)

    w_in = prm["w_in"][i]
    pad = jnp.zeros((D_MODEL, Z_COLS - 5504), F32)
    rkv = [w_in[:, 2560 + j * W_MIX:2560 + (j + 1) * W_MIX][:, perm] for j in range(3)]
    w_in_perm = jnp.concatenate(
        [w_in[:, 0:1536]] + rkv + [w_in[:, 1536:2560], w_in[:, 4480:5504], w_in[:, 4096:4480], pad],
        axis=1).astype(BF16)
    row = lambda x: x.reshape(1, -1).astype(F32)
    head_of = np.arange(W_MIX) % H_WKV
    w_branch = prm["w_branch"][i]
    w_branch = jnp.concatenate([w_branch[0], w_branch[1], w_branch[2][perm, :], w_branch[3]], axis=0)
    return {
        "w_in": w_in_perm,
        "w_gate": w_in[:, 5504:].astype(BF16),
        "norm1_g": row(prm["norm1_g"][i]),
        "norm2_g": row(prm["norm2_g"][i]),
        "conv_a_w": prm["conv_a_w"][i],
        "conv_b_w": prm["conv_b_w"][i],
        "conv_b_b": row(prm["conv_b_b"][i]),
        "lru_w": jnp.concatenate([lru_bd(prm["lru_wa"][i]), lru_bd(prm["lru_wx"][i])], axis=1).astype(BF16),
        "lru_b": jnp.concatenate([row(prm["lru_ba"][i]), row(prm["lru_bx"][i])], axis=1),
        "lru_lam": row(prm["lru_lambda"][i]),
        "w0": row(prm["rwkv_w0"][i][:, perm]),
        "w2": lora_bd(prm["rwkv_w2"][i][:, :, perm]).astype(BF16),
        "a0": row(prm["rwkv_a0"][i][:, perm]),
        "a2": lora_bd(prm["rwkv_a2"][i][:, :, perm]).astype(BF16),
        "g2": prm["rwkv_g2"][i][:, perm].astype(BF16),
        "kk": row(prm["rwkv_kk"][i][perm]),
        "ka": row(prm["rwkv_ka"][i][perm]),
        "rk": row(prm["rwkv_rk"][i].reshape(W_MIX)[perm]),
        "lnx_g": row(prm["lnx_g"][i][perm]),
        "lnx_b": row(prm["lnx_b"][i][perm]),
        "sg_ln_g": row(prm["sg_ln_g"][i]),
        "sg_ln_b": row(prm["sg_ln_b"][i]),
        "sg_ws": prm["sg_ws"][i].astype(BF16),
        "sg_bst": prm["sg_bs"][i].T,
        "gate_b": row(prm["gate_b"][i]),
        "w_branch": w_branch.astype(BF16),
        "w_out": prm["w_out"][i].astype(BF16),
        "wq_t": prm["peer_wq"][i].T.astype(BF16),
        "keys": prm["peer_keys"][i].astype(BF16),
        "ind": jnp.asarray(head_of[:, None] == head_of[None, :], BF16),
    }


def kernel(x_prompt, x_sample, state_lru, state_wkv, c, c_ctx, norm1_g, norm2_g, w_mod, b_mod, w_in, conv_a_w, conv_b_w, conv_b_b, lru_wa, lru_ba, lru_wx, lru_bx, lru_lambda, rwkv_w0, rwkv_w2, rwkv_a0, rwkv_a2, rwkv_g2, rwkv_kk, rwkv_ka, rwkv_rk, lnx_g, lnx_b, sg_ln_g, sg_ln_b, sg_ws, sg_bs, gate_b, w_branch, w_out, peer_wq, peer_keys, peer_u, peer_v, final_norm_g):
    prm = dict(norm1_g=norm1_g, norm2_g=norm2_g, w_in=w_in, conv_a_w=conv_a_w, conv_b_w=conv_b_w,
               conv_b_b=conv_b_b, lru_wa=lru_wa, lru_ba=lru_ba, lru_wx=lru_wx, lru_bx=lru_bx,
               lru_lambda=lru_lambda, rwkv_w0=rwkv_w0, rwkv_w2=rwkv_w2, rwkv_a0=rwkv_a0, rwkv_a2=rwkv_a2,
               rwkv_g2=rwkv_g2, rwkv_kk=rwkv_kk, rwkv_ka=rwkv_ka, rwkv_rk=rwkv_rk, lnx_g=lnx_g,
               lnx_b=lnx_b, sg_ln_g=sg_ln_g, sg_ln_b=sg_ln_b, sg_ws=sg_ws, sg_bs=sg_bs, gate_b=gate_b,
               w_branch=w_branch, w_out=w_out, peer_wq=peer_wq, peer_keys=peer_keys, peer_u=peer_u,
               peer_v=peer_v)
    bc, lc, _ = x_prompt.shape
    bl, ll, _ = x_sample.shape
    depth = w_mod.shape[0]
    n_ctx_tok = bc * lc
    n_ctx_tiles = n_ctx_tok // TM
    lat_tiles = ll // TM
    ctx_spb = min(WKV_CTX_SPB, bc)
    lru_spb = ll // lc
    assert lc == TM and ll % TM_PEER == 0 and n_ctx_tok % TM_PEER == 0 and bl + 1 <= SUBLANES
    assert ll % GRID_W == 0 and TM % GRID_W == 0 and bc % ctx_spb == 0 and n_ctx_tok % ll == 0
    assert LANES % (ctx_spb * H_WKV) == 0 and LANES % (bl * H_WKV) == 0

    cond = jnp.zeros((SUBLANES, D_MODEL), F32).at[0].set(c_ctx).at[1:1 + bl].set(c)
    mods = _modulation(cond, w_mod, b_mod).reshape(depth, SUBLANES, N_MOD, D_MODEL)
    fng = final_norm_g.reshape(1, D_MODEL)
    x_ctx = x_prompt.reshape(n_ctx_tok, D_MODEL)
    x_lat = x_sample.reshape(bl * ll, D_MODEL)
    ctx_vs = LANES // (ctx_spb * H_WKV)
    lat_vs = LANES // (bl * H_WKV)
    n_sb = bc // ctx_spb
    wkv_zero = jnp.zeros((N_DIR * n_sb, HS_WKV // ctx_vs, HS_WKV, LANES), F32)
    lru_zero = jnp.zeros((n_ctx_tok // ll, lru_spb, N_DIR * W_MIX), F32)
    u_all = peer_u.astype(BF16)
    v_all = peer_v.astype(BF16)
    new_lru, new_wkv = [], []
    pnames = ["ya", "yd", "gbg", "la", "lu", "g", "bon", "rt", "vt", "kkt", "wt", "kt", "bt"]
    for i in range(depth):
        lw = _layer_weights(i, prm)
        mod = mods[i]
        p = dict(zip(pnames, _prep(x_ctx, x_lat, mod, lw, ll)))

        lat_h0 = jnp.zeros((bl, lru_spb, N_DIR * W_MIX), F32).at[:, 0].set(
            state_lru[:, i].astype(F32).reshape(bl, N_DIR * W_MIX))
        h, lru_s = _lru_scan(p["la"], p["lu"], jnp.concatenate([lru_zero, lat_h0], axis=0),
                             rows=ll, n_ctx_blocks=n_ctx_tok // ll, ctx_cfg=(lru_spb, lc), lat_cfg=(1, ll))
        new_lru.append(lru_s[:n_ctx_tok // ll].reshape(bc, N_DIR, W_MIX))

        wkv_in = [p[n] for n in ("rt", "wt", "kt", "kkt", "bt", "vt")]
        y_c, s_c = _wkv_scan(*wkv_in, wkv_zero, tile0=0, n_seq=bc, seq_tiles=1, spb=ctx_spb)
        y_l, _ = _wkv_scan(*wkv_in, _wkv_state_in(state_wkv[:, i].astype(F32), lat_vs),
                           tile0=n_ctx_tiles, n_seq=bl, seq_tiles=lat_tiles, spb=1)
        new_wkv.append(_wkv_state_out(s_c, n_sb, ctx_spb, ctx_vs))

        tok_in = [p["ya"], p["yd"], p["gbg"], h, p["g"], p["bon"]]
        x1, h2 = _merge(x_ctx, x_lat, mod, lw, tok_in, y_c, y_l, ll)
        x_ctx, x_lat = _peer(h2, lw["wq_t"], lw["keys"], u_all, v_all, i, x1, mod, fng,
                             n_ctx_tok, ll, final=(i == depth - 1))
    y_prompt = x_ctx.reshape(bc, lc, D_MODEL)
    y_sample = x_lat.reshape(bl, ll, D_MODEL)
    return (y_prompt, y_sample, jnp.stack(new_lru, axis=1), jnp.stack(new_wkv, axis=1))
```

```python
import functools

import numpy as np
import jax
import jax.numpy as jnp
from jax import lax
from jax.experimental import pallas as pl
from jax.experimental.pallas import tpu as pltpu

F32 = jnp.float32
BF16 = jnp.bfloat16
I32 = jnp.int32

D_MODEL = 1024
W_MIX = 512
N_DIR = 2
N_BRANCH = 4
H_WKV = 8
HS_WKV = 64
H_LRU = 8
HB_LRU = 64
LORA_W = 64
LORA_A = 64
LORA_G = 128
GRID_W = 64
CHUNK = 128
G_SG = 4
N_KEYS = 128
PEER_HEADS = 8
PEER_TOPK = 16
N_MOD = 6
EPS = 1e-6
LNX_EPS = 64e-5
LRU_C = 8.0

LANES = 128
SUBLANES = 8
TM = 256
TM_PEER = 512
PEER_EB = 1024
PEER_SUB = 512
Z_COLS = 5632
WKV_TC = LANES
WKV_CTX_SPB = 8
VMEM_LIMIT = 56 * 1024 * 1024

_NT = (((1,), (1,)), ((), ()))
_WKV_PERM = np.array([(n % H_WKV) * HS_WKV + n // H_WKV for n in range(W_MIX)])


def _cparams(sem, vmem=None):
    return pltpu.CompilerParams(dimension_semantics=sem, vmem_limit_bytes=vmem)


def _const_spec(shape):
    nd = len(shape)
    return pl.BlockSpec(shape, lambda *_: (0,) * nd)


def _softplus(x):
    return jnp.maximum(x, 0.0) + jnp.log1p(jnp.exp(-jnp.abs(x)))


def _rms(x, g):
    return x * lax.rsqrt(jnp.mean(x * x, axis=-1, keepdims=True) + EPS) * g


def _segsum(x, ind):
    hi = x.astype(BF16)
    lo = (x - hi.astype(F32)).astype(BF16)
    return (jnp.dot(hi, ind, preferred_element_type=F32)
            + jnp.dot(lo, ind, preferred_element_type=F32))


def _mod_index(i, tm, n_ctx_tok, lat_len):
    n_ctx_tiles = n_ctx_tok // tm
    tiles_per_seq = lat_len // tm
    return jnp.where(i < n_ctx_tiles, 0, 1 + lax.div(i - n_ctx_tiles, tiles_per_seq))


def _mod_body(s_ref, w_ref, b_ref, o_ref):
    s = s_ref[...]
    s = s * jax.nn.sigmoid(s)
    o_ref[0] = jnp.dot(s.astype(BF16), w_ref[0].astype(BF16), preferred_element_type=F32) + b_ref[0]


def _modulation(cond, w_mod, b_mod):
    depth = w_mod.shape[0]
    n = w_mod.shape[2]
    tn = 1536
    return pl.pallas_call(
        _mod_body,
        grid=(depth, n // tn),
        in_specs=[_const_spec((SUBLANES, D_MODEL)),
                  pl.BlockSpec((1, D_MODEL, tn), lambda l, j: (l, 0, j)),
                  pl.BlockSpec((1, 1, tn), lambda l, j: (l, 0, j))],
        out_specs=pl.BlockSpec((1, SUBLANES, tn), lambda l, j: (l, 0, j)),
        out_shape=jax.ShapeDtypeStruct((depth, SUBLANES, n), F32),
        compiler_params=_cparams(("parallel", "parallel"), VMEM_LIMIT),
        name="modulation",
    )(cond, w_mod, b_mod.reshape(depth, 1, n))


def _prep_body(xc_ref, xl_ref, xp_ref, xn_ref, mod_ref, n1g_ref, win_ref,
               caw_ref, cbw_ref, cbb_ref, lruw_ref, lrub_ref, lam_ref,
               w0_ref, w2_ref, a0_ref, a2_ref, g2_ref, kkw_ref, ka_ref, rk_ref,
               lng_ref, lnb_ref, ws_ref, bst_ref, ind_ref,
               ya_ref, yd_ref, gbg_ref, la_ref, lu_ref, g_ref, bon_ref,
               rt_ref, vt_ref, kkt_ref, wt_ref, kt_ref, bt_ref,
               *, n_ctx_tiles, tiles_per_seq):
    i = pl.program_id(0)
    is_ctx = i < n_ctx_tiles
    t = lax.broadcasted_iota(I32, (TM, 1), 0)
    ind = ind_ref[...]
    m = mod_ref[0]

    def modulated(xv):
        return (_rms(xv, n1g_ref[...]) * (1.0 + m[1:2, :]) + m[0:1, :]).astype(BF16)

    def project(hv, lo, hi):
        return jnp.dot(hv, win_ref[:, lo:hi], preferred_element_type=F32)

    h = modulated(jnp.where(is_ctx, xc_ref[...], xl_ref[...]))
    za = project(h, 0, 1536)
    zc = project(h, 1536, 3072)
    zb = project(h, 3072, 4096)
    zd = project(h, 4096, 5120)
    zl = project(h, 5120, 5120 + 2 * LORA_W + 2 * LORA_A + LORA_G)
    halo_prev = project(modulated(xp_ref[...]), 3072 + W_MIX, 4096)
    halo_next = project(modulated(xn_ref[...]), 3072 + W_MIX, 4096)

    pm = jnp.where(is_ctx, TM - 1, GRID_W - 1)
    pos = t & pm
    a_b = za[:, 0:W_MIX]
    ac = za[:, W_MIX:2 * W_MIX] * za[:, 2 * W_MIX:3 * W_MIX]
    up = jnp.where(pos == 0, 0.0, pltpu.roll(ac, 1, 0))
    dn = jnp.where(pos == pm, 0.0, pltpu.roll(ac, TM - 1, 0))
    ya_ref[...] = a_b * (caw_ref[0:1, :] * up + caw_ref[1:2, :] * ac + caw_ref[2:3, :] * dn)

    seq_tile = lax.rem(jnp.maximum(i - n_ctx_tiles, 0), tiles_per_seq)
    first = jnp.logical_or(is_ctx, seq_tile == 0)
    last = jnp.logical_or(is_ctx, seq_tile == tiles_per_seq - 1)
    prev = jnp.where(first, 0.0, halo_prev[SUBLANES - 1:SUBLANES, :])
    nxt0 = jnp.where(last, 0.0, halo_next[0:1, :])
    nxt1 = jnp.where(last, 0.0, halo_next[1:2, :])
    bx = zb[:, W_MIX:2 * W_MIX]
    m1 = jnp.where(t == 0, prev, pltpu.roll(bx, 1, 0))
    p1 = jnp.where(t == TM - 1, nxt0, pltpu.roll(bx, TM - 1, 0))
    p2 = jnp.where(t == TM - 2, nxt0, jnp.where(t == TM - 1, nxt1, pltpu.roll(bx, TM - 2, 0)))
    xb = (cbw_ref[0:1, :] * m1 + cbw_ref[1:2, :] * bx + cbw_ref[2:3, :] * p1
          + cbw_ref[3:4, :] * p2 + cbb_ref[...])
    gates = jnp.dot(xb.astype(BF16), lruw_ref[...], preferred_element_type=F32) + lrub_ref[...]
    rg = jax.nn.sigmoid(gates[:, 0:2 * W_MIX])
    ig = jax.nn.sigmoid(gates[:, 2 * W_MIX:4 * W_MIX])
    log_a = -LRU_C * rg * _softplus(-lam_ref[...])
    xb2 = jnp.concatenate([xb, xb], axis=1)
    la_ref[...] = jnp.exp(log_a)
    lu_ref[...] = jnp.sqrt(jnp.tanh(-log_a) * (jnp.exp(2.0 * log_a) + 1.0)) * (ig * xb2)
    gbg_ref[...] = jax.nn.gelu(zb[:, 0:W_MIX])

    zr = zc[:, 0:W_MIX]
    zk = zc[:, W_MIX:2 * W_MIX]
    zv = zc[:, 2 * W_MIX:3 * W_MIX]
    zwd = zl[:, 0:2 * LORA_W]
    zad = zl[:, 2 * LORA_W:2 * LORA_W + 2 * LORA_A]
    zgd = zl[:, 2 * LORA_W + 2 * LORA_A:2 * LORA_W + 2 * LORA_A + LORA_G]
    wlin = w0_ref[...] + jnp.dot(jnp.tanh(zwd).astype(BF16), w2_ref[...], preferred_element_type=F32)
    wt_ref[0] = jnp.exp(-jnp.exp(-_softplus(-wlin) - 0.5)).T
    av = jax.nn.sigmoid(a0_ref[...] + jnp.dot(zad.astype(BF16), a2_ref[...], preferred_element_type=F32))
    g_ref[...] = jnp.dot(jax.nn.sigmoid(zgd).astype(BF16), g2_ref[...], preferred_element_type=F32)
    kkr = zk * kkw_ref[...]
    kkn = kkr / jnp.maximum(jnp.sqrt(_segsum(kkr * kkr, ind)), 1e-12)
    zk2 = jnp.concatenate([zk, zk], axis=1)
    ka2 = jnp.concatenate([ka_ref[...], ka_ref[...]], axis=1)
    kd = zk2 * (1.0 + (av - 1.0) * ka2)
    kt_ref[0] = kd.T
    bt_ref[0] = (jnp.concatenate([kkn, kkn], axis=1) * av).T
    rt_ref[0] = zr.T
    vt_ref[0] = zv.T
    kkt_ref[0] = kkn.T
    bon_ref[...] = _segsum(zr * (kd[:, 0:W_MIX] + kd[:, W_MIX:2 * W_MIX]) * rk_ref[...], ind) * zv

    zg = jax.nn.gelu(zd)
    u = zg[:, 0:W_MIX]
    vv = zg[:, W_MIX:2 * W_MIX]
    vc = vv - jnp.mean(vv, axis=-1, keepdims=True)
    vn = vc * lax.rsqrt(jnp.mean(vc * vc, axis=-1, keepdims=True) + 1e-5) * lng_ref[...] + lnb_ref[...]
    for c in range(TM // CHUNK):
        rs = slice(c * CHUNK, (c + 1) * CHUNK)
        for gi in range(G_SG):
            cs = slice(gi * LANES, (gi + 1) * LANES)
            s = jnp.dot(ws_ref[gi], vn[rs, cs].astype(BF16), preferred_element_type=F32)
            yd_ref[rs, cs] = u[rs, cs] * (s + bst_ref[:, gi:gi + 1])


def _dual_specs(rows, n_ctx_blocks, **kw):
    return [pl.BlockSpec((rows, D_MODEL), lambda i, *_: (jnp.minimum(i, n_ctx_blocks - 1), 0), **kw),
            pl.BlockSpec((rows, D_MODEL), lambda i, *_: (jnp.maximum(i - n_ctx_blocks, 0), 0), **kw)]


def _prep(x_ctx, x_lat, mod, lw, lat_len):
    n_ctx_tok = x_ctx.shape[0]
    t = n_ctx_tok + x_lat.shape[0]
    n_tiles = t // TM
    n_ctx_tiles = n_ctx_tok // TM
    tiles_per_seq = lat_len // TM
    rows8 = TM // SUBLANES
    last_blk = x_lat.shape[0] // SUBLANES - 1
    midx = functools.partial(_mod_index, tm=TM, n_ctx_tok=n_ctx_tok, lat_len=lat_len)

    def lat_blk8(i, off):
        return (jnp.clip((i - n_ctx_tiles) * rows8 + off, 0, last_blk), 0)

    x_specs = _dual_specs(TM, n_ctx_tiles) + [
        pl.BlockSpec((SUBLANES, D_MODEL), lambda i: lat_blk8(i, -1)),
        pl.BlockSpec((SUBLANES, D_MODEL), lambda i: lat_blk8(i, rows8)),
        pl.BlockSpec((1, N_MOD, D_MODEL), lambda i: (midx(i), 0, 0)),
        _const_spec((1, D_MODEL)),
        pl.BlockSpec(lw["w_in"].shape, lambda i: (0, 0), pipeline_mode=pl.Buffered(1)),
    ]
    wnames = ["conv_a_w", "conv_b_w", "conv_b_b", "lru_w", "lru_b", "lru_lam", "w0", "w2", "a0", "a2",
              "g2", "kk", "ka", "rk", "sg_ln_g", "sg_ln_b", "sg_ws", "sg_bst", "ind"]
    wts = [lw[n] for n in wnames]
    w_specs = [_const_spec(w.shape) for w in wts]
    widths = [W_MIX, W_MIX, W_MIX, 2 * W_MIX, 2 * W_MIX, W_MIX, W_MIX]
    t_rows = [W_MIX, W_MIX, W_MIX, 2 * W_MIX, 2 * W_MIX, 2 * W_MIX]
    out_specs = ([pl.BlockSpec((TM, wd), lambda i: (i, 0)) for wd in widths]
                 + [pl.BlockSpec((1, r, TM), lambda i: (i, 0, 0)) for r in t_rows])
    out_shape = ([jax.ShapeDtypeStruct((t, wd), F32) for wd in widths]
                 + [jax.ShapeDtypeStruct((n_tiles, r, TM), F32) for r in t_rows])
    return pl.pallas_call(
        functools.partial(_prep_body, n_ctx_tiles=n_ctx_tiles, tiles_per_seq=tiles_per_seq),
        grid=(n_tiles,),
        in_specs=x_specs + w_specs,
        out_specs=out_specs,
        out_shape=out_shape,
        compiler_params=_cparams(("parallel",), VMEM_LIMIT),
        name="branch_prep",
    )(x_ctx, x_lat, x_lat, x_lat, mod, lw["norm1_g"], lw["w_in"], *wts)


def _lru_body(a_ref, u_ref, h0_ref, h_ref, hf_ref, *, n_ctx_blocks, ctx_cfg, lat_cfg):
    fw, bw = slice(0, W_MIX), slice(W_MIX, 2 * W_MIX)

    def scan(nseq, l):
        def step(s, carry):
            out = []
            for j in range(nseq):
                tf = j * l + s
                tb = j * l + (l - 1 - s)
                hf = a_ref[pl.ds(tf, 1), fw] * carry[2 * j] + u_ref[pl.ds(tf, 1), fw]
                hb = a_ref[pl.ds(tb, 1), bw] * carry[2 * j + 1] + u_ref[pl.ds(tb, 1), bw]
                h_ref[pl.ds(tf, 1), fw] = hf
                h_ref[pl.ds(tb, 1), bw] = hb
                out += [hf, hb]
            return tuple(out)

        init = []
        for j in range(nseq):
            init += [h0_ref[0, j:j + 1, fw], h0_ref[0, j:j + 1, bw]]
        fin = lax.fori_loop(0, l, step, tuple(init), unroll=2)
        hf_ref[0] = h0_ref[0]
        for j in range(nseq):
            hf_ref[0, j:j + 1, fw] = fin[2 * j]
            hf_ref[0, j:j + 1, bw] = fin[2 * j + 1]

    is_ctx = pl.program_id(0) < n_ctx_blocks
    pl.when(is_ctx)(lambda: scan(*ctx_cfg))
    pl.when(jnp.logical_not(is_ctx))(lambda: scan(*lat_cfg))


def _lru_scan(a, u, h0, *, rows, n_ctx_blocks, ctx_cfg, lat_cfg):
    nb = a.shape[0] // rows
    w = a.shape[1]
    tok = pl.BlockSpec((rows, w), lambda i: (i, 0))
    st = pl.BlockSpec((1,) + h0.shape[1:], lambda i: (i, 0, 0))
    return pl.pallas_call(
        functools.partial(_lru_body, n_ctx_blocks=n_ctx_blocks, ctx_cfg=ctx_cfg, lat_cfg=lat_cfg),
        grid=(nb,),
        in_specs=[tok, tok, st],
        out_specs=[tok, st],
        out_shape=[jax.ShapeDtypeStruct(a.shape, F32), jax.ShapeDtypeStruct(h0.shape, F32)],
        compiler_params=_cparams(("parallel",), VMEM_LIMIT),
        name="lru_scan",
    )(a, u, h0)


def _wkv_body(*refs, nsrc, spb, vs, n_sb, tc, kp, vp):
    vl_n = HS_WKV // vs
    n_in = 6 * nsrc
    k_srcs = [refs[o * nsrc:(o + 1) * nsrc] for o in range(5)]
    v_srcs = refs[5 * nsrc:n_in]
    s0_ref = refs[n_in]
    y_ref = refs[n_in + 1]
    sf_ref = refs[n_in + 2]
    k_scr = refs[n_in + 3:n_in + 8]
    v_scr, y_scr, s_scr = refs[n_in + 8:n_in + 11]
    r_scr, w_scr, k_scr_, kk_scr, b_scr = k_scr
    backward = pl.program_id(0) // n_sb == 1
    seqs = [(s, j) for s in range(nsrc) for j in range(spb)]

    @pl.when(pl.program_id(1) == 0)
    def _():
        s_scr[...] = s0_ref[0]

    def build_k(c, carry):
        row = pl.multiple_of(c * H_WKV, H_WKV)
        for o in range(5):
            slab = [k_srcs[o][s][j, pl.ds(row, H_WKV), :] for s, j in seqs]
            k_scr[o][pl.ds(c, tc, stride=kp), :] = jnp.concatenate(slab * vs, axis=0).T
        return carry

    lax.fori_loop(0, HS_WKV, build_k, 0, unroll=4)

    def build_v(vl, carry):
        slab = []
        for vsi in range(vs):
            row = pl.multiple_of((vsi * vl_n + vl) * H_WKV, H_WKV)
            slab += [v_srcs[s][j, pl.ds(row, H_WKV), :] for s, j in seqs]
        v_scr[pl.ds(vl, tc, stride=vp), :] = jnp.concatenate(slab, axis=0).T
        return carry

    lax.fori_loop(0, vl_n, build_v, 0, unroll=4)

    def step(s, carry):
        t = jnp.where(backward, tc - 1 - s, s)
        krow = pl.multiple_of(t * kp, SUBLANES)
        vrow = pl.multiple_of(t * vp, SUBLANES)
        n_g = vl_n // SUBLANES
        batch = min(n_g, 1)
        n_acc = 4 // batch

        def row(ref, k):
            return jnp.broadcast_to(ref[pl.ds(krow + k, 1), :], (SUBLANES, LANES))

        def total(parts):
            while len(parts) > 1:
                parts = [a + b for a, b in zip(parts[0::2], parts[1::2])]
            return parts[0]

        def accumulate(acc, g, k, p):
            acc[g][k % n_acc] = p if acc[g][k % n_acc] is None else acc[g][k % n_acc] + p

        for g0 in range(0, n_g, batch):
            gs = range(g0, g0 + batch)
            acc = {g: [None] * n_acc for g in gs}
            for k in range(HS_WKV):
                kk = row(kk_scr, k)
                for g in gs:
                    accumulate(acc, g, k, s_scr[g, k] * kk)
            sa = {g: total(acc[g]) for g in gs}
            vv = {g: v_scr[pl.ds(pl.multiple_of(vrow + g * SUBLANES, SUBLANES), SUBLANES), :] for g in gs}
            acc = {g: [None] * n_acc for g in gs}
            for k in range(HS_WKV):
                w, b, kx, r = row(w_scr, k), row(b_scr, k), row(k_scr_, k), row(r_scr, k)
                for g in gs:
                    sn = s_scr[g, k] * w - sa[g] * b + vv[g] * kx
                    s_scr[g, k] = sn
                    accumulate(acc, g, k, sn * r)
            for g in gs:
                y_scr[pl.ds(pl.multiple_of(vrow + g * SUBLANES, SUBLANES), SUBLANES), :] = total(acc[g])
        return carry

    lax.fori_loop(0, tc, step, 0)

    def emit_y(vl, carry):
        yt = y_scr[pl.ds(vl, tc, stride=vp), :].T
        for vsi in range(vs):
            row = pl.multiple_of((vsi * vl_n + vl) * H_WKV, H_WKV)
            for n, (s, j) in enumerate(seqs):
                lane0 = (vsi * len(seqs) + n) * H_WKV
                y_ref[0, 0, s * spb + j, pl.ds(row, H_WKV), :] = yt[lane0:lane0 + H_WKV, :]
        return carry

    lax.fori_loop(0, vl_n, emit_y, 0, unroll=4)
    sf_ref[0] = s_scr[...]


def _wkv_scan(rt, wt, kt, kkt, bt, vt, s0, *, tile0, n_seq, seq_tiles, spb):
    tc = WKV_TC
    if spb > 1:
        assert seq_tiles == 1 and n_seq % spb == 0 and tile0 % spb == 0
        nsrc, n_sb = 1, n_seq // spb
    else:
        nsrc, n_sb = n_seq, 1
    inst = nsrc * spb * H_WKV
    vs = LANES // inst
    vl_n = HS_WKV // vs
    assert vl_n % SUBLANES == 0, "value rows are processed eight at a time"
    cpt = TM // tc
    n_chunks = seq_tiles * cpt
    kp = HS_WKV + SUBLANES
    vp = vl_n + SUBLANES if ((vl_n + SUBLANES) // SUBLANES) % 2 else vl_n + 2 * SUBLANES

    def chunk(g, i):
        return jnp.where(g // n_sb == 1, n_chunks - 1 - i, i)

    def in_map(g, i, *, src, per_dir):
        ce = chunk(g, i)
        rb = (g // n_sb) if per_dir else 0
        if spb > 1:
            return (tile0 // spb + g % n_sb, rb, ce)
        return (tile0 + src * seq_tiles + ce // cpt, rb, ce % cpt)

    def out_map(g, i):
        ce = chunk(g, i)
        if spb > 1:
            return (g // n_sb, 0, g % n_sb, 0, ce)
        return (g // n_sb, ce // cpt, 0, 0, ce % cpt)

    in_specs, operands = [], []
    for arr, per_dir in ((rt, False), (wt, True), (kt, True), (kkt, False), (bt, True), (vt, False)):
        for src in range(nsrc):
            in_specs.append(pl.BlockSpec((spb, W_MIX, tc), functools.partial(in_map, src=src, per_dir=per_dir),
                                         pipeline_mode=pl.Buffered(1)))
            operands.append(arr)
    sspec = pl.BlockSpec((1, vl_n // SUBLANES, HS_WKV, SUBLANES, LANES), lambda g, i: (g, 0, 0, 0, 0))
    in_specs.append(sspec)
    out_specs = [pl.BlockSpec((1, 1, nsrc * spb, W_MIX, tc), out_map)]
    out_shape = [jax.ShapeDtypeStruct((N_DIR, seq_tiles, n_seq, W_MIX, TM), F32)]
    res = pl.pallas_call(
        functools.partial(_wkv_body, nsrc=nsrc, spb=spb, vs=vs, n_sb=n_sb, tc=tc, kp=kp, vp=vp),
        grid=(N_DIR * n_sb, n_chunks),
        in_specs=in_specs,
        out_specs=out_specs + [sspec],
        out_shape=out_shape + [jax.ShapeDtypeStruct(s0.shape, F32)],
        scratch_shapes=([pltpu.VMEM((tc * kp, LANES), F32)] * 5
                        + [pltpu.VMEM((tc * vp, LANES), F32)] * 2
                        + [pltpu.VMEM((vl_n // SUBLANES, HS_WKV, SUBLANES, LANES), F32)]),
        compiler_params=_cparams(("parallel", "arbitrary"), VMEM_LIMIT),
        name="wkv_scan",
    )(*operands, s0)
    return res[0], res[1]


def _merge_body(xc_ref, xl_ref, mod_ref, n1g_ref, n2g_ref, wg_ref, gb_ref, wbr_ref, wo_ref,
                lnxg_ref, lnxb_ref, ind_ref,
                ya_ref, yd_ref, gbg_ref, h_ref, ycf_ref, ycb_ref, ylf_ref, ylb_ref, g_ref, bon_ref,
                x1_ref, h2_ref, *, n_ctx_tiles):
    is_ctx = pl.program_id(0) < n_ctx_tiles
    x = jnp.where(is_ctx, xc_ref[...], xl_ref[...])
    m = mod_ref[0]
    ind = ind_ref[...]
    h = (_rms(x, n1g_ref[...]) * (1.0 + m[1:2, :]) + m[0:1, :]).astype(BF16)
    y_b = gbg_ref[...] * (h_ref[:, 0:W_MIX] + h_ref[:, W_MIX:2 * W_MIX])
    y =jnp.where(is_ctx, ycf_ref[0, 0, 0] + ycb_ref[0, 0, 0], ylf_ref[0, 0, 0] + ylb_ref[0, 0, 0]).T
    yc = y - _segsum(y, ind) * (1.0 / HS_WKV)
    var = _segsum(yc * yc, ind) * (1.0 / HS_WKV)
    y_c = (yc * lax.rsqrt(var + LNX_EPS) * lnxg_ref[...] + lnxb_ref[...] + bon_ref[...]) * g_ref[...]
    merged = None
    for n, yn in enumerate((ya_ref[...], y_b, y_c, yd_ref[...])):
        cs = slice(n * D_MODEL, (n + 1) * D_MODEL)
        gate = jax.nn.sigmoid(jnp.dot(h, wg_ref[:, cs], preferred_element_type=F32) + gb_ref[:, cs])
        br = jnp.dot(yn.astype(BF16), wbr_ref[n * W_MIX:(n + 1) * W_MIX, :], preferred_element_type=F32)
        merged = gate * br if merged is None else merged + gate * br
    mo = jnp.dot(merged.astype(BF16), wo_ref[...], preferred_element_type=F32)
    x1 = x + m[2:3, :] * mo
    x1_ref[...] = x1
    h2_ref[...] = (_rms(x1, n2g_ref[...]) * (1.0 + m[4:5, :]) + m[3:4, :]).astype(BF16)


def _merge(x_ctx, x_lat, mod, lw, tok_in, y_ctx, y_lat, lat_len):
    n_ctx_tok = x_ctx.shape[0]
    t = n_ctx_tok + x_lat.shape[0]
    n_ctx_tiles = n_ctx_tok // TM
    tps = lat_len // TM
    midx = functools.partial(_mod_index, tm=TM, n_ctx_tok=n_ctx_tok, lat_len=lat_len)
    wnames = ["norm1_g", "norm2_g", "w_gate", "gate_b", "w_branch", "w_out", "lnx_g", "lnx_b", "ind"]
    wts = [lw[n] for n in wnames]
    tok = lambda wd: pl.BlockSpec((TM, wd), lambda i: (i, 0))
    ya, yd, gbg, h, g, bon = tok_in
    yblock = (1, 1, 1, W_MIX, TM)

    def ctx_spec(d):
        return pl.BlockSpec(yblock, lambda i: (d, 0, jnp.minimum(i, n_ctx_tiles - 1), 0, 0))

    def lat_spec(d):
        def imap(i):
            r = jnp.maximum(i - n_ctx_tiles, 0)
            return (d, lax.rem(r, tps), lax.div(r, tps), 0, 0)
        return pl.BlockSpec(yblock, imap)

    return pl.pallas_call(
        functools.partial(_merge_body, n_ctx_tiles=n_ctx_tiles),
        grid=(t // TM,),
        in_specs=(_dual_specs(TM, n_ctx_tiles)
                  + [pl.BlockSpec((1, N_MOD, D_MODEL), lambda i: (midx(i), 0, 0))]
                  + [_const_spec(w.shape) for w in wts]
                  + [tok(W_MIX), tok(W_MIX), tok(W_MIX), tok(2 * W_MIX),
                     ctx_spec(0), ctx_spec(1), lat_spec(0), lat_spec(1), tok(W_MIX), tok(W_MIX)]),
        out_specs=[tok(D_MODEL), tok(D_MODEL)],
        out_shape=[jax.ShapeDtypeStruct((t, D_MODEL), F32), jax.ShapeDtypeStruct((t, D_MODEL), BF16)],
        compiler_params=_cparams(("parallel",), VMEM_LIMIT),
        name="merge",
    )(x_ctx, x_lat, mod, *wts, ya, yd, gbg, h, y_ctx, y_ctx, y_lat, y_lat, g, bon)


_CAND_VALID = (8, 8, 8, 5, 4, 3, 2, 2, 2, 8)


def _oddeven_pairs(n):
    def merge(lo, hi, r):
        step = r * 2
        if step < hi - lo:
            yield from merge(lo, hi, step)
            yield from merge(lo + r, hi, step)
            yield from [(i, i + r) for i in range(lo + r, hi - r, step)]
        else:
            yield (lo, lo + r)

    def sort(lo, hi):
        if hi - lo >= 1:
            mid = lo + (hi - lo) // 2
            yield from sort(lo, mid)
            yield from sort(mid + 1, hi)
            yield from merge(lo, hi, 1)

    return tuple(sort(0, n - 1))


_SORT16 = _oddeven_pairs(N_KEYS // SUBLANES)


def _route_head(qs, keys_ref):
    kio = lax.broadcasted_iota(I32, (N_KEYS, LANES), 0)
    sub = lax.broadcasted_iota(I32, (SUBLANES, LANES), 0)
    kid = lax.broadcasted_iota(I32, (PEER_TOPK, LANES), 0)
    neg = -jnp.inf

    def bc(x, r):
        return jnp.broadcast_to(x[r:r + 1, :], (SUBLANES, LANES))

    def head():
        tops = []
        for p in range(2):
            s = jnp.dot(keys_ref[p], qs[p], preferred_element_type=F32)
            cols = [s[j * SUBLANES:(j + 1) * SUBLANES, :] for j in range(N_KEYS // SUBLANES)]
            cidx = [sub + j * SUBLANES for j in range(N_KEYS // SUBLANES)]
            for a, b in _SORT16:
                take = cols[b] > cols[a]
                cols[a], cols[b] = jnp.where(take, cols[b], cols[a]), jnp.where(take, cols[a], cols[b])
                cidx[a], cidx[b] = jnp.where(take, cidx[b], cidx[a]), jnp.where(take, cidx[a], cidx[b])
            vals = jnp.zeros((PEER_TOPK, LANES), F32)
            idxs = jnp.zeros((PEER_TOPK, LANES), I32)
            for r in range(PEER_TOPK):
                v8, i8 = cols[0], cidx[0]
                for sh in (4, 2, 1):
                    vr, ir = pltpu.roll(v8, sh, 0), pltpu.roll(i8, sh, 0)
                    take = vr > v8
                    v8, i8 = jnp.where(take, vr, v8), jnp.where(take, ir, i8)
                m, ix = v8[0:1, :], i8[0:1, :]
                vals = jnp.where(kid == r, m, vals)
                idxs = jnp.where(kid == r, ix, idxs)
                popped = cidx[0] == ix
                for j in range(PEER_TOPK - 1 - r):
                    cols[j] = jnp.where(popped, cols[j + 1], cols[j])
                    cidx[j] = jnp.where(popped, cidx[j + 1], cidx[j])
            tops.append((vals, idxs))
        (a0, i0), (a1, i1) = tops
        lo, hi = slice(0, SUBLANES), slice(SUBLANES, 2 * SUBLANES)
        slabs = [bc(a0, 0) + a1[lo], bc(a0, 0) + a1[hi]]
        ci = [bc(i0, 0), bc(i0, 0)]
        cj = [i1[lo], i1[hi]]
        for r in range(1, SUBLANES):
            slabs.append(bc(a0, r) + a1[lo])
            ci.append(bc(i0, r))
            cj.append(i1[lo])
        slabs.append(a0[hi] + bc(a1, 0))
        ci.append(i0[hi])
        cj.append(bc(i1, 0))
        slabs = [jnp.where(sub < nv, sl, neg) for sl, nv in zip(slabs, _CAND_VALID)]
        ids = [a * N_KEYS + b for a, b in zip(ci, cj)]
        vals = jnp.zeros((PEER_TOPK, LANES), F32)
        esel = jnp.zeros((PEER_TOPK, LANES), I32)
        for r in range(PEER_TOPK):
            level = list(zip(slabs, ids))
            while len(level) > 1:
                nxt = []
                for (va, ea), (vb, eb) in zip(level[0::2], level[1::2]):
                    take = vb > va
                    nxt.append((jnp.where(take, vb, va), jnp.where(take, eb, ea)))
                if len(level) % 2:
                    nxt.append(level[-1])
                level = nxt
            v8, e8 = level[0]
            for sh in (4, 2, 1):
                vr, er = pltpu.roll(v8, sh, 0), pltpu.roll(e8, sh, 0)
                take = vr > v8
                v8, e8 = jnp.where(take, vr, v8), jnp.where(take, er, e8)
            m, ex = v8[0:1, :], e8[0:1, :]
            slabs = [jnp.where(eid == ex, neg, sl) for sl, eid in zip(slabs, ids)]
            vals = jnp.where(kid == r, m, vals)
            esel = jnp.where(kid == r, ex, esel)
        e = jnp.exp(vals - vals[0:1, :])
        return esel, e / jnp.sum(e, axis=0, keepdims=True)

    return head()


def _peer_body(h2_ref, h2n_ref, wqt_ref, keys_ref, u_ref, v_ref, x1_ref, mod_ref, fng_ref,
               oc_ref, ol_ref,
               q_scr, e_scr, g_scr, et_scr, gt_scr, gs_scr, acc_scr,
               *, rows, pitch, units, n_ctx_tiles, final):
    m = pl.program_id(0)
    e = pl.program_id(1)
    tm = h2_ref.shape[0]
    n_chunks = tm // LANES
    nsel = PEER_HEADS * PEER_TOPK
    slot = lax.rem(m, 2)

    def project_queries(src_ref):
        q = lax.dot_general(wqt_ref[...], src_ref[...], _NT, preferred_element_type=F32).astype(BF16)
        for c in range(n_chunks):
            q_scr[c] = q[:, c * LANES:(c + 1) * LANES]

    def route_unit(u, dst):
        c = u // PEER_HEADS
        h = lax.rem(u, PEER_HEADS)
        qs = [q_scr[c, pl.ds(pl.multiple_of(h * (2 * N_KEYS) + p * N_KEYS, N_KEYS), N_KEYS), :]
              for p in range(2)]
        esel, gates = _route_head(qs, keys_ref)
        row = pl.multiple_of(h * PEER_TOPK, PEER_TOPK)
        e_scr[dst, c, pl.ds(row, PEER_TOPK), :] = esel
        g_scr[dst, c, pl.ds(row, PEER_TOPK), :] = gates

    @pl.when(jnp.logical_and(e == 0, m == 0))
    def _first_tile_routing():
        project_queries(h2_ref)

        def unit(u, c):
            route_unit(u, 0)
            return c

        lax.fori_loop(0, n_chunks * PEER_HEADS, unit, 0)

    @pl.when(e == 0)
    def _build():
        for c in range(n_chunks):
            et_scr[c * LANES:(c + 1) * LANES, :] = e_scr[slot, c].T
            gt_scr[c * LANES:(c + 1) * LANES, :] = g_scr[slot, c].T
        kio = lax.broadcasted_iota(I32, (N_KEYS, nsel), 0)

        def tok(t, c):
            erow = et_scr[pl.ds(t, 1), :]
            grow = gt_scr[pl.ds(t, 1), :]
            at = jnp.where(kio == (erow >> 7), grow, 0.0).astype(BF16)
            bt = jnp.where(kio == (erow & (N_KEYS - 1)), 1.0, 0.0).astype(BF16)
            gt = lax.dot_general(at, bt, _NT, preferred_element_type=F32)
            hi = pltpu.bitcast(gt[0:rows, :], jnp.uint32) & jnp.uint32(0xFFFF0000)
            lo = pltpu.bitcast(gt[rows:2 * rows, :], jnp.uint32) >> 16
            gs_scr[pl.ds(pl.multiple_of(t * pitch, SUBLANES), rows), :] = hi | lo
            return c

        lax.fori_loop(0, tm, tok, 0, unroll=16)
        acc_scr[...] = jnp.zeros_like(acc_scr)
        project_queries(h2n_ref)

    for k in range(units):
        route_unit(e * units + k, 1 - slot)

    per_sub = PEER_SUB // N_KEYS
    per_step = u_ref.shape[0] // N_KEYS
    steps_per_half = rows // per_step
    row0 = lax.rem(e, steps_per_half) * per_step
    shift = jnp.where(e < steps_per_half, 0, 16).astype(jnp.uint32)
    h2 = h2_ref[...]
    total = None
    for sb in range(per_step // per_sub):
        es = slice(sb * PEER_SUB, (sb + 1) * PEER_SUB)
        hmat = lax.dot_general(h2, u_ref[es, :], _NT, preferred_element_type=F32)
        words = jnp.concatenate(
            [gs_scr[pl.ds(row0 + sb * per_sub + ii, tm, stride=pitch), :] for ii in range(per_sub)], axis=1)
        gm = pltpu.bitcast((words << shift) & jnp.uint32(0xFFFF0000), F32)
        act = jax.nn.gelu(hmat.astype(BF16)) * gm.astype(BF16)
        part = jnp.dot(act, v_ref[es, :], preferred_element_type=F32)
        total = part if total is None else total + part
    acc_scr[...] += total

    def result():
        x2 = x1_ref[...] + mod_ref[0][5:6, :] * acc_scr[...]
        return _rms(x2, fng_ref[...]) if final else x2

    last = e == pl.num_programs(1) - 1

    @pl.when(jnp.logical_and(last, m < n_ctx_tiles))
    def _out_ctx():
        oc_ref[...] = result()

    @pl.when(jnp.logical_and(last, m >= n_ctx_tiles))
    def _out_lat():
        ol_ref[...] = result()


def _peer(h2, wqt, keys, u, v, layer, x1, mod, fng, n_ctx_tok, lat_len, final):
    t = h2.shape[0]
    nsel = PEER_HEADS * PEER_TOPK
    rows = N_KEYS // 2
    pitch = rows + SUBLANES
    n_e = (N_KEYS * N_KEYS) // PEER_EB
    n_m = t // TM_PEER
    n_chunks = TM_PEER // LANES
    units = (n_chunks * PEER_HEADS) // n_e
    assert units * n_e == n_chunks * PEER_HEADS
    midx = functools.partial(_mod_index, tm=TM_PEER, n_ctx_tok=n_ctx_tok, lat_len=lat_len)
    tok = lambda wd: pl.BlockSpec((TM_PEER, wd), lambda m, e: (m, 0))
    nxt = pl.BlockSpec((TM_PEER, D_MODEL), lambda m, e: (jnp.minimum(m + 1, n_m - 1), 0),
                       pipeline_mode=pl.Buffered(1))
    espec = pl.BlockSpec((None, PEER_EB, D_MODEL), lambda m, e: (layer, e, 0))
    single = dict(pipeline_mode=pl.Buffered(1))
    return pl.pallas_call(
        functools.partial(_peer_body, rows=rows, pitch=pitch, units=units,
                          n_ctx_tiles=n_ctx_tok // TM_PEER, final=final),
        grid=(n_m, n_e),
        in_specs=[tok(D_MODEL), nxt,
                  pl.BlockSpec(wqt.shape, lambda m, e: (0, 0), **single),
                  pl.BlockSpec(keys.shape, lambda m, e: (0, 0, 0), **single),
                  espec, espec,
                  pl.BlockSpec((TM_PEER, D_MODEL), lambda m, e: (m, 0), **single),
                  pl.BlockSpec((1, N_MOD, D_MODEL), lambda m, e: (midx(m), 0, 0)),
                  _const_spec((1, D_MODEL))],
        out_specs=_dual_specs(TM_PEER, n_ctx_tok // TM_PEER),
        out_shape=[jax.ShapeDtypeStruct((n_ctx_tok, D_MODEL), F32),
                   jax.ShapeDtypeStruct((t - n_ctx_tok, D_MODEL), F32)],
        scratch_shapes=[pltpu.VMEM((n_chunks, wqt.shape[0], LANES), BF16),
                        pltpu.VMEM((2, n_chunks, nsel, LANES), I32),
                        pltpu.VMEM((2, n_chunks, nsel, LANES), F32),
                        pltpu.VMEM((TM_PEER, nsel), I32),
                        pltpu.VMEM((TM_PEER, nsel), F32),
                        pltpu.VMEM((TM_PEER * pitch, N_KEYS), jnp.uint32),
                        pltpu.VMEM((TM_PEER, D_MODEL), F32)],
        compiler_params=_cparams(("arbitrary", "arbitrary"), VMEM_LIMIT),
        name="peer",
    )(h2, h2, wqt, keys, u, v, x1, mod, fng)


def _wkv_state_in(s, vs):
    n = s.shape[0]
    ng = HS_WKV // vs // SUBLANES
    s = s.reshape(n, N_DIR, H_WKV, vs, ng, SUBLANES, HS_WKV).transpose(1, 4, 6, 5, 3, 0, 2)
    return s.reshape(N_DIR, ng, HS_WKV, SUBLANES, vs * n * H_WKV)


def _wkv_state_out(s, n_sb, spb, vs):
    ng = HS_WKV // vs // SUBLANES
    s = s.reshape(N_DIR, n_sb, ng, HS_WKV, SUBLANES, vs, spb, H_WKV).transpose(1, 6, 0, 7, 5, 2, 4, 3)
    return s.reshape(n_sb * spb, N_DIR, H_WKV, HS_WKV, HS_WKV)


def _layer_weights(i, prm):
    eye_h = jnp.eye(H_LRU, dtype=F32)
    eye_d = jnp.eye(N_DIR, dtype=F32)
    perm = _WKV_PERM

    def lru_bd(wt):
        return jnp.einsum("dhij,hg->hidgj", wt, eye_h).reshape(W_MIX, N_DIR * W_MIX)

    def lora_bd(wt):
        r = wt.shape[1]
        return jnp.einsum("drc,de->drec", wt, eye_d).reshape(N_DIR * r, N_DIR * W_MIX)

    w_in = prm["w_in"][i]
    pad = jnp.zeros((D_MODEL, Z_COLS - 5504), F32)
    rkv = [w_in[:, 2560 + j * W_MIX:2560 + (j + 1) * W_MIX][:, perm] for j in range(3)]
    w_in_perm = jnp.concatenate(
        [w_in[:, 0:1536]] + rkv + [w_in[:, 1536:2560], w_in[:, 4480:5504], w_in[:, 4096:4480], pad],
        axis=1).astype(BF16)
    row = lambda x: x.reshape(1, -1).astype(F32)
    head_of = np.arange(W_MIX) % H_WKV
    w_branch = prm["w_branch"][i]
    w_branch = jnp.concatenate([w_branch[0], w_branch[1], w_branch[2][perm, :], w_branch[3]], axis=0)
    return {
        "w_in": w_in_perm,
        "w_gate": w_in[:, 5504:].astype(BF16),
        "norm1_g": row(prm["norm1_g"][i]),
        "norm2_g": row(prm["norm2_g"][i]),
        "conv_a_w": prm["conv_a_w"][i],
        "conv_b_w": prm["conv_b_w"][i],
        "conv_b_b": row(prm["conv_b_b"][i]),
        "lru_w": jnp.concatenate([lru_bd(prm["lru_wa"][i]), lru_bd(prm["lru_wx"][i])], axis=1).astype(BF16),
        "lru_b": jnp.concatenate([row(prm["lru_ba"][i]), row(prm["lru_bx"][i])], axis=1),
        "lru_lam": row(prm["lru_lambda"][i]),
        "w0": row(prm["rwkv_w0"][i][:, perm]),
        "w2": lora_bd(prm["rwkv_w2"][i][:, :, perm]).astype(BF16),
        "a0": row(prm["rwkv_a0"][i][:, perm]),
        "a2": lora_bd(prm["rwkv_a2"][i][:, :, perm]).astype(BF16),
        "g2": prm["rwkv_g2"][i][:, perm].astype(BF16),
        "kk": row(prm["rwkv_kk"][i][perm]),
        "ka": row(prm["rwkv_ka"][i][perm]),
        "rk": row(prm["rwkv_rk"][i].reshape(W_MIX)[perm]),
        "lnx_g": row(prm["lnx_g"][i][perm]),
        "lnx_b": row(prm["lnx_b"][i][perm]),
        "sg_ln_g": row(prm["sg_ln_g"][i]),
        "sg_ln_b": row(prm["sg_ln_b"][i]),
        "sg_ws": prm["sg_ws"][i].astype(BF16),
        "sg_bst": prm["sg_bs"][i].T,
        "gate_b": row(prm["gate_b"][i]),
        "w_branch": w_branch.astype(BF16),
        "w_out": prm["w_out"][i].astype(BF16),
        "wq_t": prm["peer_wq"][i].T.astype(BF16),
        "keys": prm["peer_keys"][i].astype(BF16),
        "ind": jnp.asarray(head_of[:, None] == head_of[None, :], BF16),
    }


def kernel(x_prompt, x_sample, state_lru, state_wkv, c, c_ctx, norm1_g, norm2_g, w_mod, b_mod, w_in, conv_a_w, conv_b_w, conv_b_b, lru_wa, lru_ba, lru_wx, lru_bx, lru_lambda, rwkv_w0, rwkv_w2, rwkv_a0, rwkv_a2, rwkv_g2, rwkv_kk, rwkv_ka, rwkv_rk, lnx_g, lnx_b, sg_ln_g, sg_ln_b, sg_ws, sg_bs, gate_b, w_branch, w_out, peer_wq, peer_keys, peer_u, peer_v, final_norm_g):
    prm = dict(norm1_g=norm1_g, norm2_g=norm2_g, w_in=w_in, conv_a_w=conv_a_w, conv_b_w=conv_b_w,
               conv_b_b=conv_b_b, lru_wa=lru_wa, lru_ba=lru_ba, lru_wx=lru_wx, lru_bx=lru_bx,
               lru_lambda=lru_lambda, rwkv_w0=rwkv_w0, rwkv_w2=rwkv_w2, rwkv_a0=rwkv_a0, rwkv_a2=rwkv_a2,
               rwkv_g2=rwkv_g2, rwkv_kk=rwkv_kk, rwkv_ka=rwkv_ka, rwkv_rk=rwkv_rk, lnx_g=lnx_g,
               lnx_b=lnx_b, sg_ln_g=sg_ln_g, sg_ln_b=sg_ln_b, sg_ws=sg_ws, sg_bs=sg_bs, gate_b=gate_b,
               w_branch=w_branch, w_out=w_out, peer_wq=peer_wq, peer_keys=peer_keys, peer_u=peer_u,
               peer_v=peer_v)
    bc, lc, _ = x_prompt.shape
    bl, ll, _ = x_sample.shape
    depth = w_mod.shape[0]
    n_ctx_tok = bc * lc
    n_ctx_tiles = n_ctx_tok // TM
    lat_tiles = ll // TM
    ctx_spb = min(WKV_CTX_SPB, bc)
    lru_spb = ll // lc
    assert lc == TM and ll % TM_PEER == 0 and n_ctx_tok % TM_PEER == 0 and bl + 1 <= SUBLANES
    assert ll % GRID_W == 0 and TM % GRID_W == 0 and bc % ctx_spb == 0 and n_ctx_tok % ll == 0
    assert LANES % (ctx_spb * H_WKV) == 0 and LANES % (bl * H_WKV) == 0

    cond = jnp.zeros((SUBLANES, D_MODEL), F32).at[0].set(c_ctx).at[1:1 + bl].set(c)
    mods = _modulation(cond, w_mod, b_mod).reshape(depth, SUBLANES, N_MOD, D_MODEL)
    fng = final_norm_g.reshape(1, D_MODEL)
    x_ctx = x_prompt.reshape(n_ctx_tok, D_MODEL)
    x_lat = x_sample.reshape(bl * ll, D_MODEL)
    ctx_vs = LANES // (ctx_spb * H_WKV)
    lat_vs = LANES // (bl * H_WKV)
    n_sb = bc // ctx_spb
    wkv_zero = jnp.zeros((N_DIR * n_sb, HS_WKV // ctx_vs // SUBLANES, HS_WKV, SUBLANES, LANES), F32)
    lru_zero = jnp.zeros((n_ctx_tok // ll, lru_spb, N_DIR * W_MIX), F32)
    u_all = peer_u.astype(BF16)
    v_all = peer_v.astype(BF16)
    new_lru, new_wkv = [], []
    pnames = ["ya", "yd", "gbg", "la", "lu", "g", "bon", "rt", "vt", "kkt", "wt", "kt", "bt"]
    for i in range(depth):
        lw = _layer_weights(i, prm)
        mod = mods[i]
        p = dict(zip(pnames, _prep(x_ctx, x_lat, mod, lw, ll)))

        lat_h0 = jnp.zeros((bl, lru_spb, N_DIR * W_MIX), F32).at[:, 0].set(
            state_lru[:, i].astype(F32).reshape(bl, N_DIR * W_MIX))
        h, lru_s = _lru_scan(p["la"], p["lu"], jnp.concatenate([lru_zero, lat_h0], axis=0),
                             rows=ll, n_ctx_blocks=n_ctx_tok // ll, ctx_cfg=(lru_spb, lc), lat_cfg=(1, ll))
        new_lru.append(lru_s[:n_ctx_tok // ll].reshape(bc, N_DIR, W_MIX))

        wkv_in = [p[n] for n in ("rt", "wt", "kt", "kkt", "bt", "vt")]
        y_c, s_c = _wkv_scan(*wkv_in, wkv_zero, tile0=0, n_seq=bc, seq_tiles=1, spb=ctx_spb)
        y_l, _ = _wkv_scan(*wkv_in, _wkv_state_in(state_wkv[:, i].astype(F32), lat_vs),
                           tile0=n_ctx_tiles, n_seq=bl, seq_tiles=lat_tiles, spb=1)
        new_wkv.append(_wkv_state_out(s_c, n_sb, ctx_spb, ctx_vs))

        tok_in = [p["ya"], p["yd"], p["gbg"], h, p["g"], p["bon"]]
        x1, h2 = _merge(x_ctx, x_lat, mod, lw, tok_in, y_c, y_l, ll)
        x_ctx, x_lat = _peer(h2, lw["wq_t"], lw["keys"], u_all, v_all, i, x1, mod, fng,
                             n_ctx_tok, ll, final=(i == depth - 1))
    y_prompt = x_ctx.reshape(bc, lc, D_MODEL)
    y_sample = x_lat.reshape(bl, ll, D_MODEL)
    return (y_prompt, y_sample, jnp.stack(new_lru, axis=1), jnp.stack(new_wkv, axis=1))
```

```python
import functools

import numpy as np
import jax
import jax.numpy as jnp
from jax import lax
from jax.experimental import pallas as pl
from jax.experimental.pallas import tpu as pltpu

F32 = jnp.float32
BF16 = jnp.bfloat16
I32 = jnp.int32

D_MODEL = 1024
W_MIX = 512
N_DIR = 2
N_BRANCH = 4
H_WKV = 8
HS_WKV = 64
H_LRU = 8
HB_LRU = 64
LORA_W = 64
LORA_A = 64
LORA_G = 128
GRID_W = 64
CHUNK = 128
G_SG = 4
N_KEYS = 128
PEER_HEADS = 8
PEER_TOPK = 16
N_MOD = 6
EPS = 1e-6
LNX_EPS = 64e-5
LRU_C = 8.0

LANES = 128
SUBLANES = 8
TM = 256
TM_PEER = 512
PEER_EB = 1024
PEER_SUB = 512
Z_COLS = 5632
WKV_TC = LANES
WKV_CTX_SPB = 8
VMEM_LIMIT = 56 * 1024 * 1024

_NT = (((1,), (1,)), ((), ()))


def _cparams(sem, vmem=None):
    return pltpu.CompilerParams(dimension_semantics=sem, vmem_limit_bytes=vmem)


def _const_spec(shape):
    nd = len(shape)
    return pl.BlockSpec(shape, lambda *_: (0,) * nd)


def _softplus(x):
    return jnp.maximum(x, 0.0) + jnp.log1p(jnp.exp(-jnp.abs(x)))


def _rms(x, g):
    return x * lax.rsqrt(jnp.mean(x * x, axis=-1, keepdims=True) + EPS) * g


def _segsum(x, ind):
    hi = x.astype(BF16)
    lo = (x - hi.astype(F32)).astype(BF16)
    return (jnp.dot(hi, ind, preferred_element_type=F32)
            + jnp.dot(lo, ind, preferred_element_type=F32))


def _mod_index(i, tm, n_ctx_tok, lat_len):
    n_ctx_tiles = n_ctx_tok // tm
    tiles_per_seq = lat_len // tm
    return jnp.where(i < n_ctx_tiles, 0, 1 + lax.div(i - n_ctx_tiles, tiles_per_seq))


def _mod_body(s_ref, w_ref, b_ref, o_ref):
    s = s_ref[...]
    s = s * jax.nn.sigmoid(s)
    o_ref[0] = jnp.dot(s.astype(BF16), w_ref[0].astype(BF16), preferred_element_type=F32) + b_ref[0]


def _modulation(cond, w_mod, b_mod):
    depth = w_mod.shape[0]
    n = w_mod.shape[2]
    tn = 1536
    return pl.pallas_call(
        _mod_body,
        grid=(depth, n // tn),
        in_specs=[_const_spec((SUBLANES, D_MODEL)),
                  pl.BlockSpec((1, D_MODEL, tn), lambda l, j: (l, 0, j)),
                  pl.BlockSpec((1, 1, tn), lambda l, j: (l, 0, j))],
        out_specs=pl.BlockSpec((1, SUBLANES, tn), lambda l, j: (l, 0, j)),
        out_shape=jax.ShapeDtypeStruct((depth, SUBLANES, n), F32),
        compiler_params=_cparams(("parallel", "parallel"), VMEM_LIMIT),
        name="modulation",
    )(cond, w_mod, b_mod.reshape(depth, 1, n))


def _prep_body(xc_ref, xl_ref, xp_ref, xn_ref, mod_ref, n1g_ref, win_ref,
               caw_ref, cbw_ref, cbb_ref, lruw_ref, lrub_ref, lam_ref,
               w0_ref, w2_ref, a0_ref, a2_ref, g2_ref, kkw_ref, ka_ref, rk_ref,
               lng_ref, lnb_ref, ws_ref, bst_ref, ind_ref,
               ya_ref, yd_ref, gbg_ref, la_ref, lu_ref, g_ref, bon_ref,
               rt_ref, vt_ref, kkt_ref, wt_ref, kt_ref, bt_ref,
               *, n_ctx_tiles, tiles_per_seq):
    i = pl.program_id(0)
    is_ctx = i < n_ctx_tiles
    t = lax.broadcasted_iota(I32, (TM, 1), 0)
    ind = ind_ref[...]
    m = mod_ref[0]

    def modulated(xv):
        return (_rms(xv, n1g_ref[...]) * (1.0 + m[1:2, :]) + m[0:1, :]).astype(BF16)

    def project(hv, lo, hi):
        return jnp.dot(hv, win_ref[:, lo:hi], preferred_element_type=F32)

    h = modulated(jnp.where(is_ctx, xc_ref[...], xl_ref[...]))
    za = project(h, 0, 1536)
    zc = project(h, 1536, 3072)
    zb = project(h, 3072, 4096)
    zd = project(h, 4096, 5120)
    zl = project(h, 5120, 5120 + 2 * LORA_W + 2 * LORA_A + LORA_G)
    halo_prev = project(modulated(xp_ref[...]), 3072 + W_MIX, 4096)
    halo_next = project(modulated(xn_ref[...]), 3072 + W_MIX, 4096)

    pm = jnp.where(is_ctx, TM - 1, GRID_W - 1)
    pos = t & pm
    a_b = za[:, 0:W_MIX]
    ac = za[:, W_MIX:2 * W_MIX] * za[:, 2 * W_MIX:3 * W_MIX]
    up = jnp.where(pos == 0, 0.0, pltpu.roll(ac, 1, 0))
    dn = jnp.where(pos == pm, 0.0, pltpu.roll(ac, TM - 1, 0))
    ya_ref[...] = a_b * (caw_ref[0:1, :] * up + caw_ref[1:2, :] * ac + caw_ref[2:3, :] * dn)

    seq_tile = lax.rem(jnp.maximum(i - n_ctx_tiles, 0), tiles_per_seq)
    first = jnp.logical_or(is_ctx, seq_tile == 0)
    last = jnp.logical_or(is_ctx, seq_tile == tiles_per_seq - 1)
    prev = jnp.where(first, 0.0, halo_prev[SUBLANES - 1:SUBLANES, :])
    nxt0 = jnp.where(last, 0.0, halo_next[0:1, :])
    nxt1 = jnp.where(last, 0.0, halo_next[1:2, :])
    bx = zb[:, W_MIX:2 * W_MIX]
    m1 = jnp.where(t == 0, prev, pltpu.roll(bx, 1, 0))
    p1 = jnp.where(t == TM - 1, nxt0, pltpu.roll(bx, TM - 1, 0))
    p2 = jnp.where(t == TM - 2, nxt0, jnp.where(t == TM - 1, nxt1, pltpu.roll(bx, TM - 2, 0)))
    xb = (cbw_ref[0:1, :] * m1 + cbw_ref[1:2, :] * bx + cbw_ref[2:3, :] * p1
          + cbw_ref[3:4, :] * p2 + cbb_ref[...])
    gates = jnp.dot(xb.astype(BF16), lruw_ref[...], preferred_element_type=F32) + lrub_ref[...]
    rg = jax.nn.sigmoid(gates[:, 0:2 * W_MIX])
    ig = jax.nn.sigmoid(gates[:, 2 * W_MIX:4 * W_MIX])
    log_a = -LRU_C * rg * _softplus(-lam_ref[...])
    xb2 = jnp.concatenate([xb, xb], axis=1)
    la_ref[...] = jnp.exp(log_a)
    lu_ref[...] = jnp.sqrt(jnp.tanh(-log_a) * (jnp.exp(2.0 * log_a) + 1.0)) * (ig * xb2)
    gbg_ref[...] = jax.nn.gelu(zb[:, 0:W_MIX])

    zr = zc[:, 0:W_MIX]
    zk = zc[:, W_MIX:2 * W_MIX]
    zv = zc[:, 2 * W_MIX:3 * W_MIX]
    zwd = zl[:, 0:2 * LORA_W]
    zad = zl[:, 2 * LORA_W:2 * LORA_W + 2 * LORA_A]
    zgd = zl[:, 2 * LORA_W + 2 * LORA_A:2 * LORA_W + 2 * LORA_A + LORA_G]
    wlin = w0_ref[...] + jnp.dot(jnp.tanh(zwd).astype(BF16), w2_ref[...], preferred_element_type=F32)
    wt_ref[0] = jnp.exp(-jnp.exp(-_softplus(-wlin) - 0.5)).T
    av = jax.nn.sigmoid(a0_ref[...] + jnp.dot(zad.astype(BF16), a2_ref[...], preferred_element_type=F32))
    g_ref[...] = jnp.dot(jax.nn.sigmoid(zgd).astype(BF16), g2_ref[...], preferred_element_type=F32)
    kkr = zk * kkw_ref[...]
    kkn = kkr / jnp.maximum(jnp.sqrt(_segsum(kkr * kkr, ind)), 1e-12)
    zk2 = jnp.concatenate([zk, zk], axis=1)
    ka2 = jnp.concatenate([ka_ref[...], ka_ref[...]], axis=1)
    kd = zk2 * (1.0 + (av - 1.0) * ka2)
    kt_ref[0] = kd.T
    bt_ref[0] = (jnp.concatenate([kkn, kkn], axis=1) * av).T
    rt_ref[0] = zr.T
    vt_ref[0] = zv.T
    kkt_ref[0] = kkn.T
    bon_ref[...] = _segsum(zr * (kd[:, 0:W_MIX] + kd[:, W_MIX:2 * W_MIX]) * rk_ref[...], ind) * zv

    zg = jax.nn.gelu(zd)
    u = zg[:, 0:W_MIX]
    vv = zg[:, W_MIX:2 * W_MIX]
    vc = vv - jnp.mean(vv, axis=-1, keepdims=True)
    vn = vc * lax.rsqrt(jnp.mean(vc * vc, axis=-1, keepdims=True) + 1e-5) * lng_ref[...] + lnb_ref[...]
    for c in range(TM // CHUNK):
        rs = slice(c * CHUNK, (c + 1) * CHUNK)
        for gi in range(G_SG):
            cs = slice(gi * LANES, (gi + 1) * LANES)
            s = jnp.dot(ws_ref[gi], vn[rs, cs].astype(BF16), preferred_element_type=F32)
            yd_ref[rs, cs] = u[rs, cs] * (s + bst_ref[:, gi:gi + 1])


def _dual_specs(rows, n_ctx_blocks, **kw):
    return [pl.BlockSpec((rows, D_MODEL), lambda i, *_: (jnp.minimum(i, n_ctx_blocks - 1), 0), **kw),
            pl.BlockSpec((rows, D_MODEL), lambda i, *_: (jnp.maximum(i - n_ctx_blocks, 0), 0), **kw)]


def _prep(x_ctx, x_lat, mod, lw, lat_len):
    n_ctx_tok = x_ctx.shape[0]
    t = n_ctx_tok + x_lat.shape[0]
    n_tiles = t // TM
    n_ctx_tiles = n_ctx_tok // TM
    tiles_per_seq = lat_len // TM
    rows8 = TM // SUBLANES
    last_blk = x_lat.shape[0] // SUBLANES - 1
    midx = functools.partial(_mod_index, tm=TM, n_ctx_tok=n_ctx_tok, lat_len=lat_len)

    def lat_blk8(i, off):
        return (jnp.clip((i - n_ctx_tiles) * rows8 + off, 0, last_blk), 0)

    x_specs = _dual_specs(TM, n_ctx_tiles) + [
        pl.BlockSpec((SUBLANES, D_MODEL), lambda i: lat_blk8(i, -1)),
        pl.BlockSpec((SUBLANES, D_MODEL), lambda i: lat_blk8(i, rows8)),
        pl.BlockSpec((1, N_MOD, D_MODEL), lambda i: (midx(i), 0, 0)),
        _const_spec((1, D_MODEL)),
        pl.BlockSpec(lw["w_in"].shape, lambda i: (0, 0), pipeline_mode=pl.Buffered(1)),
    ]
    wnames = ["conv_a_w", "conv_b_w", "conv_b_b", "lru_w", "lru_b", "lru_lam", "w0", "w2", "a0", "a2",
              "g2", "kk", "ka", "rk", "sg_ln_g", "sg_ln_b", "sg_ws", "sg_bst", "ind"]
    wts = [lw[n] for n in wnames]
    w_specs = [_const_spec(w.shape) for w in wts]
    widths = [W_MIX, W_MIX, W_MIX, 2 * W_MIX, 2 * W_MIX, W_MIX, W_MIX]
    t_rows = [W_MIX, W_MIX, W_MIX, 2 * W_MIX, 2 * W_MIX, 2 * W_MIX]
    out_specs = ([pl.BlockSpec((TM, wd), lambda i: (i, 0)) for wd in widths]
                 + [pl.BlockSpec((1, r, TM), lambda i: (i, 0, 0)) for r in t_rows])
    out_shape = ([jax.ShapeDtypeStruct((t, wd), F32) for wd in widths]
                 + [jax.ShapeDtypeStruct((n_tiles, r, TM), F32) for r in t_rows])
    return pl.pallas_call(
        functools.partial(_prep_body, n_ctx_tiles=n_ctx_tiles, tiles_per_seq=tiles_per_seq),
        grid=(n_tiles,),
        in_specs=x_specs + w_specs,
        out_specs=out_specs,
        out_shape=out_shape,
        compiler_params=_cparams(("parallel",), VMEM_LIMIT),
        name="branch_prep",
    )(x_ctx, x_lat, x_lat, x_lat, mod, lw["norm1_g"], lw["w_in"], *wts)


def _lru_body(a_ref, u_ref, h0_ref, h_ref, hf_ref, *, n_ctx_blocks, ctx_cfg, lat_cfg):
    fw, bw = slice(0, W_MIX), slice(W_MIX, 2 * W_MIX)

    def scan(nseq, l):
        def step(s, carry):
            out = []
            for j in range(nseq):
                tf = j * l + s
                tb = j * l + (l - 1 - s)
                hf = a_ref[pl.ds(tf, 1), fw] * carry[2 * j] + u_ref[pl.ds(tf, 1), fw]
                hb = a_ref[pl.ds(tb, 1), bw] * carry[2 * j + 1] + u_ref[pl.ds(tb, 1), bw]
                h_ref[pl.ds(tf, 1), fw] = hf
                h_ref[pl.ds(tb, 1), bw] = hb
                out += [hf, hb]
            return tuple(out)

        init = []
        for j in range(nseq):
            init += [h0_ref[0, j:j + 1, fw], h0_ref[0, j:j + 1, bw]]
        fin = lax.fori_loop(0, l, step, tuple(init), unroll=2)
        hf_ref[0] = h0_ref[0]
        for j in range(nseq):
            hf_ref[0, j:j + 1, fw] = fin[2 * j]
            hf_ref[0, j:j + 1, bw] = fin[2 * j + 1]

    is_ctx = pl.program_id(0) < n_ctx_blocks
    pl.when(is_ctx)(lambda: scan(*ctx_cfg))
    pl.when(jnp.logical_not(is_ctx))(lambda: scan(*lat_cfg))


def _lru_scan(a, u, h0, *, rows, n_ctx_blocks, ctx_cfg, lat_cfg):
    nb = a.shape[0] // rows
    w = a.shape[1]
    tok = pl.BlockSpec((rows, w), lambda i: (i, 0))
    st = pl.BlockSpec((1,) + h0.shape[1:], lambda i: (i, 0, 0))
    return pl.pallas_call(
        functools.partial(_lru_body, n_ctx_blocks=n_ctx_blocks, ctx_cfg=ctx_cfg, lat_cfg=lat_cfg),
        grid=(nb,),
        in_specs=[tok, tok, st],
        out_specs=[tok, st],
        out_shape=[jax.ShapeDtypeStruct(a.shape, F32), jax.ShapeDtypeStruct(h0.shape, F32)],
        compiler_params=_cparams(("parallel",), VMEM_LIMIT),
        name="lru_scan",
    )(a, u, h0)


def _wkv_body(*refs, nsrc, spb, vs, n_sb, tc, kp, vp):
    vl_n = HS_WKV // vs
    n_in = 6 * nsrc
    k_srcs = [refs[o * nsrc:(o + 1) * nsrc] for o in range(5)]
    v_srcs = refs[5 * nsrc:n_in]
    s0_ref = refs[n_in]
    y_ref = refs[n_in + 1]
    sf_ref = refs[n_in + 2]
    k_scr = refs[n_in + 3:n_in + 8]
    v_scr, y_scr, s_scr = refs[n_in + 8:n_in + 11]
    r_scr, w_scr, k_scr_, kk_scr, b_scr = k_scr
    backward = pl.program_id(0) // n_sb == 1
    seqs = [(s, j) for s in range(nsrc) for j in range(spb)]

    @pl.when(pl.program_id(1) == 0)
    def _():
        s_scr[...] = s0_ref[0]

    def build_k(c, carry):
        row = pl.multiple_of(c * H_WKV, H_WKV)
        for o in range(5):
            slab = [k_srcs[o][s][j, pl.ds(row, H_WKV), :] for s, j in seqs]
            k_scr[o][pl.ds(c, tc, stride=kp), :] = jnp.concatenate(slab * vs, axis=0).T
        return carry

    lax.fori_loop(0, HS_WKV, build_k, 0, unroll=4)

    def build_v(vl, carry):
        slab = []
        for vsi in range(vs):
            row = pl.multiple_of((vsi * vl_n + vl) * H_WKV, H_WKV)
            slab += [v_srcs[s][j, pl.ds(row, H_WKV), :] for s, j in seqs]
        v_scr[pl.ds(vl, tc, stride=vp), :] = jnp.concatenate(slab, axis=0).T
        return carry

    lax.fori_loop(0, vl_n, build_v, 0, unroll=4)

    def step(s, carry):
        t = jnp.where(backward, tc - 1 - s, s)
        krow = pl.multiple_of(t * kp, SUBLANES)
        vrow = pl.multiple_of(t * vp, SUBLANES)
        n_g = vl_n // SUBLANES
        batch = min(n_g, 2)
        n_acc = 4 // batch

        def row(ref, k):
            return jnp.broadcast_to(ref[pl.ds(krow + k, 1), :], (SUBLANES, LANES))

        def total(parts):
            while len(parts) > 1:
                parts = [a + b for a, b in zip(parts[0::2], parts[1::2])]
            return parts[0]

        def accumulate(acc, g, k, p):
            acc[g][k % n_acc] = p if acc[g][k % n_acc] is None else acc[g][k % n_acc] + p

        for g0 in range(0, n_g, batch):
            gs = range(g0, g0 + batch)
            acc = {g: [None] * n_acc for g in gs}
            for k in range(HS_WKV):
                kk = row(kk_scr, k)
                for g in gs:
                    accumulate(acc, g, k, s_scr[g, k] * kk)
            sa = {g: total(acc[g]) for g in gs}
            vv = {g: v_scr[pl.ds(pl.multiple_of(vrow + g * SUBLANES, SUBLANES), SUBLANES), :] for g in gs}
            acc = {g: [None] * n_acc for g in gs}
            for k in range(HS_WKV):
                w, b, kx, r = row(w_scr, k), row(b_scr, k), row(k_scr_, k), row(r_scr, k)
                for g in gs:
                    sn = s_scr[g, k] * w - sa[g] * b + vv[g] * kx
                    s_scr[g, k] = sn
                    accumulate(acc, g, k, sn * r)
            for g in gs:
                y_scr[pl.ds(pl.multiple_of(vrow + g * SUBLANES, SUBLANES), SUBLANES), :] = total(acc[g])
        return carry

    lax.fori_loop(0, tc, step, 0)

    def emit_y(vl, carry):
        yt = y_scr[pl.ds(vl, tc, stride=vp), :].T
        for vsi in range(vs):
            row = pl.multiple_of((vsi * vl_n + vl) * H_WKV, H_WKV)
            for n, (s, j) in enumerate(seqs):
                lane0 = (vsi * len(seqs) + n) * H_WKV
                y_ref[0, 0, s * spb + j, pl.ds(row, H_WKV), :] = yt[lane0:lane0 + H_WKV, :]
        return carry

    lax.fori_loop(0, vl_n, emit_y, 0, unroll=4)
    sf_ref[0] = s_scr[...]


def _wkv_scan(rt, wt, kt, kkt, bt, vt, s0, *, tile0, n_seq, seq_tiles, spb):
    tc = WKV_TC
    if spb > 1:
        assert seq_tiles == 1 and n_seq % spb == 0 and tile0 % spb == 0
        nsrc, n_sb = 1, n_seq // spb
    else:
        nsrc, n_sb = n_seq, 1
    inst = nsrc * spb * H_WKV
    vs = LANES // inst
    vl_n = HS_WKV // vs
    assert vl_n % SUBLANES == 0, "value rows are processed eight at a time"
    cpt = TM // tc
    n_chunks = seq_tiles * cpt
    kp = HS_WKV + SUBLANES
    vp = vl_n + SUBLANES if ((vl_n + SUBLANES) // SUBLANES) % 2 else vl_n + 2 * SUBLANES

    def chunk(g, i):
        return jnp.where(g // n_sb == 1, n_chunks - 1 - i, i)

    def in_map(g, i, *, src, per_dir):
        ce = chunk(g, i)
        rb = (g // n_sb) if per_dir else 0
        if spb > 1:
            return (tile0 // spb + g % n_sb, rb, ce)
        return (tile0 + src * seq_tiles + ce // cpt, rb, ce % cpt)

    def out_map(g, i):
        ce = chunk(g, i)
        if spb > 1:
            return (g // n_sb, 0, g % n_sb, 0, ce)
        return (g // n_sb, ce // cpt, 0, 0, ce % cpt)

    in_specs, operands = [], []
    for arr, per_dir in ((rt, False), (wt, True), (kt, True), (kkt, False), (bt, True), (vt, False)):
        for src in range(nsrc):
            in_specs.append(pl.BlockSpec((spb, W_MIX, tc), functools.partial(in_map, src=src, per_dir=per_dir),
                                         pipeline_mode=pl.Buffered(1)))
            operands.append(arr)
    sspec = pl.BlockSpec((1, vl_n // SUBLANES, HS_WKV, SUBLANES, LANES), lambda g, i: (g, 0, 0, 0, 0))
    in_specs.append(sspec)
    out_specs = [pl.BlockSpec((1, 1, nsrc * spb, W_MIX, tc), out_map)]
    out_shape = [jax.ShapeDtypeStruct((N_DIR, seq_tiles, n_seq, W_MIX, TM), F32)]
    res = pl.pallas_call(
        functools.partial(_wkv_body, nsrc=nsrc, spb=spb, vs=vs, n_sb=n_sb, tc=tc, kp=kp, vp=vp),
        grid=(N_DIR * n_sb, n_chunks),
        in_specs=in_specs,
        out_specs=out_specs + [sspec],
        out_shape=out_shape + [jax.ShapeDtypeStruct(s0.shape, F32)],
        scratch_shapes=([pltpu.VMEM((tc * kp, LANES), F32)] * 5
                        + [pltpu.VMEM((tc * vp, LANES), F32)] * 2
                        + [pltpu.VMEM((vl_n // SUBLANES, HS_WKV, SUBLANES, LANES), F32)]),
        compiler_params=_cparams(("parallel", "arbitrary"), VMEM_LIMIT),
        name="wkv_scan",
    )(*operands, s0)
    return res[0], res[1]


def _merge_body(xc_ref, xl_ref, mod_ref, n1g_ref, n2g_ref, wg_ref, gb_ref, wbr_ref, wo_ref,
                lnxg_ref, lnxb_ref, ind_ref,
                ya_ref, yd_ref, gbg_ref, h_ref, ycf_ref, ycb_ref, ylf_ref, ylb_ref, g_ref, bon_ref,
                x1_ref, h2_ref, *, n_ctx_tiles):
    is_ctx = pl.program_id(0) < n_ctx_tiles
    x = jnp.where(is_ctx, xc_ref[...], xl_ref[...])
    m = mod_ref[0]
    ind = ind_ref[...]
    h = (_rms(x, n1g_ref[...]) * (1.0 + m[1:2, :]) + m[0:1, :]).astype(BF16)
    y_b = gbg_ref[...] * (h_ref[:, 0:W_MIX] + h_ref[:, W_MIX:2 * W_MIX])
    y =jnp.where(is_ctx, ycf_ref[0, 0, 0] + ycb_ref[0, 0, 0], ylf_ref[0, 0, 0] + ylb_ref[0, 0, 0]).T
    yc = y - _segsum(y, ind) * (1.0 / HS_WKV)
    var = _segsum(yc * yc, ind) * (1.0 / HS_WKV)
    y_c = (yc * lax.rsqrt(var + LNX_EPS) * lnxg_ref[...] + lnxb_ref[...] + bon_ref[...]) * g_ref[...]
    merged = None
    for n, yn in enumerate((ya_ref[...], y_b, y_c, yd_ref[...])):
        cs = slice(n * D_MODEL, (n + 1) * D_MODEL)
        gate = jax.nn.sigmoid(jnp.dot(h, wg_ref[:, cs], preferred_element_type=F32) + gb_ref[:, cs])
        br = jnp.dot(yn.astype(BF16), wbr_ref[n * W_MIX:(n + 1) * W_MIX, :], preferred_element_type=F32)
        merged = gate * br if merged is None else merged + gate * br
    mo = jnp.dot(merged.astype(BF16), wo_ref[...], preferred_element_type=F32)
    x1 = x + m[2:3, :] * mo
    x1_ref[...] = x1
    h2_ref[...] = (_rms(x1, n2g_ref[...]) * (1.0 + m[4:5, :]) + m[3:4, :]).astype(BF16)


def _merge(x_ctx, x_lat, mod, lw, tok_in, y_ctx, y_lat, lat_len):
    n_ctx_tok = x_ctx.shape[0]
    t = n_ctx_tok + x_lat.shape[0]
    n_ctx_tiles = n_ctx_tok // TM
    tps = lat_len // TM
    midx = functools.partial(_mod_index, tm=TM, n_ctx_tok=n_ctx_tok, lat_len=lat_len)
    wnames = ["norm1_g", "norm2_g", "w_gate", "gate_b", "w_branch", "w_out", "lnx_g", "lnx_b", "ind"]
    wts = [lw[n] for n in wnames]
    tok = lambda wd: pl.BlockSpec((TM, wd), lambda i: (i, 0))
    ya, yd, gbg, h, g, bon = tok_in
    yblock = (1, 1, 1, W_MIX, TM)

    def ctx_spec(d):
        return pl.BlockSpec(yblock, lambda i: (d, 0, jnp.minimum(i, n_ctx_tiles - 1), 0, 0))

    def lat_spec(d):
        def imap(i):
            r = jnp.maximum(i - n_ctx_tiles, 0)
            return (d, lax.rem(r, tps), lax.div(r, tps), 0, 0)
        return pl.BlockSpec(yblock, imap)

    return pl.pallas_call(
        functools.partial(_merge_body, n_ctx_tiles=n_ctx_tiles),
        grid=(t // TM,),
        in_specs=(_dual_specs(TM, n_ctx_tiles)
                  + [pl.BlockSpec((1, N_MOD, D_MODEL), lambda i: (midx(i), 0, 0))]
                  + [_const_spec(w.shape) for w in wts]
                  + [tok(W_MIX), tok(W_MIX), tok(W_MIX), tok(2 * W_MIX),
                     ctx_spec(0), ctx_spec(1), lat_spec(0), lat_spec(1), tok(W_MIX), tok(W_MIX)]),
        out_specs=[tok(D_MODEL), tok(D_MODEL)],
        out_shape=[jax.ShapeDtypeStruct((t, D_MODEL), F32), jax.ShapeDtypeStruct((t, D_MODEL), BF16)],
        compiler_params=_cparams(("parallel",), VMEM_LIMIT),
        name="merge",
    )(x_ctx, x_lat, mod, *wts, ya, yd, gbg, h, y_ctx, y_ctx, y_lat, y_lat, g, bon)


_CAND_VALID = (8, 8, 8, 5, 4, 3, 2, 2, 2, 8)


def _oddeven_pairs(n):
    def merge(lo, hi, r):
        step = r * 2
        if step < hi - lo:
            yield from merge(lo, hi, step)
            yield from merge(lo + r, hi, step)
            yield from [(i, i + r) for i in range(lo + r, hi - r, step)]
        else:
            yield (lo, lo + r)

    def sort(lo, hi):
        if hi - lo >= 1:
            mid = lo + (hi - lo) // 2
            yield from sort(lo, mid)
            yield from sort(mid + 1, hi)
            yield from merge(lo, hi, 1)

    return tuple(sort(0, n - 1))


_SORT16 = _oddeven_pairs(N_KEYS // SUBLANES)


def _route_head(qs, keys_ref):
    kio = lax.broadcasted_iota(I32, (N_KEYS, LANES), 0)
    sub = lax.broadcasted_iota(I32, (SUBLANES, LANES), 0)
    kid = lax.broadcasted_iota(I32, (PEER_TOPK, LANES), 0)
    neg = -jnp.inf

    def bc(x, r):
        return jnp.broadcast_to(x[r:r + 1, :], (SUBLANES, LANES))

    def head():
        tops = []
        for p in range(2):
            s = jnp.dot(keys_ref[p], qs[p], preferred_element_type=F32)
            cols = [s[j * SUBLANES:(j + 1) * SUBLANES, :] for j in range(N_KEYS // SUBLANES)]
            cidx = [sub + j * SUBLANES for j in range(N_KEYS // SUBLANES)]
            for a, b in _SORT16:
                take = cols[b] > cols[a]
                cols[a], cols[b] = jnp.where(take, cols[b], cols[a]), jnp.where(take, cols[a], cols[b])
                cidx[a], cidx[b] = jnp.where(take, cidx[b], cidx[a]), jnp.where(take, cidx[a], cidx[b])
            vals = jnp.zeros((PEER_TOPK, LANES), F32)
            idxs = jnp.zeros((PEER_TOPK, LANES), I32)
            for r in range(PEER_TOPK):
                v8, i8 = cols[0], cidx[0]
                for sh in (4, 2, 1):
                    vr, ir = pltpu.roll(v8, sh, 0), pltpu.roll(i8, sh, 0)
                    take = vr > v8
                    v8, i8 = jnp.where(take, vr, v8), jnp.where(take, ir, i8)
                m, ix = v8[0:1, :], i8[0:1, :]
                vals = jnp.where(kid == r, m, vals)
                idxs = jnp.where(kid == r, ix, idxs)
                popped = cidx[0] == ix
                for j in range(PEER_TOPK - 1 - r):
                    cols[j] = jnp.where(popped, cols[j + 1], cols[j])
                    cidx[j] = jnp.where(popped, cidx[j + 1], cidx[j])
            tops.append((vals, idxs))
        (a0, i0), (a1, i1) = tops
        lo, hi = slice(0, SUBLANES), slice(SUBLANES, 2 * SUBLANES)
        slabs = [bc(a0, 0) + a1[lo], bc(a0, 0) + a1[hi]]
        ci = [bc(i0, 0), bc(i0, 0)]
        cj = [i1[lo], i1[hi]]
        for r in range(1, SUBLANES):
            slabs.append(bc(a0, r) + a1[lo])
            ci.append(bc(i0, r))
            cj.append(i1[lo])
        slabs.append(a0[hi] + bc(a1, 0))
        ci.append(i0[hi])
        cj.append(bc(i1, 0))
        slabs = [jnp.where(sub < nv, sl, neg) for sl, nv in zip(slabs, _CAND_VALID)]
        ids = [a * N_KEYS + b for a, b in zip(ci, cj)]
        vals = jnp.zeros((PEER_TOPK, LANES), F32)
        esel = jnp.zeros((PEER_TOPK, LANES), I32)
        for r in range(PEER_TOPK):
            level = list(zip(slabs, ids))
            while len(level) > 1:
                nxt = []
                for (va, ea), (vb, eb) in zip(level[0::2], level[1::2]):
                    take = vb > va
                    nxt.append((jnp.where(take, vb, va), jnp.where(take, eb, ea)))
                if len(level) % 2:
                    nxt.append(level[-1])
                level = nxt
            v8, e8 = level[0]
            for sh in (4, 2, 1):
                vr, er = pltpu.roll(v8, sh, 0), pltpu.roll(e8, sh, 0)
                take = vr > v8
                v8, e8 = jnp.where(take, vr, v8), jnp.where(take, er, e8)
            m, ex = v8[0:1, :], e8[0:1, :]
            slabs = [jnp.where(eid == ex, neg, sl) for sl, eid in zip(slabs, ids)]
            vals = jnp.where(kid == r, m, vals)
            esel = jnp.where(kid == r, ex, esel)
        e = jnp.exp(vals - vals[0:1, :])
        return esel, e / jnp.sum(e, axis=0, keepdims=True)

    return head()


def _peer_body(h2_ref, h2n_ref, wqt_ref, keys_ref, u_ref, v_ref, x1_ref, mod_ref, fng_ref,
               oc_ref, ol_ref,
               q_scr, e_scr, g_scr, et_scr, gt_scr, gs_scr, acc_scr,
               *, rows, pitch, units, n_ctx_tiles, final):
    m = pl.program_id(0)
    e = pl.program_id(1)
    tm = h2_ref.shape[0]
    n_chunks = tm // LANES
    nsel = PEER_HEADS * PEER_TOPK
    slot = lax.rem(m, 2)

    def project_queries(src_ref):
        q = lax.dot_general(wqt_ref[...], src_ref[...], _NT, preferred_element_type=F32).astype(BF16)
        for c in range(n_chunks):
            q_scr[c] = q[:, c * LANES:(c + 1) * LANES]

    def route_unit(u, dst):
        c = u // PEER_HEADS
        h = lax.rem(u, PEER_HEADS)
        qs = [q_scr[c, pl.ds(pl.multiple_of(h * (2 * N_KEYS) + p * N_KEYS, N_KEYS), N_KEYS), :]
              for p in range(2)]
        esel, gates = _route_head(qs, keys_ref)
        row = pl.multiple_of(h * PEER_TOPK, PEER_TOPK)
        e_scr[dst, c, pl.ds(row, PEER_TOPK), :] = esel
        g_scr[dst, c, pl.ds(row, PEER_TOPK), :] = gates

    @pl.when(jnp.logical_and(e == 0, m == 0))
    def _first_tile_routing():
        project_queries(h2_ref)

        def unit(u, c):
            route_unit(u, 0)
            return c

        lax.fori_loop(0, n_chunks * PEER_HEADS, unit, 0)

    @pl.when(e == 0)
    def _build():
        for c in range(n_chunks):
            et_scr[c * LANES:(c + 1) * LANES, :] = e_scr[slot, c].T
            gt_scr[c * LANES:(c + 1) * LANES, :] = g_scr[slot, c].T
        kio = lax.broadcasted_iota(I32, (N_KEYS, nsel), 0)

        def tok(t, c):
            erow = et_scr[pl.ds(t, 1), :]
            grow = gt_scr[pl.ds(t, 1), :]
            at = jnp.where(kio == (erow >> 7), grow, 0.0).astype(BF16)
            bt = jnp.where(kio == (erow & (N_KEYS - 1)), 1.0, 0.0).astype(BF16)
            gt = lax.dot_general(at, bt, _NT, preferred_element_type=F32)
            hi = pltpu.bitcast(gt[0:rows, :], jnp.uint32) & jnp.uint32(0xFFFF0000)
            lo = pltpu.bitcast(gt[rows:2 * rows, :], jnp.uint32) >> 16
            gs_scr[pl.ds(pl.multiple_of(t * pitch, SUBLANES), rows), :] = hi | lo
            return c

        lax.fori_loop(0, tm, tok, 0, unroll=16)
        acc_scr[...] = jnp.zeros_like(acc_scr)
        project_queries(h2n_ref)

    for k in range(units):
        route_unit(e * units + k, 1 - slot)

    per_sub = PEER_SUB // N_KEYS
    per_step = u_ref.shape[0] // N_KEYS
    steps_per_half = rows // per_step
    row0 = lax.rem(e, steps_per_half) * per_step
    shift = jnp.where(e < steps_per_half, 0, 16).astype(jnp.uint32)
    h2 = h2_ref[...]
    total = None
    for sb in range(per_step // per_sub):
        es = slice(sb * PEER_SUB, (sb + 1) * PEER_SUB)
        hmat = lax.dot_general(h2, u_ref[es, :], _NT, preferred_element_type=F32)
        words = jnp.concatenate(
            [gs_scr[pl.ds(row0 + sb * per_sub + ii, tm, stride=pitch), :] for ii in range(per_sub)], axis=1)
        gm = pltpu.bitcast((words << shift) & jnp.uint32(0xFFFF0000), F32)
        act = jax.nn.gelu(hmat.astype(BF16)) * gm.astype(BF16)
        part = jnp.dot(act, v_ref[es, :], preferred_element_type=F32)
        total = part if total is None else total + part
    acc_scr[...] += total

    def result():
        x2 = x1_ref[...] + mod_ref[0][5:6, :] * acc_scr[...]
        return _rms(x2, fng_ref[...]) if final else x2

    last = e == pl.num_programs(1) - 1

    @pl.when(jnp.logical_and(last, m < n_ctx_tiles))
    def _out_ctx():
        oc_ref[...] = result()

    @pl.when(jnp.logical_and(last, m >= n_ctx_tiles))
    def _out_lat():
        ol_ref[...] = result()


def _peer(h2, wqt, keys, u, v, layer, x1, mod, fng, n_ctx_tok, lat_len, final):
    t = h2.shape[0]
    nsel = PEER_HEADS * PEER_TOPK
    rows = N_KEYS // 2
    pitch = rows + SUBLANES
    n_e = (N_KEYS * N_KEYS) // PEER_EB
    n_m = t // TM_PEER
    n_chunks = TM_PEER // LANES
    units = (n_chunks * PEER_HEADS) // n_e
    assert units * n_e == n_chunks * PEER_HEADS
    midx = functools.partial(_mod_index, tm=TM_PEER, n_ctx_tok=n_ctx_tok, lat_len=lat_len)
    tok = lambda wd: pl.BlockSpec((TM_PEER, wd), lambda m, e: (m, 0))
    nxt = pl.BlockSpec((TM_PEER, D_MODEL), lambda m, e: (jnp.minimum(m + 1, n_m - 1), 0),
                       pipeline_mode=pl.Buffered(1))
    espec = pl.BlockSpec((None, PEER_EB, D_MODEL), lambda m, e: (layer, e, 0))
    single = dict(pipeline_mode=pl.Buffered(1))
    return pl.pallas_call(
        functools.partial(_peer_body, rows=rows, pitch=pitch, units=units,
                          n_ctx_tiles=n_ctx_tok // TM_PEER, final=final),
        grid=(n_m, n_e),
        in_specs=[tok(D_MODEL), nxt,
                  pl.BlockSpec(wqt.shape, lambda m, e: (0, 0), **single),
                  pl.BlockSpec(keys.shape, lambda m, e: (0, 0, 0), **single),
                  espec, espec,
                  pl.BlockSpec((TM_PEER, D_MODEL), lambda m, e: (m, 0), **single),
                  pl.BlockSpec((1, N_MOD, D_MODEL), lambda m, e: (midx(m), 0, 0)),
                  _const_spec((1, D_MODEL))],
        out_specs=_dual_specs(TM_PEER, n_ctx_tok // TM_PEER),
        out_shape=[jax.ShapeDtypeStruct((n_ctx_tok, D_MODEL), F32),
                   jax.ShapeDtypeStruct((t - n_ctx_tok, D_MODEL), F32)],
        scratch_shapes=[pltpu.VMEM((n_chunks, wqt.shape[0], LANES), BF16),
                        pltpu.VMEM((2, n_chunks, nsel, LANES), I32),
                        pltpu.VMEM((2, n_chunks, nsel, LANES), F32),
                        pltpu.VMEM((TM_PEER, nsel), I32),
                        pltpu.VMEM((TM_PEER, nsel), F32),
                        pltpu.VMEM((TM_PEER * pitch, N_KEYS), jnp.uint32),
                        pltpu.VMEM((TM_PEER, D_MODEL), F32)],
        compiler_params=_cparams(("arbitrary", "arbitrary"), VMEM_LIMIT),
        name="peer",
    )(h2, h2, wqt, keys, u, v, x1, mod, fng)


def _wkv_state_in(s, vs):
    n = s.shape[0]
    ng = HS_WKV // vs // SUBLANES
    s = s.reshape(n, N_DIR, H_WKV, vs, ng, SUBLANES, HS_WKV).transpose(1, 4, 6, 5, 3, 0, 2)
    return s.reshape(N_DIR, ng, HS_WKV, SUBLANES, vs * n * H_WKV)


def _wkv_state_out(s, n_sb, spb, vs):
    ng = HS_WKV // vs // SUBLANES
    s = s.reshape(N_DIR, n_sb, ng, HS_WKV, SUBLANES, vs, spb, H_WKV).transpose(1, 6, 0, 7, 5, 2, 4, 3)
    return s.reshape(n_sb * spb, N_DIR, H_WKV, HS_WKV, HS_WKV)


def _layer_weights(i, prm):
    eye_h = jnp.eye(H_LRU, dtype=F32)
    eye_d = jnp.eye(N_DIR, dtype=F32)

    def perm(x, axis=-1):
        x = jnp.moveaxis(x, axis, -1)
        lead = x.shape[:-1]
        x = x.reshape(lead + (H_WKV, HS_WKV)).swapaxes(-1, -2).reshape(lead + (W_MIX,))
        return jnp.moveaxis(x, -1, axis)

    def lru_bd(wt):
        return jnp.einsum("dhij,hg->hidgj", wt, eye_h).reshape(W_MIX, N_DIR * W_MIX)

    def lora_bd(wt):
        r = wt.shape[1]
        return jnp.einsum("drc,de->drec", wt, eye_d).reshape(N_DIR * r, N_DIR * W_MIX)

    w_in = prm["w_in"][i]
    pad = jnp.zeros((D_MODEL, Z_COLS - 5504), F32)
    rkv = [perm(w_in[:, 2560 + j * W_MIX:2560 + (j + 1) * W_MIX]) for j in range(3)]
    w_in_perm = jnp.concatenate(
        [w_in[:, 0:1536]] + rkv + [w_in[:, 1536:2560], w_in[:, 4480:5504], w_in[:, 4096:4480], pad],
        axis=1).astype(BF16)
    row = lambda x: x.reshape(1, -1).astype(F32)
    head_of = np.arange(W_MIX) % H_WKV
    w_branch = prm["w_branch"][i]
    w_branch = jnp.concatenate([w_branch[0], w_branch[1], perm(w_branch[2], axis=0), w_branch[3]], axis=0)
    return {
        "w_in": w_in_perm,
        "w_gate": w_in[:, 5504:].astype(BF16),
        "norm1_g": row(prm["norm1_g"][i]),
        "norm2_g": row(prm["norm2_g"][i]),
        "conv_a_w": prm["conv_a_w"][i],
        "conv_b_w": prm["conv_b_w"][i],
        "conv_b_b": row(prm["conv_b_b"][i]),
        "lru_w": jnp.concatenate([lru_bd(prm["lru_wa"][i]), lru_bd(prm["lru_wx"][i])], axis=1).astype(BF16),
        "lru_b": jnp.concatenate([row(prm["lru_ba"][i]), row(prm["lru_bx"][i])], axis=1),
        "lru_lam": row(prm["lru_lambda"][i]),
        "w0": row(perm(prm["rwkv_w0"][i])),
        "w2": lora_bd(perm(prm["rwkv_w2"][i])).astype(BF16),
        "a0": row(perm(prm["rwkv_a0"][i])),
        "a2": lora_bd(perm(prm["rwkv_a2"][i])).astype(BF16),
        "g2": perm(prm["rwkv_g2"][i]).astype(BF16),
        "kk": row(perm(prm["rwkv_kk"][i])),
        "ka": row(perm(prm["rwkv_ka"][i])),
        "rk": row(perm(prm["rwkv_rk"][i].reshape(W_MIX))),
        "lnx_g": row(perm(prm["lnx_g"][i])),
        "lnx_b": row(perm(prm["lnx_b"][i])),
        "sg_ln_g": row(prm["sg_ln_g"][i]),
        "sg_ln_b": row(prm["sg_ln_b"][i]),
        "sg_ws": prm["sg_ws"][i].astype(BF16),
        "sg_bst": prm["sg_bs"][i].T,
        "gate_b": row(prm["gate_b"][i]),
        "w_branch": w_branch.astype(BF16),
        "w_out": prm["w_out"][i].astype(BF16),
        "wq_t": prm["peer_wq"][i].T.astype(BF16),
        "keys": prm["peer_keys"][i].astype(BF16),
        "ind": jnp.asarray(head_of[:, None] == head_of[None, :], BF16),
    }


def kernel(x_prompt, x_sample, state_lru, state_wkv, c, c_ctx, norm1_g, norm2_g, w_mod, b_mod, w_in, conv_a_w, conv_b_w, conv_b_b, lru_wa, lru_ba, lru_wx, lru_bx, lru_lambda, rwkv_w0, rwkv_w2, rwkv_a0, rwkv_a2, rwkv_g2, rwkv_kk, rwkv_ka, rwkv_rk, lnx_g, lnx_b, sg_ln_g, sg_ln_b, sg_ws, sg_bs, gate_b, w_branch, w_out, peer_wq, peer_keys, peer_u, peer_v, final_norm_g):
    prm = dict(norm1_g=norm1_g, norm2_g=norm2_g, w_in=w_in, conv_a_w=conv_a_w, conv_b_w=conv_b_w,
               conv_b_b=conv_b_b, lru_wa=lru_wa, lru_ba=lru_ba, lru_wx=lru_wx, lru_bx=lru_bx,
               lru_lambda=lru_lambda, rwkv_w0=rwkv_w0, rwkv_w2=rwkv_w2, rwkv_a0=rwkv_a0, rwkv_a2=rwkv_a2,
               rwkv_g2=rwkv_g2, rwkv_kk=rwkv_kk, rwkv_ka=rwkv_ka, rwkv_rk=rwkv_rk, lnx_g=lnx_g,
               lnx_b=lnx_b, sg_ln_g=sg_ln_g, sg_ln_b=sg_ln_b, sg_ws=sg_ws, sg_bs=sg_bs, gate_b=gate_b,
               w_branch=w_branch, w_out=w_out, peer_wq=peer_wq, peer_keys=peer_keys, peer_u=peer_u,
               peer_v=peer_v)
    bc, lc, _ = x_prompt.shape
    bl, ll, _ = x_sample.shape
    depth = w_mod.shape[0]
    n_ctx_tok = bc * lc
    n_ctx_tiles = n_ctx_tok // TM
    lat_tiles = ll // TM
    ctx_spb = min(WKV_CTX_SPB, bc)
    lru_spb = ll // lc
    assert lc == TM and ll % TM_PEER == 0 and n_ctx_tok % TM_PEER == 0 and bl + 1 <= SUBLANES
    assert ll % GRID_W == 0 and TM % GRID_W == 0 and bc % ctx_spb == 0 and n_ctx_tok % ll == 0
    assert LANES % (ctx_spb * H_WKV) == 0 and LANES % (bl * H_WKV) == 0

    cond = jnp.zeros((SUBLANES, D_MODEL), F32).at[0].set(c_ctx).at[1:1 + bl].set(c)
    mods = _modulation(cond, w_mod, b_mod).reshape(depth, SUBLANES, N_MOD, D_MODEL)
    fng = final_norm_g.reshape(1, D_MODEL)
    x_ctx = x_prompt.reshape(n_ctx_tok, D_MODEL)
    x_lat = x_sample.reshape(bl * ll, D_MODEL)
    ctx_vs = LANES // (ctx_spb * H_WKV)
    lat_vs = LANES // (bl * H_WKV)
    n_sb = bc // ctx_spb
    wkv_zero = jnp.zeros((N_DIR * n_sb, HS_WKV // ctx_vs // SUBLANES, HS_WKV, SUBLANES, LANES), F32)
    lru_zero = jnp.zeros((n_ctx_tok // ll, lru_spb, N_DIR * W_MIX), F32)
    u_all = peer_u.astype(BF16)
    v_all = peer_v.astype(BF16)
    new_lru, new_wkv = [], []
    pnames = ["ya", "yd", "gbg", "la", "lu", "g", "bon", "rt", "vt", "kkt", "wt", "kt", "bt"]
    for i in range(depth):
        lw = _layer_weights(i, prm)
        mod = mods[i]
        p = dict(zip(pnames, _prep(x_ctx, x_lat, mod, lw, ll)))

        lat_h0 = jnp.zeros((bl, lru_spb, N_DIR * W_MIX), F32).at[:, 0].set(
            state_lru[:, i].astype(F32).reshape(bl, N_DIR * W_MIX))
        h, lru_s = _lru_scan(p["la"], p["lu"], jnp.concatenate([lru_zero, lat_h0], axis=0),
                             rows=ll, n_ctx_blocks=n_ctx_tok // ll, ctx_cfg=(lru_spb, lc), lat_cfg=(1, ll))
        new_lru.append(lru_s[:n_ctx_tok // ll].reshape(bc, N_DIR, W_MIX))

        wkv_in = [p[n] for n in ("rt", "wt", "kt", "kkt", "bt", "vt")]
        y_c, s_c = _wkv_scan(*wkv_in, wkv_zero, tile0=0, n_seq=bc, seq_tiles=1, spb=ctx_spb)
        y_l, _ = _wkv_scan(*wkv_in, _wkv_state_in(state_wkv[:, i].astype(F32), lat_vs),
                           tile0=n_ctx_tiles, n_seq=bl, seq_tiles=lat_tiles, spb=1)
        new_wkv.append(_wkv_state_out(s_c, n_sb, ctx_spb, ctx_vs))

        tok_in = [p["ya"], p["yd"], p["gbg"], h, p["g"], p["bon"]]
        x1, h2 = _merge(x_ctx, x_lat, mod, lw, tok_in, y_c, y_l, ll)
        x_ctx, x_lat = _peer(h2, lw["wq_t"], lw["keys"], u_all, v_all, i, x1, mod, fng,
                             n_ctx_tok, ll, final=(i == depth - 1))
    y_prompt = x_ctx.reshape(bc, lc, D_MODEL)
    y_sample = x_lat.reshape(bl, ll, D_MODEL)
    return (y_prompt, y_sample, jnp.stack(new_lru, axis=1), jnp.stack(new_wkv, axis=1))
```

```python
import functools

import numpy as np
import jax
import jax.numpy as jnp
from jax import lax
from jax.experimental import pallas as pl
from jax.experimental.pallas import tpu as pltpu

F32 = jnp.float32
BF16 = jnp.bfloat16
I32 = jnp.int32

D_MODEL = 1024
W_MIX = 512
N_DIR = 2
H_WKV = 8
HS_WKV = 64
H_LRU = 8
LORA_W = 64
LORA_A = 64
LORA_G = 128
GRID_W = 64
CHUNK = 128
G_SG = 4
N_KEYS = 128
PEER_HEADS = 8
PEER_TOPK = 16
N_MOD = 6
EPS = 1e-6
LNX_EPS = 64e-5
LRU_C = 8.0

LANES = 128
SUBLANES = 8
TM = 256
TM_PEER = 512
PEER_EB = 1024
PEER_SUB = 512
Z_COLS = 5632
WKV_TC = LANES
WKV_CTX_SPB = 8
VMEM_LIMIT = 56 * 1024 * 1024

_NT = (((1,), (1,)), ((), ()))


def _cparams(sem, vmem=None):
    return pltpu.CompilerParams(dimension_semantics=sem, vmem_limit_bytes=vmem)


def _const_spec(shape):
    nd = len(shape)
    return pl.BlockSpec(shape, lambda *_: (0,) * nd)


def _softplus(x):
    return jnp.maximum(x, 0.0) + jnp.log1p(jnp.exp(-jnp.abs(x)))


def _rms(x, g):
    return x * lax.rsqrt(jnp.mean(x * x, axis=-1, keepdims=True) + EPS) * g


def _segsum(x, ind):
    hi = x.astype(BF16)
    lo = (x - hi.astype(F32)).astype(BF16)
    return (jnp.dot(hi, ind, preferred_element_type=F32)
            + jnp.dot(lo, ind, preferred_element_type=F32))


def _mod_index(i, tm, n_ctx_tok, lat_len):
    n_ctx_tiles = n_ctx_tok // tm
    tiles_per_seq = lat_len // tm
    return jnp.where(i < n_ctx_tiles, 0, 1 + lax.div(i - n_ctx_tiles, tiles_per_seq))


def _mod_body(s_ref, w_ref, b_ref, o_ref):
    s = s_ref[...]
    s = s * jax.nn.sigmoid(s)
    o_ref[0] = jnp.dot(s.astype(BF16), w_ref[0].astype(BF16), preferred_element_type=F32) + b_ref[0]


def _modulation(cond, w_mod, b_mod):
    depth = w_mod.shape[0]
    n = w_mod.shape[2]
    tn = 1536
    return pl.pallas_call(
        _mod_body,
        grid=(depth, n // tn),
        in_specs=[_const_spec((SUBLANES, D_MODEL)),
                  pl.BlockSpec((1, D_MODEL, tn), lambda l, j: (l, 0, j)),
                  pl.BlockSpec((1, 1, tn), lambda l, j: (l, 0, j))],
        out_specs=pl.BlockSpec((1, SUBLANES, tn), lambda l, j: (l, 0, j)),
        out_shape=jax.ShapeDtypeStruct((depth, SUBLANES, n), F32),
        compiler_params=_cparams(("parallel", "parallel"), VMEM_LIMIT),
        name="modulation",
    )(cond, w_mod, b_mod.reshape(depth, 1, n))


def _prep_body(xc_ref, xl_ref, xp_ref, xn_ref, mod_ref, n1g_ref, win_ref,
               caw_ref, cbw_ref, cbb_ref, lruw_ref, lrub_ref, lam_ref,
               w0_ref, w2_ref, a0_ref, a2_ref, g2_ref, kkw_ref, ka_ref, rk_ref,
               lng_ref, lnb_ref, ws_ref, bst_ref, ind_ref,
               ya_ref, yd_ref, gbg_ref, la_ref, lu_ref, g_ref, bon_ref,
               rt_ref, vt_ref, kkt_ref, wt_ref, kt_ref, bt_ref,
               *, n_ctx_tiles, tiles_per_seq):
    i = pl.program_id(0)
    is_ctx = i < n_ctx_tiles
    t = lax.broadcasted_iota(I32, (TM, 1), 0)
    ind = ind_ref[...]
    m = mod_ref[0]

    def modulated(xv):
        return (_rms(xv, n1g_ref[...]) * (1.0 + m[1:2, :]) + m[0:1, :]).astype(BF16)

    def project(hv, lo, hi):
        return jnp.dot(hv, win_ref[:, lo:hi], preferred_element_type=F32)

    h = modulated(jnp.where(is_ctx, xc_ref[...], xl_ref[...]))
    za = project(h, 0, 1536)
    zc = project(h, 1536, 3072)
    zb = project(h, 3072, 4096)
    zd = project(h, 4096, 5120)
    zl = project(h, 5120, 5120 + 2 * LORA_W + 2 * LORA_A + LORA_G)
    halo_prev = project(modulated(xp_ref[...]), 3072 + W_MIX, 4096)
    halo_next = project(modulated(xn_ref[...]), 3072 + W_MIX, 4096)

    pm = jnp.where(is_ctx, TM - 1, GRID_W - 1)
    pos = t & pm
    a_b = za[:, 0:W_MIX]
    ac = za[:, W_MIX:2 * W_MIX] * za[:, 2 * W_MIX:3 * W_MIX]
    up = jnp.where(pos == 0, 0.0, pltpu.roll(ac, 1, 0))
    dn = jnp.where(pos == pm, 0.0, pltpu.roll(ac, TM - 1, 0))
    ya_ref[...] = a_b * (caw_ref[0:1, :] * up + caw_ref[1:2, :] * ac + caw_ref[2:3, :] * dn)

    seq_tile = lax.rem(jnp.maximum(i - n_ctx_tiles, 0), tiles_per_seq)
    first = jnp.logical_or(is_ctx, seq_tile == 0)
    last = jnp.logical_or(is_ctx, seq_tile == tiles_per_seq - 1)
    prev = jnp.where(first, 0.0, halo_prev[SUBLANES - 1:SUBLANES, :])
    nxt0 = jnp.where(last, 0.0, halo_next[0:1, :])
    nxt1 = jnp.where(last, 0.0, halo_next[1:2, :])
    bx = zb[:, W_MIX:2 * W_MIX]
    m1 = jnp.where(t == 0, prev, pltpu.roll(bx, 1, 0))
    p1 = jnp.where(t == TM - 1, nxt0, pltpu.roll(bx, TM - 1, 0))
    p2 = jnp.where(t == TM - 2, nxt0, jnp.where(t == TM - 1, nxt1, pltpu.roll(bx, TM - 2, 0)))
    xb = (cbw_ref[0:1, :] * m1 + cbw_ref[1:2, :] * bx + cbw_ref[2:3, :] * p1
          + cbw_ref[3:4, :] * p2 + cbb_ref[...])
    gates = jnp.dot(xb.astype(BF16), lruw_ref[...], preferred_element_type=F32) + lrub_ref[...]
    rg = jax.nn.sigmoid(gates[:, 0:2 * W_MIX])
    ig = jax.nn.sigmoid(gates[:, 2 * W_MIX:4 * W_MIX])
    log_a = -LRU_C * rg * _softplus(-lam_ref[...])
    xb2 = jnp.concatenate([xb, xb], axis=1)
    a = jnp.exp(log_a)
    la_ref[...] = a
    lu_ref[...] = jnp.sqrt(jnp.tanh(-log_a) * (a * a + 1.0)) * (ig * xb2)
    gbg_ref[...] = jax.nn.gelu(zb[:, 0:W_MIX])

    zr = zc[:, 0:W_MIX]
    zk = zc[:, W_MIX:2 * W_MIX]
    zv = zc[:, 2 * W_MIX:3 * W_MIX]
    zwd = zl[:, 0:2 * LORA_W]
    zad = zl[:, 2 * LORA_W:2 * LORA_W + 2 * LORA_A]
    zgd = zl[:, 2 * LORA_W + 2 * LORA_A:2 * LORA_W + 2 * LORA_A + LORA_G]
    wlin = w0_ref[...] + jnp.dot(jnp.tanh(zwd).astype(BF16), w2_ref[...], preferred_element_type=F32)
    wt_ref[0] = jnp.exp(-jnp.exp(-_softplus(-wlin) - 0.5)).T
    av = jax.nn.sigmoid(a0_ref[...] + jnp.dot(zad.astype(BF16), a2_ref[...], preferred_element_type=F32))
    g_ref[...] = jnp.dot(jax.nn.sigmoid(zgd).astype(BF16), g2_ref[...], preferred_element_type=F32)
    kkr = zk * kkw_ref[...]
    kkn = kkr / jnp.maximum(jnp.sqrt(_segsum(kkr * kkr, ind)), 1e-12)
    zk2 = jnp.concatenate([zk, zk], axis=1)
    ka2 = jnp.concatenate([ka_ref[...], ka_ref[...]], axis=1)
    kd = zk2 * (1.0 + (av - 1.0) * ka2)
    kt_ref[0] = kd.T
    bt_ref[0] = (jnp.concatenate([kkn, kkn], axis=1) * av).T
    rt_ref[0] = zr.T
    vt_ref[0] = zv.T
    kkt_ref[0] = kkn.T
    bon_ref[...] = _segsum(zr * (kd[:, 0:W_MIX] + kd[:, W_MIX:2 * W_MIX]) * rk_ref[...], ind) * zv

    zg = jax.nn.gelu(zd)
    u = zg[:, 0:W_MIX]
    vv = zg[:, W_MIX:2 * W_MIX]
    vc = vv - jnp.mean(vv, axis=-1, keepdims=True)
    vn = vc * lax.rsqrt(jnp.mean(vc * vc, axis=-1, keepdims=True) + 1e-5) * lng_ref[...] + lnb_ref[...]
    for c in range(TM // CHUNK):
        rs = slice(c * CHUNK, (c + 1) * CHUNK)
        for gi in range(G_SG):
            cs = slice(gi * LANES, (gi + 1) * LANES)
            s = jnp.dot(ws_ref[gi], vn[rs, cs].astype(BF16), preferred_element_type=F32)
            yd_ref[rs, cs] = u[rs, cs] * (s + bst_ref[:, gi:gi + 1])


def _dual_specs(rows, n_ctx_blocks, **kw):
    return [pl.BlockSpec((rows, D_MODEL), lambda i, *_: (jnp.minimum(i, n_ctx_blocks - 1), 0), **kw),
            pl.BlockSpec((rows, D_MODEL), lambda i, *_: (jnp.maximum(i - n_ctx_blocks, 0), 0), **kw)]


def _prep(x_ctx, x_lat, mod, lw, lat_len):
    n_ctx_tok = x_ctx.shape[0]
    t = n_ctx_tok + x_lat.shape[0]
    n_tiles = t // TM
    n_ctx_tiles = n_ctx_tok // TM
    tiles_per_seq = lat_len // TM
    rows8 = TM // SUBLANES
    last_blk = x_lat.shape[0] // SUBLANES - 1
    midx = functools.partial(_mod_index, tm=TM, n_ctx_tok=n_ctx_tok, lat_len=lat_len)

    def lat_blk8(i, off):
        return (jnp.clip((i - n_ctx_tiles) * rows8 + off, 0, last_blk), 0)

    x_specs = _dual_specs(TM, n_ctx_tiles) + [
        pl.BlockSpec((SUBLANES, D_MODEL), lambda i: lat_blk8(i, -1)),
        pl.BlockSpec((SUBLANES, D_MODEL), lambda i: lat_blk8(i, rows8)),
        pl.BlockSpec((1, N_MOD, D_MODEL), lambda i: (midx(i), 0, 0)),
        _const_spec((1, D_MODEL)),
        pl.BlockSpec(lw["w_in"].shape, lambda i: (0, 0), pipeline_mode=pl.Buffered(1)),
    ]
    wnames = ["conv_a_w", "conv_b_w", "conv_b_b", "lru_w", "lru_b", "lru_lam", "w0", "w2", "a0", "a2",
              "g2", "kk", "ka", "rk", "sg_ln_g", "sg_ln_b", "sg_ws", "sg_bst", "ind"]
    wts = [lw[n] for n in wnames]
    w_specs = [_const_spec(w.shape) for w in wts]
    widths = [W_MIX, W_MIX, W_MIX, 2 * W_MIX, 2 * W_MIX, W_MIX, W_MIX]
    t_rows = [W_MIX, W_MIX, W_MIX, 2 * W_MIX, 2 * W_MIX, 2 * W_MIX]
    out_specs = ([pl.BlockSpec((TM, wd), lambda i: (i, 0)) for wd in widths]
                 + [pl.BlockSpec((1, r, TM), lambda i: (i, 0, 0)) for r in t_rows])
    out_shape = ([jax.ShapeDtypeStruct((t, wd), F32) for wd in widths]
                 + [jax.ShapeDtypeStruct((n_tiles, r, TM), F32) for r in t_rows])
    return pl.pallas_call(
        functools.partial(_prep_body, n_ctx_tiles=n_ctx_tiles, tiles_per_seq=tiles_per_seq),
        grid=(n_tiles,),
        in_specs=x_specs + w_specs,
        out_specs=out_specs,
        out_shape=out_shape,
        compiler_params=_cparams(("parallel",), VMEM_LIMIT),
        name="branch_prep",
    )(x_ctx, x_lat, x_lat, x_lat, mod, lw["norm1_g"], lw["w_in"], *wts)


def _lru_body(a_ref, u_ref, h0_ref, h_ref, hf_ref, *, n_ctx_blocks, ctx_cfg, lat_cfg):
    fw, bw = slice(0, W_MIX), slice(W_MIX, 2 * W_MIX)

    def scan(nseq, l):
        def step(s, carry):
            out = []
            for j in range(nseq):
                tf = j * l + s
                tb = j * l + (l - 1 - s)
                hf = a_ref[pl.ds(tf, 1), fw] * carry[2 * j] + u_ref[pl.ds(tf, 1), fw]
                hb = a_ref[pl.ds(tb, 1), bw] * carry[2 * j + 1] + u_ref[pl.ds(tb, 1), bw]
                h_ref[pl.ds(tf, 1), fw] = hf
                h_ref[pl.ds(tb, 1), bw] = hb
                out += [hf, hb]
            return tuple(out)

        init = []
        for j in range(nseq):
            init += [h0_ref[0, j:j + 1, fw], h0_ref[0, j:j + 1, bw]]
        fin = lax.fori_loop(0, l, step, tuple(init), unroll=2)
        hf_ref[0] = h0_ref[0]
        for j in range(nseq):
            hf_ref[0, j:j + 1, fw] = fin[2 * j]
            hf_ref[0, j:j + 1, bw] = fin[2 * j + 1]

    is_ctx = pl.program_id(0) < n_ctx_blocks
    pl.when(is_ctx)(lambda: scan(*ctx_cfg))
    pl.when(jnp.logical_not(is_ctx))(lambda: scan(*lat_cfg))


def _lru_scan(a, u, h0, *, rows, n_ctx_blocks, ctx_cfg, lat_cfg):
    nb = a.shape[0] // rows
    w = a.shape[1]
    tok = pl.BlockSpec((rows, w), lambda i: (i, 0))
    st = pl.BlockSpec((1,) + h0.shape[1:], lambda i: (i, 0, 0))
    return pl.pallas_call(
        functools.partial(_lru_body, n_ctx_blocks=n_ctx_blocks, ctx_cfg=ctx_cfg, lat_cfg=lat_cfg),
        grid=(nb,),
        in_specs=[tok, tok, st],
        out_specs=[tok, st],
        out_shape=[jax.ShapeDtypeStruct(a.shape, F32), jax.ShapeDtypeStruct(h0.shape, F32)],
        compiler_params=_cparams(("parallel",), VMEM_LIMIT),
        name="lru_scan",
    )(a, u, h0)


def _wkv_body(*refs, nsrc, spb, vs, n_sb, tc, kp, vp):
    vl_n = HS_WKV // vs
    n_in = 6 * nsrc
    k_srcs = [refs[o * nsrc:(o + 1) * nsrc] for o in range(5)]
    v_srcs = refs[5 * nsrc:n_in]
    s0_ref = refs[n_in]
    y_ref = refs[n_in + 1]
    sf_ref = refs[n_in + 2]
    k_scr = refs[n_in + 3:n_in + 8]
    v_scr, y_scr, s_scr = refs[n_in + 8:n_in + 11]
    r_scr, w_scr, k_scr_, kk_scr, b_scr = k_scr
    backward = pl.program_id(0) // n_sb == 1
    seqs = [(s, j) for s in range(nsrc) for j in range(spb)]

    @pl.when(pl.program_id(1) == 0)
    def _():
        s_scr[...] = s0_ref[0]

    def build_k(c, carry):
        row = pl.multiple_of(c * H_WKV, H_WKV)
        for o in range(5):
            slab = [k_srcs[o][s][j, pl.ds(row, H_WKV), :] for s, j in seqs]
            k_scr[o][pl.ds(c, tc, stride=kp), :] = jnp.concatenate(slab * vs, axis=0).T
        return carry

    lax.fori_loop(0, HS_WKV, build_k, 0, unroll=4)

    def build_v(vl, carry):
        slab = []
        for vsi in range(vs):
            row = pl.multiple_of((vsi * vl_n + vl) * H_WKV, H_WKV)
            slab += [v_srcs[s][j, pl.ds(row, H_WKV), :] for s, j in seqs]
        v_scr[pl.ds(vl, tc, stride=vp), :] = jnp.concatenate(slab, axis=0).T
        return carry

    lax.fori_loop(0, vl_n, build_v, 0, unroll=4)

    def step(s, carry):
        t = jnp.where(backward, tc - 1 - s, s)
        krow = pl.multiple_of(t * kp, SUBLANES)
        vrow = pl.multiple_of(t * vp, SUBLANES)
        n_g = vl_n // SUBLANES
        batch = min(n_g, 2)
        n_acc = 4 // batch

        def row(ref, k):
            return jnp.broadcast_to(ref[pl.ds(krow + k, 1), :], (SUBLANES, LANES))

        def total(parts):
            while len(parts) > 1:
                parts = [a + b for a, b in zip(parts[0::2], parts[1::2])]
            return parts[0]

        def accumulate(acc, g, k, p):
            acc[g][k % n_acc] = p if acc[g][k % n_acc] is None else acc[g][k % n_acc] + p

        for g0 in range(0, n_g, batch):
            gs = range(g0, g0 + batch)
            acc = {g: [None] * n_acc for g in gs}
            for k in range(HS_WKV):
                kk = row(kk_scr, k)
                for g in gs:
                    accumulate(acc, g, k, s_scr[g, k] * kk)
            sa = {g: total(acc[g]) for g in gs}
            vv = {g: v_scr[pl.ds(pl.multiple_of(vrow + g * SUBLANES, SUBLANES), SUBLANES), :] for g in gs}
            acc = {g: [None] * n_acc for g in gs}
            for k in range(HS_WKV):
                w, b, kx, r = row(w_scr, k), row(b_scr, k), row(k_scr_, k), row(r_scr, k)
                for g in gs:
                    sn = s_scr[g, k] * w - sa[g] * b + vv[g] * kx
                    s_scr[g, k] = sn
                    accumulate(acc, g, k, sn * r)
            for g in gs:
                y_scr[pl.ds(pl.multiple_of(vrow + g * SUBLANES, SUBLANES), SUBLANES), :] = total(acc[g])
        return carry

    lax.fori_loop(0, tc, step, 0)

    def emit_y(vl, carry):
        yt = y_scr[pl.ds(vl, tc, stride=vp), :].T
        for vsi in range(vs):
            row = pl.multiple_of((vsi * vl_n + vl) * H_WKV, H_WKV)
            for n, (s, j) in enumerate(seqs):
                lane0 = (vsi * len(seqs) + n) * H_WKV
                y_ref[0, 0, s * spb + j, pl.ds(row, H_WKV), :] = yt[lane0:lane0 + H_WKV, :]
        return carry

    lax.fori_loop(0, vl_n, emit_y, 0, unroll=4)
    sf_ref[0] = s_scr[...]


def _wkv_scan(rt, wt, kt, kkt, bt, vt, s0, *, tile0, n_seq, seq_tiles, spb):
    tc = WKV_TC
    if spb > 1:
        assert seq_tiles == 1 and n_seq % spb == 0 and tile0 % spb == 0
        nsrc, n_sb = 1, n_seq // spb
    else:
        nsrc, n_sb = n_seq, 1
    inst = nsrc * spb * H_WKV
    vs = LANES // inst
    vl_n = HS_WKV // vs
    assert vl_n % SUBLANES == 0, "value rows are processed eight at a time"
    cpt = TM // tc
    n_chunks = seq_tiles * cpt
    kp = HS_WKV + SUBLANES
    vp = vl_n + SUBLANES if ((vl_n + SUBLANES) // SUBLANES) % 2 else vl_n + 2 * SUBLANES

    def chunk(g, i):
        return jnp.where(g // n_sb == 1, n_chunks - 1 - i, i)

    def in_map(g, i, *, src, per_dir):
        ce = chunk(g, i)
        rb = (g // n_sb) if per_dir else 0
        if spb > 1:
            return (tile0 // spb + g % n_sb, rb, ce)
        return (tile0 + src * seq_tiles + ce // cpt, rb, ce % cpt)

    def out_map(g, i):
        ce = chunk(g, i)
        if spb > 1:
            return (g // n_sb, 0, g % n_sb, 0, ce)
        return (g // n_sb, ce // cpt, 0, 0, ce % cpt)

    in_specs, operands = [], []
    for arr, per_dir in ((rt, False), (wt, True), (kt, True), (kkt, False), (bt, True), (vt, False)):
        for src in range(nsrc):
            in_specs.append(pl.BlockSpec((spb, W_MIX, tc), functools.partial(in_map, src=src, per_dir=per_dir),
                                         pipeline_mode=pl.Buffered(1)))
            operands.append(arr)
    sspec = pl.BlockSpec((1, vl_n // SUBLANES, HS_WKV, SUBLANES, LANES), lambda g, i: (g, 0, 0, 0, 0))
    in_specs.append(sspec)
    out_specs = [pl.BlockSpec((1, 1, nsrc * spb, W_MIX, tc), out_map)]
    out_shape = [jax.ShapeDtypeStruct((N_DIR, seq_tiles, n_seq, W_MIX, TM), F32)]
    res = pl.pallas_call(
        functools.partial(_wkv_body, nsrc=nsrc, spb=spb, vs=vs, n_sb=n_sb, tc=tc, kp=kp, vp=vp),
        grid=(N_DIR * n_sb, n_chunks),
        in_specs=in_specs,
        out_specs=out_specs + [sspec],
        out_shape=out_shape + [jax.ShapeDtypeStruct(s0.shape, F32)],
        scratch_shapes=([pltpu.VMEM((tc * kp, LANES), F32)] * 5
                        + [pltpu.VMEM((tc * vp, LANES), F32)] * 2
                        + [pltpu.VMEM((vl_n // SUBLANES, HS_WKV, SUBLANES, LANES), F32)]),
        compiler_params=_cparams(("parallel", "arbitrary"), VMEM_LIMIT),
        name="wkv_scan",
    )(*operands, s0)
    return res[0], res[1]


def _merge_body(xc_ref, xl_ref, mod_ref, n1g_ref, n2g_ref, wg_ref, gb_ref, wbr_ref, wo_ref,
                lnxg_ref, lnxb_ref, ind_ref,
                ya_ref, yd_ref, gbg_ref, h_ref, ycf_ref, ycb_ref, ylf_ref, ylb_ref, g_ref, bon_ref,
                x1_ref, h2_ref, *, n_ctx_tiles):
    is_ctx = pl.program_id(0) < n_ctx_tiles
    x = jnp.where(is_ctx, xc_ref[...], xl_ref[...])
    m = mod_ref[0]
    ind = ind_ref[...]
    h = (_rms(x, n1g_ref[...]) * (1.0 + m[1:2, :]) + m[0:1, :]).astype(BF16)
    y_b = gbg_ref[...] * (h_ref[:, 0:W_MIX] + h_ref[:, W_MIX:2 * W_MIX])
    y = jnp.where(is_ctx, ycf_ref[0, 0, 0] + ycb_ref[0, 0, 0], ylf_ref[0, 0, 0] + ylb_ref[0, 0, 0]).T
    yc = y - _segsum(y, ind) * (1.0 / HS_WKV)
    var = _segsum(yc * yc, ind) * (1.0 / HS_WKV)
    y_c = (yc * lax.rsqrt(var + LNX_EPS) * lnxg_ref[...] + lnxb_ref[...] + bon_ref[...]) * g_ref[...]
    merged = None
    for n, yn in enumerate((ya_ref[...], y_b, y_c, yd_ref[...])):
        cs = slice(n * D_MODEL, (n + 1) * D_MODEL)
        gate = jax.nn.sigmoid(jnp.dot(h, wg_ref[:, cs], preferred_element_type=F32) + gb_ref[:, cs])
        br = jnp.dot(yn.astype(BF16), wbr_ref[n * W_MIX:(n + 1) * W_MIX, :], preferred_element_type=F32)
        merged = gate * br if merged is None else merged + gate * br
    mo = jnp.dot(merged.astype(BF16), wo_ref[...], preferred_element_type=F32)
    x1 = x + m[2:3, :] * mo
    x1_ref[...] = x1
    h2_ref[...] = (_rms(x1, n2g_ref[...]) * (1.0 + m[4:5, :]) + m[3:4, :]).astype(BF16)


def _merge(x_ctx, x_lat, mod, lw, tok_in, y_ctx, y_lat, lat_len):
    n_ctx_tok = x_ctx.shape[0]
    t = n_ctx_tok + x_lat.shape[0]
    n_ctx_tiles = n_ctx_tok // TM
    tps = lat_len // TM
    midx = functools.partial(_mod_index, tm=TM, n_ctx_tok=n_ctx_tok, lat_len=lat_len)
    wnames = ["norm1_g", "norm2_g", "w_gate", "gate_b", "w_branch", "w_out", "lnx_g", "lnx_b", "ind"]
    wts = [lw[n] for n in wnames]
    tok = lambda wd: pl.BlockSpec((TM, wd), lambda i: (i, 0))
    ya, yd, gbg, h, g, bon = tok_in
    yblock = (1, 1, 1, W_MIX, TM)

    def ctx_spec(d):
        return pl.BlockSpec(yblock, lambda i: (d, 0, jnp.minimum(i, n_ctx_tiles - 1), 0, 0))

    def lat_spec(d):
        def imap(i):
            r = jnp.maximum(i - n_ctx_tiles, 0)
            return (d, lax.rem(r, tps), lax.div(r, tps), 0, 0)
        return pl.BlockSpec(yblock, imap)

    return pl.pallas_call(
        functools.partial(_merge_body, n_ctx_tiles=n_ctx_tiles),
        grid=(t // TM,),
        in_specs=(_dual_specs(TM, n_ctx_tiles)
                  + [pl.BlockSpec((1, N_MOD, D_MODEL), lambda i: (midx(i), 0, 0))]
                  + [_const_spec(w.shape) for w in wts]
                  + [tok(W_MIX), tok(W_MIX), tok(W_MIX), tok(2 * W_MIX),
                     ctx_spec(0), ctx_spec(1), lat_spec(0), lat_spec(1), tok(W_MIX), tok(W_MIX)]),
        out_specs=[tok(D_MODEL), tok(D_MODEL)],
        out_shape=[jax.ShapeDtypeStruct((t, D_MODEL), F32), jax.ShapeDtypeStruct((t, D_MODEL), BF16)],
        compiler_params=_cparams(("parallel",), VMEM_LIMIT),
        name="merge",
    )(x_ctx, x_lat, mod, *wts, ya, yd, gbg, h, y_ctx, y_ctx, y_lat, y_lat, g, bon)


_CAND_VALID = (8, 8, 8, 5, 4, 3, 2, 2, 2, 8)


def _oddeven_pairs(n):
    def merge(lo, hi, r):
        step = r * 2
        if step < hi - lo:
            yield from merge(lo, hi, step)
            yield from merge(lo + r, hi, step)
            yield from [(i, i + r) for i in range(lo + r, hi - r, step)]
        else:
            yield (lo, lo + r)

    def sort(lo, hi):
        if hi - lo >= 1:
            mid = lo + (hi - lo) // 2
            yield from sort(lo, mid)
            yield from sort(mid + 1, hi)
            yield from merge(lo, hi, 1)

    return tuple(sort(0, n - 1))


_SORT16 = _oddeven_pairs(N_KEYS // SUBLANES)


def _route_head(qs, keys_ref):
    sub = lax.broadcasted_iota(I32, (SUBLANES, LANES), 0)
    kid = lax.broadcasted_iota(I32, (PEER_TOPK, LANES), 0)
    neg = -jnp.inf

    def bc(x, r):
        return jnp.broadcast_to(x[r:r + 1, :], (SUBLANES, LANES))

    def head():
        tops = []
        for p in range(2):
            s = jnp.dot(keys_ref[p], qs[p], preferred_element_type=F32)
            cols = [s[j * SUBLANES:(j + 1) * SUBLANES, :] for j in range(N_KEYS // SUBLANES)]
            cidx = [sub + j * SUBLANES for j in range(N_KEYS // SUBLANES)]
            for a, b in _SORT16:
                take = cols[b] > cols[a]
                cols[a], cols[b] = jnp.where(take, cols[b], cols[a]), jnp.where(take, cols[a], cols[b])
                cidx[a], cidx[b] = jnp.where(take, cidx[b], cidx[a]), jnp.where(take, cidx[a], cidx[b])
            vals = jnp.zeros((PEER_TOPK, LANES), F32)
            idxs = jnp.zeros((PEER_TOPK, LANES), I32)
            for r in range(PEER_TOPK):
                v8, i8 = cols[0], cidx[0]
                for sh in (4, 2, 1):
                    vr, ir = pltpu.roll(v8, sh, 0), pltpu.roll(i8, sh, 0)
                    take = vr > v8
                    v8, i8 = jnp.where(take, vr, v8), jnp.where(take, ir, i8)
                m, ix = v8[0:1, :], i8[0:1, :]
                vals = jnp.where(kid == r, m, vals)
                idxs = jnp.where(kid == r, ix, idxs)
                popped = cidx[0] == ix
                for j in range(PEER_TOPK - 1 - r):
                    cols[j] = jnp.where(popped, cols[j + 1], cols[j])
                    cidx[j] = jnp.where(popped, cidx[j + 1], cidx[j])
            tops.append((vals, idxs))
        (a0, i0), (a1, i1) = tops
        lo, hi = slice(0, SUBLANES), slice(SUBLANES, 2 * SUBLANES)
        slabs = [bc(a0, 0) + a1[lo], bc(a0, 0) + a1[hi]]
        ci = [bc(i0, 0), bc(i0, 0)]
        cj = [i1[lo], i1[hi]]
        for r in range(1, SUBLANES):
            slabs.append(bc(a0, r) + a1[lo])
            ci.append(bc(i0, r))
            cj.append(i1[lo])
        slabs.append(a0[hi] + bc(a1, 0))
        ci.append(i0[hi])
        cj.append(bc(i1, 0))
        slabs = [jnp.where(sub < nv, sl, neg) for sl, nv in zip(slabs, _CAND_VALID)]
        ids = [a * N_KEYS + b for a, b in zip(ci, cj)]
        vals = jnp.zeros((PEER_TOPK, LANES), F32)
        esel = jnp.zeros((PEER_TOPK, LANES), I32)
        for r in range(PEER_TOPK):
            level = list(zip(slabs, ids))
            while len(level) > 1:
                nxt = []
                for (va, ea), (vb, eb) in zip(level[0::2], level[1::2]):
                    take = vb > va
                    nxt.append((jnp.where(take, vb, va), jnp.where(take, eb, ea)))
                if len(level) % 2:
                    nxt.append(level[-1])
                level = nxt
            v8, e8 = level[0]
            for sh in (4, 2, 1):
                vr, er = pltpu.roll(v8, sh, 0), pltpu.roll(e8, sh, 0)
                take = vr > v8
                v8, e8 = jnp.where(take, vr, v8), jnp.where(take, er, e8)
            m, ex = v8[0:1, :], e8[0:1, :]
            slabs = [jnp.where(eid == ex, neg, sl) for sl, eid in zip(slabs, ids)]
            vals = jnp.where(kid == r, m, vals)
            esel = jnp.where(kid == r, ex, esel)
        e = jnp.exp(vals - vals[0:1, :])
        return esel, e / jnp.sum(e, axis=0, keepdims=True)

    return head()


def _peer_body(h2_ref, h2n_ref, wqt_ref, keys_ref, u_ref, v_ref, x1_ref, mod_ref, fng_ref,
               oc_ref, ol_ref,
               q_scr, e_scr, g_scr, et_scr, gt_scr, gs_scr, acc_scr,
               *, rows, pitch, units, n_ctx_tiles, final):
    m = pl.program_id(0)
    e = pl.program_id(1)
    tm = h2_ref.shape[0]
    n_chunks = tm // LANES
    nsel = PEER_HEADS * PEER_TOPK
    slot = lax.rem(m, 2)

    def project_queries(src_ref):
        q = lax.dot_general(wqt_ref[...], src_ref[...], _NT, preferred_element_type=F32).astype(BF16)
        for c in range(n_chunks):
            q_scr[c] = q[:, c * LANES:(c + 1) * LANES]

    def route_unit(u, dst):
        c = u // PEER_HEADS
        h = lax.rem(u, PEER_HEADS)
        qs = [q_scr[c, pl.ds(pl.multiple_of(h * (2 * N_KEYS) + p * N_KEYS, N_KEYS), N_KEYS), :]
              for p in range(2)]
        esel, gates = _route_head(qs, keys_ref)
        row = pl.multiple_of(h * PEER_TOPK, PEER_TOPK)
        e_scr[dst, c, pl.ds(row, PEER_TOPK), :] = esel
        g_scr[dst, c, pl.ds(row, PEER_TOPK), :] = gates

    @pl.when(jnp.logical_and(e == 0, m == 0))
    def _first_tile_routing():
        project_queries(h2_ref)

        def unit(u, c):
            route_unit(u, 0)
            return c

        lax.fori_loop(0, n_chunks * PEER_HEADS, unit, 0)

    @pl.when(e == 0)
    def _build():
        for c in range(n_chunks):
            et_scr[c * LANES:(c + 1) * LANES, :] = e_scr[slot, c].T
            gt_scr[c * LANES:(c + 1) * LANES, :] = g_scr[slot, c].T
        kio = lax.broadcasted_iota(I32, (N_KEYS, nsel), 0)

        def tok(t, c):
            erow = et_scr[pl.ds(t, 1), :]
            grow = gt_scr[pl.ds(t, 1), :]
            at = jnp.where(kio == (erow >> 7), grow, 0.0).astype(BF16)
            bt = jnp.where(kio == (erow & (N_KEYS - 1)), 1.0, 0.0).astype(BF16)
            gt = lax.dot_general(at, bt, _NT, preferred_element_type=F32)
            hi = pltpu.bitcast(gt[0:rows, :], jnp.uint32) & jnp.uint32(0xFFFF0000)
            lo = pltpu.bitcast(gt[rows:2 * rows, :], jnp.uint32) >> 16
            gs_scr[pl.ds(pl.multiple_of(t * pitch, SUBLANES), rows), :] = hi | lo
            return c

        lax.fori_loop(0, tm, tok, 0, unroll=16)
        acc_scr[...] = jnp.zeros_like(acc_scr)
        project_queries(h2n_ref)

    for k in range(units):
        route_unit(e * units + k, 1 - slot)

    per_sub = PEER_SUB // N_KEYS
    per_step = u_ref.shape[0] // N_KEYS
    steps_per_half = rows // per_step
    row0 = lax.rem(e, steps_per_half) * per_step
    shift = jnp.where(e < steps_per_half, 0, 16).astype(jnp.uint32)
    h2 = h2_ref[...]
    total = None
    for sb in range(per_step // per_sub):
        es = slice(sb * PEER_SUB, (sb + 1) * PEER_SUB)
        hmat = lax.dot_general(h2, u_ref[es, :], _NT, preferred_element_type=F32)
        words = jnp.concatenate(
            [gs_scr[pl.ds(row0 + sb * per_sub + ii, tm, stride=pitch), :] for ii in range(per_sub)], axis=1)
        gm = pltpu.bitcast((words << shift) & jnp.uint32(0xFFFF0000), F32)
        act = jax.nn.gelu(hmat.astype(BF16)) * gm.astype(BF16)
        part = jnp.dot(act, v_ref[es, :], preferred_element_type=F32)
        total = part if total is None else total + part
    acc_scr[...] += total

    def result():
        x2 = x1_ref[...] + mod_ref[0][5:6, :] * acc_scr[...]
        return _rms(x2, fng_ref[...]) if final else x2

    last = e == pl.num_programs(1) - 1

    @pl.when(jnp.logical_and(last, m < n_ctx_tiles))
    def _out_ctx():
        oc_ref[...] = result()

    @pl.when(jnp.logical_and(last, m >= n_ctx_tiles))
    def _out_lat():
        ol_ref[...] = result()


def _peer(h2, wqt, keys, u, v, layer, x1, mod, fng, n_ctx_tok, lat_len, final):
    t = h2.shape[0]
    nsel = PEER_HEADS * PEER_TOPK
    rows = N_KEYS // 2
    pitch = rows + SUBLANES
    n_e = (N_KEYS * N_KEYS) // PEER_EB
    n_m = t // TM_PEER
    n_chunks = TM_PEER // LANES
    units = (n_chunks * PEER_HEADS) // n_e
    assert units * n_e == n_chunks * PEER_HEADS
    midx = functools.partial(_mod_index, tm=TM_PEER, n_ctx_tok=n_ctx_tok, lat_len=lat_len)
    tok = lambda wd: pl.BlockSpec((TM_PEER, wd), lambda m, e: (m, 0))
    nxt = pl.BlockSpec((TM_PEER, D_MODEL), lambda m, e: (jnp.minimum(m + 1, n_m - 1), 0),
                       pipeline_mode=pl.Buffered(1))
    espec = pl.BlockSpec((None, PEER_EB, D_MODEL), lambda m, e: (layer, e, 0))
    single = dict(pipeline_mode=pl.Buffered(1))
    return pl.pallas_call(
        functools.partial(_peer_body, rows=rows, pitch=pitch, units=units,
                          n_ctx_tiles=n_ctx_tok // TM_PEER, final=final),
        grid=(n_m, n_e),
        in_specs=[tok(D_MODEL), nxt,
                  pl.BlockSpec(wqt.shape, lambda m, e: (0, 0), **single),
                  pl.BlockSpec(keys.shape, lambda m, e: (0, 0, 0), **single),
                  espec, espec,
                  pl.BlockSpec((TM_PEER, D_MODEL), lambda m, e: (m, 0), **single),
                  pl.BlockSpec((1, N_MOD, D_MODEL), lambda m, e: (midx(m), 0, 0)),
                  _const_spec((1, D_MODEL))],
        out_specs=_dual_specs(TM_PEER, n_ctx_tok // TM_PEER),
        out_shape=[jax.ShapeDtypeStruct((n_ctx_tok, D_MODEL), F32),
                   jax.ShapeDtypeStruct((t - n_ctx_tok, D_MODEL), F32)],
        scratch_shapes=[pltpu.VMEM((n_chunks, wqt.shape[0], LANES), BF16),
                        pltpu.VMEM((2, n_chunks, nsel, LANES), I32),
                        pltpu.VMEM((2, n_chunks, nsel, LANES), F32),
                        pltpu.VMEM((TM_PEER, nsel), I32),
                        pltpu.VMEM((TM_PEER, nsel), F32),
                        pltpu.VMEM((TM_PEER * pitch, N_KEYS), jnp.uint32),
                        pltpu.VMEM((TM_PEER, D_MODEL), F32)],
        compiler_params=_cparams(("arbitrary", "arbitrary"), VMEM_LIMIT),
        name="peer",
    )(h2, h2, wqt, keys, u, v, x1, mod, fng)


def _wkv_state_in(s, vs):
    n = s.shape[0]
    ng = HS_WKV // vs // SUBLANES
    s = s.reshape(n, N_DIR, H_WKV, vs, ng, SUBLANES, HS_WKV).transpose(1, 4, 6, 5, 3, 0, 2)
    return s.reshape(N_DIR, ng, HS_WKV, SUBLANES, vs * n * H_WKV)


def _wkv_state_out(s, n_sb, spb, vs):
    ng = HS_WKV // vs // SUBLANES
    s = s.reshape(N_DIR, n_sb, ng, HS_WKV, SUBLANES, vs, spb, H_WKV).transpose(1, 6, 0, 7, 5, 2, 4, 3)
    return s.reshape(n_sb * spb, N_DIR, H_WKV, HS_WKV, HS_WKV)


def _layer_weights(i, prm):
    eye_h = jnp.eye(H_LRU, dtype=F32)
    eye_d = jnp.eye(N_DIR, dtype=F32)

    def perm(x, axis=-1):
        x = jnp.moveaxis(x, axis, -1)
        lead = x.shape[:-1]
        x = x.reshape(lead + (H_WKV, HS_WKV)).swapaxes(-1, -2).reshape(lead + (W_MIX,))
        return jnp.moveaxis(x, -1, axis)

    def lru_bd(wt):
        return jnp.einsum("dhij,hg->hidgj", wt, eye_h).reshape(W_MIX, N_DIR * W_MIX)

    def lora_bd(wt):
        r = wt.shape[1]
        return jnp.einsum("drc,de->drec", wt, eye_d).reshape(N_DIR * r, N_DIR * W_MIX)

    w_in = prm["w_in"][i]
    pad = jnp.zeros((D_MODEL, Z_COLS - 5504), F32)
    rkv = [perm(w_in[:, 2560 + j * W_MIX:2560 + (j + 1) * W_MIX]) for j in range(3)]
    w_in_perm = jnp.concatenate(
        [w_in[:, 0:1536]] + rkv + [w_in[:, 1536:2560], w_in[:, 4480:5504], w_in[:, 4096:4480], pad],
        axis=1).astype(BF16)
    row = lambda x: x.reshape(1, -1).astype(F32)
    head_of = np.arange(W_MIX) % H_WKV
    w_branch = prm["w_branch"][i]
    w_branch = jnp.concatenate([w_branch[0], w_branch[1], perm(w_branch[2], axis=0), w_branch[3]], axis=0)
    return {
        "w_in": w_in_perm,
        "w_gate": w_in[:, 5504:].astype(BF16),
        "norm1_g": row(prm["norm1_g"][i]),
        "norm2_g": row(prm["norm2_g"][i]),
        "conv_a_w": prm["conv_a_w"][i],
        "conv_b_w": prm["conv_b_w"][i],
        "conv_b_b": row(prm["conv_b_b"][i]),
        "lru_w": jnp.concatenate([lru_bd(prm["lru_wa"][i]), lru_bd(prm["lru_wx"][i])], axis=1).astype(BF16),
        "lru_b": jnp.concatenate([row(prm["lru_ba"][i]), row(prm["lru_bx"][i])], axis=1),
        "lru_lam": row(prm["lru_lambda"][i]),
        "w0": row(perm(prm["rwkv_w0"][i])),
        "w2": lora_bd(perm(prm["rwkv_w2"][i])).astype(BF16),
        "a0": row(perm(prm["rwkv_a0"][i])),
        "a2": lora_bd(perm(prm["rwkv_a2"][i])).astype(BF16),
        "g2": perm(prm["rwkv_g2"][i]).astype(BF16),
        "kk": row(perm(prm["rwkv_kk"][i])),
        "ka": row(perm(prm["rwkv_ka"][i])),
        "rk": row(perm(prm["rwkv_rk"][i].reshape(W_MIX))),
        "lnx_g": row(perm(prm["lnx_g"][i])),
        "lnx_b": row(perm(prm["lnx_b"][i])),
        "sg_ln_g": row(prm["sg_ln_g"][i]),
        "sg_ln_b": row(prm["sg_ln_b"][i]),
        "sg_ws": prm["sg_ws"][i].astype(BF16),
        "sg_bst": prm["sg_bs"][i].T,
        "gate_b": row(prm["gate_b"][i]),
        "w_branch": w_branch.astype(BF16),
        "w_out": prm["w_out"][i].astype(BF16),
        "wq_t": prm["peer_wq"][i].T.astype(BF16),
        "keys": prm["peer_keys"][i].astype(BF16),
        "ind": jnp.asarray(head_of[:, None] == head_of[None, :], BF16),
    }


def kernel(x_prompt, x_sample, state_lru, state_wkv, c, c_ctx, norm1_g, norm2_g, w_mod, b_mod, w_in, conv_a_w, conv_b_w, conv_b_b, lru_wa, lru_ba, lru_wx, lru_bx, lru_lambda, rwkv_w0, rwkv_w2, rwkv_a0, rwkv_a2, rwkv_g2, rwkv_kk, rwkv_ka, rwkv_rk, lnx_g, lnx_b, sg_ln_g, sg_ln_b, sg_ws, sg_bs, gate_b, w_branch, w_out, peer_wq, peer_keys, peer_u, peer_v, final_norm_g):
    prm = dict(norm1_g=norm1_g, norm2_g=norm2_g, w_in=w_in, conv_a_w=conv_a_w, conv_b_w=conv_b_w,
               conv_b_b=conv_b_b, lru_wa=lru_wa, lru_ba=lru_ba, lru_wx=lru_wx, lru_bx=lru_bx,
               lru_lambda=lru_lambda, rwkv_w0=rwkv_w0, rwkv_w2=rwkv_w2, rwkv_a0=rwkv_a0, rwkv_a2=rwkv_a2,
               rwkv_g2=rwkv_g2, rwkv_kk=rwkv_kk, rwkv_ka=rwkv_ka, rwkv_rk=rwkv_rk, lnx_g=lnx_g,
               lnx_b=lnx_b, sg_ln_g=sg_ln_g, sg_ln_b=sg_ln_b, sg_ws=sg_ws, sg_bs=sg_bs, gate_b=gate_b,
               w_branch=w_branch, w_out=w_out, peer_wq=peer_wq, peer_keys=peer_keys, peer_u=peer_u,
               peer_v=peer_v)
    bc, lc, _ = x_prompt.shape
    bl, ll, _ = x_sample.shape
    depth = w_mod.shape[0]
    n_ctx_tok = bc * lc
    n_ctx_tiles = n_ctx_tok // TM
    lat_tiles = ll // TM
    ctx_spb = min(WKV_CTX_SPB, bc)
    lru_spb = ll // lc
    assert lc == TM and ll % TM_PEER == 0 and n_ctx_tok % TM_PEER == 0 and bl + 1 <= SUBLANES
    assert ll % GRID_W == 0 and TM % GRID_W == 0 and bc % ctx_spb == 0 and n_ctx_tok % ll == 0
    assert LANES % (ctx_spb * H_WKV) == 0 and LANES % (bl * H_WKV) == 0

    cond = jnp.zeros((SUBLANES, D_MODEL), F32).at[0].set(c_ctx).at[1:1 + bl].set(c)
    mods = _modulation(cond, w_mod, b_mod).reshape(depth, SUBLANES, N_MOD, D_MODEL)
    fng = final_norm_g.reshape(1, D_MODEL)
    x_ctx = x_prompt.reshape(n_ctx_tok, D_MODEL)
    x_lat = x_sample.reshape(bl * ll, D_MODEL)
    ctx_vs = LANES // (ctx_spb * H_WKV)
    lat_vs = LANES // (bl * H_WKV)
    n_sb = bc // ctx_spb
    wkv_zero = jnp.zeros((N_DIR * n_sb, HS_WKV // ctx_vs // SUBLANES, HS_WKV, SUBLANES, LANES), F32)
    lru_zero = jnp.zeros((n_ctx_tok // ll, lru_spb, N_DIR * W_MIX), F32)
    u_all = peer_u.astype(BF16)
    v_all = peer_v.astype(BF16)
    new_lru, new_wkv = [], []
    pnames = ["ya", "yd", "gbg", "la", "lu", "g", "bon", "rt", "vt", "kkt", "wt", "kt", "bt"]
    for i in range(depth):
        lw = _layer_weights(i, prm)
        mod = mods[i]
        p = dict(zip(pnames, _prep(x_ctx, x_lat, mod, lw, ll)))

        lat_h0 = jnp.zeros((bl, lru_spb, N_DIR * W_MIX), F32).at[:, 0].set(
            state_lru[:, i].astype(F32).reshape(bl, N_DIR * W_MIX))
        h, lru_s = _lru_scan(p["la"], p["lu"], jnp.concatenate([lru_zero, lat_h0], axis=0),
                             rows=ll, n_ctx_blocks=n_ctx_tok // ll, ctx_cfg=(lru_spb, lc), lat_cfg=(1, ll))
        new_lru.append(lru_s[:n_ctx_tok // ll].reshape(bc, N_DIR, W_MIX))

        wkv_in = [p[n] for n in ("rt", "wt", "kt", "kkt", "bt", "vt")]
        y_c, s_c = _wkv_scan(*wkv_in, wkv_zero, tile0=0, n_seq=bc, seq_tiles=1, spb=ctx_spb)
        y_l, _ = _wkv_scan(*wkv_in, _wkv_state_in(state_wkv[:, i].astype(F32), lat_vs),
                           tile0=n_ctx_tiles, n_seq=bl, seq_tiles=lat_tiles, spb=1)
        new_wkv.append(_wkv_state_out(s_c, n_sb, ctx_spb, ctx_vs))

        tok_in = [p["ya"], p["yd"], p["gbg"], h, p["g"], p["bon"]]
        x1, h2 = _merge(x_ctx, x_lat, mod, lw, tok_in, y_c, y_l, ll)
        x_ctx, x_lat = _peer(h2, lw["wq_t"], lw["keys"], u_all, v_all, i, x1, mod, fng,
                             n_ctx_tok, ll, final=(i == depth - 1))
    y_prompt = x_ctx.reshape(bc, lc, D_MODEL)
    y_sample = x_lat.reshape(bl, ll, D_MODEL)
    return (y_prompt, y_sample, jnp.stack(new_lru, axis=1), jnp.stack(new_wkv, axis=1))
```

```python
import functools

import numpy as np
import jax
import jax.numpy as jnp
from jax import lax
from jax.experimental import pallas as pl
from jax.experimental.pallas import tpu as pltpu

F32 = jnp.float32
BF16 = jnp.bfloat16
I32 = jnp.int32

D_MODEL = 1024
W_MIX = 512
N_DIR = 2
H_WKV = 8
HS_WKV = 64
H_LRU = 8
LORA_W = 64
LORA_A = 64
LORA_G = 128
GRID_W = 64
CHUNK = 128
G_SG = 4
N_KEYS = 128
PEER_HEADS = 8
PEER_TOPK = 16
N_MOD = 6
EPS = 1e-6
LNX_EPS = 64e-5
LRU_C = 8.0

LANES = 128
SUBLANES = 8
TM = 256
TM_PEER = 512
PEER_EB = 1024
PEER_SUB = 512
Z_COLS = 5632
WKV_TC = LANES
WKV_CTX_SPB = 8
VMEM_LIMIT = 56 * 1024 * 1024

_NT = (((1,), (1,)), ((), ()))


def _cparams(sem, vmem=None):
    return pltpu.CompilerParams(dimension_semantics=sem, vmem_limit_bytes=vmem)


def _const_spec(shape):
    nd = len(shape)
    return pl.BlockSpec(shape, lambda *_: (0,) * nd)


def _softplus(x):
    return jnp.maximum(x, 0.0) + jnp.log1p(jnp.exp(-jnp.abs(x)))


def _rms(x, g):
    return x * lax.rsqrt(jnp.mean(x * x, axis=-1, keepdims=True) + EPS) * g


def _segsum(x, ind):
    hi = x.astype(BF16)
    lo = (x - hi.astype(F32)).astype(BF16)
    return (jnp.dot(hi, ind, preferred_element_type=F32)
            + jnp.dot(lo, ind, preferred_element_type=F32))


def _mod_index(i, tm, n_ctx_tok, lat_len):
    n_ctx_tiles = n_ctx_tok // tm
    tiles_per_seq = lat_len // tm
    return jnp.where(i < n_ctx_tiles, 0, 1 + lax.div(i - n_ctx_tiles, tiles_per_seq))


def _mod_body(s_ref, w_ref, b_ref, o_ref):
    s = s_ref[...]
    s = s * jax.nn.sigmoid(s)
    o_ref[0] = jnp.dot(s.astype(BF16), w_ref[0].astype(BF16), preferred_element_type=F32) + b_ref[0]


def _modulation(cond, w_mod, b_mod):
    depth = w_mod.shape[0]
    n = w_mod.shape[2]
    tn = 1536
    return pl.pallas_call(
        _mod_body,
        grid=(depth, n // tn),
        in_specs=[_const_spec((SUBLANES, D_MODEL)),
                  pl.BlockSpec((1, D_MODEL, tn), lambda l, j: (l, 0, j)),
                  pl.BlockSpec((1, 1, tn), lambda l, j: (l, 0, j))],
        out_specs=pl.BlockSpec((1, SUBLANES, tn), lambda l, j: (l, 0, j)),
        out_shape=jax.ShapeDtypeStruct((depth, SUBLANES, n), F32),
        compiler_params=_cparams(("parallel", "parallel"), VMEM_LIMIT),
        name="modulation",
    )(cond, w_mod, b_mod.reshape(depth, 1, n))


def _prep_body(xc_ref, xl_ref, xp_ref, xn_ref, mod_ref, n1g_ref, win_ref,
               caw_ref, cbw_ref, cbb_ref, lruw_ref, lrub_ref, lam_ref,
               w0_ref, w2_ref, a0_ref, a2_ref, g2_ref, kkw_ref, ka_ref, rk_ref,
               lng_ref, lnb_ref, ws_ref, bst_ref, ind_ref,
               ya_ref, yd_ref, gbg_ref, la_ref, lu_ref, g_ref, bon_ref,
               rt_ref, vt_ref, kkt_ref, wt_ref, kt_ref, bt_ref,
               *, n_ctx_tiles, tiles_per_seq):
    i = pl.program_id(0)
    is_ctx = i < n_ctx_tiles
    t = lax.broadcasted_iota(I32, (TM, 1), 0)
    ind = ind_ref[...]
    m = mod_ref[0]

    def modulated(xv):
        return (_rms(xv, n1g_ref[...]) * (1.0 + m[1:2, :]) + m[0:1, :]).astype(BF16)

    def project(hv, lo, hi):
        return jnp.dot(hv, win_ref[:, lo:hi], preferred_element_type=F32)

    h = modulated(jnp.where(is_ctx, xc_ref[...], xl_ref[...]))
    za = project(h, 0, 1536)
    zc = project(h, 1536, 3072)
    zb = project(h, 3072, 4096)
    zd = project(h, 4096, 5120)
    zl = project(h, 5120, 5120 + 2 * LORA_W + 2 * LORA_A + LORA_G)
    halo_prev = project(modulated(xp_ref[...]), 3072 + W_MIX, 4096)
    halo_next = project(modulated(xn_ref[...]), 3072 + W_MIX, 4096)

    pm = jnp.where(is_ctx, TM - 1, GRID_W - 1)
    pos = t & pm
    a_b = za[:, 0:W_MIX]
    ac = za[:, W_MIX:2 * W_MIX] * za[:, 2 * W_MIX:3 * W_MIX]
    up = jnp.where(pos == 0, 0.0, pltpu.roll(ac, 1, 0))
    dn = jnp.where(pos == pm, 0.0, pltpu.roll(ac, TM - 1, 0))
    ya_ref[...] = a_b * (caw_ref[0:1, :] * up + caw_ref[1:2, :] * ac + caw_ref[2:3, :] * dn)

    seq_tile = lax.rem(jnp.maximum(i - n_ctx_tiles, 0), tiles_per_seq)
    first = jnp.logical_or(is_ctx, seq_tile == 0)
    last = jnp.logical_or(is_ctx, seq_tile == tiles_per_seq - 1)
    prev = jnp.where(first, 0.0, halo_prev[SUBLANES - 1:SUBLANES, :])
    nxt0 = jnp.where(last, 0.0, halo_next[0:1, :])
    nxt1 = jnp.where(last, 0.0, halo_next[1:2, :])
    bx = zb[:, W_MIX:2 * W_MIX]
    m1 = jnp.where(t == 0, prev, pltpu.roll(bx, 1, 0))
    p1 = jnp.where(t == TM - 1, nxt0, pltpu.roll(bx, TM - 1, 0))
    p2 = jnp.where(t == TM - 2, nxt0, jnp.where(t == TM - 1, nxt1, pltpu.roll(bx, TM - 2, 0)))
    xb = (cbw_ref[0:1, :] * m1 + cbw_ref[1:2, :] * bx + cbw_ref[2:3, :] * p1
          + cbw_ref[3:4, :] * p2 + cbb_ref[...])
    gates = jnp.dot(xb.astype(BF16), lruw_ref[...], preferred_element_type=F32) + lrub_ref[...]
    rg = jax.nn.sigmoid(gates[:, 0:2 * W_MIX])
    ig = jax.nn.sigmoid(gates[:, 2 * W_MIX:4 * W_MIX])
    log_a = -LRU_C * rg * _softplus(-lam_ref[...])
    xb2 = jnp.concatenate([xb, xb], axis=1)
    a = jnp.exp(log_a)
    la_ref[...] = a
    lu_ref[...] = jnp.sqrt(jnp.tanh(-log_a) * (a * a + 1.0)) * (ig * xb2)
    gbg_ref[...] = jax.nn.gelu(zb[:, 0:W_MIX])

    zr = zc[:, 0:W_MIX]
    zk = zc[:, W_MIX:2 * W_MIX]
    zv = zc[:, 2 * W_MIX:3 * W_MIX]
    zwd = zl[:, 0:2 * LORA_W]
    zad = zl[:, 2 * LORA_W:2 * LORA_W + 2 * LORA_A]
    zgd = zl[:, 2 * LORA_W + 2 * LORA_A:2 * LORA_W + 2 * LORA_A + LORA_G]
    wlin = w0_ref[...] + jnp.dot(jnp.tanh(zwd).astype(BF16), w2_ref[...], preferred_element_type=F32)
    wt_ref[0] = jnp.exp(-jnp.exp(-_softplus(-wlin) - 0.5)).T
    av = jax.nn.sigmoid(a0_ref[...] + jnp.dot(zad.astype(BF16), a2_ref[...], preferred_element_type=F32))
    g_ref[...] = jnp.dot(jax.nn.sigmoid(zgd).astype(BF16), g2_ref[...], preferred_element_type=F32)
    kkr = zk * kkw_ref[...]
    kkn = kkr / jnp.maximum(jnp.sqrt(_segsum(kkr * kkr, ind)), 1e-12)
    zk2 = jnp.concatenate([zk, zk], axis=1)
    ka2 = jnp.concatenate([ka_ref[...], ka_ref[...]], axis=1)
    kd = zk2 * (1.0 + (av - 1.0) * ka2)
    kt_ref[0] = kd.T
    bt_ref[0] = (jnp.concatenate([kkn, kkn], axis=1) * av).T
    rt_ref[0] = zr.T
    vt_ref[0] = zv.T
    kkt_ref[0] = kkn.T
    bon_ref[...] = _segsum(zr * (kd[:, 0:W_MIX] + kd[:, W_MIX:2 * W_MIX]) * rk_ref[...], ind) * zv

    zg = jax.nn.gelu(zd)
    u = zg[:, 0:W_MIX]
    vv = zg[:, W_MIX:2 * W_MIX]
    vc = vv - jnp.mean(vv, axis=-1, keepdims=True)
    vn = vc * lax.rsqrt(jnp.mean(vc * vc, axis=-1, keepdims=True) + 1e-5) * lng_ref[...] + lnb_ref[...]
    for c in range(TM // CHUNK):
        rs = slice(c * CHUNK, (c + 1) * CHUNK)
        for gi in range(G_SG):
            cs = slice(gi * LANES, (gi + 1) * LANES)
            s = jnp.dot(ws_ref[gi], vn[rs, cs].astype(BF16), preferred_element_type=F32)
            yd_ref[rs, cs] = u[rs, cs] * (s + bst_ref[:, gi:gi + 1])


def _dual_specs(rows, n_ctx_blocks, **kw):
    return [pl.BlockSpec((rows, D_MODEL), lambda i, *_: (jnp.minimum(i, n_ctx_blocks - 1), 0), **kw),
            pl.BlockSpec((rows, D_MODEL), lambda i, *_: (jnp.maximum(i - n_ctx_blocks, 0), 0), **kw)]


def _prep(x_ctx, x_lat, mod, lw, lat_len):
    n_ctx_tok = x_ctx.shape[0]
    t = n_ctx_tok + x_lat.shape[0]
    n_tiles = t // TM
    n_ctx_tiles = n_ctx_tok // TM
    tiles_per_seq = lat_len // TM
    rows8 = TM // SUBLANES
    last_blk = x_lat.shape[0] // SUBLANES - 1
    midx = functools.partial(_mod_index, tm=TM, n_ctx_tok=n_ctx_tok, lat_len=lat_len)

    def lat_blk8(i, off):
        return (jnp.clip((i - n_ctx_tiles) * rows8 + off, 0, last_blk), 0)

    x_specs = _dual_specs(TM, n_ctx_tiles) + [
        pl.BlockSpec((SUBLANES, D_MODEL), lambda i: lat_blk8(i, -1)),
        pl.BlockSpec((SUBLANES, D_MODEL), lambda i: lat_blk8(i, rows8)),
        pl.BlockSpec((1, N_MOD, D_MODEL), lambda i: (midx(i), 0, 0)),
        _const_spec((1, D_MODEL)),
        pl.BlockSpec(lw["w_in"].shape, lambda i: (0, 0), pipeline_mode=pl.Buffered(1)),
    ]
    wnames = ["conv_a_w", "conv_b_w", "conv_b_b", "lru_w", "lru_b", "lru_lam", "w0", "w2", "a0", "a2",
              "g2", "kk", "ka", "rk", "sg_ln_g", "sg_ln_b", "sg_ws", "sg_bst", "ind"]
    wts = [lw[n] for n in wnames]
    w_specs = [_const_spec(w.shape) for w in wts]
    widths = [W_MIX, W_MIX, W_MIX, 2 * W_MIX, 2 * W_MIX, W_MIX, W_MIX]
    t_rows = [W_MIX, W_MIX, W_MIX, 2 * W_MIX, 2 * W_MIX, 2 * W_MIX]
    out_specs = ([pl.BlockSpec((TM, wd), lambda i: (i, 0)) for wd in widths]
                 + [pl.BlockSpec((1, r, TM), lambda i: (i, 0, 0)) for r in t_rows])
    out_shape = ([jax.ShapeDtypeStruct((t, wd), F32) for wd in widths]
                 + [jax.ShapeDtypeStruct((n_tiles, r, TM), F32) for r in t_rows])
    return pl.pallas_call(
        functools.partial(_prep_body, n_ctx_tiles=n_ctx_tiles, tiles_per_seq=tiles_per_seq),
        grid=(n_tiles,),
        in_specs=x_specs + w_specs,
        out_specs=out_specs,
        out_shape=out_shape,
        compiler_params=_cparams(("parallel",), VMEM_LIMIT),
        name="branch_prep",
    )(x_ctx, x_lat, x_lat, x_lat, mod, lw["norm1_g"], lw["w_in"], *wts)


def _lru_body(a_ref, u_ref, h0_ref, h_ref, hf_ref, *, n_ctx_blocks, ctx_cfg, lat_cfg):
    fw, bw = slice(0, W_MIX), slice(W_MIX, 2 * W_MIX)

    def scan(nseq, l):
        def step(s, carry):
            out = []
            for j in range(nseq):
                tf = j * l + s
                tb = j * l + (l - 1 - s)
                hf = a_ref[pl.ds(tf, 1), fw] * carry[2 * j] + u_ref[pl.ds(tf, 1), fw]
                hb = a_ref[pl.ds(tb, 1), bw] * carry[2 * j + 1] + u_ref[pl.ds(tb, 1), bw]
                h_ref[pl.ds(tf, 1), fw] = hf
                h_ref[pl.ds(tb, 1), bw] = hb
                out += [hf, hb]
            return tuple(out)

        init = []
        for j in range(nseq):
            init += [h0_ref[0, j:j + 1, fw], h0_ref[0, j:j + 1, bw]]
        fin = lax.fori_loop(0, l, step, tuple(init), unroll=2)
        hf_ref[0] = h0_ref[0]
        for j in range(nseq):
            hf_ref[0, j:j + 1, fw] = fin[2 * j]
            hf_ref[0, j:j + 1, bw] = fin[2 * j + 1]

    is_ctx = pl.program_id(0) < n_ctx_blocks
    pl.when(is_ctx)(lambda: scan(*ctx_cfg))
    pl.when(jnp.logical_not(is_ctx))(lambda: scan(*lat_cfg))


def _lru_scan(a, u, h0, *, rows, n_ctx_blocks, ctx_cfg, lat_cfg):
    nb = a.shape[0] // rows
    w = a.shape[1]
    tok = pl.BlockSpec((rows, w), lambda i: (i, 0))
    st = pl.BlockSpec((1,) + h0.shape[1:], lambda i: (i, 0, 0))
    return pl.pallas_call(
        functools.partial(_lru_body, n_ctx_blocks=n_ctx_blocks, ctx_cfg=ctx_cfg, lat_cfg=lat_cfg),
        grid=(nb,),
        in_specs=[tok, tok, st],
        out_specs=[tok, st],
        out_shape=[jax.ShapeDtypeStruct(a.shape, F32), jax.ShapeDtypeStruct(h0.shape, F32)],
        compiler_params=_cparams(("parallel",), VMEM_LIMIT),
        name="lru_scan",
    )(a, u, h0)


def _wkv_body(*refs, nsrc, spb, vs, n_sb, tc, kp, vp):
    vl_n = HS_WKV // vs
    n_in = 6 * nsrc
    k_srcs = [refs[o * nsrc:(o + 1) * nsrc] for o in range(5)]
    v_srcs = refs[5 * nsrc:n_in]
    s0_ref = refs[n_in]
    y_ref = refs[n_in + 1]
    sf_ref = refs[n_in + 2]
    k_scr = refs[n_in + 3:n_in + 8]
    v_scr, y_scr, s_scr = refs[n_in + 8:n_in + 11]
    r_scr, w_scr, k_scr_, kk_scr, b_scr = k_scr
    backward = pl.program_id(0) // n_sb == 1
    seqs = [(s, j) for s in range(nsrc) for j in range(spb)]

    @pl.when(pl.program_id(1) == 0)
    def _():
        s_scr[...] = s0_ref[0]

    def build_k(c, carry):
        row = pl.multiple_of(c * H_WKV, H_WKV)
        for o in range(5):
            slab = [k_srcs[o][s][j, pl.ds(row, H_WKV), :] for s, j in seqs]
            k_scr[o][pl.ds(c, tc, stride=kp), :] = jnp.concatenate(slab * vs, axis=0).T
        return carry

    lax.fori_loop(0, HS_WKV, build_k, 0, unroll=4)

    def build_v(vl, carry):
        slab = []
        for vsi in range(vs):
            row = pl.multiple_of((vsi * vl_n + vl) * H_WKV, H_WKV)
            slab += [v_srcs[s][j, pl.ds(row, H_WKV), :] for s, j in seqs]
        v_scr[pl.ds(vl, tc, stride=vp), :] = jnp.concatenate(slab, axis=0).T
        return carry

    lax.fori_loop(0, vl_n, build_v, 0, unroll=4)

    def step(s, carry):
        t = jnp.where(backward, tc - 1 - s, s)
        krow = pl.multiple_of(t * kp, SUBLANES)
        vrow = pl.multiple_of(t * vp, SUBLANES)
        n_g = vl_n // SUBLANES
        batch = min(n_g, 2)
        n_acc = 4 // batch

        def row(ref, k):
            return jnp.broadcast_to(ref[pl.ds(krow + k, 1), :], (SUBLANES, LANES))

        def total(parts):
            while len(parts) > 1:
                parts = [a + b for a, b in zip(parts[0::2], parts[1::2])]
            return parts[0]

        def accumulate(acc, g, k, p):
            acc[g][k % n_acc] = p if acc[g][k % n_acc] is None else acc[g][k % n_acc] + p

        for g0 in range(0, n_g, batch):
            gs = range(g0, g0 + batch)
            acc = {g: [None] * n_acc for g in gs}
            for k in range(HS_WKV):
                kk = row(kk_scr, k)
                for g in gs:
                    accumulate(acc, g, k, s_scr[g, k] * kk)
            sa = {g: total(acc[g]) for g in gs}
            vv = {g: v_scr[pl.ds(pl.multiple_of(vrow + g * SUBLANES, SUBLANES), SUBLANES), :] for g in gs}
            acc = {g: [None] * n_acc for g in gs}
            for k in range(HS_WKV):
                w, b, kx, r = row(w_scr, k), row(b_scr, k), row(k_scr_, k), row(r_scr, k)
                for g in gs:
                    sn = s_scr[g, k] * w - sa[g] * b + vv[g] * kx
                    s_scr[g, k] = sn
                    accumulate(acc, g, k, sn * r)
            for g in gs:
                y_scr[pl.ds(pl.multiple_of(vrow + g * SUBLANES, SUBLANES), SUBLANES), :] = total(acc[g])
        return carry

    lax.fori_loop(0, tc, step, 0)

    def emit_y(vl, carry):
        yt = y_scr[pl.ds(vl, tc, stride=vp), :].T
        for vsi in range(vs):
            row = pl.multiple_of((vsi * vl_n + vl) * H_WKV, H_WKV)
            for n, (s, j) in enumerate(seqs):
                lane0 = (vsi * len(seqs) + n) * H_WKV
                y_ref[0, 0, s * spb + j, pl.ds(row, H_WKV), :] = yt[lane0:lane0 + H_WKV, :]
        return carry

    lax.fori_loop(0, vl_n, emit_y, 0, unroll=4)
    sf_ref[0] = s_scr[...]


def _wkv_scan(rt, wt, kt, kkt, bt, vt, s0, *, tile0, n_seq, seq_tiles, spb):
    tc = WKV_TC
    if spb > 1:
        assert seq_tiles == 1 and n_seq % spb == 0 and tile0 % spb == 0
        nsrc, n_sb = 1, n_seq // spb
    else:
        nsrc, n_sb = n_seq, 1
    inst = nsrc * spb * H_WKV
    vs = LANES // inst
    vl_n = HS_WKV // vs
    assert vl_n % SUBLANES == 0, "value rows are processed eight at a time"
    cpt = TM // tc
    n_chunks = seq_tiles * cpt
    kp = HS_WKV + SUBLANES
    vp = vl_n + SUBLANES if ((vl_n + SUBLANES) // SUBLANES) % 2 else vl_n + 2 * SUBLANES

    def chunk(g, i):
        return jnp.where(g // n_sb == 1, n_chunks - 1 - i, i)

    def in_map(g, i, *, src, per_dir):
        ce = chunk(g, i)
        rb = (g // n_sb) if per_dir else 0
        if spb > 1:
            return (tile0 // spb + g % n_sb, rb, ce)
        return (tile0 + src * seq_tiles + ce // cpt, rb, ce % cpt)

    def out_map(g, i):
        ce = chunk(g, i)
        if spb > 1:
            return (g // n_sb, 0, g % n_sb, 0, ce)
        return (g // n_sb, ce // cpt, 0, 0, ce % cpt)

    in_specs, operands = [], []
    for arr, per_dir in ((rt, False), (wt, True), (kt, True), (kkt, False), (bt, True), (vt, False)):
        for src in range(nsrc):
            in_specs.append(pl.BlockSpec((spb, W_MIX, tc), functools.partial(in_map, src=src, per_dir=per_dir),
                                         pipeline_mode=pl.Buffered(1)))
            operands.append(arr)
    sspec = pl.BlockSpec((1, vl_n // SUBLANES, HS_WKV, SUBLANES, LANES), lambda g, i: (g, 0, 0, 0, 0))
    in_specs.append(sspec)
    out_specs = [pl.BlockSpec((1, 1, nsrc * spb, W_MIX, tc), out_map)]
    out_shape = [jax.ShapeDtypeStruct((N_DIR, seq_tiles, n_seq, W_MIX, TM), F32)]
    res = pl.pallas_call(
        functools.partial(_wkv_body, nsrc=nsrc, spb=spb, vs=vs, n_sb=n_sb, tc=tc, kp=kp, vp=vp),
        grid=(N_DIR * n_sb, n_chunks),
        in_specs=in_specs,
        out_specs=out_specs + [sspec],
        out_shape=out_shape + [jax.ShapeDtypeStruct(s0.shape, F32)],
        scratch_shapes=([pltpu.VMEM((tc * kp, LANES), F32)] * 5
                        + [pltpu.VMEM((tc * vp, LANES), F32)] * 2
                        + [pltpu.VMEM((vl_n // SUBLANES, HS_WKV, SUBLANES, LANES), F32)]),
        compiler_params=_cparams(("parallel", "arbitrary"), VMEM_LIMIT),
        name="wkv_scan",
    )(*operands, s0)
    return res[0], res[1]


def _merge_body(xc_ref, xl_ref, mod_ref, n1g_ref, n2g_ref, wg_ref, gb_ref, wbr_ref, wo_ref,
                lnxg_ref, lnxb_ref, ind_ref,
                ya_ref, yd_ref, gbg_ref, h_ref, ycf_ref, ycb_ref, ylf_ref, ylb_ref, g_ref, bon_ref,
                x1_ref, h2_ref, *, n_ctx_tiles):
    is_ctx = pl.program_id(0) < n_ctx_tiles
    x = jnp.where(is_ctx, xc_ref[...], xl_ref[...])
    m = mod_ref[0]
    ind = ind_ref[...]
    h = (_rms(x, n1g_ref[...]) * (1.0 + m[1:2, :]) + m[0:1, :]).astype(BF16)
    y_b = gbg_ref[...] * (h_ref[:, 0:W_MIX] + h_ref[:, W_MIX:2 * W_MIX])
    y = jnp.where(is_ctx, ycf_ref[0, 0, 0] + ycb_ref[0, 0, 0], ylf_ref[0, 0, 0] + ylb_ref[0, 0, 0]).T
    yc = y - _segsum(y, ind) * (1.0 / HS_WKV)
    var = _segsum(yc * yc, ind) * (1.0 / HS_WKV)
    y_c = (yc * lax.rsqrt(var + LNX_EPS) * lnxg_ref[...] + lnxb_ref[...] + bon_ref[...]) * g_ref[...]
    merged = None
    for n, yn in enumerate((ya_ref[...], y_b, y_c, yd_ref[...])):
        cs = slice(n * D_MODEL, (n + 1) * D_MODEL)
        gate = jax.nn.sigmoid(jnp.dot(h, wg_ref[:, cs], preferred_element_type=F32) + gb_ref[:, cs])
        br = jnp.dot(yn.astype(BF16), wbr_ref[n * W_MIX:(n + 1) * W_MIX, :], preferred_element_type=F32)
        merged = gate * br if merged is None else merged + gate * br
    mo = jnp.dot(merged.astype(BF16), wo_ref[...], preferred_element_type=F32)
    x1 = x + m[2:3, :] * mo
    x1_ref[...] = x1
    h2_ref[...] = (_rms(x1, n2g_ref[...]) * (1.0 + m[4:5, :]) + m[3:4, :]).astype(BF16)


def _merge(x_ctx, x_lat, mod, lw, tok_in, y_ctx, y_lat, lat_len):
    n_ctx_tok = x_ctx.shape[0]
    t = n_ctx_tok + x_lat.shape[0]
    n_ctx_tiles = n_ctx_tok // TM
    tps = lat_len // TM
    midx = functools.partial(_mod_index, tm=TM, n_ctx_tok=n_ctx_tok, lat_len=lat_len)
    wnames = ["norm1_g", "norm2_g", "w_gate", "gate_b", "w_branch", "w_out", "lnx_g", "lnx_b", "ind"]
    wts = [lw[n] for n in wnames]
    tok = lambda wd: pl.BlockSpec((TM, wd), lambda i: (i, 0))
    ya, yd, gbg, h, g, bon = tok_in
    yblock = (1, 1, 1, W_MIX, TM)

    def ctx_spec(d):
        return pl.BlockSpec(yblock, lambda i: (d, 0, jnp.minimum(i, n_ctx_tiles - 1), 0, 0))

    def lat_spec(d):
        def imap(i):
            r = jnp.maximum(i - n_ctx_tiles, 0)
            return (d, lax.rem(r, tps), lax.div(r, tps), 0, 0)
        return pl.BlockSpec(yblock, imap)

    return pl.pallas_call(
        functools.partial(_merge_body, n_ctx_tiles=n_ctx_tiles),
        grid=(t // TM,),
        in_specs=(_dual_specs(TM, n_ctx_tiles)
                  + [pl.BlockSpec((1, N_MOD, D_MODEL), lambda i: (midx(i), 0, 0))]
                  + [_const_spec(w.shape) for w in wts]
                  + [tok(W_MIX), tok(W_MIX), tok(W_MIX), tok(2 * W_MIX),
                     ctx_spec(0), ctx_spec(1), lat_spec(0), lat_spec(1), tok(W_MIX), tok(W_MIX)]),
        out_specs=[tok(D_MODEL), tok(D_MODEL)],
        out_shape=[jax.ShapeDtypeStruct((t, D_MODEL), F32), jax.ShapeDtypeStruct((t, D_MODEL), BF16)],
        compiler_params=_cparams(("parallel",), VMEM_LIMIT),
        name="merge",
    )(x_ctx, x_lat, mod, *wts, ya, yd, gbg, h, y_ctx, y_ctx, y_lat, y_lat, g, bon)


_CAND_VALID = (8, 8, 8, 5, 4, 3, 2, 2, 2, 8)


def _oddeven_pairs(n):
    def merge(lo, hi, r):
        step = r * 2
        if step < hi - lo:
            yield from merge(lo, hi, step)
            yield from merge(lo + r, hi, step)
            yield from [(i, i + r) for i in range(lo + r, hi - r, step)]
        else:
            yield (lo, lo + r)

    def sort(lo, hi):
        if hi - lo >= 1:
            mid = lo + (hi - lo) // 2
            yield from sort(lo, mid)
            yield from sort(mid + 1, hi)
            yield from merge(lo, hi, 1)

    return tuple(sort(0, n - 1))


_SORT16 = _oddeven_pairs(N_KEYS // SUBLANES)


def _route_head(qs, keys_ref):
    sub = lax.broadcasted_iota(I32, (SUBLANES, LANES), 0)
    kid = lax.broadcasted_iota(I32, (PEER_TOPK, LANES), 0)
    neg = -jnp.inf

    def bc(x, r):
        return jnp.broadcast_to(x[r:r + 1, :], (SUBLANES, LANES))

    def head():
        tops = []
        for p in range(2):
            s = jnp.dot(keys_ref[p], qs[p], preferred_element_type=F32)
            cols = [s[j * SUBLANES:(j + 1) * SUBLANES, :] for j in range(N_KEYS // SUBLANES)]
            cidx = [sub + j * SUBLANES for j in range(N_KEYS // SUBLANES)]
            for a, b in _SORT16:
                take = cols[b] > cols[a]
                cols[a], cols[b] = jnp.where(take, cols[b], cols[a]), jnp.where(take, cols[a], cols[b])
                cidx[a], cidx[b] = jnp.where(take, cidx[b], cidx[a]), jnp.where(take, cidx[a], cidx[b])
            vals = jnp.zeros((PEER_TOPK, LANES), F32)
            idxs = jnp.zeros((PEER_TOPK, LANES), I32)
            for r in range(PEER_TOPK):
                v8, i8 = cols[0], cidx[0]
                for sh in (4, 2, 1):
                    vr, ir = pltpu.roll(v8, sh, 0), pltpu.roll(i8, sh, 0)
                    take = vr > v8
                    v8, i8 = jnp.where(take, vr, v8), jnp.where(take, ir, i8)
                m, ix = v8[0:1, :], i8[0:1, :]
                vals = jnp.where(kid == r, m, vals)
                idxs = jnp.where(kid == r, ix, idxs)
                popped = cidx[0] == ix
                for j in range(PEER_TOPK - 1 - r):
                    cols[j] = jnp.where(popped, cols[j + 1], cols[j])
                    cidx[j] = jnp.where(popped, cidx[j + 1], cidx[j])
            tops.append((vals, idxs))
        (a0, i0), (a1, i1) = tops
        lo, hi = slice(0, SUBLANES), slice(SUBLANES, 2 * SUBLANES)
        slabs = [bc(a0, 0) + a1[lo], bc(a0, 0) + a1[hi]]
        ci = [bc(i0, 0), bc(i0, 0)]
        cj = [i1[lo], i1[hi]]
        for r in range(1, SUBLANES):
            slabs.append(bc(a0, r) + a1[lo])
            ci.append(bc(i0, r))
            cj.append(i1[lo])
        slabs.append(a0[hi] + bc(a1, 0))
        ci.append(i0[hi])
        cj.append(bc(i1, 0))
        slabs = [jnp.where(sub < nv, sl, neg) for sl, nv in zip(slabs, _CAND_VALID)]
        ids = [a * N_KEYS + b for a, b in zip(ci, cj)]
        vals = jnp.zeros((PEER_TOPK, LANES), F32)
        esel = jnp.zeros((PEER_TOPK, LANES), I32)
        for r in range(PEER_TOPK):
            level = list(zip(slabs, ids))
            while len(level) > 1:
                nxt = []
                for (va, ea), (vb, eb) in zip(level[0::2], level[1::2]):
                    take = vb > va
                    nxt.append((jnp.where(take, vb, va), jnp.where(take, eb, ea)))
                if len(level) % 2:
                    nxt.append(level[-1])
                level = nxt
            v8, e8 = level[0]
            for sh in (4, 2, 1):
                vr, er = pltpu.roll(v8, sh, 0), pltpu.roll(e8, sh, 0)
                take = vr > v8
                v8, e8 = jnp.where(take, vr, v8), jnp.where(take, er, e8)
            m, ex = v8[0:1, :], e8[0:1, :]
            slabs = [jnp.where(eid == ex, neg, sl) for sl, eid in zip(slabs, ids)]
            vals = jnp.where(kid == r, m, vals)
            esel = jnp.where(kid == r, ex, esel)
        e = jnp.exp(vals - vals[0:1, :])
        return esel, e / jnp.sum(e, axis=0, keepdims=True)

    return head()


def _peer_body(h2_ref, h2n_ref, wqt_ref, keys_ref, u_ref, v_ref, x1_ref, mod_ref, fng_ref,
               oc_ref, ol_ref,
               q_scr, e_scr, g_scr, et_scr, gt_scr, gs_scr, acc_scr,
               *, rows, pitch, units, n_ctx_tiles, final):
    m = pl.program_id(0)
    e = pl.program_id(1)
    tm = h2_ref.shape[0]
    n_chunks = tm // LANES
    nsel = PEER_HEADS * PEER_TOPK
    slot = lax.rem(m, 2)

    def project_queries(src_ref):
        q = lax.dot_general(wqt_ref[...], src_ref[...], _NT, preferred_element_type=F32).astype(BF16)
        for c in range(n_chunks):
            q_scr[c] = q[:, c * LANES:(c + 1) * LANES]

    def route_unit(u, dst):
        c = u // PEER_HEADS
        h = lax.rem(u, PEER_HEADS)
        qs = [q_scr[c, pl.ds(pl.multiple_of(h * (2 * N_KEYS) + p * N_KEYS, N_KEYS), N_KEYS), :]
              for p in range(2)]
        esel, gates = _route_head(qs, keys_ref)
        row = pl.multiple_of(h * PEER_TOPK, PEER_TOPK)
        e_scr[dst, c, pl.ds(row, PEER_TOPK), :] = esel
        g_scr[dst, c, pl.ds(row, PEER_TOPK), :] = gates

    @pl.when(jnp.logical_and(e == 0, m == 0))
    def _first_tile_routing():
        project_queries(h2_ref)

        def unit(u, c):
            route_unit(u, 0)
            return c

        lax.fori_loop(0, n_chunks * PEER_HEADS, unit, 0)

    @pl.when(e == 0)
    def _build():
        for c in range(n_chunks):
            et_scr[c * LANES:(c + 1) * LANES, :] = e_scr[slot, c].T
            gt_scr[c * LANES:(c + 1) * LANES, :] = g_scr[slot, c].T
        kio = lax.broadcasted_iota(I32, (N_KEYS, nsel), 0)

        def tok(t, c):
            erow = et_scr[pl.ds(t, 1), :]
            grow = gt_scr[pl.ds(t, 1), :]
            at = jnp.where(kio == (erow >> 7), grow, 0.0).astype(BF16)
            bt = jnp.where(kio == (erow & (N_KEYS - 1)), 1.0, 0.0).astype(BF16)
            gt = lax.dot_general(at, bt, _NT, preferred_element_type=F32)
            hi = pltpu.bitcast(gt[0:rows, :], jnp.uint32) & jnp.uint32(0xFFFF0000)
            lo = pltpu.bitcast(gt[rows:2 * rows, :], jnp.uint32) >> 16
            gs_scr[pl.ds(pl.multiple_of(t * pitch, SUBLANES), rows), :] = hi | lo
            return c

        lax.fori_loop(0, tm, tok, 0, unroll=128)
        acc_scr[...] = jnp.zeros_like(acc_scr)
        project_queries(h2n_ref)

    for k in range(units):
        route_unit(e * units + k, 1 - slot)

    per_sub = PEER_SUB // N_KEYS
    per_step = u_ref.shape[0] // N_KEYS
    steps_per_half = rows // per_step
    row0 = lax.rem(e, steps_per_half) * per_step
    shift = jnp.where(e < steps_per_half, 0, 16).astype(jnp.uint32)
    h2 = h2_ref[...]
    total = None
    for sb in range(per_step // per_sub):
        es = slice(sb * PEER_SUB, (sb + 1) * PEER_SUB)
        hmat = lax.dot_general(h2, u_ref[es, :], _NT, preferred_element_type=F32)
        words = jnp.concatenate(
            [gs_scr[pl.ds(row0 + sb * per_sub + ii, tm, stride=pitch), :] for ii in range(per_sub)], axis=1)
        gm = pltpu.bitcast((words << shift) & jnp.uint32(0xFFFF0000), F32)
        act = jax.nn.gelu(hmat.astype(BF16)) * gm.astype(BF16)
        part = jnp.dot(act, v_ref[es, :], preferred_element_type=F32)
        total = part if total is None else total + part
    acc_scr[...] += total

    def result():
        x2 = x1_ref[...] + mod_ref[0][5:6, :] * acc_scr[...]
        return _rms(x2, fng_ref[...]) if final else x2

    last = e == pl.num_programs(1) - 1

    @pl.when(jnp.logical_and(last, m < n_ctx_tiles))
    def _out_ctx():
        oc_ref[...] = result()

    @pl.when(jnp.logical_and(last, m >= n_ctx_tiles))
    def _out_lat():
        ol_ref[...] = result()


def _peer(h2, wqt, keys, u, v, layer, x1, mod, fng, n_ctx_tok, lat_len, final):
    t = h2.shape[0]
    nsel = PEER_HEADS * PEER_TOPK
    rows = N_KEYS // 2
    pitch = rows + SUBLANES
    n_e = (N_KEYS * N_KEYS) // PEER_EB
    n_m = t // TM_PEER
    n_chunks = TM_PEER // LANES
    units = (n_chunks * PEER_HEADS) // n_e
    assert units * n_e == n_chunks * PEER_HEADS
    midx = functools.partial(_mod_index, tm=TM_PEER, n_ctx_tok=n_ctx_tok, lat_len=lat_len)
    tok = lambda wd: pl.BlockSpec((TM_PEER, wd), lambda m, e: (m, 0))
    nxt = pl.BlockSpec((TM_PEER, D_MODEL), lambda m, e: (jnp.minimum(m + 1, n_m - 1), 0),
                       pipeline_mode=pl.Buffered(1))
    espec = pl.BlockSpec((None, PEER_EB, D_MODEL), lambda m, e: (layer, e, 0))
    single = dict(pipeline_mode=pl.Buffered(1))
    return pl.pallas_call(
        functools.partial(_peer_body, rows=rows, pitch=pitch, units=units,
                          n_ctx_tiles=n_ctx_tok // TM_PEER, final=final),
        grid=(n_m, n_e),
        in_specs=[tok(D_MODEL), nxt,
                  pl.BlockSpec(wqt.shape, lambda m, e: (0, 0), **single),
                  pl.BlockSpec(keys.shape, lambda m, e: (0, 0, 0), **single),
                  espec, espec,
                  pl.BlockSpec((TM_PEER, D_MODEL), lambda m, e: (m, 0), **single),
                  pl.BlockSpec((1, N_MOD, D_MODEL), lambda m, e: (midx(m), 0, 0)),
                  _const_spec((1, D_MODEL))],
        out_specs=_dual_specs(TM_PEER, n_ctx_tok // TM_PEER),
        out_shape=[jax.ShapeDtypeStruct((n_ctx_tok, D_MODEL), F32),
                   jax.ShapeDtypeStruct((t - n_ctx_tok, D_MODEL), F32)],
        scratch_shapes=[pltpu.VMEM((n_chunks, wqt.shape[0], LANES), BF16),
                        pltpu.VMEM((2, n_chunks, nsel, LANES), I32),
                        pltpu.VMEM((2, n_chunks, nsel, LANES), F32),
                        pltpu.VMEM((TM_PEER, nsel), I32),
                        pltpu.VMEM((TM_PEER, nsel), F32),
                        pltpu.VMEM((TM_PEER * pitch, N_KEYS), jnp.uint32),
                        pltpu.VMEM((TM_PEER, D_MODEL), F32)],
        compiler_params=_cparams(("arbitrary", "arbitrary"), VMEM_LIMIT),
        name="peer",
    )(h2, h2, wqt, keys, u, v, x1, mod, fng)


def _wkv_state_in(s, vs):
    n = s.shape[0]
    ng = HS_WKV // vs // SUBLANES
    s = s.reshape(n, N_DIR, H_WKV, vs, ng, SUBLANES, HS_WKV).transpose(1, 4, 6, 5, 3, 0, 2)
    return s.reshape(N_DIR, ng, HS_WKV, SUBLANES, vs * n * H_WKV)


def _wkv_state_out(s, n_sb, spb, vs):
    ng = HS_WKV // vs // SUBLANES
    s = s.reshape(N_DIR, n_sb, ng, HS_WKV, SUBLANES, vs, spb, H_WKV).transpose(1, 6, 0, 7, 5, 2, 4, 3)
    return s.reshape(n_sb * spb, N_DIR, H_WKV, HS_WKV, HS_WKV)


def _layer_weights(i, prm):
    eye_h = jnp.eye(H_LRU, dtype=F32)
    eye_d = jnp.eye(N_DIR, dtype=F32)

    def perm(x, axis=-1):
        x = jnp.moveaxis(x, axis, -1)
        lead = x.shape[:-1]
        x = x.reshape(lead + (H_WKV, HS_WKV)).swapaxes(-1, -2).reshape(lead + (W_MIX,))
        return jnp.moveaxis(x, -1, axis)

    def lru_bd(wt):
        return jnp.einsum("dhij,hg->hidgj", wt, eye_h).reshape(W_MIX, N_DIR * W_MIX)

    def lora_bd(wt):
        r = wt.shape[1]
        return jnp.einsum("drc,de->drec", wt, eye_d).reshape(N_DIR * r, N_DIR * W_MIX)

    w_in = prm["w_in"][i]
    pad = jnp.zeros((D_MODEL, Z_COLS - 5504), F32)
    rkv = [perm(w_in[:, 2560 + j * W_MIX:2560 + (j + 1) * W_MIX]) for j in range(3)]
    w_in_perm = jnp.concatenate(
        [w_in[:, 0:1536]] + rkv + [w_in[:, 1536:2560], w_in[:, 4480:5504], w_in[:, 4096:4480], pad],
        axis=1).astype(BF16)
    row = lambda x: x.reshape(1, -1).astype(F32)
    head_of = np.arange(W_MIX) % H_WKV
    w_branch = prm["w_branch"][i]
    w_branch = jnp.concatenate([w_branch[0], w_branch[1], perm(w_branch[2], axis=0), w_branch[3]], axis=0)
    return {
        "w_in": w_in_perm,
        "w_gate": w_in[:, 5504:].astype(BF16),
        "norm1_g": row(prm["norm1_g"][i]),
        "norm2_g": row(prm["norm2_g"][i]),
        "conv_a_w": prm["conv_a_w"][i],
        "conv_b_w": prm["conv_b_w"][i],
        "conv_b_b": row(prm["conv_b_b"][i]),
        "lru_w": jnp.concatenate([lru_bd(prm["lru_wa"][i]), lru_bd(prm["lru_wx"][i])], axis=1).astype(BF16),
        "lru_b": jnp.concatenate([row(prm["lru_ba"][i]), row(prm["lru_bx"][i])], axis=1),
        "lru_lam": row(prm["lru_lambda"][i]),
        "w0": row(perm(prm["rwkv_w0"][i])),
        "w2": lora_bd(perm(prm["rwkv_w2"][i])).astype(BF16),
        "a0": row(perm(prm["rwkv_a0"][i])),
        "a2": lora_bd(perm(prm["rwkv_a2"][i])).astype(BF16),
        "g2": perm(prm["rwkv_g2"][i]).astype(BF16),
        "kk": row(perm(prm["rwkv_kk"][i])),
        "ka": row(perm(prm["rwkv_ka"][i])),
        "rk": row(perm(prm["rwkv_rk"][i].reshape(W_MIX))),
        "lnx_g": row(perm(prm["lnx_g"][i])),
        "lnx_b": row(perm(prm["lnx_b"][i])),
        "sg_ln_g": row(prm["sg_ln_g"][i]),
        "sg_ln_b": row(prm["sg_ln_b"][i]),
        "sg_ws": prm["sg_ws"][i].astype(BF16),
        "sg_bst": prm["sg_bs"][i].T,
        "gate_b": row(prm["gate_b"][i]),
        "w_branch": w_branch.astype(BF16),
        "w_out": prm["w_out"][i].astype(BF16),
        "wq_t": prm["peer_wq"][i].T.astype(BF16),
        "keys": prm["peer_keys"][i].astype(BF16),
        "ind": jnp.asarray(head_of[:, None] == head_of[None, :], BF16),
    }


def kernel(x_prompt, x_sample, state_lru, state_wkv, c, c_ctx, norm1_g, norm2_g, w_mod, b_mod, w_in, conv_a_w, conv_b_w, conv_b_b, lru_wa, lru_ba, lru_wx, lru_bx, lru_lambda, rwkv_w0, rwkv_w2, rwkv_a0, rwkv_a2, rwkv_g2, rwkv_kk, rwkv_ka, rwkv_rk, lnx_g, lnx_b, sg_ln_g, sg_ln_b, sg_ws, sg_bs, gate_b, w_branch, w_out, peer_wq, peer_keys, peer_u, peer_v, final_norm_g):
    prm = dict(norm1_g=norm1_g, norm2_g=norm2_g, w_in=w_in, conv_a_w=conv_a_w, conv_b_w=conv_b_w,
               conv_b_b=conv_b_b, lru_wa=lru_wa, lru_ba=lru_ba, lru_wx=lru_wx, lru_bx=lru_bx,
               lru_lambda=lru_lambda, rwkv_w0=rwkv_w0, rwkv_w2=rwkv_w2, rwkv_a0=rwkv_a0, rwkv_a2=rwkv_a2,
               rwkv_g2=rwkv_g2, rwkv_kk=rwkv_kk, rwkv_ka=rwkv_ka, rwkv_rk=rwkv_rk, lnx_g=lnx_g,
               lnx_b=lnx_b, sg_ln_g=sg_ln_g, sg_ln_b=sg_ln_b, sg_ws=sg_ws, sg_bs=sg_bs, gate_b=gate_b,
               w_branch=w_branch, w_out=w_out, peer_wq=peer_wq, peer_keys=peer_keys, peer_u=peer_u,
               peer_v=peer_v)
    bc, lc, _ = x_prompt.shape
    bl, ll, _ = x_sample.shape
    depth = w_mod.shape[0]
    n_ctx_tok = bc * lc
    n_ctx_tiles = n_ctx_tok // TM
    lat_tiles = ll // TM
    ctx_spb = min(WKV_CTX_SPB, bc)
    lru_spb = ll // lc
    assert lc == TM and ll % TM_PEER == 0 and n_ctx_tok % TM_PEER == 0 and bl + 1 <= SUBLANES
    assert ll % GRID_W == 0 and TM % GRID_W == 0 and bc % ctx_spb == 0 and n_ctx_tok % ll == 0
    assert LANES % (ctx_spb * H_WKV) == 0 and LANES % (bl * H_WKV) == 0

    cond = jnp.zeros((SUBLANES, D_MODEL), F32).at[0].set(c_ctx).at[1:1 + bl].set(c)
    mods = _modulation(cond, w_mod, b_mod).reshape(depth, SUBLANES, N_MOD, D_MODEL)
    fng = final_norm_g.reshape(1, D_MODEL)
    x_ctx = x_prompt.reshape(n_ctx_tok, D_MODEL)
    x_lat = x_sample.reshape(bl * ll, D_MODEL)
    ctx_vs = LANES // (ctx_spb * H_WKV)
    lat_vs = LANES // (bl * H_WKV)
    n_sb = bc // ctx_spb
    wkv_zero = jnp.zeros((N_DIR * n_sb, HS_WKV // ctx_vs // SUBLANES, HS_WKV, SUBLANES, LANES), F32)
    lru_zero = jnp.zeros((n_ctx_tok // ll, lru_spb, N_DIR * W_MIX), F32)
    u_all = peer_u.astype(BF16)
    v_all = peer_v.astype(BF16)
    new_lru, new_wkv = [], []
    pnames = ["ya", "yd", "gbg", "la", "lu", "g", "bon", "rt", "vt", "kkt", "wt", "kt", "bt"]
    for i in range(depth):
        lw = _layer_weights(i, prm)
        mod = mods[i]
        p = dict(zip(pnames, _prep(x_ctx, x_lat, mod, lw, ll)))

        lat_h0 = jnp.zeros((bl, lru_spb, N_DIR * W_MIX), F32).at[:, 0].set(
            state_lru[:, i].astype(F32).reshape(bl, N_DIR * W_MIX))
        h, lru_s = _lru_scan(p["la"], p["lu"], jnp.concatenate([lru_zero, lat_h0], axis=0),
                             rows=ll, n_ctx_blocks=n_ctx_tok // ll, ctx_cfg=(lru_spb, lc), lat_cfg=(1, ll))
        new_lru.append(lru_s[:n_ctx_tok // ll].reshape(bc, N_DIR, W_MIX))

        wkv_in = [p[n] for n in ("rt", "wt", "kt", "kkt", "bt", "vt")]
        y_c, s_c = _wkv_scan(*wkv_in, wkv_zero, tile0=0, n_seq=bc, seq_tiles=1, spb=ctx_spb)
        y_l, _ = _wkv_scan(*wkv_in, _wkv_state_in(state_wkv[:, i].astype(F32), lat_vs),
                           tile0=n_ctx_tiles, n_seq=bl, seq_tiles=lat_tiles, spb=1)
        new_wkv.append(_wkv_state_out(s_c, n_sb, ctx_spb, ctx_vs))

        tok_in = [p["ya"], p["yd"], p["gbg"], h, p["g"], p["bon"]]
        x1, h2 = _merge(x_ctx, x_lat, mod, lw, tok_in, y_c, y_l, ll)
        x_ctx, x_lat = _peer(h2, lw["wq_t"], lw["keys"], u_all, v_all, i, x1, mod, fng,
                             n_ctx_tok, ll, final=(i == depth - 1))
    y_prompt = x_ctx.reshape(bc, lc, D_MODEL)
    y_sample = x_lat.reshape(bl, ll, D_MODEL)
    return (y_prompt, y_sample, jnp.stack(new_lru, axis=1), jnp.stack(new_wkv, axis=1))
```

```python
import functools

import numpy as np
import jax
import jax.numpy as jnp
from jax import lax
from jax.experimental import pallas as pl
from jax.experimental.pallas import tpu as pltpu

F32 = jnp.float32
BF16 = jnp.bfloat16
I32 = jnp.int32

D_MODEL = 1024
W_MIX = 512
N_DIR = 2
H_WKV = 8
HS_WKV = 64
H_LRU = 8
LORA_W = 64
LORA_A = 64
LORA_G = 128
GRID_W = 64
CHUNK = 128
G_SG = 4
N_KEYS = 128
PEER_HEADS = 8
PEER_TOPK = 16
N_MOD = 6
EPS = 1e-6
LNX_EPS = 64e-5
LRU_C = 8.0

LANES = 128
SUBLANES = 8
TM = 256
TM_PEER = 512
PEER_EB = 1024
PEER_SUB = 512
Z_COLS = 5632
WKV_TC = LANES
WKV_CTX_SPB = 8
VMEM_LIMIT = 56 * 1024 * 1024

_NT = (((1,), (1,)), ((), ()))


def _cparams(sem, vmem=None):
    return pltpu.CompilerParams(dimension_semantics=sem, vmem_limit_bytes=vmem)


def _const_spec(shape):
    nd = len(shape)
    return pl.BlockSpec(shape, lambda *_: (0,) * nd)


def _softplus(x):
    return jnp.maximum(x, 0.0) + jnp.log1p(jnp.exp(-jnp.abs(x)))


def _rms(x, g):
    return x * lax.rsqrt(jnp.mean(x * x, axis=-1, keepdims=True) + EPS) * g


def _segsum(x, ind):
    hi = x.astype(BF16)
    lo = (x - hi.astype(F32)).astype(BF16)
    return (jnp.dot(hi, ind, preferred_element_type=F32)
            + jnp.dot(lo, ind, preferred_element_type=F32))


def _mod_index(i, tm, n_ctx_tok, lat_len):
    n_ctx_tiles = n_ctx_tok // tm
    tiles_per_seq = lat_len // tm
    return jnp.where(i < n_ctx_tiles, 0, 1 + lax.div(i - n_ctx_tiles, tiles_per_seq))


def _mod_body(s_ref, w_ref, b_ref, o_ref):
    s = s_ref[...]
    s = s * jax.nn.sigmoid(s)
    o_ref[0] = jnp.dot(s.astype(BF16), w_ref[0].astype(BF16), preferred_element_type=F32) + b_ref[0]


def _modulation(cond, w_mod, b_mod):
    depth = w_mod.shape[0]
    n = w_mod.shape[2]
    tn = 1536
    return pl.pallas_call(
        _mod_body,
        grid=(depth, n // tn),
        in_specs=[_const_spec((SUBLANES, D_MODEL)),
                  pl.BlockSpec((1, D_MODEL, tn), lambda l, j: (l, 0, j)),
                  pl.BlockSpec((1, 1, tn), lambda l, j: (l, 0, j))],
        out_specs=pl.BlockSpec((1, SUBLANES, tn), lambda l, j: (l, 0, j)),
        out_shape=jax.ShapeDtypeStruct((depth, SUBLANES, n), F32),
        compiler_params=_cparams(("parallel", "parallel"), VMEM_LIMIT),
        name="modulation",
    )(cond, w_mod, b_mod.reshape(depth, 1, n))


def _prep_body(xc_ref, xl_ref, xp_ref, xn_ref, mod_ref, n1g_ref, win_ref,
               caw_ref, cbw_ref, cbb_ref, lruw_ref, lrub_ref, lam_ref,
               w0_ref, w2_ref, a0_ref, a2_ref, g2_ref, kkw_ref, ka_ref, rk_ref,
               lng_ref, lnb_ref, ws_ref, bst_ref, ind_ref,
               ya_ref, yd_ref, gbg_ref, la_ref, lu_ref, g_ref, bon_ref,
               rt_ref, vt_ref, kkt_ref, wt_ref, kt_ref, bt_ref,
               *, n_ctx_tiles, tiles_per_seq):
    i = pl.program_id(0)
    is_ctx = i < n_ctx_tiles
    t = lax.broadcasted_iota(I32, (TM, 1), 0)
    ind = ind_ref[...]
    m = mod_ref[0]

    def modulated(xv):
        return (_rms(xv, n1g_ref[...]) * (1.0 + m[1:2, :]) + m[0:1, :]).astype(BF16)

    def project(hv, lo, hi):
        return jnp.dot(hv, win_ref[:, lo:hi], preferred_element_type=F32)

    h = modulated(jnp.where(is_ctx, xc_ref[...], xl_ref[...]))
    za = project(h, 0, 1536)
    zc = project(h, 1536, 3072)
    zb = project(h, 3072, 4096)
    zd = project(h, 4096, 5120)
    zl = project(h, 5120, 5120 + 2 * LORA_W + 2 * LORA_A + LORA_G)
    halo_prev = project(modulated(xp_ref[...]), 3072 + W_MIX, 4096)
    halo_next = project(modulated(xn_ref[...]), 3072 + W_MIX, 4096)

    pm = jnp.where(is_ctx, TM - 1, GRID_W - 1)
    pos = t & pm
    a_b = za[:, 0:W_MIX]
    ac = za[:, W_MIX:2 * W_MIX] * za[:, 2 * W_MIX:3 * W_MIX]
    up = jnp.where(pos == 0, 0.0, pltpu.roll(ac, 1, 0))
    dn = jnp.where(pos == pm, 0.0, pltpu.roll(ac, TM - 1, 0))
    ya_ref[...] = a_b * (caw_ref[0:1, :] * up + caw_ref[1:2, :] * ac + caw_ref[2:3, :] * dn)

    seq_tile = lax.rem(jnp.maximum(i - n_ctx_tiles, 0), tiles_per_seq)
    first = jnp.logical_or(is_ctx, seq_tile == 0)
    last = jnp.logical_or(is_ctx, seq_tile == tiles_per_seq - 1)
    prev = jnp.where(first, 0.0, halo_prev[SUBLANES - 1:SUBLANES, :])
    nxt0 = jnp.where(last, 0.0, halo_next[0:1, :])
    nxt1 = jnp.where(last, 0.0, halo_next[1:2, :])
    bx = zb[:, W_MIX:2 * W_MIX]
    m1 = jnp.where(t == 0, prev, pltpu.roll(bx, 1, 0))
    p1 = jnp.where(t == TM - 1, nxt0, pltpu.roll(bx, TM - 1, 0))
    p2 = jnp.where(t == TM - 2, nxt0, jnp.where(t == TM - 1, nxt1, pltpu.roll(bx, TM - 2, 0)))
    xb = (cbw_ref[0:1, :] * m1 + cbw_ref[1:2, :] * bx + cbw_ref[2:3, :] * p1
          + cbw_ref[3:4, :] * p2 + cbb_ref[...])
    gates = jnp.dot(xb.astype(BF16), lruw_ref[...], preferred_element_type=F32) + lrub_ref[...]
    rg = jax.nn.sigmoid(gates[:, 0:2 * W_MIX])
    ig = jax.nn.sigmoid(gates[:, 2 * W_MIX:4 * W_MIX])
    log_a = -LRU_C * rg * _softplus(-lam_ref[...])
    xb2 = jnp.concatenate([xb, xb], axis=1)
    a = jnp.exp(log_a)
    la_ref[...] = a
    lu_ref[...] = jnp.sqrt(jnp.tanh(-log_a) * (a * a + 1.0)) * (ig * xb2)
    gbg_ref[...] = jax.nn.gelu(zb[:, 0:W_MIX])

    zr = zc[:, 0:W_MIX]
    zk = zc[:, W_MIX:2 * W_MIX]
    zv = zc[:, 2 * W_MIX:3 * W_MIX]
    zwd = zl[:, 0:2 * LORA_W]
    zad = zl[:, 2 * LORA_W:2 * LORA_W + 2 * LORA_A]
    zgd = zl[:, 2 * LORA_W + 2 * LORA_A:2 * LORA_W + 2 * LORA_A + LORA_G]
    wlin = w0_ref[...] + jnp.dot(jnp.tanh(zwd).astype(BF16), w2_ref[...], preferred_element_type=F32)
    wt_ref[0] = jnp.exp(-jnp.exp(-_softplus(-wlin) - 0.5)).T
    av = jax.nn.sigmoid(a0_ref[...] + jnp.dot(zad.astype(BF16), a2_ref[...], preferred_element_type=F32))
    g_ref[...] = jnp.dot(jax.nn.sigmoid(zgd).astype(BF16), g2_ref[...], preferred_element_type=F32)
    kkr = zk * kkw_ref[...]
    kkn = kkr / jnp.maximum(jnp.sqrt(_segsum(kkr * kkr, ind)), 1e-12)
    zk2 = jnp.concatenate([zk, zk], axis=1)
    ka2 = jnp.concatenate([ka_ref[...], ka_ref[...]], axis=1)
    kd = zk2 * (1.0 + (av - 1.0) * ka2)
    kt_ref[0] = kd.T
    bt_ref[0] = (jnp.concatenate([kkn, kkn], axis=1) * av).T
    rt_ref[0] = zr.T
    vt_ref[0] = zv.T
    kkt_ref[0] = kkn.T
    bon_ref[...] = _segsum(zr * (kd[:, 0:W_MIX] + kd[:, W_MIX:2 * W_MIX]) * rk_ref[...], ind) * zv

    zg = jax.nn.gelu(zd)
    u = zg[:, 0:W_MIX]
    vv = zg[:, W_MIX:2 * W_MIX]
    vc = vv - jnp.mean(vv, axis=-1, keepdims=True)
    vn = vc * lax.rsqrt(jnp.mean(vc * vc, axis=-1, keepdims=True) + 1e-5) * lng_ref[...] + lnb_ref[...]
    for c in range(TM // CHUNK):
        rs = slice(c * CHUNK, (c + 1) * CHUNK)
        for gi in range(G_SG):
            cs = slice(gi * LANES, (gi + 1) * LANES)
            s = jnp.dot(ws_ref[gi], vn[rs, cs].astype(BF16), preferred_element_type=F32)
            yd_ref[rs, cs] = u[rs, cs] * (s + bst_ref[:, gi:gi + 1])


def _dual_specs(rows, n_ctx_blocks, **kw):
    return [pl.BlockSpec((rows, D_MODEL), lambda i, *_: (jnp.minimum(i, n_ctx_blocks - 1), 0), **kw),
            pl.BlockSpec((rows, D_MODEL), lambda i, *_: (jnp.maximum(i - n_ctx_blocks, 0), 0), **kw)]


def _prep(x_ctx, x_lat, mod, lw, lat_len):
    n_ctx_tok = x_ctx.shape[0]
    t = n_ctx_tok + x_lat.shape[0]
    n_tiles = t // TM
    n_ctx_tiles = n_ctx_tok // TM
    tiles_per_seq = lat_len // TM
    rows8 = TM // SUBLANES
    last_blk = x_lat.shape[0] // SUBLANES - 1
    midx = functools.partial(_mod_index, tm=TM, n_ctx_tok=n_ctx_tok, lat_len=lat_len)

    def lat_blk8(i, off):
        return (jnp.clip((i - n_ctx_tiles) * rows8 + off, 0, last_blk), 0)

    x_specs = _dual_specs(TM, n_ctx_tiles) + [
        pl.BlockSpec((SUBLANES, D_MODEL), lambda i: lat_blk8(i, -1)),
        pl.BlockSpec((SUBLANES, D_MODEL), lambda i: lat_blk8(i, rows8)),
        pl.BlockSpec((1, N_MOD, D_MODEL), lambda i: (midx(i), 0, 0)),
        _const_spec((1, D_MODEL)),
        pl.BlockSpec(lw["w_in"].shape, lambda i: (0, 0), pipeline_mode=pl.Buffered(1)),
    ]
    wnames = ["conv_a_w", "conv_b_w", "conv_b_b", "lru_w", "lru_b", "lru_lam", "w0", "w2", "a0", "a2",
              "g2", "kk", "ka", "rk", "sg_ln_g", "sg_ln_b", "sg_ws", "sg_bst", "ind"]
    wts = [lw[n] for n in wnames]
    w_specs = [_const_spec(w.shape) for w in wts]
    widths = [W_MIX, W_MIX, W_MIX, 2 * W_MIX, 2 * W_MIX, W_MIX, W_MIX]
    t_rows = [W_MIX, W_MIX, W_MIX, 2 * W_MIX, 2 * W_MIX, 2 * W_MIX]
    out_specs = ([pl.BlockSpec((TM, wd), lambda i: (i, 0)) for wd in widths]
                 + [pl.BlockSpec((1, r, TM), lambda i: (i, 0, 0)) for r in t_rows])
    out_shape = ([jax.ShapeDtypeStruct((t, wd), F32) for wd in widths]
                 + [jax.ShapeDtypeStruct((n_tiles, r, TM), F32) for r in t_rows])
    return pl.pallas_call(
        functools.partial(_prep_body, n_ctx_tiles=n_ctx_tiles, tiles_per_seq=tiles_per_seq),
        grid=(n_tiles,),
        in_specs=x_specs + w_specs,
        out_specs=out_specs,
        out_shape=out_shape,
        compiler_params=_cparams(("parallel",), VMEM_LIMIT),
        name="branch_prep",
    )(x_ctx, x_lat, x_lat, x_lat, mod, lw["norm1_g"], lw["w_in"], *wts)


def _lru_body(a_ref, u_ref, h0_ref, h_ref, hf_ref, *, n_ctx_blocks, ctx_cfg, lat_cfg):
    fw, bw = slice(0, W_MIX), slice(W_MIX, 2 * W_MIX)

    def scan(nseq, l):
        def step(s, carry):
            out = []
            for j in range(nseq):
                tf = j * l + s
                tb = j * l + (l - 1 - s)
                hf = a_ref[pl.ds(tf, 1), fw] * carry[2 * j] + u_ref[pl.ds(tf, 1), fw]
                hb = a_ref[pl.ds(tb, 1), bw] * carry[2 * j + 1] + u_ref[pl.ds(tb, 1), bw]
                h_ref[pl.ds(tf, 1), fw] = hf
                h_ref[pl.ds(tb, 1), bw] = hb
                out += [hf, hb]
            return tuple(out)

        init = []
        for j in range(nseq):
            init += [h0_ref[0, j:j + 1, fw], h0_ref[0, j:j + 1, bw]]
        fin = lax.fori_loop(0, l, step, tuple(init), unroll=2)
        hf_ref[0] = h0_ref[0]
        for j in range(nseq):
            hf_ref[0, j:j + 1, fw] = fin[2 * j]
            hf_ref[0, j:j + 1, bw] = fin[2 * j + 1]

    is_ctx = pl.program_id(0) < n_ctx_blocks
    pl.when(is_ctx)(lambda: scan(*ctx_cfg))
    pl.when(jnp.logical_not(is_ctx))(lambda: scan(*lat_cfg))


def _lru_scan(a, u, h0, *, rows, n_ctx_blocks, ctx_cfg, lat_cfg):
    nb = a.shape[0] // rows
    w = a.shape[1]
    tok = pl.BlockSpec((rows, w), lambda i: (i, 0))
    st = pl.BlockSpec((1,) + h0.shape[1:], lambda i: (i, 0, 0))
    return pl.pallas_call(
        functools.partial(_lru_body, n_ctx_blocks=n_ctx_blocks, ctx_cfg=ctx_cfg, lat_cfg=lat_cfg),
        grid=(nb,),
        in_specs=[tok, tok, st],
        out_specs=[tok, st],
        out_shape=[jax.ShapeDtypeStruct(a.shape, F32), jax.ShapeDtypeStruct(h0.shape, F32)],
        compiler_params=_cparams(("parallel",), VMEM_LIMIT),
        name="lru_scan",
    )(a, u, h0)


def _wkv_body(*refs, nsrc, spb, vs, n_sb, tc, kp, vp):
    vl_n = HS_WKV // vs
    n_in = 6 * nsrc
    k_srcs = [refs[o * nsrc:(o + 1) * nsrc] for o in range(5)]
    v_srcs = refs[5 * nsrc:n_in]
    s0_ref = refs[n_in]
    y_ref = refs[n_in + 1]
    sf_ref = refs[n_in + 2]
    kall_scr, v_scr, y_scr, s_scr = refs[n_in + 3:n_in + 7]
    slab_rows = tc * kp + SUBLANES
    slot_of = (0, 1, 2, 4, 3)
    k_scr = [kall_scr.at[pl.ds(slot_of[o] * slab_rows, tc * kp)] for o in range(5)]
    r_scr, w_scr, k_scr_, kk_scr, b_scr = k_scr
    backward = pl.program_id(0) // n_sb == 1
    seqs = [(s, j) for s in range(nsrc) for j in range(spb)]

    @pl.when(pl.program_id(1) == 0)
    def _():
        s_scr[...] = s0_ref[0]

    def build_k(c, carry):
        row = pl.multiple_of(c * H_WKV, H_WKV)
        for o in range(5):
            slab = [k_srcs[o][s][j, pl.ds(row, H_WKV), :] for s, j in seqs]
            k_scr[o][pl.ds(c, tc, stride=kp), :] = jnp.concatenate(slab * vs, axis=0).T
        return carry

    lax.fori_loop(0, HS_WKV, build_k, 0, unroll=4)

    def build_v(vl, carry):
        slab = []
        for vsi in range(vs):
            row = pl.multiple_of((vsi * vl_n + vl) * H_WKV, H_WKV)
            slab += [v_srcs[s][j, pl.ds(row, H_WKV), :] for s, j in seqs]
        v_scr[pl.ds(vl, tc, stride=vp), :] = jnp.concatenate(slab, axis=0).T
        return carry

    lax.fori_loop(0, vl_n, build_v, 0, unroll=4)

    def step(s, carry):
        t = jnp.where(backward, tc - 1 - s, s)
        krow = pl.multiple_of(t * kp, SUBLANES)
        vrow = pl.multiple_of(t * vp, SUBLANES)
        n_g = vl_n // SUBLANES
        batch = min(n_g, 2)
        n_acc = 4 // batch

        def row(ref, k):
            return jnp.broadcast_to(ref[pl.ds(krow + k, 1), :], (SUBLANES, LANES))

        def total(parts):
            while len(parts) > 1:
                parts = [a + b for a, b in zip(parts[0::2], parts[1::2])]
            return parts[0]

        def accumulate(acc, g, k, p):
            acc[g][k % n_acc] = p if acc[g][k % n_acc] is None else acc[g][k % n_acc] + p

        for g0 in range(0, n_g, batch):
            gs = range(g0, g0 + batch)
            acc = {g: [None] * n_acc for g in gs}
            for k in range(HS_WKV):
                kk = row(kk_scr, k)
                for g in gs:
                    accumulate(acc, g, k, s_scr[g, k] * kk)
            sa = {g: total(acc[g]) for g in gs}
            vv = {g: v_scr[pl.ds(pl.multiple_of(vrow + g * SUBLANES, SUBLANES), SUBLANES), :] for g in gs}
            acc = {g: [None] * n_acc for g in gs}
            for k in range(HS_WKV):
                w, b, kx, r = row(w_scr, k), row(b_scr, k), row(k_scr_, k), row(r_scr, k)
                for g in gs:
                    sn = s_scr[g, k] * w - sa[g] * b + vv[g] * kx
                    s_scr[g, k] = sn
                    accumulate(acc, g, k, sn * r)
            for g in gs:
                y_scr[pl.ds(pl.multiple_of(vrow + g * SUBLANES, SUBLANES), SUBLANES), :] = total(acc[g])
        return carry

    lax.fori_loop(0, tc, step, 0)

    def emit_y(vl, carry):
        yt = y_scr[pl.ds(vl, tc, stride=vp), :].T
        for vsi in range(vs):
            row = pl.multiple_of((vsi * vl_n + vl) * H_WKV, H_WKV)
            for n, (s, j) in enumerate(seqs):
                lane0 = (vsi * len(seqs) + n) * H_WKV
                y_ref[0, 0, s * spb + j, pl.ds(row, H_WKV), :] = yt[lane0:lane0 + H_WKV, :]
        return carry

    lax.fori_loop(0, vl_n, emit_y, 0, unroll=4)
    sf_ref[0] = s_scr[...]


def _wkv_scan(rt, wt, kt, kkt, bt, vt, s0, *, tile0, n_seq, seq_tiles, spb):
    tc = WKV_TC
    if spb > 1:
        assert seq_tiles == 1 and n_seq % spb == 0 and tile0 % spb == 0
        nsrc, n_sb = 1, n_seq // spb
    else:
        nsrc, n_sb = n_seq, 1
    inst = nsrc * spb * H_WKV
    vs = LANES // inst
    vl_n = HS_WKV // vs
    assert vl_n % SUBLANES == 0, "value rows are processed eight at a time"
    cpt = TM // tc
    n_chunks = seq_tiles * cpt
    kp = HS_WKV + SUBLANES
    vp = vl_n + SUBLANES if ((vl_n + SUBLANES) // SUBLANES) % 2 else vl_n + 2 * SUBLANES

    def chunk(g, i):
        return jnp.where(g // n_sb == 1, n_chunks - 1 - i, i)

    def in_map(g, i, *, src, per_dir):
        ce = chunk(g, i)
        rb = (g // n_sb) if per_dir else 0
        if spb > 1:
            return (tile0 // spb + g % n_sb, rb, ce)
        return (tile0 + src * seq_tiles + ce // cpt, rb, ce % cpt)

    def out_map(g, i):
        ce = chunk(g, i)
        if spb > 1:
            return (g // n_sb, 0, g % n_sb, 0, ce)
        return (g // n_sb, ce // cpt, 0, 0, ce % cpt)

    in_specs, operands = [], []
    for arr, per_dir in ((rt, False), (wt, True), (kt, True), (kkt, False), (bt, True), (vt, False)):
        for src in range(nsrc):
            in_specs.append(pl.BlockSpec((spb, W_MIX, tc), functools.partial(in_map, src=src, per_dir=per_dir),
                                         pipeline_mode=pl.Buffered(1)))
            operands.append(arr)
    sspec = pl.BlockSpec((1, vl_n // SUBLANES, HS_WKV, SUBLANES, LANES), lambda g, i: (g, 0, 0, 0, 0))
    in_specs.append(sspec)
    out_specs = [pl.BlockSpec((1, 1, nsrc * spb, W_MIX, tc), out_map)]
    out_shape = [jax.ShapeDtypeStruct((N_DIR, seq_tiles, n_seq, W_MIX, TM), F32)]
    res = pl.pallas_call(
        functools.partial(_wkv_body, nsrc=nsrc, spb=spb, vs=vs, n_sb=n_sb, tc=tc, kp=kp, vp=vp),
        grid=(N_DIR * n_sb, n_chunks),
        in_specs=in_specs,
        out_specs=out_specs + [sspec],
        out_shape=out_shape + [jax.ShapeDtypeStruct(s0.shape, F32)],
        scratch_shapes=([pltpu.VMEM((5 * (tc * kp + SUBLANES), LANES), F32)]
                        + [pltpu.VMEM((tc * vp, LANES), F32)] * 2
                        + [pltpu.VMEM((vl_n // SUBLANES, HS_WKV, SUBLANES, LANES), F32)]),
        compiler_params=_cparams(("parallel", "arbitrary"), VMEM_LIMIT),
        name="wkv_scan",
    )(*operands, s0)
    return res[0], res[1]


def _merge_body(xc_ref, xl_ref, mod_ref, n1g_ref, n2g_ref, wg_ref, gb_ref, wbr_ref, wo_ref,
                lnxg_ref, lnxb_ref, ind_ref,
                ya_ref, yd_ref, gbg_ref, h_ref, ycf_ref, ycb_ref, ylf_ref, ylb_ref, g_ref, bon_ref,
                x1_ref, h2_ref, *, n_ctx_tiles):
    is_ctx = pl.program_id(0) < n_ctx_tiles
    x = jnp.where(is_ctx, xc_ref[...], xl_ref[...])
    m = mod_ref[0]
    ind = ind_ref[...]
    h = (_rms(x, n1g_ref[...]) * (1.0 + m[1:2, :]) + m[0:1, :]).astype(BF16)
    y_b = gbg_ref[...] * (h_ref[:, 0:W_MIX] + h_ref[:, W_MIX:2 * W_MIX])
    y = jnp.where(is_ctx, ycf_ref[0, 0, 0] + ycb_ref[0, 0, 0], ylf_ref[0, 0, 0] + ylb_ref[0, 0, 0]).T
    yc = y - _segsum(y, ind) * (1.0 / HS_WKV)
    var = _segsum(yc * yc, ind) * (1.0 / HS_WKV)
    y_c = (yc * lax.rsqrt(var + LNX_EPS) * lnxg_ref[...] + lnxb_ref[...] + bon_ref[...]) * g_ref[...]
    merged = None
    for n, yn in enumerate((ya_ref[...], y_b, y_c, yd_ref[...])):
        cs = slice(n * D_MODEL, (n + 1) * D_MODEL)
        gate = jax.nn.sigmoid(jnp.dot(h, wg_ref[:, cs], preferred_element_type=F32) + gb_ref[:, cs])
        br = jnp.dot(yn.astype(BF16), wbr_ref[n * W_MIX:(n + 1) * W_MIX, :], preferred_element_type=F32)
        merged = gate * br if merged is None else merged + gate * br
    mo = jnp.dot(merged.astype(BF16), wo_ref[...], preferred_element_type=F32)
    x1 = x + m[2:3, :] * mo
    x1_ref[...] = x1
    h2_ref[...] = (_rms(x1, n2g_ref[...]) * (1.0 + m[4:5, :]) + m[3:4, :]).astype(BF16)


def _merge(x_ctx, x_lat, mod, lw, tok_in, y_ctx, y_lat, lat_len):
    n_ctx_tok = x_ctx.shape[0]
    t = n_ctx_tok + x_lat.shape[0]
    n_ctx_tiles = n_ctx_tok // TM
    tps = lat_len // TM
    midx = functools.partial(_mod_index, tm=TM, n_ctx_tok=n_ctx_tok, lat_len=lat_len)
    wnames = ["norm1_g", "norm2_g", "w_gate", "gate_b", "w_branch", "w_out", "lnx_g", "lnx_b", "ind"]
    wts = [lw[n] for n in wnames]
    tok = lambda wd: pl.BlockSpec((TM, wd), lambda i: (i, 0))
    ya, yd, gbg, h, g, bon = tok_in
    yblock = (1, 1, 1, W_MIX, TM)

    def ctx_spec(d):
        return pl.BlockSpec(yblock, lambda i: (d, 0, jnp.minimum(i, n_ctx_tiles - 1), 0, 0))

    def lat_spec(d):
        def imap(i):
            r = jnp.maximum(i - n_ctx_tiles, 0)
            return (d, lax.rem(r, tps), lax.div(r, tps), 0, 0)
        return pl.BlockSpec(yblock, imap)

    return pl.pallas_call(
        functools.partial(_merge_body, n_ctx_tiles=n_ctx_tiles),
        grid=(t // TM,),
        in_specs=(_dual_specs(TM, n_ctx_tiles)
                  + [pl.BlockSpec((1, N_MOD, D_MODEL), lambda i: (midx(i), 0, 0))]
                  + [_const_spec(w.shape) for w in wts]
                  + [tok(W_MIX), tok(W_MIX), tok(W_MIX), tok(2 * W_MIX),
                     ctx_spec(0), ctx_spec(1), lat_spec(0), lat_spec(1), tok(W_MIX), tok(W_MIX)]),
        out_specs=[tok(D_MODEL), tok(D_MODEL)],
        out_shape=[jax.ShapeDtypeStruct((t, D_MODEL), F32), jax.ShapeDtypeStruct((t, D_MODEL), BF16)],
        compiler_params=_cparams(("parallel",), VMEM_LIMIT),
        name="merge",
    )(x_ctx, x_lat, mod, *wts, ya, yd, gbg, h, y_ctx, y_ctx, y_lat, y_lat, g, bon)


_CAND_VALID = (8, 8, 8, 5, 4, 3, 2, 2, 2, 8)


def _oddeven_pairs(n):
    def merge(lo, hi, r):
        step = r * 2
        if step < hi - lo:
            yield from merge(lo, hi, step)
            yield from merge(lo + r, hi, step)
            yield from [(i, i + r) for i in range(lo + r, hi - r, step)]
        else:
            yield (lo, lo + r)

    def sort(lo, hi):
        if hi - lo >= 1:
            mid = lo + (hi - lo) // 2
            yield from sort(lo, mid)
            yield from sort(mid + 1, hi)
            yield from merge(lo, hi, 1)

    return tuple(sort(0, n - 1))


_SORT16 = _oddeven_pairs(N_KEYS // SUBLANES)


def _route_head(qs, keys_ref):
    sub = lax.broadcasted_iota(I32, (SUBLANES, LANES), 0)
    kid = lax.broadcasted_iota(I32, (PEER_TOPK, LANES), 0)
    neg = -jnp.inf

    def bc(x, r):
        return jnp.broadcast_to(x[r:r + 1, :], (SUBLANES, LANES))

    def head():
        tops = []
        for p in range(2):
            s = jnp.dot(keys_ref[p], qs[p], preferred_element_type=F32)
            cols = [s[j * SUBLANES:(j + 1) * SUBLANES, :] for j in range(N_KEYS // SUBLANES)]
            cidx = [sub + j * SUBLANES for j in range(N_KEYS // SUBLANES)]
            for a, b in _SORT16:
                take = cols[b] > cols[a]
                cols[a], cols[b] = jnp.where(take, cols[b], cols[a]), jnp.where(take, cols[a], cols[b])
                cidx[a], cidx[b] = jnp.where(take, cidx[b], cidx[a]), jnp.where(take, cidx[a], cidx[b])
            vals = jnp.zeros((PEER_TOPK, LANES), F32)
            idxs = jnp.zeros((PEER_TOPK, LANES), I32)
            for r in range(PEER_TOPK):
                v8, i8 = cols[0], cidx[0]
                for sh in (4, 2, 1):
                    vr, ir = pltpu.roll(v8, sh, 0), pltpu.roll(i8, sh, 0)
                    take = vr > v8
                    v8, i8 = jnp.where(take, vr, v8), jnp.where(take, ir, i8)
                m, ix = v8[0:1, :], i8[0:1, :]
                vals = jnp.where(kid == r, m, vals)
                idxs = jnp.where(kid == r, ix, idxs)
                popped = cidx[0] == ix
                for j in range(PEER_TOPK - 1 - r):
                    cols[j] = jnp.where(popped, cols[j + 1], cols[j])
                    cidx[j] = jnp.where(popped, cidx[j + 1], cidx[j])
            tops.append((vals, idxs))
        (a0, i0), (a1, i1) = tops
        lo, hi = slice(0, SUBLANES), slice(SUBLANES, 2 * SUBLANES)
        slabs = [bc(a0, 0) + a1[lo], bc(a0, 0) + a1[hi]]
        ci = [bc(i0, 0), bc(i0, 0)]
        cj = [i1[lo], i1[hi]]
        for r in range(1, SUBLANES):
            slabs.append(bc(a0, r) + a1[lo])
            ci.append(bc(i0, r))
            cj.append(i1[lo])
        slabs.append(a0[hi] + bc(a1, 0))
        ci.append(i0[hi])
        cj.append(bc(i1, 0))
        slabs = [jnp.where(sub < nv, sl, neg) for sl, nv in zip(slabs, _CAND_VALID)]
        ids = [a * N_KEYS + b for a, b in zip(ci, cj)]
        vals = jnp.zeros((PEER_TOPK, LANES), F32)
        esel = jnp.zeros((PEER_TOPK, LANES), I32)
        for r in range(PEER_TOPK):
            level = list(zip(slabs, ids))
            while len(level) > 1:
                nxt = []
                for (va, ea), (vb, eb) in zip(level[0::2], level[1::2]):
                    take = vb > va
                    nxt.append((jnp.where(take, vb, va), jnp.where(take, eb, ea)))
                if len(level) % 2:
                    nxt.append(level[-1])
                level = nxt
            v8, e8 = level[0]
            for sh in (4, 2, 1):
                vr, er = pltpu.roll(v8, sh, 0), pltpu.roll(e8, sh, 0)
                take = vr > v8
                v8, e8 = jnp.where(take, vr, v8), jnp.where(take, er, e8)
            m, ex = v8[0:1, :], e8[0:1, :]
            slabs = [jnp.where(eid == ex, neg, sl) for sl, eid in zip(slabs, ids)]
            vals = jnp.where(kid == r, m, vals)
            esel = jnp.where(kid == r, ex, esel)
        e = jnp.exp(vals - vals[0:1, :])
        return esel, e / jnp.sum(e, axis=0, keepdims=True)

    return head()


def _peer_body(h2_ref, h2n_ref, wqt_ref, keys_ref, u_ref, v_ref, x1_ref, mod_ref, fng_ref,
               oc_ref, ol_ref,
               q_scr, e_scr, g_scr, et_scr, gt_scr, gs_scr, acc_scr,
               *, rows, pitch, units, n_ctx_tiles, final):
    m = pl.program_id(0)
    e = pl.program_id(1)
    tm = h2_ref.shape[0]
    n_chunks = tm // LANES
    nsel = PEER_HEADS * PEER_TOPK
    slot = lax.rem(m, 2)

    def project_queries(src_ref):
        q = lax.dot_general(wqt_ref[...], src_ref[...], _NT, preferred_element_type=F32).astype(BF16)
        for c in range(n_chunks):
            q_scr[c] = q[:, c * LANES:(c + 1) * LANES]

    def route_unit(u, dst):
        c = u // PEER_HEADS
        h = lax.rem(u, PEER_HEADS)
        qs = [q_scr[c, pl.ds(pl.multiple_of(h * (2 * N_KEYS) + p * N_KEYS, N_KEYS), N_KEYS), :]
              for p in range(2)]
        esel, gates = _route_head(qs, keys_ref)
        row = pl.multiple_of(h * PEER_TOPK, PEER_TOPK)
        e_scr[dst, c, pl.ds(row, PEER_TOPK), :] = esel
        g_scr[dst, c, pl.ds(row, PEER_TOPK), :] = gates

    @pl.when(jnp.logical_and(e == 0, m == 0))
    def _first_tile_routing():
        project_queries(h2_ref)

        def unit(u, c):
            route_unit(u, 0)
            return c

        lax.fori_loop(0, n_chunks * PEER_HEADS, unit, 0)

    @pl.when(e == 0)
    def _build():
        for c in range(n_chunks):
            et_scr[c * LANES:(c + 1) * LANES, :] = e_scr[slot, c].T
            gt_scr[c * LANES:(c + 1) * LANES, :] = g_scr[slot, c].T
        kio = lax.broadcasted_iota(I32, (N_KEYS, nsel), 0)

        def tok(t, c):
            erow = et_scr[pl.ds(t, 1), :]
            grow = gt_scr[pl.ds(t, 1), :]
            at = jnp.where(kio == (erow >> 7), grow, 0.0).astype(BF16)
            bt = jnp.where(kio == (erow & (N_KEYS - 1)), 1.0, 0.0).astype(BF16)
            gt = lax.dot_general(at, bt, _NT, preferred_element_type=F32)
            hi = pltpu.bitcast(gt[0:rows, :], jnp.uint32) & jnp.uint32(0xFFFF0000)
            lo = pltpu.bitcast(gt[rows:2 * rows, :], jnp.uint32) >> 16
            gs_scr[pl.ds(pl.multiple_of(t * pitch, SUBLANES), rows), :] = hi | lo
            return c

        lax.fori_loop(0, tm, tok, 0, unroll=128)
        acc_scr[...] = jnp.zeros_like(acc_scr)
        project_queries(h2n_ref)

    for k in range(units):
        route_unit(e * units + k, 1 - slot)

    per_sub = PEER_SUB // N_KEYS
    per_step = u_ref.shape[0] // N_KEYS
    steps_per_half = rows // per_step
    row0 = lax.rem(e, steps_per_half) * per_step
    shift = jnp.where(e < steps_per_half, 0, 16).astype(jnp.uint32)
    h2 = h2_ref[...]
    total = None
    for sb in range(per_step // per_sub):
        es = slice(sb * PEER_SUB, (sb + 1) * PEER_SUB)
        hmat = lax.dot_general(h2, u_ref[es, :], _NT, preferred_element_type=F32)
        words = jnp.concatenate(
            [gs_scr[pl.ds(row0 + sb * per_sub + ii, tm, stride=pitch), :] for ii in range(per_sub)], axis=1)
        gm = pltpu.bitcast((words << shift) & jnp.uint32(0xFFFF0000), F32)
        act = jax.nn.gelu(hmat.astype(BF16)) * gm.astype(BF16)
        part = jnp.dot(act, v_ref[es, :], preferred_element_type=F32)
        total = part if total is None else total + part
    acc_scr[...] += total

    def result():
        x2 = x1_ref[...] + mod_ref[0][5:6, :] * acc_scr[...]
        return _rms(x2, fng_ref[...]) if final else x2

    last = e == pl.num_programs(1) - 1

    @pl.when(jnp.logical_and(last, m < n_ctx_tiles))
    def _out_ctx():
        oc_ref[...] = result()

    @pl.when(jnp.logical_and(last, m >= n_ctx_tiles))
    def _out_lat():
        ol_ref[...] = result()


def _peer(h2, wqt, keys, u, v, layer, x1, mod, fng, n_ctx_tok, lat_len, final):
    t = h2.shape[0]
    nsel = PEER_HEADS * PEER_TOPK
    rows = N_KEYS // 2
    pitch = rows + SUBLANES
    n_e = (N_KEYS * N_KEYS) // PEER_EB
    n_m = t // TM_PEER
    n_chunks = TM_PEER // LANES
    units = (n_chunks * PEER_HEADS) // n_e
    assert units * n_e == n_chunks * PEER_HEADS
    midx = functools.partial(_mod_index, tm=TM_PEER, n_ctx_tok=n_ctx_tok, lat_len=lat_len)
    tok = lambda wd: pl.BlockSpec((TM_PEER, wd), lambda m, e: (m, 0))
    nxt = pl.BlockSpec((TM_PEER, D_MODEL), lambda m, e: (jnp.minimum(m + 1, n_m - 1), 0),
                       pipeline_mode=pl.Buffered(1))
    espec = pl.BlockSpec((None, PEER_EB, D_MODEL), lambda m, e: (layer, e, 0))
    single = dict(pipeline_mode=pl.Buffered(1))
    return pl.pallas_call(
        functools.partial(_peer_body, rows=rows, pitch=pitch, units=units,
                          n_ctx_tiles=n_ctx_tok // TM_PEER, final=final),
        grid=(n_m, n_e),
        in_specs=[tok(D_MODEL), nxt,
                  pl.BlockSpec(wqt.shape, lambda m, e: (0, 0), **single),
                  pl.BlockSpec(keys.shape, lambda m, e: (0, 0, 0), **single),
                  espec, espec,
                  pl.BlockSpec((TM_PEER, D_MODEL), lambda m, e: (m, 0), **single),
                  pl.BlockSpec((1, N_MOD, D_MODEL), lambda m, e: (midx(m), 0, 0)),
                  _const_spec((1, D_MODEL))],
        out_specs=_dual_specs(TM_PEER, n_ctx_tok // TM_PEER),
        out_shape=[jax.ShapeDtypeStruct((n_ctx_tok, D_MODEL), F32),
                   jax.ShapeDtypeStruct((t - n_ctx_tok, D_MODEL), F32)],
        scratch_shapes=[pltpu.VMEM((n_chunks, wqt.shape[0], LANES), BF16),
                        pltpu.VMEM((2, n_chunks, nsel, LANES), I32),
                        pltpu.VMEM((2, n_chunks, nsel, LANES), F32),
                        pltpu.VMEM((TM_PEER, nsel), I32),
                        pltpu.VMEM((TM_PEER, nsel), F32),
                        pltpu.VMEM((TM_PEER * pitch, N_KEYS), jnp.uint32),
                        pltpu.VMEM((TM_PEER, D_MODEL), F32)],
        compiler_params=_cparams(("arbitrary", "arbitrary"), VMEM_LIMIT),
        name="peer",
    )(h2, h2, wqt, keys, u, v, x1, mod, fng)


def _wkv_state_in(s, vs):
    n = s.shape[0]
    ng = HS_WKV // vs // SUBLANES
    s = s.reshape(n, N_DIR, H_WKV, vs, ng, SUBLANES, HS_WKV).transpose(1, 4, 6, 5, 3, 0, 2)
    return s.reshape(N_DIR, ng, HS_WKV, SUBLANES, vs * n * H_WKV)


def _wkv_state_out(s, n_sb, spb, vs):
    ng = HS_WKV // vs // SUBLANES
    s = s.reshape(N_DIR, n_sb, ng, HS_WKV, SUBLANES, vs, spb, H_WKV).transpose(1, 6, 0, 7, 5, 2, 4, 3)
    return s.reshape(n_sb * spb, N_DIR, H_WKV, HS_WKV, HS_WKV)


def _layer_weights(i, prm):
    eye_h = jnp.eye(H_LRU, dtype=F32)
    eye_d = jnp.eye(N_DIR, dtype=F32)

    def perm(x, axis=-1):
        x = jnp.moveaxis(x, axis, -1)
        lead = x.shape[:-1]
        x = x.reshape(lead + (H_WKV, HS_WKV)).swapaxes(-1, -2).reshape(lead + (W_MIX,))
        return jnp.moveaxis(x, -1, axis)

    def lru_bd(wt):
        return jnp.einsum("dhij,hg->hidgj", wt, eye_h).reshape(W_MIX, N_DIR * W_MIX)

    def lora_bd(wt):
        r = wt.shape[1]
        return jnp.einsum("drc,de->drec", wt, eye_d).reshape(N_DIR * r, N_DIR * W_MIX)

    w_in = prm["w_in"][i]
    pad = jnp.zeros((D_MODEL, Z_COLS - 5504), F32)
    rkv = [perm(w_in[:, 2560 + j * W_MIX:2560 + (j + 1) * W_MIX]) for j in range(3)]
    w_in_perm = jnp.concatenate(
        [w_in[:, 0:1536]] + rkv + [w_in[:, 1536:2560], w_in[:, 4480:5504], w_in[:, 4096:4480], pad],
        axis=1).astype(BF16)
    row = lambda x: x.reshape(1, -1).astype(F32)
    head_of = np.arange(W_MIX) % H_WKV
    w_branch = prm["w_branch"][i]
    w_branch = jnp.concatenate([w_branch[0], w_branch[1], perm(w_branch[2], axis=0), w_branch[3]], axis=0)
    return {
        "w_in": w_in_perm,
        "w_gate": w_in[:, 5504:].astype(BF16),
        "norm1_g": row(prm["norm1_g"][i]),
        "norm2_g": row(prm["norm2_g"][i]),
        "conv_a_w": prm["conv_a_w"][i],
        "conv_b_w": prm["conv_b_w"][i],
        "conv_b_b": row(prm["conv_b_b"][i]),
        "lru_w": jnp.concatenate([lru_bd(prm["lru_wa"][i]), lru_bd(prm["lru_wx"][i])], axis=1).astype(BF16),
        "lru_b": jnp.concatenate([row(prm["lru_ba"][i]), row(prm["lru_bx"][i])], axis=1),
        "lru_lam": row(prm["lru_lambda"][i]),
        "w0": row(perm(prm["rwkv_w0"][i])),
        "w2": lora_bd(perm(prm["rwkv_w2"][i])).astype(BF16),
        "a0": row(perm(prm["rwkv_a0"][i])),
        "a2": lora_bd(perm(prm["rwkv_a2"][i])).astype(BF16),
        "g2": perm(prm["rwkv_g2"][i]).astype(BF16),
        "kk": row(perm(prm["rwkv_kk"][i])),
        "ka": row(perm(prm["rwkv_ka"][i])),
        "rk": row(perm(prm["rwkv_rk"][i].reshape(W_MIX))),
        "lnx_g": row(perm(prm["lnx_g"][i])),
        "lnx_b": row(perm(prm["lnx_b"][i])),
        "sg_ln_g": row(prm["sg_ln_g"][i]),
        "sg_ln_b": row(prm["sg_ln_b"][i]),
        "sg_ws": prm["sg_ws"][i].astype(BF16),
        "sg_bst": prm["sg_bs"][i].T,
        "gate_b": row(prm["gate_b"][i]),
        "w_branch": w_branch.astype(BF16),
        "w_out": prm["w_out"][i].astype(BF16),
        "wq_t": prm["peer_wq"][i].T.astype(BF16),
        "keys": prm["peer_keys"][i].astype(BF16),
        "ind": jnp.asarray(head_of[:, None] == head_of[None, :], BF16),
    }


def kernel(x_prompt, x_sample, state_lru, state_wkv, c, c_ctx, norm1_g, norm2_g, w_mod, b_mod, w_in, conv_a_w, conv_b_w, conv_b_b, lru_wa, lru_ba, lru_wx, lru_bx, lru_lambda, rwkv_w0, rwkv_w2, rwkv_a0, rwkv_a2, rwkv_g2, rwkv_kk, rwkv_ka, rwkv_rk, lnx_g, lnx_b, sg_ln_g, sg_ln_b, sg_ws, sg_bs, gate_b, w_branch, w_out, peer_wq, peer_keys, peer_u, peer_v, final_norm_g):
    prm = dict(norm1_g=norm1_g, norm2_g=norm2_g, w_in=w_in, conv_a_w=conv_a_w, conv_b_w=conv_b_w,
               conv_b_b=conv_b_b, lru_wa=lru_wa, lru_ba=lru_ba, lru_wx=lru_wx, lru_bx=lru_bx,
               lru_lambda=lru_lambda, rwkv_w0=rwkv_w0, rwkv_w2=rwkv_w2, rwkv_a0=rwkv_a0, rwkv_a2=rwkv_a2,
               rwkv_g2=rwkv_g2, rwkv_kk=rwkv_kk, rwkv_ka=rwkv_ka, rwkv_rk=rwkv_rk, lnx_g=lnx_g,
               lnx_b=lnx_b, sg_ln_g=sg_ln_g, sg_ln_b=sg_ln_b, sg_ws=sg_ws, sg_bs=sg_bs, gate_b=gate_b,
               w_branch=w_branch, w_out=w_out, peer_wq=peer_wq, peer_keys=peer_keys, peer_u=peer_u,
               peer_v=peer_v)
    bc, lc, _ = x_prompt.shape
    bl, ll, _ = x_sample.shape
    depth = w_mod.shape[0]
    n_ctx_tok = bc * lc
    n_ctx_tiles = n_ctx_tok // TM
    lat_tiles = ll // TM
    ctx_spb = min(WKV_CTX_SPB, bc)
    lru_spb = ll // lc
    assert lc == TM and ll % TM_PEER == 0 and n_ctx_tok % TM_PEER == 0 and bl + 1 <= SUBLANES
    assert ll % GRID_W == 0 and TM % GRID_W == 0 and bc % ctx_spb == 0 and n_ctx_tok % ll == 0
    assert LANES % (ctx_spb * H_WKV) == 0 and LANES % (bl * H_WKV) == 0

    cond = jnp.zeros((SUBLANES, D_MODEL), F32).at[0].set(c_ctx).at[1:1 + bl].set(c)
    mods = _modulation(cond, w_mod, b_mod).reshape(depth, SUBLANES, N_MOD, D_MODEL)
    fng = final_norm_g.reshape(1, D_MODEL)
    x_ctx = x_prompt.reshape(n_ctx_tok, D_MODEL)
    x_lat = x_sample.reshape(bl * ll, D_MODEL)
    ctx_vs = LANES // (ctx_spb * H_WKV)
    lat_vs = LANES // (bl * H_WKV)
    n_sb = bc // ctx_spb
    wkv_zero = jnp.zeros((N_DIR * n_sb, HS_WKV // ctx_vs // SUBLANES, HS_WKV, SUBLANES, LANES), F32)
    lru_zero = jnp.zeros((n_ctx_tok // ll, lru_spb, N_DIR * W_MIX), F32)
    u_all = peer_u.astype(BF16)
    v_all = peer_v.astype(BF16)
    new_lru, new_wkv = [], []
    pnames = ["ya", "yd", "gbg", "la", "lu", "g", "bon", "rt", "vt", "kkt", "wt", "kt", "bt"]
    for i in range(depth):
        lw = _layer_weights(i, prm)
        mod = mods[i]
        p = dict(zip(pnames, _prep(x_ctx, x_lat, mod, lw, ll)))

        lat_h0 = jnp.zeros((bl, lru_spb, N_DIR * W_MIX), F32).at[:, 0].set(
            state_lru[:, i].astype(F32).reshape(bl, N_DIR * W_MIX))
        h, lru_s = _lru_scan(p["la"], p["lu"], jnp.concatenate([lru_zero, lat_h0], axis=0),
                             rows=ll, n_ctx_blocks=n_ctx_tok // ll, ctx_cfg=(lru_spb, lc), lat_cfg=(1, ll))
        new_lru.append(lru_s[:n_ctx_tok // ll].reshape(bc, N_DIR, W_MIX))

        wkv_in = [p[n] for n in ("rt", "wt", "kt", "kkt", "bt", "vt")]
        y_c, s_c = _wkv_scan(*wkv_in, wkv_zero, tile0=0, n_seq=bc, seq_tiles=1, spb=ctx_spb)
        y_l, _ = _wkv_scan(*wkv_in, _wkv_state_in(state_wkv[:, i].astype(F32), lat_vs),
                           tile0=n_ctx_tiles, n_seq=bl, seq_tiles=lat_tiles, spb=1)
        new_wkv.append(_wkv_state_out(s_c, n_sb, ctx_spb, ctx_vs))

        tok_in = [p["ya"], p["yd"], p["gbg"], h, p["g"], p["bon"]]
        x1, h2 = _merge(x_ctx, x_lat, mod, lw, tok_in, y_c, y_l, ll)
        x_ctx, x_lat = _peer(h2, lw["wq_t"], lw["keys"], u_all, v_all, i, x1, mod, fng,
                             n_ctx_tok, ll, final=(i == depth - 1))
    y_prompt = x_ctx.reshape(bc, lc, D_MODEL)
    y_sample = x_lat.reshape(bl, ll, D_MODEL)
    return (y_prompt, y_sample, jnp.stack(new_lru, axis=1), jnp.stack(new_wkv, axis=1))
```

```python
import functools

import numpy as np
import jax
import jax.numpy as jnp
from jax import lax
from jax.experimental import pallas as pl
from jax.experimental.pallas import tpu as pltpu

F32 = jnp.float32
BF16 = jnp.bfloat16
I32 = jnp.int32

D_MODEL = 1024
W_MIX = 512
N_DIR = 2
H_WKV = 8
HS_WKV = 64
H_LRU = 8
LORA_W = 64
LORA_A = 64
LORA_G = 128
GRID_W = 64
CHUNK = 128
G_SG = 4
N_KEYS = 128
PEER_HEADS = 8
PEER_TOPK = 16
N_MOD = 6
EPS = 1e-6
LNX_EPS = 64e-5
LRU_C = 8.0

LANES = 128
SUBLANES = 8
TM = 256
TM_PEER = 512
PEER_EB = 1024
PEER_SUB = 512
Z_COLS = 5632
WKV_TC = LANES
WKV_CTX_SPB = 8
VMEM_LIMIT = 56 * 1024 * 1024

_NT = (((1,), (1,)), ((), ()))


def _cparams(sem, vmem=None):
    return pltpu.CompilerParams(dimension_semantics=sem, vmem_limit_bytes=vmem)


def _const_spec(shape):
    nd = len(shape)
    return pl.BlockSpec(shape, lambda *_: (0,) * nd)


def _softplus(x):
    return jnp.maximum(x, 0.0) + jnp.log1p(jnp.exp(-jnp.abs(x)))


def _rms(x, g):
    return x * lax.rsqrt(jnp.mean(x * x, axis=-1, keepdims=True) + EPS) * g


def _segsum(x, ind):
    hi = x.astype(BF16)
    lo = (x - hi.astype(F32)).astype(BF16)
    return (jnp.dot(hi, ind, preferred_element_type=F32)
            + jnp.dot(lo, ind, preferred_element_type=F32))


def _mod_index(i, tm, n_ctx_tok, lat_len):
    n_ctx_tiles = n_ctx_tok // tm
    tiles_per_seq = lat_len // tm
    return jnp.where(i < n_ctx_tiles, 0, 1 + lax.div(i - n_ctx_tiles, tiles_per_seq))


def _mod_body(s_ref, w_ref, b_ref, o_ref):
    s = s_ref[...]
    s = s * jax.nn.sigmoid(s)
    o_ref[0] = jnp.dot(s.astype(BF16), w_ref[0].astype(BF16), preferred_element_type=F32) + b_ref[0]


def _modulation(cond, w_mod, b_mod):
    depth = w_mod.shape[0]
    n = w_mod.shape[2]
    tn = 1536
    return pl.pallas_call(
        _mod_body,
        grid=(depth, n // tn),
        in_specs=[_const_spec((SUBLANES, D_MODEL)),
                  pl.BlockSpec((1, D_MODEL, tn), lambda l, j: (l, 0, j)),
                  pl.BlockSpec((1, 1, tn), lambda l, j: (l, 0, j))],
        out_specs=pl.BlockSpec((1, SUBLANES, tn), lambda l, j: (l, 0, j)),
        out_shape=jax.ShapeDtypeStruct((depth, SUBLANES, n), F32),
        compiler_params=_cparams(("parallel", "parallel"), VMEM_LIMIT),
        name="modulation",
    )(cond, w_mod, b_mod.reshape(depth, 1, n))


def _prep_body(xc_ref, xl_ref, xp_ref, xn_ref, mod_ref, n1g_ref, win_ref,
               caw_ref, cbw_ref, cbb_ref, lruw_ref, lrub_ref, lam_ref,
               w0_ref, w2_ref, a0_ref, a2_ref, g2_ref, kkw_ref, ka_ref, rk_ref,
               lng_ref, lnb_ref, ws_ref, bst_ref, ind_ref,
               ya_ref, yd_ref, gbg_ref, la_ref, lu_ref, g_ref, bon_ref,
               rt_ref, vt_ref, kkt_ref, wt_ref, kt_ref, bt_ref,
               *, n_ctx_tiles, tiles_per_seq):
    i = pl.program_id(0)
    is_ctx = i < n_ctx_tiles
    t = lax.broadcasted_iota(I32, (TM, 1), 0)
    ind = ind_ref[...]
    m = mod_ref[0]

    def modulated(xv):
        return (_rms(xv, n1g_ref[...]) * (1.0 + m[1:2, :]) + m[0:1, :]).astype(BF16)

    def project(hv, lo, hi):
        return jnp.dot(hv, win_ref[:, lo:hi], preferred_element_type=F32)

    h = modulated(jnp.where(is_ctx, xc_ref[...], xl_ref[...]))
    za = project(h, 0, 1536)
    zc = project(h, 1536, 3072)
    zb = project(h, 3072, 4096)
    zd = project(h, 4096, 5120)
    zl = project(h, 5120, 5120 + 2 * LORA_W + 2 * LORA_A + LORA_G)
    halo_prev = project(modulated(xp_ref[...]), 3072 + W_MIX, 4096)
    halo_next = project(modulated(xn_ref[...]), 3072 + W_MIX, 4096)

    pm = jnp.where(is_ctx, TM - 1, GRID_W - 1)
    pos = t & pm
    a_b = za[:, 0:W_MIX]
    ac = za[:, W_MIX:2 * W_MIX] * za[:, 2 * W_MIX:3 * W_MIX]
    up = jnp.where(pos == 0, 0.0, pltpu.roll(ac, 1, 0))
    dn = jnp.where(pos == pm, 0.0, pltpu.roll(ac, TM - 1, 0))
    ya_ref[...] = a_b * (caw_ref[0:1, :] * up + caw_ref[1:2, :] * ac + caw_ref[2:3, :] * dn)

    seq_tile = lax.rem(jnp.maximum(i - n_ctx_tiles, 0), tiles_per_seq)
    first = jnp.logical_or(is_ctx, seq_tile == 0)
    last = jnp.logical_or(is_ctx, seq_tile == tiles_per_seq - 1)
    prev = jnp.where(first, 0.0, halo_prev[SUBLANES - 1:SUBLANES, :])
    nxt0 = jnp.where(last, 0.0, halo_next[0:1, :])
    nxt1 = jnp.where(last, 0.0, halo_next[1:2, :])
    bx = zb[:, W_MIX:2 * W_MIX]
    m1 = jnp.where(t == 0, prev, pltpu.roll(bx, 1, 0))
    p1 = jnp.where(t == TM - 1, nxt0, pltpu.roll(bx, TM - 1, 0))
    p2 = jnp.where(t == TM - 2, nxt0, jnp.where(t == TM - 1, nxt1, pltpu.roll(bx, TM - 2, 0)))
    xb = (cbw_ref[0:1, :] * m1 + cbw_ref[1:2, :] * bx + cbw_ref[2:3, :] * p1
          + cbw_ref[3:4, :] * p2 + cbb_ref[...])
    gates = jnp.dot(xb.astype(BF16), lruw_ref[...], preferred_element_type=F32) + lrub_ref[...]
    rg = jax.nn.sigmoid(gates[:, 0:2 * W_MIX])
    ig = jax.nn.sigmoid(gates[:, 2 * W_MIX:4 * W_MIX])
    log_a = -LRU_C * rg * _softplus(-lam_ref[...])
    xb2 = jnp.concatenate([xb, xb], axis=1)
    a = jnp.exp(log_a)
    la_ref[...] = a
    lu_ref[...] = jnp.sqrt(jnp.tanh(-log_a) * (a * a + 1.0)) * (ig * xb2)
    gbg_ref[...] = jax.nn.gelu(zb[:, 0:W_MIX])

    zr = zc[:, 0:W_MIX]
    zk = zc[:, W_MIX:2 * W_MIX]
    zv = zc[:, 2 * W_MIX:3 * W_MIX]
    zwd = zl[:, 0:2 * LORA_W]
    zad = zl[:, 2 * LORA_W:2 * LORA_W + 2 * LORA_A]
    zgd = zl[:, 2 * LORA_W + 2 * LORA_A:2 * LORA_W + 2 * LORA_A + LORA_G]
    wlin = w0_ref[...] + jnp.dot(jnp.tanh(zwd).astype(BF16), w2_ref[...], preferred_element_type=F32)
    wt_ref[0] = jnp.exp(-jnp.exp(-_softplus(-wlin) - 0.5)).T
    av = jax.nn.sigmoid(a0_ref[...] + jnp.dot(zad.astype(BF16), a2_ref[...], preferred_element_type=F32))
    g_ref[...] = jnp.dot(jax.nn.sigmoid(zgd).astype(BF16), g2_ref[...], preferred_element_type=F32)
    kkr = zk * kkw_ref[...]
    kkn = kkr / jnp.maximum(jnp.sqrt(_segsum(kkr * kkr, ind)), 1e-12)
    zk2 = jnp.concatenate([zk, zk], axis=1)
    ka2 = jnp.concatenate([ka_ref[...], ka_ref[...]], axis=1)
    kd = zk2 * (1.0 + (av - 1.0) * ka2)
    kt_ref[0] = kd.T
    bt_ref[0] = (jnp.concatenate([kkn, kkn], axis=1) * av).T
    rt_ref[0] = zr.T
    vt_ref[0] = zv.T
    kkt_ref[0] = kkn.T
    bon_ref[...] = _segsum(zr * (kd[:, 0:W_MIX] + kd[:, W_MIX:2 * W_MIX]) * rk_ref[...], ind) * zv

    zg = jax.nn.gelu(zd)
    u = zg[:, 0:W_MIX]
    vv = zg[:, W_MIX:2 * W_MIX]
    vc = vv - jnp.mean(vv, axis=-1, keepdims=True)
    vn = vc * lax.rsqrt(jnp.mean(vc * vc, axis=-1, keepdims=True) + 1e-5) * lng_ref[...] + lnb_ref[...]
    for c in range(TM // CHUNK):
        rs = slice(c * CHUNK, (c + 1) * CHUNK)
        for gi in range(G_SG):
            cs = slice(gi * LANES, (gi + 1) * LANES)
            s = jnp.dot(ws_ref[gi], vn[rs, cs].astype(BF16), preferred_element_type=F32)
            yd_ref[rs, cs] = u[rs, cs] * (s + bst_ref[:, gi:gi + 1])


def _dual_specs(rows, n_ctx_blocks, **kw):
    return [pl.BlockSpec((rows, D_MODEL), lambda i, *_: (jnp.minimum(i, n_ctx_blocks - 1), 0), **kw),
            pl.BlockSpec((rows, D_MODEL), lambda i, *_: (jnp.maximum(i - n_ctx_blocks, 0), 0), **kw)]


def _prep(x_ctx, x_lat, mod, lw, lat_len):
    n_ctx_tok = x_ctx.shape[0]
    t = n_ctx_tok + x_lat.shape[0]
    n_tiles = t // TM
    n_ctx_tiles = n_ctx_tok // TM
    tiles_per_seq = lat_len // TM
    rows8 = TM // SUBLANES
    last_blk = x_lat.shape[0] // SUBLANES - 1
    midx = functools.partial(_mod_index, tm=TM, n_ctx_tok=n_ctx_tok, lat_len=lat_len)

    def lat_blk8(i, off):
        return (jnp.clip((i - n_ctx_tiles) * rows8 + off, 0, last_blk), 0)

    x_specs = _dual_specs(TM, n_ctx_tiles) + [
        pl.BlockSpec((SUBLANES, D_MODEL), lambda i: lat_blk8(i, -1)),
        pl.BlockSpec((SUBLANES, D_MODEL), lambda i: lat_blk8(i, rows8)),
        pl.BlockSpec((1, N_MOD, D_MODEL), lambda i: (midx(i), 0, 0)),
        _const_spec((1, D_MODEL)),
        pl.BlockSpec(lw["w_in"].shape, lambda i: (0, 0), pipeline_mode=pl.Buffered(1)),
    ]
    wnames = ["conv_a_w", "conv_b_w", "conv_b_b", "lru_w", "lru_b", "lru_lam", "w0", "w2", "a0", "a2",
              "g2", "kk", "ka", "rk", "sg_ln_g", "sg_ln_b", "sg_ws", "sg_bst", "ind"]
    wts = [lw[n] for n in wnames]
    w_specs = [_const_spec(w.shape) for w in wts]
    widths = [W_MIX, W_MIX, W_MIX, 2 * W_MIX, 2 * W_MIX, W_MIX, W_MIX]
    t_rows = [W_MIX, W_MIX, W_MIX, 2 * W_MIX, 2 * W_MIX, 2 * W_MIX]
    out_specs = ([pl.BlockSpec((TM, wd), lambda i: (i, 0)) for wd in widths]
                 + [pl.BlockSpec((1, r, TM), lambda i: (i, 0, 0)) for r in t_rows])
    out_shape = ([jax.ShapeDtypeStruct((t, wd), F32) for wd in widths]
                 + [jax.ShapeDtypeStruct((n_tiles, r, TM), F32) for r in t_rows])
    return pl.pallas_call(
        functools.partial(_prep_body, n_ctx_tiles=n_ctx_tiles, tiles_per_seq=tiles_per_seq),
        grid=(n_tiles,),
        in_specs=x_specs + w_specs,
        out_specs=out_specs,
        out_shape=out_shape,
        compiler_params=_cparams(("parallel",), VMEM_LIMIT),
        name="branch_prep",
    )(x_ctx, x_lat, x_lat, x_lat, mod, lw["norm1_g"], lw["w_in"], *wts)


def _lru_body(a_ref, u_ref, h0_ref, h_ref, hf_ref, *, n_ctx_blocks, ctx_cfg, lat_cfg):
    fw, bw = slice(0, W_MIX), slice(W_MIX, 2 * W_MIX)

    def scan(nseq, l):
        def step(s, carry):
            out = []
            for j in range(nseq):
                tf = j * l + s
                tb = j * l + (l - 1 - s)
                hf = a_ref[pl.ds(tf, 1), fw] * carry[2 * j] + u_ref[pl.ds(tf, 1), fw]
                hb = a_ref[pl.ds(tb, 1), bw] * carry[2 * j + 1] + u_ref[pl.ds(tb, 1), bw]
                h_ref[pl.ds(tf, 1), fw] = hf
                h_ref[pl.ds(tb, 1), bw] = hb
                out += [hf, hb]
            return tuple(out)

        init = []
        for j in range(nseq):
            init += [h0_ref[0, j:j + 1, fw], h0_ref[0, j:j + 1, bw]]
        fin = lax.fori_loop(0, l, step, tuple(init), unroll=2)
        hf_ref[0] = h0_ref[0]
        for j in range(nseq):
            hf_ref[0, j:j + 1, fw] = fin[2 * j]
            hf_ref[0, j:j + 1, bw] = fin[2 * j + 1]

    is_ctx = pl.program_id(0) < n_ctx_blocks
    pl.when(is_ctx)(lambda: scan(*ctx_cfg))
    pl.when(jnp.logical_not(is_ctx))(lambda: scan(*lat_cfg))


def _lru_scan(a, u, h0, *, rows, n_ctx_blocks, ctx_cfg, lat_cfg):
    nb = a.shape[0] // rows
    w = a.shape[1]
    tok = pl.BlockSpec((rows, w), lambda i: (i, 0))
    st = pl.BlockSpec((1,) + h0.shape[1:], lambda i: (i, 0, 0))
    return pl.pallas_call(
        functools.partial(_lru_body, n_ctx_blocks=n_ctx_blocks, ctx_cfg=ctx_cfg, lat_cfg=lat_cfg),
        grid=(nb,),
        in_specs=[tok, tok, st],
        out_specs=[tok, st],
        out_shape=[jax.ShapeDtypeStruct(a.shape, F32), jax.ShapeDtypeStruct(h0.shape, F32)],
        compiler_params=_cparams(("parallel",), VMEM_LIMIT),
        name="lru_scan",
    )(a, u, h0)


_SLOT_ORDER = (0, 4, 2, 6, 1, 5, 3, 7)


def _rowsum8(parts):
    sub = lax.broadcasted_iota(I32, (SUBLANES, LANES), 0)
    slots = [parts[i] for i in _SLOT_ORDER]
    roll = pltpu.roll
    lvl1 = [jnp.where(sub < 4, a + roll(a, 4, 0), b + roll(b, 4, 0))
            for a, b in zip(slots[0::2], slots[1::2])]
    lvl2 = [jnp.where((sub & 3) < 2, a + roll(a, 6, 0), roll(b + roll(b, 6, 0), 2, 0))
            for a, b in zip(lvl1[0::2], lvl1[1::2])]
    a, b = lvl2
    return jnp.where((sub & 1) == 0, a + roll(a, 7, 0), roll(b + roll(b, 7, 0), 1, 0))


def _fold8(x):
    return jnp.sum(x.reshape(HS_WKV // SUBLANES, SUBLANES, LANES), axis=0)


def _value_rows_on_sublanes(vl_n):
    return vl_n // SUBLANES < 4


def _wkv_body(*refs, nsrc, spb, vs, n_sb, tc, kp, vp):
    vl_n = HS_WKV // vs
    n_in = 6 * nsrc
    k_srcs = [refs[o * nsrc:(o + 1) * nsrc] for o in range(5)]
    v_srcs = refs[5 * nsrc:n_in]
    s0_ref = refs[n_in]
    y_ref = refs[n_in + 1]
    sf_ref = refs[n_in + 2]
    k_scr = refs[n_in + 3:n_in + 8]
    v_scr, y_scr, s_scr = refs[n_in + 8:n_in + 11]
    rows_on_sublanes = _value_rows_on_sublanes(HS_WKV // vs)
    sa_scr = None if rows_on_sublanes else refs[n_in + 11]
    r_scr, w_scr, k_scr_, kk_scr, b_scr = k_scr
    backward = pl.program_id(0) // n_sb == 1
    seqs = [(s, j) for s in range(nsrc) for j in range(spb)]

    @pl.when(pl.program_id(1) == 0)
    def _():
        s_scr[...] = s0_ref[0]

    def build_k(c, carry):
        row = pl.multiple_of(c * H_WKV, H_WKV)
        for o in range(5):
            slab = [k_srcs[o][s][j, pl.ds(row, H_WKV), :] for s, j in seqs]
            k_scr[o][pl.ds(c, tc, stride=kp), :] = jnp.concatenate(slab * vs, axis=0).T
        return carry

    lax.fori_loop(0, HS_WKV, build_k, 0, unroll=4)

    def build_v(vl, carry):
        slab = []
        for vsi in range(vs):
            row = pl.multiple_of((vsi * vl_n + vl) * H_WKV, H_WKV)
            slab += [v_srcs[s][j, pl.ds(row, H_WKV), :] for s, j in seqs]
        v_scr[pl.ds(vl, tc, stride=vp), :] = jnp.concatenate(slab, axis=0).T
        return carry

    lax.fori_loop(0, vl_n, build_v, 0, unroll=4)

    def step_keys_on_sublanes(s, carry):
        t = jnp.where(backward, tc - 1 - s, s)
        krow = pl.multiple_of(t * kp, SUBLANES)
        vrow = pl.multiple_of(t * vp, SUBLANES)
        kslab = pl.ds(krow, HS_WKV)
        for g in range(vl_n // SUBLANES):
            parts = [_fold8(s_scr[g * SUBLANES + i] * kk_scr[kslab, :]) for i in range(SUBLANES)]
            sa_scr[g * SUBLANES:(g + 1) * SUBLANES, :] = _rowsum8(parts)
        for g in range(vl_n // SUBLANES):
            parts = []
            for i in range(SUBLANES):
                vl = g * SUBLANES + i
                sa = sa_scr[vl:vl + 1, :]
                vv = v_scr[pl.ds(vrow + vl, 1), :]
                sn = s_scr[vl] * w_scr[kslab, :] - sa * b_scr[kslab, :] + vv * k_scr_[kslab, :]
                s_scr[vl] = sn
                parts.append(_fold8(sn * r_scr[kslab, :]))
            y_scr[pl.ds(pl.multiple_of(vrow + g * SUBLANES, SUBLANES), SUBLANES), :] = _rowsum8(parts)
        return carry

    def step_rows_on_sublanes(s, carry):
        t = jnp.where(backward, tc - 1 - s, s)
        krow = pl.multiple_of(t * kp, SUBLANES)
        vrow = pl.multiple_of(t * vp, SUBLANES)
        n_g = vl_n // SUBLANES
        batch = min(n_g, 2)
        n_acc = 4 // batch

        def row(ref, k):
            return jnp.broadcast_to(ref[pl.ds(krow + k, 1), :], (SUBLANES, LANES))

        def total(parts):
            while len(parts) > 1:
                parts = [a + b for a, b in zip(parts[0::2], parts[1::2])]
            return parts[0]

        def accumulate(acc, g, k, p):
            acc[g][k % n_acc] = p if acc[g][k % n_acc] is None else acc[g][k % n_acc] + p

        for g0 in range(0, n_g, batch):
            gs = range(g0, g0 + batch)
            acc = {g: [None] * n_acc for g in gs}
            for k in range(HS_WKV):
                kk = row(kk_scr, k)
                for g in gs:
                    accumulate(acc, g, k, s_scr[g, k] * kk)
            sa = {g: total(acc[g]) for g in gs}
            vv = {g: v_scr[pl.ds(pl.multiple_of(vrow + g * SUBLANES, SUBLANES), SUBLANES), :] for g in gs}
            acc = {g: [None] * n_acc for g in gs}
            for k in range(HS_WKV):
                w, b, kx, r = row(w_scr, k), row(b_scr, k), row(k_scr_, k), row(r_scr, k)
                for g in gs:
                    sn = s_scr[g, k] * w - sa[g] * b + vv[g] * kx
                    s_scr[g, k] = sn
                    accumulate(acc, g, k, sn * r)
            for g in gs:
                y_scr[pl.ds(pl.multiple_of(vrow + g * SUBLANES, SUBLANES), SUBLANES), :] = total(acc[g])
        return carry

    lax.fori_loop(0, tc, step_rows_on_sublanes if rows_on_sublanes else step_keys_on_sublanes, 0)

    def emit_y(vl, carry):
        yt = y_scr[pl.ds(vl, tc, stride=vp), :].T
        for vsi in range(vs):
            row = pl.multiple_of((vsi * vl_n + vl) * H_WKV, H_WKV)
            for n, (s, j) in enumerate(seqs):
                lane0 = (vsi * len(seqs) + n) * H_WKV
                y_ref[0, 0, s * spb + j, pl.ds(row, H_WKV), :] = yt[lane0:lane0 + H_WKV, :]
        return carry

    lax.fori_loop(0, vl_n, emit_y, 0, unroll=4)
    sf_ref[0] = s_scr[...]


def _wkv_scan(rt, wt, kt, kkt, bt, vt, s0, *, tile0, n_seq, seq_tiles, spb):
    tc = WKV_TC
    if spb > 1:
        assert seq_tiles == 1 and n_seq % spb == 0 and tile0 % spb == 0
        nsrc, n_sb = 1, n_seq // spb
    else:
        nsrc, n_sb = n_seq, 1
    inst = nsrc * spb * H_WKV
    vs = LANES // inst
    vl_n = HS_WKV // vs
    assert vl_n % SUBLANES == 0, "value rows are processed eight at a time"
    cpt = TM // tc
    n_chunks = seq_tiles * cpt
    kp = HS_WKV + SUBLANES
    vp = vl_n + SUBLANES if ((vl_n + SUBLANES) // SUBLANES) % 2 else vl_n + 2 * SUBLANES

    def chunk(g, i):
        return jnp.where(g // n_sb == 1, n_chunks - 1 - i, i)

    def in_map(g, i, *, src, per_dir):
        ce = chunk(g, i)
        rb = (g // n_sb) if per_dir else 0
        if spb > 1:
            return (tile0 // spb + g % n_sb, rb, ce)
        return (tile0 + src * seq_tiles + ce // cpt, rb, ce % cpt)

    def out_map(g, i):
        ce = chunk(g, i)
        if spb > 1:
            return (g // n_sb, 0, g % n_sb, 0, ce)
        return (g // n_sb, ce // cpt, 0, 0, ce % cpt)

    in_specs, operands = [], []
    for arr, per_dir in ((rt, False), (wt, True), (kt, True), (kkt, False), (bt, True), (vt, False)):
        for src in range(nsrc):
            in_specs.append(pl.BlockSpec((spb, W_MIX, tc), functools.partial(in_map, src=src, per_dir=per_dir),
                                         pipeline_mode=pl.Buffered(1)))
            operands.append(arr)
    state_block = (1,) + _wkv_state_shape(1, vs)[1:]
    sspec = pl.BlockSpec(state_block, lambda g, i: (g,) + (0,) * (len(state_block) - 1))
    in_specs.append(sspec)
    out_specs = [pl.BlockSpec((1, 1, nsrc * spb, W_MIX, tc), out_map)]
    out_shape = [jax.ShapeDtypeStruct((N_DIR, seq_tiles, n_seq, W_MIX, TM), F32)]
    res = pl.pallas_call(
        functools.partial(_wkv_body, nsrc=nsrc, spb=spb, vs=vs, n_sb=n_sb, tc=tc, kp=kp, vp=vp),
        grid=(N_DIR * n_sb, n_chunks),
        in_specs=in_specs,
        out_specs=out_specs + [sspec],
        out_shape=out_shape + [jax.ShapeDtypeStruct(s0.shape, F32)],
        scratch_shapes=([pltpu.VMEM((tc * kp, LANES), F32)] * 5
                        + [pltpu.VMEM((tc * vp, LANES), F32)] * 2
                        + [pltpu.VMEM(state_block[1:], F32)]
                        + ([] if _value_rows_on_sublanes(vl_n) else [pltpu.VMEM((vl_n, LANES), F32)])),
        compiler_params=_cparams(("parallel", "arbitrary"), VMEM_LIMIT),
        name="wkv_scan",
    )(*operands, s0)
    return res[0], res[1]


def _merge_body(xc_ref, xl_ref, mod_ref, n1g_ref, n2g_ref, wg_ref, gb_ref, wbr_ref, wo_ref,
                lnxg_ref, lnxb_ref, ind_ref,
                ya_ref, yd_ref, gbg_ref, h_ref, ycf_ref, ycb_ref, ylf_ref, ylb_ref, g_ref, bon_ref,
                x1_ref, h2_ref, *, n_ctx_tiles):
    is_ctx = pl.program_id(0) < n_ctx_tiles
    x = jnp.where(is_ctx, xc_ref[...], xl_ref[...])
    m = mod_ref[0]
    ind = ind_ref[...]
    h = (_rms(x, n1g_ref[...]) * (1.0 + m[1:2, :]) + m[0:1, :]).astype(BF16)
    y_b = gbg_ref[...] * (h_ref[:, 0:W_MIX] + h_ref[:, W_MIX:2 * W_MIX])
    y = jnp.where(is_ctx, ycf_ref[0, 0, 0] + ycb_ref[0, 0, 0], ylf_ref[0, 0, 0] + ylb_ref[0, 0, 0]).T
    yc = y - _segsum(y, ind) * (1.0 / HS_WKV)
    var = _segsum(yc * yc, ind) * (1.0 / HS_WKV)
    y_c = (yc * lax.rsqrt(var + LNX_EPS) * lnxg_ref[...] + lnxb_ref[...] + bon_ref[...]) * g_ref[...]
    merged = None
    for n, yn in enumerate((ya_ref[...], y_b, y_c, yd_ref[...])):
        cs = slice(n * D_MODEL, (n + 1) * D_MODEL)
        gate = jax.nn.sigmoid(jnp.dot(h, wg_ref[:, cs], preferred_element_type=F32) + gb_ref[:, cs])
        br = jnp.dot(yn.astype(BF16), wbr_ref[n * W_MIX:(n + 1) * W_MIX, :], preferred_element_type=F32)
        merged = gate * br if merged is None else merged + gate * br
    mo = jnp.dot(merged.astype(BF16), wo_ref[...], preferred_element_type=F32)
    x1 = x + m[2:3, :] * mo
    x1_ref[...] = x1
    h2_ref[...] = (_rms(x1, n2g_ref[...]) * (1.0 + m[4:5, :]) + m[3:4, :]).astype(BF16)


def _merge(x_ctx, x_lat, mod, lw, tok_in, y_ctx, y_lat, lat_len):
    n_ctx_tok = x_ctx.shape[0]
    t = n_ctx_tok + x_lat.shape[0]
    n_ctx_tiles = n_ctx_tok // TM
    tps = lat_len // TM
    midx = functools.partial(_mod_index, tm=TM, n_ctx_tok=n_ctx_tok, lat_len=lat_len)
    wnames = ["norm1_g", "norm2_g", "w_gate", "gate_b", "w_branch", "w_out", "lnx_g", "lnx_b", "ind"]
    wts = [lw[n] for n in wnames]
    tok = lambda wd: pl.BlockSpec((TM, wd), lambda i: (i, 0))
    ya, yd, gbg, h, g, bon = tok_in
    yblock = (1, 1, 1, W_MIX, TM)

    def ctx_spec(d):
        return pl.BlockSpec(yblock, lambda i: (d, 0, jnp.minimum(i, n_ctx_tiles - 1), 0, 0))

    def lat_spec(d):
        def imap(i):
            r = jnp.maximum(i - n_ctx_tiles, 0)
            return (d, lax.rem(r, tps), lax.div(r, tps), 0, 0)
        return pl.BlockSpec(yblock, imap)

    return pl.pallas_call(
        functools.partial(_merge_body, n_ctx_tiles=n_ctx_tiles),
        grid=(t // TM,),
        in_specs=(_dual_specs(TM, n_ctx_tiles)
                  + [pl.BlockSpec((1, N_MOD, D_MODEL), lambda i: (midx(i), 0, 0))]
                  + [_const_spec(w.shape) for w in wts]
                  + [tok(W_MIX), tok(W_MIX), tok(W_MIX), tok(2 * W_MIX),
                     ctx_spec(0), ctx_spec(1), lat_spec(0), lat_spec(1), tok(W_MIX), tok(W_MIX)]),
        out_specs=[tok(D_MODEL), tok(D_MODEL)],
        out_shape=[jax.ShapeDtypeStruct((t, D_MODEL), F32), jax.ShapeDtypeStruct((t, D_MODEL), BF16)],
        compiler_params=_cparams(("parallel",), VMEM_LIMIT),
        name="merge",
    )(x_ctx, x_lat, mod, *wts, ya, yd, gbg, h, y_ctx, y_ctx, y_lat, y_lat, g, bon)


_CAND_VALID = (8, 8, 8, 5, 4, 3, 2, 2, 2, 8)


def _oddeven_pairs(n):
    def merge(lo, hi, r):
        step = r * 2
        if step < hi - lo:
            yield from merge(lo, hi, step)
            yield from merge(lo + r, hi, step)
            yield from [(i, i + r) for i in range(lo + r, hi - r, step)]
        else:
            yield (lo, lo + r)

    def sort(lo, hi):
        if hi - lo >= 1:
            mid = lo + (hi - lo) // 2
            yield from sort(lo, mid)
            yield from sort(mid + 1, hi)
            yield from merge(lo, hi, 1)

    return tuple(sort(0, n - 1))


_SORT16 = _oddeven_pairs(N_KEYS // SUBLANES)


def _route_head(qs, keys_ref):
    sub = lax.broadcasted_iota(I32, (SUBLANES, LANES), 0)
    kid = lax.broadcasted_iota(I32, (PEER_TOPK, LANES), 0)
    neg = -jnp.inf

    def bc(x, r):
        return jnp.broadcast_to(x[r:r + 1, :], (SUBLANES, LANES))

    def head():
        tops = []
        for p in range(2):
            s = jnp.dot(keys_ref[p], qs[p], preferred_element_type=F32)
            cols = [s[j * SUBLANES:(j + 1) * SUBLANES, :] for j in range(N_KEYS // SUBLANES)]
            cidx = [sub + j * SUBLANES for j in range(N_KEYS // SUBLANES)]
            for a, b in _SORT16:
                take = cols[b] > cols[a]
                cols[a], cols[b] = jnp.where(take, cols[b], cols[a]), jnp.where(take, cols[a], cols[b])
                cidx[a], cidx[b] = jnp.where(take, cidx[b], cidx[a]), jnp.where(take, cidx[a], cidx[b])
            vals = jnp.zeros((PEER_TOPK, LANES), F32)
            idxs = jnp.zeros((PEER_TOPK, LANES), I32)
            for r in range(PEER_TOPK):
                v8, i8 = cols[0], cidx[0]
                for sh in (4, 2, 1):
                    vr, ir = pltpu.roll(v8, sh, 0), pltpu.roll(i8, sh, 0)
                    take = vr > v8
                    v8, i8 = jnp.where(take, vr, v8), jnp.where(take, ir, i8)
                m, ix = v8[0:1, :], i8[0:1, :]
                vals = jnp.where(kid == r, m, vals)
                idxs = jnp.where(kid == r, ix, idxs)
                popped = cidx[0] == ix
                for j in range(PEER_TOPK - 1 - r):
                    cols[j] = jnp.where(popped, cols[j + 1], cols[j])
                    cidx[j] = jnp.where(popped, cidx[j + 1], cidx[j])
            tops.append((vals, idxs))
        (a0, i0), (a1, i1) = tops
        lo, hi = slice(0, SUBLANES), slice(SUBLANES, 2 * SUBLANES)
        slabs = [bc(a0, 0) + a1[lo], bc(a0, 0) + a1[hi]]
        ci = [bc(i0, 0), bc(i0, 0)]
        cj = [i1[lo], i1[hi]]
        for r in range(1, SUBLANES):
            slabs.append(bc(a0, r) + a1[lo])
            ci.append(bc(i0, r))
            cj.append(i1[lo])
        slabs.append(a0[hi] + bc(a1, 0))
        ci.append(i0[hi])
        cj.append(bc(i1, 0))
        slabs = [jnp.where(sub < nv, sl, neg) for sl, nv in zip(slabs, _CAND_VALID)]
        ids = [a * N_KEYS + b for a, b in zip(ci, cj)]
        vals = jnp.zeros((PEER_TOPK, LANES), F32)
        esel = jnp.zeros((PEER_TOPK, LANES), I32)
        for r in range(PEER_TOPK):
            level = list(zip(slabs, ids))
            while len(level) > 1:
                nxt = []
                for (va, ea), (vb, eb) in zip(level[0::2], level[1::2]):
                    take = vb > va
                    nxt.append((jnp.where(take, vb, va), jnp.where(take, eb, ea)))
                if len(level) % 2:
                    nxt.append(level[-1])
                level = nxt
            v8, e8 = level[0]
            for sh in (4, 2, 1):
                vr, er = pltpu.roll(v8, sh, 0), pltpu.roll(e8, sh, 0)
                take = vr > v8
                v8, e8 = jnp.where(take, vr, v8), jnp.where(take, er, e8)
            m, ex = v8[0:1, :], e8[0:1, :]
            slabs = [jnp.where(eid == ex, neg, sl) for sl, eid in zip(slabs, ids)]
            vals = jnp.where(kid == r, m, vals)
            esel = jnp.where(kid == r, ex, esel)
        e = jnp.exp(vals - vals[0:1, :])
        return esel, e / jnp.sum(e, axis=0, keepdims=True)

    return head()


def _peer_body(h2_ref, h2n_ref, wqt_ref, keys_ref, u_ref, v_ref, x1_ref, mod_ref, fng_ref,
               oc_ref, ol_ref,
               q_scr, e_scr, g_scr, et_scr, gt_scr, gs_scr, acc_scr,
               *, rows, pitch, units, n_ctx_tiles, final):
    m = pl.program_id(0)
    e = pl.program_id(1)
    tm = h2_ref.shape[0]
    n_chunks = tm // LANES
    nsel = PEER_HEADS * PEER_TOPK
    slot = lax.rem(m, 2)

    def project_queries(src_ref):
        q = lax.dot_general(wqt_ref[...], src_ref[...], _NT, preferred_element_type=F32).astype(BF16)
        for c in range(n_chunks):
            q_scr[c] = q[:, c * LANES:(c + 1) * LANES]

    def route_unit(u, dst):
        c = u // PEER_HEADS
        h = lax.rem(u, PEER_HEADS)
        qs = [q_scr[c, pl.ds(pl.multiple_of(h * (2 * N_KEYS) + p * N_KEYS, N_KEYS), N_KEYS), :]
              for p in range(2)]
        esel, gates = _route_head(qs, keys_ref)
        row = pl.multiple_of(h * PEER_TOPK, PEER_TOPK)
        e_scr[dst, c, pl.ds(row, PEER_TOPK), :] = esel
        g_scr[dst, c, pl.ds(row, PEER_TOPK), :] = gates

    @pl.when(jnp.logical_and(e == 0, m == 0))
    def _first_tile_routing():
        project_queries(h2_ref)

        def unit(u, c):
            route_unit(u, 0)
            return c

        lax.fori_loop(0, n_chunks * PEER_HEADS, unit, 0)

    @pl.when(e == 0)
    def _build():
        for c in range(n_chunks):
            et_scr[c * LANES:(c + 1) * LANES, :] = e_scr[slot, c].T
            gt_scr[c * LANES:(c + 1) * LANES, :] = g_scr[slot, c].T
        kio = lax.broadcasted_iota(I32, (N_KEYS, nsel), 0)

        def tok(t, c):
            erow = et_scr[pl.ds(t, 1), :]
            grow = gt_scr[pl.ds(t, 1), :]
            at = jnp.where(kio == (erow >> 7), grow, 0.0).astype(BF16)
            bt = jnp.where(kio == (erow & (N_KEYS - 1)), 1.0, 0.0).astype(BF16)
            gt = lax.dot_general(at, bt, _NT, preferred_element_type=F32)
            hi = pltpu.bitcast(gt[0:rows, :], jnp.uint32) & jnp.uint32(0xFFFF0000)
            lo = pltpu.bitcast(gt[rows:2 * rows, :], jnp.uint32) >> 16
            gs_scr[pl.ds(pl.multiple_of(t * pitch, SUBLANES), rows), :] = hi | lo
            return c

        lax.fori_loop(0, tm, tok, 0, unroll=128)
        acc_scr[...] = jnp.zeros_like(acc_scr)
        project_queries(h2n_ref)

    for k in range(units):
        route_unit(e * units + k, 1 - slot)

    per_sub = PEER_SUB // N_KEYS
    per_step = u_ref.shape[0] // N_KEYS
    steps_per_half = rows // per_step
    row0 = lax.rem(e, steps_per_half) * per_step
    shift = jnp.where(e < steps_per_half, 0, 16).astype(jnp.uint32)
    h2 = h2_ref[...]
    total = None
    for sb in range(per_step // per_sub):
        es = slice(sb * PEER_SUB, (sb + 1) * PEER_SUB)
        hmat = lax.dot_general(h2, u_ref[es, :], _NT, preferred_element_type=F32)
        words = jnp.concatenate(
            [gs_scr[pl.ds(row0 + sb * per_sub + ii, tm, stride=pitch), :] for ii in range(per_sub)], axis=1)
        gm = pltpu.bitcast((words << shift) & jnp.uint32(0xFFFF0000), F32)
        act = jax.nn.gelu(hmat.astype(BF16)) * gm.astype(BF16)
        part = jnp.dot(act, v_ref[es, :], preferred_element_type=F32)
        total = part if total is None else total + part
    acc_scr[...] += total

    def result():
        x2 = x1_ref[...] + mod_ref[0][5:6, :] * acc_scr[...]
        return _rms(x2, fng_ref[...]) if final else x2

    last = e == pl.num_programs(1) - 1

    @pl.when(jnp.logical_and(last, m < n_ctx_tiles))
    def _out_ctx():
        oc_ref[...] = result()

    @pl.when(jnp.logical_and(last, m >= n_ctx_tiles))
    def _out_lat():
        ol_ref[...] = result()


def _peer(h2, wqt, keys, u, v, layer, x1, mod, fng, n_ctx_tok, lat_len, final):
    t = h2.shape[0]
    nsel = PEER_HEADS * PEER_TOPK
    rows = N_KEYS // 2
    pitch = rows + SUBLANES
    n_e = (N_KEYS * N_KEYS) // PEER_EB
    n_m = t // TM_PEER
    n_chunks = TM_PEER // LANES
    units = (n_chunks * PEER_HEADS) // n_e
    assert units * n_e == n_chunks * PEER_HEADS
    midx = functools.partial(_mod_index, tm=TM_PEER, n_ctx_tok=n_ctx_tok, lat_len=lat_len)
    tok = lambda wd: pl.BlockSpec((TM_PEER, wd), lambda m, e: (m, 0))
    nxt = pl.BlockSpec((TM_PEER, D_MODEL), lambda m, e: (jnp.minimum(m + 1, n_m - 1), 0),
                       pipeline_mode=pl.Buffered(1))
    espec = pl.BlockSpec((None, PEER_EB, D_MODEL), lambda m, e: (layer, e, 0))
    single = dict(pipeline_mode=pl.Buffered(1))
    return pl.pallas_call(
        functools.partial(_peer_body, rows=rows, pitch=pitch, units=units,
                          n_ctx_tiles=n_ctx_tok // TM_PEER, final=final),
        grid=(n_m, n_e),
        in_specs=[tok(D_MODEL), nxt,
                  pl.BlockSpec(wqt.shape, lambda m, e: (0, 0), **single),
                  pl.BlockSpec(keys.shape, lambda m, e: (0, 0, 0), **single),
                  espec, espec,
                  pl.BlockSpec((TM_PEER, D_MODEL), lambda m, e: (m, 0), **single),
                  pl.BlockSpec((1, N_MOD, D_MODEL), lambda m, e: (midx(m), 0, 0)),
                  _const_spec((1, D_MODEL))],
        out_specs=_dual_specs(TM_PEER, n_ctx_tok // TM_PEER),
        out_shape=[jax.ShapeDtypeStruct((n_ctx_tok, D_MODEL), F32),
                   jax.ShapeDtypeStruct((t - n_ctx_tok, D_MODEL), F32)],
        scratch_shapes=[pltpu.VMEM((n_chunks, wqt.shape[0], LANES), BF16),
                        pltpu.VMEM((2, n_chunks, nsel, LANES), I32),
                        pltpu.VMEM((2, n_chunks, nsel, LANES), F32),
                        pltpu.VMEM((TM_PEER, nsel), I32),
                        pltpu.VMEM((TM_PEER, nsel), F32),
                        pltpu.VMEM((TM_PEER * pitch, N_KEYS), jnp.uint32),
                        pltpu.VMEM((TM_PEER, D_MODEL), F32)],
        compiler_params=_cparams(("arbitrary", "arbitrary"), VMEM_LIMIT),
        name="peer",
    )(h2, h2, wqt, keys, u, v, x1, mod, fng)


def _wkv_state_shape(n_groups, vs):
    vl = HS_WKV // vs
    if _value_rows_on_sublanes(vl):
        return (n_groups, vl // SUBLANES, HS_WKV, SUBLANES, LANES)
    return (n_groups, vl, HS_WKV, LANES)


def _wkv_state_in(s, vs):
    n = s.shape[0]
    vl = HS_WKV // vs
    if _value_rows_on_sublanes(vl):
        s = s.reshape(n, N_DIR, H_WKV, vs, vl // SUBLANES, SUBLANES, HS_WKV).transpose(1, 4, 6, 5, 3, 0, 2)
    else:
        s = s.reshape(n, N_DIR, H_WKV, vs, vl, HS_WKV).transpose(1, 4, 5, 3, 0, 2)
    return s.reshape(_wkv_state_shape(N_DIR, vs))


def _wkv_state_out(s, n_sb, spb, vs):
    vl = HS_WKV // vs
    if _value_rows_on_sublanes(vl):
        s = s.reshape(N_DIR, n_sb, vl // SUBLANES, HS_WKV, SUBLANES, vs, spb, H_WKV)
        s = s.transpose(1, 6, 0, 7, 5, 2, 4, 3)
    else:
        s = s.reshape(N_DIR, n_sb, vl, HS_WKV, vs, spb, H_WKV).transpose(1, 5, 0, 6, 4, 2, 3)
    return s.reshape(n_sb * spb, N_DIR, H_WKV, HS_WKV, HS_WKV)


def _layer_weights(i, prm):
    eye_h = jnp.eye(H_LRU, dtype=F32)
    eye_d = jnp.eye(N_DIR, dtype=F32)

    def perm(x, axis=-1):
        x = jnp.moveaxis(x, axis, -1)
        lead = x.shape[:-1]
        x = x.reshape(lead + (H_WKV, HS_WKV)).swapaxes(-1, -2).reshape(lead + (W_MIX,))
        return jnp.moveaxis(x, -1, axis)

    def lru_bd(wt):
        return jnp.einsum("dhij,hg->hidgj", wt, eye_h).reshape(W_MIX, N_DIR * W_MIX)

    def lora_bd(wt):
        r = wt.shape[1]
        return jnp.einsum("drc,de->drec", wt, eye_d).reshape(N_DIR * r, N_DIR * W_MIX)

    w_in = prm["w_in"][i]
    pad = jnp.zeros((D_MODEL, Z_COLS - 5504), F32)
    rkv = [perm(w_in[:, 2560 + j * W_MIX:2560 + (j + 1) * W_MIX]) for j in range(3)]
    w_in_perm = jnp.concatenate(
        [w_in[:, 0:1536]] + rkv + [w_in[:, 1536:2560], w_in[:, 4480:5504], w_in[:, 4096:4480], pad],
        axis=1).astype(BF16)
    row = lambda x: x.reshape(1, -1).astype(F32)
    head_of = np.arange(W_MIX) % H_WKV
    w_branch = prm["w_branch"][i]
    w_branch = jnp.concatenate([w_branch[0], w_branch[1], perm(w_branch[2], axis=0), w_branch[3]], axis=0)
    return {
        "w_in": w_in_perm,
        "w_gate": w_in[:, 5504:].astype(BF16),
        "norm1_g": row(prm["norm1_g"][i]),
        "norm2_g": row(prm["norm2_g"][i]),
        "conv_a_w": prm["conv_a_w"][i],
        "conv_b_w": prm["conv_b_w"][i],
        "conv_b_b": row(prm["conv_b_b"][i]),
        "lru_w": jnp.concatenate([lru_bd(prm["lru_wa"][i]), lru_bd(prm["lru_wx"][i])], axis=1).astype(BF16),
        "lru_b": jnp.concatenate([row(prm["lru_ba"][i]), row(prm["lru_bx"][i])], axis=1),
        "lru_lam": row(prm["lru_lambda"][i]),
        "w0": row(perm(prm["rwkv_w0"][i])),
        "w2": lora_bd(perm(prm["rwkv_w2"][i])).astype(BF16),
        "a0": row(perm(prm["rwkv_a0"][i])),
        "a2": lora_bd(perm(prm["rwkv_a2"][i])).astype(BF16),
        "g2": perm(prm["rwkv_g2"][i]).astype(BF16),
        "kk": row(perm(prm["rwkv_kk"][i])),
        "ka": row(perm(prm["rwkv_ka"][i])),
        "rk": row(perm(prm["rwkv_rk"][i].reshape(W_MIX))),
        "lnx_g": row(perm(prm["lnx_g"][i])),
        "lnx_b": row(perm(prm["lnx_b"][i])),
        "sg_ln_g": row(prm["sg_ln_g"][i]),
        "sg_ln_b": row(prm["sg_ln_b"][i]),
        "sg_ws": prm["sg_ws"][i].astype(BF16),
        "sg_bst": prm["sg_bs"][i].T,
        "gate_b": row(prm["gate_b"][i]),
        "w_branch": w_branch.astype(BF16),
        "w_out": prm["w_out"][i].astype(BF16),
        "wq_t": prm["peer_wq"][i].T.astype(BF16),
        "keys": prm["peer_keys"][i].astype(BF16),
        "ind": jnp.asarray(head_of[:, None] == head_of[None, :], BF16),
    }


def kernel(x_prompt, x_sample, state_lru, state_wkv, c, c_ctx, norm1_g, norm2_g, w_mod, b_mod, w_in, conv_a_w, conv_b_w, conv_b_b, lru_wa, lru_ba, lru_wx, lru_bx, lru_lambda, rwkv_w0, rwkv_w2, rwkv_a0, rwkv_a2, rwkv_g2, rwkv_kk, rwkv_ka, rwkv_rk, lnx_g, lnx_b, sg_ln_g, sg_ln_b, sg_ws, sg_bs, gate_b, w_branch, w_out, peer_wq, peer_keys, peer_u, peer_v, final_norm_g):
    prm = dict(norm1_g=norm1_g, norm2_g=norm2_g, w_in=w_in, conv_a_w=conv_a_w, conv_b_w=conv_b_w,
               conv_b_b=conv_b_b, lru_wa=lru_wa, lru_ba=lru_ba, lru_wx=lru_wx, lru_bx=lru_bx,
               lru_lambda=lru_lambda, rwkv_w0=rwkv_w0, rwkv_w2=rwkv_w2, rwkv_a0=rwkv_a0, rwkv_a2=rwkv_a2,
               rwkv_g2=rwkv_g2, rwkv_kk=rwkv_kk, rwkv_ka=rwkv_ka, rwkv_rk=rwkv_rk, lnx_g=lnx_g,
               lnx_b=lnx_b, sg_ln_g=sg_ln_g, sg_ln_b=sg_ln_b, sg_ws=sg_ws, sg_bs=sg_bs, gate_b=gate_b,
               w_branch=w_branch, w_out=w_out, peer_wq=peer_wq, peer_keys=peer_keys, peer_u=peer_u,
               peer_v=peer_v)
    bc, lc, _ = x_prompt.shape
    bl, ll, _ = x_sample.shape
    depth = w_mod.shape[0]
    n_ctx_tok = bc * lc
    n_ctx_tiles = n_ctx_tok // TM
    lat_tiles = ll // TM
    ctx_spb = min(WKV_CTX_SPB, bc)
    lru_spb = ll // lc
    assert lc == TM and ll % TM_PEER == 0 and n_ctx_tok % TM_PEER == 0 and bl + 1 <= SUBLANES
    assert ll % GRID_W == 0 and TM % GRID_W == 0 and bc % ctx_spb == 0 and n_ctx_tok % ll == 0
    assert LANES % (ctx_spb * H_WKV) == 0 and LANES % (bl * H_WKV) == 0

    cond = jnp.zeros((SUBLANES, D_MODEL), F32).at[0].set(c_ctx).at[1:1 + bl].set(c)
    mods = _modulation(cond, w_mod, b_mod).reshape(depth, SUBLANES, N_MOD, D_MODEL)
    fng = final_norm_g.reshape(1, D_MODEL)
    x_ctx = x_prompt.reshape(n_ctx_tok, D_MODEL)
    x_lat = x_sample.reshape(bl * ll, D_MODEL)
    ctx_vs = LANES // (ctx_spb * H_WKV)
    lat_vs = LANES // (bl * H_WKV)
    n_sb = bc // ctx_spb
    wkv_zero = jnp.zeros(_wkv_state_shape(N_DIR * n_sb, ctx_vs), F32)
    lru_zero = jnp.zeros((n_ctx_tok // ll, lru_spb, N_DIR * W_MIX), F32)
    u_all = peer_u.astype(BF16)
    v_all = peer_v.astype(BF16)
    new_lru, new_wkv = [], []
    pnames = ["ya", "yd", "gbg", "la", "lu", "g", "bon", "rt", "vt", "kkt", "wt", "kt", "bt"]
    for i in range(depth):
        lw = _layer_weights(i, prm)
        mod = mods[i]
        p = dict(zip(pnames, _prep(x_ctx, x_lat, mod, lw, ll)))

        lat_h0 = jnp.zeros((bl, lru_spb, N_DIR * W_MIX), F32).at[:, 0].set(
            state_lru[:, i].astype(F32).reshape(bl, N_DIR * W_MIX))
        h, lru_s = _lru_scan(p["la"], p["lu"], jnp.concatenate([lru_zero, lat_h0], axis=0),
                             rows=ll, n_ctx_blocks=n_ctx_tok // ll, ctx_cfg=(lru_spb, lc), lat_cfg=(1, ll))
        new_lru.append(lru_s[:n_ctx_tok // ll].reshape(bc, N_DIR, W_MIX))

        wkv_in = [p[n] for n in ("rt", "wt", "kt", "kkt", "bt", "vt")]
        y_c, s_c = _wkv_scan(*wkv_in, wkv_zero, tile0=0, n_seq=bc, seq_tiles=1, spb=ctx_spb)
        y_l, _ = _wkv_scan(*wkv_in, _wkv_state_in(state_wkv[:, i].astype(F32), lat_vs),
                           tile0=n_ctx_tiles, n_seq=bl, seq_tiles=lat_tiles, spb=1)
        new_wkv.append(_wkv_state_out(s_c, n_sb, ctx_spb, ctx_vs))

        tok_in = [p["ya"], p["yd"], p["gbg"], h, p["g"], p["bon"]]
        x1, h2 = _merge(x_ctx, x_lat, mod, lw, tok_in, y_c, y_l, ll)
        x_ctx, x_lat = _peer(h2, lw["wq_t"], lw["keys"], u_all, v_all, i, x1, mod, fng,
                             n_ctx_tok, ll, final=(i == depth - 1))
    y_prompt = x_ctx.reshape(bc, lc, D_MODEL)
    y_sample = x_lat.reshape(bl, ll, D_MODEL)
    return (y_prompt, y_sample, jnp.stack(new_lru, axis=1), jnp.stack(new_wkv, axis=1))
```

```python
import functools

import numpy as np
import jax
import jax.numpy as jnp
from jax import lax
from jax.experimental import pallas as pl
from jax.experimental.pallas import tpu as pltpu

F32 = jnp.float32
BF16 = jnp.bfloat16
I32 = jnp.int32

D_MODEL = 1024
W_MIX = 512
N_DIR = 2
H_WKV = 8
HS_WKV = 64
H_LRU = 8
LORA_W = 64
LORA_A = 64
LORA_G = 128
GRID_W = 64
CHUNK = 128
G_SG = 4
N_KEYS = 128
PEER_HEADS = 8
PEER_TOPK = 16
N_MOD = 6
EPS = 1e-6
LNX_EPS = 64e-5
LRU_C = 8.0

LANES = 128
SUBLANES = 8
TM = 256
TM_PEER = 512
PEER_EB = 1024
PEER_SUB = 512
Z_COLS = 5632
WKV_TC = LANES
WKV_CTX_SPB = 8
VMEM_LIMIT = 56 * 1024 * 1024

_NT = (((1,), (1,)), ((), ()))


def _cparams(sem, vmem=None):
    return pltpu.CompilerParams(dimension_semantics=sem, vmem_limit_bytes=vmem)


def _const_spec(shape):
    nd = len(shape)
    return pl.BlockSpec(shape, lambda *_: (0,) * nd)


def _layer_spec(arr, layer, **kw):
    nd = arr.ndim
    idx = layer if arr.shape[0] > 1 else 0
    return pl.BlockSpec((None,) + arr.shape[1:], lambda *_: (idx,) + (0,) * (nd - 1), **kw)


def _softplus(x):
    return jnp.maximum(x, 0.0) + jnp.log1p(jnp.exp(-jnp.abs(x)))


def _rms(x, g):
    return x * lax.rsqrt(jnp.mean(x * x, axis=-1, keepdims=True) + EPS) * g


def _segsum(x, ind):
    hi = x.astype(BF16)
    lo = (x - hi.astype(F32)).astype(BF16)
    return (jnp.dot(hi, ind, preferred_element_type=F32)
            + jnp.dot(lo, ind, preferred_element_type=F32))


def _mod_index(i, tm, n_ctx_tok, lat_len):
    n_ctx_tiles = n_ctx_tok // tm
    tiles_per_seq = lat_len // tm
    return jnp.where(i < n_ctx_tiles, 0, 1 + lax.div(i - n_ctx_tiles, tiles_per_seq))


def _mod_body(s_ref, w_ref, b_ref, o_ref):
    s = s_ref[...]
    s = s * jax.nn.sigmoid(s)
    o_ref[0] = jnp.dot(s.astype(BF16), w_ref[0].astype(BF16), preferred_element_type=F32) + b_ref[0]


def _modulation(cond, w_mod, b_mod):
    depth = w_mod.shape[0]
    n = w_mod.shape[2]
    tn = 1536
    return pl.pallas_call(
        _mod_body,
        grid=(depth, n // tn),
        in_specs=[_const_spec((SUBLANES, D_MODEL)),
                  pl.BlockSpec((1, D_MODEL, tn), lambda l, j: (l, 0, j)),
                  pl.BlockSpec((1, 1, tn), lambda l, j: (l, 0, j))],
        out_specs=pl.BlockSpec((1, SUBLANES, tn), lambda l, j: (l, 0, j)),
        out_shape=jax.ShapeDtypeStruct((depth, SUBLANES, n), F32),
        compiler_params=_cparams(("parallel", "parallel"), VMEM_LIMIT),
        name="modulation",
    )(cond, w_mod, b_mod.reshape(depth, 1, n))


def _prep_body(xc_ref, xl_ref, xp_ref, xn_ref, mod_ref, n1g_ref, win_ref,
               caw_ref, cbw_ref, cbb_ref, lruw_ref, lrub_ref, lam_ref,
               w0_ref, w2_ref, a0_ref, a2_ref, g2_ref, kkw_ref, ka_ref, rk_ref,
               lng_ref, lnb_ref, ws_ref, bst_ref, ind_ref,
               ya_ref, yd_ref, gbg_ref, la_ref, lu_ref, g_ref, bon_ref,
               rt_ref, vt_ref, kkt_ref, wt_ref, kt_ref, bt_ref,
               *, n_ctx_tiles, tiles_per_seq):
    i = pl.program_id(0)
    is_ctx = i < n_ctx_tiles
    t = lax.broadcasted_iota(I32, (TM, 1), 0)
    ind = ind_ref[...]
    m = mod_ref[0]

    def modulated(xv):
        return (_rms(xv, n1g_ref[...]) * (1.0 + m[1:2, :]) + m[0:1, :]).astype(BF16)

    def project(hv, lo, hi):
        return jnp.dot(hv, win_ref[:, lo:hi], preferred_element_type=F32)

    h = modulated(jnp.where(is_ctx, xc_ref[...], xl_ref[...]))
    za = project(h, 0, 1536)
    zc = project(h, 1536, 3072)
    zb = project(h, 3072, 4096)
    zd = project(h, 4096, 5120)
    zl = project(h, 5120, 5120 + 2 * LORA_W + 2 * LORA_A + LORA_G)
    halo_prev = project(modulated(xp_ref[...]), 3072 + W_MIX, 4096)
    halo_next = project(modulated(xn_ref[...]), 3072 + W_MIX, 4096)

    pm = jnp.where(is_ctx, TM - 1, GRID_W - 1)
    pos = t & pm
    a_b = za[:, 0:W_MIX]
    ac = za[:, W_MIX:2 * W_MIX] * za[:, 2 * W_MIX:3 * W_MIX]
    up = jnp.where(pos == 0, 0.0, pltpu.roll(ac, 1, 0))
    dn = jnp.where(pos == pm, 0.0, pltpu.roll(ac, TM - 1, 0))
    ya_ref[...] = a_b * (caw_ref[0:1, :] * up + caw_ref[1:2, :] * ac + caw_ref[2:3, :] * dn)

    seq_tile = lax.rem(jnp.maximum(i - n_ctx_tiles, 0), tiles_per_seq)
    first = jnp.logical_or(is_ctx, seq_tile == 0)
    last = jnp.logical_or(is_ctx, seq_tile == tiles_per_seq - 1)
    prev = jnp.where(first, 0.0, halo_prev[SUBLANES - 1:SUBLANES, :])
    nxt0 = jnp.where(last, 0.0, halo_next[0:1, :])
    nxt1 = jnp.where(last, 0.0, halo_next[1:2, :])
    bx = zb[:, W_MIX:2 * W_MIX]
    m1 = jnp.where(t == 0, prev, pltpu.roll(bx, 1, 0))
    p1 = jnp.where(t == TM - 1, nxt0, pltpu.roll(bx, TM - 1, 0))
    p2 = jnp.where(t == TM - 2, nxt0, jnp.where(t == TM - 1, nxt1, pltpu.roll(bx, TM - 2, 0)))
    xb = (cbw_ref[0:1, :] * m1 + cbw_ref[1:2, :] * bx + cbw_ref[2:3, :] * p1
          + cbw_ref[3:4, :] * p2 + cbb_ref[...])
    gates = jnp.dot(xb.astype(BF16), lruw_ref[...], preferred_element_type=F32) + lrub_ref[...]
    rg = jax.nn.sigmoid(gates[:, 0:2 * W_MIX])
    ig = jax.nn.sigmoid(gates[:, 2 * W_MIX:4 * W_MIX])
    log_a = -LRU_C * rg * _softplus(-lam_ref[...])
    xb2 = jnp.concatenate([xb, xb], axis=1)
    a = jnp.exp(log_a)
    la_ref[...] = a
    lu_ref[...] = jnp.sqrt(jnp.tanh(-log_a) * (a * a + 1.0)) * (ig * xb2)
    gbg_ref[...] = jax.nn.gelu(zb[:, 0:W_MIX])

    zr = zc[:, 0:W_MIX]
    zk = zc[:, W_MIX:2 * W_MIX]
    zv = zc[:, 2 * W_MIX:3 * W_MIX]
    zwd = zl[:, 0:2 * LORA_W]
    zad = zl[:, 2 * LORA_W:2 * LORA_W + 2 * LORA_A]
    zgd = zl[:, 2 * LORA_W + 2 * LORA_A:2 * LORA_W + 2 * LORA_A + LORA_G]
    wlin = w0_ref[...] + jnp.dot(jnp.tanh(zwd).astype(BF16), w2_ref[...], preferred_element_type=F32)
    wt_ref[0] = jnp.exp(-jnp.exp(-_softplus(-wlin) - 0.5)).T
    av = jax.nn.sigmoid(a0_ref[...] + jnp.dot(zad.astype(BF16), a2_ref[...], preferred_element_type=F32))
    g_ref[...] = jnp.dot(jax.nn.sigmoid(zgd).astype(BF16), g2_ref[...], preferred_element_type=F32)
    kkr = zk * kkw_ref[...]
    kkn = kkr / jnp.maximum(jnp.sqrt(_segsum(kkr * kkr, ind)), 1e-12)
    zk2 = jnp.concatenate([zk, zk], axis=1)
    ka2 = jnp.concatenate([ka_ref[...], ka_ref[...]], axis=1)
    kd = zk2 * (1.0 + (av - 1.0) * ka2)
    kt_ref[0] = kd.T
    bt_ref[0] = (jnp.concatenate([kkn, kkn], axis=1) * av).T
    rt_ref[0] = zr.T
    vt_ref[0] = zv.T
    kkt_ref[0] = kkn.T
    bon_ref[...] = _segsum(zr * (kd[:, 0:W_MIX] + kd[:, W_MIX:2 * W_MIX]) * rk_ref[...], ind) * zv

    zg = jax.nn.gelu(zd)
    u = zg[:, 0:W_MIX]
    vv = zg[:, W_MIX:2 * W_MIX]
    vc = vv - jnp.mean(vv, axis=-1, keepdims=True)
    vn = vc * lax.rsqrt(jnp.mean(vc * vc, axis=-1, keepdims=True) + 1e-5) * lng_ref[...] + lnb_ref[...]
    for c in range(TM // CHUNK):
        rs = slice(c * CHUNK, (c + 1) * CHUNK)
        for gi in range(G_SG):
            cs = slice(gi * LANES, (gi + 1) * LANES)
            s = jnp.dot(ws_ref[gi], vn[rs, cs].astype(BF16), preferred_element_type=F32)
            yd_ref[rs, cs] = u[rs, cs] * (s + bst_ref[:, gi:gi + 1])


def _dual_specs(rows, n_ctx_blocks, **kw):
    return [pl.BlockSpec((rows, D_MODEL), lambda i, *_: (jnp.minimum(i, n_ctx_blocks - 1), 0), **kw),
            pl.BlockSpec((rows, D_MODEL), lambda i, *_: (jnp.maximum(i - n_ctx_blocks, 0), 0), **kw)]


def _prep(x_ctx, x_lat, mod, lw, layer, lat_len):
    n_ctx_tok = x_ctx.shape[0]
    t = n_ctx_tok + x_lat.shape[0]
    n_tiles = t // TM
    n_ctx_tiles = n_ctx_tok // TM
    tiles_per_seq = lat_len // TM
    rows8 = TM // SUBLANES
    last_blk = x_lat.shape[0] // SUBLANES - 1
    midx = functools.partial(_mod_index, tm=TM, n_ctx_tok=n_ctx_tok, lat_len=lat_len)

    def lat_blk8(i, off):
        return (jnp.clip((i - n_ctx_tiles) * rows8 + off, 0, last_blk), 0)

    x_specs = _dual_specs(TM, n_ctx_tiles) + [
        pl.BlockSpec((SUBLANES, D_MODEL), lambda i: lat_blk8(i, -1)),
        pl.BlockSpec((SUBLANES, D_MODEL), lambda i: lat_blk8(i, rows8)),
        pl.BlockSpec((1, N_MOD, D_MODEL), lambda i: (midx(i), 0, 0)),
        _layer_spec(lw["norm1_g"], layer),
        _layer_spec(lw["w_in"], layer, pipeline_mode=pl.Buffered(1)),
    ]
    wnames = ["conv_a_w", "conv_b_w", "conv_b_b", "lru_w", "lru_b", "lru_lam", "w0", "w2", "a0", "a2",
              "g2", "kk", "ka", "rk", "sg_ln_g", "sg_ln_b", "sg_ws", "sg_bst", "ind"]
    wts = [lw[n] for n in wnames]
    w_specs = [_layer_spec(w, layer) for w in wts]
    widths = [W_MIX, W_MIX, W_MIX, 2 * W_MIX, 2 * W_MIX, W_MIX, W_MIX]
    t_rows = [W_MIX, W_MIX, W_MIX, 2 * W_MIX, 2 * W_MIX, 2 * W_MIX]
    out_specs = ([pl.BlockSpec((TM, wd), lambda i: (i, 0)) for wd in widths]
                 + [pl.BlockSpec((1, r, TM), lambda i: (i, 0, 0)) for r in t_rows])
    out_shape = ([jax.ShapeDtypeStruct((t, wd), F32) for wd in widths]
                 + [jax.ShapeDtypeStruct((n_tiles, r, TM), F32) for r in t_rows])
    return pl.pallas_call(
        functools.partial(_prep_body, n_ctx_tiles=n_ctx_tiles, tiles_per_seq=tiles_per_seq),
        grid=(n_tiles,),
        in_specs=x_specs + w_specs,
        out_specs=out_specs,
        out_shape=out_shape,
        compiler_params=_cparams(("parallel",), VMEM_LIMIT),
        name="branch_prep",
    )(x_ctx, x_lat, x_lat, x_lat, mod, lw["norm1_g"], lw["w_in"], *wts)


def _lru_body(a_ref, u_ref, h0_ref, h_ref, hf_ref, *, n_ctx_blocks, ctx_cfg, lat_cfg):
    fw, bw = slice(0, W_MIX), slice(W_MIX, 2 * W_MIX)

    def scan(nseq, l):
        def step(s, carry):
            out = []
            for j in range(nseq):
                tf = j * l + s
                tb = j * l + (l - 1 - s)
                hf = a_ref[pl.ds(tf, 1), fw] * carry[2 * j] + u_ref[pl.ds(tf, 1), fw]
                hb = a_ref[pl.ds(tb, 1), bw] * carry[2 * j + 1] + u_ref[pl.ds(tb, 1), bw]
                h_ref[pl.ds(tf, 1), fw] = hf
                h_ref[pl.ds(tb, 1), bw] = hb
                out += [hf, hb]
            return tuple(out)

        init = []
        for j in range(nseq):
            init += [h0_ref[0, j:j + 1, fw], h0_ref[0, j:j + 1, bw]]
        fin = lax.fori_loop(0, l, step, tuple(init), unroll=2)
        hf_ref[0] = h0_ref[0]
        for j in range(nseq):
            hf_ref[0, j:j + 1, fw] = fin[2 * j]
            hf_ref[0, j:j + 1, bw] = fin[2 * j + 1]

    is_ctx = pl.program_id(0) < n_ctx_blocks
    pl.when(is_ctx)(lambda: scan(*ctx_cfg))
    pl.when(jnp.logical_not(is_ctx))(lambda: scan(*lat_cfg))


def _lru_scan(a, u, h0, *, rows, n_ctx_blocks, ctx_cfg, lat_cfg):
    nb = a.shape[0] // rows
    w = a.shape[1]
    tok = pl.BlockSpec((rows, w), lambda i: (i, 0))
    st = pl.BlockSpec((1,) + h0.shape[1:], lambda i: (i, 0, 0))
    return pl.pallas_call(
        functools.partial(_lru_body, n_ctx_blocks=n_ctx_blocks, ctx_cfg=ctx_cfg, lat_cfg=lat_cfg),
        grid=(nb,),
        in_specs=[tok, tok, st],
        out_specs=[tok, st],
        out_shape=[jax.ShapeDtypeStruct(a.shape, F32), jax.ShapeDtypeStruct(h0.shape, F32)],
        compiler_params=_cparams(("parallel",), VMEM_LIMIT),
        name="lru_scan",
    )(a, u, h0)


_SLOT_ORDER = (0, 4, 2, 6, 1, 5, 3, 7)


def _rowsum8(parts):
    sub = lax.broadcasted_iota(I32, (SUBLANES, LANES), 0)
    slots = [parts[i] for i in _SLOT_ORDER]
    roll = pltpu.roll
    lvl1 = [jnp.where(sub < 4, a + roll(a, 4, 0), b + roll(b, 4, 0))
            for a, b in zip(slots[0::2], slots[1::2])]
    lvl2 = [jnp.where((sub & 3) < 2, a + roll(a, 6, 0), roll(b + roll(b, 6, 0), 2, 0))
            for a, b in zip(lvl1[0::2], lvl1[1::2])]
    a, b = lvl2
    return jnp.where((sub & 1) == 0, a + roll(a, 7, 0), roll(b + roll(b, 7, 0), 1, 0))


def _fold8(x):
    return jnp.sum(x.reshape(HS_WKV // SUBLANES, SUBLANES, LANES), axis=0)


def _value_rows_on_sublanes(vl_n):
    return vl_n // SUBLANES < 4


def _wkv_body(*refs, nsrc, spb, vs, n_sb, tc, kp, vp):
    vl_n = HS_WKV // vs
    n_in = 6 * nsrc
    k_srcs = [refs[o * nsrc:(o + 1) * nsrc] for o in range(5)]
    v_srcs = refs[5 * nsrc:n_in]
    s0_ref = refs[n_in]
    y_ref = refs[n_in + 1]
    sf_ref = refs[n_in + 2]
    k_scr = refs[n_in + 3:n_in + 8]
    v_scr, y_scr, s_scr = refs[n_in + 8:n_in + 11]
    rows_on_sublanes = _value_rows_on_sublanes(HS_WKV // vs)
    sa_scr = None if rows_on_sublanes else refs[n_in + 11]
    r_scr, w_scr, k_scr_, kk_scr, b_scr = k_scr
    backward = pl.program_id(0) // n_sb == 1
    seqs = [(s, j) for s in range(nsrc) for j in range(spb)]

    @pl.when(pl.program_id(1) == 0)
    def _():
        s_scr[...] = s0_ref[0]

    def build_k(c, carry):
        row = pl.multiple_of(c * H_WKV, H_WKV)
        for o in range(5):
            slab = [k_srcs[o][s][j, pl.ds(row, H_WKV), :] for s, j in seqs]
            k_scr[o][pl.ds(c, tc, stride=kp), :] = jnp.concatenate(slab * vs, axis=0).T
        return carry

    lax.fori_loop(0, HS_WKV, build_k, 0, unroll=4)

    def build_v(vl, carry):
        slab = []
        for vsi in range(vs):
            row = pl.multiple_of((vsi * vl_n + vl) * H_WKV, H_WKV)
            slab += [v_srcs[s][j, pl.ds(row, H_WKV), :] for s, j in seqs]
        v_scr[pl.ds(vl, tc, stride=vp), :] = jnp.concatenate(slab, axis=0).T
        return carry

    lax.fori_loop(0, vl_n, build_v, 0, unroll=4)

    def step_keys_on_sublanes(s, carry):
        t = jnp.where(backward, tc - 1 - s, s)
        krow = pl.multiple_of(t * kp, SUBLANES)
        vrow = pl.multiple_of(t * vp, SUBLANES)
        kslab = pl.ds(krow, HS_WKV)
        for g in range(vl_n // SUBLANES):
            parts = [_fold8(s_scr[g * SUBLANES + i] * kk_scr[kslab, :]) for i in range(SUBLANES)]
            sa_scr[g * SUBLANES:(g + 1) * SUBLANES, :] = _rowsum8(parts)
        for g in range(vl_n // SUBLANES):
            parts = []
            for i in range(SUBLANES):
                vl = g * SUBLANES + i
                sa = sa_scr[vl:vl + 1, :]
                vv = v_scr[pl.ds(vrow + vl, 1), :]
                sn = s_scr[vl] * w_scr[kslab, :] - sa * b_scr[kslab, :] + vv * k_scr_[kslab, :]
                s_scr[vl] = sn
                parts.append(_fold8(sn * r_scr[kslab, :]))
            y_scr[pl.ds(pl.multiple_of(vrow + g * SUBLANES, SUBLANES), SUBLANES), :] = _rowsum8(parts)
        return carry

    def step_rows_on_sublanes(s, carry):
        t = jnp.where(backward, tc - 1 - s, s)
        krow = pl.multiple_of(t * kp, SUBLANES)
        vrow = pl.multiple_of(t * vp, SUBLANES)
        n_g = vl_n // SUBLANES
        batch = min(n_g, 2)
        n_acc = 4 // batch

        def row(ref, k):
            return jnp.broadcast_to(ref[pl.ds(krow + k, 1), :], (SUBLANES, LANES))

        def total(parts):
            while len(parts) > 1:
                parts = [a + b for a, b in zip(parts[0::2], parts[1::2])]
            return parts[0]

        def accumulate(acc, g, k, p):
            acc[g][k % n_acc] = p if acc[g][k % n_acc] is None else acc[g][k % n_acc] + p

        for g0 in range(0, n_g, batch):
            gs = range(g0, g0 + batch)
            acc = {g: [None] * n_acc for g in gs}
            for k in range(HS_WKV):
                kk = row(kk_scr, k)
                for g in gs:
                    accumulate(acc, g, k, s_scr[g, k] * kk)
            sa = {g: total(acc[g]) for g in gs}
            vv = {g: v_scr[pl.ds(pl.multiple_of(vrow + g * SUBLANES, SUBLANES), SUBLANES), :] for g in gs}
            acc = {g: [None] * n_acc for g in gs}
            for k in range(HS_WKV):
                w, b, kx, r = row(w_scr, k), row(b_scr, k), row(k_scr_, k), row(r_scr, k)
                for g in gs:
                    sn = s_scr[g, k] * w - sa[g] * b + vv[g] * kx
                    s_scr[g, k] = sn
                    accumulate(acc, g, k, sn * r)
            for g in gs:
                y_scr[pl.ds(pl.multiple_of(vrow + g * SUBLANES, SUBLANES), SUBLANES), :] = total(acc[g])
        return carry

    lax.fori_loop(0, tc, step_rows_on_sublanes if rows_on_sublanes else step_keys_on_sublanes, 0)

    def emit_y(vl, carry):
        yt = y_scr[pl.ds(vl, tc, stride=vp), :].T
        for vsi in range(vs):
            row = pl.multiple_of((vsi * vl_n + vl) * H_WKV, H_WKV)
            for n, (s, j) in enumerate(seqs):
                lane0 = (vsi * len(seqs) + n) * H_WKV
                y_ref[0, 0, s * spb + j, pl.ds(row, H_WKV), :] = yt[lane0:lane0 + H_WKV, :]
        return carry

    lax.fori_loop(0, vl_n, emit_y, 0, unroll=4)
    sf_ref[0] = s_scr[...]


def _wkv_scan(rt, wt, kt, kkt, bt, vt, s0, *, tile0, n_seq, seq_tiles, spb):
    tc = WKV_TC
    if spb > 1:
        assert seq_tiles == 1 and n_seq % spb == 0 and tile0 % spb == 0
        nsrc, n_sb = 1, n_seq // spb
    else:
        nsrc, n_sb = n_seq, 1
    inst = nsrc * spb * H_WKV
    vs = LANES // inst
    vl_n = HS_WKV // vs
    assert vl_n % SUBLANES == 0, "value rows are processed eight at a time"
    cpt = TM // tc
    n_chunks = seq_tiles * cpt
    kp = HS_WKV + SUBLANES
    vp = vl_n + SUBLANES if ((vl_n + SUBLANES) // SUBLANES) % 2 else vl_n + 2 * SUBLANES

    def chunk(g, i):
        return jnp.where(g // n_sb == 1, n_chunks - 1 - i, i)

    def in_map(g, i, *, src, per_dir):
        ce = chunk(g, i)
        rb = (g // n_sb) if per_dir else 0
        if spb > 1:
            return (tile0 // spb + g % n_sb, rb, ce)
        return (tile0 + src * seq_tiles + ce // cpt, rb, ce % cpt)

    def out_map(g, i):
        ce = chunk(g, i)
        if spb > 1:
            return (g // n_sb, 0, g % n_sb, 0, ce)
        return (g // n_sb, ce // cpt, 0, 0, ce % cpt)

    in_specs, operands = [], []
    for arr, per_dir in ((rt, False), (wt, True), (kt, True), (kkt, False), (bt, True), (vt, False)):
        for src in range(nsrc):
            in_specs.append(pl.BlockSpec((spb, W_MIX, tc), functools.partial(in_map, src=src, per_dir=per_dir),
                                         pipeline_mode=pl.Buffered(1)))
            operands.append(arr)
    state_block = (1,) + _wkv_state_shape(1, vs)[1:]
    sspec = pl.BlockSpec(state_block, lambda g, i: (g,) + (0,) * (len(state_block) - 1))
    in_specs.append(sspec)
    out_specs = [pl.BlockSpec((1, 1, nsrc * spb, W_MIX, tc), out_map)]
    out_shape = [jax.ShapeDtypeStruct((N_DIR, seq_tiles, n_seq, W_MIX, TM), F32)]
    res = pl.pallas_call(
        functools.partial(_wkv_body, nsrc=nsrc, spb=spb, vs=vs, n_sb=n_sb, tc=tc, kp=kp, vp=vp),
        grid=(N_DIR * n_sb, n_chunks),
        in_specs=in_specs,
        out_specs=out_specs + [sspec],
        out_shape=out_shape + [jax.ShapeDtypeStruct(s0.shape, F32)],
        scratch_shapes=([pltpu.VMEM((tc * kp, LANES), F32)] * 5
                        + [pltpu.VMEM((tc * vp, LANES), F32)] * 2
                        + [pltpu.VMEM(state_block[1:], F32)]
                        + ([] if _value_rows_on_sublanes(vl_n) else [pltpu.VMEM((vl_n, LANES), F32)])),
        compiler_params=_cparams(("parallel", "arbitrary"), VMEM_LIMIT),
        name="wkv_scan",
    )(*operands, s0)
    return res[0], res[1]


def _merge_body(xc_ref, xl_ref, mod_ref, n1g_ref, n2g_ref, wg_ref, gb_ref, wbr_ref, wo_ref,
                lnxg_ref, lnxb_ref, ind_ref,
                ya_ref, yd_ref, gbg_ref, h_ref, ycf_ref, ycb_ref, ylf_ref, ylb_ref, g_ref, bon_ref,
                x1_ref, h2_ref, *, n_ctx_tiles):
    is_ctx = pl.program_id(0) < n_ctx_tiles
    x = jnp.where(is_ctx, xc_ref[...], xl_ref[...])
    m = mod_ref[0]
    ind = ind_ref[...]
    h = (_rms(x, n1g_ref[...]) * (1.0 + m[1:2, :]) + m[0:1, :]).astype(BF16)
    y_b = gbg_ref[...] * (h_ref[:, 0:W_MIX] + h_ref[:, W_MIX:2 * W_MIX])
    y = jnp.where(is_ctx, ycf_ref[0, 0, 0] + ycb_ref[0, 0, 0], ylf_ref[0, 0, 0] + ylb_ref[0, 0, 0]).T
    yc = y - _segsum(y, ind) * (1.0 / HS_WKV)
    var = _segsum(yc * yc, ind) * (1.0 / HS_WKV)
    y_c = (yc * lax.rsqrt(var + LNX_EPS) * lnxg_ref[...] + lnxb_ref[...] + bon_ref[...]) * g_ref[...]
    merged = None
    for n, yn in enumerate((ya_ref[...], y_b, y_c, yd_ref[...])):
        cs = slice(n * D_MODEL, (n + 1) * D_MODEL)
        gate = jax.nn.sigmoid(jnp.dot(h, wg_ref[:, cs], preferred_element_type=F32) + gb_ref[:, cs])
        br = jnp.dot(yn.astype(BF16), wbr_ref[n * W_MIX:(n + 1) * W_MIX, :], preferred_element_type=F32)
        merged = gate * br if merged is None else merged + gate * br
    mo = jnp.dot(merged.astype(BF16), wo_ref[...], preferred_element_type=F32)
    x1 = x + m[2:3, :] * mo
    x1_ref[...] = x1
    h2_ref[...] = (_rms(x1, n2g_ref[...]) * (1.0 + m[4:5, :]) + m[3:4, :]).astype(BF16)


def _merge(x_ctx, x_lat, mod, lw, layer, tok_in, y_ctx, y_lat, lat_len):
    n_ctx_tok = x_ctx.shape[0]
    t = n_ctx_tok + x_lat.shape[0]
    n_ctx_tiles = n_ctx_tok // TM
    tps = lat_len // TM
    midx = functools.partial(_mod_index, tm=TM, n_ctx_tok=n_ctx_tok, lat_len=lat_len)
    wnames = ["norm1_g", "norm2_g", "w_gate", "gate_b", "w_branch", "w_out", "lnx_g", "lnx_b", "ind"]
    wts = [lw[n] for n in wnames]
    tok = lambda wd: pl.BlockSpec((TM, wd), lambda i: (i, 0))
    ya, yd, gbg, h, g, bon = tok_in
    yblock = (1, 1, 1, W_MIX, TM)

    def ctx_spec(d):
        return pl.BlockSpec(yblock, lambda i: (d, 0, jnp.minimum(i, n_ctx_tiles - 1), 0, 0))

    def lat_spec(d):
        def imap(i):
            r = jnp.maximum(i - n_ctx_tiles, 0)
            return (d, lax.rem(r, tps), lax.div(r, tps), 0, 0)
        return pl.BlockSpec(yblock, imap)

    return pl.pallas_call(
        functools.partial(_merge_body, n_ctx_tiles=n_ctx_tiles),
        grid=(t // TM,),
        in_specs=(_dual_specs(TM, n_ctx_tiles)
                  + [pl.BlockSpec((1, N_MOD, D_MODEL), lambda i: (midx(i), 0, 0))]
                  + [_layer_spec(w, layer) for w in wts]
                  + [tok(W_MIX), tok(W_MIX), tok(W_MIX), tok(2 * W_MIX),
                     ctx_spec(0), ctx_spec(1), lat_spec(0), lat_spec(1), tok(W_MIX), tok(W_MIX)]),
        out_specs=[tok(D_MODEL), tok(D_MODEL)],
        out_shape=[jax.ShapeDtypeStruct((t, D_MODEL), F32), jax.ShapeDtypeStruct((t, D_MODEL), BF16)],
        compiler_params=_cparams(("parallel",), VMEM_LIMIT),
        name="merge",
    )(x_ctx, x_lat, mod, *wts, ya, yd, gbg, h, y_ctx, y_ctx, y_lat, y_lat, g, bon)


_CAND_VALID = (8, 8, 8, 5, 4, 3, 2, 2, 2, 8)


def _oddeven_pairs(n):
    def merge(lo, hi, r):
        step = r * 2
        if step < hi - lo:
            yield from merge(lo, hi, step)
            yield from merge(lo + r, hi, step)
            yield from [(i, i + r) for i in range(lo + r, hi - r, step)]
        else:
            yield (lo, lo + r)

    def sort(lo, hi):
        if hi - lo >= 1:
            mid = lo + (hi - lo) // 2
            yield from sort(lo, mid)
            yield from sort(mid + 1, hi)
            yield from merge(lo, hi, 1)

    return tuple(sort(0, n - 1))


_SORT16 = _oddeven_pairs(N_KEYS // SUBLANES)


def _route_head(qs, keys_ref):
    sub = lax.broadcasted_iota(I32, (SUBLANES, LANES), 0)
    kid = lax.broadcasted_iota(I32, (PEER_TOPK, LANES), 0)
    neg = -jnp.inf

    def bc(x, r):
        return jnp.broadcast_to(x[r:r + 1, :], (SUBLANES, LANES))

    def head():
        tops = []
        for p in range(2):
            s = jnp.dot(keys_ref[p], qs[p], preferred_element_type=F32)
            cols = [s[j * SUBLANES:(j + 1) * SUBLANES, :] for j in range(N_KEYS // SUBLANES)]
            cidx = [sub + j * SUBLANES for j in range(N_KEYS // SUBLANES)]
            for a, b in _SORT16:
                take = cols[b] > cols[a]
                cols[a], cols[b] = jnp.where(take, cols[b], cols[a]), jnp.where(take, cols[a], cols[b])
                cidx[a], cidx[b] = jnp.where(take, cidx[b], cidx[a]), jnp.where(take, cidx[a], cidx[b])
            vals = jnp.zeros((PEER_TOPK, LANES), F32)
            idxs = jnp.zeros((PEER_TOPK, LANES), I32)
            for r in range(PEER_TOPK):
                v8, i8 = cols[0], cidx[0]
                for sh in (4, 2, 1):
                    vr, ir = pltpu.roll(v8, sh, 0), pltpu.roll(i8, sh, 0)
                    take = vr > v8
                    v8, i8 = jnp.where(take, vr, v8), jnp.where(take, ir, i8)
                m, ix = v8[0:1, :], i8[0:1, :]
                vals = jnp.where(kid == r, m, vals)
                idxs = jnp.where(kid == r, ix, idxs)
                popped = cidx[0] == ix
                for j in range(PEER_TOPK - 1 - r):
                    cols[j] = jnp.where(popped, cols[j + 1], cols[j])
                    cidx[j] = jnp.where(popped, cidx[j + 1], cidx[j])
            tops.append((vals, idxs))
        (a0, i0), (a1, i1) = tops
        lo, hi = slice(0, SUBLANES), slice(SUBLANES, 2 * SUBLANES)
        slabs = [bc(a0, 0) + a1[lo], bc(a0, 0) + a1[hi]]
        ci = [bc(i0, 0), bc(i0, 0)]
        cj = [i1[lo], i1[hi]]
        for r in range(1, SUBLANES):
            slabs.append(bc(a0, r) + a1[lo])
            ci.append(bc(i0, r))
            cj.append(i1[lo])
        slabs.append(a0[hi] + bc(a1, 0))
        ci.append(i0[hi])
        cj.append(bc(i1, 0))
        slabs = [jnp.where(sub < nv, sl, neg) for sl, nv in zip(slabs, _CAND_VALID)]
        ids = [a * N_KEYS + b for a, b in zip(ci, cj)]
        vals = jnp.zeros((PEER_TOPK, LANES), F32)
        esel = jnp.zeros((PEER_TOPK, LANES), I32)
        for r in range(PEER_TOPK):
            level = list(zip(slabs, ids))
            while len(level) > 1:
                nxt = []
                for (va, ea), (vb, eb) in zip(level[0::2], level[1::2]):
                    take = vb > va
                    nxt.append((jnp.where(take, vb, va), jnp.where(take, eb, ea)))
                if len(level) % 2:
                    nxt.append(level[-1])
                level = nxt
            v8, e8 = level[0]
            for sh in (4, 2, 1):
                vr, er = pltpu.roll(v8, sh, 0), pltpu.roll(e8, sh, 0)
                take = vr > v8
                v8, e8 = jnp.where(take, vr, v8), jnp.where(take, er, e8)
            m, ex = v8[0:1, :], e8[0:1, :]
            slabs = [jnp.where(eid == ex, neg, sl) for sl, eid in zip(slabs, ids)]
            vals = jnp.where(kid == r, m, vals)
            esel = jnp.where(kid == r, ex, esel)
        e = jnp.exp(vals - vals[0:1, :])
        return esel, e / jnp.sum(e, axis=0, keepdims=True)

    return head()


def _peer_body(h2_ref, h2n_ref, wqt_ref, keys_ref, u_ref, v_ref, x1_ref, mod_ref, fng_ref,
               oc_ref, ol_ref,
               q_scr, e_scr, g_scr, et_scr, gt_scr, gs_scr, acc_scr,
               *, rows, pitch, units, n_ctx_tiles, final):
    m = pl.program_id(0)
    e = pl.program_id(1)
    tm = h2_ref.shape[0]
    n_chunks = tm // LANES
    nsel = PEER_HEADS * PEER_TOPK
    slot = lax.rem(m, 2)

    def project_queries(src_ref):
        q = lax.dot_general(wqt_ref[...], src_ref[...], _NT, preferred_element_type=F32).astype(BF16)
        for c in range(n_chunks):
            q_scr[c] = q[:, c * LANES:(c + 1) * LANES]

    def route_unit(u, dst):
        c = u // PEER_HEADS
        h = lax.rem(u, PEER_HEADS)
        qs = [q_scr[c, pl.ds(pl.multiple_of(h * (2 * N_KEYS) + p * N_KEYS, N_KEYS), N_KEYS), :]
              for p in range(2)]
        esel, gates = _route_head(qs, keys_ref)
        row = pl.multiple_of(h * PEER_TOPK, PEER_TOPK)
        e_scr[dst, c, pl.ds(row, PEER_TOPK), :] = esel
        g_scr[dst, c, pl.ds(row, PEER_TOPK), :] = gates

    @pl.when(jnp.logical_and(e == 0, m == 0))
    def _first_tile_routing():
        project_queries(h2_ref)

        def unit(u, c):
            route_unit(u, 0)
            return c

        lax.fori_loop(0, n_chunks * PEER_HEADS, unit, 0)

    @pl.when(e == 0)
    def _build():
        for c in range(n_chunks):
            et_scr[c * LANES:(c + 1) * LANES, :] = e_scr[slot, c].T
            gt_scr[c * LANES:(c + 1) * LANES, :] = g_scr[slot, c].T
        kio = lax.broadcasted_iota(I32, (N_KEYS, nsel), 0)

        def tok(t, c):
            erow = et_scr[pl.ds(t, 1), :]
            grow = gt_scr[pl.ds(t, 1), :]
            at = jnp.where(kio == (erow >> 7), grow, 0.0).astype(BF16)
            bt = jnp.where(kio == (erow & (N_KEYS - 1)), 1.0, 0.0).astype(BF16)
            gt = lax.dot_general(at, bt, _NT, preferred_element_type=F32)
            hi = pltpu.bitcast(gt[0:rows, :], jnp.uint32) & jnp.uint32(0xFFFF0000)
            lo = pltpu.bitcast(gt[rows:2 * rows, :], jnp.uint32) >> 16
            gs_scr[pl.ds(pl.multiple_of(t * pitch, SUBLANES), rows), :] = hi | lo
            return c

        lax.fori_loop(0, tm, tok, 0, unroll=128)
        acc_scr[...] = jnp.zeros_like(acc_scr)
        project_queries(h2n_ref)

    for k in range(units):
        route_unit(e * units + k, 1 - slot)

    per_sub = PEER_SUB // N_KEYS
    per_step = u_ref.shape[0] // N_KEYS
    steps_per_half = rows // per_step
    row0 = lax.rem(e, steps_per_half) * per_step
    shift = jnp.where(e < steps_per_half, 0, 16).astype(jnp.uint32)
    h2 = h2_ref[...]
    total = None
    for sb in range(per_step // per_sub):
        es = slice(sb * PEER_SUB, (sb + 1) * PEER_SUB)
        hmat = lax.dot_general(h2, u_ref[es, :], _NT, preferred_element_type=F32)
        words = jnp.concatenate(
            [gs_scr[pl.ds(row0 + sb * per_sub + ii, tm, stride=pitch), :] for ii in range(per_sub)], axis=1)
        gm = pltpu.bitcast((words << shift) & jnp.uint32(0xFFFF0000), F32)
        act = jax.nn.gelu(hmat.astype(BF16)) * gm.astype(BF16)
        part = jnp.dot(act, v_ref[es, :], preferred_element_type=F32)
        total = part if total is None else total + part
    acc_scr[...] += total

    def result():
        x2 = x1_ref[...] + mod_ref[0][5:6, :] * acc_scr[...]
        return _rms(x2, fng_ref[...]) if final else x2

    last = e == pl.num_programs(1) - 1

    @pl.when(jnp.logical_and(last, m < n_ctx_tiles))
    def _out_ctx():
        oc_ref[...] = result()

    @pl.when(jnp.logical_and(last, m >= n_ctx_tiles))
    def _out_lat():
        ol_ref[...] = result()


def _peer(h2, wqt, keys, u, v, layer, x1, mod, fng, n_ctx_tok, lat_len, final):
    t = h2.shape[0]
    nsel = PEER_HEADS * PEER_TOPK
    rows = N_KEYS // 2
    pitch = rows + SUBLANES
    n_e = (N_KEYS * N_KEYS) // PEER_EB
    n_m = t // TM_PEER
    n_chunks = TM_PEER // LANES
    units = (n_chunks * PEER_HEADS) // n_e
    assert units * n_e == n_chunks * PEER_HEADS
    midx = functools.partial(_mod_index, tm=TM_PEER, n_ctx_tok=n_ctx_tok, lat_len=lat_len)
    tok = lambda wd: pl.BlockSpec((TM_PEER, wd), lambda m, e: (m, 0))
    nxt = pl.BlockSpec((TM_PEER, D_MODEL), lambda m, e: (jnp.minimum(m + 1, n_m - 1), 0),
                       pipeline_mode=pl.Buffered(1))
    espec = pl.BlockSpec((None, PEER_EB, D_MODEL), lambda m, e: (layer, e, 0))
    single = dict(pipeline_mode=pl.Buffered(1))
    return pl.pallas_call(
        functools.partial(_peer_body, rows=rows, pitch=pitch, units=units,
                          n_ctx_tiles=n_ctx_tok // TM_PEER, final=final),
        grid=(n_m, n_e),
        in_specs=[tok(D_MODEL), nxt,
                  _layer_spec(wqt, layer, **single),
                  _layer_spec(keys, layer, **single),
                  espec, espec,
                  pl.BlockSpec((TM_PEER, D_MODEL), lambda m, e: (m, 0), **single),
                  pl.BlockSpec((1, N_MOD, D_MODEL), lambda m, e: (midx(m), 0, 0)),
                  _const_spec((1, D_MODEL))],
        out_specs=_dual_specs(TM_PEER, n_ctx_tok // TM_PEER),
        out_shape=[jax.ShapeDtypeStruct((n_ctx_tok, D_MODEL), F32),
                   jax.ShapeDtypeStruct((t - n_ctx_tok, D_MODEL), F32)],
        scratch_shapes=[pltpu.VMEM((n_chunks, wqt.shape[1], LANES), BF16),
                        pltpu.VMEM((2, n_chunks, nsel, LANES), I32),
                        pltpu.VMEM((2, n_chunks, nsel, LANES), F32),
                        pltpu.VMEM((TM_PEER, nsel), I32),
                        pltpu.VMEM((TM_PEER, nsel), F32),
                        pltpu.VMEM((TM_PEER * pitch, N_KEYS), jnp.uint32),
                        pltpu.VMEM((TM_PEER, D_MODEL), F32)],
        compiler_params=_cparams(("arbitrary", "arbitrary"), VMEM_LIMIT),
        name="peer",
    )(h2, h2, wqt, keys, u, v, x1, mod, fng)


def _wkv_state_shape(n_groups, vs):
    vl = HS_WKV // vs
    if _value_rows_on_sublanes(vl):
        return (n_groups, vl // SUBLANES, HS_WKV, SUBLANES, LANES)
    return (n_groups, vl, HS_WKV, LANES)


def _wkv_state_in(s, vs):
    n = s.shape[0]
    vl = HS_WKV // vs
    if _value_rows_on_sublanes(vl):
        s = s.reshape(n, N_DIR, H_WKV, vs, vl // SUBLANES, SUBLANES, HS_WKV).transpose(1, 4, 6, 5, 3, 0, 2)
    else:
        s = s.reshape(n, N_DIR, H_WKV, vs, vl, HS_WKV).transpose(1, 4, 5, 3, 0, 2)
    return s.reshape(_wkv_state_shape(N_DIR, vs))


def _wkv_state_out(s, n_sb, spb, vs):
    vl = HS_WKV // vs
    if _value_rows_on_sublanes(vl):
        s = s.reshape(N_DIR, n_sb, vl // SUBLANES, HS_WKV, SUBLANES, vs, spb, H_WKV)
        s = s.transpose(1, 6, 0, 7, 5, 2, 4, 3)
    else:
        s = s.reshape(N_DIR, n_sb, vl, HS_WKV, vs, spb, H_WKV).transpose(1, 5, 0, 6, 4, 2, 3)
    return s.reshape(n_sb * spb, N_DIR, H_WKV, HS_WKV, HS_WKV)


def _stacked_weights(prm):
    depth = prm["w_in"].shape[0]
    eye_h = jnp.eye(H_LRU, dtype=F32)
    eye_d = jnp.eye(N_DIR, dtype=F32)

    def perm(x, axis=-1):
        x = jnp.moveaxis(x, axis, -1)
        lead = x.shape[:-1]
        x = x.reshape(lead + (H_WKV, HS_WKV)).swapaxes(-1, -2).reshape(lead + (W_MIX,))
        return jnp.moveaxis(x, -1, axis)

    def lru_bd(wt):
        return jnp.einsum("ldhij,hg->lhidgj", wt, eye_h).reshape(depth, W_MIX, N_DIR * W_MIX)

    def lora_bd(wt):
        r = wt.shape[2]
        return jnp.einsum("ldrc,de->ldrec", wt, eye_d).reshape(depth, N_DIR * r, N_DIR * W_MIX)

    w_in = prm["w_in"]
    pad = jnp.zeros((depth, D_MODEL, Z_COLS - 5504), F32)
    rkv = [perm(w_in[:, :, 2560 + j * W_MIX:2560 + (j + 1) * W_MIX]) for j in range(3)]
    w_in_perm = jnp.concatenate(
        [w_in[:, :, 0:1536]] + rkv
        + [w_in[:, :, 1536:2560], w_in[:, :, 4480:5504], w_in[:, :, 4096:4480], pad], axis=2).astype(BF16)
    row = lambda x: x.reshape(depth, 1, -1).astype(F32)
    head_of = np.arange(W_MIX) % H_WKV
    w_branch = prm["w_branch"]
    w_branch = jnp.concatenate(
        [w_branch[:, 0], w_branch[:, 1], perm(w_branch[:, 2], axis=1), w_branch[:, 3]], axis=1)
    return {
        "w_in": w_in_perm,
        "w_gate": w_in[:, :, 5504:].astype(BF16),
        "norm1_g": row(prm["norm1_g"]),
        "norm2_g": row(prm["norm2_g"]),
        "conv_a_w": prm["conv_a_w"],
        "conv_b_w": prm["conv_b_w"],
        "conv_b_b": row(prm["conv_b_b"]),
        "lru_w": jnp.concatenate([lru_bd(prm["lru_wa"]), lru_bd(prm["lru_wx"])], axis=2).astype(BF16),
        "lru_b": jnp.concatenate([row(prm["lru_ba"]), row(prm["lru_bx"])], axis=2),
        "lru_lam": row(prm["lru_lambda"]),
        "w0": row(perm(prm["rwkv_w0"])),
        "w2": lora_bd(perm(prm["rwkv_w2"])).astype(BF16),
        "a0": row(perm(prm["rwkv_a0"])),
        "a2": lora_bd(perm(prm["rwkv_a2"])).astype(BF16),
        "g2": perm(prm["rwkv_g2"]).astype(BF16),
        "kk": row(perm(prm["rwkv_kk"])),
        "ka": row(perm(prm["rwkv_ka"])),
        "rk": row(perm(prm["rwkv_rk"].reshape(depth, W_MIX))),
        "lnx_g": row(perm(prm["lnx_g"])),
        "lnx_b": row(perm(prm["lnx_b"])),
        "sg_ln_g": row(prm["sg_ln_g"]),
        "sg_ln_b": row(prm["sg_ln_b"]),
        "sg_ws": prm["sg_ws"].astype(BF16),
        "sg_bst": prm["sg_bs"].transpose(0, 2, 1),
        "gate_b": row(prm["gate_b"]),
        "w_branch": w_branch.astype(BF16),
        "w_out": prm["w_out"].astype(BF16),
        "wq_t": prm["peer_wq"].transpose(0, 2, 1).astype(BF16),
        "keys": prm["peer_keys"].astype(BF16),
        "ind": jnp.asarray(head_of[:, None] == head_of[None, :], BF16)[None],
    }


def kernel(x_prompt, x_sample, state_lru, state_wkv, c, c_ctx, norm1_g, norm2_g, w_mod, b_mod, w_in, conv_a_w, conv_b_w, conv_b_b, lru_wa, lru_ba, lru_wx, lru_bx, lru_lambda, rwkv_w0, rwkv_w2, rwkv_a0, rwkv_a2, rwkv_g2, rwkv_kk, rwkv_ka, rwkv_rk, lnx_g, lnx_b, sg_ln_g, sg_ln_b, sg_ws, sg_bs, gate_b, w_branch, w_out, peer_wq, peer_keys, peer_u, peer_v, final_norm_g):
    prm = dict(norm1_g=norm1_g, norm2_g=norm2_g, w_in=w_in, conv_a_w=conv_a_w, conv_b_w=conv_b_w,
               conv_b_b=conv_b_b, lru_wa=lru_wa, lru_ba=lru_ba, lru_wx=lru_wx, lru_bx=lru_bx,
               lru_lambda=lru_lambda, rwkv_w0=rwkv_w0, rwkv_w2=rwkv_w2, rwkv_a0=rwkv_a0, rwkv_a2=rwkv_a2,
               rwkv_g2=rwkv_g2, rwkv_kk=rwkv_kk, rwkv_ka=rwkv_ka, rwkv_rk=rwkv_rk, lnx_g=lnx_g,
               lnx_b=lnx_b, sg_ln_g=sg_ln_g, sg_ln_b=sg_ln_b, sg_ws=sg_ws, sg_bs=sg_bs, gate_b=gate_b,
               w_branch=w_branch, w_out=w_out, peer_wq=peer_wq, peer_keys=peer_keys, peer_u=peer_u,
               peer_v=peer_v)
    bc, lc, _ = x_prompt.shape
    bl, ll, _ = x_sample.shape
    depth = w_mod.shape[0]
    n_ctx_tok = bc * lc
    n_ctx_tiles = n_ctx_tok // TM
    lat_tiles = ll // TM
    ctx_spb = min(WKV_CTX_SPB, bc)
    lru_spb = ll // lc
    assert lc == TM and ll % TM_PEER == 0 and n_ctx_tok % TM_PEER == 0 and bl + 1 <= SUBLANES
    assert ll % GRID_W == 0 and TM % GRID_W == 0 and bc % ctx_spb == 0 and n_ctx_tok % ll == 0
    assert LANES % (ctx_spb * H_WKV) == 0 and LANES % (bl * H_WKV) == 0

    cond = jnp.zeros((SUBLANES, D_MODEL), F32).at[0].set(c_ctx).at[1:1 + bl].set(c)
    mods = _modulation(cond, w_mod, b_mod).reshape(depth, SUBLANES, N_MOD, D_MODEL)
    fng = final_norm_g.reshape(1, D_MODEL)
    x_ctx = x_prompt.reshape(n_ctx_tok, D_MODEL)
    x_lat = x_sample.reshape(bl * ll, D_MODEL)
    ctx_vs = LANES // (ctx_spb * H_WKV)
    lat_vs = LANES // (bl * H_WKV)
    n_sb = bc // ctx_spb
    wkv_zero = jnp.zeros(_wkv_state_shape(N_DIR * n_sb, ctx_vs), F32)
    lru_zero = jnp.zeros((n_ctx_tok // ll, lru_spb, N_DIR * W_MIX), F32)
    u_all = peer_u.astype(BF16)
    v_all = peer_v.astype(BF16)
    new_lru, new_wkv = [], []
    pnames = ["ya", "yd", "gbg", "la", "lu", "g", "bon", "rt", "vt", "kkt", "wt", "kt", "bt"]
    lw = _stacked_weights(prm)
    for i in range(depth):
        mod = mods[i]
        p = dict(zip(pnames, _prep(x_ctx, x_lat, mod, lw, i, ll)))

        lat_h0 = jnp.zeros((bl, lru_spb, N_DIR * W_MIX), F32).at[:, 0].set(
            state_lru[:, i].astype(F32).reshape(bl, N_DIR * W_MIX))
        h, lru_s = _lru_scan(p["la"], p["lu"], jnp.concatenate([lru_zero, lat_h0], axis=0),
                             rows=ll, n_ctx_blocks=n_ctx_tok // ll, ctx_cfg=(lru_spb, lc), lat_cfg=(1, ll))
        new_lru.append(lru_s[:n_ctx_tok // ll].reshape(bc, N_DIR, W_MIX))

        wkv_in = [p[n] for n in ("rt", "wt", "kt", "kkt", "bt", "vt")]
        y_c, s_c = _wkv_scan(*wkv_in, wkv_zero, tile0=0, n_seq=bc, seq_tiles=1, spb=ctx_spb)
        y_l, _ = _wkv_scan(*wkv_in, _wkv_state_in(state_wkv[:, i].astype(F32), lat_vs),
                           tile0=n_ctx_tiles, n_seq=bl, seq_tiles=lat_tiles, spb=1)
        new_wkv.append(_wkv_state_out(s_c, n_sb, ctx_spb, ctx_vs))

        tok_in = [p["ya"], p["yd"], p["gbg"], h, p["g"], p["bon"]]
        x1, h2 = _merge(x_ctx, x_lat, mod, lw, i, tok_in, y_c, y_l, ll)
        x_ctx, x_lat = _peer(h2, lw["wq_t"], lw["keys"], u_all, v_all, i, x1, mod, fng,
                             n_ctx_tok, ll, final=(i == depth - 1))
    y_prompt = x_ctx.reshape(bc, lc, D_MODEL)
    y_sample = x_lat.reshape(bl, ll, D_MODEL)
    return (y_prompt, y_sample, jnp.stack(new_lru, axis=1), jnp.stack(new_wkv, axis=1))
```

```python
import functools

import numpy as np
import jax
import jax.numpy as jnp
from jax import lax
from jax.experimental import pallas as pl
from jax.experimental.pallas import tpu as pltpu

F32 = jnp.float32
BF16 = jnp.bfloat16
I32 = jnp.int32

D_MODEL = 1024
W_MIX = 512
N_DIR = 2
H_WKV = 8
HS_WKV = 64
H_LRU = 8
LORA_W = 64
LORA_A = 64
LORA_G = 128
GRID_W = 64
CHUNK = 128
G_SG = 4
N_KEYS = 128
PEER_HEADS = 8
PEER_TOPK = 16
N_MOD = 6
EPS = 1e-6
LNX_EPS = 64e-5
LRU_C = 8.0

LANES = 128
SUBLANES = 8
TM = 256
TM_PEER = 512
PEER_EB = 1024
PEER_SUB = 512
Z_COLS = 5632
WKV_TC = LANES
WKV_CTX_SPB = 8
VMEM_LIMIT = 56 * 1024 * 1024

_NT = (((1,), (1,)), ((), ()))


def _cparams(sem, vmem=None):
    return pltpu.CompilerParams(dimension_semantics=sem, vmem_limit_bytes=vmem)


def _const_spec(shape):
    nd = len(shape)
    return pl.BlockSpec(shape, lambda *_: (0,) * nd)


def _layer_spec(arr, layer, **kw):
    nd = arr.ndim
    idx = layer if arr.shape[0] > 1 else 0
    return pl.BlockSpec((None,) + arr.shape[1:], lambda *_: (idx,) + (0,) * (nd - 1), **kw)


def _softplus(x):
    return jnp.maximum(x, 0.0) + jnp.log1p(jnp.exp(-jnp.abs(x)))


def _rms(x, g):
    return x * lax.rsqrt(jnp.mean(x * x, axis=-1, keepdims=True) + EPS) * g


def _segsum(x, ind):
    hi = x.astype(BF16)
    lo = (x - hi.astype(F32)).astype(BF16)
    return (jnp.dot(hi, ind, preferred_element_type=F32)
            + jnp.dot(lo, ind, preferred_element_type=F32))


def _mod_index(i, tm, n_ctx_tok, lat_len):
    n_ctx_tiles = n_ctx_tok // tm
    tiles_per_seq = lat_len // tm
    return jnp.where(i < n_ctx_tiles, 0, 1 + lax.div(i - n_ctx_tiles, tiles_per_seq))


def _mod_body(s_ref, w_ref, b_ref, o_ref):
    s = s_ref[...]
    s = s * jax.nn.sigmoid(s)
    o_ref[0] = jnp.dot(s.astype(BF16), w_ref[0].astype(BF16), preferred_element_type=F32) + b_ref[0]


def _modulation(cond, w_mod, b_mod):
    depth = w_mod.shape[0]
    n = w_mod.shape[2]
    tn = 1536
    return pl.pallas_call(
        _mod_body,
        grid=(depth, n // tn),
        in_specs=[_const_spec((SUBLANES, D_MODEL)),
                  pl.BlockSpec((1, D_MODEL, tn), lambda l, j: (l, 0, j)),
                  pl.BlockSpec((1, 1, tn), lambda l, j: (l, 0, j))],
        out_specs=pl.BlockSpec((1, SUBLANES, tn), lambda l, j: (l, 0, j)),
        out_shape=jax.ShapeDtypeStruct((depth, SUBLANES, n), F32),
        compiler_params=_cparams(("parallel", "parallel"), VMEM_LIMIT),
        name="modulation",
    )(cond, w_mod, b_mod.reshape(depth, 1, n))


def _prep_body(xc_ref, xl_ref, xp_ref, xn_ref, mod_ref, n1g_ref, win_ref,
               caw_ref, cbw_ref, cbb_ref, lruw_ref, lrub_ref, lam_ref,
               w0_ref, w2_ref, a0_ref, a2_ref, g2_ref, kkw_ref, ka_ref, rk_ref,
               lng_ref, lnb_ref, ws_ref, bst_ref, ind_ref,
               ya_ref, yd_ref, gbg_ref, la_ref, lu_ref, g_ref, bon_ref,
               rt_ref, vt_ref, kkt_ref, wt_ref, kt_ref, bt_ref,
               *, n_ctx_tiles, tiles_per_seq):
    i = pl.program_id(0)
    is_ctx = i < n_ctx_tiles
    t = lax.broadcasted_iota(I32, (TM, 1), 0)
    ind = ind_ref[...]
    m = mod_ref[0]

    def modulated(xv):
        return (_rms(xv, n1g_ref[...]) * (1.0 + m[1:2, :]) + m[0:1, :]).astype(BF16)

    def project(hv, lo, hi):
        return jnp.dot(hv, win_ref[:, lo:hi], preferred_element_type=F32)

    h = modulated(jnp.where(is_ctx, xc_ref[...], xl_ref[...]))
    za = project(h, 0, 1536)
    zc = project(h, 1536, 3072)
    zb = project(h, 3072, 4096)
    zd = project(h, 4096, 5120)
    zl = project(h, 5120, 5120 + 2 * LORA_W + 2 * LORA_A + LORA_G)
    halo_prev = project(modulated(xp_ref[...]), 3072 + W_MIX, 4096)
    halo_next = project(modulated(xn_ref[...]), 3072 + W_MIX, 4096)

    pm = jnp.where(is_ctx, TM - 1, GRID_W - 1)
    pos = t & pm
    a_b = za[:, 0:W_MIX]
    ac = za[:, W_MIX:2 * W_MIX] * za[:, 2 * W_MIX:3 * W_MIX]
    up = jnp.where(pos == 0, 0.0, pltpu.roll(ac, 1, 0))
    dn = jnp.where(pos == pm, 0.0, pltpu.roll(ac, TM - 1, 0))
    ya_ref[...] = a_b * (caw_ref[0:1, :] * up + caw_ref[1:2, :] * ac + caw_ref[2:3, :] * dn)

    seq_tile = lax.rem(jnp.maximum(i - n_ctx_tiles, 0), tiles_per_seq)
    first = jnp.logical_or(is_ctx, seq_tile == 0)
    last = jnp.logical_or(is_ctx, seq_tile == tiles_per_seq - 1)
    prev = jnp.where(first, 0.0, halo_prev[SUBLANES - 1:SUBLANES, :])
    nxt0 = jnp.where(last, 0.0, halo_next[0:1, :])
    nxt1 = jnp.where(last, 0.0, halo_next[1:2, :])
    bx = zb[:, W_MIX:2 * W_MIX]
    m1 = jnp.where(t == 0, prev, pltpu.roll(bx, 1, 0))
    p1 = jnp.where(t == TM - 1, nxt0, pltpu.roll(bx, TM - 1, 0))
    p2 = jnp.where(t == TM - 2, nxt0, jnp.where(t == TM - 1, nxt1, pltpu.roll(bx, TM - 2, 0)))
    xb = (cbw_ref[0:1, :] * m1 + cbw_ref[1:2, :] * bx + cbw_ref[2:3, :] * p1
          + cbw_ref[3:4, :] * p2 + cbb_ref[...])
    gates = jnp.dot(xb.astype(BF16), lruw_ref[...], preferred_element_type=F32) + lrub_ref[...]
    rg = jax.nn.sigmoid(gates[:, 0:2 * W_MIX])
    ig = jax.nn.sigmoid(gates[:, 2 * W_MIX:4 * W_MIX])
    log_a = -LRU_C * rg * _softplus(-lam_ref[...])
    xb2 = jnp.concatenate([xb, xb], axis=1)
    a = jnp.exp(log_a)
    la_ref[...] = a
    lu_ref[...] = jnp.sqrt(jnp.tanh(-log_a) * (a * a + 1.0)) * (ig * xb2)
    gbg_ref[...] = jax.nn.gelu(zb[:, 0:W_MIX])

    zr = zc[:, 0:W_MIX]
    zk = zc[:, W_MIX:2 * W_MIX]
    zv = zc[:, 2 * W_MIX:3 * W_MIX]
    zwd = zl[:, 0:2 * LORA_W]
    zad = zl[:, 2 * LORA_W:2 * LORA_W + 2 * LORA_A]
    zgd = zl[:, 2 * LORA_W + 2 * LORA_A:2 * LORA_W + 2 * LORA_A + LORA_G]
    wlin = w0_ref[...] + jnp.dot(jnp.tanh(zwd).astype(BF16), w2_ref[...], preferred_element_type=F32)
    wt_ref[0] = jnp.exp(-jnp.exp(-_softplus(-wlin) - 0.5)).T
    av = jax.nn.sigmoid(a0_ref[...] + jnp.dot(zad.astype(BF16), a2_ref[...], preferred_element_type=F32))
    g_ref[...] = jnp.dot(jax.nn.sigmoid(zgd).astype(BF16), g2_ref[...], preferred_element_type=F32)
    kkr = zk * kkw_ref[...]
    kkn = kkr / jnp.maximum(jnp.sqrt(_segsum(kkr * kkr, ind)), 1e-12)
    zk2 = jnp.concatenate([zk, zk], axis=1)
    ka2 = jnp.concatenate([ka_ref[...], ka_ref[...]], axis=1)
    kd = zk2 * (1.0 + (av - 1.0) * ka2)
    kt_ref[0] = kd.T
    bt_ref[0] = (jnp.concatenate([kkn, kkn], axis=1) * av).T
    rt_ref[0] = zr.T
    vt_ref[0] = zv.T
    kkt_ref[0] = kkn.T
    bon_ref[...] = _segsum(zr * (kd[:, 0:W_MIX] + kd[:, W_MIX:2 * W_MIX]) * rk_ref[...], ind) * zv

    zg = jax.nn.gelu(zd)
    u = zg[:, 0:W_MIX]
    vv = zg[:, W_MIX:2 * W_MIX]
    vc = vv - jnp.mean(vv, axis=-1, keepdims=True)
    vn = vc * lax.rsqrt(jnp.mean(vc * vc, axis=-1, keepdims=True) + 1e-5) * lng_ref[...] + lnb_ref[...]
    for c in range(TM // CHUNK):
        rs = slice(c * CHUNK, (c + 1) * CHUNK)
        for gi in range(G_SG):
            cs = slice(gi * LANES, (gi + 1) * LANES)
            s = jnp.dot(ws_ref[gi], vn[rs, cs].astype(BF16), preferred_element_type=F32)
            yd_ref[rs, cs] = u[rs, cs] * (s + bst_ref[:, gi:gi + 1])


def _dual_specs(rows, n_ctx_blocks, **kw):
    return [pl.BlockSpec((rows, D_MODEL), lambda i, *_: (jnp.minimum(i, n_ctx_blocks - 1), 0), **kw),
            pl.BlockSpec((rows, D_MODEL), lambda i, *_: (jnp.maximum(i - n_ctx_blocks, 0), 0), **kw)]


def _prep(x_ctx, x_lat, mod, lw, layer, lat_len):
    n_ctx_tok = x_ctx.shape[0]
    t = n_ctx_tok + x_lat.shape[0]
    n_tiles = t // TM
    n_ctx_tiles = n_ctx_tok // TM
    tiles_per_seq = lat_len // TM
    rows8 = TM // SUBLANES
    last_blk = x_lat.shape[0] // SUBLANES - 1
    midx = functools.partial(_mod_index, tm=TM, n_ctx_tok=n_ctx_tok, lat_len=lat_len)

    def lat_blk8(i, off):
        return (jnp.clip((i - n_ctx_tiles) * rows8 + off, 0, last_blk), 0)

    x_specs = _dual_specs(TM, n_ctx_tiles) + [
        pl.BlockSpec((SUBLANES, D_MODEL), lambda i: lat_blk8(i, -1)),
        pl.BlockSpec((SUBLANES, D_MODEL), lambda i: lat_blk8(i, rows8)),
        pl.BlockSpec((1, N_MOD, D_MODEL), lambda i: (midx(i), 0, 0)),
        _layer_spec(lw["norm1_g"], layer),
        _layer_spec(lw["w_in"], layer, pipeline_mode=pl.Buffered(1)),
    ]
    wnames = ["conv_a_w", "conv_b_w", "conv_b_b", "lru_w", "lru_b", "lru_lam", "w0", "w2", "a0", "a2",
              "g2", "kk", "ka", "rk", "sg_ln_g", "sg_ln_b", "sg_ws", "sg_bst", "ind"]
    wts = [lw[n] for n in wnames]
    w_specs = [_layer_spec(w, layer) for w in wts]
    widths = [W_MIX, W_MIX, W_MIX, 2 * W_MIX, 2 * W_MIX, W_MIX, W_MIX]
    t_rows = [W_MIX, W_MIX, W_MIX, 2 * W_MIX, 2 * W_MIX, 2 * W_MIX]
    out_specs = ([pl.BlockSpec((TM, wd), lambda i: (i, 0)) for wd in widths]
                 + [pl.BlockSpec((1, r, TM), lambda i: (i, 0, 0)) for r in t_rows])
    out_shape = ([jax.ShapeDtypeStruct((t, wd), F32) for wd in widths]
                 + [jax.ShapeDtypeStruct((n_tiles, r, TM), F32) for r in t_rows])
    return pl.pallas_call(
        functools.partial(_prep_body, n_ctx_tiles=n_ctx_tiles, tiles_per_seq=tiles_per_seq),
        grid=(n_tiles,),
        in_specs=x_specs + w_specs,
        out_specs=out_specs,
        out_shape=out_shape,
        compiler_params=_cparams(("parallel",), VMEM_LIMIT),
        name="branch_prep",
    )(x_ctx, x_lat, x_lat, x_lat, mod, lw["norm1_g"], lw["w_in"], *wts)


def _lru_body(a_ref, u_ref, h0_ref, h_ref, hf_ref, *, n_ctx_blocks, ctx_cfg, lat_cfg):
    fw, bw = slice(0, W_MIX), slice(W_MIX, 2 * W_MIX)

    def scan(nseq, l):
        def step(s, carry):
            out = []
            for j in range(nseq):
                tf = j * l + s
                tb = j * l + (l - 1 - s)
                hf = a_ref[pl.ds(tf, 1), fw] * carry[2 * j] + u_ref[pl.ds(tf, 1), fw]
                hb = a_ref[pl.ds(tb, 1), bw] * carry[2 * j + 1] + u_ref[pl.ds(tb, 1), bw]
                h_ref[pl.ds(tf, 1), fw] = hf
                h_ref[pl.ds(tb, 1), bw] = hb
                out += [hf, hb]
            return tuple(out)

        init = []
        for j in range(nseq):
            init += [h0_ref[0, j:j + 1, fw], h0_ref[0, j:j + 1, bw]]
        fin = lax.fori_loop(0, l, step, tuple(init), unroll=2)
        hf_ref[0] = h0_ref[0]
        for j in range(nseq):
            hf_ref[0, j:j + 1, fw] = fin[2 * j]
            hf_ref[0, j:j + 1, bw] = fin[2 * j + 1]

    is_ctx = pl.program_id(0) < n_ctx_blocks
    pl.when(is_ctx)(lambda: scan(*ctx_cfg))
    pl.when(jnp.logical_not(is_ctx))(lambda: scan(*lat_cfg))


def _lru_scan(a, u, h0, *, rows, n_ctx_blocks, ctx_cfg, lat_cfg):
    nb = a.shape[0] // rows
    w = a.shape[1]
    tok = pl.BlockSpec((rows, w), lambda i: (i, 0))
    st = pl.BlockSpec((1,) + h0.shape[1:], lambda i: (i, 0, 0))
    return pl.pallas_call(
        functools.partial(_lru_body, n_ctx_blocks=n_ctx_blocks, ctx_cfg=ctx_cfg, lat_cfg=lat_cfg),
        grid=(nb,),
        in_specs=[tok, tok, st],
        out_specs=[tok, st],
        out_shape=[jax.ShapeDtypeStruct(a.shape, F32), jax.ShapeDtypeStruct(h0.shape, F32)],
        compiler_params=_cparams(("parallel",), VMEM_LIMIT),
        name="lru_scan",
    )(a, u, h0)


_SLOT_ORDER = (0, 4, 2, 6, 1, 5, 3, 7)


def _rowsum8(parts):
    sub = lax.broadcasted_iota(I32, (SUBLANES, LANES), 0)
    slots = [parts[i] for i in _SLOT_ORDER]
    roll = pltpu.roll
    lvl1 = [jnp.where(sub < 4, a + roll(a, 4, 0), b + roll(b, 4, 0))
            for a, b in zip(slots[0::2], slots[1::2])]
    lvl2 = [jnp.where((sub & 3) < 2, a + roll(a, 6, 0), roll(b + roll(b, 6, 0), 2, 0))
            for a, b in zip(lvl1[0::2], lvl1[1::2])]
    a, b = lvl2
    return jnp.where((sub & 1) == 0, a + roll(a, 7, 0), roll(b + roll(b, 7, 0), 1, 0))


def _fold8(x):
    return jnp.sum(x.reshape(HS_WKV // SUBLANES, SUBLANES, LANES), axis=0)


def _value_rows_on_sublanes(vl_n):
    return vl_n // SUBLANES < 4


def _wkv_body(*refs, nsrc, spb, vs, n_sb, tc, kp, vp):
    vl_n = HS_WKV // vs
    n_in = 6 * nsrc
    k_srcs = [refs[o * nsrc:(o + 1) * nsrc] for o in range(5)]
    v_srcs = refs[5 * nsrc:n_in]
    s0_ref = refs[n_in]
    y_ref = refs[n_in + 1]
    sf_ref = refs[n_in + 2]
    k_scr = refs[n_in + 3:n_in + 8]
    v_scr, y_scr, s_scr = refs[n_in + 8:n_in + 11]
    rows_on_sublanes = _value_rows_on_sublanes(HS_WKV // vs)
    sa_scr = None if rows_on_sublanes else refs[n_in + 11]
    r_scr, w_scr, k_scr_, kk_scr, b_scr = k_scr
    backward = pl.program_id(0) // n_sb == 1
    seqs = [(s, j) for s in range(nsrc) for j in range(spb)]

    @pl.when(pl.program_id(1) == 0)
    def _():
        s_scr[...] = s0_ref[0]

    def build_k(c, carry):
        row = pl.multiple_of(c * H_WKV, H_WKV)
        for o in range(5):
            slab = [k_srcs[o][s][j, pl.ds(row, H_WKV), :] for s, j in seqs]
            k_scr[o][pl.ds(c, tc, stride=kp), :] = jnp.concatenate(slab * vs, axis=0).T
        return carry

    lax.fori_loop(0, HS_WKV, build_k, 0, unroll=4)

    def build_v(vl, carry):
        slab = []
        for vsi in range(vs):
            row = pl.multiple_of((vsi * vl_n + vl) * H_WKV, H_WKV)
            slab += [v_srcs[s][j, pl.ds(row, H_WKV), :] for s, j in seqs]
        v_scr[pl.ds(vl, tc, stride=vp), :] = jnp.concatenate(slab, axis=0).T
        return carry

    lax.fori_loop(0, vl_n, build_v, 0, unroll=4)

    def step_keys_on_sublanes(s, carry):
        t = jnp.where(backward, tc - 1 - s, s)
        krow = pl.multiple_of(t * kp, SUBLANES)
        vrow = pl.multiple_of(t * vp, SUBLANES)
        kslab = pl.ds(krow, HS_WKV)
        for g in range(vl_n // SUBLANES):
            parts = [_fold8(s_scr[g * SUBLANES + i] * kk_scr[kslab, :]) for i in range(SUBLANES)]
            sa_scr[g * SUBLANES:(g + 1) * SUBLANES, :] = _rowsum8(parts)
        for g in range(vl_n // SUBLANES):
            parts = []
            for i in range(SUBLANES):
                vl = g * SUBLANES + i
                sa = sa_scr[vl:vl + 1, :]
                vv = v_scr[pl.ds(vrow + vl, 1), :]
                sn = s_scr[vl] * w_scr[kslab, :] - sa * b_scr[kslab, :] + vv * k_scr_[kslab, :]
                s_scr[vl] = sn
                parts.append(_fold8(sn * r_scr[kslab, :]))
            y_scr[pl.ds(pl.multiple_of(vrow + g * SUBLANES, SUBLANES), SUBLANES), :] = _rowsum8(parts)
        return carry

    def step_rows_on_sublanes(s, carry):
        t = jnp.where(backward, tc - 1 - s, s)
        krow = pl.multiple_of(t * kp, SUBLANES)
        vrow = pl.multiple_of(t * vp, SUBLANES)
        n_g = vl_n // SUBLANES
        batch = min(n_g, 2)
        n_acc = 4 // batch

        def row(ref, k):
            return jnp.broadcast_to(ref[pl.ds(krow + k, 1), :], (SUBLANES, LANES))

        def total(parts):
            while len(parts) > 1:
                parts = [a + b for a, b in zip(parts[0::2], parts[1::2])]
            return parts[0]

        def accumulate(acc, g, k, p):
            acc[g][k % n_acc] = p if acc[g][k % n_acc] is None else acc[g][k % n_acc] + p

        for g0 in range(0, n_g, batch):
            gs = range(g0, g0 + batch)
            acc = {g: [None] * n_acc for g in gs}
            for k in range(HS_WKV):
                kk = row(kk_scr, k)
                for g in gs:
                    accumulate(acc, g, k, s_scr[g, k] * kk)
            sa = {g: total(acc[g]) for g in gs}
            vv = {g: v_scr[pl.ds(pl.multiple_of(vrow + g * SUBLANES, SUBLANES), SUBLANES), :] for g in gs}
            acc = {g: [None] * n_acc for g in gs}
            for k in range(HS_WKV):
                w, b, kx, r = row(w_scr, k), row(b_scr, k), row(k_scr_, k), row(r_scr, k)
                for g in gs:
                    sn = s_scr[g, k] * w - sa[g] * b + vv[g] * kx
                    s_scr[g, k] = sn
                    accumulate(acc, g, k, sn * r)
            for g in gs:
                y_scr[pl.ds(pl.multiple_of(vrow + g * SUBLANES, SUBLANES), SUBLANES), :] = total(acc[g])
        return carry

    lax.fori_loop(0, tc, step_rows_on_sublanes if rows_on_sublanes else step_keys_on_sublanes, 0)

    def emit_y(vl, carry):
        yt = y_scr[pl.ds(vl, tc, stride=vp), :].T
        for vsi in range(vs):
            row = pl.multiple_of((vsi * vl_n + vl) * H_WKV, H_WKV)
            for n, (s, j) in enumerate(seqs):
                lane0 = (vsi * len(seqs) + n) * H_WKV
                y_ref[0, 0, s * spb + j, pl.ds(row, H_WKV), :] = yt[lane0:lane0 + H_WKV, :]
        return carry

    lax.fori_loop(0, vl_n, emit_y, 0, unroll=4)
    sf_ref[0] = s_scr[...]


def _wkv_scan(rt, wt, kt, kkt, bt, vt, s0, *, tile0, n_seq, seq_tiles, spb):
    tc = WKV_TC
    if spb > 1:
        assert seq_tiles == 1 and n_seq % spb == 0 and tile0 % spb == 0
        nsrc, n_sb = 1, n_seq // spb
    else:
        nsrc, n_sb = n_seq, 1
    inst = nsrc * spb * H_WKV
    vs = LANES // inst
    vl_n = HS_WKV // vs
    assert vl_n % SUBLANES == 0, "value rows are processed eight at a time"
    cpt = TM // tc
    n_chunks = seq_tiles * cpt
    kp = HS_WKV + SUBLANES
    vp = vl_n + SUBLANES if ((vl_n + SUBLANES) // SUBLANES) % 2 else vl_n + 2 * SUBLANES

    def chunk(g, i):
        return jnp.where(g // n_sb == 1, n_chunks - 1 - i, i)

    def in_map(g, i, *, src, per_dir):
        ce = chunk(g, i)
        rb = (g // n_sb) if per_dir else 0
        if spb > 1:
            return (tile0 // spb + g % n_sb, rb, ce)
        return (tile0 + src * seq_tiles + ce // cpt, rb, ce % cpt)

    def out_map(g, i):
        ce = chunk(g, i)
        if spb > 1:
            return (g // n_sb, 0, g % n_sb, 0, ce)
        return (g // n_sb, ce // cpt, 0, 0, ce % cpt)

    in_specs, operands = [], []
    for arr, per_dir in ((rt, False), (wt, True), (kt, True), (kkt, False), (bt, True), (vt, False)):
        for src in range(nsrc):
            in_specs.append(pl.BlockSpec((spb, W_MIX, tc), functools.partial(in_map, src=src, per_dir=per_dir),
                                         pipeline_mode=pl.Buffered(1)))
            operands.append(arr)
    state_block = (1,) + _wkv_state_shape(1, vs)[1:]
    sspec = pl.BlockSpec(state_block, lambda g, i: (g,) + (0,) * (len(state_block) - 1))
    in_specs.append(sspec)
    out_specs = [pl.BlockSpec((1, 1, nsrc * spb, W_MIX, tc), out_map)]
    out_shape = [jax.ShapeDtypeStruct((N_DIR, seq_tiles, n_seq, W_MIX, TM), F32)]
    res = pl.pallas_call(
        functools.partial(_wkv_body, nsrc=nsrc, spb=spb, vs=vs, n_sb=n_sb, tc=tc, kp=kp, vp=vp),
        grid=(N_DIR * n_sb, n_chunks),
        in_specs=in_specs,
        out_specs=out_specs + [sspec],
        out_shape=out_shape + [jax.ShapeDtypeStruct(s0.shape, F32)],
        scratch_shapes=([pltpu.VMEM((tc * kp, LANES), F32)] * 5
                        + [pltpu.VMEM((tc * vp, LANES), F32)] * 2
                        + [pltpu.VMEM(state_block[1:], F32)]
                        + ([] if _value_rows_on_sublanes(vl_n) else [pltpu.VMEM((vl_n, LANES), F32)])),
        compiler_params=_cparams(("parallel", "arbitrary"), VMEM_LIMIT),
        name="wkv_scan",
    )(*operands, s0)
    return res[0], res[1]


def _merge_body(xc_ref, xl_ref, mod_ref, n1g_ref, n2g_ref, wg_ref, gb_ref, wbr_ref, wo_ref,
                lnxg_ref, lnxb_ref, ind_ref,
                ya_ref, yd_ref, gbg_ref, h_ref, ycf_ref, ycb_ref, ylf_ref, ylb_ref, g_ref, bon_ref,
                x1_ref, h2_ref, *, n_ctx_tiles):
    is_ctx = pl.program_id(0) < n_ctx_tiles
    x = jnp.where(is_ctx, xc_ref[...], xl_ref[...])
    m = mod_ref[0]
    ind = ind_ref[...]
    h = (_rms(x, n1g_ref[...]) * (1.0 + m[1:2, :]) + m[0:1, :]).astype(BF16)
    y_b = gbg_ref[...] * (h_ref[:, 0:W_MIX] + h_ref[:, W_MIX:2 * W_MIX])
    y = jnp.where(is_ctx, ycf_ref[0, 0, 0] + ycb_ref[0, 0, 0], ylf_ref[0, 0, 0] + ylb_ref[0, 0, 0]).T
    yc = y - _segsum(y, ind) * (1.0 / HS_WKV)
    var = _segsum(yc * yc, ind) * (1.0 / HS_WKV)
    y_c = (yc * lax.rsqrt(var + LNX_EPS) * lnxg_ref[...] + lnxb_ref[...] + bon_ref[...]) * g_ref[...]
    merged = None
    for n, yn in enumerate((ya_ref[...], y_b, y_c, yd_ref[...])):
        cs = slice(n * D_MODEL, (n + 1) * D_MODEL)
        gate = jax.nn.sigmoid(jnp.dot(h, wg_ref[:, cs], preferred_element_type=F32) + gb_ref[:, cs])
        br = jnp.dot(yn.astype(BF16), wbr_ref[n * W_MIX:(n + 1) * W_MIX, :], preferred_element_type=F32)
        merged = gate * br if merged is None else merged + gate * br
    mo = jnp.dot(merged.astype(BF16), wo_ref[...], preferred_element_type=F32)
    x1 = x + m[2:3, :] * mo
    x1_ref[...] = x1
    h2_ref[...] = (_rms(x1, n2g_ref[...]) * (1.0 + m[4:5, :]) + m[3:4, :]).astype(BF16)


def _merge(x_ctx, x_lat, mod, lw, layer, tok_in, y_ctx, y_lat, lat_len):
    n_ctx_tok = x_ctx.shape[0]
    t = n_ctx_tok + x_lat.shape[0]
    n_ctx_tiles = n_ctx_tok // TM
    tps = lat_len // TM
    midx = functools.partial(_mod_index, tm=TM, n_ctx_tok=n_ctx_tok, lat_len=lat_len)
    wnames = ["norm1_g", "norm2_g", "w_gate", "gate_b", "w_branch", "w_out", "lnx_g", "lnx_b", "ind"]
    wts = [lw[n] for n in wnames]
    tok = lambda wd: pl.BlockSpec((TM, wd), lambda i: (i, 0))
    ya, yd, gbg, h, g, bon = tok_in
    yblock = (1, 1, 1, W_MIX, TM)

    def ctx_spec(d):
        return pl.BlockSpec(yblock, lambda i: (d, 0, jnp.minimum(i, n_ctx_tiles - 1), 0, 0))

    def lat_spec(d):
        def imap(i):
            r = jnp.maximum(i - n_ctx_tiles, 0)
            return (d, lax.rem(r, tps), lax.div(r, tps), 0, 0)
        return pl.BlockSpec(yblock, imap)

    return pl.pallas_call(
        functools.partial(_merge_body, n_ctx_tiles=n_ctx_tiles),
        grid=(t // TM,),
        in_specs=(_dual_specs(TM, n_ctx_tiles)
                  + [pl.BlockSpec((1, N_MOD, D_MODEL), lambda i: (midx(i), 0, 0))]
                  + [_layer_spec(w, layer) for w in wts]
                  + [tok(W_MIX), tok(W_MIX), tok(W_MIX), tok(2 * W_MIX),
                     ctx_spec(0), ctx_spec(1), lat_spec(0), lat_spec(1), tok(W_MIX), tok(W_MIX)]),
        out_specs=[tok(D_MODEL), tok(D_MODEL)],
        out_shape=[jax.ShapeDtypeStruct((t, D_MODEL), F32), jax.ShapeDtypeStruct((t, D_MODEL), BF16)],
        compiler_params=_cparams(("parallel",), VMEM_LIMIT),
        name="merge",
    )(x_ctx, x_lat, mod, *wts, ya, yd, gbg, h, y_ctx, y_ctx, y_lat, y_lat, g, bon)


_CAND_VALID = (8, 8, 8, 5, 4, 3, 2, 2, 2, 8)


def _oddeven_pairs(n):
    def merge(lo, hi, r):
        step = r * 2
        if step < hi - lo:
            yield from merge(lo, hi, step)
            yield from merge(lo + r, hi, step)
            yield from [(i, i + r) for i in range(lo + r, hi - r, step)]
        else:
            yield (lo, lo + r)

    def sort(lo, hi):
        if hi - lo >= 1:
            mid = lo + (hi - lo) // 2
            yield from sort(lo, mid)
            yield from sort(mid + 1, hi)
            yield from merge(lo, hi, 1)

    return tuple(sort(0, n - 1))


_SORT16 = _oddeven_pairs(N_KEYS // SUBLANES)


def _route_head(qs, keys_ref):
    sub = lax.broadcasted_iota(I32, (SUBLANES, LANES), 0)
    kid = lax.broadcasted_iota(I32, (PEER_TOPK, LANES), 0)
    neg = -jnp.inf

    def bc(x, r):
        return jnp.broadcast_to(x[r:r + 1, :], (SUBLANES, LANES))

    def head():
        tops = []
        for p in range(2):
            s = jnp.dot(keys_ref[p], qs[p], preferred_element_type=F32)
            cols = [s[j * SUBLANES:(j + 1) * SUBLANES, :] for j in range(N_KEYS // SUBLANES)]
            cidx = [sub + j * SUBLANES for j in range(N_KEYS // SUBLANES)]
            for a, b in _SORT16:
                take = cols[b] > cols[a]
                cols[a], cols[b] = jnp.where(take, cols[b], cols[a]), jnp.where(take, cols[a], cols[b])
                cidx[a], cidx[b] = jnp.where(take, cidx[b], cidx[a]), jnp.where(take, cidx[a], cidx[b])
            vals = jnp.zeros((PEER_TOPK, LANES), F32)
            idxs = jnp.zeros((PEER_TOPK, LANES), I32)
            for r in range(PEER_TOPK):
                v8, i8 = cols[0], cidx[0]
                for sh in (4, 2, 1):
                    vr, ir = pltpu.roll(v8, sh, 0), pltpu.roll(i8, sh, 0)
                    take = vr > v8
                    v8, i8 = jnp.where(take, vr, v8), jnp.where(take, ir, i8)
                m, ix = v8[0:1, :], i8[0:1, :]
                vals = jnp.where(kid == r, m, vals)
                idxs = jnp.where(kid == r, ix, idxs)
                popped = cidx[0] == ix
                for j in range(PEER_TOPK - 1 - r):
                    cols[j] = jnp.where(popped, cols[j + 1], cols[j])
                    cidx[j] = jnp.where(popped, cidx[j + 1], cidx[j])
            tops.append((vals, idxs))
        (a0, i0), (a1, i1) = tops
        lo, hi = slice(0, SUBLANES), slice(SUBLANES, 2 * SUBLANES)
        slabs = [bc(a0, 0) + a1[lo], bc(a0, 0) + a1[hi]]
        ci = [bc(i0, 0), bc(i0, 0)]
        cj = [i1[lo], i1[hi]]
        for r in range(1, SUBLANES):
            slabs.append(bc(a0, r) + a1[lo])
            ci.append(bc(i0, r))
            cj.append(i1[lo])
        slabs.append(a0[hi] + bc(a1, 0))
        ci.append(i0[hi])
        cj.append(bc(i1, 0))
        slabs = [jnp.where(sub < nv, sl, neg) for sl, nv in zip(slabs, _CAND_VALID)]
        ids = [a * N_KEYS + b for a, b in zip(ci, cj)]
        vals = jnp.zeros((PEER_TOPK, LANES), F32)
        esel = jnp.zeros((PEER_TOPK, LANES), I32)
        for r in range(PEER_TOPK):
            level = list(zip(slabs, ids))
            while len(level) > 1:
                nxt = []
                for (va, ea), (vb, eb) in zip(level[0::2], level[1::2]):
                    take = vb > va
                    nxt.append((jnp.where(take, vb, va), jnp.where(take, eb, ea)))
                if len(level) % 2:
                    nxt.append(level[-1])
                level = nxt
            v8, e8 = level[0]
            for sh in (4, 2, 1):
                vr, er = pltpu.roll(v8, sh, 0), pltpu.roll(e8, sh, 0)
                take = vr > v8
                v8, e8 = jnp.where(take, vr, v8), jnp.where(take, er, e8)
            m, ex = v8[0:1, :], e8[0:1, :]
            slabs = [jnp.where(eid == ex, neg, sl) for sl, eid in zip(slabs, ids)]
            vals = jnp.where(kid == r, m, vals)
            esel = jnp.where(kid == r, ex, esel)
        e = jnp.exp(vals - vals[0:1, :])
        return esel, e / jnp.sum(e, axis=0, keepdims=True)

    return head()


def _peer_body(h2_ref, h2n_ref, wqt_ref, keys_ref, u_ref, v_ref, x1_ref, mod_ref, fng_ref,
               oc_ref, ol_ref,
               q_scr, e_scr, g_scr, et_scr, gt_scr, gs_scr, acc_scr,
               *, rows, pitch, units, n_ctx_tiles, final):
    m = pl.program_id(0)
    e = pl.program_id(1)
    tm = h2_ref.shape[0]
    n_chunks = tm // LANES
    nsel = PEER_HEADS * PEER_TOPK
    slot = lax.rem(m, 2)

    def project_queries(src_ref):
        q = lax.dot_general(wqt_ref[...], src_ref[...], _NT, preferred_element_type=F32).astype(BF16)
        for c in range(n_chunks):
            q_scr[c] = q[:, c * LANES:(c + 1) * LANES]

    def route_unit(u, dst):
        c = u // PEER_HEADS
        h = lax.rem(u, PEER_HEADS)
        qs = [q_scr[c, pl.ds(pl.multiple_of(h * (2 * N_KEYS) + p * N_KEYS, N_KEYS), N_KEYS), :]
              for p in range(2)]
        esel, gates = _route_head(qs, keys_ref)
        row = pl.multiple_of(h * PEER_TOPK, PEER_TOPK)
        e_scr[dst, c, pl.ds(row, PEER_TOPK), :] = esel
        g_scr[dst, c, pl.ds(row, PEER_TOPK), :] = gates

    @pl.when(jnp.logical_and(e == 0, m == 0))
    def _first_tile_routing():
        project_queries(h2_ref)

        def unit(u, c):
            route_unit(u, 0)
            return c

        lax.fori_loop(0, n_chunks * PEER_HEADS, unit, 0)

    @pl.when(e == 0)
    def _build():
        for c in range(n_chunks):
            et_scr[c * LANES:(c + 1) * LANES, :] = e_scr[slot, c].T
            gt_scr[c * LANES:(c + 1) * LANES, :] = g_scr[slot, c].T
        kio = lax.broadcasted_iota(I32, (N_KEYS, nsel), 0)

        def tok(t, c):
            erow = et_scr[pl.ds(t, 1), :]
            grow = gt_scr[pl.ds(t, 1), :]
            at = jnp.where(kio == (erow >> 7), grow, 0.0).astype(BF16)
            bt = jnp.where(kio == (erow & (N_KEYS - 1)), 1.0, 0.0).astype(BF16)
            gt = lax.dot_general(at, bt, _NT, preferred_element_type=F32)
            hi = pltpu.bitcast(gt[0:rows, :], jnp.uint32) & jnp.uint32(0xFFFF0000)
            lo = pltpu.bitcast(gt[rows:2 * rows, :], jnp.uint32) >> 16
            gs_scr[pl.ds(pl.multiple_of(t * pitch, SUBLANES), rows), :] = hi | lo
            return c

        lax.fori_loop(0, tm, tok, 0, unroll=128)
        acc_scr[...] = jnp.zeros_like(acc_scr)
        project_queries(h2n_ref)

    for k in range(units):
        route_unit(e * units + k, 1 - slot)

    per_sub = PEER_SUB // N_KEYS
    per_step = u_ref.shape[0] // N_KEYS
    steps_per_half = rows // per_step
    row0 = lax.rem(e, steps_per_half) * per_step
    shift = jnp.where(e < steps_per_half, 0, 16).astype(jnp.uint32)
    h2 = h2_ref[...]
    total = None
    for sb in range(per_step // per_sub):
        es = slice(sb * PEER_SUB, (sb + 1) * PEER_SUB)
        hmat = lax.dot_general(h2, u_ref[es, :], _NT, preferred_element_type=F32)
        words = jnp.concatenate(
            [gs_scr[pl.ds(row0 + sb * per_sub + ii, tm, stride=pitch), :] for ii in range(per_sub)], axis=1)
        gm = pltpu.bitcast((words << shift) & jnp.uint32(0xFFFF0000), F32)
        act = jax.nn.gelu(hmat.astype(BF16)) * gm.astype(BF16)
        part = jnp.dot(act, v_ref[es, :], preferred_element_type=F32)
        total = part if total is None else total + part
    acc_scr[...] += total

    def result():
        x2 = x1_ref[...] + mod_ref[0][5:6, :] * acc_scr[...]
        return _rms(x2, fng_ref[...]) if final else x2

    last = e == pl.num_programs(1) - 1

    @pl.when(jnp.logical_and(last, m < n_ctx_tiles))
    def _out_ctx():
        oc_ref[...] = result()

    @pl.when(jnp.logical_and(last, m >= n_ctx_tiles))
    def _out_lat():
        ol_ref[...] = result()


def _peer(h2, wqt, keys, u, v, layer, x1, mod, fng, n_ctx_tok, lat_len, final):
    t = h2.shape[0]
    nsel = PEER_HEADS * PEER_TOPK
    rows = N_KEYS // 2
    pitch = rows + SUBLANES
    n_e = (N_KEYS * N_KEYS) // PEER_EB
    n_m = t // TM_PEER
    n_chunks = TM_PEER // LANES
    units = (n_chunks * PEER_HEADS) // n_e
    assert units * n_e == n_chunks * PEER_HEADS
    midx = functools.partial(_mod_index, tm=TM_PEER, n_ctx_tok=n_ctx_tok, lat_len=lat_len)
    tok = lambda wd: pl.BlockSpec((TM_PEER, wd), lambda m, e: (m, 0))
    nxt = pl.BlockSpec((TM_PEER, D_MODEL), lambda m, e: (jnp.minimum(m + 1, n_m - 1), 0),
                       pipeline_mode=pl.Buffered(1))
    espec = pl.BlockSpec((None, PEER_EB, D_MODEL), lambda m, e: (layer, e, 0))
    single = dict(pipeline_mode=pl.Buffered(1))
    return pl.pallas_call(
        functools.partial(_peer_body, rows=rows, pitch=pitch, units=units,
                          n_ctx_tiles=n_ctx_tok // TM_PEER, final=final),
        grid=(n_m, n_e),
        in_specs=[tok(D_MODEL), nxt,
                  _layer_spec(wqt, layer, **single),
                  _layer_spec(keys, layer, **single),
                  espec, espec,
                  pl.BlockSpec((TM_PEER, D_MODEL), lambda m, e: (m, 0), **single),
                  pl.BlockSpec((1, N_MOD, D_MODEL), lambda m, e: (midx(m), 0, 0)),
                  _const_spec((1, D_MODEL))],
        out_specs=_dual_specs(TM_PEER, n_ctx_tok // TM_PEER),
        out_shape=[jax.ShapeDtypeStruct((n_ctx_tok, D_MODEL), F32),
                   jax.ShapeDtypeStruct((t - n_ctx_tok, D_MODEL), F32)],
        scratch_shapes=[pltpu.VMEM((n_chunks, wqt.shape[1], LANES), BF16),
                        pltpu.VMEM((2, n_chunks, nsel, LANES), I32),
                        pltpu.VMEM((2, n_chunks, nsel, LANES), F32),
                        pltpu.VMEM((TM_PEER, nsel), I32),
                        pltpu.VMEM((TM_PEER, nsel), F32),
                        pltpu.VMEM((TM_PEER * pitch, N_KEYS), jnp.uint32),
                        pltpu.VMEM((TM_PEER, D_MODEL), F32)],
        compiler_params=_cparams(("arbitrary", "arbitrary"), VMEM_LIMIT),
        name="peer",
    )(h2, h2, wqt, keys, u, v, x1, mod, fng)


def _wkv_state_shape(n_groups, vs):
    vl = HS_WKV // vs
    if _value_rows_on_sublanes(vl):
        return (n_groups, vl // SUBLANES, HS_WKV, SUBLANES, LANES)
    return (n_groups, vl, HS_WKV, LANES)


def _wkv_state_in(s, vs):
    n, depth = s.shape[:2]
    vl = HS_WKV // vs
    if _value_rows_on_sublanes(vl):
        s = s.reshape(n, depth, N_DIR, H_WKV, vs, vl // SUBLANES, SUBLANES, HS_WKV)
        s = s.transpose(1, 2, 5, 7, 6, 4, 0, 3)
    else:
        s = s.reshape(n, depth, N_DIR, H_WKV, vs, vl, HS_WKV).transpose(1, 2, 5, 6, 4, 0, 3)
    return s.reshape((depth,) + _wkv_state_shape(N_DIR, vs))


def _wkv_state_out(layers, n_sb, spb, vs):
    depth = len(layers)
    s = jnp.stack(layers, axis=0)
    vl = HS_WKV // vs
    if _value_rows_on_sublanes(vl):
        s = s.reshape(depth, N_DIR, n_sb, vl // SUBLANES, HS_WKV, SUBLANES, vs, spb, H_WKV)
        s = s.transpose(2, 7, 0, 1, 8, 6, 3, 5, 4)
    else:
        s = s.reshape(depth, N_DIR, n_sb, vl, HS_WKV, vs, spb, H_WKV).transpose(2, 6, 0, 1, 7, 5, 3, 4)
    return s.reshape(n_sb * spb, depth, N_DIR, H_WKV, HS_WKV, HS_WKV)


def _stacked_weights(prm):
    depth = prm["w_in"].shape[0]
    eye_h = jnp.eye(H_LRU, dtype=F32)
    eye_d = jnp.eye(N_DIR, dtype=F32)

    def perm(x, axis=-1):
        x = jnp.moveaxis(x, axis, -1)
        lead = x.shape[:-1]
        x = x.reshape(lead + (H_WKV, HS_WKV)).swapaxes(-1, -2).reshape(lead + (W_MIX,))
        return jnp.moveaxis(x, -1, axis)

    def lru_bd(wt):
        return jnp.einsum("ldhij,hg->lhidgj", wt, eye_h).reshape(depth, W_MIX, N_DIR * W_MIX)

    def lora_bd(wt):
        r = wt.shape[2]
        return jnp.einsum("ldrc,de->ldrec", wt, eye_d).reshape(depth, N_DIR * r, N_DIR * W_MIX)

    w_in = prm["w_in"]
    pad = jnp.zeros((depth, D_MODEL, Z_COLS - 5504), F32)
    rkv = [perm(w_in[:, :, 2560 + j * W_MIX:2560 + (j + 1) * W_MIX]) for j in range(3)]
    w_in_perm = jnp.concatenate(
        [w_in[:, :, 0:1536]] + rkv
        + [w_in[:, :, 1536:2560], w_in[:, :, 4480:5504], w_in[:, :, 4096:4480], pad], axis=2).astype(BF16)
    row = lambda x: x.reshape(depth, 1, -1).astype(F32)
    head_of = np.arange(W_MIX) % H_WKV
    w_branch = prm["w_branch"]
    w_branch = jnp.concatenate(
        [w_branch[:, 0], w_branch[:, 1], perm(w_branch[:, 2], axis=1), w_branch[:, 3]], axis=1)
    return {
        "w_in": w_in_perm,
        "w_gate": w_in[:, :, 5504:].astype(BF16),
        "norm1_g": row(prm["norm1_g"]),
        "norm2_g": row(prm["norm2_g"]),
        "conv_a_w": prm["conv_a_w"],
        "conv_b_w": prm["conv_b_w"],
        "conv_b_b": row(prm["conv_b_b"]),
        "lru_w": jnp.concatenate([lru_bd(prm["lru_wa"]), lru_bd(prm["lru_wx"])], axis=2).astype(BF16),
        "lru_b": jnp.concatenate([row(prm["lru_ba"]), row(prm["lru_bx"])], axis=2),
        "lru_lam": row(prm["lru_lambda"]),
        "w0": row(perm(prm["rwkv_w0"])),
        "w2": lora_bd(perm(prm["rwkv_w2"])).astype(BF16),
        "a0": row(perm(prm["rwkv_a0"])),
        "a2": lora_bd(perm(prm["rwkv_a2"])).astype(BF16),
        "g2": perm(prm["rwkv_g2"]).astype(BF16),
        "kk": row(perm(prm["rwkv_kk"])),
        "ka": row(perm(prm["rwkv_ka"])),
        "rk": row(perm(prm["rwkv_rk"].reshape(depth, W_MIX))),
        "lnx_g": row(perm(prm["lnx_g"])),
        "lnx_b": row(perm(prm["lnx_b"])),
        "sg_ln_g": row(prm["sg_ln_g"]),
        "sg_ln_b": row(prm["sg_ln_b"]),
        "sg_ws": prm["sg_ws"].astype(BF16),
        "sg_bst": prm["sg_bs"].transpose(0, 2, 1),
        "gate_b": row(prm["gate_b"]),
        "w_branch": w_branch.astype(BF16),
        "w_out": prm["w_out"].astype(BF16),
        "wq_t": prm["peer_wq"].transpose(0, 2, 1).astype(BF16),
        "keys": prm["peer_keys"].astype(BF16),
        "ind": jnp.asarray(head_of[:, None] == head_of[None, :], BF16)[None],
    }


def kernel(x_prompt, x_sample, state_lru, state_wkv, c, c_ctx, norm1_g, norm2_g, w_mod, b_mod, w_in, conv_a_w, conv_b_w, conv_b_b, lru_wa, lru_ba, lru_wx, lru_bx, lru_lambda, rwkv_w0, rwkv_w2, rwkv_a0, rwkv_a2, rwkv_g2, rwkv_kk, rwkv_ka, rwkv_rk, lnx_g, lnx_b, sg_ln_g, sg_ln_b, sg_ws, sg_bs, gate_b, w_branch, w_out, peer_wq, peer_keys, peer_u, peer_v, final_norm_g):
    prm = dict(norm1_g=norm1_g, norm2_g=norm2_g, w_in=w_in, conv_a_w=conv_a_w, conv_b_w=conv_b_w,
               conv_b_b=conv_b_b, lru_wa=lru_wa, lru_ba=lru_ba, lru_wx=lru_wx, lru_bx=lru_bx,
               lru_lambda=lru_lambda, rwkv_w0=rwkv_w0, rwkv_w2=rwkv_w2, rwkv_a0=rwkv_a0, rwkv_a2=rwkv_a2,
               rwkv_g2=rwkv_g2, rwkv_kk=rwkv_kk, rwkv_ka=rwkv_ka, rwkv_rk=rwkv_rk, lnx_g=lnx_g,
               lnx_b=lnx_b, sg_ln_g=sg_ln_g, sg_ln_b=sg_ln_b, sg_ws=sg_ws, sg_bs=sg_bs, gate_b=gate_b,
               w_branch=w_branch, w_out=w_out, peer_wq=peer_wq, peer_keys=peer_keys, peer_u=peer_u,
               peer_v=peer_v)
    bc, lc, _ = x_prompt.shape
    bl, ll, _ = x_sample.shape
    depth = w_mod.shape[0]
    n_ctx_tok = bc * lc
    n_ctx_tiles = n_ctx_tok // TM
    lat_tiles = ll // TM
    ctx_spb = min(WKV_CTX_SPB, bc)
    lru_spb = ll // lc
    assert lc == TM and ll % TM_PEER == 0 and n_ctx_tok % TM_PEER == 0 and bl + 1 <= SUBLANES
    assert ll % GRID_W == 0 and TM % GRID_W == 0 and bc % ctx_spb == 0 and n_ctx_tok % ll == 0
    assert LANES % (ctx_spb * H_WKV) == 0 and LANES % (bl * H_WKV) == 0

    cond = jnp.zeros((SUBLANES, D_MODEL), F32).at[0].set(c_ctx).at[1:1 + bl].set(c)
    mods = _modulation(cond, w_mod, b_mod).reshape(depth, SUBLANES, N_MOD, D_MODEL)
    fng = final_norm_g.reshape(1, D_MODEL)
    x_ctx = x_prompt.reshape(n_ctx_tok, D_MODEL)
    x_lat = x_sample.reshape(bl * ll, D_MODEL)
    ctx_vs = LANES // (ctx_spb * H_WKV)
    lat_vs = LANES // (bl * H_WKV)
    n_sb = bc // ctx_spb
    wkv_zero = jnp.zeros(_wkv_state_shape(N_DIR * n_sb, ctx_vs), F32)
    u_all = peer_u.astype(BF16)
    v_all = peer_v.astype(BF16)
    new_lru, new_wkv = [], []
    pnames = ["ya", "yd", "gbg", "la", "lu", "g", "bon", "rt", "vt", "kkt", "wt", "kt", "bt"]
    lw = _stacked_weights(prm)
    wkv_lat0 = _wkv_state_in(state_wkv.astype(F32), lat_vs)
    lru_h0 = jnp.zeros((depth, n_ctx_tok // ll + bl, lru_spb, N_DIR * W_MIX), F32).at[:, n_ctx_tok // ll:, 0].set(
        state_lru.astype(F32).reshape(bl, depth, N_DIR * W_MIX).swapaxes(0, 1))
    for i in range(depth):
        mod = mods[i]
        p = dict(zip(pnames, _prep(x_ctx, x_lat, mod, lw, i, ll)))

        h, lru_s = _lru_scan(p["la"], p["lu"], lru_h0[i],
                             rows=ll, n_ctx_blocks=n_ctx_tok // ll, ctx_cfg=(lru_spb, lc), lat_cfg=(1, ll))
        new_lru.append(lru_s[:n_ctx_tok // ll].reshape(bc, N_DIR, W_MIX))

        wkv_in = [p[n] for n in ("rt", "wt", "kt", "kkt", "bt", "vt")]
        y_c, s_c = _wkv_scan(*wkv_in, wkv_zero, tile0=0, n_seq=bc, seq_tiles=1, spb=ctx_spb)
        y_l, _ = _wkv_scan(*wkv_in, wkv_lat0[i], tile0=n_ctx_tiles, n_seq=bl, seq_tiles=lat_tiles, spb=1)
        new_wkv.append(s_c)

        tok_in = [p["ya"], p["yd"], p["gbg"], h, p["g"], p["bon"]]
        x1, h2 = _merge(x_ctx, x_lat, mod, lw, i, tok_in, y_c, y_l, ll)
        x_ctx, x_lat = _peer(h2, lw["wq_t"], lw["keys"], u_all, v_all, i, x1, mod, fng,
                             n_ctx_tok, ll, final=(i == depth - 1))
    y_prompt = x_ctx.reshape(bc, lc, D_MODEL)
    y_sample = x_lat.reshape(bl, ll, D_MODEL)
    return (y_prompt, y_sample, jnp.stack(new_lru, axis=1), _wkv_state_out(new_wkv, n_sb, ctx_spb, ctx_vs))
```

```python
import functools

import numpy as np
import jax
import jax.numpy as jnp
from jax import lax
from jax.experimental import pallas as pl
from jax.experimental.pallas import tpu as pltpu

F32 = jnp.float32
BF16 = jnp.bfloat16
I32 = jnp.int32

D_MODEL = 1024
W_MIX = 512
N_DIR = 2
H_WKV = 8
HS_WKV = 64
H_LRU = 8
LORA_W = 64
LORA_A = 64
LORA_G = 128
GRID_W = 64
CHUNK = 128
G_SG = 4
N_KEYS = 128
PEER_HEADS = 8
PEER_TOPK = 16
N_MOD = 6
EPS = 1e-6
LNX_EPS = 64e-5
LRU_C = 8.0

LANES = 128
SUBLANES = 8
TM = 256
TM_PEER = 512
PEER_EB = 1024
PEER_SUB = 512
Z_COLS = 5632
WKV_TC = LANES
WKV_CTX_SPB = 8
VMEM_LIMIT = 56 * 1024 * 1024

_NT = (((1,), (1,)), ((), ()))


def _cparams(sem, vmem=None):
    return pltpu.CompilerParams(dimension_semantics=sem, vmem_limit_bytes=vmem)


def _const_spec(shape):
    nd = len(shape)
    return pl.BlockSpec(shape, lambda *_: (0,) * nd)


def _layer_spec(arr, layer, **kw):
    nd = arr.ndim
    idx = layer if arr.shape[0] > 1 else 0
    return pl.BlockSpec((None,) + arr.shape[1:], lambda *_: (idx,) + (0,) * (nd - 1), **kw)


def _softplus(x):
    return jnp.maximum(x, 0.0) + jnp.log1p(jnp.exp(-jnp.abs(x)))


def _rms(x, g):
    return x * lax.rsqrt(jnp.mean(x * x, axis=-1, keepdims=True) + EPS) * g


def _segsum(x, ind):
    hi = x.astype(BF16)
    lo = (x - hi.astype(F32)).astype(BF16)
    return (jnp.dot(hi, ind, preferred_element_type=F32)
            + jnp.dot(lo, ind, preferred_element_type=F32))


def _mod_index(i, tm, n_ctx_tok, lat_len):
    n_ctx_tiles = n_ctx_tok // tm
    tiles_per_seq = lat_len // tm
    return jnp.where(i < n_ctx_tiles, 0, 1 + lax.div(i - n_ctx_tiles, tiles_per_seq))


def _mod_body(s_ref, w_ref, b_ref, o_ref):
    s = s_ref[...]
    s = s * jax.nn.sigmoid(s)
    o_ref[0] = jnp.dot(s.astype(BF16), w_ref[0].astype(BF16), preferred_element_type=F32) + b_ref[0]


def _modulation(cond, w_mod, b_mod):
    depth = w_mod.shape[0]
    n = w_mod.shape[2]
    tn = 1536
    return pl.pallas_call(
        _mod_body,
        grid=(depth, n // tn),
        in_specs=[_const_spec((SUBLANES, D_MODEL)),
                  pl.BlockSpec((1, D_MODEL, tn), lambda l, j: (l, 0, j)),
                  pl.BlockSpec((1, 1, tn), lambda l, j: (l, 0, j))],
        out_specs=pl.BlockSpec((1, SUBLANES, tn), lambda l, j: (l, 0, j)),
        out_shape=jax.ShapeDtypeStruct((depth, SUBLANES, n), F32),
        compiler_params=_cparams(("parallel", "parallel"), VMEM_LIMIT),
        name="modulation",
    )(cond, w_mod, b_mod.reshape(depth, 1, n))


def _prep_body(xc_ref, xl_ref, xp_ref, xn_ref, mod_ref, n1g_ref, win_ref,
               caw_ref, cbw_ref, cbb_ref, lruw_ref, lrub_ref, lam_ref,
               w0_ref, w2_ref, a0_ref, a2_ref, g2_ref, kkw_ref, ka_ref, rk_ref,
               lng_ref, lnb_ref, ws_ref, bst_ref, ind_ref,
               ya_ref, yd_ref, gbg_ref, la_ref, lu_ref, g_ref, bon_ref,
               rt_ref, vt_ref, kkt_ref, wt_ref, kt_ref, bt_ref,
               *, n_ctx_tiles, tiles_per_seq):
    i = pl.program_id(0)
    is_ctx = i < n_ctx_tiles
    t = lax.broadcasted_iota(I32, (TM, 1), 0)
    ind = ind_ref[...]
    m = mod_ref[0]

    def modulated(xv):
        return (_rms(xv, n1g_ref[...]) * (1.0 + m[1:2, :]) + m[0:1, :]).astype(BF16)

    def project(hv, lo, hi):
        return jnp.dot(hv, win_ref[:, lo:hi], preferred_element_type=F32)

    h = modulated(jnp.where(is_ctx, xc_ref[...], xl_ref[...]))
    za = project(h, 0, 1536)
    zc = project(h, 1536, 3072)
    zb = project(h, 3072, 4096)
    zd = project(h, 4096, 5120)
    zl = project(h, 5120, 5120 + 2 * LORA_W + 2 * LORA_A + LORA_G)
    halo_prev = project(modulated(xp_ref[...]), 3072 + W_MIX, 4096)
    halo_next = project(modulated(xn_ref[...]), 3072 + W_MIX, 4096)

    pm = jnp.where(is_ctx, TM - 1, GRID_W - 1)
    pos = t & pm
    a_b = za[:, 0:W_MIX]
    ac = za[:, W_MIX:2 * W_MIX] * za[:, 2 * W_MIX:3 * W_MIX]
    up = jnp.where(pos == 0, 0.0, pltpu.roll(ac, 1, 0))
    dn = jnp.where(pos == pm, 0.0, pltpu.roll(ac, TM - 1, 0))
    ya_ref[...] = a_b * (caw_ref[0:1, :] * up + caw_ref[1:2, :] * ac + caw_ref[2:3, :] * dn)

    seq_tile = lax.rem(jnp.maximum(i - n_ctx_tiles, 0), tiles_per_seq)
    first = jnp.logical_or(is_ctx, seq_tile == 0)
    last = jnp.logical_or(is_ctx, seq_tile == tiles_per_seq - 1)
    prev = jnp.where(first, 0.0, halo_prev[SUBLANES - 1:SUBLANES, :])
    nxt0 = jnp.where(last, 0.0, halo_next[0:1, :])
    nxt1 = jnp.where(last, 0.0, halo_next[1:2, :])
    bx = zb[:, W_MIX:2 * W_MIX]
    m1 = jnp.where(t == 0, prev, pltpu.roll(bx, 1, 0))
    p1 = jnp.where(t == TM - 1, nxt0, pltpu.roll(bx, TM - 1, 0))
    p2 = jnp.where(t == TM - 2, nxt0, jnp.where(t == TM - 1, nxt1, pltpu.roll(bx, TM - 2, 0)))
    xb = (cbw_ref[0:1, :] * m1 + cbw_ref[1:2, :] * bx + cbw_ref[2:3, :] * p1
          + cbw_ref[3:4, :] * p2 + cbb_ref[...])
    gates = jnp.dot(xb.astype(BF16), lruw_ref[...], preferred_element_type=F32) + lrub_ref[...]
    rg = jax.nn.sigmoid(gates[:, 0:2 * W_MIX])
    ig = jax.nn.sigmoid(gates[:, 2 * W_MIX:4 * W_MIX])
    log_a = -LRU_C * rg * _softplus(-lam_ref[...])
    xb2 = jnp.concatenate([xb, xb], axis=1)
    a = jnp.exp(log_a)
    la_ref[...] = a
    lu_ref[...] = jnp.sqrt(jnp.tanh(-log_a) * (a * a + 1.0)) * (ig * xb2)
    gbg_ref[...] = jax.nn.gelu(zb[:, 0:W_MIX])

    zr = zc[:, 0:W_MIX]
    zk = zc[:, W_MIX:2 * W_MIX]
    zv = zc[:, 2 * W_MIX:3 * W_MIX]
    zwd = zl[:, 0:2 * LORA_W]
    zad = zl[:, 2 * LORA_W:2 * LORA_W + 2 * LORA_A]
    zgd = zl[:, 2 * LORA_W + 2 * LORA_A:2 * LORA_W + 2 * LORA_A + LORA_G]
    wlin = w0_ref[...] + jnp.dot(jnp.tanh(zwd).astype(BF16), w2_ref[...], preferred_element_type=F32)
    wt_ref[0] = jnp.exp(-jnp.exp(-_softplus(-wlin) - 0.5)).T
    av = jax.nn.sigmoid(a0_ref[...] + jnp.dot(zad.astype(BF16), a2_ref[...], preferred_element_type=F32))
    g_ref[...] = jnp.dot(jax.nn.sigmoid(zgd).astype(BF16), g2_ref[...], preferred_element_type=F32)
    kkr = zk * kkw_ref[...]
    kkn = kkr / jnp.maximum(jnp.sqrt(_segsum(kkr * kkr, ind)), 1e-12)
    zk2 = jnp.concatenate([zk, zk], axis=1)
    ka2 = jnp.concatenate([ka_ref[...], ka_ref[...]], axis=1)
    kd = zk2 * (1.0 + (av - 1.0) * ka2)
    kt_ref[0] = kd.T
    bt_ref[0] = (jnp.concatenate([kkn, kkn], axis=1) * av).T
    rt_ref[0] = zr.T
    vt_ref[0] = zv.T
    kkt_ref[0] = kkn.T
    bon_ref[...] = _segsum(zr * (kd[:, 0:W_MIX] + kd[:, W_MIX:2 * W_MIX]) * rk_ref[...], ind) * zv

    zg = jax.nn.gelu(zd)
    u = zg[:, 0:W_MIX]
    vv = zg[:, W_MIX:2 * W_MIX]
    vc = vv - jnp.mean(vv, axis=-1, keepdims=True)
    vn = vc * lax.rsqrt(jnp.mean(vc * vc, axis=-1, keepdims=True) + 1e-5) * lng_ref[...] + lnb_ref[...]
    for c in range(TM // CHUNK):
        rs = slice(c * CHUNK, (c + 1) * CHUNK)
        for gi in range(G_SG):
            cs = slice(gi * LANES, (gi + 1) * LANES)
            s = jnp.dot(ws_ref[gi], vn[rs, cs].astype(BF16), preferred_element_type=F32)
            yd_ref[rs, cs] = u[rs, cs] * (s + bst_ref[:, gi:gi + 1])


def _dual_specs(rows, n_ctx_blocks, **kw):
    return [pl.BlockSpec((rows, D_MODEL), lambda i, *_: (jnp.minimum(i, n_ctx_blocks - 1), 0), **kw),
            pl.BlockSpec((rows, D_MODEL), lambda i, *_: (jnp.maximum(i - n_ctx_blocks, 0), 0), **kw)]


def _prep(x_ctx, x_lat, mod, lw, layer, lat_len):
    n_ctx_tok = x_ctx.shape[0]
    t = n_ctx_tok + x_lat.shape[0]
    n_tiles = t // TM
    n_ctx_tiles = n_ctx_tok // TM
    tiles_per_seq = lat_len // TM
    rows8 = TM // SUBLANES
    last_blk = x_lat.shape[0] // SUBLANES - 1
    midx = functools.partial(_mod_index, tm=TM, n_ctx_tok=n_ctx_tok, lat_len=lat_len)

    def lat_blk8(i, off):
        return (jnp.clip((i - n_ctx_tiles) * rows8 + off, 0, last_blk), 0)

    x_specs = _dual_specs(TM, n_ctx_tiles) + [
        pl.BlockSpec((SUBLANES, D_MODEL), lambda i: lat_blk8(i, -1)),
        pl.BlockSpec((SUBLANES, D_MODEL), lambda i: lat_blk8(i, rows8)),
        pl.BlockSpec((1, N_MOD, D_MODEL), lambda i: (midx(i), 0, 0)),
        _layer_spec(lw["norm1_g"], layer),
        _layer_spec(lw["w_in"], layer, pipeline_mode=pl.Buffered(1)),
    ]
    wnames = ["conv_a_w", "conv_b_w", "conv_b_b", "lru_w", "lru_b", "lru_lam", "w0", "w2", "a0", "a2",
              "g2", "kk", "ka", "rk", "sg_ln_g", "sg_ln_b", "sg_ws", "sg_bst", "ind"]
    wts = [lw[n] for n in wnames]
    w_specs = [_layer_spec(w, layer) for w in wts]
    widths = [W_MIX, W_MIX, W_MIX, 2 * W_MIX, 2 * W_MIX, W_MIX, W_MIX]
    t_rows = [W_MIX, W_MIX, W_MIX, 2 * W_MIX, 2 * W_MIX, 2 * W_MIX]
    out_specs = ([pl.BlockSpec((TM, wd), lambda i: (i, 0)) for wd in widths]
                 + [pl.BlockSpec((1, r, TM), lambda i: (i, 0, 0)) for r in t_rows])
    out_shape = ([jax.ShapeDtypeStruct((t, wd), F32) for wd in widths]
                 + [jax.ShapeDtypeStruct((n_tiles, r, TM), F32) for r in t_rows])
    return pl.pallas_call(
        functools.partial(_prep_body, n_ctx_tiles=n_ctx_tiles, tiles_per_seq=tiles_per_seq),
        grid=(n_tiles,),
        in_specs=x_specs + w_specs,
        out_specs=out_specs,
        out_shape=out_shape,
        compiler_params=_cparams(("parallel",), VMEM_LIMIT),
        name="branch_prep",
    )(x_ctx, x_lat, x_lat, x_lat, mod, lw["norm1_g"], lw["w_in"], *wts)


def _lru_body(a_ref, u_ref, h0_ref, h_ref, hf_ref, *, n_ctx_blocks, ctx_cfg, lat_cfg):
    fw, bw = slice(0, W_MIX), slice(W_MIX, 2 * W_MIX)

    def scan(nseq, l):
        def step(s, carry):
            out = []
            for j in range(nseq):
                tf = j * l + s
                tb = j * l + (l - 1 - s)
                hf = a_ref[pl.ds(tf, 1), fw] * carry[2 * j] + u_ref[pl.ds(tf, 1), fw]
                hb = a_ref[pl.ds(tb, 1), bw] * carry[2 * j + 1] + u_ref[pl.ds(tb, 1), bw]
                h_ref[pl.ds(tf, 1), fw] = hf
                h_ref[pl.ds(tb, 1), bw] = hb
                out += [hf, hb]
            return tuple(out)

        init = []
        for j in range(nseq):
            init += [h0_ref[0, j:j + 1, fw], h0_ref[0, j:j + 1, bw]]
        fin = lax.fori_loop(0, l, step, tuple(init), unroll=2)
        hf_ref[0] = h0_ref[0]
        for j in range(nseq):
            hf_ref[0, j:j + 1, fw] = fin[2 * j]
            hf_ref[0, j:j + 1, bw] = fin[2 * j + 1]

    is_ctx = pl.program_id(0) < n_ctx_blocks
    pl.when(is_ctx)(lambda: scan(*ctx_cfg))
    pl.when(jnp.logical_not(is_ctx))(lambda: scan(*lat_cfg))


def _lru_scan(a, u, h0, *, rows, n_ctx_blocks, ctx_cfg, lat_cfg):
    nb = a.shape[0] // rows
    w = a.shape[1]
    tok = pl.BlockSpec((rows, w), lambda i: (i, 0))
    st = pl.BlockSpec((1,) + h0.shape[1:], lambda i: (i, 0, 0))
    return pl.pallas_call(
        functools.partial(_lru_body, n_ctx_blocks=n_ctx_blocks, ctx_cfg=ctx_cfg, lat_cfg=lat_cfg),
        grid=(nb,),
        in_specs=[tok, tok, st],
        out_specs=[tok, st],
        out_shape=[jax.ShapeDtypeStruct(a.shape, F32), jax.ShapeDtypeStruct(h0.shape, F32)],
        compiler_params=_cparams(("parallel",), VMEM_LIMIT),
        name="lru_scan",
    )(a, u, h0)


_SLOT_ORDER = (0, 4, 2, 6, 1, 5, 3, 7)


def _rowsum8(parts):
    sub = lax.broadcasted_iota(I32, (SUBLANES, LANES), 0)
    slots = [parts[i] for i in _SLOT_ORDER]
    roll = pltpu.roll
    lvl1 = [jnp.where(sub < 4, a + roll(a, 4, 0), b + roll(b, 4, 0))
            for a, b in zip(slots[0::2], slots[1::2])]
    lvl2 = [jnp.where((sub & 3) < 2, a + roll(a, 6, 0), roll(b + roll(b, 6, 0), 2, 0))
            for a, b in zip(lvl1[0::2], lvl1[1::2])]
    a, b = lvl2
    return jnp.where((sub & 1) == 0, a + roll(a, 7, 0), roll(b + roll(b, 7, 0), 1, 0))


def _fold8(x):
    return jnp.sum(x.reshape(HS_WKV // SUBLANES, SUBLANES, LANES), axis=0)


def _value_rows_on_sublanes(vl_n):
    return vl_n // SUBLANES < 4


def _wkv_body(*refs, nsrc, spb, vs, n_sb, tc, kp, vp):
    vl_n = HS_WKV // vs
    n_in = 6 * nsrc
    k_srcs = [refs[o * nsrc:(o + 1) * nsrc] for o in range(5)]
    v_srcs = refs[5 * nsrc:n_in]
    s0_ref = refs[n_in]
    y_ref = refs[n_in + 1]
    sf_ref = refs[n_in + 2]
    k_scr = refs[n_in + 3:n_in + 8]
    v_scr, y_scr, s_scr = refs[n_in + 8:n_in + 11]
    rows_on_sublanes = _value_rows_on_sublanes(HS_WKV // vs)
    sa_scr = None if rows_on_sublanes else refs[n_in + 11]
    r_scr, w_scr, k_scr_, kk_scr, b_scr = k_scr
    backward = pl.program_id(0) // n_sb == 1
    seqs = [(s, j) for s in range(nsrc) for j in range(spb)]

    @pl.when(pl.program_id(1) == 0)
    def _():
        s_scr[...] = s0_ref[0]

    def build_k(c, carry):
        row = pl.multiple_of(c * H_WKV, H_WKV)
        for o in range(5):
            slab = [k_srcs[o][s][j, pl.ds(row, H_WKV), :] for s, j in seqs]
            k_scr[o][pl.ds(c, tc, stride=kp), :] = jnp.concatenate(slab * vs, axis=0).T
        return carry

    lax.fori_loop(0, HS_WKV, build_k, 0, unroll=4)

    def build_v(vl, carry):
        slab = []
        for vsi in range(vs):
            row = pl.multiple_of((vsi * vl_n + vl) * H_WKV, H_WKV)
            slab += [v_srcs[s][j, pl.ds(row, H_WKV), :] for s, j in seqs]
        v_scr[pl.ds(vl, tc, stride=vp), :] = jnp.concatenate(slab, axis=0).T
        return carry

    lax.fori_loop(0, vl_n, build_v, 0, unroll=4)

    def step_keys_on_sublanes(s, carry):
        t = jnp.where(backward, tc - 1 - s, s)
        krow = pl.multiple_of(t * kp, SUBLANES)
        vrow = pl.multiple_of(t * vp, SUBLANES)
        kslab = pl.ds(krow, HS_WKV)
        for g in range(vl_n // SUBLANES):
            parts = [_fold8(s_scr[g * SUBLANES + i] * kk_scr[kslab, :]) for i in range(SUBLANES)]
            sa_scr[g * SUBLANES:(g + 1) * SUBLANES, :] = _rowsum8(parts)
        for g in range(vl_n // SUBLANES):
            parts = []
            for i in range(SUBLANES):
                vl = g * SUBLANES + i
                sa = sa_scr[vl:vl + 1, :]
                vv = v_scr[pl.ds(vrow + vl, 1), :]
                sn = s_scr[vl] * w_scr[kslab, :] - sa * b_scr[kslab, :] + vv * k_scr_[kslab, :]
                s_scr[vl] = sn
                parts.append(_fold8(sn * r_scr[kslab, :]))
            y_scr[pl.ds(pl.multiple_of(vrow + g * SUBLANES, SUBLANES), SUBLANES), :] = _rowsum8(parts)
        return carry

    def step_rows_on_sublanes(s, carry):
        t = jnp.where(backward, tc - 1 - s, s)
        krow = pl.multiple_of(t * kp, SUBLANES)
        vrow = pl.multiple_of(t * vp, SUBLANES)
        n_g = vl_n // SUBLANES
        batch = min(n_g, 2)
        n_acc = 4 // batch

        def row(ref, k):
            return jnp.broadcast_to(ref[pl.ds(krow + k, 1), :], (SUBLANES, LANES))

        def total(parts):
            while len(parts) > 1:
                parts = [a + b for a, b in zip(parts[0::2], parts[1::2])]
            return parts[0]

        def accumulate(acc, g, k, p):
            acc[g][k % n_acc] = p if acc[g][k % n_acc] is None else acc[g][k % n_acc] + p

        for g0 in range(0, n_g, batch):
            gs = range(g0, g0 + batch)
            acc = {g: [None] * n_acc for g in gs}
            for k in range(HS_WKV):
                kk = row(kk_scr, k)
                for g in gs:
                    accumulate(acc, g, k, s_scr[g, k] * kk)
            sa = {g: total(acc[g]) for g in gs}
            vv = {g: v_scr[pl.ds(pl.multiple_of(vrow + g * SUBLANES, SUBLANES), SUBLANES), :] for g in gs}
            acc = {g: [None] * n_acc for g in gs}
            for k in range(HS_WKV):
                w, b, kx, r = row(w_scr, k), row(b_scr, k), row(k_scr_, k), row(r_scr, k)
                for g in gs:
                    sn = s_scr[g, k] * w - sa[g] * b + vv[g] * kx
                    s_scr[g, k] = sn
                    accumulate(acc, g, k, sn * r)
            for g in gs:
                y_scr[pl.ds(pl.multiple_of(vrow + g * SUBLANES, SUBLANES), SUBLANES), :] = total(acc[g])
        return carry

    lax.fori_loop(0, tc, step_rows_on_sublanes if rows_on_sublanes else step_keys_on_sublanes, 0)

    def emit_y(vl, carry):
        yt = y_scr[pl.ds(vl, tc, stride=vp), :].T
        for vsi in range(vs):
            row = pl.multiple_of((vsi * vl_n + vl) * H_WKV, H_WKV)
            for n, (s, j) in enumerate(seqs):
                lane0 = (vsi * len(seqs) + n) * H_WKV
                y_ref[0, 0, s * spb + j, pl.ds(row, H_WKV), :] = yt[lane0:lane0 + H_WKV, :]
        return carry

    lax.fori_loop(0, vl_n, emit_y, 0, unroll=4)
    sf_ref[0] = s_scr[...]


def _wkv_scan(rt, wt, kt, kkt, bt, vt, s0, *, tile0, n_seq, seq_tiles, spb):
    tc = WKV_TC
    if spb > 1:
        assert seq_tiles == 1 and n_seq % spb == 0 and tile0 % spb == 0
        nsrc, n_sb = 1, n_seq // spb
    else:
        nsrc, n_sb = n_seq, 1
    inst = nsrc * spb * H_WKV
    vs = LANES // inst
    vl_n = HS_WKV // vs
    assert vl_n % SUBLANES == 0, "value rows are processed eight at a time"
    cpt = TM // tc
    n_chunks = seq_tiles * cpt
    kp = HS_WKV + SUBLANES
    vp = vl_n + SUBLANES if ((vl_n + SUBLANES) // SUBLANES) % 2 else vl_n + 2 * SUBLANES

    def chunk(g, i):
        return jnp.where(g // n_sb == 1, n_chunks - 1 - i, i)

    def in_map(g, i, *, src, per_dir):
        ce = chunk(g, i)
        rb = (g // n_sb) if per_dir else 0
        if spb > 1:
            return (tile0 // spb + g % n_sb, rb, ce)
        return (tile0 + src * seq_tiles + ce // cpt, rb, ce % cpt)

    def out_map(g, i):
        ce = chunk(g, i)
        if spb > 1:
            return (g // n_sb, 0, g % n_sb, 0, ce)
        return (g // n_sb, ce // cpt, 0, 0, ce % cpt)

    in_specs, operands = [], []
    for arr, per_dir in ((rt, False), (wt, True), (kt, True), (kkt, False), (bt, True), (vt, False)):
        for src in range(nsrc):
            in_specs.append(pl.BlockSpec((spb, W_MIX, tc), functools.partial(in_map, src=src, per_dir=per_dir),
                                         pipeline_mode=pl.Buffered(1)))
            operands.append(arr)
    state_block = (1,) + _wkv_state_shape(1, vs)[1:]
    sspec = pl.BlockSpec(state_block, lambda g, i: (g,) + (0,) * (len(state_block) - 1))
    in_specs.append(sspec)
    out_specs = [pl.BlockSpec((1, 1, nsrc * spb, W_MIX, tc), out_map)]
    out_shape = [jax.ShapeDtypeStruct((N_DIR, seq_tiles, n_seq, W_MIX, TM), F32)]
    res = pl.pallas_call(
        functools.partial(_wkv_body, nsrc=nsrc, spb=spb, vs=vs, n_sb=n_sb, tc=tc, kp=kp, vp=vp),
        grid=(N_DIR * n_sb, n_chunks),
        in_specs=in_specs,
        out_specs=out_specs + [sspec],
        out_shape=out_shape + [jax.ShapeDtypeStruct(s0.shape, F32)],
        scratch_shapes=([pltpu.VMEM((tc * kp, LANES), F32)] * 5
                        + [pltpu.VMEM((tc * vp, LANES), F32)] * 2
                        + [pltpu.VMEM(state_block[1:], F32)]
                        + ([] if _value_rows_on_sublanes(vl_n) else [pltpu.VMEM((vl_n, LANES), F32)])),
        compiler_params=_cparams(("parallel", "arbitrary"), VMEM_LIMIT),
        name="wkv_scan",
    )(*operands, s0)
    return res[0], res[1]


def _merge_body(xc_ref, xl_ref, mod_ref, n1g_ref, n2g_ref, wg_ref, gb_ref, wbr_ref, wo_ref,
                lnxg_ref, lnxb_ref, ind_ref,
                ya_ref, yd_ref, gbg_ref, h_ref, ycf_ref, ycb_ref, ylf_ref, ylb_ref, g_ref, bon_ref,
                x1_ref, h2_ref, *, n_ctx_tiles):
    is_ctx = pl.program_id(0) < n_ctx_tiles
    x = jnp.where(is_ctx, xc_ref[...], xl_ref[...])
    m = mod_ref[0]
    ind = ind_ref[...]
    h = (_rms(x, n1g_ref[...]) * (1.0 + m[1:2, :]) + m[0:1, :]).astype(BF16)
    y_b = gbg_ref[...] * (h_ref[:, 0:W_MIX] + h_ref[:, W_MIX:2 * W_MIX])
    y = jnp.where(is_ctx, ycf_ref[0, 0, 0] + ycb_ref[0, 0, 0], ylf_ref[0, 0, 0] + ylb_ref[0, 0, 0]).T
    yc = y - _segsum(y, ind) * (1.0 / HS_WKV)
    var = _segsum(yc * yc, ind) * (1.0 / HS_WKV)
    y_c = (yc * lax.rsqrt(var + LNX_EPS) * lnxg_ref[...] + lnxb_ref[...] + bon_ref[...]) * g_ref[...]
    merged = None
    for n, yn in enumerate((ya_ref[...], y_b, y_c, yd_ref[...])):
        cs = slice(n * D_MODEL, (n + 1) * D_MODEL)
        gate = jax.nn.sigmoid(jnp.dot(h, wg_ref[:, cs], preferred_element_type=F32) + gb_ref[:, cs])
        br = jnp.dot(yn.astype(BF16), wbr_ref[n * W_MIX:(n + 1) * W_MIX, :], preferred_element_type=F32)
        merged = gate * br if merged is None else merged + gate * br
    mo = jnp.dot(merged.astype(BF16), wo_ref[...], preferred_element_type=F32)
    x1 = x + m[2:3, :] * mo
    x1_ref[...] = x1
    h2_ref[...] = (_rms(x1, n2g_ref[...]) * (1.0 + m[4:5, :]) + m[3:4, :]).astype(BF16)


def _merge(x_ctx, x_lat, mod, lw, layer, tok_in, y_ctx, y_lat, lat_len):
    n_ctx_tok = x_ctx.shape[0]
    t = n_ctx_tok + x_lat.shape[0]
    n_ctx_tiles = n_ctx_tok // TM
    tps = lat_len // TM
    midx = functools.partial(_mod_index, tm=TM, n_ctx_tok=n_ctx_tok, lat_len=lat_len)
    wnames = ["norm1_g", "norm2_g", "w_gate", "gate_b", "w_branch", "w_out", "lnx_g", "lnx_b", "ind"]
    wts = [lw[n] for n in wnames]
    tok = lambda wd: pl.BlockSpec((TM, wd), lambda i: (i, 0))
    ya, yd, gbg, h, g, bon = tok_in
    yblock = (1, 1, 1, W_MIX, TM)

    def ctx_spec(d):
        return pl.BlockSpec(yblock, lambda i: (d, 0, jnp.minimum(i, n_ctx_tiles - 1), 0, 0))

    def lat_spec(d):
        def imap(i):
            r = jnp.maximum(i - n_ctx_tiles, 0)
            return (d, lax.rem(r, tps), lax.div(r, tps), 0, 0)
        return pl.BlockSpec(yblock, imap)

    return pl.pallas_call(
        functools.partial(_merge_body, n_ctx_tiles=n_ctx_tiles),
        grid=(t // TM,),
        in_specs=(_dual_specs(TM, n_ctx_tiles)
                  + [pl.BlockSpec((1, N_MOD, D_MODEL), lambda i: (midx(i), 0, 0))]
                  + [_layer_spec(w, layer) for w in wts]
                  + [tok(W_MIX), tok(W_MIX), tok(W_MIX), tok(2 * W_MIX),
                     ctx_spec(0), ctx_spec(1), lat_spec(0), lat_spec(1), tok(W_MIX), tok(W_MIX)]),
        out_specs=[tok(D_MODEL), tok(D_MODEL)],
        out_shape=[jax.ShapeDtypeStruct((t, D_MODEL), F32), jax.ShapeDtypeStruct((t, D_MODEL), BF16)],
        compiler_params=_cparams(("parallel",), VMEM_LIMIT),
        name="merge",
    )(x_ctx, x_lat, mod, *wts, ya, yd, gbg, h, y_ctx, y_ctx, y_lat, y_lat, g, bon)


_CAND_VALID = (8, 8, 8, 5, 4, 3, 2, 2, 2, 8)


def _oddeven_pairs(n):
    def merge(lo, hi, r):
        step = r * 2
        if step < hi - lo:
            yield from merge(lo, hi, step)
            yield from merge(lo + r, hi, step)
            yield from [(i, i + r) for i in range(lo + r, hi - r, step)]
        else:
            yield (lo, lo + r)

    def sort(lo, hi):
        if hi - lo >= 1:
            mid = lo + (hi - lo) // 2
            yield from sort(lo, mid)
            yield from sort(mid + 1, hi)
            yield from merge(lo, hi, 1)

    return tuple(sort(0, n - 1))


_SORT16 = _oddeven_pairs(N_KEYS // SUBLANES)


def _route_head(qs, keys_ref):
    sub = lax.broadcasted_iota(I32, (SUBLANES, LANES), 0)
    kid = lax.broadcasted_iota(I32, (PEER_TOPK, LANES), 0)
    neg = -jnp.inf

    def bc(x, r):
        return jnp.broadcast_to(x[r:r + 1, :], (SUBLANES, LANES))

    def head():
        tops = []
        for p in range(2):
            s = jnp.dot(keys_ref[p], qs[p], preferred_element_type=F32)
            cols = [s[j * SUBLANES:(j + 1) * SUBLANES, :] for j in range(N_KEYS // SUBLANES)]
            cidx = [sub + j * SUBLANES for j in range(N_KEYS // SUBLANES)]
            for a, b in _SORT16:
                take = cols[b] > cols[a]
                cols[a], cols[b] = jnp.where(take, cols[b], cols[a]), jnp.where(take, cols[a], cols[b])
                cidx[a], cidx[b] = jnp.where(take, cidx[b], cidx[a]), jnp.where(take, cidx[a], cidx[b])
            vals = jnp.zeros((PEER_TOPK, LANES), F32)
            idxs = jnp.zeros((PEER_TOPK, LANES), I32)
            for r in range(PEER_TOPK):
                v8, i8 = cols[0], cidx[0]
                for sh in (4, 2, 1):
                    vr, ir = pltpu.roll(v8, sh, 0), pltpu.roll(i8, sh, 0)
                    take = vr > v8
                    v8, i8 = jnp.where(take, vr, v8), jnp.where(take, ir, i8)
                m, ix = v8[0:1, :], i8[0:1, :]
                vals = jnp.where(kid == r, m, vals)
                idxs = jnp.where(kid == r, ix, idxs)
                popped = cidx[0] == ix
                for j in range(PEER_TOPK - 1 - r):
                    cols[j] = jnp.where(popped, cols[j + 1], cols[j])
                    cidx[j] = jnp.where(popped, cidx[j + 1], cidx[j])
            tops.append((vals, idxs))
        (a0, i0), (a1, i1) = tops
        lo, hi = slice(0, SUBLANES), slice(SUBLANES, 2 * SUBLANES)
        slabs = [bc(a0, 0) + a1[lo], bc(a0, 0) + a1[hi]]
        ci = [bc(i0, 0), bc(i0, 0)]
        cj = [i1[lo], i1[hi]]
        for r in range(1, SUBLANES):
            slabs.append(bc(a0, r) + a1[lo])
            ci.append(bc(i0, r))
            cj.append(i1[lo])
        slabs.append(a0[hi] + bc(a1, 0))
        ci.append(i0[hi])
        cj.append(bc(i1, 0))
        slabs = [jnp.where(sub < nv, sl, neg) for sl, nv in zip(slabs, _CAND_VALID)]
        ids = [a * N_KEYS + b for a, b in zip(ci, cj)]
        vals = jnp.zeros((PEER_TOPK, LANES), F32)
        esel = jnp.zeros((PEER_TOPK, LANES), I32)
        for r in range(PEER_TOPK):
            level = list(zip(slabs, ids))
            while len(level) > 1:
                nxt = []
                for (va, ea), (vb, eb) in zip(level[0::2], level[1::2]):
                    take = vb > va
                    nxt.append((jnp.where(take, vb, va), jnp.where(take, eb, ea)))
                if len(level) % 2:
                    nxt.append(level[-1])
                level = nxt
            v8, e8 = level[0]
            for sh in (4, 2, 1):
                vr, er = pltpu.roll(v8, sh, 0), pltpu.roll(e8, sh, 0)
                take = vr > v8
                v8, e8 = jnp.where(take, vr, v8), jnp.where(take, er, e8)
            m, ex = v8[0:1, :], e8[0:1, :]
            slabs = [jnp.where(eid == ex, neg, sl) for sl, eid in zip(slabs, ids)]
            vals = jnp.where(kid == r, m, vals)
            esel = jnp.where(kid == r, ex, esel)
        e = jnp.exp(vals - vals[0:1, :])
        return esel, e / jnp.sum(e, axis=0, keepdims=True)

    return head()


def _peer_body(h2_ref, h2n_ref, wqt_ref, keys_ref, u_ref, v_ref, x1_ref, mod_ref, fng_ref,
               oc_ref, ol_ref,
               q_scr, e_scr, g_scr, et_scr, gt_scr, gs_scr, acc_scr,
               *, rows, pitch, units, n_ctx_tiles, final):
    m = pl.program_id(0)
    e = pl.program_id(1)
    tm = h2_ref.shape[0]
    n_chunks = tm // LANES
    nsel = PEER_HEADS * PEER_TOPK
    slot = lax.rem(m, 2)

    def project_queries(src_ref):
        q = lax.dot_general(wqt_ref[...], src_ref[...], _NT, preferred_element_type=F32).astype(BF16)
        for c in range(n_chunks):
            q_scr[c] = q[:, c * LANES:(c + 1) * LANES]

    def route_unit(u, dst):
        c = u // PEER_HEADS
        h = lax.rem(u, PEER_HEADS)
        qs = [q_scr[c, pl.ds(pl.multiple_of(h * (2 * N_KEYS) + p * N_KEYS, N_KEYS), N_KEYS), :]
              for p in range(2)]
        esel, gates = _route_head(qs, keys_ref)
        row = pl.multiple_of(h * PEER_TOPK, PEER_TOPK)
        e_scr[dst, c, pl.ds(row, PEER_TOPK), :] = esel
        g_scr[dst, c, pl.ds(row, PEER_TOPK), :] = gates

    @pl.when(jnp.logical_and(e == 0, m == 0))
    def _first_tile_routing():
        project_queries(h2_ref)

        def unit(u, c):
            route_unit(u, 0)
            return c

        lax.fori_loop(0, n_chunks * PEER_HEADS, unit, 0)

    @pl.when(e == 0)
    def _build():
        for c in range(n_chunks):
            et_scr[c * LANES:(c + 1) * LANES, :] = e_scr[slot, c].T
            gt_scr[c * LANES:(c + 1) * LANES, :] = g_scr[slot, c].T
        kio = lax.broadcasted_iota(I32, (N_KEYS, nsel), 0)

        def tok(t, c):
            erow = et_scr[pl.ds(t, 1), :]
            grow = gt_scr[pl.ds(t, 1), :]
            at = jnp.where(kio == (erow >> 7), grow, 0.0).astype(BF16)
            bt = jnp.where(kio == (erow & (N_KEYS - 1)), 1.0, 0.0).astype(BF16)
            gt = lax.dot_general(at, bt, _NT, preferred_element_type=F32)
            hi = pltpu.bitcast(gt[0:rows, :], jnp.uint32) & jnp.uint32(0xFFFF0000)
            lo = pltpu.bitcast(gt[rows:2 * rows, :], jnp.uint32) >> 16
            gs_scr[pl.ds(pl.multiple_of(t * pitch, SUBLANES), rows), :] = hi | lo
            return c

        lax.fori_loop(0, tm, tok, 0, unroll=128)
        acc_scr[...] = jnp.zeros_like(acc_scr)
        project_queries(h2n_ref)

    for k in range(units):
        route_unit(e * units + k, 1 - slot)

    per_sub = PEER_SUB // N_KEYS
    per_step = u_ref.shape[0] // N_KEYS
    steps_per_half = rows // per_step
    row0 = lax.rem(e, steps_per_half) * per_step
    shift = jnp.where(e < steps_per_half, 0, 16).astype(jnp.uint32)
    h2 = h2_ref[...]
    total = None
    for sb in range(per_step // per_sub):
        es = slice(sb * PEER_SUB, (sb + 1) * PEER_SUB)
        hmat = lax.dot_general(h2, u_ref[es, :], _NT, preferred_element_type=F32)
        words = jnp.concatenate(
            [gs_scr[pl.ds(row0 + sb * per_sub + ii, tm, stride=pitch), :] for ii in range(per_sub)], axis=1)
        gm = pltpu.bitcast((words << shift) & jnp.uint32(0xFFFF0000), F32)
        act = jax.nn.gelu(hmat.astype(BF16)) * gm.astype(BF16)
        part = jnp.dot(act, v_ref[es, :], preferred_element_type=F32)
        total = part if total is None else total + part
    acc_scr[...] += total

    def result():
        x2 = x1_ref[...] + mod_ref[0][5:6, :] * acc_scr[...]
        return _rms(x2, fng_ref[...]) if final else x2

    last = e == pl.num_programs(1) - 1

    @pl.when(jnp.logical_and(last, m < n_ctx_tiles))
    def _out_ctx():
        oc_ref[...] = result()

    @pl.when(jnp.logical_and(last, m >= n_ctx_tiles))
    def _out_lat():
        ol_ref[...] = result()


def _peer(h2, wqt, keys, u, v, layer, x1, mod, fng, n_ctx_tok, lat_len, final):
    t = h2.shape[0]
    nsel = PEER_HEADS * PEER_TOPK
    rows = N_KEYS // 2
    pitch = rows + SUBLANES
    n_e = (N_KEYS * N_KEYS) // PEER_EB
    n_m = t // TM_PEER
    n_chunks = TM_PEER // LANES
    units = (n_chunks * PEER_HEADS) // n_e
    assert units * n_e == n_chunks * PEER_HEADS
    midx = functools.partial(_mod_index, tm=TM_PEER, n_ctx_tok=n_ctx_tok, lat_len=lat_len)
    tok = lambda wd: pl.BlockSpec((TM_PEER, wd), lambda m, e: (m, 0))
    nxt = pl.BlockSpec((TM_PEER, D_MODEL), lambda m, e: (jnp.minimum(m + 1, n_m - 1), 0),
                       pipeline_mode=pl.Buffered(1))
    espec = pl.BlockSpec((None, PEER_EB, D_MODEL), lambda m, e: (layer, e, 0))
    single = dict(pipeline_mode=pl.Buffered(1))
    return pl.pallas_call(
        functools.partial(_peer_body, rows=rows, pitch=pitch, units=units,
                          n_ctx_tiles=n_ctx_tok // TM_PEER, final=final),
        grid=(n_m, n_e),
        in_specs=[tok(D_MODEL), nxt,
                  _layer_spec(wqt, layer, **single),
                  _layer_spec(keys, layer, **single),
                  espec, espec,
                  pl.BlockSpec((TM_PEER, D_MODEL), lambda m, e: (m, 0), **single),
                  pl.BlockSpec((1, N_MOD, D_MODEL), lambda m, e: (midx(m), 0, 0)),
                  _const_spec((1, D_MODEL))],
        out_specs=_dual_specs(TM_PEER, n_ctx_tok // TM_PEER),
        out_shape=[jax.ShapeDtypeStruct((n_ctx_tok, D_MODEL), F32),
                   jax.ShapeDtypeStruct((t - n_ctx_tok, D_MODEL), F32)],
        scratch_shapes=[pltpu.VMEM((n_chunks, wqt.shape[1], LANES), BF16),
                        pltpu.VMEM((2, n_chunks, nsel, LANES), I32),
                        pltpu.VMEM((2, n_chunks, nsel, LANES), F32),
                        pltpu.VMEM((TM_PEER, nsel), I32),
                        pltpu.VMEM((TM_PEER, nsel), F32),
                        pltpu.VMEM((TM_PEER * pitch, N_KEYS), jnp.uint32),
                        pltpu.VMEM((TM_PEER, D_MODEL), F32)],
        compiler_params=_cparams(("arbitrary", "arbitrary"), VMEM_LIMIT),
        name="peer",
    )(h2, h2, wqt, keys, u, v, x1, mod, fng)


def _cast_body(u_ref, v_ref, uo_ref, vo_ref):
    uo_ref[...] = u_ref[...].astype(BF16)
    vo_ref[...] = v_ref[...].astype(BF16)


def _cast_tables(u, v):
    depth, n_e, d = u.shape
    rows = 2 * PEER_EB
    assert u.shape == v.shape and n_e % rows == 0
    spec = pl.BlockSpec((None, rows, d), lambda l, e: (l, e, 0))
    return pl.pallas_call(
        _cast_body,
        grid=(depth, n_e // rows),
        in_specs=[spec, spec],
        out_specs=[spec, spec],
        out_shape=[jax.ShapeDtypeStruct(u.shape, BF16)] * 2,
        compiler_params=_cparams(("parallel", "parallel"), VMEM_LIMIT),
        name="cast_tables",
    )(u, v)


def _wkv_state_shape(n_groups, vs):
    vl = HS_WKV // vs
    if _value_rows_on_sublanes(vl):
        return (n_groups, vl // SUBLANES, HS_WKV, SUBLANES, LANES)
    return (n_groups, vl, HS_WKV, LANES)


def _wkv_state_in(s, vs):
    n, depth = s.shape[:2]
    vl = HS_WKV // vs
    if _value_rows_on_sublanes(vl):
        s = s.reshape(n, depth, N_DIR, H_WKV, vs, vl // SUBLANES, SUBLANES, HS_WKV)
        s = s.transpose(1, 2, 5, 7, 6, 4, 0, 3)
    else:
        s = s.reshape(n, depth, N_DIR, H_WKV, vs, vl, HS_WKV).transpose(1, 2, 5, 6, 4, 0, 3)
    return s.reshape((depth,) + _wkv_state_shape(N_DIR, vs))


def _wkv_state_out(layers, n_sb, spb, vs):
    depth = len(layers)
    s = jnp.stack(layers, axis=0)
    vl = HS_WKV // vs
    if _value_rows_on_sublanes(vl):
        s = s.reshape(depth, N_DIR, n_sb, vl // SUBLANES, HS_WKV, SUBLANES, vs, spb, H_WKV)
        s = s.transpose(2, 7, 0, 1, 8, 6, 3, 5, 4)
    else:
        s = s.reshape(depth, N_DIR, n_sb, vl, HS_WKV, vs, spb, H_WKV).transpose(2, 6, 0, 1, 7, 5, 3, 4)
    return s.reshape(n_sb * spb, depth, N_DIR, H_WKV, HS_WKV, HS_WKV)


def _stacked_weights(prm):
    depth = prm["w_in"].shape[0]
    eye_h = jnp.eye(H_LRU, dtype=F32)
    eye_d = jnp.eye(N_DIR, dtype=F32)

    def perm(x, axis=-1):
        x = jnp.moveaxis(x, axis, -1)
        lead = x.shape[:-1]
        x = x.reshape(lead + (H_WKV, HS_WKV)).swapaxes(-1, -2).reshape(lead + (W_MIX,))
        return jnp.moveaxis(x, -1, axis)

    def lru_bd(wt):
        return jnp.einsum("ldhij,hg->lhidgj", wt, eye_h).reshape(depth, W_MIX, N_DIR * W_MIX)

    def lora_bd(wt):
        r = wt.shape[2]
        return jnp.einsum("ldrc,de->ldrec", wt, eye_d).reshape(depth, N_DIR * r, N_DIR * W_MIX)

    w_in = prm["w_in"]
    pad = jnp.zeros((depth, D_MODEL, Z_COLS - 5504), F32)
    rkv = [perm(w_in[:, :, 2560 + j * W_MIX:2560 + (j + 1) * W_MIX]) for j in range(3)]
    w_in_perm = jnp.concatenate(
        [w_in[:, :, 0:1536]] + rkv
        + [w_in[:, :, 1536:2560], w_in[:, :, 4480:5504], w_in[:, :, 4096:4480], pad], axis=2).astype(BF16)
    row = lambda x: x.reshape(depth, 1, -1).astype(F32)
    head_of = np.arange(W_MIX) % H_WKV
    w_branch = prm["w_branch"]
    w_branch = jnp.concatenate(
        [w_branch[:, 0], w_branch[:, 1], perm(w_branch[:, 2], axis=1), w_branch[:, 3]], axis=1)
    return {
        "w_in": w_in_perm,
        "w_gate": w_in[:, :, 5504:].astype(BF16),
        "norm1_g": row(prm["norm1_g"]),
        "norm2_g": row(prm["norm2_g"]),
        "conv_a_w": prm["conv_a_w"],
        "conv_b_w": prm["conv_b_w"],
        "conv_b_b": row(prm["conv_b_b"]),
        "lru_w": jnp.concatenate([lru_bd(prm["lru_wa"]), lru_bd(prm["lru_wx"])], axis=2).astype(BF16),
        "lru_b": jnp.concatenate([row(prm["lru_ba"]), row(prm["lru_bx"])], axis=2),
        "lru_lam": row(prm["lru_lambda"]),
        "w0": row(perm(prm["rwkv_w0"])),
        "w2": lora_bd(perm(prm["rwkv_w2"])).astype(BF16),
        "a0": row(perm(prm["rwkv_a0"])),
        "a2": lora_bd(perm(prm["rwkv_a2"])).astype(BF16),
        "g2": perm(prm["rwkv_g2"]).astype(BF16),
        "kk": row(perm(prm["rwkv_kk"])),
        "ka": row(perm(prm["rwkv_ka"])),
        "rk": row(perm(prm["rwkv_rk"].reshape(depth, W_MIX))),
        "lnx_g": row(perm(prm["lnx_g"])),
        "lnx_b": row(perm(prm["lnx_b"])),
        "sg_ln_g": row(prm["sg_ln_g"]),
        "sg_ln_b": row(prm["sg_ln_b"]),
        "sg_ws": prm["sg_ws"].astype(BF16),
        "sg_bst": prm["sg_bs"].transpose(0, 2, 1),
        "gate_b": row(prm["gate_b"]),
        "w_branch": w_branch.astype(BF16),
        "w_out": prm["w_out"].astype(BF16),
        "wq_t": prm["peer_wq"].transpose(0, 2, 1).astype(BF16),
        "keys": prm["peer_keys"].astype(BF16),
        "ind": jnp.asarray(head_of[:, None] == head_of[None, :], BF16)[None],
    }


def kernel(x_prompt, x_sample, state_lru, state_wkv, c, c_ctx, norm1_g, norm2_g, w_mod, b_mod, w_in, conv_a_w, conv_b_w, conv_b_b, lru_wa, lru_ba, lru_wx, lru_bx, lru_lambda, rwkv_w0, rwkv_w2, rwkv_a0, rwkv_a2, rwkv_g2, rwkv_kk, rwkv_ka, rwkv_rk, lnx_g, lnx_b, sg_ln_g, sg_ln_b, sg_ws, sg_bs, gate_b, w_branch, w_out, peer_wq, peer_keys, peer_u, peer_v, final_norm_g):
    prm = dict(norm1_g=norm1_g, norm2_g=norm2_g, w_in=w_in, conv_a_w=conv_a_w, conv_b_w=conv_b_w,
               conv_b_b=conv_b_b, lru_wa=lru_wa, lru_ba=lru_ba, lru_wx=lru_wx, lru_bx=lru_bx,
               lru_lambda=lru_lambda, rwkv_w0=rwkv_w0, rwkv_w2=rwkv_w2, rwkv_a0=rwkv_a0, rwkv_a2=rwkv_a2,
               rwkv_g2=rwkv_g2, rwkv_kk=rwkv_kk, rwkv_ka=rwkv_ka, rwkv_rk=rwkv_rk, lnx_g=lnx_g,
               lnx_b=lnx_b, sg_ln_g=sg_ln_g, sg_ln_b=sg_ln_b, sg_ws=sg_ws, sg_bs=sg_bs, gate_b=gate_b,
               w_branch=w_branch, w_out=w_out, peer_wq=peer_wq, peer_keys=peer_keys, peer_u=peer_u,
               peer_v=peer_v)
    bc, lc, _ = x_prompt.shape
    bl, ll, _ = x_sample.shape
    depth = w_mod.shape[0]
    n_ctx_tok = bc * lc
    n_ctx_tiles = n_ctx_tok // TM
    lat_tiles = ll // TM
    ctx_spb = min(WKV_CTX_SPB, bc)
    lru_spb = ll // lc
    assert lc == TM and ll % TM_PEER == 0 and n_ctx_tok % TM_PEER == 0 and bl + 1 <= SUBLANES
    assert ll % GRID_W == 0 and TM % GRID_W == 0 and bc % ctx_spb == 0 and n_ctx_tok % ll == 0
    assert LANES % (ctx_spb * H_WKV) == 0 and LANES % (bl * H_WKV) == 0

    cond = jnp.zeros((SUBLANES, D_MODEL), F32).at[0].set(c_ctx).at[1:1 + bl].set(c)
    mods = _modulation(cond, w_mod, b_mod).reshape(depth, SUBLANES, N_MOD, D_MODEL)
    fng = final_norm_g.reshape(1, D_MODEL)
    x_ctx = x_prompt.reshape(n_ctx_tok, D_MODEL)
    x_lat = x_sample.reshape(bl * ll, D_MODEL)
    ctx_vs = LANES // (ctx_spb * H_WKV)
    lat_vs = LANES // (bl * H_WKV)
    n_sb = bc // ctx_spb
    wkv_zero = jnp.zeros(_wkv_state_shape(N_DIR * n_sb, ctx_vs), F32)
    u_all, v_all = _cast_tables(peer_u, peer_v)
    new_lru, new_wkv = [], []
    pnames = ["ya", "yd", "gbg", "la", "lu", "g", "bon", "rt", "vt", "kkt", "wt", "kt", "bt"]
    lw = _stacked_weights(prm)
    wkv_lat0 = _wkv_state_in(state_wkv.astype(F32), lat_vs)
    lru_h0 = jnp.zeros((depth, n_ctx_tok // ll + bl, lru_spb, N_DIR * W_MIX), F32).at[:, n_ctx_tok // ll:, 0].set(
        state_lru.astype(F32).reshape(bl, depth, N_DIR * W_MIX).swapaxes(0, 1))
    for i in range(depth):
        mod = mods[i]
        p = dict(zip(pnames, _prep(x_ctx, x_lat, mod, lw, i, ll)))

        h, lru_s = _lru_scan(p["la"], p["lu"], lru_h0[i],
                             rows=ll, n_ctx_blocks=n_ctx_tok // ll, ctx_cfg=(lru_spb, lc), lat_cfg=(1, ll))
        new_lru.append(lru_s[:n_ctx_tok // ll].reshape(bc, N_DIR, W_MIX))

        wkv_in = [p[n] for n in ("rt", "wt", "kt", "kkt", "bt", "vt")]
        y_c, s_c = _wkv_scan(*wkv_in, wkv_zero, tile0=0, n_seq=bc, seq_tiles=1, spb=ctx_spb)
        y_l, _ = _wkv_scan(*wkv_in, wkv_lat0[i], tile0=n_ctx_tiles, n_seq=bl, seq_tiles=lat_tiles, spb=1)
        new_wkv.append(s_c)

        tok_in = [p["ya"], p["yd"], p["gbg"], h, p["g"], p["bon"]]
        x1, h2 = _merge(x_ctx, x_lat, mod, lw, i, tok_in, y_c, y_l, ll)
        x_ctx, x_lat = _peer(h2, lw["wq_t"], lw["keys"], u_all, v_all, i, x1, mod, fng,
                             n_ctx_tok, ll, final=(i == depth - 1))
    y_prompt = x_ctx.reshape(bc, lc, D_MODEL)
    y_sample = x_lat.reshape(bl, ll, D_MODEL)
    return (y_prompt, y_sample, jnp.stack(new_lru, axis=1), _wkv_state_out(new_wkv, n_sb, ctx_spb, ctx_vs))
```

```python
import functools

import numpy as np
import jax
import jax.numpy as jnp
from jax import lax
from jax.experimental import pallas as pl
from jax.experimental.pallas import tpu as pltpu

F32 = jnp.float32
BF16 = jnp.bfloat16
I32 = jnp.int32

D_MODEL = 1024
W_MIX = 512
N_DIR = 2
H_WKV = 8
HS_WKV = 64
H_LRU = 8
LORA_W = 64
LORA_A = 64
LORA_G = 128
GRID_W = 64
CHUNK = 128
G_SG = 4
N_KEYS = 128
PEER_HEADS = 8
PEER_TOPK = 16
N_MOD = 6
EPS = 1e-6
LNX_EPS = 64e-5
LRU_C = 8.0

LANES = 128
SUBLANES = 8
TM = 256
TM_PEER = 512
PEER_EB = 1024
PEER_SUB = 512
Z_COLS = 5632
WKV_TC = LANES
WKV_CTX_SPB = 8
VMEM_LIMIT = 56 * 1024 * 1024

_NT = (((1,), (1,)), ((), ()))


def _cparams(sem, vmem=None):
    return pltpu.CompilerParams(dimension_semantics=sem, vmem_limit_bytes=vmem)


def _const_spec(shape):
    nd = len(shape)
    return pl.BlockSpec(shape, lambda *_: (0,) * nd)


def _layer_spec(arr, layer, **kw):
    nd = arr.ndim
    idx = layer if arr.shape[0] > 1 else 0
    return pl.BlockSpec((None,) + arr.shape[1:], lambda *_: (idx,) + (0,) * (nd - 1), **kw)


def _softplus(x):
    return jnp.maximum(x, 0.0) + jnp.log1p(jnp.exp(-jnp.abs(x)))


def _rms(x, g):
    return x * lax.rsqrt(jnp.mean(x * x, axis=-1, keepdims=True) + EPS) * g


def _segsum(x, ind):
    hi = x.astype(BF16)
    lo = (x - hi.astype(F32)).astype(BF16)
    return (jnp.dot(hi, ind, preferred_element_type=F32)
            + jnp.dot(lo, ind, preferred_element_type=F32))


def _mod_index(i, tm, n_ctx_tok, lat_len):
    n_ctx_tiles = n_ctx_tok // tm
    tiles_per_seq = lat_len // tm
    return jnp.where(i < n_ctx_tiles, 0, 1 + lax.div(i - n_ctx_tiles, tiles_per_seq))


def _mod_body(s_ref, w_ref, b_ref, o_ref):
    s = s_ref[...]
    s = s * jax.nn.sigmoid(s)
    o_ref[0] = jnp.dot(s.astype(BF16), w_ref[0].astype(BF16), preferred_element_type=F32) + b_ref[0]


def _modulation(cond, w_mod, b_mod):
    depth = w_mod.shape[0]
    n = w_mod.shape[2]
    tn = 1536
    return pl.pallas_call(
        _mod_body,
        grid=(depth, n // tn),
        in_specs=[_const_spec((SUBLANES, D_MODEL)),
                  pl.BlockSpec((1, D_MODEL, tn), lambda l, j: (l, 0, j)),
                  pl.BlockSpec((1, 1, tn), lambda l, j: (l, 0, j))],
        out_specs=pl.BlockSpec((1, SUBLANES, tn), lambda l, j: (l, 0, j)),
        out_shape=jax.ShapeDtypeStruct((depth, SUBLANES, n), F32),
        compiler_params=_cparams(("parallel", "parallel"), VMEM_LIMIT),
        name="modulation",
    )(cond, w_mod, b_mod.reshape(depth, 1, n))


def _prep_body(xc_ref, xl_ref, xp_ref, xn_ref, mod_ref, n1g_ref, win_ref,
               caw_ref, cbw_ref, cbb_ref, lruw_ref, lrub_ref, lam_ref,
               w0_ref, w2_ref, a0_ref, a2_ref, g2_ref, kkw_ref, ka_ref, rk_ref,
               lng_ref, lnb_ref, ws_ref, bst_ref, ind_ref,
               ya_ref, yd_ref, gbg_ref, la_ref, lu_ref, g_ref, bon_ref,
               rt_ref, vt_ref, kkt_ref, wt_ref, kt_ref, bt_ref,
               *, n_ctx_tiles, tiles_per_seq):
    i = pl.program_id(0)
    is_ctx = i < n_ctx_tiles
    t = lax.broadcasted_iota(I32, (TM, 1), 0)
    ind = ind_ref[...]
    m = mod_ref[0]

    def modulated(xv):
        return (_rms(xv, n1g_ref[...]) * (1.0 + m[1:2, :]) + m[0:1, :]).astype(BF16)

    def project(hv, lo, hi):
        return jnp.dot(hv, win_ref[:, lo:hi], preferred_element_type=F32)

    h = modulated(jnp.where(is_ctx, xc_ref[...], xl_ref[...]))
    za = project(h, 0, 1536)
    zc = project(h, 1536, 3072)
    zb = project(h, 3072, 4096)
    zd = project(h, 4096, 5120)
    zl = project(h, 5120, 5120 + 2 * LORA_W + 2 * LORA_A + LORA_G)
    halo_prev = project(modulated(xp_ref[...]), 3072 + W_MIX, 4096)
    halo_next = project(modulated(xn_ref[...]), 3072 + W_MIX, 4096)

    pm = jnp.where(is_ctx, TM - 1, GRID_W - 1)
    pos = t & pm
    a_b = za[:, 0:W_MIX]
    ac = za[:, W_MIX:2 * W_MIX] * za[:, 2 * W_MIX:3 * W_MIX]
    up = jnp.where(pos == 0, 0.0, pltpu.roll(ac, 1, 0))
    dn = jnp.where(pos == pm, 0.0, pltpu.roll(ac, TM - 1, 0))
    ya_ref[...] = a_b * (caw_ref[0:1, :] * up + caw_ref[1:2, :] * ac + caw_ref[2:3, :] * dn)

    seq_tile = lax.rem(jnp.maximum(i - n_ctx_tiles, 0), tiles_per_seq)
    first = jnp.logical_or(is_ctx, seq_tile == 0)
    last = jnp.logical_or(is_ctx, seq_tile == tiles_per_seq - 1)
    prev = jnp.where(first, 0.0, halo_prev[SUBLANES - 1:SUBLANES, :])
    nxt0 = jnp.where(last, 0.0, halo_next[0:1, :])
    nxt1 = jnp.where(last, 0.0, halo_next[1:2, :])
    bx = zb[:, W_MIX:2 * W_MIX]
    m1 = jnp.where(t == 0, prev, pltpu.roll(bx, 1, 0))
    p1 = jnp.where(t == TM - 1, nxt0, pltpu.roll(bx, TM - 1, 0))
    p2 = jnp.where(t == TM - 2, nxt0, jnp.where(t == TM - 1, nxt1, pltpu.roll(bx, TM - 2, 0)))
    xb = (cbw_ref[0:1, :] * m1 + cbw_ref[1:2, :] * bx + cbw_ref[2:3, :] * p1
          + cbw_ref[3:4, :] * p2 + cbb_ref[...])
    gates = jnp.dot(xb.astype(BF16), lruw_ref[...], preferred_element_type=F32) + lrub_ref[...]
    rg = jax.nn.sigmoid(gates[:, 0:2 * W_MIX])
    ig = jax.nn.sigmoid(gates[:, 2 * W_MIX:4 * W_MIX])
    log_a = -LRU_C * rg * _softplus(-lam_ref[...])
    xb2 = jnp.concatenate([xb, xb], axis=1)
    a = jnp.exp(log_a)
    la_ref[...] = a
    lu_ref[...] = jnp.sqrt(jnp.tanh(-log_a) * (a * a + 1.0)) * (ig * xb2)
    gbg_ref[...] = jax.nn.gelu(zb[:, 0:W_MIX])

    zr = zc[:, 0:W_MIX]
    zk = zc[:, W_MIX:2 * W_MIX]
    zv = zc[:, 2 * W_MIX:3 * W_MIX]
    zwd = zl[:, 0:2 * LORA_W]
    zad = zl[:, 2 * LORA_W:2 * LORA_W + 2 * LORA_A]
    zgd = zl[:, 2 * LORA_W + 2 * LORA_A:2 * LORA_W + 2 * LORA_A + LORA_G]
    wlin = w0_ref[...] + jnp.dot(jnp.tanh(zwd).astype(BF16), w2_ref[...], preferred_element_type=F32)
    wt_ref[0] = jnp.exp(-jnp.exp(-_softplus(-wlin) - 0.5)).T
    av = jax.nn.sigmoid(a0_ref[...] + jnp.dot(zad.astype(BF16), a2_ref[...], preferred_element_type=F32))
    g_ref[...] = jnp.dot(jax.nn.sigmoid(zgd).astype(BF16), g2_ref[...], preferred_element_type=F32)
    kkr = zk * kkw_ref[...]
    kkn = kkr / jnp.maximum(jnp.sqrt(_segsum(kkr * kkr, ind)), 1e-12)
    zk2 = jnp.concatenate([zk, zk], axis=1)
    ka2 = jnp.concatenate([ka_ref[...], ka_ref[...]], axis=1)
    kd = zk2 * (1.0 + (av - 1.0) * ka2)
    kt_ref[0] = kd.T
    bt_ref[0] = (jnp.concatenate([kkn, kkn], axis=1) * av).T
    rt_ref[0] = zr.T
    vt_ref[0] = zv.T
    kkt_ref[0] = kkn.T
    bon_ref[...] = _segsum(zr * (kd[:, 0:W_MIX] + kd[:, W_MIX:2 * W_MIX]) * rk_ref[...], ind) * zv

    zg = jax.nn.gelu(zd)
    u = zg[:, 0:W_MIX]
    vv = zg[:, W_MIX:2 * W_MIX]
    vc = vv - jnp.mean(vv, axis=-1, keepdims=True)
    vn = vc * lax.rsqrt(jnp.mean(vc * vc, axis=-1, keepdims=True) + 1e-5) * lng_ref[...] + lnb_ref[...]
    for c in range(TM // CHUNK):
        rs = slice(c * CHUNK, (c + 1) * CHUNK)
        for gi in range(G_SG):
            cs = slice(gi * LANES, (gi + 1) * LANES)
            s = jnp.dot(ws_ref[gi], vn[rs, cs].astype(BF16), preferred_element_type=F32)
            yd_ref[rs, cs] = u[rs, cs] * (s + bst_ref[:, gi:gi + 1])


def _dual_specs(rows, n_ctx_blocks, **kw):
    return [pl.BlockSpec((rows, D_MODEL), lambda i, *_: (jnp.minimum(i, n_ctx_blocks - 1), 0), **kw),
            pl.BlockSpec((rows, D_MODEL), lambda i, *_: (jnp.maximum(i - n_ctx_blocks, 0), 0), **kw)]


def _prep(x_ctx, x_lat, mod, lw, layer, lat_len):
    n_ctx_tok = x_ctx.shape[0]
    t = n_ctx_tok + x_lat.shape[0]
    n_tiles = t // TM
    n_ctx_tiles = n_ctx_tok // TM
    tiles_per_seq = lat_len // TM
    rows8 = TM // SUBLANES
    last_blk = x_lat.shape[0] // SUBLANES - 1
    midx = functools.partial(_mod_index, tm=TM, n_ctx_tok=n_ctx_tok, lat_len=lat_len)

    def lat_blk8(i, off):
        return (jnp.clip((i - n_ctx_tiles) * rows8 + off, 0, last_blk), 0)

    x_specs = _dual_specs(TM, n_ctx_tiles) + [
        pl.BlockSpec((SUBLANES, D_MODEL), lambda i: lat_blk8(i, -1)),
        pl.BlockSpec((SUBLANES, D_MODEL), lambda i: lat_blk8(i, rows8)),
        pl.BlockSpec((1, N_MOD, D_MODEL), lambda i: (midx(i), 0, 0)),
        _layer_spec(lw["norm1_g"], layer),
        _layer_spec(lw["w_in"], layer, pipeline_mode=pl.Buffered(1)),
    ]
    wnames = ["conv_a_w", "conv_b_w", "conv_b_b", "lru_w", "lru_b", "lru_lam", "w0", "w2", "a0", "a2",
              "g2", "kk", "ka", "rk", "sg_ln_g", "sg_ln_b", "sg_ws", "sg_bst", "ind"]
    wts = [lw[n] for n in wnames]
    w_specs = [_layer_spec(w, layer) for w in wts]
    widths = [W_MIX, W_MIX, W_MIX, 2 * W_MIX, 2 * W_MIX, W_MIX, W_MIX]
    t_rows = [W_MIX, W_MIX, W_MIX, 2 * W_MIX, 2 * W_MIX, 2 * W_MIX]
    out_specs = ([pl.BlockSpec((TM, wd), lambda i: (i, 0)) for wd in widths]
                 + [pl.BlockSpec((1, r, TM), lambda i: (i, 0, 0)) for r in t_rows])
    out_shape = ([jax.ShapeDtypeStruct((t, wd), F32) for wd in widths]
                 + [jax.ShapeDtypeStruct((n_tiles, r, TM), F32) for r in t_rows])
    return pl.pallas_call(
        functools.partial(_prep_body, n_ctx_tiles=n_ctx_tiles, tiles_per_seq=tiles_per_seq),
        grid=(n_tiles,),
        in_specs=x_specs + w_specs,
        out_specs=out_specs,
        out_shape=out_shape,
        compiler_params=_cparams(("parallel",), VMEM_LIMIT),
        name="branch_prep",
    )(x_ctx, x_lat, x_lat, x_lat, mod, lw["norm1_g"], lw["w_in"], *wts)


def _lru_body(a_ref, u_ref, h0_ref, h_ref, hf_ref, *, n_ctx_blocks, ctx_cfg, lat_cfg):
    fw, bw = slice(0, W_MIX), slice(W_MIX, 2 * W_MIX)

    def scan(nseq, l):
        def step(s, carry):
            out = []
            for j in range(nseq):
                tf = j * l + s
                tb = j * l + (l - 1 - s)
                hf = a_ref[pl.ds(tf, 1), fw] * carry[2 * j] + u_ref[pl.ds(tf, 1), fw]
                hb = a_ref[pl.ds(tb, 1), bw] * carry[2 * j + 1] + u_ref[pl.ds(tb, 1), bw]
                h_ref[pl.ds(tf, 1), fw] = hf
                h_ref[pl.ds(tb, 1), bw] = hb
                out += [hf, hb]
            return tuple(out)

        init = []
        for j in range(nseq):
            init += [h0_ref[0, j:j + 1, fw], h0_ref[0, j:j + 1, bw]]
        fin = lax.fori_loop(0, l, step, tuple(init), unroll=2)
        hf_ref[0] = h0_ref[0]
        for j in range(nseq):
            hf_ref[0, j:j + 1, fw] = fin[2 * j]
            hf_ref[0, j:j + 1, bw] = fin[2 * j + 1]

    is_ctx = pl.program_id(0) < n_ctx_blocks
    pl.when(is_ctx)(lambda: scan(*ctx_cfg))
    pl.when(jnp.logical_not(is_ctx))(lambda: scan(*lat_cfg))


def _lru_scan(a, u, h0, *, rows, n_ctx_blocks, ctx_cfg, lat_cfg):
    nb = a.shape[0] // rows
    w = a.shape[1]
    tok = pl.BlockSpec((rows, w), lambda i: (i, 0))
    st = pl.BlockSpec((1,) + h0.shape[1:], lambda i: (i, 0, 0))
    return pl.pallas_call(
        functools.partial(_lru_body, n_ctx_blocks=n_ctx_blocks, ctx_cfg=ctx_cfg, lat_cfg=lat_cfg),
        grid=(nb,),
        in_specs=[tok, tok, st],
        out_specs=[tok, st],
        out_shape=[jax.ShapeDtypeStruct(a.shape, F32), jax.ShapeDtypeStruct(h0.shape, F32)],
        compiler_params=_cparams(("parallel",), VMEM_LIMIT),
        name="lru_scan",
    )(a, u, h0)


_SLOT_ORDER = (0, 4, 2, 6, 1, 5, 3, 7)


def _rowsum8(parts):
    sub = lax.broadcasted_iota(I32, (SUBLANES, LANES), 0)
    slots = [parts[i] for i in _SLOT_ORDER]
    roll = pltpu.roll
    lvl1 = [jnp.where(sub < 4, a + roll(a, 4, 0), b + roll(b, 4, 0))
            for a, b in zip(slots[0::2], slots[1::2])]
    lvl2 = [jnp.where((sub & 3) < 2, a + roll(a, 6, 0), roll(b + roll(b, 6, 0), 2, 0))
            for a, b in zip(lvl1[0::2], lvl1[1::2])]
    a, b = lvl2
    return jnp.where((sub & 1) == 0, a + roll(a, 7, 0), roll(b + roll(b, 7, 0), 1, 0))


def _fold8(x):
    return jnp.sum(x.reshape(HS_WKV // SUBLANES, SUBLANES, LANES), axis=0)


def _value_rows_on_sublanes(vl_n):
    return vl_n // SUBLANES < 4


def _wkv_body(*refs, nsrc, spb, vs, n_sb, tc, kp, vp):
    vl_n = HS_WKV // vs
    n_in = 6 * nsrc
    k_srcs = [refs[o * nsrc:(o + 1) * nsrc] for o in range(5)]
    v_srcs = refs[5 * nsrc:n_in]
    s0_ref = refs[n_in]
    y_ref = refs[n_in + 1]
    sf_ref = refs[n_in + 2]
    k_scr = refs[n_in + 3:n_in + 8]
    v_scr, y_scr, s_scr = refs[n_in + 8:n_in + 11]
    rows_on_sublanes = _value_rows_on_sublanes(HS_WKV // vs)
    sa_scr = None if rows_on_sublanes else refs[n_in + 11]
    r_scr, w_scr, k_scr_, kk_scr, b_scr = k_scr
    backward = pl.program_id(0) // n_sb == 1
    seqs = [(s, j) for s in range(nsrc) for j in range(spb)]

    @pl.when(pl.program_id(1) == 0)
    def _():
        s_scr[...] = s0_ref[0]

    def build_k(c, carry):
        row = pl.multiple_of(c * H_WKV, H_WKV)
        for o in range(5):
            slab = [k_srcs[o][s][j, pl.ds(row, H_WKV), :] for s, j in seqs]
            k_scr[o][pl.ds(c, tc, stride=kp), :] = jnp.concatenate(slab * vs, axis=0).T
        return carry

    lax.fori_loop(0, HS_WKV, build_k, 0, unroll=4)

    def build_v(vl, carry):
        slab = []
        for vsi in range(vs):
            row = pl.multiple_of((vsi * vl_n + vl) * H_WKV, H_WKV)
            slab += [v_srcs[s][j, pl.ds(row, H_WKV), :] for s, j in seqs]
        v_scr[pl.ds(vl, tc, stride=vp), :] = jnp.concatenate(slab, axis=0).T
        return carry

    lax.fori_loop(0, vl_n, build_v, 0, unroll=4)

    def step_keys_on_sublanes(s, carry):
        t = jnp.where(backward, tc - 1 - s, s)
        krow = pl.multiple_of(t * kp, SUBLANES)
        vrow = pl.multiple_of(t * vp, SUBLANES)
        kslab = pl.ds(krow, HS_WKV)
        for g in range(vl_n // SUBLANES):
            parts = [_fold8(s_scr[g * SUBLANES + i] * kk_scr[kslab, :]) for i in range(SUBLANES)]
            sa_scr[g * SUBLANES:(g + 1) * SUBLANES, :] = _rowsum8(parts)
        for g in range(vl_n // SUBLANES):
            parts = []
            for i in range(SUBLANES):
                vl = g * SUBLANES + i
                sa = sa_scr[vl:vl + 1, :]
                vv = v_scr[pl.ds(vrow + vl, 1), :]
                sn = s_scr[vl] * w_scr[kslab, :] - sa * b_scr[kslab, :] + vv * k_scr_[kslab, :]
                s_scr[vl] = sn
                parts.append(_fold8(sn * r_scr[kslab, :]))
            y_scr[pl.ds(pl.multiple_of(vrow + g * SUBLANES, SUBLANES), SUBLANES), :] = _rowsum8(parts)
        return carry

    def step_rows_on_sublanes(s, carry):
        t = jnp.where(backward, tc - 1 - s, s)
        krow = pl.multiple_of(t * kp, SUBLANES)
        vrow = pl.multiple_of(t * vp, SUBLANES)
        n_g = vl_n // SUBLANES
        batch = min(n_g, 2)
        n_acc = 4 // batch

        def row(ref, k):
            return jnp.broadcast_to(ref[pl.ds(krow + k, 1), :], (SUBLANES, LANES))

        def total(parts):
            while len(parts) > 1:
                parts = [a + b for a, b in zip(parts[0::2], parts[1::2])]
            return parts[0]

        def accumulate(acc, g, k, p):
            acc[g][k % n_acc] = p if acc[g][k % n_acc] is None else acc[g][k % n_acc] + p

        for g0 in range(0, n_g, batch):
            gs = range(g0, g0 + batch)
            acc = {g: [None] * n_acc for g in gs}
            for k in range(HS_WKV):
                kk = row(kk_scr, k)
                for g in gs:
                    accumulate(acc, g, k, s_scr[g, k] * kk)
            sa = {g: total(acc[g]) for g in gs}
            vv = {g: v_scr[pl.ds(pl.multiple_of(vrow + g * SUBLANES, SUBLANES), SUBLANES), :] for g in gs}
            acc = {g: [None] * n_acc for g in gs}
            for k in range(HS_WKV):
                w, b, kx, r = row(w_scr, k), row(b_scr, k), row(k_scr_, k), row(r_scr, k)
                for g in gs:
                    sn = s_scr[g, k] * w - sa[g] * b + vv[g] * kx
                    s_scr[g, k] = sn
                    accumulate(acc, g, k, sn * r)
            for g in gs:
                y_scr[pl.ds(pl.multiple_of(vrow + g * SUBLANES, SUBLANES), SUBLANES), :] = total(acc[g])
        return carry

    lax.fori_loop(0, tc, step_rows_on_sublanes if rows_on_sublanes else step_keys_on_sublanes, 0)

    def emit_y(vl, carry):
        yt = y_scr[pl.ds(vl, tc, stride=vp), :].T
        for vsi in range(vs):
            row = pl.multiple_of((vsi * vl_n + vl) * H_WKV, H_WKV)
            for n, (s, j) in enumerate(seqs):
                lane0 = (vsi * len(seqs) + n) * H_WKV
                y_ref[0, 0, s * spb + j, pl.ds(row, H_WKV), :] = yt[lane0:lane0 + H_WKV, :]
        return carry

    lax.fori_loop(0, vl_n, emit_y, 0, unroll=4)
    sf_ref[0] = s_scr[...]


def _wkv_scan(rt, wt, kt, kkt, bt, vt, s0, *, tile0, n_seq, seq_tiles, spb):
    tc = WKV_TC
    if spb > 1:
        assert seq_tiles == 1 and n_seq % spb == 0 and tile0 % spb == 0
        nsrc, n_sb = 1, n_seq // spb
    else:
        nsrc, n_sb = n_seq, 1
    inst = nsrc * spb * H_WKV
    vs = LANES // inst
    vl_n = HS_WKV // vs
    assert vl_n % SUBLANES == 0, "value rows are processed eight at a time"
    cpt = TM // tc
    n_chunks = seq_tiles * cpt
    kp = HS_WKV + SUBLANES
    vp = vl_n + SUBLANES if ((vl_n + SUBLANES) // SUBLANES) % 2 else vl_n + 2 * SUBLANES

    def chunk(g, i):
        return jnp.where(g // n_sb == 1, n_chunks - 1 - i, i)

    def in_map(g, i, *, src, per_dir):
        ce = chunk(g, i)
        rb = (g // n_sb) if per_dir else 0
        if spb > 1:
            return (tile0 // spb + g % n_sb, rb, ce)
        return (tile0 + src * seq_tiles + ce // cpt, rb, ce % cpt)

    def out_map(g, i):
        ce = chunk(g, i)
        if spb > 1:
            return (g // n_sb, 0, g % n_sb, 0, ce)
        return (g // n_sb, ce // cpt, 0, 0, ce % cpt)

    in_specs, operands = [], []
    for arr, per_dir in ((rt, False), (wt, True), (kt, True), (kkt, False), (bt, True), (vt, False)):
        for src in range(nsrc):
            in_specs.append(pl.BlockSpec((spb, W_MIX, tc), functools.partial(in_map, src=src, per_dir=per_dir),
                                         pipeline_mode=pl.Buffered(1)))
            operands.append(arr)
    state_block = (1,) + _wkv_state_shape(1, vs)[1:]
    sspec = pl.BlockSpec(state_block, lambda g, i: (g,) + (0,) * (len(state_block) - 1))
    in_specs.append(sspec)
    out_specs = [pl.BlockSpec((1, 1, nsrc * spb, W_MIX, tc), out_map)]
    out_shape = [jax.ShapeDtypeStruct((N_DIR, seq_tiles, n_seq, W_MIX, TM), F32)]
    res = pl.pallas_call(
        functools.partial(_wkv_body, nsrc=nsrc, spb=spb, vs=vs, n_sb=n_sb, tc=tc, kp=kp, vp=vp),
        grid=(N_DIR * n_sb, n_chunks),
        in_specs=in_specs,
        out_specs=out_specs + [sspec],
        out_shape=out_shape + [jax.ShapeDtypeStruct(s0.shape, F32)],
        scratch_shapes=([pltpu.VMEM((tc * kp, LANES), F32)] * 5
                        + [pltpu.VMEM((tc * vp, LANES), F32)] * 2
                        + [pltpu.VMEM(state_block[1:], F32)]
                        + ([] if _value_rows_on_sublanes(vl_n) else [pltpu.VMEM((vl_n, LANES), F32)])),
        compiler_params=_cparams(("parallel", "arbitrary"), VMEM_LIMIT),
        name="wkv_scan",
    )(*operands, s0)
    return res[0], res[1]


def _merge_body(xc_ref, xl_ref, mod_ref, n1g_ref, n2g_ref, wg_ref, gb_ref, wbr_ref, wo_ref,
                lnxg_ref, lnxb_ref, ind_ref,
                ya_ref, yd_ref, gbg_ref, h_ref, ycf_ref, ycb_ref, ylf_ref, ylb_ref, g_ref, bon_ref,
                x1_ref, h2_ref, *, n_ctx_tiles):
    is_ctx = pl.program_id(0) < n_ctx_tiles
    x = jnp.where(is_ctx, xc_ref[...], xl_ref[...])
    m = mod_ref[0]
    ind = ind_ref[...]
    h = (_rms(x, n1g_ref[...]) * (1.0 + m[1:2, :]) + m[0:1, :]).astype(BF16)
    y_b = gbg_ref[...] * (h_ref[:, 0:W_MIX] + h_ref[:, W_MIX:2 * W_MIX])
    y = jnp.where(is_ctx, ycf_ref[0, 0, 0] + ycb_ref[0, 0, 0], ylf_ref[0, 0, 0] + ylb_ref[0, 0, 0]).T
    yc = y - _segsum(y, ind) * (1.0 / HS_WKV)
    var = _segsum(yc * yc, ind) * (1.0 / HS_WKV)
    y_c = (yc * lax.rsqrt(var + LNX_EPS) * lnxg_ref[...] + lnxb_ref[...] + bon_ref[...]) * g_ref[...]
    merged = None
    for n, yn in enumerate((ya_ref[...], y_b, y_c, yd_ref[...])):
        cs = slice(n * D_MODEL, (n + 1) * D_MODEL)
        gate = jax.nn.sigmoid(jnp.dot(h, wg_ref[:, cs], preferred_element_type=F32) + gb_ref[:, cs])
        br = jnp.dot(yn.astype(BF16), wbr_ref[n * W_MIX:(n + 1) * W_MIX, :], preferred_element_type=F32)
        merged = gate * br if merged is None else merged + gate * br
    mo = jnp.dot(merged.astype(BF16), wo_ref[...], preferred_element_type=F32)
    x1 = x + m[2:3, :] * mo
    x1_ref[...] = x1
    h2_ref[...] = (_rms(x1, n2g_ref[...]) * (1.0 + m[4:5, :]) + m[3:4, :]).astype(BF16)


def _merge(x_ctx, x_lat, mod, lw, layer, tok_in, y_ctx, y_lat, lat_len):
    n_ctx_tok = x_ctx.shape[0]
    t = n_ctx_tok + x_lat.shape[0]
    n_ctx_tiles = n_ctx_tok // TM
    tps = lat_len // TM
    midx = functools.partial(_mod_index, tm=TM, n_ctx_tok=n_ctx_tok, lat_len=lat_len)
    wnames = ["norm1_g", "norm2_g", "w_gate", "gate_b", "w_branch", "w_out", "lnx_g", "lnx_b", "ind"]
    wts = [lw[n] for n in wnames]
    tok = lambda wd: pl.BlockSpec((TM, wd), lambda i: (i, 0))
    ya, yd, gbg, h, g, bon = tok_in
    yblock = (1, 1, 1, W_MIX, TM)

    def ctx_spec(d):
        return pl.BlockSpec(yblock, lambda i: (d, 0, jnp.minimum(i, n_ctx_tiles - 1), 0, 0))

    def lat_spec(d):
        def imap(i):
            r = jnp.maximum(i - n_ctx_tiles, 0)
            return (d, lax.rem(r, tps), lax.div(r, tps), 0, 0)
        return pl.BlockSpec(yblock, imap)

    return pl.pallas_call(
        functools.partial(_merge_body, n_ctx_tiles=n_ctx_tiles),
        grid=(t // TM,),
        in_specs=(_dual_specs(TM, n_ctx_tiles)
                  + [pl.BlockSpec((1, N_MOD, D_MODEL), lambda i: (midx(i), 0, 0))]
                  + [_layer_spec(w, layer) for w in wts]
                  + [tok(W_MIX), tok(W_MIX), tok(W_MIX), tok(2 * W_MIX),
                     ctx_spec(0), ctx_spec(1), lat_spec(0), lat_spec(1), tok(W_MIX), tok(W_MIX)]),
        out_specs=[tok(D_MODEL), tok(D_MODEL)],
        out_shape=[jax.ShapeDtypeStruct((t, D_MODEL), F32), jax.ShapeDtypeStruct((t, D_MODEL), BF16)],
        compiler_params=_cparams(("parallel",), VMEM_LIMIT),
        name="merge",
    )(x_ctx, x_lat, mod, *wts, ya, yd, gbg, h, y_ctx, y_ctx, y_lat, y_lat, g, bon)


_CAND_VALID = (8, 8, 8, 5, 4, 3, 2, 2, 2, 8)


def _oddeven_pairs(n):
    def merge(lo, hi, r):
        step = r * 2
        if step < hi - lo:
            yield from merge(lo, hi, step)
            yield from merge(lo + r, hi, step)
            yield from [(i, i + r) for i in range(lo + r, hi - r, step)]
        else:
            yield (lo, lo + r)

    def sort(lo, hi):
        if hi - lo >= 1:
            mid = lo + (hi - lo) // 2
            yield from sort(lo, mid)
            yield from sort(mid + 1, hi)
            yield from merge(lo, hi, 1)

    return tuple(sort(0, n - 1))


_SORT16 = _oddeven_pairs(N_KEYS // SUBLANES)


def _route_head(qs, keys_ref):
    sub = lax.broadcasted_iota(I32, (SUBLANES, LANES), 0)
    kid = lax.broadcasted_iota(I32, (PEER_TOPK, LANES), 0)
    neg = -jnp.inf

    def bc(x, r):
        return jnp.broadcast_to(x[r:r + 1, :], (SUBLANES, LANES))

    def head():
        tops = []
        for p in range(2):
            s = jnp.dot(keys_ref[p], qs[p], preferred_element_type=F32)
            cols = [s[j * SUBLANES:(j + 1) * SUBLANES, :] for j in range(N_KEYS // SUBLANES)]
            cidx = [sub + j * SUBLANES for j in range(N_KEYS // SUBLANES)]
            for a, b in _SORT16:
                take = cols[b] > cols[a]
                cols[a], cols[b] = jnp.where(take, cols[b], cols[a]), jnp.where(take, cols[a], cols[b])
                cidx[a], cidx[b] = jnp.where(take, cidx[b], cidx[a]), jnp.where(take, cidx[a], cidx[b])
            vals = jnp.zeros((PEER_TOPK, LANES), F32)
            idxs = jnp.zeros((PEER_TOPK, LANES), I32)
            for r in range(PEER_TOPK):
                v8, i8 = cols[0], cidx[0]
                for sh in (4, 2, 1):
                    vr, ir = pltpu.roll(v8, sh, 0), pltpu.roll(i8, sh, 0)
                    take = vr > v8
                    v8, i8 = jnp.where(take, vr, v8), jnp.where(take, ir, i8)
                m, ix = v8[0:1, :], i8[0:1, :]
                vals = jnp.where(kid == r, m, vals)
                idxs = jnp.where(kid == r, ix, idxs)
                popped = cidx[0] == ix
                for j in range(PEER_TOPK - 1 - r):
                    cols[j] = jnp.where(popped, cols[j + 1], cols[j])
                    cidx[j] = jnp.where(popped, cidx[j + 1], cidx[j])
            tops.append((vals, idxs))
        (a0, i0), (a1, i1) = tops
        lo, hi = slice(0, SUBLANES), slice(SUBLANES, 2 * SUBLANES)
        slabs = [bc(a0, 0) + a1[lo], bc(a0, 0) + a1[hi]]
        ci = [bc(i0, 0), bc(i0, 0)]
        cj = [i1[lo], i1[hi]]
        for r in range(1, SUBLANES):
            slabs.append(bc(a0, r) + a1[lo])
            ci.append(bc(i0, r))
            cj.append(i1[lo])
        slabs.append(a0[hi] + bc(a1, 0))
        ci.append(i0[hi])
        cj.append(bc(i1, 0))
        slabs = [jnp.where(sub < nv, sl, neg) for sl, nv in zip(slabs, _CAND_VALID)]
        ids = [a * N_KEYS + b for a, b in zip(ci, cj)]
        vals = jnp.zeros((PEER_TOPK, LANES), F32)
        esel = jnp.zeros((PEER_TOPK, LANES), I32)
        for r in range(PEER_TOPK):
            level = list(zip(slabs, ids))
            while len(level) > 1:
                nxt = []
                for (va, ea), (vb, eb) in zip(level[0::2], level[1::2]):
                    take = vb > va
                    nxt.append((jnp.where(take, vb, va), jnp.where(take, eb, ea)))
                if len(level) % 2:
                    nxt.append(level[-1])
                level = nxt
            v8, e8 = level[0]
            for sh in (4, 2, 1):
                vr, er = pltpu.roll(v8, sh, 0), pltpu.roll(e8, sh, 0)
                take = vr > v8
                v8, e8 = jnp.where(take, vr, v8), jnp.where(take, er, e8)
            m, ex = v8[0:1, :], e8[0:1, :]
            slabs = [jnp.where(eid == ex, neg, sl) for sl, eid in zip(slabs, ids)]
            vals = jnp.where(kid == r, m, vals)
            esel = jnp.where(kid == r, ex, esel)
        e = jnp.exp(vals - vals[0:1, :])
        return esel, e / jnp.sum(e, axis=0, keepdims=True)

    return head()


def _peer_body(h2_ref, h2n_ref, wqt_ref, keys_ref, u_ref, v_ref, x1_ref, mod_ref, fng_ref,
               oc_ref, ol_ref,
               q_scr, e_scr, g_scr, et_scr, gt_scr, gs_scr, acc_scr,
               *, rows, pitch, units, n_ctx_tiles, final):
    m = pl.program_id(0)
    e = pl.program_id(1)
    tm = h2_ref.shape[0]
    n_chunks = tm // LANES
    nsel = PEER_HEADS * PEER_TOPK
    slot = lax.rem(m, 2)

    def project_queries(src_ref):
        q = lax.dot_general(wqt_ref[...], src_ref[...], _NT, preferred_element_type=F32).astype(BF16)
        for c in range(n_chunks):
            q_scr[c] = q[:, c * LANES:(c + 1) * LANES]

    def route_unit(u, dst):
        c = u // PEER_HEADS
        h = lax.rem(u, PEER_HEADS)
        qs = [q_scr[c, pl.ds(pl.multiple_of(h * (2 * N_KEYS) + p * N_KEYS, N_KEYS), N_KEYS), :]
              for p in range(2)]
        esel, gates = _route_head(qs, keys_ref)
        row = pl.multiple_of(h * PEER_TOPK, PEER_TOPK)
        e_scr[dst, c, pl.ds(row, PEER_TOPK), :] = esel
        g_scr[dst, c, pl.ds(row, PEER_TOPK), :] = gates

    @pl.when(jnp.logical_and(e == 0, m == 0))
    def _first_tile_routing():
        project_queries(h2_ref)

        def unit(u, c):
            route_unit(u, 0)
            return c

        lax.fori_loop(0, n_chunks * PEER_HEADS, unit, 0)

    @pl.when(e == 0)
    def _build():
        for c in range(n_chunks):
            et_scr[c * LANES:(c + 1) * LANES, :] = e_scr[slot, c].T
            gt_scr[c * LANES:(c + 1) * LANES, :] = g_scr[slot, c].T
        kio = lax.broadcasted_iota(I32, (N_KEYS, nsel), 0)

        def tok(t, c):
            erow = et_scr[pl.ds(t, 1), :]
            grow = gt_scr[pl.ds(t, 1), :]
            at = jnp.where(kio == (erow >> 7), grow, 0.0).astype(BF16)
            bt = jnp.where(kio == (erow & (N_KEYS - 1)), 1.0, 0.0).astype(BF16)
            gt = lax.dot_general(at, bt, _NT, preferred_element_type=F32)
            hi = pltpu.bitcast(gt[0:rows, :], jnp.uint32) & jnp.uint32(0xFFFF0000)
            lo = pltpu.bitcast(gt[rows:2 * rows, :], jnp.uint32) >> 16
            gs_scr[pl.ds(pl.multiple_of(t * pitch, SUBLANES), rows), :] = hi | lo
            return c

        lax.fori_loop(0, tm, tok, 0, unroll=128)
        acc_scr[...] = jnp.zeros_like(acc_scr)
        project_queries(h2n_ref)

    for k in range(units):
        route_unit(e * units + k, 1 - slot)

    per_sub = PEER_SUB // N_KEYS
    per_step = u_ref.shape[0] // N_KEYS
    steps_per_half = rows // per_step
    row0 = lax.rem(e, steps_per_half) * per_step
    shift = jnp.where(e < steps_per_half, 0, 16).astype(jnp.uint32)
    h2 = h2_ref[...]
    total = None
    for sb in range(per_step // per_sub):
        es = slice(sb * PEER_SUB, (sb + 1) * PEER_SUB)
        hmat = lax.dot_general(h2, u_ref[es, :], _NT, preferred_element_type=F32)
        words = jnp.concatenate(
            [gs_scr[pl.ds(row0 + sb * per_sub + ii, tm, stride=pitch), :] for ii in range(per_sub)], axis=1)
        gm = pltpu.bitcast((words << shift) & jnp.uint32(0xFFFF0000), F32)
        act = jax.nn.gelu(hmat.astype(BF16)) * gm.astype(BF16)
        part = jnp.dot(act, v_ref[es, :], preferred_element_type=F32)
        total = part if total is None else total + part
    acc_scr[...] += total

    def result():
        x2 = x1_ref[...] + mod_ref[0][5:6, :] * acc_scr[...]
        return _rms(x2, fng_ref[...]) if final else x2

    last = e == pl.num_programs(1) - 1

    @pl.when(jnp.logical_and(last, m < n_ctx_tiles))
    def _out_ctx():
        oc_ref[...] = result()

    @pl.when(jnp.logical_and(last, m >= n_ctx_tiles))
    def _out_lat():
        ol_ref[...] = result()


def _peer(h2, wqt, keys, u, v, layer, x1, mod, fng, n_ctx_tok, lat_len, final):
    t = h2.shape[0]
    nsel = PEER_HEADS * PEER_TOPK
    rows = N_KEYS // 2
    pitch = rows + SUBLANES
    n_e = (N_KEYS * N_KEYS) // PEER_EB
    n_m = t // TM_PEER
    n_chunks = TM_PEER // LANES
    units = (n_chunks * PEER_HEADS) // n_e
    assert units * n_e == n_chunks * PEER_HEADS
    midx = functools.partial(_mod_index, tm=TM_PEER, n_ctx_tok=n_ctx_tok, lat_len=lat_len)
    tok = lambda wd: pl.BlockSpec((TM_PEER, wd), lambda m, e: (m, 0))
    nxt = pl.BlockSpec((TM_PEER, D_MODEL), lambda m, e: (jnp.minimum(m + 1, n_m - 1), 0),
                       pipeline_mode=pl.Buffered(1))
    espec = pl.BlockSpec((None, PEER_EB, D_MODEL), lambda m, e: (layer, e, 0))
    single = dict(pipeline_mode=pl.Buffered(1))
    return pl.pallas_call(
        functools.partial(_peer_body, rows=rows, pitch=pitch, units=units,
                          n_ctx_tiles=n_ctx_tok // TM_PEER, final=final),
        grid=(n_m, n_e),
        in_specs=[tok(D_MODEL), nxt,
                  _layer_spec(wqt, layer, **single),
                  _layer_spec(keys, layer, **single),
                  espec, espec,
                  pl.BlockSpec((TM_PEER, D_MODEL), lambda m, e: (m, 0), **single),
                  pl.BlockSpec((1, N_MOD, D_MODEL), lambda m, e: (midx(m), 0, 0)),
                  _const_spec((1, D_MODEL))],
        out_specs=_dual_specs(TM_PEER, n_ctx_tok // TM_PEER),
        out_shape=[jax.ShapeDtypeStruct((n_ctx_tok, D_MODEL), F32),
                   jax.ShapeDtypeStruct((t - n_ctx_tok, D_MODEL), F32)],
        scratch_shapes=[pltpu.VMEM((n_chunks, wqt.shape[1], LANES), BF16),
                        pltpu.VMEM((2, n_chunks, nsel, LANES), I32),
                        pltpu.VMEM((2, n_chunks, nsel, LANES), F32),
                        pltpu.VMEM((TM_PEER, nsel), I32),
                        pltpu.VMEM((TM_PEER, nsel), F32),
                        pltpu.VMEM((TM_PEER * pitch, N_KEYS), jnp.uint32),
                        pltpu.VMEM((TM_PEER, D_MODEL), F32)],
        compiler_params=_cparams(("arbitrary", "arbitrary"), VMEM_LIMIT),
        name="peer",
    )(h2, h2, wqt, keys, u, v, x1, mod, fng)


CAST_STREAMS = 4


def _cast_body(*refs):
    ins, (uo_ref, vo_ref) = refs[:2 * CAST_STREAMS], refs[2 * CAST_STREAMS:]
    for j in range(CAST_STREAMS):
        uo_ref[j] = ins[j][...].astype(BF16)
        vo_ref[j] = ins[CAST_STREAMS + j][...].astype(BF16)


def _cast_tables(u, v):
    depth, n_e, d = u.shape
    rows = PEER_EB // 2
    part = n_e // CAST_STREAMS
    assert u.shape == v.shape and part % rows == 0
    u = u.reshape(depth, CAST_STREAMS, part, d)
    v = v.reshape(depth, CAST_STREAMS, part, d)
    in_specs = [pl.BlockSpec((None, None, rows, d), lambda l, e, j=j: (l, j, e, 0)) for j in range(CAST_STREAMS)]
    out_spec = pl.BlockSpec((None, CAST_STREAMS, rows, d), lambda l, e: (l, 0, e, 0))
    uo, vo = pl.pallas_call(
        _cast_body,
        grid=(depth, part // rows),
        in_specs=in_specs * 2,
        out_specs=[out_spec, out_spec],
        out_shape=[jax.ShapeDtypeStruct(u.shape, BF16)] * 2,
        compiler_params=_cparams(("parallel", "parallel"), VMEM_LIMIT),
        name="cast_tables",
    )(*([u] * CAST_STREAMS + [v] * CAST_STREAMS))
    return uo.reshape(depth, n_e, d), vo.reshape(depth, n_e, d)


def _wkv_state_shape(n_groups, vs):
    vl = HS_WKV // vs
    if _value_rows_on_sublanes(vl):
        return (n_groups, vl // SUBLANES, HS_WKV, SUBLANES, LANES)
    return (n_groups, vl, HS_WKV, LANES)


def _wkv_state_in(s, vs):
    n, depth = s.shape[:2]
    vl = HS_WKV // vs
    if _value_rows_on_sublanes(vl):
        s = s.reshape(n, depth, N_DIR, H_WKV, vs, vl // SUBLANES, SUBLANES, HS_WKV)
        s = s.transpose(1, 2, 5, 7, 6, 4, 0, 3)
    else:
        s = s.reshape(n, depth, N_DIR, H_WKV, vs, vl, HS_WKV).transpose(1, 2, 5, 6, 4, 0, 3)
    return s.reshape((depth,) + _wkv_state_shape(N_DIR, vs))


def _wkv_state_out(layers, n_sb, spb, vs):
    depth = len(layers)
    s = jnp.stack(layers, axis=0)
    vl = HS_WKV // vs
    if _value_rows_on_sublanes(vl):
        s = s.reshape(depth, N_DIR, n_sb, vl // SUBLANES, HS_WKV, SUBLANES, vs, spb, H_WKV)
        s = s.transpose(2, 7, 0, 1, 8, 6, 3, 5, 4)
    else:
        s = s.reshape(depth, N_DIR, n_sb, vl, HS_WKV, vs, spb, H_WKV).transpose(2, 6, 0, 1, 7, 5, 3, 4)
    return s.reshape(n_sb * spb, depth, N_DIR, H_WKV, HS_WKV, HS_WKV)


def _stacked_weights(prm):
    depth = prm["w_in"].shape[0]
    eye_h = jnp.eye(H_LRU, dtype=F32)
    eye_d = jnp.eye(N_DIR, dtype=F32)

    def perm(x, axis=-1):
        x = jnp.moveaxis(x, axis, -1)
        lead = x.shape[:-1]
        x = x.reshape(lead + (H_WKV, HS_WKV)).swapaxes(-1, -2).reshape(lead + (W_MIX,))
        return jnp.moveaxis(x, -1, axis)

    def lru_bd(wt):
        return jnp.einsum("ldhij,hg->lhidgj", wt, eye_h).reshape(depth, W_MIX, N_DIR * W_MIX)

    def lora_bd(wt):
        r = wt.shape[2]
        return jnp.einsum("ldrc,de->ldrec", wt, eye_d).reshape(depth, N_DIR * r, N_DIR * W_MIX)

    w_in = prm["w_in"]
    pad = jnp.zeros((depth, D_MODEL, Z_COLS - 5504), F32)
    rkv = [perm(w_in[:, :, 2560 + j * W_MIX:2560 + (j + 1) * W_MIX]) for j in range(3)]
    w_in_perm = jnp.concatenate(
        [w_in[:, :, 0:1536]] + rkv
        + [w_in[:, :, 1536:2560], w_in[:, :, 4480:5504], w_in[:, :, 4096:4480], pad], axis=2).astype(BF16)
    row = lambda x: x.reshape(depth, 1, -1).astype(F32)
    head_of = np.arange(W_MIX) % H_WKV
    w_branch = prm["w_branch"]
    w_branch = jnp.concatenate(
        [w_branch[:, 0], w_branch[:, 1], perm(w_branch[:, 2], axis=1), w_branch[:, 3]], axis=1)
    return {
        "w_in": w_in_perm,
        "w_gate": w_in[:, :, 5504:].astype(BF16),
        "norm1_g": row(prm["norm1_g"]),
        "norm2_g": row(prm["norm2_g"]),
        "conv_a_w": prm["conv_a_w"],
        "conv_b_w": prm["conv_b_w"],
        "conv_b_b": row(prm["conv_b_b"]),
        "lru_w": jnp.concatenate([lru_bd(prm["lru_wa"]), lru_bd(prm["lru_wx"])], axis=2).astype(BF16),
        "lru_b": jnp.concatenate([row(prm["lru_ba"]), row(prm["lru_bx"])], axis=2),
        "lru_lam": row(prm["lru_lambda"]),
        "w0": row(perm(prm["rwkv_w0"])),
        "w2": lora_bd(perm(prm["rwkv_w2"])).astype(BF16),
        "a0": row(perm(prm["rwkv_a0"])),
        "a2": lora_bd(perm(prm["rwkv_a2"])).astype(BF16),
        "g2": perm(prm["rwkv_g2"]).astype(BF16),
        "kk": row(perm(prm["rwkv_kk"])),
        "ka": row(perm(prm["rwkv_ka"])),
        "rk": row(perm(prm["rwkv_rk"].reshape(depth, W_MIX))),
        "lnx_g": row(perm(prm["lnx_g"])),
        "lnx_b": row(perm(prm["lnx_b"])),
        "sg_ln_g": row(prm["sg_ln_g"]),
        "sg_ln_b": row(prm["sg_ln_b"]),
        "sg_ws": prm["sg_ws"].astype(BF16),
        "sg_bst": prm["sg_bs"].transpose(0, 2, 1),
        "gate_b": row(prm["gate_b"]),
        "w_branch": w_branch.astype(BF16),
        "w_out": prm["w_out"].astype(BF16),
        "wq_t": prm["peer_wq"].transpose(0, 2, 1).astype(BF16),
        "keys": prm["peer_keys"].astype(BF16),
        "ind": jnp.asarray(head_of[:, None] == head_of[None, :], BF16)[None],
    }


def kernel(x_prompt, x_sample, state_lru, state_wkv, c, c_ctx, norm1_g, norm2_g, w_mod, b_mod, w_in, conv_a_w, conv_b_w, conv_b_b, lru_wa, lru_ba, lru_wx, lru_bx, lru_lambda, rwkv_w0, rwkv_w2, rwkv_a0, rwkv_a2, rwkv_g2, rwkv_kk, rwkv_ka, rwkv_rk, lnx_g, lnx_b, sg_ln_g, sg_ln_b, sg_ws, sg_bs, gate_b, w_branch, w_out, peer_wq, peer_keys, peer_u, peer_v, final_norm_g):
    prm = dict(norm1_g=norm1_g, norm2_g=norm2_g, w_in=w_in, conv_a_w=conv_a_w, conv_b_w=conv_b_w,
               conv_b_b=conv_b_b, lru_wa=lru_wa, lru_ba=lru_ba, lru_wx=lru_wx, lru_bx=lru_bx,
               lru_lambda=lru_lambda, rwkv_w0=rwkv_w0, rwkv_w2=rwkv_w2, rwkv_a0=rwkv_a0, rwkv_a2=rwkv_a2,
               rwkv_g2=rwkv_g2, rwkv_kk=rwkv_kk, rwkv_ka=rwkv_ka, rwkv_rk=rwkv_rk, lnx_g=lnx_g,
               lnx_b=lnx_b, sg_ln_g=sg_ln_g, sg_ln_b=sg_ln_b, sg_ws=sg_ws, sg_bs=sg_bs, gate_b=gate_b,
               w_branch=w_branch, w_out=w_out, peer_wq=peer_wq, peer_keys=peer_keys, peer_u=peer_u,
               peer_v=peer_v)
    bc, lc, _ = x_prompt.shape
    bl, ll, _ = x_sample.shape
    depth = w_mod.shape[0]
    n_ctx_tok = bc * lc
    n_ctx_tiles = n_ctx_tok // TM
    lat_tiles = ll // TM
    ctx_spb = min(WKV_CTX_SPB, bc)
    lru_spb = ll // lc
    assert lc == TM and ll % TM_PEER == 0 and n_ctx_tok % TM_PEER == 0 and bl + 1 <= SUBLANES
    assert ll % GRID_W == 0 and TM % GRID_W == 0 and bc % ctx_spb == 0 and n_ctx_tok % ll == 0
    assert LANES % (ctx_spb * H_WKV) == 0 and LANES % (bl * H_WKV) == 0

    cond = jnp.zeros((SUBLANES, D_MODEL), F32).at[0].set(c_ctx).at[1:1 + bl].set(c)
    mods = _modulation(cond, w_mod, b_mod).reshape(depth, SUBLANES, N_MOD, D_MODEL)
    fng = final_norm_g.reshape(1, D_MODEL)
    x_ctx = x_prompt.reshape(n_ctx_tok, D_MODEL)
    x_lat = x_sample.reshape(bl * ll, D_MODEL)
    ctx_vs = LANES // (ctx_spb * H_WKV)
    lat_vs = LANES // (bl * H_WKV)
    n_sb = bc // ctx_spb
    wkv_zero = jnp.zeros(_wkv_state_shape(N_DIR * n_sb, ctx_vs), F32)
    u_all, v_all = _cast_tables(peer_u, peer_v)
    new_lru, new_wkv = [], []
    pnames = ["ya", "yd", "gbg", "la", "lu", "g", "bon", "rt", "vt", "kkt", "wt", "kt", "bt"]
    lw = _stacked_weights(prm)
    wkv_lat0 = _wkv_state_in(state_wkv.astype(F32), lat_vs)
    lru_h0 = jnp.zeros((depth, n_ctx_tok // ll + bl, lru_spb, N_DIR * W_MIX), F32).at[:, n_ctx_tok // ll:, 0].set(
        state_lru.astype(F32).reshape(bl, depth, N_DIR * W_MIX).swapaxes(0, 1))
    for i in range(depth):
        mod = mods[i]
        p = dict(zip(pnames, _prep(x_ctx, x_lat, mod, lw, i, ll)))

        h, lru_s = _lru_scan(p["la"], p["lu"], lru_h0[i],
                             rows=ll, n_ctx_blocks=n_ctx_tok // ll, ctx_cfg=(lru_spb, lc), lat_cfg=(1, ll))
        new_lru.append(lru_s[:n_ctx_tok // ll].reshape(bc, N_DIR, W_MIX))

        wkv_in = [p[n] for n in ("rt", "wt", "kt", "kkt", "bt", "vt")]
        y_c, s_c = _wkv_scan(*wkv_in, wkv_zero, tile0=0, n_seq=bc, seq_tiles=1, spb=ctx_spb)
        y_l, _ = _wkv_scan(*wkv_in, wkv_lat0[i], tile0=n_ctx_tiles, n_seq=bl, seq_tiles=lat_tiles, spb=1)
        new_wkv.append(s_c)

        tok_in = [p["ya"], p["yd"], p["gbg"], h, p["g"], p["bon"]]
        x1, h2 = _merge(x_ctx, x_lat, mod, lw, i, tok_in, y_c, y_l, ll)
        x_ctx, x_lat = _peer(h2, lw["wq_t"], lw["keys"], u_all, v_all, i, x1, mod, fng,
                             n_ctx_tok, ll, final=(i == depth - 1))
    y_prompt = x_ctx.reshape(bc, lc, D_MODEL)
    y_sample = x_lat.reshape(bl, ll, D_MODEL)
    return (y_prompt, y_sample, jnp.stack(new_lru, axis=1), _wkv_state_out(new_wkv, n_sb, ctx_spb, ctx_vs))
```

```python
import functools

import numpy as np
import jax
import jax.numpy as jnp
from jax import lax
from jax.experimental import pallas as pl
from jax.experimental.pallas import tpu as pltpu

F32 = jnp.float32
BF16 = jnp.bfloat16
I32 = jnp.int32

D_MODEL = 1024
W_MIX = 512
N_DIR = 2
H_WKV = 8
HS_WKV = 64
H_LRU = 8
LORA_W = 64
LORA_A = 64
LORA_G = 128
GRID_W = 64
CHUNK = 128
G_SG = 4
N_KEYS = 128
PEER_HEADS = 8
PEER_TOPK = 16
N_MOD = 6
EPS = 1e-6
LNX_EPS = 64e-5
LRU_C = 8.0

LANES = 128
SUBLANES = 8
TM = 256
TM_PEER = 512
PEER_EB = 1024
PEER_SUB = 512
Z_COLS = 5632
WKV_TC = LANES
WKV_CTX_SPB = 8
VMEM_LIMIT = 56 * 1024 * 1024

_NT = (((1,), (1,)), ((), ()))


def _cparams(sem, vmem=None):
    return pltpu.CompilerParams(dimension_semantics=sem, vmem_limit_bytes=vmem)


def _const_spec(shape):
    nd = len(shape)
    return pl.BlockSpec(shape, lambda *_: (0,) * nd)


def _layer_spec(arr, layer, **kw):
    nd = arr.ndim
    idx = layer if arr.shape[0] > 1 else 0
    return pl.BlockSpec((None,) + arr.shape[1:], lambda *_: (idx,) + (0,) * (nd - 1), **kw)


def _softplus(x):
    return jnp.maximum(x, 0.0) + jnp.log1p(jnp.exp(-jnp.abs(x)))


def _rms(x, g):
    return x * lax.rsqrt(jnp.mean(x * x, axis=-1, keepdims=True) + EPS) * g


def _segsum(x, ind):
    hi = x.astype(BF16)
    lo = (x - hi.astype(F32)).astype(BF16)
    return (jnp.dot(hi, ind, preferred_element_type=F32)
            + jnp.dot(lo, ind, preferred_element_type=F32))


def _mod_index(i, tm, n_ctx_tok, lat_len):
    n_ctx_tiles = n_ctx_tok // tm
    tiles_per_seq = lat_len // tm
    return jnp.where(i < n_ctx_tiles, 0, 1 + lax.div(i - n_ctx_tiles, tiles_per_seq))


def _mod_body(s_ref, w_ref, b_ref, o_ref):
    s = s_ref[...]
    s = s * jax.nn.sigmoid(s)
    o_ref[0] = jnp.dot(s.astype(BF16), w_ref[0].astype(BF16), preferred_element_type=F32) + b_ref[0]


def _modulation(cond, w_mod, b_mod):
    depth = w_mod.shape[0]
    n = w_mod.shape[2]
    tn = 1536
    return pl.pallas_call(
        _mod_body,
        grid=(depth, n // tn),
        in_specs=[_const_spec((SUBLANES, D_MODEL)),
                  pl.BlockSpec((1, D_MODEL, tn), lambda l, j: (l, 0, j)),
                  pl.BlockSpec((1, 1, tn), lambda l, j: (l, 0, j))],
        out_specs=pl.BlockSpec((1, SUBLANES, tn), lambda l, j: (l, 0, j)),
        out_shape=jax.ShapeDtypeStruct((depth, SUBLANES, n), F32),
        compiler_params=_cparams(("parallel", "parallel"), VMEM_LIMIT),
        name="modulation",
    )(cond, w_mod, b_mod.reshape(depth, 1, n))


def _prep_body(xc_ref, xl_ref, xp_ref, xn_ref, mod_ref, n1g_ref, win_ref,
               caw_ref, cbw_ref, cbb_ref, lruw_ref, lrub_ref, lam_ref,
               w0_ref, w2_ref, a0_ref, a2_ref, g2_ref, kkw_ref, ka_ref, rk_ref,
               lng_ref, lnb_ref, ws_ref, bst_ref, ind_ref,
               ya_ref, yd_ref, gbg_ref, la_ref, lu_ref, g_ref, bon_ref,
               rt_ref, vt_ref, kkt_ref, wt_ref, kt_ref, bt_ref,
               *, n_ctx_tiles, tiles_per_seq):
    i = pl.program_id(0)
    is_ctx = i < n_ctx_tiles
    t = lax.broadcasted_iota(I32, (TM, 1), 0)
    ind = ind_ref[...]
    m = mod_ref[0]

    def modulated(xv):
        return (_rms(xv, n1g_ref[...]) * (1.0 + m[1:2, :]) + m[0:1, :]).astype(BF16)

    def project(hv, lo, hi):
        return jnp.dot(hv, win_ref[:, lo:hi], preferred_element_type=F32)

    h = modulated(jnp.where(is_ctx, xc_ref[...], xl_ref[...]))
    za = project(h, 0, 1536)
    zc = project(h, 1536, 3072)
    zb = project(h, 3072, 4096)
    zd = project(h, 4096, 5120)
    zl = project(h, 5120, 5120 + 2 * LORA_W + 2 * LORA_A + LORA_G)
    halo_prev = project(modulated(xp_ref[...]), 3072 + W_MIX, 4096)
    halo_next = project(modulated(xn_ref[...]), 3072 + W_MIX, 4096)

    pm = jnp.where(is_ctx, TM - 1, GRID_W - 1)
    pos = t & pm
    a_b = za[:, 0:W_MIX]
    ac = za[:, W_MIX:2 * W_MIX] * za[:, 2 * W_MIX:3 * W_MIX]
    up = jnp.where(pos == 0, 0.0, pltpu.roll(ac, 1, 0))
    dn = jnp.where(pos == pm, 0.0, pltpu.roll(ac, TM - 1, 0))
    ya_ref[...] = a_b * (caw_ref[0:1, :] * up + caw_ref[1:2, :] * ac + caw_ref[2:3, :] * dn)

    seq_tile = lax.rem(jnp.maximum(i - n_ctx_tiles, 0), tiles_per_seq)
    first = jnp.logical_or(is_ctx, seq_tile == 0)
    last = jnp.logical_or(is_ctx, seq_tile == tiles_per_seq - 1)
    prev = jnp.where(first, 0.0, halo_prev[SUBLANES - 1:SUBLANES, :])
    nxt0 = jnp.where(last, 0.0, halo_next[0:1, :])
    nxt1 = jnp.where(last, 0.0, halo_next[1:2, :])
    bx = zb[:, W_MIX:2 * W_MIX]
    m1 = jnp.where(t == 0, prev, pltpu.roll(bx, 1, 0))
    p1 = jnp.where(t == TM - 1, nxt0, pltpu.roll(bx, TM - 1, 0))
    p2 = jnp.where(t == TM - 2, nxt0, jnp.where(t == TM - 1, nxt1, pltpu.roll(bx, TM - 2, 0)))
    xb = (cbw_ref[0:1, :] * m1 + cbw_ref[1:2, :] * bx + cbw_ref[2:3, :] * p1
          + cbw_ref[3:4, :] * p2 + cbb_ref[...])
    gates = jnp.dot(xb.astype(BF16), lruw_ref[...], preferred_element_type=F32) + lrub_ref[...]
    rg = jax.nn.sigmoid(gates[:, 0:2 * W_MIX])
    ig = jax.nn.sigmoid(gates[:, 2 * W_MIX:4 * W_MIX])
    log_a = -LRU_C * rg * _softplus(-lam_ref[...])
    xb2 = jnp.concatenate([xb, xb], axis=1)
    a = jnp.exp(log_a)
    la_ref[...] = a
    lu_ref[...] = jnp.sqrt(jnp.tanh(-log_a) * (a * a + 1.0)) * (ig * xb2)
    gbg_ref[...] = jax.nn.gelu(zb[:, 0:W_MIX])

    zr = zc[:, 0:W_MIX]
    zk = zc[:, W_MIX:2 * W_MIX]
    zv = zc[:, 2 * W_MIX:3 * W_MIX]
    zwd = zl[:, 0:2 * LORA_W]
    zad = zl[:, 2 * LORA_W:2 * LORA_W + 2 * LORA_A]
    zgd = zl[:, 2 * LORA_W + 2 * LORA_A:2 * LORA_W + 2 * LORA_A + LORA_G]
    wlin = w0_ref[...] + jnp.dot(jnp.tanh(zwd).astype(BF16), w2_ref[...], preferred_element_type=F32)
    wt_ref[0] = jnp.exp(-jnp.exp(-_softplus(-wlin) - 0.5)).T
    av = jax.nn.sigmoid(a0_ref[...] + jnp.dot(zad.astype(BF16), a2_ref[...], preferred_element_type=F32))
    g_ref[...] = jnp.dot(jax.nn.sigmoid(zgd).astype(BF16), g2_ref[...], preferred_element_type=F32)
    kkr = zk * kkw_ref[...]
    kkn = kkr / jnp.maximum(jnp.sqrt(_segsum(kkr * kkr, ind)), 1e-12)
    zk2 = jnp.concatenate([zk, zk], axis=1)
    ka2 = jnp.concatenate([ka_ref[...], ka_ref[...]], axis=1)
    kd = zk2 * (1.0 + (av - 1.0) * ka2)
    kt_ref[0] = kd.T
    bt_ref[0] = (jnp.concatenate([kkn, kkn], axis=1) * av).T
    rt_ref[0] = zr.T
    vt_ref[0] = zv.T
    kkt_ref[0] = kkn.T
    bon_ref[...] = _segsum(zr * (kd[:, 0:W_MIX] + kd[:, W_MIX:2 * W_MIX]) * rk_ref[...], ind) * zv

    zg = jax.nn.gelu(zd)
    u = zg[:, 0:W_MIX]
    vv = zg[:, W_MIX:2 * W_MIX]
    vc = vv - jnp.mean(vv, axis=-1, keepdims=True)
    vn = vc * lax.rsqrt(jnp.mean(vc * vc, axis=-1, keepdims=True) + 1e-5) * lng_ref[...] + lnb_ref[...]
    for c in range(TM // CHUNK):
        rs = slice(c * CHUNK, (c + 1) * CHUNK)
        for gi in range(G_SG):
            cs = slice(gi * LANES, (gi + 1) * LANES)
            s = jnp.dot(ws_ref[gi], vn[rs, cs].astype(BF16), preferred_element_type=F32)
            yd_ref[rs, cs] = u[rs, cs] * (s + bst_ref[:, gi:gi + 1])


def _dual_specs(rows, n_ctx_blocks, **kw):
    return [pl.BlockSpec((rows, D_MODEL), lambda i, *_: (jnp.minimum(i, n_ctx_blocks - 1), 0), **kw),
            pl.BlockSpec((rows, D_MODEL), lambda i, *_: (jnp.maximum(i - n_ctx_blocks, 0), 0), **kw)]


def _prep(x_ctx, x_lat, mod, lw, layer, lat_len):
    n_ctx_tok = x_ctx.shape[0]
    t = n_ctx_tok + x_lat.shape[0]
    n_tiles = t // TM
    n_ctx_tiles = n_ctx_tok // TM
    tiles_per_seq = lat_len // TM
    rows8 = TM // SUBLANES
    last_blk = x_lat.shape[0] // SUBLANES - 1
    midx = functools.partial(_mod_index, tm=TM, n_ctx_tok=n_ctx_tok, lat_len=lat_len)

    def lat_blk8(i, off):
        return (jnp.clip((i - n_ctx_tiles) * rows8 + off, 0, last_blk), 0)

    x_specs = _dual_specs(TM, n_ctx_tiles) + [
        pl.BlockSpec((SUBLANES, D_MODEL), lambda i: lat_blk8(i, -1)),
        pl.BlockSpec((SUBLANES, D_MODEL), lambda i: lat_blk8(i, rows8)),
        pl.BlockSpec((1, N_MOD, D_MODEL), lambda i: (midx(i), 0, 0)),
        _layer_spec(lw["norm1_g"], layer),
        _layer_spec(lw["w_in"], layer, pipeline_mode=pl.Buffered(1)),
    ]
    wnames = ["conv_a_w", "conv_b_w", "conv_b_b", "lru_w", "lru_b", "lru_lam", "w0", "w2", "a0", "a2",
              "g2", "kk", "ka", "rk", "sg_ln_g", "sg_ln_b", "sg_ws", "sg_bst", "ind"]
    wts = [lw[n] for n in wnames]
    w_specs = [_layer_spec(w, layer) for w in wts]
    widths = [W_MIX, W_MIX, W_MIX, 2 * W_MIX, 2 * W_MIX, W_MIX, W_MIX]
    t_rows = [W_MIX, W_MIX, W_MIX, 2 * W_MIX, 2 * W_MIX, 2 * W_MIX]
    out_specs = ([pl.BlockSpec((TM, wd), lambda i: (i, 0)) for wd in widths]
                 + [pl.BlockSpec((1, r, TM), lambda i: (i, 0, 0)) for r in t_rows])
    out_shape = ([jax.ShapeDtypeStruct((t, wd), F32) for wd in widths]
                 + [jax.ShapeDtypeStruct((n_tiles, r, TM), F32) for r in t_rows])
    return pl.pallas_call(
        functools.partial(_prep_body, n_ctx_tiles=n_ctx_tiles, tiles_per_seq=tiles_per_seq),
        grid=(n_tiles,),
        in_specs=x_specs + w_specs,
        out_specs=out_specs,
        out_shape=out_shape,
        compiler_params=_cparams(("parallel",), VMEM_LIMIT),
        name="branch_prep",
    )(x_ctx, x_lat, x_lat, x_lat, mod, lw["norm1_g"], lw["w_in"], *wts)


def _lru_body(a_ref, u_ref, h0_ref, h_ref, hf_ref, *, n_ctx_blocks, ctx_cfg, lat_cfg):
    fw, bw = slice(0, W_MIX), slice(W_MIX, 2 * W_MIX)

    def scan(nseq, l):
        def step(s, carry):
            out = []
            for j in range(nseq):
                tf = j * l + s
                tb = j * l + (l - 1 - s)
                hf = a_ref[pl.ds(tf, 1), fw] * carry[2 * j] + u_ref[pl.ds(tf, 1), fw]
                hb = a_ref[pl.ds(tb, 1), bw] * carry[2 * j + 1] + u_ref[pl.ds(tb, 1), bw]
                h_ref[pl.ds(tf, 1), fw] = hf
                h_ref[pl.ds(tb, 1), bw] = hb
                out += [hf, hb]
            return tuple(out)

        init = []
        for j in range(nseq):
            init += [h0_ref[0, j:j + 1, fw], h0_ref[0, j:j + 1, bw]]
        fin = lax.fori_loop(0, l, step, tuple(init), unroll=2)
        hf_ref[0] = h0_ref[0]
        for j in range(nseq):
            hf_ref[0, j:j + 1, fw] = fin[2 * j]
            hf_ref[0, j:j + 1, bw] = fin[2 * j + 1]

    is_ctx = pl.program_id(0) < n_ctx_blocks
    pl.when(is_ctx)(lambda: scan(*ctx_cfg))
    pl.when(jnp.logical_not(is_ctx))(lambda: scan(*lat_cfg))


def _lru_scan(a, u, h0, *, rows, n_ctx_blocks, ctx_cfg, lat_cfg):
    nb = a.shape[0] // rows
    w = a.shape[1]
    tok = pl.BlockSpec((rows, w), lambda i: (i, 0))
    st = pl.BlockSpec((1,) + h0.shape[1:], lambda i: (i, 0, 0))
    return pl.pallas_call(
        functools.partial(_lru_body, n_ctx_blocks=n_ctx_blocks, ctx_cfg=ctx_cfg, lat_cfg=lat_cfg),
        grid=(nb,),
        in_specs=[tok, tok, st],
        out_specs=[tok, st],
        out_shape=[jax.ShapeDtypeStruct(a.shape, F32), jax.ShapeDtypeStruct(h0.shape, F32)],
        compiler_params=_cparams(("parallel",), VMEM_LIMIT),
        name="lru_scan",
    )(a, u, h0)


_SLOT_ORDER = (0, 4, 2, 6, 1, 5, 3, 7)


def _rowsum8(parts):
    sub = lax.broadcasted_iota(I32, (SUBLANES, LANES), 0)
    slots = [parts[i] for i in _SLOT_ORDER]
    roll = pltpu.roll
    lvl1 = [jnp.where(sub < 4, a + roll(a, 4, 0), b + roll(b, 4, 0))
            for a, b in zip(slots[0::2], slots[1::2])]
    lvl2 = [jnp.where((sub & 3) < 2, a + roll(a, 6, 0), roll(b + roll(b, 6, 0), 2, 0))
            for a, b in zip(lvl1[0::2], lvl1[1::2])]
    a, b = lvl2
    return jnp.where((sub & 1) == 0, a + roll(a, 7, 0), roll(b + roll(b, 7, 0), 1, 0))


def _fold8(x):
    return jnp.sum(x.reshape(HS_WKV // SUBLANES, SUBLANES, LANES), axis=0)


def _value_rows_on_sublanes(vl_n):
    return vl_n // SUBLANES < 4


def _wkv_body(*refs, nsrc, spb, vs, n_sb, tc, kp, vp):
    vl_n = HS_WKV // vs
    n_in = 6 * nsrc
    k_srcs = [refs[o * nsrc:(o + 1) * nsrc] for o in range(5)]
    v_srcs = refs[5 * nsrc:n_in]
    s0_ref = refs[n_in]
    y_ref = refs[n_in + 1]
    sf_ref = refs[n_in + 2]
    k_scr = refs[n_in + 3:n_in + 8]
    v_scr, y_scr, s_scr = refs[n_in + 8:n_in + 11]
    rows_on_sublanes = _value_rows_on_sublanes(HS_WKV // vs)
    sa_scr = None if rows_on_sublanes else refs[n_in + 11]
    r_scr, w_scr, k_scr_, kk_scr, b_scr = k_scr
    backward = pl.program_id(0) // n_sb == 1
    seqs = [(s, j) for s in range(nsrc) for j in range(spb)]

    @pl.when(pl.program_id(1) == 0)
    def _():
        s_scr[...] = s0_ref[0]

    def build_k(c, carry):
        row = pl.multiple_of(c * H_WKV, H_WKV)
        for o in range(5):
            slab = [k_srcs[o][s][j, pl.ds(row, H_WKV), :] for s, j in seqs]
            k_scr[o][pl.ds(c, tc, stride=kp), :] = jnp.concatenate(slab * vs, axis=0).T
        return carry

    lax.fori_loop(0, HS_WKV, build_k, 0, unroll=4)

    def build_v(vl, carry):
        slab = []
        for vsi in range(vs):
            row = pl.multiple_of((vsi * vl_n + vl) * H_WKV, H_WKV)
            slab += [v_srcs[s][j, pl.ds(row, H_WKV), :] for s, j in seqs]
        v_scr[pl.ds(vl, tc, stride=vp), :] = jnp.concatenate(slab, axis=0).T
        return carry

    lax.fori_loop(0, vl_n, build_v, 0, unroll=4)

    def step_keys_on_sublanes(s, carry):
        t = jnp.where(backward, tc - 1 - s, s)
        krow = pl.multiple_of(t * kp, SUBLANES)
        vrow = pl.multiple_of(t * vp, SUBLANES)
        kslab = pl.ds(krow, HS_WKV)
        for g in range(vl_n // SUBLANES):
            parts = [_fold8(s_scr[g * SUBLANES + i] * kk_scr[kslab, :]) for i in range(SUBLANES)]
            sa_scr[g * SUBLANES:(g + 1) * SUBLANES, :] = _rowsum8(parts)
        for g in range(vl_n // SUBLANES):
            parts = []
            for i in range(SUBLANES):
                vl = g * SUBLANES + i
                sa = sa_scr[vl:vl + 1, :]
                vv = v_scr[pl.ds(vrow + vl, 1), :]
                sn = s_scr[vl] * w_scr[kslab, :] - sa * b_scr[kslab, :] + vv * k_scr_[kslab, :]
                s_scr[vl] = sn
                parts.append(_fold8(sn * r_scr[kslab, :]))
            y_scr[pl.ds(pl.multiple_of(vrow + g * SUBLANES, SUBLANES), SUBLANES), :] = _rowsum8(parts)
        return carry

    def step_rows_on_sublanes(s, carry):
        t = jnp.where(backward, tc - 1 - s, s)
        krow = pl.multiple_of(t * kp, SUBLANES)
        vrow = pl.multiple_of(t * vp, SUBLANES)
        n_g = vl_n // SUBLANES
        batch = min(n_g, 2)
        n_acc = 4 // batch

        def row(ref, k):
            return jnp.broadcast_to(ref[pl.ds(krow + k, 1), :], (SUBLANES, LANES))

        def total(parts):
            while len(parts) > 1:
                parts = [a + b for a, b in zip(parts[0::2], parts[1::2])]
            return parts[0]

        def accumulate(acc, g, k, p):
            acc[g][k % n_acc] = p if acc[g][k % n_acc] is None else acc[g][k % n_acc] + p

        for g0 in range(0, n_g, batch):
            gs = range(g0, g0 + batch)
            acc = {g: [None] * n_acc for g in gs}
            for k in range(HS_WKV):
                kk = row(kk_scr, k)
                for g in gs:
                    accumulate(acc, g, k, s_scr[g, k] * kk)
            sa = {g: total(acc[g]) for g in gs}
            vv = {g: v_scr[pl.ds(pl.multiple_of(vrow + g * SUBLANES, SUBLANES), SUBLANES), :] for g in gs}
            acc = {g: [None] * n_acc for g in gs}
            for k in range(HS_WKV):
                w, b, kx, r = row(w_scr, k), row(b_scr, k), row(k_scr_, k), row(r_scr, k)
                for g in gs:
                    sn = s_scr[g, k] * w - sa[g] * b + vv[g] * kx
                    s_scr[g, k] = sn
                    accumulate(acc, g, k, sn * r)
            for g in gs:
                y_scr[pl.ds(pl.multiple_of(vrow + g * SUBLANES, SUBLANES), SUBLANES), :] = total(acc[g])
        return carry

    lax.fori_loop(0, tc, step_rows_on_sublanes if rows_on_sublanes else step_keys_on_sublanes, 0)

    def emit_y(vl, carry):
        yt = y_scr[pl.ds(vl, tc, stride=vp), :].T
        for vsi in range(vs):
            row = pl.multiple_of((vsi * vl_n + vl) * H_WKV, H_WKV)
            for n, (s, j) in enumerate(seqs):
                lane0 = (vsi * len(seqs) + n) * H_WKV
                y_ref[0, 0, s * spb + j, pl.ds(row, H_WKV), :] = yt[lane0:lane0 + H_WKV, :]
        return carry

    lax.fori_loop(0, vl_n, emit_y, 0, unroll=4)
    sf_ref[0] = s_scr[...]


def _wkv_scan(rt, wt, kt, kkt, bt, vt, s0, *, tile0, n_seq, seq_tiles, spb, state_layer=None):
    tc = WKV_TC
    if spb > 1:
        assert seq_tiles == 1 and n_seq % spb == 0 and tile0 % spb == 0
        nsrc, n_sb = 1, n_seq // spb
    else:
        nsrc, n_sb = n_seq, 1
    inst = nsrc * spb * H_WKV
    vs = LANES // inst
    vl_n = HS_WKV // vs
    assert vl_n % SUBLANES == 0, "value rows are processed eight at a time"
    cpt = TM // tc
    n_chunks = seq_tiles * cpt
    kp = HS_WKV + SUBLANES
    vp = vl_n + SUBLANES if ((vl_n + SUBLANES) // SUBLANES) % 2 else vl_n + 2 * SUBLANES

    def chunk(g, i):
        return jnp.where(g // n_sb == 1, n_chunks - 1 - i, i)

    def in_map(g, i, *, src, per_dir):
        ce = chunk(g, i)
        rb = (g // n_sb) if per_dir else 0
        if spb > 1:
            return (tile0 // spb + g % n_sb, rb, ce)
        return (tile0 + src * seq_tiles + ce // cpt, rb, ce % cpt)

    def out_map(g, i):
        ce = chunk(g, i)
        if spb > 1:
            return (g // n_sb, 0, g % n_sb, 0, ce)
        return (g // n_sb, ce // cpt, 0, 0, ce % cpt)

    in_specs, operands = [], []
    for arr, per_dir in ((rt, False), (wt, True), (kt, True), (kkt, False), (bt, True), (vt, False)):
        for src in range(nsrc):
            in_specs.append(pl.BlockSpec((spb, W_MIX, tc), functools.partial(in_map, src=src, per_dir=per_dir),
                                         pipeline_mode=pl.Buffered(1)))
            operands.append(arr)
    state_block = (1,) + _wkv_state_shape(1, vs)[1:]
    g0 = 0 if state_layer is None else state_layer * N_DIR * n_sb
    sspec = pl.BlockSpec(state_block, lambda g, i: (g0 + g,) + (0,) * (len(state_block) - 1))
    in_specs.append(sspec)
    out_specs = [pl.BlockSpec((1, 1, nsrc * spb, W_MIX, tc), out_map)]
    out_shape = [jax.ShapeDtypeStruct((N_DIR, seq_tiles, n_seq, W_MIX, TM), F32)]
    res = pl.pallas_call(
        functools.partial(_wkv_body, nsrc=nsrc, spb=spb, vs=vs, n_sb=n_sb, tc=tc, kp=kp, vp=vp),
        grid=(N_DIR * n_sb, n_chunks),
        in_specs=in_specs,
        out_specs=out_specs + [sspec],
        out_shape=out_shape + [jax.ShapeDtypeStruct(s0.shape, F32)],
        scratch_shapes=([pltpu.VMEM((tc * kp, LANES), F32)] * 5
                        + [pltpu.VMEM((tc * vp, LANES), F32)] * 2
                        + [pltpu.VMEM(state_block[1:], F32)]
                        + ([] if _value_rows_on_sublanes(vl_n) else [pltpu.VMEM((vl_n, LANES), F32)])),
        input_output_aliases={} if state_layer is None else {len(operands): 1},
        compiler_params=_cparams(("parallel", "arbitrary"), VMEM_LIMIT),
        name="wkv_scan",
    )(*operands, s0)
    return res[0], res[1]


def _merge_body(xc_ref, xl_ref, mod_ref, n1g_ref, n2g_ref, wg_ref, gb_ref, wbr_ref, wo_ref,
                lnxg_ref, lnxb_ref, ind_ref,
                ya_ref, yd_ref, gbg_ref, h_ref, ycf_ref, ycb_ref, ylf_ref, ylb_ref, g_ref, bon_ref,
                x1_ref, h2_ref, *, n_ctx_tiles):
    is_ctx = pl.program_id(0) < n_ctx_tiles
    x = jnp.where(is_ctx, xc_ref[...], xl_ref[...])
    m = mod_ref[0]
    ind = ind_ref[...]
    h = (_rms(x, n1g_ref[...]) * (1.0 + m[1:2, :]) + m[0:1, :]).astype(BF16)
    y_b = gbg_ref[...] * (h_ref[:, 0:W_MIX] + h_ref[:, W_MIX:2 * W_MIX])
    y = jnp.where(is_ctx, ycf_ref[0, 0, 0] + ycb_ref[0, 0, 0], ylf_ref[0, 0, 0] + ylb_ref[0, 0, 0]).T
    yc = y - _segsum(y, ind) * (1.0 / HS_WKV)
    var = _segsum(yc * yc, ind) * (1.0 / HS_WKV)
    y_c = (yc * lax.rsqrt(var + LNX_EPS) * lnxg_ref[...] + lnxb_ref[...] + bon_ref[...]) * g_ref[...]
    merged = None
    for n, yn in enumerate((ya_ref[...], y_b, y_c, yd_ref[...])):
        cs = slice(n * D_MODEL, (n + 1) * D_MODEL)
        gate = jax.nn.sigmoid(jnp.dot(h, wg_ref[:, cs], preferred_element_type=F32) + gb_ref[:, cs])
        br = jnp.dot(yn.astype(BF16), wbr_ref[n * W_MIX:(n + 1) * W_MIX, :], preferred_element_type=F32)
        merged = gate * br if merged is None else merged + gate * br
    mo = jnp.dot(merged.astype(BF16), wo_ref[...], preferred_element_type=F32)
    x1 = x + m[2:3, :] * mo
    x1_ref[...] = x1
    h2_ref[...] = (_rms(x1, n2g_ref[...]) * (1.0 + m[4:5, :]) + m[3:4, :]).astype(BF16)


def _merge(x_ctx, x_lat, mod, lw, layer, tok_in, y_ctx, y_lat, lat_len):
    n_ctx_tok = x_ctx.shape[0]
    t = n_ctx_tok + x_lat.shape[0]
    n_ctx_tiles = n_ctx_tok // TM
    tps = lat_len // TM
    midx = functools.partial(_mod_index, tm=TM, n_ctx_tok=n_ctx_tok, lat_len=lat_len)
    wnames = ["norm1_g", "norm2_g", "w_gate", "gate_b", "w_branch", "w_out", "lnx_g", "lnx_b", "ind"]
    wts = [lw[n] for n in wnames]
    tok = lambda wd: pl.BlockSpec((TM, wd), lambda i: (i, 0))
    ya, yd, gbg, h, g, bon = tok_in
    yblock = (1, 1, 1, W_MIX, TM)

    def ctx_spec(d):
        return pl.BlockSpec(yblock, lambda i: (d, 0, jnp.minimum(i, n_ctx_tiles - 1), 0, 0))

    def lat_spec(d):
        def imap(i):
            r = jnp.maximum(i - n_ctx_tiles, 0)
            return (d, lax.rem(r, tps), lax.div(r, tps), 0, 0)
        return pl.BlockSpec(yblock, imap)

    return pl.pallas_call(
        functools.partial(_merge_body, n_ctx_tiles=n_ctx_tiles),
        grid=(t // TM,),
        in_specs=(_dual_specs(TM, n_ctx_tiles)
                  + [pl.BlockSpec((1, N_MOD, D_MODEL), lambda i: (midx(i), 0, 0))]
                  + [_layer_spec(w, layer) for w in wts]
                  + [tok(W_MIX), tok(W_MIX), tok(W_MIX), tok(2 * W_MIX),
                     ctx_spec(0), ctx_spec(1), lat_spec(0), lat_spec(1), tok(W_MIX), tok(W_MIX)]),
        out_specs=[tok(D_MODEL), tok(D_MODEL)],
        out_shape=[jax.ShapeDtypeStruct((t, D_MODEL), F32), jax.ShapeDtypeStruct((t, D_MODEL), BF16)],
        compiler_params=_cparams(("parallel",), VMEM_LIMIT),
        name="merge",
    )(x_ctx, x_lat, mod, *wts, ya, yd, gbg, h, y_ctx, y_ctx, y_lat, y_lat, g, bon)


_CAND_VALID = (8, 8, 8, 5, 4, 3, 2, 2, 2, 8)


def _oddeven_pairs(n):
    def merge(lo, hi, r):
        step = r * 2
        if step < hi - lo:
            yield from merge(lo, hi, step)
            yield from merge(lo + r, hi, step)
            yield from [(i, i + r) for i in range(lo + r, hi - r, step)]
        else:
            yield (lo, lo + r)

    def sort(lo, hi):
        if hi - lo >= 1:
            mid = lo + (hi - lo) // 2
            yield from sort(lo, mid)
            yield from sort(mid + 1, hi)
            yield from merge(lo, hi, 1)

    return tuple(sort(0, n - 1))


_SORT16 = _oddeven_pairs(N_KEYS // SUBLANES)


def _route_head(qs, keys_ref):
    sub = lax.broadcasted_iota(I32, (SUBLANES, LANES), 0)
    kid = lax.broadcasted_iota(I32, (PEER_TOPK, LANES), 0)
    neg = -jnp.inf

    def bc(x, r):
        return jnp.broadcast_to(x[r:r + 1, :], (SUBLANES, LANES))

    def head():
        tops = []
        for p in range(2):
            s = jnp.dot(keys_ref[p], qs[p], preferred_element_type=F32)
            cols = [s[j * SUBLANES:(j + 1) * SUBLANES, :] for j in range(N_KEYS // SUBLANES)]
            cidx = [sub + j * SUBLANES for j in range(N_KEYS // SUBLANES)]
            for a, b in _SORT16:
                take = cols[b] > cols[a]
                cols[a], cols[b] = jnp.where(take, cols[b], cols[a]), jnp.where(take, cols[a], cols[b])
                cidx[a], cidx[b] = jnp.where(take, cidx[b], cidx[a]), jnp.where(take, cidx[a], cidx[b])
            vals = jnp.zeros((PEER_TOPK, LANES), F32)
            idxs = jnp.zeros((PEER_TOPK, LANES), I32)
            for r in range(PEER_TOPK):
                v8, i8 = cols[0], cidx[0]
                for sh in (4, 2, 1):
                    vr, ir = pltpu.roll(v8, sh, 0), pltpu.roll(i8, sh, 0)
                    take = vr > v8
                    v8, i8 = jnp.where(take, vr, v8), jnp.where(take, ir, i8)
                m, ix = v8[0:1, :], i8[0:1, :]
                vals = jnp.where(kid == r, m, vals)
                idxs = jnp.where(kid == r, ix, idxs)
                popped = cidx[0] == ix
                for j in range(PEER_TOPK - 1 - r):
                    cols[j] = jnp.where(popped, cols[j + 1], cols[j])
                    cidx[j] = jnp.where(popped, cidx[j + 1], cidx[j])
            tops.append((vals, idxs))
        (a0, i0), (a1, i1) = tops
        lo, hi = slice(0, SUBLANES), slice(SUBLANES, 2 * SUBLANES)
        slabs = [bc(a0, 0) + a1[lo], bc(a0, 0) + a1[hi]]
        ci = [bc(i0, 0), bc(i0, 0)]
        cj = [i1[lo], i1[hi]]
        for r in range(1, SUBLANES):
            slabs.append(bc(a0, r) + a1[lo])
            ci.append(bc(i0, r))
            cj.append(i1[lo])
        slabs.append(a0[hi] + bc(a1, 0))
        ci.append(i0[hi])
        cj.append(bc(i1, 0))
        slabs = [jnp.where(sub < nv, sl, neg) for sl, nv in zip(slabs, _CAND_VALID)]
        ids = [a * N_KEYS + b for a, b in zip(ci, cj)]
        vals = jnp.zeros((PEER_TOPK, LANES), F32)
        esel = jnp.zeros((PEER_TOPK, LANES), I32)
        for r in range(PEER_TOPK):
            level = list(zip(slabs, ids))
            while len(level) > 1:
                nxt = []
                for (va, ea), (vb, eb) in zip(level[0::2], level[1::2]):
                    take = vb > va
                    nxt.append((jnp.where(take, vb, va), jnp.where(take, eb, ea)))
                if len(level) % 2:
                    nxt.append(level[-1])
                level = nxt
            v8, e8 = level[0]
            for sh in (4, 2, 1):
                vr, er = pltpu.roll(v8, sh, 0), pltpu.roll(e8, sh, 0)
                take = vr > v8
                v8, e8 = jnp.where(take, vr, v8), jnp.where(take, er, e8)
            m, ex = v8[0:1, :], e8[0:1, :]
            slabs = [jnp.where(eid == ex, neg, sl) for sl, eid in zip(slabs, ids)]
            vals = jnp.where(kid == r, m, vals)
            esel = jnp.where(kid == r, ex, esel)
        e = jnp.exp(vals - vals[0:1, :])
        return esel, e / jnp.sum(e, axis=0, keepdims=True)

    return head()


def _peer_body(h2_ref, h2n_ref, wqt_ref, keys_ref, u_ref, v_ref, x1_ref, mod_ref, fng_ref,
               oc_ref, ol_ref,
               q_scr, e_scr, g_scr, et_scr, gt_scr, gs_scr, acc_scr,
               *, rows, pitch, units, n_ctx_tiles, final):
    m = pl.program_id(0)
    e = pl.program_id(1)
    tm = h2_ref.shape[0]
    n_chunks = tm // LANES
    nsel = PEER_HEADS * PEER_TOPK
    slot = lax.rem(m, 2)

    def project_queries(src_ref):
        q = lax.dot_general(wqt_ref[...], src_ref[...], _NT, preferred_element_type=F32).astype(BF16)
        for c in range(n_chunks):
            q_scr[c] = q[:, c * LANES:(c + 1) * LANES]

    def route_unit(u, dst):
        c = u // PEER_HEADS
        h = lax.rem(u, PEER_HEADS)
        qs = [q_scr[c, pl.ds(pl.multiple_of(h * (2 * N_KEYS) + p * N_KEYS, N_KEYS), N_KEYS), :]
              for p in range(2)]
        esel, gates = _route_head(qs, keys_ref)
        row = pl.multiple_of(h * PEER_TOPK, PEER_TOPK)
        e_scr[dst, c, pl.ds(row, PEER_TOPK), :] = esel
        g_scr[dst, c, pl.ds(row, PEER_TOPK), :] = gates

    @pl.when(jnp.logical_and(e == 0, m == 0))
    def _first_tile_routing():
        project_queries(h2_ref)

        def unit(u, c):
            route_unit(u, 0)
            return c

        lax.fori_loop(0, n_chunks * PEER_HEADS, unit, 0)

    @pl.when(e == 0)
    def _build():
        for c in range(n_chunks):
            et_scr[c * LANES:(c + 1) * LANES, :] = e_scr[slot, c].T
            gt_scr[c * LANES:(c + 1) * LANES, :] = g_scr[slot, c].T
        kio = lax.broadcasted_iota(I32, (N_KEYS, nsel), 0)

        def tok(t, c):
            erow = et_scr[pl.ds(t, 1), :]
            grow = gt_scr[pl.ds(t, 1), :]
            at = jnp.where(kio == (erow >> 7), grow, 0.0).astype(BF16)
            bt = jnp.where(kio == (erow & (N_KEYS - 1)), 1.0, 0.0).astype(BF16)
            gt = lax.dot_general(at, bt, _NT, preferred_element_type=F32)
            hi = pltpu.bitcast(gt[0:rows, :], jnp.uint32) & jnp.uint32(0xFFFF0000)
            lo = pltpu.bitcast(gt[rows:2 * rows, :], jnp.uint32) >> 16
            gs_scr[pl.ds(pl.multiple_of(t * pitch, SUBLANES), rows), :] = hi | lo
            return c

        lax.fori_loop(0, tm, tok, 0, unroll=128)
        acc_scr[...] = jnp.zeros_like(acc_scr)
        project_queries(h2n_ref)

    for k in range(units):
        route_unit(e * units + k, 1 - slot)

    per_sub = PEER_SUB // N_KEYS
    per_step = u_ref.shape[0] // N_KEYS
    steps_per_half = rows // per_step
    row0 = lax.rem(e, steps_per_half) * per_step
    shift = jnp.where(e < steps_per_half, 0, 16).astype(jnp.uint32)
    h2 = h2_ref[...]
    total = None
    for sb in range(per_step // per_sub):
        es = slice(sb * PEER_SUB, (sb + 1) * PEER_SUB)
        hmat = lax.dot_general(h2, u_ref[es, :], _NT, preferred_element_type=F32)
        words = jnp.concatenate(
            [gs_scr[pl.ds(row0 + sb * per_sub + ii, tm, stride=pitch), :] for ii in range(per_sub)], axis=1)
        gm = pltpu.bitcast((words << shift) & jnp.uint32(0xFFFF0000), F32)
        act = jax.nn.gelu(hmat.astype(BF16)) * gm.astype(BF16)
        part = jnp.dot(act, v_ref[es, :], preferred_element_type=F32)
        total = part if total is None else total + part
    acc_scr[...] += total

    def result():
        x2 = x1_ref[...] + mod_ref[0][5:6, :] * acc_scr[...]
        return _rms(x2, fng_ref[...]) if final else x2

    last = e == pl.num_programs(1) - 1

    @pl.when(jnp.logical_and(last, m < n_ctx_tiles))
    def _out_ctx():
        oc_ref[...] = result()

    @pl.when(jnp.logical_and(last, m >= n_ctx_tiles))
    def _out_lat():
        ol_ref[...] = result()


def _peer(h2, wqt, keys, u, v, layer, x1, mod, fng, n_ctx_tok, lat_len, final):
    t = h2.shape[0]
    nsel = PEER_HEADS * PEER_TOPK
    rows = N_KEYS // 2
    pitch = rows + SUBLANES
    n_e = (N_KEYS * N_KEYS) // PEER_EB
    n_m = t // TM_PEER
    n_chunks = TM_PEER // LANES
    units = (n_chunks * PEER_HEADS) // n_e
    assert units * n_e == n_chunks * PEER_HEADS
    midx = functools.partial(_mod_index, tm=TM_PEER, n_ctx_tok=n_ctx_tok, lat_len=lat_len)
    tok = lambda wd: pl.BlockSpec((TM_PEER, wd), lambda m, e: (m, 0))
    nxt = pl.BlockSpec((TM_PEER, D_MODEL), lambda m, e: (jnp.minimum(m + 1, n_m - 1), 0),
                       pipeline_mode=pl.Buffered(1))
    espec = pl.BlockSpec((None, PEER_EB, D_MODEL), lambda m, e: (layer, e, 0))
    single = dict(pipeline_mode=pl.Buffered(1))
    return pl.pallas_call(
        functools.partial(_peer_body, rows=rows, pitch=pitch, units=units,
                          n_ctx_tiles=n_ctx_tok // TM_PEER, final=final),
        grid=(n_m, n_e),
        in_specs=[tok(D_MODEL), nxt,
                  _layer_spec(wqt, layer, **single),
                  _layer_spec(keys, layer, **single),
                  espec, espec,
                  pl.BlockSpec((TM_PEER, D_MODEL), lambda m, e: (m, 0), **single),
                  pl.BlockSpec((1, N_MOD, D_MODEL), lambda m, e: (midx(m), 0, 0)),
                  _const_spec((1, D_MODEL))],
        out_specs=_dual_specs(TM_PEER, n_ctx_tok // TM_PEER),
        out_shape=[jax.ShapeDtypeStruct((n_ctx_tok, D_MODEL), F32),
                   jax.ShapeDtypeStruct((t - n_ctx_tok, D_MODEL), F32)],
        scratch_shapes=[pltpu.VMEM((n_chunks, wqt.shape[1], LANES), BF16),
                        pltpu.VMEM((2, n_chunks, nsel, LANES), I32),
                        pltpu.VMEM((2, n_chunks, nsel, LANES), F32),
                        pltpu.VMEM((TM_PEER, nsel), I32),
                        pltpu.VMEM((TM_PEER, nsel), F32),
                        pltpu.VMEM((TM_PEER * pitch, N_KEYS), jnp.uint32),
                        pltpu.VMEM((TM_PEER, D_MODEL), F32)],
        compiler_params=_cparams(("arbitrary", "arbitrary"), VMEM_LIMIT),
        name="peer",
    )(h2, h2, wqt, keys, u, v, x1, mod, fng)


def _cast_body(u_ref, v_ref, uo_ref, vo_ref):
    uo_ref[...] = u_ref[...].astype(BF16)
    vo_ref[...] = v_ref[...].astype(BF16)


def _cast_tables(u, v):
    depth, n_e, d = u.shape
    rows = 2 * PEER_EB
    assert u.shape == v.shape and n_e % rows == 0
    spec = pl.BlockSpec((None, rows, d), lambda l, e: (l, e, 0))
    return pl.pallas_call(
        _cast_body,
        grid=(depth, n_e // rows),
        in_specs=[spec, spec],
        out_specs=[spec, spec],
        out_shape=[jax.ShapeDtypeStruct(u.shape, BF16)] * 2,
        compiler_params=_cparams(("parallel", "parallel"), VMEM_LIMIT),
        name="cast_tables",
    )(u, v)


def _wkv_state_shape(n_groups, vs):
    vl = HS_WKV // vs
    if _value_rows_on_sublanes(vl):
        return (n_groups, vl // SUBLANES, HS_WKV, SUBLANES, LANES)
    return (n_groups, vl, HS_WKV, LANES)


def _wkv_state_in(s, vs):
    n, depth = s.shape[:2]
    vl = HS_WKV // vs
    if _value_rows_on_sublanes(vl):
        s = s.reshape(n, depth, N_DIR, H_WKV, vs, vl // SUBLANES, SUBLANES, HS_WKV)
        s = s.transpose(1, 2, 5, 7, 6, 4, 0, 3)
    else:
        s = s.reshape(n, depth, N_DIR, H_WKV, vs, vl, HS_WKV).transpose(1, 2, 5, 6, 4, 0, 3)
    return s.reshape((depth,) + _wkv_state_shape(N_DIR, vs))


def _wkv_state_out(s, depth, n_sb, spb, vs):
    vl = HS_WKV // vs
    if _value_rows_on_sublanes(vl):
        s = s.reshape(depth, N_DIR, n_sb, vl // SUBLANES, HS_WKV, SUBLANES, vs, spb, H_WKV)
        s = s.transpose(2, 7, 0, 1, 8, 6, 3, 5, 4)
    else:
        s = s.reshape(depth, N_DIR, n_sb, vl, HS_WKV, vs, spb, H_WKV).transpose(2, 6, 0, 1, 7, 5, 3, 4)
    return s.reshape(n_sb * spb, depth, N_DIR, H_WKV, HS_WKV, HS_WKV)


def _stacked_weights(prm):
    depth = prm["w_in"].shape[0]
    eye_h = jnp.eye(H_LRU, dtype=F32)
    eye_d = jnp.eye(N_DIR, dtype=F32)

    def perm(x, axis=-1):
        x = jnp.moveaxis(x, axis, -1)
        lead = x.shape[:-1]
        x = x.reshape(lead + (H_WKV, HS_WKV)).swapaxes(-1, -2).reshape(lead + (W_MIX,))
        return jnp.moveaxis(x, -1, axis)

    def lru_bd(wt):
        return jnp.einsum("ldhij,hg->lhidgj", wt, eye_h).reshape(depth, W_MIX, N_DIR * W_MIX)

    def lora_bd(wt):
        r = wt.shape[2]
        return jnp.einsum("ldrc,de->ldrec", wt, eye_d).reshape(depth, N_DIR * r, N_DIR * W_MIX)

    w_in = prm["w_in"]
    pad = jnp.zeros((depth, D_MODEL, Z_COLS - 5504), F32)
    rkv = [perm(w_in[:, :, 2560 + j * W_MIX:2560 + (j + 1) * W_MIX]) for j in range(3)]
    w_in_perm = jnp.concatenate(
        [w_in[:, :, 0:1536]] + rkv
        + [w_in[:, :, 1536:2560], w_in[:, :, 4480:5504], w_in[:, :, 4096:4480], pad], axis=2).astype(BF16)
    row = lambda x: x.reshape(depth, 1, -1).astype(F32)
    head_of = np.arange(W_MIX) % H_WKV
    w_branch = prm["w_branch"]
    w_branch = jnp.concatenate(
        [w_branch[:, 0], w_branch[:, 1], perm(w_branch[:, 2], axis=1), w_branch[:, 3]], axis=1)
    return {
        "w_in": w_in_perm,
        "w_gate": w_in[:, :, 5504:].astype(BF16),
        "norm1_g": row(prm["norm1_g"]),
        "norm2_g": row(prm["norm2_g"]),
        "conv_a_w": prm["conv_a_w"],
        "conv_b_w": prm["conv_b_w"],
        "conv_b_b": row(prm["conv_b_b"]),
        "lru_w": jnp.concatenate([lru_bd(prm["lru_wa"]), lru_bd(prm["lru_wx"])], axis=2).astype(BF16),
        "lru_b": jnp.concatenate([row(prm["lru_ba"]), row(prm["lru_bx"])], axis=2),
        "lru_lam": row(prm["lru_lambda"]),
        "w0": row(perm(prm["rwkv_w0"])),
        "w2": lora_bd(perm(prm["rwkv_w2"])).astype(BF16),
        "a0": row(perm(prm["rwkv_a0"])),
        "a2": lora_bd(perm(prm["rwkv_a2"])).astype(BF16),
        "g2": perm(prm["rwkv_g2"]).astype(BF16),
        "kk": row(perm(prm["rwkv_kk"])),
        "ka": row(perm(prm["rwkv_ka"])),
        "rk": row(perm(prm["rwkv_rk"].reshape(depth, W_MIX))),
        "lnx_g": row(perm(prm["lnx_g"])),
        "lnx_b": row(perm(prm["lnx_b"])),
        "sg_ln_g": row(prm["sg_ln_g"]),
        "sg_ln_b": row(prm["sg_ln_b"]),
        "sg_ws": prm["sg_ws"].astype(BF16),
        "sg_bst": prm["sg_bs"].transpose(0, 2, 1),
        "gate_b": row(prm["gate_b"]),
        "w_branch": w_branch.astype(BF16),
        "w_out": prm["w_out"].astype(BF16),
        "wq_t": prm["peer_wq"].transpose(0, 2, 1).astype(BF16),
        "keys": prm["peer_keys"].astype(BF16),
        "ind": jnp.asarray(head_of[:, None] == head_of[None, :], BF16)[None],
    }


def kernel(x_prompt, x_sample, state_lru, state_wkv, c, c_ctx, norm1_g, norm2_g, w_mod, b_mod, w_in, conv_a_w, conv_b_w, conv_b_b, lru_wa, lru_ba, lru_wx, lru_bx, lru_lambda, rwkv_w0, rwkv_w2, rwkv_a0, rwkv_a2, rwkv_g2, rwkv_kk, rwkv_ka, rwkv_rk, lnx_g, lnx_b, sg_ln_g, sg_ln_b, sg_ws, sg_bs, gate_b, w_branch, w_out, peer_wq, peer_keys, peer_u, peer_v, final_norm_g):
    prm = dict(norm1_g=norm1_g, norm2_g=norm2_g, w_in=w_in, conv_a_w=conv_a_w, conv_b_w=conv_b_w,
               conv_b_b=conv_b_b, lru_wa=lru_wa, lru_ba=lru_ba, lru_wx=lru_wx, lru_bx=lru_bx,
               lru_lambda=lru_lambda, rwkv_w0=rwkv_w0, rwkv_w2=rwkv_w2, rwkv_a0=rwkv_a0, rwkv_a2=rwkv_a2,
               rwkv_g2=rwkv_g2, rwkv_kk=rwkv_kk, rwkv_ka=rwkv_ka, rwkv_rk=rwkv_rk, lnx_g=lnx_g,
               lnx_b=lnx_b, sg_ln_g=sg_ln_g, sg_ln_b=sg_ln_b, sg_ws=sg_ws, sg_bs=sg_bs, gate_b=gate_b,
               w_branch=w_branch, w_out=w_out, peer_wq=peer_wq, peer_keys=peer_keys, peer_u=peer_u,
               peer_v=peer_v)
    bc, lc, _ = x_prompt.shape
    bl, ll, _ = x_sample.shape
    depth = w_mod.shape[0]
    n_ctx_tok = bc * lc
    n_ctx_tiles = n_ctx_tok // TM
    lat_tiles = ll // TM
    ctx_spb = min(WKV_CTX_SPB, bc)
    lru_spb = ll // lc
    assert lc == TM and ll % TM_PEER == 0 and n_ctx_tok % TM_PEER == 0 and bl + 1 <= SUBLANES
    assert ll % GRID_W == 0 and TM % GRID_W == 0 and bc % ctx_spb == 0 and n_ctx_tok % ll == 0
    assert LANES % (ctx_spb * H_WKV) == 0 and LANES % (bl * H_WKV) == 0

    cond = jnp.zeros((SUBLANES, D_MODEL), F32).at[0].set(c_ctx).at[1:1 + bl].set(c)
    mods = _modulation(cond, w_mod, b_mod).reshape(depth, SUBLANES, N_MOD, D_MODEL)
    fng = final_norm_g.reshape(1, D_MODEL)
    x_ctx = x_prompt.reshape(n_ctx_tok, D_MODEL)
    x_lat = x_sample.reshape(bl * ll, D_MODEL)
    ctx_vs = LANES // (ctx_spb * H_WKV)
    lat_vs = LANES // (bl * H_WKV)
    n_sb = bc // ctx_spb
    wkv_ctx = jnp.zeros(_wkv_state_shape(depth * N_DIR * n_sb, ctx_vs), F32)
    u_all, v_all = _cast_tables(peer_u, peer_v)
    new_lru = []
    pnames = ["ya", "yd", "gbg", "la", "lu", "g", "bon", "rt", "vt", "kkt", "wt", "kt", "bt"]
    lw = _stacked_weights(prm)
    wkv_lat0 = _wkv_state_in(state_wkv.astype(F32), lat_vs)
    lru_h0 = jnp.zeros((depth, n_ctx_tok // ll + bl, lru_spb, N_DIR * W_MIX), F32).at[:, n_ctx_tok // ll:, 0].set(
        state_lru.astype(F32).reshape(bl, depth, N_DIR * W_MIX).swapaxes(0, 1))
    for i in range(depth):
        mod = mods[i]
        p = dict(zip(pnames, _prep(x_ctx, x_lat, mod, lw, i, ll)))

        h, lru_s = _lru_scan(p["la"], p["lu"], lru_h0[i],
                             rows=ll, n_ctx_blocks=n_ctx_tok // ll, ctx_cfg=(lru_spb, lc), lat_cfg=(1, ll))
        new_lru.append(lru_s[:n_ctx_tok // ll].reshape(bc, N_DIR, W_MIX))

        wkv_in = [p[n] for n in ("rt", "wt", "kt", "kkt", "bt", "vt")]
        y_c, wkv_ctx = _wkv_scan(*wkv_in, wkv_ctx, tile0=0, n_seq=bc, seq_tiles=1, spb=ctx_spb, state_layer=i)
        y_l, _ = _wkv_scan(*wkv_in, wkv_lat0[i], tile0=n_ctx_tiles, n_seq=bl, seq_tiles=lat_tiles, spb=1)

        tok_in = [p["ya"], p["yd"], p["gbg"], h, p["g"], p["bon"]]
        x1, h2 = _merge(x_ctx, x_lat, mod, lw, i, tok_in, y_c, y_l, ll)
        x_ctx, x_lat = _peer(h2, lw["wq_t"], lw["keys"], u_all, v_all, i, x1, mod, fng,
                             n_ctx_tok, ll, final=(i == depth - 1))
    y_prompt = x_ctx.reshape(bc, lc, D_MODEL)
    y_sample = x_lat.reshape(bl, ll, D_MODEL)
    return (y_prompt, y_sample, jnp.stack(new_lru, axis=1), _wkv_state_out(wkv_ctx, depth, n_sb, ctx_spb, ctx_vs))
```
